```python
import jax, jax.numpy as jnp
from jax import lax
import numpy as np


D_MODEL = 1024
BATCH = 8
SEQ = 8192
DEPTH = 1

N_META = 16
CHUNK = 128
FRONT = CHUNK - N_META
EPS = 1e-6

SSD_EXPAND = 2
SSD_D_INNER = SSD_EXPAND * D_MODEL
SSD_HEAD_DIM = 64
SSD_HEADS = SSD_D_INNER // SSD_HEAD_DIM
SSD_GROUPS = 4
SSD_STATE = 128
SSD_CONV = 4
SSD_CONV_DIM = SSD_D_INNER + 2 * SSD_GROUPS * SSD_STATE

RET_HEADS = 4
RET_QK_DIM = D_MODEL
RET_V_DIM = 2 * D_MODEL
RET_HEAD_QK = RET_QK_DIM // RET_HEADS
RET_HEAD_V = RET_V_DIM // RET_HEADS
ROPE_BASE = 10000.0

D_FF = ((8 * D_MODEL // 3 + 127) // 128) * 128
FFN_CONV = 3

IN_PROJ_DIM = SSD_D_INNER + SSD_CONV_DIM + SSD_HEADS + 2 * RET_QK_DIM + 2 * RET_V_DIM + 2 * D_MODEL

kernel_name = 'hybrid_ssd_retention_meta_block'


def _split_points():
    widths = [SSD_D_INNER, SSD_CONV_DIM, SSD_HEADS, RET_QK_DIM, RET_QK_DIM,
              RET_V_DIM, RET_V_DIM, D_MODEL, D_MODEL]
    pts = []
    acc = 0
    for w in widths[:-1]:
        acc += w
        pts.append(acc)
    return pts


def rmsnorm(x, w):
    xf = x.astype(jnp.float32)
    y = xf * lax.rsqrt(jnp.mean(xf * xf, axis=-1, keepdims=True) + EPS)
    return (y * w.astype(jnp.float32)).astype(x.dtype)


def causal_dwconv(x, w, b):
    k_width = w.shape[0]
    seq_len = x.shape[1]
    xp = jnp.pad(x, ((0, 0), (k_width - 1, 0), (0, 0)))
    y = b + xp[:, 0:seq_len] * w[0]
    for k in range(1, k_width):
        y = y + xp[:, k:k + seq_len] * w[k]
    return y


def _pad_to_chunks(t):
    n_tok = t.shape[1] - N_META
    end = (-n_tok) % CHUNK
    widths = [(0, 0), (FRONT, end)] + [(0, 0)] * (t.ndim - 2)
    return jnp.pad(t, widths)


def ssd_mixer(z, xbc, dt_raw, conv_w, conv_b, dt_bias, A_log, D_skip, norm_w):
    f32 = jnp.float32
    bsz, seq_len, _ = z.shape
    hpg = SSD_HEADS // SSD_GROUPS
    xbc = jax.nn.silu(causal_dwconv(xbc, conv_w, conv_b))
    xs, b_in, c_in = jnp.split(xbc, [SSD_D_INNER, SSD_D_INNER + SSD_GROUPS * SSD_STATE], axis=-1)
    dt = jax.nn.softplus(dt_raw.astype(f32) + dt_bias.astype(f32))
    xs, b_in, c_in, dt = [_pad_to_chunks(t) for t in (xs, b_in, c_in, dt)]
    n_chunks = xs.shape[1] // CHUNK
    xs = xs.reshape(bsz, n_chunks, CHUNK, SSD_GROUPS, hpg, SSD_HEAD_DIM)
    b_in = b_in.reshape(bsz, n_chunks, CHUNK, SSD_GROUPS, SSD_STATE)
    c_in = c_in.reshape(bsz, n_chunks, CHUNK, SSD_GROUPS, SSD_STATE)
    dt = dt.reshape(bsz, n_chunks, CHUNK, SSD_GROUPS, hpg)
    A = -jnp.exp(A_log.astype(f32)).reshape(SSD_GROUPS, hpg)
    a_cs = jnp.cumsum(dt * A, axis=2)
    xdt = xs * dt[..., None]
    causal = jnp.tril(jnp.ones((CHUNK, CHUNK), dtype=bool))
    seg = a_cs[:, :, :, None] - a_cs[:, :, None, :]
    decay_ls = jnp.exp(jnp.where(causal[:, :, None, None], seg, -jnp.inf))
    cb = jnp.einsum('bclgn,bcsgn->bclsg', c_in, b_in)
    y = jnp.einsum('bclsgh,bcsghp->bclghp', cb[..., None] * decay_ls, xdt)
    decay_to_end = jnp.exp(a_cs[:, :, -1:] - a_cs)
    chunk_states = jnp.einsum('bcsgn,bcsghp->bcghpn', b_in, xdt * decay_to_end[..., None])
    chunk_decay = jnp.exp(a_cs[:, :, -1])

    def step(state, inp):
        st, dec = inp
        return state * dec[..., None, None] + st, state

    init = jnp.zeros_like(chunk_states[:, 0])
    _, prev = lax.scan(step, init, (jnp.moveaxis(chunk_states, 1, 0), jnp.moveaxis(chunk_decay, 1, 0)))
    prev = jnp.moveaxis(prev, 0, 1)
    y = y + jnp.einsum('bclgn,bcghpn->bclghp', c_in, prev) * jnp.exp(a_cs)[..., None]
    y = y + D_skip.astype(f32).reshape(SSD_GROUPS, hpg, 1) * xs
    y = y.reshape(bsz, n_chunks * CHUNK, SSD_D_INNER)[:, FRONT:FRONT + seq_len]
    yz = (y * jax.nn.silu(z.astype(f32))).reshape(bsz, seq_len, SSD_GROUPS, -1)
    yz = yz * lax.rsqrt(jnp.mean(yz * yz, axis=-1, keepdims=True) + EPS)
    return (yz.reshape(bsz, seq_len, SSD_D_INNER) * norm_w.astype(f32)).astype(z.dtype)


def _rotary(t, cos, sin):
    half = t.shape[-1] // 2
    t1, t2 = t[..., :half], t[..., half:]
    return jnp.concatenate([t1 * cos - t2 * sin, t2 * cos + t1 * sin], axis=-1).astype(t.dtype)


def retention_mixer(q, k, v, g):
    f32 = jnp.float32
    bsz, seq_len, _ = q.shape
    q = q.reshape(bsz, seq_len, RET_HEADS, RET_HEAD_QK)
    k = k.reshape(bsz, seq_len, RET_HEADS, RET_HEAD_QK)
    v = v.reshape(bsz, seq_len, RET_HEADS, RET_HEAD_V)
    pos = jnp.arange(seq_len, dtype=f32)
    inv_freq = ROPE_BASE ** (-jnp.linspace(0.0, 1.0, RET_HEAD_QK // 2, dtype=f32))
    ang = pos[:, None] * inv_freq[None, :]
    cos = jnp.cos(ang)[None, :, None, :]
    sin = jnp.sin(ang)[None, :, None, :]
    q = _rotary(q, cos, sin)
    k = _rotary(k, cos, sin) * (RET_HEAD_QK ** -0.5)
    q, k, v = [_pad_to_chunks(t) for t in (q, k, v)]
    n_chunks = q.shape[1] // CHUNK
    q = q.reshape(bsz, n_chunks, CHUNK, RET_HEADS, RET_HEAD_QK)
    k = k.reshape(bsz, n_chunks, CHUNK, RET_HEADS, RET_HEAD_QK)
    v = v.reshape(bsz, n_chunks, CHUNK, RET_HEADS, RET_HEAD_V)
    log_gamma = jnp.log(1.0 - 2.0 ** (-5.0 - jnp.arange(RET_HEADS, dtype=f32)))
    idx = jnp.arange(CHUNK, dtype=f32)
    causal = jnp.tril(jnp.ones((CHUNK, CHUNK), dtype=bool))
    dist = (idx[:, None] - idx[None, :])[..., None] * log_gamma
    decay_ls = jnp.exp(jnp.where(causal[..., None], dist, -jnp.inf)).transpose(2, 0, 1)
    scores = jnp.einsum('bclhd,bcshd->bchls', q, k) * decay_ls
    out = jnp.einsum('bchls,bcshe->bclhe', scores, v)
    k_dec = k * jnp.exp((CHUNK - 1.0 - idx)[:, None] * log_gamma)[..., None]
    kv = jnp.einsum('bcshd,bcshe->bchde', k_dec, v)
    chunk_decay = jnp.exp(CHUNK * log_gamma)

    def step(state, kv_c):
        return state * chunk_decay[:, None, None] + kv_c, state

    init = jnp.zeros_like(kv[:, 0])
    _, prev = lax.scan(step, init, jnp.moveaxis(kv, 1, 0))
    prev = jnp.moveaxis(prev, 0, 1)
    cross = jnp.einsum('bclhd,bchde->bclhe', q, prev) * jnp.exp((idx + 1.0)[:, None] * log_gamma)[..., None]
    out = (out + cross).reshape(bsz, n_chunks * CHUNK, RET_HEADS, RET_HEAD_V)[:, FRONT:FRONT + seq_len]
    out = out.astype(f32)
    out = out * lax.rsqrt(jnp.mean(out * out, axis=-1, keepdims=True) + EPS)
    return (jax.nn.silu(g.astype(f32)) * out.reshape(bsz, seq_len, RET_V_DIM)).astype(g.dtype)


def _fwd_setup_inputs(seed: int = 0) -> dict:
    key = jax.random.key(seed)
    ks = jax.random.split(key, 24)
    f32 = jnp.float32
    nrm = lambda k, shape, scale: jax.random.normal(k, shape, f32) * scale
    dt0 = jnp.exp(jax.random.uniform(ks[6], (DEPTH, SSD_HEADS), f32, np.log(1e-3), np.log(1e-1)))
    return {
        'x': nrm(ks[0], (BATCH, SEQ, D_MODEL), 1.0),
        'meta_tokens': nrm(ks[1], (N_META, D_MODEL), 1.0),
        'mix_norm_w': 1.0 + nrm(ks[2], (DEPTH, D_MODEL), 0.01),
        'w_in': nrm(ks[3], (DEPTH, D_MODEL, IN_PROJ_DIM), D_MODEL ** -0.5),
        'ssd_conv_w': nrm(ks[4], (DEPTH, SSD_CONV, SSD_CONV_DIM), SSD_CONV ** -0.5),
        'ssd_conv_b': nrm(ks[5], (DEPTH, SSD_CONV_DIM), 0.01),
        'ssd_dt_bias': dt0 + jnp.log(-jnp.expm1(-dt0)),
        'ssd_A_log': jnp.log(jax.random.uniform(ks[7], (DEPTH, SSD_HEADS), f32, 1.0, 16.0)),
        'ssd_D': 1.0 + nrm(ks[8], (DEPTH, SSD_HEADS), 0.01),
        'ssd_norm_w': 1.0 + nrm(ks[9], (DEPTH, SSD_D_INNER), 0.01),
        'w_branch_ssd': nrm(ks[10], (DEPTH, SSD_D_INNER, D_MODEL), SSD_D_INNER ** -0.5),
        'w_branch_ret': nrm(ks[11], (DEPTH, RET_V_DIM, D_MODEL), RET_V_DIM ** -0.5),
        'w_out': nrm(ks[12], (DEPTH, D_MODEL, D_MODEL), D_MODEL ** -0.5),
        'ffn_norm_w': 1.0 + nrm(ks[13], (DEPTH, D_MODEL), 0.01),
        'w_up': nrm(ks[14], (DEPTH, D_MODEL, 2 * D_FF), D_MODEL ** -0.5),
        'ffn_conv_w': nrm(ks[15], (DEPTH, FFN_CONV, 2 * D_FF), FFN_CONV ** -0.5),
        'ffn_conv_b': nrm(ks[16], (DEPTH, 2 * D_FF), 0.01),
        'w_down': nrm(ks[17], (DEPTH, D_FF, D_MODEL), D_FF ** -0.5),
        'final_norm_w': 1.0 + nrm(ks[18], (D_MODEL,), 0.01),
    }


def _fwd_reference(x, meta_tokens, mix_norm_w, w_in, ssd_conv_w, ssd_conv_b, ssd_dt_bias, ssd_A_log,
              ssd_D, ssd_norm_w, w_branch_ssd, w_branch_ret, w_out, ffn_norm_w, w_up,
              ffn_conv_w, ffn_conv_b, w_down, final_norm_w):
    bsz = x.shape[0]
    meta = jnp.broadcast_to(meta_tokens[None].astype(x.dtype), (bsz, N_META, D_MODEL))
    h = jnp.concatenate([meta, x], axis=1)
    for layer in range(DEPTH):
        u = rmsnorm(h, mix_norm_w[layer])
        proj = u @ w_in[layer]
        z, xbc, dt_raw, q, k, v, g, gate_ssd, gate_ret = jnp.split(proj, _split_points(), axis=-1)
        y_ssd = ssd_mixer(z, xbc, dt_raw, ssd_conv_w[layer], ssd_conv_b[layer], ssd_dt_bias[layer],
                          ssd_A_log[layer], ssd_D[layer], ssd_norm_w[layer])
        y_ret = retention_mixer(q, k, v, g)
        merged = (jax.nn.sigmoid(gate_ssd) * (y_ssd @ w_branch_ssd[layer])
                  + jax.nn.sigmoid(gate_ret) * (y_ret @ w_branch_ret[layer]))
        h = h + merged @ w_out[layer]
        u = rmsnorm(h, ffn_norm_w[layer])
        a = causal_dwconv(u @ w_up[layer], ffn_conv_w[layer], ffn_conv_b[layer])
        a_gate, a_val = jnp.split(a, 2, axis=-1)
        h = h + (jax.nn.silu(a_gate) * a_val) @ w_down[layer]
    out = rmsnorm(h, final_norm_w)
    return out[:, N_META:]


import jax as _jax
import jax.numpy as _jnp

TWIN_FORMAT = 'train_step'
FWD_PARAMS = ['x', 'meta_tokens', 'mix_norm_w', 'w_in', 'ssd_conv_w', 'ssd_conv_b', 'ssd_dt_bias', 'ssd_A_log', 'ssd_D', 'ssd_norm_w', 'w_branch_ssd', 'w_branch_ret', 'w_out', 'ffn_norm_w', 'w_up', 'ffn_conv_w', 'ffn_conv_b', 'w_down', 'final_norm_w']
TWIN_WEIGHTS = ['meta_tokens', 'mix_norm_w', 'w_in', 'ssd_conv_w', 'ssd_conv_b', 'ssd_dt_bias', 'ssd_A_log', 'ssd_D', 'ssd_norm_w', 'w_branch_ssd', 'w_branch_ret', 'w_out', 'ffn_norm_w', 'w_up', 'ffn_conv_w', 'ffn_conv_b', 'w_down', 'final_norm_w']
TWIN_DIFF_INPUT = 'x'
TWIN_INPUTS = ['x', 'meta_tokens', 'mix_norm_w', 'w_in', 'ssd_conv_w', 'ssd_conv_b', 'ssd_dt_bias', 'ssd_A_log', 'ssd_D', 'ssd_norm_w', 'w_branch_ssd', 'w_branch_ret', 'w_out', 'ffn_norm_w', 'w_up', 'ffn_conv_w', 'ffn_conv_b', 'w_down', 'final_norm_w', 'loss_target', 'm_meta_tokens', 'm_mix_norm_w', 'm_w_in', 'm_ssd_conv_w', 'm_ssd_conv_b', 'm_ssd_dt_bias', 'm_ssd_A_log', 'm_ssd_D', 'm_ssd_norm_w', 'm_w_branch_ssd', 'm_w_branch_ret', 'm_w_out', 'm_ffn_norm_w', 'm_w_up', 'm_ffn_conv_w', 'm_ffn_conv_b', 'm_w_down', 'm_final_norm_w', 'v_meta_tokens', 'v_mix_norm_w', 'v_w_in', 'v_ssd_conv_w', 'v_ssd_conv_b', 'v_ssd_dt_bias', 'v_ssd_A_log', 'v_ssd_D', 'v_ssd_norm_w', 'v_w_branch_ssd', 'v_w_branch_ret', 'v_w_out', 'v_ffn_norm_w', 'v_w_up', 'v_ffn_conv_w', 'v_ffn_conv_b', 'v_w_down', 'v_final_norm_w']
TWIN_OUTPUTS = ['loss', 'grad_x', 'grad_meta_tokens', 'grad_mix_norm_w', 'grad_w_in', 'grad_ssd_conv_w', 'grad_ssd_conv_b', 'grad_ssd_dt_bias', 'grad_ssd_A_log', 'grad_ssd_D', 'grad_ssd_norm_w', 'grad_w_branch_ssd', 'grad_w_branch_ret', 'grad_w_out', 'grad_ffn_norm_w', 'grad_w_up', 'grad_ffn_conv_w', 'grad_ffn_conv_b', 'grad_w_down', 'grad_final_norm_w', 'delta_meta_tokens', 'delta_mix_norm_w', 'delta_w_in', 'delta_ssd_conv_w', 'delta_ssd_conv_b', 'delta_ssd_dt_bias', 'delta_ssd_A_log', 'delta_ssd_D', 'delta_ssd_norm_w', 'delta_w_branch_ssd', 'delta_w_branch_ret', 'delta_w_out', 'delta_ffn_norm_w', 'delta_w_up', 'delta_ffn_conv_w', 'delta_ffn_conv_b', 'delta_w_down', 'delta_final_norm_w', 'new_m_meta_tokens', 'new_m_mix_norm_w', 'new_m_w_in', 'new_m_ssd_conv_w', 'new_m_ssd_conv_b', 'new_m_ssd_dt_bias', 'new_m_ssd_A_log', 'new_m_ssd_D', 'new_m_ssd_norm_w', 'new_m_w_branch_ssd', 'new_m_w_branch_ret', 'new_m_w_out', 'new_m_ffn_norm_w', 'new_m_w_up', 'new_m_ffn_conv_w', 'new_m_ffn_conv_b', 'new_m_w_down', 'new_m_final_norm_w', 'new_v_meta_tokens', 'new_v_mix_norm_w', 'new_v_w_in', 'new_v_ssd_conv_w', 'new_v_ssd_conv_b', 'new_v_ssd_dt_bias', 'new_v_ssd_A_log', 'new_v_ssd_D', 'new_v_ssd_norm_w', 'new_v_w_branch_ssd', 'new_v_w_branch_ret', 'new_v_w_out', 'new_v_ffn_norm_w', 'new_v_w_up', 'new_v_ffn_conv_w', 'new_v_ffn_conv_b', 'new_v_w_down', 'new_v_final_norm_w']
TWIN_LEAF_KINDS = {'loss': 'loss', 'grad_x': 'grad_x', 'grad_meta_tokens': 'grad_w', 'grad_mix_norm_w': 'grad_w', 'grad_w_in': 'grad_w', 'grad_ssd_conv_w': 'grad_w', 'grad_ssd_conv_b': 'grad_w', 'grad_ssd_dt_bias': 'grad_w', 'grad_ssd_A_log': 'grad_w', 'grad_ssd_D': 'grad_w', 'grad_ssd_norm_w': 'grad_w', 'grad_w_branch_ssd': 'grad_w', 'grad_w_branch_ret': 'grad_w', 'grad_w_out': 'grad_w', 'grad_ffn_norm_w': 'grad_w', 'grad_w_up': 'grad_w', 'grad_ffn_conv_w': 'grad_w', 'grad_ffn_conv_b': 'grad_w', 'grad_w_down': 'grad_w', 'grad_final_norm_w': 'grad_w', 'delta_meta_tokens': 'delta_w', 'delta_mix_norm_w': 'delta_w', 'delta_w_in': 'delta_w', 'delta_ssd_conv_w': 'delta_w', 'delta_ssd_conv_b': 'delta_w', 'delta_ssd_dt_bias': 'delta_w', 'delta_ssd_A_log': 'delta_w', 'delta_ssd_D': 'delta_w', 'delta_ssd_norm_w': 'delta_w', 'delta_w_branch_ssd': 'delta_w', 'delta_w_branch_ret': 'delta_w', 'delta_w_out': 'delta_w', 'delta_ffn_norm_w': 'delta_w', 'delta_w_up': 'delta_w', 'delta_ffn_conv_w': 'delta_w', 'delta_ffn_conv_b': 'delta_w', 'delta_w_down': 'delta_w', 'delta_final_norm_w': 'delta_w', 'new_m_meta_tokens': 'new_m', 'new_m_mix_norm_w': 'new_m', 'new_m_w_in': 'new_m', 'new_m_ssd_conv_w': 'new_m', 'new_m_ssd_conv_b': 'new_m', 'new_m_ssd_dt_bias': 'new_m', 'new_m_ssd_A_log': 'new_m', 'new_m_ssd_D': 'new_m', 'new_m_ssd_norm_w': 'new_m', 'new_m_w_branch_ssd': 'new_m', 'new_m_w_branch_ret': 'new_m', 'new_m_w_out': 'new_m', 'new_m_ffn_norm_w': 'new_m', 'new_m_w_up': 'new_m', 'new_m_ffn_conv_w': 'new_m', 'new_m_ffn_conv_b': 'new_m', 'new_m_w_down': 'new_m', 'new_m_final_norm_w': 'new_m', 'new_v_meta_tokens': 'new_v', 'new_v_mix_norm_w': 'new_v', 'new_v_w_in': 'new_v', 'new_v_ssd_conv_w': 'new_v', 'new_v_ssd_conv_b': 'new_v', 'new_v_ssd_dt_bias': 'new_v', 'new_v_ssd_A_log': 'new_v', 'new_v_ssd_D': 'new_v', 'new_v_ssd_norm_w': 'new_v', 'new_v_w_branch_ssd': 'new_v', 'new_v_w_branch_ret': 'new_v', 'new_v_w_out': 'new_v', 'new_v_ffn_norm_w': 'new_v', 'new_v_w_up': 'new_v', 'new_v_ffn_conv_w': 'new_v', 'new_v_ffn_conv_b': 'new_v', 'new_v_w_down': 'new_v', 'new_v_final_norm_w': 'new_v'}


def _forward(args):
    return _fwd_reference(*[args[k] for k in FWD_PARAMS])


def _output_shape():
    def fwd():
        inp = _fwd_setup_inputs(0)
        return _fwd_reference(*[inp[k] for k in FWD_PARAMS])
    out = _jax.eval_shape(fwd)
    return out.shape, out.dtype

N_MICROBATCH = 1
ADAM_LR = 0.001
ADAM_B1 = 0.9
ADAM_B2 = 0.999
ADAM_EPS = 1e-08
ADAM_WD = 0.01
ADAM_STEP = 10
PER_EXAMPLE_BATCH_AXIS = {'x': 0, 'loss_target': 0}
SHARED_INPUTS = []
_WEIGHT_DTYPES = {'meta_tokens': _jnp.float32, 'mix_norm_w': _jnp.float32, 'w_in': _jnp.float32, 'ssd_conv_w': _jnp.float32, 'ssd_conv_b': _jnp.float32, 'ssd_dt_bias': _jnp.float32, 'ssd_A_log': _jnp.float32, 'ssd_D': _jnp.float32, 'ssd_norm_w': _jnp.float32, 'w_branch_ssd': _jnp.float32, 'w_branch_ret': _jnp.float32, 'w_out': _jnp.float32, 'ffn_norm_w': _jnp.float32, 'w_up': _jnp.float32, 'ffn_conv_w': _jnp.float32, 'ffn_conv_b': _jnp.float32, 'w_down': _jnp.float32, 'final_norm_w': _jnp.float32}
MOMENT_SCALE = {'meta_tokens': 9.424080e-03, 'mix_norm_w': 2.494747e-01, 'w_in': 7.000088e-02, 'ssd_conv_w': 7.714109e-02, 'ssd_conv_b': 1.070737e-01, 'ssd_dt_bias': 1.762915e-01, 'ssd_A_log': 3.090906e-01, 'ssd_D': 4.875109e-01, 'ssd_norm_w': 9.332126e-02, 'w_branch_ssd': 1.272595e-01, 'w_branch_ret': 7.609632e-02, 'w_out': 1.478968e-01, 'ffn_norm_w': 1.691539e-01, 'w_up': 7.160383e-02, 'ffn_conv_w': 7.051700e-02, 'ffn_conv_b': 7.129245e-02, 'w_down': 1.176304e-01, 'final_norm_w': 6.392745e+01}


def _to_microbatches(a, axis):
    t = _jnp.moveaxis(a, axis, 0)
    t = t.reshape((N_MICROBATCH, t.shape[0] // N_MICROBATCH) + t.shape[1:])
    return _jnp.moveaxis(t, 1, axis + 1)


def setup_inputs(seed: int = 0) -> dict:
    inp = _fwd_setup_inputs(seed)
    key = _jax.random.fold_in(_jax.random.key(seed), 7919)
    shape, _ = _output_shape()
    out = dict(inp)
    out["loss_target"] = _jax.random.normal(_jax.random.fold_in(key, 0), shape, _jnp.float32)
    for i, name in enumerate(TWIN_WEIGHTS):
        w = inp[name].astype(_jnp.float32)
        if MOMENT_SCALE is None:
            s = _jnp.sqrt(_jnp.mean(_jnp.square(w)) + 1e-30)
        else:
            s = MOMENT_SCALE[name]
        km, kv = _jax.random.split(_jax.random.fold_in(key, i + 1))
        out[name] = w
        out["m_" + name] = s * _jax.random.normal(km, w.shape, _jnp.float32)
        out["v_" + name] = (s * s) * _jax.random.uniform(kv, w.shape, _jnp.float32, 0.5, 1.5)
    if N_MICROBATCH > 1:
        for name, axis in PER_EXAMPLE_BATCH_AXIS.items():
            out[name] = _to_microbatches(out[name], axis)
    return {'x': out['x'], 'meta_tokens': out['meta_tokens'], 'mix_norm_w': out['mix_norm_w'], 'w_in': out['w_in'], 'ssd_conv_w': out['ssd_conv_w'], 'ssd_conv_b': out['ssd_conv_b'], 'ssd_dt_bias': out['ssd_dt_bias'], 'ssd_A_log': out['ssd_A_log'], 'ssd_D': out['ssd_D'], 'ssd_norm_w': out['ssd_norm_w'], 'w_branch_ssd': out['w_branch_ssd'], 'w_branch_ret': out['w_branch_ret'], 'w_out': out['w_out'], 'ffn_norm_w': out['ffn_norm_w'], 'w_up': out['w_up'], 'ffn_conv_w': out['ffn_conv_w'], 'ffn_conv_b': out['ffn_conv_b'], 'w_down': out['w_down'], 'final_norm_w': out['final_norm_w'], 'loss_target': out['loss_target'], 'm_meta_tokens': out['m_meta_tokens'], 'm_mix_norm_w': out['m_mix_norm_w'], 'm_w_in': out['m_w_in'], 'm_ssd_conv_w': out['m_ssd_conv_w'], 'm_ssd_conv_b': out['m_ssd_conv_b'], 'm_ssd_dt_bias': out['m_ssd_dt_bias'], 'm_ssd_A_log': out['m_ssd_A_log'], 'm_ssd_D': out['m_ssd_D'], 'm_ssd_norm_w': out['m_ssd_norm_w'], 'm_w_branch_ssd': out['m_w_branch_ssd'], 'm_w_branch_ret': out['m_w_branch_ret'], 'm_w_out': out['m_w_out'], 'm_ffn_norm_w': out['m_ffn_norm_w'], 'm_w_up': out['m_w_up'], 'm_ffn_conv_w': out['m_ffn_conv_w'], 'm_ffn_conv_b': out['m_ffn_conv_b'], 'm_w_down': out['m_w_down'], 'm_final_norm_w': out['m_final_norm_w'], 'v_meta_tokens': out['v_meta_tokens'], 'v_mix_norm_w': out['v_mix_norm_w'], 'v_w_in': out['v_w_in'], 'v_ssd_conv_w': out['v_ssd_conv_w'], 'v_ssd_conv_b': out['v_ssd_conv_b'], 'v_ssd_dt_bias': out['v_ssd_dt_bias'], 'v_ssd_A_log': out['v_ssd_A_log'], 'v_ssd_D': out['v_ssd_D'], 'v_ssd_norm_w': out['v_ssd_norm_w'], 'v_w_branch_ssd': out['v_w_branch_ssd'], 'v_w_branch_ret': out['v_w_branch_ret'], 'v_w_out': out['v_w_out'], 'v_ffn_norm_w': out['v_ffn_norm_w'], 'v_w_up': out['v_w_up'], 'v_ffn_conv_w': out['v_ffn_conv_w'], 'v_ffn_conv_b': out['v_ffn_conv_b'], 'v_w_down': out['v_w_down'], 'v_final_norm_w': out['v_final_norm_w']}


def _loss(weights, diff, rest, loss_target):
    with _jax.named_scope("forward"):
        args = {**rest, TWIN_DIFF_INPUT: diff, **{k: w.astype(_WEIGHT_DTYPES[k]) for k, w in weights.items()}}
        y = _forward(args)
    with _jax.named_scope("loss_head"):
        err = _jnp.square(y.astype(_jnp.float32) - loss_target)
        return 0.5 * _jnp.sum(_jnp.mean(err, axis=-1)) if err.ndim else 0.5 * err


def _adamw(w, g, m, v):
    m = ADAM_B1 * m + (1.0 - ADAM_B1) * g
    v = ADAM_B2 * v + (1.0 - ADAM_B2) * _jnp.square(g)
    m_hat = m / (1.0 - ADAM_B1 ** ADAM_STEP)
    v_hat = v / (1.0 - ADAM_B2 ** ADAM_STEP)
    delta = -ADAM_LR * (m_hat / (_jnp.sqrt(v_hat) + ADAM_EPS) + ADAM_WD * w)
    return delta, m, v


def reference(x, meta_tokens, mix_norm_w, w_in, ssd_conv_w, ssd_conv_b, ssd_dt_bias, ssd_A_log, ssd_D, ssd_norm_w, w_branch_ssd, w_branch_ret, w_out, ffn_norm_w, w_up, ffn_conv_w, ffn_conv_b, w_down, final_norm_w, loss_target, m_meta_tokens, m_mix_norm_w, m_w_in, m_ssd_conv_w, m_ssd_conv_b, m_ssd_dt_bias, m_ssd_A_log, m_ssd_D, m_ssd_norm_w, m_w_branch_ssd, m_w_branch_ret, m_w_out, m_ffn_norm_w, m_w_up, m_ffn_conv_w, m_ffn_conv_b, m_w_down, m_final_norm_w, v_meta_tokens, v_mix_norm_w, v_w_in, v_ssd_conv_w, v_ssd_conv_b, v_ssd_dt_bias, v_ssd_A_log, v_ssd_D, v_ssd_norm_w, v_w_branch_ssd, v_w_branch_ret, v_w_out, v_ffn_norm_w, v_w_up, v_ffn_conv_w, v_ffn_conv_b, v_w_down, v_final_norm_w):
    given = dict(x=x, meta_tokens=meta_tokens, mix_norm_w=mix_norm_w, w_in=w_in, ssd_conv_w=ssd_conv_w, ssd_conv_b=ssd_conv_b, ssd_dt_bias=ssd_dt_bias, ssd_A_log=ssd_A_log, ssd_D=ssd_D, ssd_norm_w=ssd_norm_w, w_branch_ssd=w_branch_ssd, w_branch_ret=w_branch_ret, w_out=w_out, ffn_norm_w=ffn_norm_w, w_up=w_up, ffn_conv_w=ffn_conv_w, ffn_conv_b=ffn_conv_b, w_down=w_down, final_norm_w=final_norm_w, loss_target=loss_target, m_meta_tokens=m_meta_tokens, m_mix_norm_w=m_mix_norm_w, m_w_in=m_w_in, m_ssd_conv_w=m_ssd_conv_w, m_ssd_conv_b=m_ssd_conv_b, m_ssd_dt_bias=m_ssd_dt_bias, m_ssd_A_log=m_ssd_A_log, m_ssd_D=m_ssd_D, m_ssd_norm_w=m_ssd_norm_w, m_w_branch_ssd=m_w_branch_ssd, m_w_branch_ret=m_w_branch_ret, m_w_out=m_w_out, m_ffn_norm_w=m_ffn_norm_w, m_w_up=m_w_up, m_ffn_conv_w=m_ffn_conv_w, m_ffn_conv_b=m_ffn_conv_b, m_w_down=m_w_down, m_final_norm_w=m_final_norm_w, v_meta_tokens=v_meta_tokens, v_mix_norm_w=v_mix_norm_w, v_w_in=v_w_in, v_ssd_conv_w=v_ssd_conv_w, v_ssd_conv_b=v_ssd_conv_b, v_ssd_dt_bias=v_ssd_dt_bias, v_ssd_A_log=v_ssd_A_log, v_ssd_D=v_ssd_D, v_ssd_norm_w=v_ssd_norm_w, v_w_branch_ssd=v_w_branch_ssd, v_w_branch_ret=v_w_branch_ret, v_w_out=v_w_out, v_ffn_norm_w=v_ffn_norm_w, v_w_up=v_w_up, v_ffn_conv_w=v_ffn_conv_w, v_ffn_conv_b=v_ffn_conv_b, v_w_down=v_w_down, v_final_norm_w=v_final_norm_w)
    weights = {n: given[n] for n in TWIN_WEIGHTS}
    shared = {n: given[n] for n in SHARED_INPUTS}
    per_example = {n: given[n] for n in ['x']}
    grad_fn = _jax.value_and_grad(_loss, argnums=(0, 1))

    def one_microbatch(ex, loss_target):
        ex = dict(ex)
        diff = ex.pop(TWIN_DIFF_INPUT)
        return grad_fn(weights, diff, {**shared, **ex}, loss_target)

    if N_MICROBATCH == 1:
        loss, (grad_w, grad_x) = one_microbatch(per_example, given["loss_target"])
    else:
        def body(carry, xs):
            loss_sum, grad_sum = carry
            l_k, (gw_k, gx_k) = one_microbatch(xs[0], xs[1])
            with _jax.named_scope("update"):
                return (loss_sum + l_k, _jax.tree.map(_jnp.add, grad_sum, gw_k)), gx_k

        init = (_jnp.zeros((), _jnp.float32), _jax.tree.map(_jnp.zeros_like, weights))
        (loss, grad_w), grad_x = _jax.lax.scan(body, init, (per_example, given["loss_target"]))
    with _jax.named_scope("update"):
        delta_w, new_m, new_v = {}, {}, {}
        for n in TWIN_WEIGHTS:
            delta_w[n], new_m[n], new_v[n] = _adamw(weights[n], grad_w[n], given["m_" + n], given["v_" + n])
    return (loss, grad_x, *[grad_w[n] for n in TWIN_WEIGHTS], *[delta_w[n] for n in TWIN_WEIGHTS],
            *[new_m[n] for n in TWIN_WEIGHTS], *[new_v[n] for n in TWIN_WEIGHTS])
```

```python
import functools
import math

import jax
import jax.numpy as jnp
from jax import lax
from jax.experimental import pallas as pl
from jax.experimental.pallas import tpu as pltpu

F32 = jnp.float32
MXU_DTYPE = jnp.bfloat16
ACT_DTYPE = jnp.bfloat16
COMM_DTYPE = jnp.bfloat16

N_META = 16
CHUNK = 128
FRONT = CHUNK - N_META
PAD_ROWS = FRONT + N_META
EPS = 1e-6
N_DEV = 8

SSD_D_INNER = 2048
SSD_HEAD_DIM = 64
SSD_HEADS = 32
SSD_GROUPS = 4
SSD_HPG = SSD_HEADS // SSD_GROUPS
SSD_STATE = 128
SSD_CONV = 4
SSD_CONV_DIM = SSD_D_INNER + 2 * SSD_GROUPS * SSD_STATE
SSD_GW = SSD_D_INNER // SSD_GROUPS
RET_HEADS = 4
RET_QK = 256
RET_V = 512
ROPE_BASE = 10000.0
FFN_CONV = 3
HALO = 16
LANES = 128

ADAM_LR = 0.001
ADAM_B1 = 0.9
ADAM_B2 = 0.999
ADAM_EPS = 1e-08
ADAM_WD = 0.01
ADAM_STEP = 10

VMEM_LIMIT = 56 * 1024 * 1024
MESH = pl.DeviceIdType.MESH

NN = (((1,), (0,)), ((), ()))
NT = (((1,), (1,)), ((), ()))
TN = (((0,), (0,)), ((), ()))


def _params(sem):
    return pltpu.CompilerParams(dimension_semantics=sem, vmem_limit_bytes=VMEM_LIMIT)


def _dot(a, b, dn=NN):
    return lax.dot_general(a.astype(MXU_DTYPE), b.astype(MXU_DTYPE), dn, preferred_element_type=F32)


def _silu(x):
    return x * jax.nn.sigmoid(x)


def _dsilu(x):
    s = jax.nn.sigmoid(x)
    return s * (1.0 + x * (1.0 - s))


def _row_tile(rows):
    return 640 if rows % 640 == 0 else 128


def _mm(a, b, *, mode, out_dtype, tm, tn, tk, name, add=None):
    if mode == "nn":
        (m, k), (k2, n) = a.shape, b.shape
    elif mode == "nt":
        (m, k), (n, k2) = a.shape, b.shape
    else:
        (k, m), (k2, n) = a.shape, b.shape
    assert k == k2 and m % tm == 0 and n % tn == 0 and k % tk == 0, (name, a.shape, b.shape, tm, tn, tk)
    nk = k // tk
    dn = {"nn": NN, "nt": NT, "tn": TN}[mode]
    has_add = add is not None

    def body(*refs):
        a_ref, b_ref = refs[0], refs[1]
        add_ref = refs[2] if has_add else None
        o_ref = refs[2 + has_add]
        p = _dot(a_ref[...], b_ref[...], dn)
        if nk == 1:
            if has_add:
                p = p + add_ref[...]
            o_ref[...] = p.astype(out_dtype)
        else:
            acc_ref = refs[3 + has_add]
            kk = pl.program_id(2)

            @pl.when(kk == 0)
            def _():
                acc_ref[...] = p

            @pl.when(kk > 0)
            def _():
                acc_ref[...] += p

            @pl.when(kk == nk - 1)
            def _():
                r = acc_ref[...]
                if has_add:
                    r = r + add_ref[...]
                o_ref[...] = r.astype(out_dtype)

    if mode == "tn":
        a_spec = pl.BlockSpec((tk, tm), lambda j, i, kk: (kk, i))
    else:
        a_spec = pl.BlockSpec((tm, tk), lambda j, i, kk: (i, kk))
    if mode == "nt":
        b_spec = pl.BlockSpec((tn, tk), lambda j, i, kk: (j, kk))
    else:
        b_spec = pl.BlockSpec((tk, tn), lambda j, i, kk: (kk, j))
    o_spec = pl.BlockSpec((tm, tn), lambda j, i, kk: (i, j))
    in_specs = [a_spec, b_spec] + ([o_spec] if has_add else [])
    args = (a, b) + ((add,) if has_add else ())
    return pl.pallas_call(
        body, name=name, grid=(n // tn, m // tm, nk), in_specs=in_specs, out_specs=o_spec,
        out_shape=jax.ShapeDtypeStruct((m, n), out_dtype),
        scratch_shapes=[pltpu.VMEM((tm, tn), F32)] if nk > 1 else [],
        compiler_params=_params(("parallel", "parallel", "arbitrary")),
    )(*args)


def _pick(n, cands):
    for c in cands:
        if n % c == 0:
            return c
    return n


def _rms_fwd(h, w, name):
    rows, d = h.shape
    tm = _row_tile(rows)

    def body(h_ref, w_ref, u_ref):
        x = h_ref[...]
        r = lax.rsqrt(jnp.mean(x * x, axis=-1, keepdims=True) + EPS)
        u_ref[...] = (x * r * w_ref[...]).astype(ACT_DTYPE)

    return pl.pallas_call(
        body, name=name, grid=(rows // tm,),
        in_specs=[pl.BlockSpec((tm, d), lambda i: (i, 0)), pl.BlockSpec((1, d), lambda i: (0, 0))],
        out_specs=pl.BlockSpec((tm, d), lambda i: (i, 0)),
        out_shape=jax.ShapeDtypeStruct((rows, d), ACT_DTYPE),
        compiler_params=_params(("parallel",)),
    )(h, w)


def _rms_bwd(du, h, w, dres, name):
    rows, d = h.shape
    tm = _row_tile(rows)

    def body(du_ref, h_ref, w_ref, dres_ref, dh_ref, dw_ref):
        x = h_ref[...]
        dy = du_ref[...].astype(F32)
        r = lax.rsqrt(jnp.mean(x * x, axis=-1, keepdims=True) + EPS)
        xhat = x * r
        dxn = dy * w_ref[...]
        dx = r * (dxn - xhat * jnp.mean(dxn * xhat, axis=-1, keepdims=True))
        dh_ref[...] = dres_ref[...] + dx

        @pl.when(pl.program_id(0) == 0)
        def _():
            dw_ref[...] = jnp.zeros_like(dw_ref)

        dw_ref[...] += jnp.sum(dy * xhat, axis=0, keepdims=True)

    return pl.pallas_call(
        body, name=name, grid=(rows // tm,),
        in_specs=[pl.BlockSpec((tm, d), lambda i: (i, 0)), pl.BlockSpec((tm, d), lambda i: (i, 0)),
                  pl.BlockSpec((1, d), lambda i: (0, 0)), pl.BlockSpec((tm, d), lambda i: (i, 0))],
        out_specs=[pl.BlockSpec((tm, d), lambda i: (i, 0)), pl.BlockSpec((1, d), lambda i: (0, 0))],
        out_shape=[jax.ShapeDtypeStruct((rows, d), F32), jax.ShapeDtypeStruct((1, d), F32)],
        compiler_params=_params(("arbitrary",)),
    )(du, h, w, dres)


def _loss_head(h2, tgt, w):
    rows, d = h2.shape
    tm = _row_tile(rows)

    def body(h_ref, t_ref, w_ref, dh_ref, loss_ref, dw_ref):
        i = pl.program_id(0)
        x = h_ref[...]
        r = lax.rsqrt(jnp.mean(x * x, axis=-1, keepdims=True) + EPS)
        xhat = x * r
        wv = w_ref[...]
        row = i * tm + lax.broadcasted_iota(jnp.int32, (tm, 1), 0)
        live = row >= PAD_ROWS
        diff = jnp.where(live, xhat * wv - t_ref[...], 0.0)
        dy = diff * (1.0 / d)
        dxn = dy * wv
        dh_ref[...] = r * (dxn - xhat * jnp.mean(dxn * xhat, axis=-1, keepdims=True))

        @pl.when(i == 0)
        def _():
            loss_ref[...] = jnp.zeros_like(loss_ref)
            dw_ref[...] = jnp.zeros_like(dw_ref)

        loss_ref[...] += 0.5 * jnp.sum(jnp.mean(diff * diff, axis=-1, keepdims=True))
        dw_ref[...] += jnp.sum(dy * xhat, axis=0, keepdims=True)

    return pl.pallas_call(
        body, name="loss_head", grid=(rows // tm,),
        in_specs=[pl.BlockSpec((tm, d), lambda i: (i, 0)), pl.BlockSpec((tm, d), lambda i: (i, 0)),
                  pl.BlockSpec((1, d), lambda i: (0, 0))],
        out_specs=[pl.BlockSpec((tm, d), lambda i: (i, 0)), pl.BlockSpec((8, LANES), lambda i: (0, 0)),
                   pl.BlockSpec((1, d), lambda i: (0, 0))],
        out_shape=[jax.ShapeDtypeStruct((rows, d), F32), jax.ShapeDtypeStruct((8, LANES), F32),
                   jax.ShapeDtypeStruct((1, d), F32)],
        compiler_params=_params(("arbitrary",)),
    )(h2, tgt, w)


def _prev_halo_spec(tm, width, col):
    return pl.BlockSpec((HALO, width), lambda j, i: (jnp.maximum(i * (tm // HALO) - 1, 0), col(j)))


def _next_halo_spec(tm, rows, width, col):
    last = rows // HALO - 1
    return pl.BlockSpec((HALO, width), lambda j, i: (jnp.minimum((i + 1) * (tm // HALO), last), col(j)))


def _conv_taps(cat, w_ref, b_ref, kw):
    acc = b_ref[...] + w_ref[kw - 1:kw, :] * cat
    for s in range(1, kw):
        acc = acc + w_ref[kw - 1 - s:kw - s, :] * pltpu.roll(cat, s, 0)
    return acc


def _conv_back(dpre, w_ref, kw):
    n = dpre.shape[0]
    acc = w_ref[kw - 1:kw, :] * dpre
    for s in range(1, kw):
        acc = acc + w_ref[kw - 1 - s:kw - s, :] * pltpu.roll(dpre, n - s, 0)
    return acc


def _ssd_conv_fwd(xbc, w, b):
    rows, width = xbc.shape
    tm, tc = _row_tile(rows), 512

    def body(x_ref, xp_ref, w_ref, b_ref, o_ref):
        i = pl.program_id(1)
        xp = jnp.where(i == 0, 0.0, xp_ref[...].astype(F32))
        cat = jnp.concatenate([xp, x_ref[...].astype(F32)], axis=0)
        pre = _conv_taps(cat, w_ref, b_ref, SSD_CONV)[HALO:]
        row = i * tm + lax.broadcasted_iota(jnp.int32, (tm, 1), 0)
        o_ref[...] = jnp.where(row >= FRONT, _silu(pre), 0.0).astype(ACT_DTYPE)

    main = pl.BlockSpec((tm, tc), lambda j, i: (i, j))
    par = lambda r: pl.BlockSpec((r, tc), lambda j, i: (0, j))
    return pl.pallas_call(
        body, name="ssd_conv_fwd", grid=(width // tc, rows // tm),
        in_specs=[main, _prev_halo_spec(tm, tc, lambda j: j), par(SSD_CONV), par(1)],
        out_specs=main, out_shape=jax.ShapeDtypeStruct((rows, width), ACT_DTYPE),
        compiler_params=_params(("parallel", "parallel")),
    )(xbc, xbc, w, b)


def _ssd_conv_bwd(xbc, dxc, w, b):
    rows, width = xbc.shape
    tm, tc = _row_tile(rows), 512
    kw = SSD_CONV

    def body(x_ref, xp_ref, xn_ref, d_ref, dn_ref, w_ref, b_ref, dx_ref, dw_ref, db_ref):
        i = pl.program_id(1)
        xp = jnp.where(i == 0, 0.0, xp_ref[...].astype(F32))
        cat = jnp.concatenate([xp, x_ref[...].astype(F32), xn_ref[...].astype(F32)], axis=0)
        pre = _conv_taps(cat, w_ref, b_ref, kw)[HALO:]
        row = i * tm + lax.broadcasted_iota(jnp.int32, (tm + HALO, 1), 0)
        live = (row >= FRONT) & (row < rows)
        dout = jnp.concatenate([d_ref[...].astype(F32), dn_ref[...].astype(F32)], axis=0)
        dpre = jnp.where(live, dout * _dsilu(pre), 0.0)
        dx_ref[...] = _conv_back(dpre, w_ref, kw)[:tm].astype(ACT_DTYPE)

        @pl.when(i == 0)
        def _():
            dw_ref[...] = jnp.zeros_like(dw_ref)
            db_ref[...] = jnp.zeros_like(db_ref)

        dmain = dpre[:tm]
        db_ref[...] += jnp.sum(dmain, axis=0, keepdims=True)
        for k in range(kw):
            s = kw - 1 - k
            xs = (pltpu.roll(cat, s, 0) if s else cat)[HALO:HALO + tm]
            dw_ref[k:k + 1, :] += jnp.sum(dmain * xs, axis=0, keepdims=True)

    main = pl.BlockSpec((tm, tc), lambda j, i: (i, j))
    par = lambda r: pl.BlockSpec((r, tc), lambda j, i: (0, j))
    col = lambda j: j
    return pl.pallas_call(
        body, name="ssd_conv_bwd", grid=(width // tc, rows // tm),
        in_specs=[main, _prev_halo_spec(tm, tc, col), _next_halo_spec(tm, rows, tc, col),
                  main, _next_halo_spec(tm, rows, tc, col), par(kw), par(1)],
        out_specs=[main, par(kw), par(1)],
        out_shape=[jax.ShapeDtypeStruct((rows, width), ACT_DTYPE), jax.ShapeDtypeStruct((kw, width), F32),
                   jax.ShapeDtypeStruct((1, width), F32)],
        compiler_params=_params(("parallel", "arbitrary")),
    )(xbc, xbc, xbc, dxc, dxc, w, b)


def _ffn_conv_fwd(up, w, b):
    rows, width = up.shape
    dff = width // 2
    tm, tc = _row_tile(rows), _pick(dff, (256, 128))
    nb = dff // tc
    kw = FFN_CONV

    def body(g_ref, gp_ref, v_ref, vp_ref, wg_ref, bg_ref, wv_ref, bv_ref, o_ref):
        i = pl.program_id(1)

        def pre(x_ref, xp_ref, w_ref, b_ref):
            xp = jnp.where(i == 0, 0.0, xp_ref[...].astype(F32))
            cat = jnp.concatenate([xp, x_ref[...].astype(F32)], axis=0)
            return _conv_taps(cat, w_ref, b_ref, kw)[HALO:]

        o_ref[...] = (_silu(pre(g_ref, gp_ref, wg_ref, bg_ref)) * pre(v_ref, vp_ref, wv_ref, bv_ref)).astype(ACT_DTYPE)

    gcol, vcol = (lambda j: j), (lambda j: j + nb)
    main = lambda col: pl.BlockSpec((tm, tc), lambda j, i: (i, col(j)))
    par = lambda r, col: pl.BlockSpec((r, tc), lambda j, i: (0, col(j)))
    return pl.pallas_call(
        body, name="ffn_conv_fwd", grid=(nb, rows // tm),
        in_specs=[main(gcol), _prev_halo_spec(tm, tc, gcol), main(vcol), _prev_halo_spec(tm, tc, vcol),
                  par(kw, gcol), par(1, gcol), par(kw, vcol), par(1, vcol)],
        out_specs=pl.BlockSpec((tm, tc), lambda j, i: (i, j)),
        out_shape=jax.ShapeDtypeStruct((rows, dff), ACT_DTYPE),
        compiler_params=_params(("parallel", "parallel")),
    )(up, up, up, up, w, b, w, b)


def _ffn_conv_bwd(up, dact, w, b):
    rows, width = up.shape
    dff = width // 2
    tm, tc = _row_tile(rows), _pick(dff, (256, 128))
    nb = dff // tc
    kw = FFN_CONV

    def body(g_ref, gp_ref, gn_ref, v_ref, vp_ref, vn_ref, d_ref, dn_ref, wg_ref, bg_ref, wv_ref, bv_ref, ws_ref,
             dx_ref, dw_ref, db_ref):
        half = pl.program_id(0)
        i = pl.program_id(2)

        def ext(x_ref, xp_ref, xn_ref):
            xp = jnp.where(i == 0, 0.0, xp_ref[...].astype(F32))
            return jnp.concatenate([xp, x_ref[...].astype(F32), xn_ref[...].astype(F32)], axis=0)

        cat_g, cat_v = ext(g_ref, gp_ref, gn_ref), ext(v_ref, vp_ref, vn_ref)
        ag = _conv_taps(cat_g, wg_ref, bg_ref, kw)[HALO:]
        av = _conv_taps(cat_v, wv_ref, bv_ref, kw)[HALO:]
        row = i * tm + lax.broadcasted_iota(jnp.int32, (tm + HALO, 1), 0)
        dout = jnp.concatenate([d_ref[...].astype(F32), dn_ref[...].astype(F32)], axis=0)
        dout = jnp.where(row < rows, dout, 0.0)
        dpre = jnp.where(half == 0, dout * av * _dsilu(ag), dout * _silu(ag))
        cat = jnp.where(half == 0, cat_g, cat_v)
        dx_ref[...] = _conv_back(dpre, ws_ref, kw)[:tm].astype(ACT_DTYPE)

        @pl.when(i == 0)
        def _():
            dw_ref[...] = jnp.zeros_like(dw_ref)
            db_ref[...] = jnp.zeros_like(db_ref)

        dmain = dpre[:tm]
        db_ref[...] += jnp.sum(dmain, axis=0, keepdims=True)
        for k in range(kw):
            s = kw - 1 - k
            xs = (pltpu.roll(cat, s, 0) if s else cat)[HALO:HALO + tm]
            dw_ref[k:k + 1, :] += jnp.sum(dmain * xs, axis=0, keepdims=True)

    gcol, vcol, scol = (lambda h, j: j), (lambda h, j: j + nb), (lambda h, j: h * nb + j)
    main = lambda col: pl.BlockSpec((tm, tc), lambda h, j, i: (i, col(h, j)))
    prev = lambda col: pl.BlockSpec((HALO, tc), lambda h, j, i: (jnp.maximum(i * (tm // HALO) - 1, 0), col(h, j)))
    last = rows // HALO - 1
    nxt = lambda col: pl.BlockSpec((HALO, tc), lambda h, j, i: (jnp.minimum((i + 1) * (tm // HALO), last), col(h, j)))
    par = lambda r, col: pl.BlockSpec((r, tc), lambda h, j, i: (0, col(h, j)))
    return pl.pallas_call(
        body, name="ffn_conv_bwd", grid=(2, nb, rows // tm),
        in_specs=[main(gcol), prev(gcol), nxt(gcol), main(vcol), prev(vcol), nxt(vcol), main(gcol), nxt(gcol),
                  par(kw, gcol), par(1, gcol), par(kw, vcol), par(1, vcol), par(kw, scol)],
        out_specs=[main(scol), par(kw, scol), par(1, scol)],
        out_shape=[jax.ShapeDtypeStruct((rows, width), ACT_DTYPE), jax.ShapeDtypeStruct((kw, width), F32),
                   jax.ShapeDtypeStruct((1, width), F32)],
        compiler_params=_params(("parallel", "parallel", "arbitrary")),
    )(up, up, up, up, up, up, dact, dact, w, b, w, b, w)


def _ssd_chunk(xs, bm, cm, dtr, z, prev, dt_bias, a_log, d_skip, nw, live):
    q = CHUNK
    dt = jnp.where(live, jax.nn.softplus(dtr + dt_bias), 0.0)
    a = dt * (-jnp.exp(a_log))
    li = lax.broadcasted_iota(jnp.int32, (q, q), 0)
    si = lax.broadcasted_iota(jnp.int32, (q, q), 1)
    causal = li >= si
    tri = jnp.where(causal, 1.0, 0.0).astype(F32)
    a_cs = jnp.dot(tri, a, precision=lax.Precision.HIGHEST, preferred_element_type=F32)
    a_cs_t = a_cs.T
    a_end = a_cs[q - 1:q, :]
    cb = _dot(cm, bm, NT)
    ys, news = [], []
    for hh in range(SSD_HPG):
        acol = a_cs[:, hh:hh + 1]
        arow = a_cs_t[hh:hh + 1, :]
        aend = a_end[:, hh:hh + 1]
        decay = jnp.exp(jnp.where(causal, acol - arow, -jnp.inf))
        xh = xs[:, hh * SSD_HEAD_DIM:(hh + 1) * SSD_HEAD_DIM]
        xdt = xh * dt[:, hh:hh + 1]
        y = _dot(cb * decay, xdt)
        ph = prev[hh * SSD_HEAD_DIM:(hh + 1) * SSD_HEAD_DIM, :]
        y = y + _dot(cm, ph, NT) * jnp.exp(acol)
        y = y + d_skip[:, hh:hh + 1] * xh
        ys.append(y)
        cs = _dot(xdt * jnp.exp(aend - acol), bm, TN)
        news.append(ph * jnp.exp(aend) + cs)
    yz = jnp.concatenate(ys, axis=1) * _silu(z)
    out = yz * lax.rsqrt(jnp.mean(yz * yz, axis=-1, keepdims=True) + EPS) * nw
    return out, jnp.concatenate(news, axis=0)


def _ssd_specs(rev, nc):
    cidx = (lambda c: nc - 1 - c) if rev else (lambda c: c)
    nb_b = SSD_D_INNER // SSD_STATE
    row = lambda width, col: pl.BlockSpec((CHUNK, width), lambda g, c: (cidx(c), col(g)))
    par = lambda width: pl.BlockSpec((1, width), lambda g, c: (0, g))
    state = lambda: pl.BlockSpec((1, 1, SSD_GW, SSD_STATE), lambda g, c: (cidx(c), g, 0, 0))
    ins = [row(SSD_GW, lambda g: g), row(SSD_STATE, lambda g: nb_b + g), row(SSD_STATE, lambda g: nb_b + SSD_GROUPS + g),
           row(LANES, lambda g: g), row(SSD_GW, lambda g: g), par(LANES), par(LANES), par(LANES), par(SSD_GW)]
    return cidx, row, par, state, ins


def _ssd_fwd(xbc_c, dtr, z, dt_bias, a_log, d_skip, nw):
    rows = z.shape[0]
    nc = rows // CHUNK
    _, row, _, state, ins = _ssd_specs(False, nc)

    def body(xs_ref, b_ref, c_ref, dt_ref, z_ref, bias_ref, al_ref, dk_ref, nw_ref, y_ref, st_ref, carry):
        c = pl.program_id(1)

        @pl.when(c == 0)
        def _():
            carry[...] = jnp.zeros_like(carry)

        live = c * CHUNK + lax.broadcasted_iota(jnp.int32, (CHUNK, 1), 0) >= FRONT
        prev = carry[...]
        st_ref[0, 0] = prev
        out, new = _ssd_chunk(xs_ref[...].astype(F32), b_ref[...].astype(F32), c_ref[...].astype(F32), dt_ref[...],
                              z_ref[...].astype(F32), prev, bias_ref[...], al_ref[...], dk_ref[...], nw_ref[...], live)
        y_ref[...] = out.astype(ACT_DTYPE)
        carry[...] = new

    return pl.pallas_call(
        body, name="ssd_fwd", grid=(SSD_GROUPS, nc), in_specs=ins,
        out_specs=[row(SSD_GW, lambda g: g), state()],
        out_shape=[jax.ShapeDtypeStruct((rows, SSD_D_INNER), ACT_DTYPE),
                   jax.ShapeDtypeStruct((nc, SSD_GROUPS, SSD_GW, SSD_STATE), F32)],
        scratch_shapes=[pltpu.VMEM((SSD_GW, SSD_STATE), F32)],
        compiler_params=_params(("parallel", "arbitrary")),
    )(xbc_c, xbc_c, xbc_c, dtr, z, dt_bias, a_log, d_skip, nw)


def _ssd_bwd(xbc_c, dtr, z, dt_bias, a_log, d_skip, nw, states, dy):
    rows = z.shape[0]
    nc = rows // CHUNK
    cidx, row, par, state, ins = _ssd_specs(True, nc)

    def body(xs_ref, b_ref, c_ref, dt_ref, z_ref, bias_ref, al_ref, dk_ref, nw_ref, st_ref, dy_ref,
             dxs_ref, db_ref, dc_ref, ddt_ref, dz_ref, dbias_ref, dal_ref, ddk_ref, dnw_ref, carry):
        c = pl.program_id(1)

        @pl.when(c == 0)
        def _():
            carry[...] = jnp.zeros_like(carry)
            for r in (dbias_ref, dal_ref, ddk_ref, dnw_ref):
                r[...] = jnp.zeros_like(r)

        live = cidx(c) * CHUNK + lax.broadcasted_iota(jnp.int32, (CHUNK, 1), 0) >= FRONT
        fn = functools.partial(_ssd_chunk, live=live)
        _, vjp = jax.vjp(fn, xs_ref[...].astype(F32), b_ref[...].astype(F32), c_ref[...].astype(F32), dt_ref[...],
                         z_ref[...].astype(F32), st_ref[0, 0], bias_ref[...], al_ref[...], dk_ref[...], nw_ref[...])
        dxs, dbm, dcm, ddt, dz, dprev, dbias, dal, ddk, dnw = vjp((dy_ref[...].astype(F32), carry[...]))
        dxs_ref[...] = dxs.astype(ACT_DTYPE)
        db_ref[...] = dbm.astype(ACT_DTYPE)
        dc_ref[...] = dcm.astype(ACT_DTYPE)
        ddt_ref[...] = ddt
        dz_ref[...] = dz.astype(ACT_DTYPE)
        carry[...] = dprev
        dbias_ref[...] += dbias
        dal_ref[...] += dal
        ddk_ref[...] += ddk
        dnw_ref[...] += dnw

    gcol = lambda g: g
    return pl.pallas_call(
        body, name="ssd_bwd", grid=(SSD_GROUPS, nc),
        in_specs=ins + [state(), row(SSD_GW, gcol)],
        out_specs=[row(SSD_GW, gcol), row(SSD_STATE, gcol), row(SSD_STATE, gcol), row(LANES, gcol), row(SSD_GW, gcol),
                   par(LANES), par(LANES), par(LANES), par(SSD_GW)],
        out_shape=[jax.ShapeDtypeStruct((rows, SSD_D_INNER), ACT_DTYPE),
                   jax.ShapeDtypeStruct((rows, SSD_GROUPS * SSD_STATE), ACT_DTYPE),
                   jax.ShapeDtypeStruct((rows, SSD_GROUPS * SSD_STATE), ACT_DTYPE),
                   jax.ShapeDtypeStruct((rows, SSD_GROUPS * LANES), F32),
                   jax.ShapeDtypeStruct((rows, SSD_D_INNER), ACT_DTYPE),
                   jax.ShapeDtypeStruct((1, SSD_GROUPS * LANES), F32), jax.ShapeDtypeStruct((1, SSD_GROUPS * LANES), F32),
                   jax.ShapeDtypeStruct((1, SSD_GROUPS * LANES), F32), jax.ShapeDtypeStruct((1, SSD_D_INNER), F32)],
        scratch_shapes=[pltpu.VMEM((SSD_GW, SSD_STATE), F32)],
        compiler_params=_params(("parallel", "arbitrary")),
    )(xbc_c, xbc_c, xbc_c, dtr, z, dt_bias, a_log, d_skip, nw, states, dy)


def _rotary(t, cos, sin):
    half = t.shape[-1] // 2
    t1, t2 = t[:, :half], t[:, half:]
    return jnp.concatenate([t1 * cos - t2 * sin, t2 * cos + t1 * sin], axis=1)


def _ret_chunk(qh, kh, vh, gh, prev, cos, sin, lg):
    q = CHUNK
    qr = _rotary(qh, cos, sin)
    kr = _rotary(kh, cos, sin) * (RET_QK ** -0.5)
    li = lax.broadcasted_iota(jnp.int32, (q, q), 0)
    si = lax.broadcasted_iota(jnp.int32, (q, q), 1)
    dist = (li - si).astype(F32)
    decay = jnp.exp(jnp.where(li >= si, dist * lg, -jnp.inf))
    idx = lax.broadcasted_iota(jnp.int32, (q, 1), 0).astype(F32)
    scores = _dot(qr, kr, NT) * decay
    out = _dot(scores, vh)
    kv = _dot(kr * jnp.exp((q - 1.0 - idx) * lg), vh, TN)
    out = out + _dot(qr, prev) * jnp.exp((idx + 1.0) * lg)
    new = prev * jnp.exp(q * lg) + kv
    out = out * lax.rsqrt(jnp.mean(out * out, axis=-1, keepdims=True) + EPS)
    return _silu(gh) * out, new


def _ret_specs(rev, nc):
    cidx = (lambda c: nc - 1 - c) if rev else (lambda c: c)
    row = lambda width: pl.BlockSpec((CHUNK, width), lambda h, c: (cidx(c), h))
    tab = lambda: pl.BlockSpec((CHUNK, RET_QK // 2), lambda h, c: (cidx(c), 0))
    lgs = lambda: pl.BlockSpec((1, 8, LANES), lambda h, c: (h, 0, 0))
    state = lambda: pl.BlockSpec((1, 1, RET_QK, RET_V), lambda h, c: (cidx(c), h, 0, 0))
    ins = [row(RET_QK), row(RET_QK), row(RET_V), row(RET_V), tab(), tab(), lgs()]
    return row, state, ins


def _ret_fwd(q, k, v, g, cos, sin, lgam):
    rows = q.shape[0]
    nc = rows // CHUNK
    row, state, ins = _ret_specs(False, nc)

    def body(q_ref, k_ref, v_ref, g_ref, cos_ref, sin_ref, lg_ref, y_ref, st_ref, carry):
        c = pl.program_id(1)

        @pl.when(c == 0)
        def _():
            carry[...] = jnp.zeros_like(carry)

        prev = carry[...]
        st_ref[0, 0] = prev.astype(ACT_DTYPE)
        out, new = _ret_chunk(q_ref[...].astype(F32), k_ref[...].astype(F32), v_ref[...].astype(F32),
                              g_ref[...].astype(F32), prev, cos_ref[...], sin_ref[...], lg_ref[0, 0:1, 0:1])
        y_ref[...] = out.astype(ACT_DTYPE)
        carry[...] = new

    return pl.pallas_call(
        body, name="ret_fwd", grid=(RET_HEADS, nc), in_specs=ins, out_specs=[row(RET_V), state()],
        out_shape=[jax.ShapeDtypeStruct((rows, RET_HEADS * RET_V), ACT_DTYPE),
                   jax.ShapeDtypeStruct((nc, RET_HEADS, RET_QK, RET_V), ACT_DTYPE)],
        scratch_shapes=[pltpu.VMEM((RET_QK, RET_V), F32)],
        compiler_params=_params(("parallel", "arbitrary")),
    )(q, k, v, g, cos, sin, lgam)


def _ret_bwd(q, k, v, g, cos, sin, lgam, states, dy):
    rows = q.shape[0]
    nc = rows // CHUNK
    row, state, ins = _ret_specs(True, nc)

    def body(q_ref, k_ref, v_ref, g_ref, cos_ref, sin_ref, lg_ref, st_ref, dy_ref, dq_ref, dk_ref, dv_ref, dg_ref, carry):
        c = pl.program_id(1)

        @pl.when(c == 0)
        def _():
            carry[...] = jnp.zeros_like(carry)

        fn = functools.partial(_ret_chunk, cos=cos_ref[...], sin=sin_ref[...], lg=lg_ref[0, 0:1, 0:1])
        _, vjp = jax.vjp(fn, q_ref[...].astype(F32), k_ref[...].astype(F32), v_ref[...].astype(F32),
                         g_ref[...].astype(F32), st_ref[0, 0].astype(F32))
        dq, dk, dv, dg, dprev = vjp((dy_ref[...].astype(F32), carry[...]))
        dq_ref[...] = dq.astype(ACT_DTYPE)
        dk_ref[...] = dk.astype(ACT_DTYPE)
        dv_ref[...] = dv.astype(ACT_DTYPE)
        dg_ref[...] = dg.astype(ACT_DTYPE)
        carry[...] = dprev

    shp = lambda width: jax.ShapeDtypeStruct((rows, RET_HEADS * width), ACT_DTYPE)
    return pl.pallas_call(
        body, name="ret_bwd", grid=(RET_HEADS, nc), in_specs=ins + [state(), row(RET_V)],
        out_specs=[row(RET_QK), row(RET_QK), row(RET_V), row(RET_V)],
        out_shape=[shp(RET_QK), shp(RET_QK), shp(RET_V), shp(RET_V)],
        scratch_shapes=[pltpu.VMEM((RET_QK, RET_V), F32)],
        compiler_params=_params(("parallel", "arbitrary")),
    )(q, k, v, g, cos, sin, lgam, states, dy)


def _merge_fwd(bs, br, gs, gr):
    rows, d = bs.shape
    tm = _row_tile(rows)

    def body(bs_ref, br_ref, gs_ref, gr_ref, o_ref):
        o_ref[...] = (jax.nn.sigmoid(gs_ref[...].astype(F32)) * bs_ref[...].astype(F32)
                      + jax.nn.sigmoid(gr_ref[...].astype(F32)) * br_ref[...].astype(F32)).astype(ACT_DTYPE)

    spec = pl.BlockSpec((tm, d), lambda i: (i, 0))
    return pl.pallas_call(
        body, name="merge_fwd", grid=(rows // tm,), in_specs=[spec] * 4, out_specs=spec,
        out_shape=jax.ShapeDtypeStruct((rows, d), ACT_DTYPE), compiler_params=_params(("parallel",)),
    )(bs, br, gs, gr)


def _merge_bwd(dm, bs, br, gs, gr):
    rows, d = bs.shape
    tm = _row_tile(rows)

    def body(dm_ref, bs_ref, br_ref, gs_ref, gr_ref, dbs_ref, dbr_ref, dgs_ref, dgr_ref):
        dmv = dm_ref[...].astype(F32)
        for b_ref, g_ref, db_ref, dg_ref in ((bs_ref, gs_ref, dbs_ref, dgs_ref), (br_ref, gr_ref, dbr_ref, dgr_ref)):
            s = jax.nn.sigmoid(g_ref[...].astype(F32))
            db_ref[...] = (dmv * s).astype(ACT_DTYPE)
            dg_ref[...] = (dmv * b_ref[...].astype(F32) * s * (1.0 - s)).astype(ACT_DTYPE)

    spec = pl.BlockSpec((tm, d), lambda i: (i, 0))
    shp = jax.ShapeDtypeStruct((rows, d), ACT_DTYPE)
    return pl.pallas_call(
        body, name="merge_bwd", grid=(rows // tm,), in_specs=[spec] * 5, out_specs=[spec] * 4,
        out_shape=[shp] * 4, compiler_params=_params(("parallel",)),
    )(dm, bs, br, gs, gr)


def _place():
    x, y, c = lax.axis_index("x"), lax.axis_index("y"), lax.axis_index("c")
    return x, y, c


def _slot(p):
    return 4 * p[0] + 2 * p[1] + p[2]


def _allgather(arrs, name):
    n = len(arrs)
    any_spec = pl.BlockSpec(memory_space=pl.ANY)

    def body(*refs):
        ins, outs = refs[:n], refs[n:2 * n]
        send_sems, recv_sems, local_sems = refs[2 * n:]
        x, y, c = _place()
        me, sibling = (x, y, c), (x, y, 1 - c)
        chips = [(1 - x, y), (x, 1 - y), (1 - x, 1 - y)]

        def copy(a, k, block, to, src=None):
            dst = outs[a].at[_slot(block)]
            return pltpu.make_async_remote_copy(
                src_ref=dst if src is None else src, dst_ref=dst, send_sem=send_sems.at[a * 7 + k],
                recv_sem=recv_sems.at[a * 7 + k], device_id=to, device_id_type=MESH)

        mine, first, passed = [], [], []
        for a in range(n):
            cp = pltpu.make_async_copy(ins[a], outs[a].at[_slot(me)], local_sems.at[a])
            cp.start()
            mine.append(cp)
            first.append(copy(a, 0, me, sibling, src=ins[a]))
            first += [copy(a, 1 + j, me, (*chip, c), src=ins[a]) for j, chip in enumerate(chips)]
        for cp in first:
            cp.start()
        for j, chip in enumerate(chips):
            for a in range(n):
                copy(a, 1 + j, (*chip, c), me).wait_recv()
                cp = copy(a, 4 + j, (*chip, c), sibling)
                cp.start()
                passed.append(cp)
        for a in range(n):
            copy(a, 0, sibling, me).wait_recv()
            for j, chip in enumerate(chips):
                copy(a, 4 + j, (*chip, 1 - c), me).wait_recv()
        for cp in first + passed:
            cp.wait_send()
        for cp in mine:
            cp.wait()

    return pl.pallas_call(
        body, name=name, in_specs=[any_spec] * n, out_specs=[any_spec] * n,
        out_shape=[jax.ShapeDtypeStruct((N_DEV,) + a.shape, a.dtype) for a in arrs],
        scratch_shapes=[pltpu.SemaphoreType.DMA((7 * n,)), pltpu.SemaphoreType.DMA((7 * n,)), pltpu.SemaphoreType.DMA((n,))],
    )(*arrs)


def _scatter(arrs, name):
    n = len(arrs)
    any_spec = pl.BlockSpec(memory_space=pl.ANY)

    def body(*refs):
        ins, outs = refs[:n], refs[n:2 * n]
        send_sems, recv_sems, local_sems = refs[2 * n:]
        x, y, c = _place()
        me = (x, y, c)
        peers = [(x ^ dx, y ^ dy, c ^ dc) for dx in (0, 1) for dy in (0, 1) for dc in (0, 1)][1:]
        mine, sent = [], []
        for a in range(n):
            cp = pltpu.make_async_copy(ins[a].at[_slot(me)], outs[a].at[_slot(me)], local_sems.at[a])
            cp.start()
            mine.append(cp)
            for k, peer in enumerate(peers):
                cp = pltpu.make_async_remote_copy(
                    src_ref=ins[a].at[_slot(peer)], dst_ref=outs[a].at[_slot(me)], send_sem=send_sems.at[a * 7 + k],
                    recv_sem=recv_sems.at[a * 7 + k], device_id=peer, device_id_type=MESH)
                cp.start()
                sent.append(cp)
        for a in range(n):
            for k, peer in enumerate(peers):
                pltpu.make_async_remote_copy(
                    src_ref=ins[a].at[_slot(me)], dst_ref=outs[a].at[_slot(peer)], send_sem=send_sems.at[a * 7 + k],
                    recv_sem=recv_sems.at[a * 7 + k], device_id=peer, device_id_type=MESH).wait_recv()
        for cp in sent:
            cp.wait_send()
        for cp in mine:
            cp.wait()

    return pl.pallas_call(
        body, name=name, in_specs=[any_spec] * n, out_specs=[any_spec] * n,
        out_shape=[jax.ShapeDtypeStruct(a.shape, a.dtype) for a in arrs],
        scratch_shapes=[pltpu.SemaphoreType.DMA((7 * n,)), pltpu.SemaphoreType.DMA((7 * n,)), pltpu.SemaphoreType.DMA((n,))],
    )(*arrs)


def _allreduce_small(pack):
    rows, lanes = pack.shape

    def body(x_ref, o_ref, buf, send_sems, recv_sems):
        x, y, c = _place()
        me, sibling = (x, y, c), (x, y, 1 - c)
        chips = [(1 - x, y), (x, 1 - y), (1 - x, 1 - y)]

        def copy(k, block, to, src=None):
            dst = buf.at[_slot(block)]
            return pltpu.make_async_remote_copy(
                src_ref=dst if src is None else src, dst_ref=dst, send_sem=send_sems.at[k], recv_sem=recv_sems.at[k],
                device_id=to, device_id_type=MESH)

        buf[_slot(me)] = x_ref[...]
        first = [copy(0, me, sibling, src=x_ref)]
        first += [copy(1 + j, me, (*chip, c), src=x_ref) for j, chip in enumerate(chips)]
        for cp in first:
            cp.start()
        passed = [copy(4 + j, (*chip, c), sibling) for j, chip in enumerate(chips)]
        for j, chip in enumerate(chips):
            copy(1 + j, (*chip, c), me).wait_recv()
            passed[j].start()
        copy(0, sibling, me).wait_recv()
        for j, chip in enumerate(chips):
            copy(4 + j, (*chip, 1 - c), me).wait_recv()
        for cp in first + passed:
            cp.wait_send()
        acc = buf[0]
        for i in range(1, N_DEV):
            acc = acc + buf[i]
        o_ref[...] = acc

    vmem = pl.BlockSpec(memory_space=pltpu.VMEM)
    return pl.pallas_call(
        body, name="allreduce_small", in_specs=[vmem], out_specs=vmem,
        out_shape=jax.ShapeDtypeStruct((rows, lanes), F32),
        scratch_shapes=[pltpu.VMEM((N_DEV, rows, lanes), F32), pltpu.SemaphoreType.DMA((7,)), pltpu.SemaphoreType.DMA((7,))],
        compiler_params=pltpu.CompilerParams(vmem_limit_bytes=VMEM_LIMIT),
    )(pack)


def _adamw(w, g, m, v):
    m = ADAM_B1 * m + (1.0 - ADAM_B1) * g
    v = ADAM_B2 * v + (1.0 - ADAM_B2) * jnp.square(g)
    m_hat = m / (1.0 - ADAM_B1 ** ADAM_STEP)
    v_hat = v / (1.0 - ADAM_B2 ** ADAM_STEP)
    delta = -ADAM_LR * (m_hat / (jnp.sqrt(v_hat) + ADAM_EPS) + ADAM_WD * w)
    return delta, m, v


def _adam_shard(parts, w, m, v, name):
    r, c = w.shape
    tr = _pick(r, (128, 64, 32, 16, 8))

    def body(p_ref, w_ref, m_ref, v_ref, g_ref, d_ref, nm_ref, nv_ref):
        g = p_ref[0].astype(F32)
        for i in range(1, N_DEV):
            g = g + p_ref[i].astype(F32)
        g_ref[...] = g
        d_ref[...], nm_ref[...], nv_ref[...] = _adamw(w_ref[...], g, m_ref[...], v_ref[...])

    spec = pl.BlockSpec((tr, c), lambda i: (i, 0))
    shp = jax.ShapeDtypeStruct((r, c), F32)
    return pl.pallas_call(
        body, name=name, grid=(r // tr,),
        in_specs=[pl.BlockSpec((N_DEV, tr, c), lambda i: (0, i, 0)), spec, spec, spec], out_specs=[spec] * 4,
        out_shape=[shp] * 4, compiler_params=_params(("parallel",)),
    )(parts, w, m, v)


def _adam_small(w, g, m, v):
    r, c = w.shape

    def body(w_ref, g_ref, m_ref, v_ref, d_ref, nm_ref, nv_ref):
        d_ref[...], nm_ref[...], nv_ref[...] = _adamw(w_ref[...], g_ref[...], m_ref[...], v_ref[...])

    shp = jax.ShapeDtypeStruct((r, c), F32)
    return pl.pallas_call(body, name="adam_small", out_shape=[shp] * 3)(w, g, m, v)


def _pack(arrs):
    rows = []
    for a in arrs:
        flat = a.reshape(-1).astype(F32)
        rows.append(jnp.pad(flat, (0, (-flat.shape[0]) % (8 * LANES))).reshape(-1, LANES))
    return jnp.concatenate(rows, axis=0)


def _unpack(pack, shapes):
    out, r = [], 0
    for s in shapes:
        size = math.prod(s)
        nr = -(-size // (8 * LANES)) * 8
        out.append(pack[r:r + nr].reshape(-1)[:size].reshape(s))
        r += nr
    return out


def _group_lanes(t):
    lead = t.shape[:-1]
    t = t.reshape(lead + (SSD_GROUPS, SSD_HPG))
    t = jnp.pad(t, [(0, 0)] * len(lead) + [(0, 0), (0, LANES - SSD_HPG)])
    return t.reshape(lead + (SSD_GROUPS * LANES,))


def _ungroup_lanes(t):
    lead = t.shape[:-1]
    return t.reshape(lead + (SSD_GROUPS, LANES))[..., :SSD_HPG].reshape(lead + (SSD_HEADS,))


def kernel(x, meta_tokens, mix_norm_w, w_in, ssd_conv_w, ssd_conv_b, ssd_dt_bias, ssd_A_log, ssd_D, ssd_norm_w, w_branch_ssd, w_branch_ret, w_out, ffn_norm_w, w_up, ffn_conv_w, ffn_conv_b, w_down, final_norm_w, loss_target, m_meta_tokens, m_mix_norm_w, m_w_in, m_ssd_conv_w, m_ssd_conv_b, m_ssd_dt_bias, m_ssd_A_log, m_ssd_D, m_ssd_norm_w, m_w_branch_ssd, m_w_branch_ret, m_w_out, m_ffn_norm_w, m_w_up, m_ffn_conv_w, m_ffn_conv_b, m_w_down, m_final_norm_w, v_meta_tokens, v_mix_norm_w, v_w_in, v_ssd_conv_w, v_ssd_conv_b, v_ssd_dt_bias, v_ssd_A_log, v_ssd_D, v_ssd_norm_w, v_w_branch_ssd, v_w_branch_ret, v_w_out, v_ffn_norm_w, v_w_up, v_ffn_conv_w, v_ffn_conv_b, v_w_down, v_final_norm_w):
    seq, d = x.shape[1], x.shape[2]
    rows = seq + PAD_ROWS
    tm = _row_tile(rows)
    me = _slot(_place())
    d_ff = w_down.shape[1] * N_DEV

    big = [w_in[0], w_branch_ssd[0], w_branch_ret[0], w_out[0], w_up[0], w_down[0]]
    gathered = _allgather([b.astype(COMM_DTYPE) for b in big] + [meta_tokens, ssd_conv_w[0], ffn_conv_w[0]], "gather_weights")
    cols = lambda t: jnp.transpose(t, (1, 0, 2)).reshape(t.shape[1], -1)
    rws = lambda t: t.reshape(-1, t.shape[2])
    w_in_f, w_bs, w_br, w_o, w_up_f, w_dn = (cols(gathered[0]), rws(gathered[1]), rws(gathered[2]), rws(gathered[3]),
                                             cols(gathered[4]), rws(gathered[5]))
    meta_full, conv_w, fconv_w = cols(gathered[6]), cols(gathered[7]), cols(gathered[8])
    widths = [SSD_D_INNER, SSD_CONV_DIM, SSD_HEADS, RET_HEADS * RET_QK, RET_HEADS * RET_QK, RET_HEADS * RET_V,
              RET_HEADS * RET_V, d, d]
    offs = [0]
    for wd in widths:
        offs.append(offs[-1] + wd)
    seg = [w_in_f[:, offs[i]:offs[i + 1]] for i in range(9)]
    seg[2] = _group_lanes(seg[2])
    w_in_p = jnp.concatenate(seg, axis=1)
    widths_p = [s.shape[1] for s in seg]
    offs_p = [0]
    for wd in widths_p:
        offs_p.append(offs_p[-1] + wd)
    in_p = offs_p[-1]

    h0 = jnp.concatenate([jnp.zeros((FRONT, d), F32), meta_full, x[0]], axis=0)
    u1 = _rms_fwd(h0, mix_norm_w, "rms1")
    seg_dtype = [ACT_DTYPE, ACT_DTYPE, F32] + [ACT_DTYPE] * 6
    proj = [_mm(u1, seg[i], mode="nn", out_dtype=seg_dtype[i], tm=tm, tn=_pick(widths_p[i], (1024, 512)), tk=d,
                name=f"in_proj_{i}") for i in range(9)]
    z, xbc, dtr, q, k, v, g, gs, gr = proj
    xbc_c = _ssd_conv_fwd(xbc, conv_w, ssd_conv_b)
    bias_p, alog_p, dsk_p = _group_lanes(ssd_dt_bias), _group_lanes(ssd_A_log), _group_lanes(ssd_D)
    y_ssd, ssd_states = _ssd_fwd(xbc_c, dtr, z, bias_p, alog_p, dsk_p, ssd_norm_w)
    pos = jnp.arange(rows, dtype=F32) - float(FRONT)
    inv_freq = ROPE_BASE ** (-jnp.linspace(0.0, 1.0, RET_QK // 2, dtype=F32))
    ang = pos[:, None] * inv_freq[None, :]
    cos, sin = jnp.cos(ang), jnp.sin(ang)
    lgam = jnp.log(1.0 - 2.0 ** (-5.0 - jnp.arange(RET_HEADS, dtype=F32)))
    lgam = jnp.broadcast_to(lgam[:, None, None], (RET_HEADS, 8, LANES))
    y_ret, ret_states = _ret_fwd(q, k, v, g, cos, sin, lgam)
    bs = _mm(y_ssd, w_bs, mode="nn", out_dtype=ACT_DTYPE, tm=tm, tn=d, tk=SSD_D_INNER, name="branch_ssd")
    br = _mm(y_ret, w_br, mode="nn", out_dtype=ACT_DTYPE, tm=tm, tn=d, tk=RET_HEADS * RET_V, name="branch_ret")
    merged = _merge_fwd(bs, br, gs, gr)
    h1 = _mm(merged, w_o, mode="nn", out_dtype=F32, tm=tm, tn=d, tk=d, name="out_proj", add=h0)
    u2 = _rms_fwd(h1, ffn_norm_w, "rms2")
    up = _mm(u2, w_up_f, mode="nn", out_dtype=ACT_DTYPE, tm=tm, tn=_pick(2 * d_ff, (512,)), tk=d, name="ffn_up")
    act = _ffn_conv_fwd(up, fconv_w, ffn_conv_b)
    h2 = _mm(act, w_dn, mode="nn", out_dtype=F32, tm=tm, tn=d, tk=d_ff, name="ffn_down", add=h1)
    tgt = jnp.pad(loss_target[0], ((PAD_ROWS, 0), (0, 0)))
    dh2, loss_acc, g_final = _loss_head(h2, tgt, final_norm_w.reshape(1, d))

    tff = _pick(d_ff, (1408, 256))
    d_act = _mm(dh2, w_dn, mode="nt", out_dtype=ACT_DTYPE, tm=tm, tn=tff, tk=d, name="d_act")
    g_w_dn = _mm(act, dh2, mode="tn", out_dtype=F32, tm=tff, tn=d, tk=tm, name="g_w_down")
    d_up, g_fconv_w, g_fconv_b = _ffn_conv_bwd(up, d_act, fconv_w, ffn_conv_b)
    du2 = _mm(d_up, w_up_f, mode="nt", out_dtype=F32, tm=tm, tn=d, tk=_pick(2 * d_ff, (2816, 512)), name="d_u2")
    g_w_up = _mm(u2, d_up, mode="tn", out_dtype=F32, tm=d, tn=_pick(2 * d_ff, (512,)), tk=tm, name="g_w_up")
    dh1, g_ffn_norm = _rms_bwd(du2, h1, ffn_norm_w, dh2, "rms2_bwd")
    d_merged = _mm(dh1, w_o, mode="nt", out_dtype=F32, tm=tm, tn=d, tk=d, name="d_merged")
    g_w_o = _mm(merged, dh1, mode="tn", out_dtype=F32, tm=d, tn=d, tk=tm, name="g_w_out")
    d_bs, d_br, d_gs, d_gr = _merge_bwd(d_merged, bs, br, gs, gr)
    d_yssd = _mm(d_bs, w_bs, mode="nt", out_dtype=ACT_DTYPE, tm=tm, tn=1024, tk=d, name="d_y_ssd")
    g_w_bs = _mm(y_ssd, d_bs, mode="tn", out_dtype=F32, tm=1024, tn=d, tk=tm, name="g_w_branch_ssd")
    d_yret = _mm(d_br, w_br, mode="nt", out_dtype=ACT_DTYPE, tm=tm, tn=1024, tk=d, name="d_y_ret")
    g_w_br = _mm(y_ret, d_br, mode="tn", out_dtype=F32, tm=1024, tn=d, tk=tm, name="g_w_branch_ret")
    dxs, d_bm, d_cm, d_dtr, d_z, g_bias_p, g_alog_p, g_dsk_p, g_nw = _ssd_bwd(
        xbc_c, dtr, z, bias_p, alog_p, dsk_p, ssd_norm_w, ssd_states, d_yssd)
    d_xbc, g_conv_w, g_conv_b = _ssd_conv_bwd(xbc, jnp.concatenate([dxs, d_bm, d_cm], axis=1), conv_w, ssd_conv_b)
    d_q, d_k, d_v, d_g = _ret_bwd(q, k, v, g, cos, sin, lgam, ret_states, d_yret)
    dproj = jnp.concatenate([d_z, d_xbc, d_dtr.astype(ACT_DTYPE), d_q, d_k, d_v, d_g, d_gs, d_gr], axis=1)
    du1 = _mm(dproj, w_in_p, mode="nt", out_dtype=F32, tm=tm, tn=d, tk=_pick(in_p, (1536, 512)), name="d_u1")
    g_w_in_p = _mm(u1, dproj, mode="tn", out_dtype=F32, tm=d, tn=_pick(in_p, (768, 512)), tk=tm, name="g_w_in")
    dh0, g_mix_norm = _rms_bwd(du1, h0, mix_norm_w, dh1, "rms1_bwd")
    grad_x = dh0[PAD_ROWS:][None]
    gseg = [g_w_in_p[:, offs_p[i]:offs_p[i + 1]] for i in range(9)]
    gseg[2] = _ungroup_lanes(gseg[2])
    g_w_in = jnp.concatenate(gseg, axis=1)

    cparts = lambda t: jnp.transpose(t.reshape(t.shape[0], N_DEV, -1), (1, 0, 2))
    rparts = lambda t: t.reshape(N_DEV, -1, t.shape[1])
    contrib = [cparts(g_w_in), rparts(g_w_bs), rparts(g_w_br), rparts(g_w_o), cparts(g_w_up), rparts(g_w_dn)]
    parts = _scatter([t.astype(COMM_DTYPE) for t in contrib], "scatter_grads")
    big_m = [m_w_in, m_w_branch_ssd, m_w_branch_ret, m_w_out, m_w_up, m_w_down]
    big_v = [v_w_in, v_w_branch_ssd, v_w_branch_ret, v_w_out, v_w_up, v_w_down]
    big_names = ["w_in", "w_branch_ssd", "w_branch_ret", "w_out", "w_up", "w_down"]
    big_out = {}
    for nm, pt, w, m, v_ in zip(big_names, parts, big, big_m, big_v, strict=True):
        big_out[nm] = [t[None] for t in _adam_shard(pt, w, m[0], v_[0], "adam_" + nm)]

    small_g = [dh0[FRONT:PAD_ROWS], g_mix_norm, g_conv_w, g_conv_b, _ungroup_lanes(g_bias_p), _ungroup_lanes(g_alog_p),
               _ungroup_lanes(g_dsk_p), g_nw, g_ffn_norm, g_fconv_w, g_fconv_b, g_final, loss_acc[0:1, 0:1]]
    total = _unpack(_allreduce_small(_pack(small_g)), [t.shape for t in small_g])
    loss = total[12].reshape(())
    shard = lambda t, width: lax.dynamic_slice_in_dim(t, me * width, width, axis=1)
    small_names = ["meta_tokens", "mix_norm_w", "ssd_conv_w", "ssd_conv_b", "ssd_dt_bias", "ssd_A_log", "ssd_D", "ssd_norm_w",
                   "ffn_norm_w", "ffn_conv_w", "ffn_conv_b", "final_norm_w"]
    small_w = [meta_tokens, mix_norm_w, ssd_conv_w, ssd_conv_b, ssd_dt_bias, ssd_A_log, ssd_D, ssd_norm_w, ffn_norm_w,
               ffn_conv_w, ffn_conv_b, final_norm_w]
    small_m = [m_meta_tokens, m_mix_norm_w, m_ssd_conv_w, m_ssd_conv_b, m_ssd_dt_bias, m_ssd_A_log, m_ssd_D, m_ssd_norm_w,
               m_ffn_norm_w, m_ffn_conv_w, m_ffn_conv_b, m_final_norm_w]
    small_v = [v_meta_tokens, v_mix_norm_w, v_ssd_conv_w, v_ssd_conv_b, v_ssd_dt_bias, v_ssd_A_log, v_ssd_D, v_ssd_norm_w,
               v_ffn_norm_w, v_ffn_conv_w, v_ffn_conv_b, v_final_norm_w]
    grads = total[:12]
    grads[0] = shard(grads[0], meta_tokens.shape[1])
    grads[2] = shard(grads[2], ssd_conv_w.shape[2])
    grads[9] = shard(grads[9], ffn_conv_w.shape[2])
    grads = [t.reshape(w.shape) for t, w in zip(grads, small_w, strict=True)]
    shapes = [w.shape for w in small_w]
    upd = _adam_small(_pack(small_w), _pack(grads), _pack(small_m), _pack(small_v))
    small_out = {nm: [gr_] + [u[i] for u in (_unpack(t, shapes) for t in upd)]
                 for i, (nm, gr_) in enumerate(zip(small_names, grads, strict=True))}

    order = ["meta_tokens", "mix_norm_w", "w_in", "ssd_conv_w", "ssd_conv_b", "ssd_dt_bias", "ssd_A_log", "ssd_D", "ssd_norm_w",
             "w_branch_ssd", "w_branch_ret", "w_out", "ffn_norm_w", "w_up", "ffn_conv_w", "ffn_conv_b", "w_down", "final_norm_w"]
    res = {**big_out, **small_out}
    return (loss, grad_x, *[res[nm][0] for nm in order], *[res[nm][1] for nm in order], *[res[nm][2] for nm in order],
            *[res[nm][3] for nm in order])
```

```python
import functools
import math

import jax
import jax.numpy as jnp
from jax import lax
from jax.experimental import pallas as pl
from jax.experimental.pallas import tpu as pltpu

F32 = jnp.float32
MXU_DTYPE = jnp.bfloat16
ACT_DTYPE = jnp.bfloat16
COMM_DTYPE = jnp.bfloat16

N_META = 16
CHUNK = 128
FRONT = CHUNK - N_META
PAD_ROWS = FRONT + N_META
EPS = 1e-6
N_DEV = 8

SSD_D_INNER = 2048
SSD_HEAD_DIM = 64
SSD_HEADS = 32
SSD_GROUPS = 4
SSD_HPG = SSD_HEADS // SSD_GROUPS
SSD_STATE = 128
SSD_CONV = 4
SSD_CONV_DIM = SSD_D_INNER + 2 * SSD_GROUPS * SSD_STATE
SSD_GW = SSD_D_INNER // SSD_GROUPS
RET_HEADS = 4
RET_QK = 256
RET_V = 512
ROPE_BASE = 10000.0
FFN_CONV = 3
HALO = 16
LANES = 128

ADAM_LR = 0.001
ADAM_B1 = 0.9
ADAM_B2 = 0.999
ADAM_EPS = 1e-08
ADAM_WD = 0.01
ADAM_STEP = 10

VMEM_LIMIT = 56 * 1024 * 1024
MESH = pl.DeviceIdType.MESH

NN = (((1,), (0,)), ((), ()))
NT = (((1,), (1,)), ((), ()))
TN = (((0,), (0,)), ((), ()))


def _params(sem):
    return pltpu.CompilerParams(dimension_semantics=sem, vmem_limit_bytes=VMEM_LIMIT)


def _dot(a, b, dn=NN):
    return lax.dot_general(a.astype(MXU_DTYPE), b.astype(MXU_DTYPE), dn, preferred_element_type=F32)


def _silu(x):
    return x * jax.nn.sigmoid(x)


def _dsilu(x):
    s = jax.nn.sigmoid(x)
    return s * (1.0 + x * (1.0 - s))


def _row_tile(rows):
    return 640 if rows % 640 == 0 else 128


def _mm(a, b, *, mode, out_dtype, tm, tn, tk, name, add=None, after=None):
    if mode == "nn":
        (m, k), (k2, n) = a.shape, b.shape
    elif mode == "nt":
        (m, k), (n, k2) = a.shape, b.shape
    else:
        (k, m), (k2, n) = a.shape, b.shape
    assert k == k2 and m % tm == 0 and n % tn == 0 and k % tk == 0, (name, a.shape, b.shape, tm, tn, tk)
    nk = k // tk
    dn = {"nn": NN, "nt": NT, "tn": TN}[mode]
    has_add = add is not None
    n_in = 2 + has_add + (after is not None)

    def body(*refs):
        a_ref, b_ref = refs[0], refs[1]
        add_ref = refs[2] if has_add else None
        o_ref = refs[n_in]
        p = _dot(a_ref[...], b_ref[...], dn)
        if nk == 1:
            if has_add:
                p = p + add_ref[...]
            o_ref[...] = p.astype(out_dtype)
        else:
            acc_ref = refs[n_in + 1]
            kk = pl.program_id(2)

            @pl.when(kk == 0)
            def _():
                acc_ref[...] = p

            @pl.when(kk > 0)
            def _():
                acc_ref[...] += p

            @pl.when(kk == nk - 1)
            def _():
                r = acc_ref[...]
                if has_add:
                    r = r + add_ref[...]
                o_ref[...] = r.astype(out_dtype)

    if mode == "tn":
        a_spec = pl.BlockSpec((tk, tm), lambda j, i, kk: (kk, i))
    else:
        a_spec = pl.BlockSpec((tm, tk), lambda j, i, kk: (i, kk))
    if mode == "nt":
        b_spec = pl.BlockSpec((tn, tk), lambda j, i, kk: (j, kk))
    else:
        b_spec = pl.BlockSpec((tk, tn), lambda j, i, kk: (kk, j))
    o_spec = pl.BlockSpec((tm, tn), lambda j, i, kk: (i, j))
    in_specs = [a_spec, b_spec] + ([o_spec] if has_add else [])
    args = (a, b) + ((add,) if has_add else ())
    if after is not None:
        in_specs.append(pl.BlockSpec(memory_space=pl.ANY))
        args += (after,)
    return pl.pallas_call(
        body, name=name, grid=(n // tn, m // tm, nk), in_specs=in_specs, out_specs=o_spec,
        out_shape=jax.ShapeDtypeStruct((m, n), out_dtype),
        scratch_shapes=[pltpu.VMEM((tm, tn), F32)] if nk > 1 else [],
        compiler_params=_params(("parallel", "parallel", "arbitrary")),
    )(*args)


def _pick(n, cands):
    for c in cands:
        if n % c == 0:
            return c
    return n


def _rms_fwd(h, w, name):
    rows, d = h.shape
    tm = _row_tile(rows)

    def body(h_ref, w_ref, u_ref):
        x = h_ref[...]
        r = lax.rsqrt(jnp.mean(x * x, axis=-1, keepdims=True) + EPS)
        u_ref[...] = (x * r * w_ref[...]).astype(ACT_DTYPE)

    return pl.pallas_call(
        body, name=name, grid=(rows // tm,),
        in_specs=[pl.BlockSpec((tm, d), lambda i: (i, 0)), pl.BlockSpec((1, d), lambda i: (0, 0))],
        out_specs=pl.BlockSpec((tm, d), lambda i: (i, 0)),
        out_shape=jax.ShapeDtypeStruct((rows, d), ACT_DTYPE),
        compiler_params=_params(("parallel",)),
    )(h, w)


def _rms_bwd(du, h, w, dres, name):
    rows, d = h.shape
    tm = _row_tile(rows)

    def body(du_ref, h_ref, w_ref, dres_ref, dh_ref, dw_ref):
        x = h_ref[...]
        dy = du_ref[...].astype(F32)
        r = lax.rsqrt(jnp.mean(x * x, axis=-1, keepdims=True) + EPS)
        xhat = x * r
        dxn = dy * w_ref[...]
        dx = r * (dxn - xhat * jnp.mean(dxn * xhat, axis=-1, keepdims=True))
        dh_ref[...] = dres_ref[...] + dx

        @pl.when(pl.program_id(0) == 0)
        def _():
            dw_ref[...] = jnp.zeros_like(dw_ref)

        dw_ref[...] += jnp.sum(dy * xhat, axis=0, keepdims=True)

    return pl.pallas_call(
        body, name=name, grid=(rows // tm,),
        in_specs=[pl.BlockSpec((tm, d), lambda i: (i, 0)), pl.BlockSpec((tm, d), lambda i: (i, 0)),
                  pl.BlockSpec((1, d), lambda i: (0, 0)), pl.BlockSpec((tm, d), lambda i: (i, 0))],
        out_specs=[pl.BlockSpec((tm, d), lambda i: (i, 0)), pl.BlockSpec((1, d), lambda i: (0, 0))],
        out_shape=[jax.ShapeDtypeStruct((rows, d), F32), jax.ShapeDtypeStruct((1, d), F32)],
        compiler_params=_params(("arbitrary",)),
    )(du, h, w, dres)


def _loss_head(h2, tgt, w):
    rows, d = h2.shape
    tm = _row_tile(rows)

    def body(h_ref, t_ref, w_ref, dh_ref, loss_ref, dw_ref):
        i = pl.program_id(0)
        x = h_ref[...]
        r = lax.rsqrt(jnp.mean(x * x, axis=-1, keepdims=True) + EPS)
        xhat = x * r
        wv = w_ref[...]
        row = i * tm + lax.broadcasted_iota(jnp.int32, (tm, 1), 0)
        live = row >= PAD_ROWS
        diff = jnp.where(live, xhat * wv - t_ref[...], 0.0)
        dy = diff * (1.0 / d)
        dxn = dy * wv
        dh_ref[...] = r * (dxn - xhat * jnp.mean(dxn * xhat, axis=-1, keepdims=True))

        @pl.when(i == 0)
        def _():
            loss_ref[...] = jnp.zeros_like(loss_ref)
            dw_ref[...] = jnp.zeros_like(dw_ref)

        loss_ref[...] += 0.5 * jnp.sum(jnp.mean(diff * diff, axis=-1, keepdims=True))
        dw_ref[...] += jnp.sum(dy * xhat, axis=0, keepdims=True)

    return pl.pallas_call(
        body, name="loss_head", grid=(rows // tm,),
        in_specs=[pl.BlockSpec((tm, d), lambda i: (i, 0)), pl.BlockSpec((tm, d), lambda i: (i, 0)),
                  pl.BlockSpec((1, d), lambda i: (0, 0))],
        out_specs=[pl.BlockSpec((tm, d), lambda i: (i, 0)), pl.BlockSpec((8, LANES), lambda i: (0, 0)),
                   pl.BlockSpec((1, d), lambda i: (0, 0))],
        out_shape=[jax.ShapeDtypeStruct((rows, d), F32), jax.ShapeDtypeStruct((8, LANES), F32),
                   jax.ShapeDtypeStruct((1, d), F32)],
        compiler_params=_params(("arbitrary",)),
    )(h2, tgt, w)


def _prev_halo_spec(tm, width, col):
    return pl.BlockSpec((HALO, width), lambda j, i: (jnp.maximum(i * (tm // HALO) - 1, 0), col(j)))


def _next_halo_spec(tm, rows, width, col):
    last = rows // HALO - 1
    return pl.BlockSpec((HALO, width), lambda j, i: (jnp.minimum((i + 1) * (tm // HALO), last), col(j)))


def _conv_taps(cat, w_ref, b_ref, kw):
    acc = b_ref[...] + w_ref[kw - 1:kw, :] * cat
    for s in range(1, kw):
        acc = acc + w_ref[kw - 1 - s:kw - s, :] * pltpu.roll(cat, s, 0)
    return acc


def _conv_back(dpre, w_ref, kw):
    n = dpre.shape[0]
    acc = w_ref[kw - 1:kw, :] * dpre
    for s in range(1, kw):
        acc = acc + w_ref[kw - 1 - s:kw - s, :] * pltpu.roll(dpre, n - s, 0)
    return acc


def _ssd_conv_fwd(xbc, w, b):
    rows, width = xbc.shape
    tm, tc = _row_tile(rows), 512

    def body(x_ref, xp_ref, w_ref, b_ref, o_ref):
        i = pl.program_id(1)
        xp = jnp.where(i == 0, 0.0, xp_ref[...].astype(F32))
        cat = jnp.concatenate([xp, x_ref[...].astype(F32)], axis=0)
        pre = _conv_taps(cat, w_ref, b_ref, SSD_CONV)[HALO:]
        row = i * tm + lax.broadcasted_iota(jnp.int32, (tm, 1), 0)
        o_ref[...] = jnp.where(row >= FRONT, _silu(pre), 0.0).astype(ACT_DTYPE)

    main = pl.BlockSpec((tm, tc), lambda j, i: (i, j))
    par = lambda r: pl.BlockSpec((r, tc), lambda j, i: (0, j))
    return pl.pallas_call(
        body, name="ssd_conv_fwd", grid=(width // tc, rows // tm),
        in_specs=[main, _prev_halo_spec(tm, tc, lambda j: j), par(SSD_CONV), par(1)],
        out_specs=main, out_shape=jax.ShapeDtypeStruct((rows, width), ACT_DTYPE),
        compiler_params=_params(("parallel", "parallel")),
    )(xbc, xbc, w, b)


def _ssd_conv_bwd(xbc, dxc, w, b):
    rows, width = xbc.shape
    tm, tc = _row_tile(rows), 512
    kw = SSD_CONV

    def body(x_ref, xp_ref, xn_ref, d_ref, dn_ref, w_ref, b_ref, dx_ref, dw_ref, db_ref):
        i = pl.program_id(1)
        xp = jnp.where(i == 0, 0.0, xp_ref[...].astype(F32))
        cat = jnp.concatenate([xp, x_ref[...].astype(F32), xn_ref[...].astype(F32)], axis=0)
        pre = _conv_taps(cat, w_ref, b_ref, kw)[HALO:]
        row = i * tm + lax.broadcasted_iota(jnp.int32, (tm + HALO, 1), 0)
        live = (row >= FRONT) & (row < rows)
        dout = jnp.concatenate([d_ref[...].astype(F32), dn_ref[...].astype(F32)], axis=0)
        dpre = jnp.where(live, dout * _dsilu(pre), 0.0)
        dx_ref[...] = _conv_back(dpre, w_ref, kw)[:tm].astype(ACT_DTYPE)

        @pl.when(i == 0)
        def _():
            dw_ref[...] = jnp.zeros_like(dw_ref)
            db_ref[...] = jnp.zeros_like(db_ref)

        dmain = dpre[:tm]
        db_ref[...] += jnp.sum(dmain, axis=0, keepdims=True)
        for k in range(kw):
            s = kw - 1 - k
            xs = (pltpu.roll(cat, s, 0) if s else cat)[HALO:HALO + tm]
            dw_ref[k:k + 1, :] += jnp.sum(dmain * xs, axis=0, keepdims=True)

    main = pl.BlockSpec((tm, tc), lambda j, i: (i, j))
    par = lambda r: pl.BlockSpec((r, tc), lambda j, i: (0, j))
    col = lambda j: j
    return pl.pallas_call(
        body, name="ssd_conv_bwd", grid=(width // tc, rows // tm),
        in_specs=[main, _prev_halo_spec(tm, tc, col), _next_halo_spec(tm, rows, tc, col),
                  main, _next_halo_spec(tm, rows, tc, col), par(kw), par(1)],
        out_specs=[main, par(kw), par(1)],
        out_shape=[jax.ShapeDtypeStruct((rows, width), ACT_DTYPE), jax.ShapeDtypeStruct((kw, width), F32),
                   jax.ShapeDtypeStruct((1, width), F32)],
        compiler_params=_params(("parallel", "arbitrary")),
    )(xbc, xbc, xbc, dxc, dxc, w, b)


def _ffn_conv_fwd(up, w, b):
    rows, width = up.shape
    dff = width // 2
    tm, tc = _row_tile(rows), _pick(dff, (256, 128))
    nb = dff // tc
    kw = FFN_CONV

    def body(g_ref, gp_ref, v_ref, vp_ref, wg_ref, bg_ref, wv_ref, bv_ref, o_ref):
        i = pl.program_id(1)

        def pre(x_ref, xp_ref, w_ref, b_ref):
            xp = jnp.where(i == 0, 0.0, xp_ref[...].astype(F32))
            cat = jnp.concatenate([xp, x_ref[...].astype(F32)], axis=0)
            return _conv_taps(cat, w_ref, b_ref, kw)[HALO:]

        o_ref[...] = (_silu(pre(g_ref, gp_ref, wg_ref, bg_ref)) * pre(v_ref, vp_ref, wv_ref, bv_ref)).astype(ACT_DTYPE)

    gcol, vcol = (lambda j: j), (lambda j: j + nb)
    main = lambda col: pl.BlockSpec((tm, tc), lambda j, i: (i, col(j)))
    par = lambda r, col: pl.BlockSpec((r, tc), lambda j, i: (0, col(j)))
    return pl.pallas_call(
        body, name="ffn_conv_fwd", grid=(nb, rows // tm),
        in_specs=[main(gcol), _prev_halo_spec(tm, tc, gcol), main(vcol), _prev_halo_spec(tm, tc, vcol),
                  par(kw, gcol), par(1, gcol), par(kw, vcol), par(1, vcol)],
        out_specs=pl.BlockSpec((tm, tc), lambda j, i: (i, j)),
        out_shape=jax.ShapeDtypeStruct((rows, dff), ACT_DTYPE),
        compiler_params=_params(("parallel", "parallel")),
    )(up, up, up, up, w, b, w, b)


def _ffn_conv_bwd(up, dact, w, b):
    rows, width = up.shape
    dff = width // 2
    tm, tc = _row_tile(rows), _pick(dff, (256, 128))
    nb = dff // tc
    kw = FFN_CONV

    def body(g_ref, gp_ref, gn_ref, v_ref, vp_ref, vn_ref, d_ref, dn_ref, wg_ref, bg_ref, wv_ref, bv_ref,
             dxg_ref, dxv_ref, dwg_ref, dbg_ref, dwv_ref, dbv_ref):
        i = pl.program_id(1)

        def shifted(x_ref, xp_ref, xn_ref):
            xp = jnp.where(i == 0, 0.0, xp_ref[...].astype(F32))
            cat = jnp.concatenate([xp, x_ref[...].astype(F32), xn_ref[...].astype(F32)], axis=0)
            return [cat] + [pltpu.roll(cat, s, 0) for s in range(1, kw)]

        def taps(sh, w_ref, b_ref):
            acc = b_ref[...] + w_ref[kw - 1:kw, :] * sh[0]
            for s in range(1, kw):
                acc = acc + w_ref[kw - 1 - s:kw - s, :] * sh[s]
            return acc[HALO:]

        sh_g, sh_v = shifted(g_ref, gp_ref, gn_ref), shifted(v_ref, vp_ref, vn_ref)
        ag, av = taps(sh_g, wg_ref, bg_ref), taps(sh_v, wv_ref, bv_ref)
        row = i * tm + lax.broadcasted_iota(jnp.int32, (tm + HALO, 1), 0)
        dout = jnp.concatenate([d_ref[...].astype(F32), dn_ref[...].astype(F32)], axis=0)
        dout = jnp.where(row < rows, dout, 0.0)
        s = jax.nn.sigmoid(ag)
        silu = ag * s
        dpre_v = dout * silu
        dpre_g = dout * av * (s + silu * (1.0 - s))

        @pl.when(i == 0)
        def _():
            for r in (dwg_ref, dbg_ref, dwv_ref, dbv_ref):
                r[...] = jnp.zeros_like(r)

        for dpre, sh, w_ref, dx_ref, dw_ref, db_ref in ((dpre_g, sh_g, wg_ref, dxg_ref, dwg_ref, dbg_ref),
                                                          (dpre_v, sh_v, wv_ref, dxv_ref, dwv_ref, dbv_ref)):
            dx_ref[...] = _conv_back(dpre, w_ref, kw)[:tm].astype(ACT_DTYPE)
            dmain = dpre[:tm]
            db_ref[...] += jnp.sum(dmain, axis=0, keepdims=True)
            for k in range(kw):
                dw_ref[k:k + 1, :] += jnp.sum(dmain * sh[kw - 1 - k][HALO:HALO + tm], axis=0, keepdims=True)

    gcol, vcol = (lambda j: j), (lambda j: j + nb)
    main = lambda col: pl.BlockSpec((tm, tc), lambda j, i: (i, col(j)))
    par = lambda r, col: pl.BlockSpec((r, tc), lambda j, i: (0, col(j)))
    halos = lambda col: [_prev_halo_spec(tm, tc, col), _next_halo_spec(tm, rows, tc, col)]
    act_shape = jax.ShapeDtypeStruct((rows, dff), ACT_DTYPE)
    par_shapes = [jax.ShapeDtypeStruct((kw, dff), F32), jax.ShapeDtypeStruct((1, dff), F32)]
    return pl.pallas_call(
        body, name="ffn_conv_bwd", grid=(nb, rows // tm),
        in_specs=[main(gcol)] + halos(gcol) + [main(vcol)] + halos(vcol) + [main(gcol), _next_halo_spec(tm, rows, tc, gcol),
                  par(kw, gcol), par(1, gcol), par(kw, vcol), par(1, vcol)],
        out_specs=[main(gcol), main(gcol), par(kw, gcol), par(1, gcol), par(kw, gcol), par(1, gcol)],
        out_shape=[act_shape, act_shape] + par_shapes + par_shapes,
        compiler_params=_params(("parallel", "arbitrary")),
    )(up, up, up, up, up, up, dact, dact, w, b, w, b)


def _ssd_chunk(xs, bm, cm, dtr, z, prev, dt_bias, a_log, d_skip, nw, live):
    q = CHUNK
    dt = jnp.where(live, jax.nn.softplus(dtr + dt_bias), 0.0)
    a = dt * (-jnp.exp(a_log))
    li = lax.broadcasted_iota(jnp.int32, (q, q), 0)
    si = lax.broadcasted_iota(jnp.int32, (q, q), 1)
    causal = li >= si
    tri = jnp.where(causal, 1.0, 0.0).astype(F32)
    a_cs = jnp.dot(tri, a, precision=lax.Precision.HIGHEST, preferred_element_type=F32)
    a_cs_t = a_cs.T
    a_end = a_cs[q - 1:q, :]
    dt_x, e_x, w_x = _head_cols(dt), _head_cols(jnp.exp(a_cs)), _head_cols(jnp.exp(a_end - a_cs))
    xdt = xs * dt_x
    cb = _dot(cm, bm, NT)
    y = _dot(cm, prev) * e_x + _head_cols(d_skip) * xs
    new = prev * _head_cols(jnp.exp(a_end)) + _dot(bm, xdt * w_x, TN)
    ys = []
    for hh in range(SSD_HPG):
        decay = jnp.exp(jnp.where(causal, a_cs[:, hh:hh + 1] - a_cs_t[hh:hh + 1, :], -jnp.inf))
        ys.append(_dot(cb * decay, xdt[:, hh * SSD_HEAD_DIM:(hh + 1) * SSD_HEAD_DIM]))
    yz = (y + jnp.concatenate(ys, axis=1)) * _silu(z)
    out = yz * lax.rsqrt(jnp.mean(yz * yz, axis=-1, keepdims=True) + EPS) * nw
    return out, new


def _head_cols(t):
    return jnp.concatenate([jnp.broadcast_to(t[:, hh:hh + 1], (t.shape[0], SSD_HEAD_DIM)) for hh in range(SSD_HPG)], axis=1)


def _ssd_specs(rev, nc):
    cidx = (lambda c: nc - 1 - c) if rev else (lambda c: c)
    nb_b = SSD_D_INNER // SSD_STATE
    row = lambda width, col: pl.BlockSpec((CHUNK, width), lambda g, c: (cidx(c), col(g)))
    par = lambda width: pl.BlockSpec((1, width), lambda g, c: (0, g))
    state = lambda: pl.BlockSpec((1, 1, SSD_STATE, SSD_GW), lambda g, c: (cidx(c), g, 0, 0))
    ins = [row(SSD_GW, lambda g: g), row(SSD_STATE, lambda g: nb_b + g), row(SSD_STATE, lambda g: nb_b + SSD_GROUPS + g),
           row(LANES, lambda g: g), row(SSD_GW, lambda g: g), par(LANES), par(LANES), par(LANES), par(SSD_GW)]
    return cidx, row, par, state, ins


def _ssd_fwd(xbc_c, dtr, z, dt_bias, a_log, d_skip, nw):
    rows = z.shape[0]
    nc = rows // CHUNK
    _, row, _, state, ins = _ssd_specs(False, nc)

    def body(xs_ref, b_ref, c_ref, dt_ref, z_ref, bias_ref, al_ref, dk_ref, nw_ref, y_ref, st_ref, carry):
        c = pl.program_id(1)

        @pl.when(c == 0)
        def _():
            carry[...] = jnp.zeros_like(carry)

        live = c * CHUNK + lax.broadcasted_iota(jnp.int32, (CHUNK, 1), 0) >= FRONT
        prev = carry[...]
        st_ref[0, 0] = prev
        out, new = _ssd_chunk(xs_ref[...].astype(F32), b_ref[...].astype(F32), c_ref[...].astype(F32), dt_ref[...],
                              z_ref[...].astype(F32), prev, bias_ref[...], al_ref[...], dk_ref[...], nw_ref[...], live)
        y_ref[...] = out.astype(ACT_DTYPE)
        carry[...] = new

    return pl.pallas_call(
        body, name="ssd_fwd", grid=(SSD_GROUPS, nc), in_specs=ins,
        out_specs=[row(SSD_GW, lambda g: g), state()],
        out_shape=[jax.ShapeDtypeStruct((rows, SSD_D_INNER), ACT_DTYPE),
                   jax.ShapeDtypeStruct((nc, SSD_GROUPS, SSD_STATE, SSD_GW), F32)],
        scratch_shapes=[pltpu.VMEM((SSD_STATE, SSD_GW), F32)],
        compiler_params=_params(("parallel", "arbitrary")),
    )(xbc_c, xbc_c, xbc_c, dtr, z, dt_bias, a_log, d_skip, nw)


def _ssd_bwd(xbc_c, dtr, z, dt_bias, a_log, d_skip, nw, states, dy):
    rows = z.shape[0]
    nc = rows // CHUNK
    cidx, row, par, state, ins = _ssd_specs(True, nc)

    def body(xs_ref, b_ref, c_ref, dt_ref, z_ref, bias_ref, al_ref, dk_ref, nw_ref, st_ref, dy_ref,
             dxs_ref, db_ref, dc_ref, ddt_ref, dz_ref, dbias_ref, dal_ref, ddk_ref, dnw_ref, carry):
        c = pl.program_id(1)

        @pl.when(c == 0)
        def _():
            carry[...] = jnp.zeros_like(carry)
            for r in (dbias_ref, dal_ref, ddk_ref, dnw_ref):
                r[...] = jnp.zeros_like(r)

        live = cidx(c) * CHUNK + lax.broadcasted_iota(jnp.int32, (CHUNK, 1), 0) >= FRONT
        fn = functools.partial(_ssd_chunk, live=live)
        _, vjp = jax.vjp(fn, xs_ref[...].astype(F32), b_ref[...].astype(F32), c_ref[...].astype(F32), dt_ref[...],
                         z_ref[...].astype(F32), st_ref[0, 0], bias_ref[...], al_ref[...], dk_ref[...], nw_ref[...])
        dxs, dbm, dcm, ddt, dz, dprev, dbias, dal, ddk, dnw = vjp((dy_ref[...].astype(F32), carry[...]))
        dxs_ref[...] = dxs.astype(ACT_DTYPE)
        db_ref[...] = dbm.astype(ACT_DTYPE)
        dc_ref[...] = dcm.astype(ACT_DTYPE)
        ddt_ref[...] = ddt
        dz_ref[...] = dz.astype(ACT_DTYPE)
        carry[...] = dprev
        dbias_ref[...] += dbias
        dal_ref[...] += dal
        ddk_ref[...] += ddk
        dnw_ref[...] += dnw

    gcol = lambda g: g
    return pl.pallas_call(
        body, name="ssd_bwd", grid=(SSD_GROUPS, nc),
        in_specs=ins + [state(), row(SSD_GW, gcol)],
        out_specs=[row(SSD_GW, gcol), row(SSD_STATE, gcol), row(SSD_STATE, gcol), row(LANES, gcol), row(SSD_GW, gcol),
                   par(LANES), par(LANES), par(LANES), par(SSD_GW)],
        out_shape=[jax.ShapeDtypeStruct((rows, SSD_D_INNER), ACT_DTYPE),
                   jax.ShapeDtypeStruct((rows, SSD_GROUPS * SSD_STATE), ACT_DTYPE),
                   jax.ShapeDtypeStruct((rows, SSD_GROUPS * SSD_STATE), ACT_DTYPE),
                   jax.ShapeDtypeStruct((rows, SSD_GROUPS * LANES), F32),
                   jax.ShapeDtypeStruct((rows, SSD_D_INNER), ACT_DTYPE),
                   jax.ShapeDtypeStruct((1, SSD_GROUPS * LANES), F32), jax.ShapeDtypeStruct((1, SSD_GROUPS * LANES), F32),
                   jax.ShapeDtypeStruct((1, SSD_GROUPS * LANES), F32), jax.ShapeDtypeStruct((1, SSD_D_INNER), F32)],
        scratch_shapes=[pltpu.VMEM((SSD_STATE, SSD_GW), F32)],
        compiler_params=_params(("parallel", "arbitrary")),
    )(xbc_c, xbc_c, xbc_c, dtr, z, dt_bias, a_log, d_skip, nw, states, dy)


def _rotary(t, cos, sin):
    half = t.shape[-1] // 2
    t1, t2 = t[:, :half], t[:, half:]
    return jnp.concatenate([t1 * cos - t2 * sin, t2 * cos + t1 * sin], axis=1)


def _ret_chunk(qh, kh, vh, gh, prev, cos, sin, lg):
    q = CHUNK
    qr = _rotary(qh, cos, sin)
    kr = _rotary(kh, cos, sin) * (RET_QK ** -0.5)
    li = lax.broadcasted_iota(jnp.int32, (q, q), 0)
    si = lax.broadcasted_iota(jnp.int32, (q, q), 1)
    dist = (li - si).astype(F32)
    decay = jnp.exp(jnp.where(li >= si, dist * lg, -jnp.inf))
    idx = lax.broadcasted_iota(jnp.int32, (q, 1), 0).astype(F32)
    scores = _dot(qr, kr, NT) * decay
    out = _dot(scores, vh)
    kv = _dot(kr * jnp.exp((q - 1.0 - idx) * lg), vh, TN)
    out = out + _dot(qr, prev) * jnp.exp((idx + 1.0) * lg)
    new = prev * jnp.exp(q * lg) + kv
    out = out * lax.rsqrt(jnp.mean(out * out, axis=-1, keepdims=True) + EPS)
    return _silu(gh) * out, new


def _ret_specs(rev, nc):
    cidx = (lambda c: nc - 1 - c) if rev else (lambda c: c)
    row = lambda width: pl.BlockSpec((CHUNK, width), lambda h, c: (cidx(c), h))
    tab = lambda: pl.BlockSpec((CHUNK, RET_QK // 2), lambda h, c: (cidx(c), 0))
    lgs = lambda: pl.BlockSpec((1, 8, LANES), lambda h, c: (h, 0, 0))
    state = lambda: pl.BlockSpec((1, 1, RET_QK, RET_V), lambda h, c: (cidx(c), h, 0, 0))
    ins = [row(RET_QK), row(RET_QK), row(RET_V), row(RET_V), tab(), tab(), lgs()]
    return row, state, ins


def _ret_fwd(q, k, v, g, cos, sin, lgam):
    rows = q.shape[0]
    nc = rows // CHUNK
    row, state, ins = _ret_specs(False, nc)

    def body(q_ref, k_ref, v_ref, g_ref, cos_ref, sin_ref, lg_ref, y_ref, st_ref, carry):
        c = pl.program_id(1)

        @pl.when(c == 0)
        def _():
            carry[...] = jnp.zeros_like(carry)

        prev = carry[...]
        st_ref[0, 0] = prev.astype(ACT_DTYPE)
        out, new = _ret_chunk(q_ref[...].astype(F32), k_ref[...].astype(F32), v_ref[...].astype(F32),
                              g_ref[...].astype(F32), prev, cos_ref[...], sin_ref[...], lg_ref[0, 0:1, 0:1])
        y_ref[...] = out.astype(ACT_DTYPE)
        carry[...] = new

    return pl.pallas_call(
        body, name="ret_fwd", grid=(RET_HEADS, nc), in_specs=ins, out_specs=[row(RET_V), state()],
        out_shape=[jax.ShapeDtypeStruct((rows, RET_HEADS * RET_V), ACT_DTYPE),
                   jax.ShapeDtypeStruct((nc, RET_HEADS, RET_QK, RET_V), ACT_DTYPE)],
        scratch_shapes=[pltpu.VMEM((RET_QK, RET_V), F32)],
        compiler_params=_params(("parallel", "arbitrary")),
    )(q, k, v, g, cos, sin, lgam)


def _ret_bwd(q, k, v, g, cos, sin, lgam, states, dy):
    rows = q.shape[0]
    nc = rows // CHUNK
    row, state, ins = _ret_specs(True, nc)

    def body(q_ref, k_ref, v_ref, g_ref, cos_ref, sin_ref, lg_ref, st_ref, dy_ref, dq_ref, dk_ref, dv_ref, dg_ref, carry):
        c = pl.program_id(1)

        @pl.when(c == 0)
        def _():
            carry[...] = jnp.zeros_like(carry)

        fn = functools.partial(_ret_chunk, cos=cos_ref[...], sin=sin_ref[...], lg=lg_ref[0, 0:1, 0:1])
        _, vjp = jax.vjp(fn, q_ref[...].astype(F32), k_ref[...].astype(F32), v_ref[...].astype(F32),
                         g_ref[...].astype(F32), st_ref[0, 0].astype(F32))
        dq, dk, dv, dg, dprev = vjp((dy_ref[...].astype(F32), carry[...]))
        dq_ref[...] = dq.astype(ACT_DTYPE)
        dk_ref[...] = dk.astype(ACT_DTYPE)
        dv_ref[...] = dv.astype(ACT_DTYPE)
        dg_ref[...] = dg.astype(ACT_DTYPE)
        carry[...] = dprev

    shp = lambda width: jax.ShapeDtypeStruct((rows, RET_HEADS * width), ACT_DTYPE)
    return pl.pallas_call(
        body, name="ret_bwd", grid=(RET_HEADS, nc), in_specs=ins + [state(), row(RET_V)],
        out_specs=[row(RET_QK), row(RET_QK), row(RET_V), row(RET_V)],
        out_shape=[shp(RET_QK), shp(RET_QK), shp(RET_V), shp(RET_V)],
        scratch_shapes=[pltpu.VMEM((RET_QK, RET_V), F32)],
        compiler_params=_params(("parallel", "arbitrary")),
    )(q, k, v, g, cos, sin, lgam, states, dy)


def _merge_fwd(bs, br, gs, gr):
    rows, d = bs.shape
    tm = _row_tile(rows)

    def body(bs_ref, br_ref, gs_ref, gr_ref, o_ref):
        o_ref[...] = (jax.nn.sigmoid(gs_ref[...].astype(F32)) * bs_ref[...].astype(F32)
                      + jax.nn.sigmoid(gr_ref[...].astype(F32)) * br_ref[...].astype(F32)).astype(ACT_DTYPE)

    spec = pl.BlockSpec((tm, d), lambda i: (i, 0))
    return pl.pallas_call(
        body, name="merge_fwd", grid=(rows // tm,), in_specs=[spec] * 4, out_specs=spec,
        out_shape=jax.ShapeDtypeStruct((rows, d), ACT_DTYPE), compiler_params=_params(("parallel",)),
    )(bs, br, gs, gr)


def _merge_bwd(dm, bs, br, gs, gr):
    rows, d = bs.shape
    tm = _row_tile(rows)

    def body(dm_ref, bs_ref, br_ref, gs_ref, gr_ref, dbs_ref, dbr_ref, dgs_ref, dgr_ref):
        dmv = dm_ref[...].astype(F32)
        for b_ref, g_ref, db_ref, dg_ref in ((bs_ref, gs_ref, dbs_ref, dgs_ref), (br_ref, gr_ref, dbr_ref, dgr_ref)):
            s = jax.nn.sigmoid(g_ref[...].astype(F32))
            db_ref[...] = (dmv * s).astype(ACT_DTYPE)
            dg_ref[...] = (dmv * b_ref[...].astype(F32) * s * (1.0 - s)).astype(ACT_DTYPE)

    spec = pl.BlockSpec((tm, d), lambda i: (i, 0))
    shp = jax.ShapeDtypeStruct((rows, d), ACT_DTYPE)
    return pl.pallas_call(
        body, name="merge_bwd", grid=(rows // tm,), in_specs=[spec] * 5, out_specs=[spec] * 4,
        out_shape=[shp] * 4, compiler_params=_params(("parallel",)),
    )(dm, bs, br, gs, gr)


def _place():
    x, y, c = lax.axis_index("x"), lax.axis_index("y"), lax.axis_index("c")
    return x, y, c


def _slot(p):
    return 4 * p[0] + 2 * p[1] + p[2]


def _allgather(arrs, name):
    n = len(arrs)
    any_spec = pl.BlockSpec(memory_space=pl.ANY)

    def body(*refs):
        ins, outs = refs[:n], refs[n:2 * n]
        send_sems, recv_sems, local_sems = refs[2 * n:]
        x, y, c = _place()
        me, sibling = (x, y, c), (x, y, 1 - c)
        chips = [(1 - x, y), (x, 1 - y), (1 - x, 1 - y)]

        def copy(a, k, block, to, src=None):
            dst = outs[a].at[_slot(block)]
            return pltpu.make_async_remote_copy(
                src_ref=dst if src is None else src, dst_ref=dst, send_sem=send_sems.at[a * 7 + k],
                recv_sem=recv_sems.at[a * 7 + k], device_id=to, device_id_type=MESH)

        mine, first, passed = [], [], []
        for a in range(n):
            cp = pltpu.make_async_copy(ins[a], outs[a].at[_slot(me)], local_sems.at[a])
            cp.start()
            mine.append(cp)
            first.append(copy(a, 0, me, sibling, src=ins[a]))
            first += [copy(a, 1 + j, me, (*chip, c), src=ins[a]) for j, chip in enumerate(chips)]
        for cp in first:
            cp.start()
        for j, chip in enumerate(chips):
            for a in range(n):
                copy(a, 1 + j, (*chip, c), me).wait_recv()
                cp = copy(a, 4 + j, (*chip, c), sibling)
                cp.start()
                passed.append(cp)
        for a in range(n):
            copy(a, 0, sibling, me).wait_recv()
            for j, chip in enumerate(chips):
                copy(a, 4 + j, (*chip, 1 - c), me).wait_recv()
        for cp in first + passed:
            cp.wait_send()
        for cp in mine:
            cp.wait()

    return pl.pallas_call(
        body, name=name, in_specs=[any_spec] * n, out_specs=[any_spec] * n,
        out_shape=[jax.ShapeDtypeStruct((N_DEV,) + a.shape, a.dtype) for a in arrs],
        scratch_shapes=[pltpu.SemaphoreType.DMA((7 * n,)), pltpu.SemaphoreType.DMA((7 * n,)), pltpu.SemaphoreType.DMA((n,))],
    )(*arrs)


def _peers():
    x, y, c = _place()
    return (x, y, c), [(x ^ dx, y ^ dy, c ^ dc) for dx in (0, 1) for dy in (0, 1) for dc in (0, 1)][1:]


def _exchange_copies(srcs, lands, send_sems, recv_sems, scatter, sender):
    me, peers = _peers()
    out = []
    for a, (src, land) in enumerate(zip(srcs, lands, strict=True)):
        for k, peer in enumerate(peers):
            src_ref = src.at[_slot(peer)] if scatter else src
            out.append(pltpu.make_async_remote_copy(
                src_ref=src_ref, dst_ref=land.at[_slot(me if sender else peer)], send_sem=send_sems.at[a * 7 + k],
                recv_sem=recv_sems.at[a * 7 + k], device_id=peer, device_id_type=MESH))
    return out


_HBM = pl.BlockSpec(memory_space=pltpu.HBM)
_SEM = pl.BlockSpec(memory_space=pltpu.SEMAPHORE)
_EFFECT = pltpu.SideEffectType.DATAFLOW_SIDE_EFFECTING


def _exchange_start(srcs, scatter, name):
    n = len(srcs)
    land_shapes = [s.shape if scatter else (N_DEV,) + s.shape for s in srcs]

    def body(*refs):
        for cp in _exchange_copies(refs[:n], refs[n:2 * n], refs[2 * n], refs[2 * n + 1], scatter, True):
            cp.start()
        refs[-1][...] = jnp.zeros_like(refs[-1])

    args = [pltpu.with_memory_space_constraint(s, pltpu.HBM) for s in srcs]
    args += [pltpu.with_memory_space_constraint(lax.empty(shp, s.dtype), pltpu.HBM) for s, shp in zip(srcs, land_shapes)]
    outs = pl.pallas_call(
        body, name=name,
        out_shape=(pltpu.SemaphoreType.DMA((7 * n,)), pltpu.SemaphoreType.DMA((7 * n,)))
        + tuple(pltpu.HBM(a.shape, a.dtype) for a in args) + (jax.ShapeDtypeStruct((8, LANES), F32),),
        in_specs=[_HBM] * (2 * n), out_specs=(_SEM, _SEM) + (_HBM,) * (2 * n) + (pl.BlockSpec(memory_space=pltpu.VMEM),),
        input_output_aliases={i: 2 + i for i in range(2 * n)},
        compiler_params=pltpu.CompilerParams(has_side_effects=_EFFECT),
    )(*args)
    return outs[:-1], outs[-1]


def _exchange_wait(handle, scatter, after, name):
    n = (len(handle) - 2) // 2
    thru = handle[2:]

    def body(*refs):
        for cp in _exchange_copies(refs[:n], refs[n:2 * n], refs[2 * n], refs[2 * n + 1], scatter, False):
            cp.wait_send()
            cp.wait_recv()

    outs = pl.pallas_call(
        body, name=name, out_shape=tuple(pltpu.HBM(t.shape, t.dtype) for t in thru),
        in_specs=[_HBM] * (2 * n) + [_SEM, _SEM, pl.BlockSpec(memory_space=pl.ANY)], out_specs=(_HBM,) * (2 * n),
        input_output_aliases={i: i for i in range(2 * n)},
        compiler_params=pltpu.CompilerParams(has_side_effects=_EFFECT),
    )(*thru, handle[0], handle[1], after)
    return list(outs[:n]), list(outs[n:])


def _allreduce_small(pack):
    rows, lanes = pack.shape

    def body(x_ref, o_ref, buf, send_sems, recv_sems):
        x, y, c = _place()
        me, sibling = (x, y, c), (x, y, 1 - c)
        chips = [(1 - x, y), (x, 1 - y), (1 - x, 1 - y)]

        def copy(k, block, to, src=None):
            dst = buf.at[_slot(block)]
            return pltpu.make_async_remote_copy(
                src_ref=dst if src is None else src, dst_ref=dst, send_sem=send_sems.at[k], recv_sem=recv_sems.at[k],
                device_id=to, device_id_type=MESH)

        buf[_slot(me)] = x_ref[...]
        first = [copy(0, me, sibling, src=x_ref)]
        first += [copy(1 + j, me, (*chip, c), src=x_ref) for j, chip in enumerate(chips)]
        for cp in first:
            cp.start()
        passed = [copy(4 + j, (*chip, c), sibling) for j, chip in enumerate(chips)]
        for j, chip in enumerate(chips):
            copy(1 + j, (*chip, c), me).wait_recv()
            passed[j].start()
        copy(0, sibling, me).wait_recv()
        for j, chip in enumerate(chips):
            copy(4 + j, (*chip, 1 - c), me).wait_recv()
        for cp in first + passed:
            cp.wait_send()
        acc = buf[0]
        for i in range(1, N_DEV):
            acc = acc + buf[i]
        o_ref[...] = acc

    vmem = pl.BlockSpec(memory_space=pltpu.VMEM)
    return pl.pallas_call(
        body, name="allreduce_small", in_specs=[vmem], out_specs=vmem,
        out_shape=jax.ShapeDtypeStruct((rows, lanes), F32),
        scratch_shapes=[pltpu.VMEM((N_DEV, rows, lanes), F32), pltpu.SemaphoreType.DMA((7,)), pltpu.SemaphoreType.DMA((7,))],
        compiler_params=pltpu.CompilerParams(vmem_limit_bytes=VMEM_LIMIT),
    )(pack)


def _adamw(w, g, m, v):
    m = ADAM_B1 * m + (1.0 - ADAM_B1) * g
    v = ADAM_B2 * v + (1.0 - ADAM_B2) * jnp.square(g)
    m_hat = m / (1.0 - ADAM_B1 ** ADAM_STEP)
    v_hat = v / (1.0 - ADAM_B2 ** ADAM_STEP)
    delta = -ADAM_LR * (m_hat / (jnp.sqrt(v_hat) + ADAM_EPS) + ADAM_WD * w)
    return delta, m, v


def _adam_shard(own, parts, w, m, v, name):
    r, c = w.shape
    tr = _pick(r, (128, 64, 32, 16, 8))

    def body(own_ref, p_ref, w_ref, m_ref, v_ref, g_ref, d_ref, nm_ref, nv_ref):
        _, peers = _peers()
        g = own_ref[...].astype(F32)
        for peer in peers:
            g = g + p_ref[_slot(peer)].astype(F32)
        g_ref[...] = g
        d_ref[...], nm_ref[...], nv_ref[...] = _adamw(w_ref[...], g, m_ref[...], v_ref[...])

    spec = pl.BlockSpec((tr, c), lambda i: (i, 0))
    shp = jax.ShapeDtypeStruct((r, c), F32)
    return pl.pallas_call(
        body, name=name, grid=(r // tr,),
        in_specs=[spec, pl.BlockSpec((N_DEV, tr, c), lambda i: (0, i, 0)), spec, spec, spec], out_specs=[spec] * 4,
        out_shape=[shp] * 4, compiler_params=_params(("parallel",)),
    )(own, parts, w, m, v)


def _adam_small(w, g, m, v):
    r, c = w.shape

    def body(w_ref, g_ref, m_ref, v_ref, d_ref, nm_ref, nv_ref):
        d_ref[...], nm_ref[...], nv_ref[...] = _adamw(w_ref[...], g_ref[...], m_ref[...], v_ref[...])

    shp = jax.ShapeDtypeStruct((r, c), F32)
    return pl.pallas_call(body, name="adam_small", out_shape=[shp] * 3)(w, g, m, v)


def _pack(arrs):
    rows = []
    for a in arrs:
        flat = a.reshape(-1).astype(F32)
        rows.append(jnp.pad(flat, (0, (-flat.shape[0]) % (8 * LANES))).reshape(-1, LANES))
    return jnp.concatenate(rows, axis=0)


def _unpack(pack, shapes):
    out, r = [], 0
    for s in shapes:
        size = math.prod(s)
        nr = -(-size // (8 * LANES)) * 8
        out.append(pack[r:r + nr].reshape(-1)[:size].reshape(s))
        r += nr
    return out


def _group_lanes(t):
    lead = t.shape[:-1]
    t = t.reshape(lead + (SSD_GROUPS, SSD_HPG))
    t = jnp.pad(t, [(0, 0)] * len(lead) + [(0, 0), (0, LANES - SSD_HPG)])
    return t.reshape(lead + (SSD_GROUPS * LANES,))


def _ungroup_lanes(t):
    lead = t.shape[:-1]
    return t.reshape(lead + (SSD_GROUPS, LANES))[..., :SSD_HPG].reshape(lead + (SSD_HEADS,))


def kernel(x, meta_tokens, mix_norm_w, w_in, ssd_conv_w, ssd_conv_b, ssd_dt_bias, ssd_A_log, ssd_D, ssd_norm_w, w_branch_ssd, w_branch_ret, w_out, ffn_norm_w, w_up, ffn_conv_w, ffn_conv_b, w_down, final_norm_w, loss_target, m_meta_tokens, m_mix_norm_w, m_w_in, m_ssd_conv_w, m_ssd_conv_b, m_ssd_dt_bias, m_ssd_A_log, m_ssd_D, m_ssd_norm_w, m_w_branch_ssd, m_w_branch_ret, m_w_out, m_ffn_norm_w, m_w_up, m_ffn_conv_w, m_ffn_conv_b, m_w_down, m_final_norm_w, v_meta_tokens, v_mix_norm_w, v_w_in, v_ssd_conv_w, v_ssd_conv_b, v_ssd_dt_bias, v_ssd_A_log, v_ssd_D, v_ssd_norm_w, v_w_branch_ssd, v_w_branch_ret, v_w_out, v_ffn_norm_w, v_w_up, v_ffn_conv_w, v_ffn_conv_b, v_w_down, v_final_norm_w):
    seq, d = x.shape[1], x.shape[2]
    rows = seq + PAD_ROWS
    tm = _row_tile(rows)
    me = _slot(_place())
    d_ff = w_down.shape[1] * N_DEV

    big = [w_in[0], w_branch_ssd[0], w_branch_ret[0], w_out[0], w_up[0], w_down[0]]
    first = _allgather([w_in[0].astype(COMM_DTYPE), meta_tokens, ssd_conv_w[0], ffn_conv_w[0]], "gather_first")
    rest_src = [b.astype(COMM_DTYPE) for b in big[1:]]
    rest_handle, rest_token = _exchange_start(rest_src, False, "gather_rest_start")
    cols = lambda t: jnp.transpose(t, (1, 0, 2)).reshape(t.shape[1], -1)
    rws = lambda t: t.reshape(-1, t.shape[2])
    w_in_f, conv_w, fconv_w = cols(first[0]), cols(first[2]), cols(first[3])
    meta_full = cols(first[1]) + rest_token[0, 0]
    widths = [SSD_D_INNER, SSD_CONV_DIM, SSD_HEADS, RET_HEADS * RET_QK, RET_HEADS * RET_QK, RET_HEADS * RET_V,
              RET_HEADS * RET_V, d, d]
    offs = [0]
    for wd in widths:
        offs.append(offs[-1] + wd)
    seg = [w_in_f[:, offs[i]:offs[i + 1]] for i in range(9)]
    seg[2] = _group_lanes(seg[2])
    w_in_p = jnp.concatenate(seg, axis=1)
    widths_p = [s.shape[1] for s in seg]
    offs_p = [0]
    for wd in widths_p:
        offs_p.append(offs_p[-1] + wd)
    in_p = offs_p[-1]

    h0 = jnp.concatenate([jnp.zeros((FRONT, d), F32), meta_full, x[0]], axis=0)
    u1 = _rms_fwd(h0, mix_norm_w, "rms1")
    seg_dtype = [ACT_DTYPE, ACT_DTYPE, F32] + [ACT_DTYPE] * 6
    proj = [_mm(u1, seg[i], mode="nn", out_dtype=seg_dtype[i], tm=tm, tn=_pick(widths_p[i], (1024, 512)), tk=d,
                name=f"in_proj_{i}") for i in range(9)]
    z, xbc, dtr, q, k, v, g, gs, gr = proj
    xbc_c = _ssd_conv_fwd(xbc, conv_w, ssd_conv_b)
    bias_p, alog_p, dsk_p = _group_lanes(ssd_dt_bias), _group_lanes(ssd_A_log), _group_lanes(ssd_D)
    y_ssd, ssd_states = _ssd_fwd(xbc_c, dtr, z, bias_p, alog_p, dsk_p, ssd_norm_w)
    pos = jnp.arange(rows, dtype=F32) - float(FRONT)
    inv_freq = ROPE_BASE ** (-jnp.linspace(0.0, 1.0, RET_QK // 2, dtype=F32))
    ang = pos[:, None] * inv_freq[None, :]
    cos, sin = jnp.cos(ang), jnp.sin(ang)
    lgam = jnp.log(1.0 - 2.0 ** (-5.0 - jnp.arange(RET_HEADS, dtype=F32)))
    lgam = jnp.broadcast_to(lgam[:, None, None], (RET_HEADS, 8, LANES))
    y_ret, ret_states = _ret_fwd(q, k, v, g, cos, sin, lgam)
    rest_own, rest = _exchange_wait(rest_handle, False, y_ret, "gather_rest_wait")
    rest = [lax.dynamic_update_index_in_dim(land, own, me, 0) for land, own in zip(rest, rest_own, strict=True)]
    w_bs, w_br, w_o, w_up_f, w_dn = rws(rest[0]), rws(rest[1]), rws(rest[2]), cols(rest[3]), rws(rest[4])
    bs = _mm(y_ssd, w_bs, mode="nn", out_dtype=ACT_DTYPE, tm=tm, tn=d, tk=SSD_D_INNER, name="branch_ssd")
    br = _mm(y_ret, w_br, mode="nn", out_dtype=ACT_DTYPE, tm=tm, tn=d, tk=RET_HEADS * RET_V, name="branch_ret")
    merged = _merge_fwd(bs, br, gs, gr)
    h1 = _mm(merged, w_o, mode="nn", out_dtype=F32, tm=tm, tn=d, tk=d, name="out_proj", add=h0)
    u2 = _rms_fwd(h1, ffn_norm_w, "rms2")
    up = _mm(u2, w_up_f, mode="nn", out_dtype=ACT_DTYPE, tm=tm, tn=_pick(2 * d_ff, (512,)), tk=d, name="ffn_up")
    act = _ffn_conv_fwd(up, fconv_w, ffn_conv_b)
    h2 = _mm(act, w_dn, mode="nn", out_dtype=F32, tm=tm, tn=d, tk=d_ff, name="ffn_down", add=h1)
    tgt = jnp.pad(loss_target[0], ((PAD_ROWS, 0), (0, 0)))
    dh2, loss_acc, g_final = _loss_head(h2, tgt, final_norm_w.reshape(1, d))

    tff = _pick(d_ff, (1408, 256))
    tkr = _pick(rows, (1664, 128))
    cparts = lambda t, n=N_DEV: jnp.transpose(t.reshape(t.shape[0], n, -1), (1, 0, 2)).astype(COMM_DTYPE)
    rparts = lambda t: t.reshape(N_DEV, -1, t.shape[1]).astype(COMM_DTYPE)
    d_act = _mm(dh2, w_dn, mode="nt", out_dtype=ACT_DTYPE, tm=tm, tn=tff, tk=d, name="d_act")
    g_w_dn = _mm(act, dh2, mode="tn", out_dtype=F32, tm=tff, tn=d, tk=tkr, name="g_w_down")
    c_dn = [rparts(g_w_dn)]
    h_dn, t_dn = _exchange_start(c_dn, True, "scatter_down_start")
    d_up_g, d_up_v, g_fcw_g, g_fcb_g, g_fcw_v, g_fcb_v = _ffn_conv_bwd(up, d_act, fconv_w, ffn_conv_b + t_dn[0, 0])
    g_fconv_w = jnp.concatenate([g_fcw_g, g_fcw_v], axis=1)
    g_fconv_b = jnp.concatenate([g_fcb_g, g_fcb_v], axis=1)
    g_w_up_g = _mm(u2, d_up_g, mode="tn", out_dtype=F32, tm=d, tn=tff, tk=tkr, name="g_w_up_gate")
    g_w_up_v = _mm(u2, d_up_v, mode="tn", out_dtype=F32, tm=d, tn=tff, tk=tkr, name="g_w_up_value")
    c_up = [jnp.concatenate([cparts(g_w_up_g, N_DEV // 2), cparts(g_w_up_v, N_DEV // 2)], axis=0)]
    h_up, t_up = _exchange_start(c_up, True, "scatter_up_start")
    du2 = _mm(d_up_g, w_up_f[:, :d_ff], mode="nt", out_dtype=F32, tm=tm, tn=d, tk=d_ff, name="d_u2_gate", after=t_up)
    du2 = _mm(d_up_v, w_up_f[:, d_ff:], mode="nt", out_dtype=F32, tm=tm, tn=d, tk=d_ff, name="d_u2_value", add=du2)
    dh1, g_ffn_norm = _rms_bwd(du2, h1, ffn_norm_w, dh2, "rms2_bwd")
    d_merged = _mm(dh1, w_o, mode="nt", out_dtype=F32, tm=tm, tn=d, tk=d, name="d_merged")
    g_w_o = _mm(merged, dh1, mode="tn", out_dtype=F32, tm=d, tn=d, tk=tkr, name="g_w_out")
    d_bs, d_br, d_gs, d_gr = _merge_bwd(d_merged, bs, br, gs, gr)
    d_yssd = _mm(d_bs, w_bs, mode="nt", out_dtype=ACT_DTYPE, tm=tm, tn=1024, tk=d, name="d_y_ssd")
    g_w_bs = _mm(y_ssd, d_bs, mode="tn", out_dtype=F32, tm=1024, tn=d, tk=tkr, name="g_w_branch_ssd")
    d_yret = _mm(d_br, w_br, mode="nt", out_dtype=ACT_DTYPE, tm=tm, tn=1024, tk=d, name="d_y_ret")
    g_w_br = _mm(y_ret, d_br, mode="tn", out_dtype=F32, tm=1024, tn=d, tk=tkr, name="g_w_branch_ret")
    c_mid = [rparts(g_w_bs), rparts(g_w_br), rparts(g_w_o)]
    h_mid, t_mid = _exchange_start(c_mid, True, "scatter_mid_start")
    dxs, d_bm, d_cm, d_dtr, d_z, g_bias_p, g_alog_p, g_dsk_p, g_nw = _ssd_bwd(
        xbc_c, dtr, z, bias_p, alog_p, dsk_p, ssd_norm_w + t_mid[0, 0], ssd_states, d_yssd)
    d_xbc, g_conv_w, g_conv_b = _ssd_conv_bwd(xbc, jnp.concatenate([dxs, d_bm, d_cm], axis=1), conv_w, ssd_conv_b)
    d_q, d_k, d_v, d_g = _ret_bwd(q, k, v, g, cos, sin, lgam, ret_states, d_yret)
    dproj = jnp.concatenate([d_z, d_xbc, d_dtr.astype(ACT_DTYPE), d_q, d_k, d_v, d_g, d_gs, d_gr], axis=1)
    g_w_in_p = _mm(u1, dproj, mode="tn", out_dtype=F32, tm=d, tn=_pick(in_p, (768, 512)), tk=tkr, name="g_w_in")
    gseg = [g_w_in_p[:, offs_p[i]:offs_p[i + 1]] for i in range(9)]
    gseg[2] = _ungroup_lanes(gseg[2])
    c_in = [cparts(jnp.concatenate(gseg, axis=1))]
    h_in, t_in = _exchange_start(c_in, True, "scatter_in_start")
    du1 = _mm(dproj, w_in_p, mode="nt", out_dtype=F32, tm=tm, tn=d, tk=_pick(in_p, (1536, 512)), name="d_u1", after=t_in)
    dh0, g_mix_norm = _rms_bwd(du1, h0, mix_norm_w, dh1, "rms1_bwd")
    grad_x = dh0[PAD_ROWS:][None]

    landed = {}
    for key, handle, names in (("in", h_in, ["w_in"]), ("mid", h_mid, ["w_branch_ssd", "w_branch_ret", "w_out"]),
                               ("up", h_up, ["w_up"]), ("down", h_dn, ["w_down"])):
        srcs, lands = _exchange_wait(handle, True, dh0, f"scatter_{key}_wait")
        for nm, land, src in zip(names, lands, srcs, strict=True):
            landed[nm] = (lax.dynamic_index_in_dim(src, me, 0, keepdims=False), land)
    big_m = [m_w_in, m_w_branch_ssd, m_w_branch_ret, m_w_out, m_w_up, m_w_down]
    big_v = [v_w_in, v_w_branch_ssd, v_w_branch_ret, v_w_out, v_w_up, v_w_down]
    big_names = ["w_in", "w_branch_ssd", "w_branch_ret", "w_out", "w_up", "w_down"]
    big_out = {}
    for nm, w, m, v_ in zip(big_names, big, big_m, big_v, strict=True):
        big_out[nm] = [t[None] for t in _adam_shard(*landed[nm], w, m[0], v_[0], "adam_" + nm)]

    small_g = [dh0[FRONT:PAD_ROWS], g_mix_norm, g_conv_w, g_conv_b, _ungroup_lanes(g_bias_p), _ungroup_lanes(g_alog_p),
               _ungroup_lanes(g_dsk_p), g_nw, g_ffn_norm, g_fconv_w, g_fconv_b, g_final, loss_acc[0:1, 0:1]]
    total = _unpack(_allreduce_small(_pack(small_g)), [t.shape for t in small_g])
    loss = total[12].reshape(())
    shard = lambda t, width: lax.dynamic_slice_in_dim(t, me * width, width, axis=1)
    small_names = ["meta_tokens", "mix_norm_w", "ssd_conv_w", "ssd_conv_b", "ssd_dt_bias", "ssd_A_log", "ssd_D", "ssd_norm_w",
                   "ffn_norm_w", "ffn_conv_w", "ffn_conv_b", "final_norm_w"]
    small_w = [meta_tokens, mix_norm_w, ssd_conv_w, ssd_conv_b, ssd_dt_bias, ssd_A_log, ssd_D, ssd_norm_w, ffn_norm_w,
               ffn_conv_w, ffn_conv_b, final_norm_w]
    small_m = [m_meta_tokens, m_mix_norm_w, m_ssd_conv_w, m_ssd_conv_b, m_ssd_dt_bias, m_ssd_A_log, m_ssd_D, m_ssd_norm_w,
               m_ffn_norm_w, m_ffn_conv_w, m_ffn_conv_b, m_final_norm_w]
    small_v = [v_meta_tokens, v_mix_norm_w, v_ssd_conv_w, v_ssd_conv_b, v_ssd_dt_bias, v_ssd_A_log, v_ssd_D, v_ssd_norm_w,
               v_ffn_norm_w, v_ffn_conv_w, v_ffn_conv_b, v_final_norm_w]
    grads = total[:12]
    grads[0] = shard(grads[0], meta_tokens.shape[1])
    grads[2] = shard(grads[2], ssd_conv_w.shape[2])
    grads[9] = shard(grads[9], ffn_conv_w.shape[2])
    grads = [t.reshape(w.shape) for t, w in zip(grads, small_w, strict=True)]
    shapes = [w.shape for w in small_w]
    upd = _adam_small(_pack(small_w), _pack(grads), _pack(small_m), _pack(small_v))
    small_out = {nm: [gr_] + [u[i] for u in (_unpack(t, shapes) for t in upd)]
                 for i, (nm, gr_) in enumerate(zip(small_names, grads, strict=True))}

    order = ["meta_tokens", "mix_norm_w", "w_in", "ssd_conv_w", "ssd_conv_b", "ssd_dt_bias", "ssd_A_log", "ssd_D", "ssd_norm_w",
             "w_branch_ssd", "w_branch_ret", "w_out", "ffn_norm_w", "w_up", "ffn_conv_w", "ffn_conv_b", "w_down", "final_norm_w"]
    res = {**big_out, **small_out}
    return (loss, grad_x, *[res[nm][0] for nm in order], *[res[nm][1] for nm in order], *[res[nm][2] for nm in order],
            *[res[nm][3] for nm in order])
```

```python
import functools
import math

import jax
import jax.numpy as jnp
from jax import lax
from jax.experimental import pallas as pl
from jax.experimental.pallas import tpu as pltpu

F32 = jnp.float32
MXU_DTYPE = jnp.bfloat16
ACT_DTYPE = jnp.bfloat16
COMM_DTYPE = jnp.bfloat16

N_META = 16
CHUNK = 128
FRONT = CHUNK - N_META
PAD_ROWS = FRONT + N_META
EPS = 1e-6
N_DEV = 8

SSD_D_INNER = 2048
SSD_HEAD_DIM = 64
SSD_HEADS = 32
SSD_GROUPS = 4
SSD_HPG = SSD_HEADS // SSD_GROUPS
SSD_STATE = 128
SSD_CONV = 4
SSD_CONV_DIM = SSD_D_INNER + 2 * SSD_GROUPS * SSD_STATE
SSD_GW = SSD_D_INNER // SSD_GROUPS
RET_HEADS = 4
RET_QK = 256
RET_V = 512
ROPE_BASE = 10000.0
FFN_CONV = 3
HALO = 16
LANES = 128

ADAM_LR = 0.001
ADAM_B1 = 0.9
ADAM_B2 = 0.999
ADAM_EPS = 1e-08
ADAM_WD = 0.01
ADAM_STEP = 10

VMEM_LIMIT = 56 * 1024 * 1024
MESH = pl.DeviceIdType.MESH

NN = (((1,), (0,)), ((), ()))
NT = (((1,), (1,)), ((), ()))
TN = (((0,), (0,)), ((), ()))


def _params(sem):
    return pltpu.CompilerParams(dimension_semantics=sem, vmem_limit_bytes=VMEM_LIMIT)


def _mxu(a, b, dn):
    return lax.dot_general(a.astype(MXU_DTYPE), b.astype(MXU_DTYPE), dn, preferred_element_type=F32)


@functools.partial(jax.custom_vjp, nondiff_argnums=(2,))
def _dot(a, b, dn=NN):
    return _mxu(a, b, dn)


def _dot_fwd(a, b, dn):
    return _mxu(a, b, dn), (a, b)


def _dot_bwd(dn, res, g):
    a, b = res
    if dn == NN:
        return _mxu(g, b, NT), _mxu(a, g, TN)
    if dn == NT:
        return _mxu(g, b, NN), _mxu(g, a, TN)
    assert dn == TN
    return _mxu(b, g, NT), _mxu(a, g, NN)


_dot.defvjp(_dot_fwd, _dot_bwd)


def _silu(x):
    return x * jax.nn.sigmoid(x)


def _dsilu(x):
    s = jax.nn.sigmoid(x)
    return s * (1.0 + x * (1.0 - s))


def _row_tile(rows):
    return 640 if rows % 640 == 0 else 128


def _mm(a, b, *, mode, out_dtype, tm, tn, tk, name, add=None, after=None):
    if mode == "nn":
        (m, k), (k2, n) = a.shape, b.shape
    elif mode == "nt":
        (m, k), (n, k2) = a.shape, b.shape
    else:
        (k, m), (k2, n) = a.shape, b.shape
    assert k == k2 and m % tm == 0 and n % tn == 0 and k % tk == 0, (name, a.shape, b.shape, tm, tn, tk)
    nk = k // tk
    dn = {"nn": NN, "nt": NT, "tn": TN}[mode]
    has_add = add is not None
    n_in = 2 + has_add + (after is not None)

    def body(*refs):
        a_ref, b_ref = refs[0], refs[1]
        add_ref = refs[2] if has_add else None
        o_ref = refs[n_in]
        p = _dot(a_ref[...], b_ref[...], dn)
        if nk == 1:
            if has_add:
                p = p + add_ref[...]
            o_ref[...] = p.astype(out_dtype)
        else:
            acc_ref = refs[n_in + 1]
            kk = pl.program_id(2)

            @pl.when(kk == 0)
            def _():
                acc_ref[...] = p

            @pl.when(kk > 0)
            def _():
                acc_ref[...] += p

            @pl.when(kk == nk - 1)
            def _():
                r = acc_ref[...]
                if has_add:
                    r = r + add_ref[...]
                o_ref[...] = r.astype(out_dtype)

    if mode == "tn":
        a_spec = pl.BlockSpec((tk, tm), lambda j, i, kk: (kk, i))
    else:
        a_spec = pl.BlockSpec((tm, tk), lambda j, i, kk: (i, kk))
    if mode == "nt":
        b_spec = pl.BlockSpec((tn, tk), lambda j, i, kk: (j, kk))
    else:
        b_spec = pl.BlockSpec((tk, tn), lambda j, i, kk: (kk, j))
    o_spec = pl.BlockSpec((tm, tn), lambda j, i, kk: (i, j))
    in_specs = [a_spec, b_spec] + ([o_spec] if has_add else [])
    args = (a, b) + ((add,) if has_add else ())
    if after is not None:
        in_specs.append(pl.BlockSpec(memory_space=pl.ANY))
        args += (after,)
    return pl.pallas_call(
        body, name=name, grid=(n // tn, m // tm, nk), in_specs=in_specs, out_specs=o_spec,
        out_shape=jax.ShapeDtypeStruct((m, n), out_dtype),
        scratch_shapes=[pltpu.VMEM((tm, tn), F32)] if nk > 1 else [],
        compiler_params=_params(("parallel", "parallel", "arbitrary")),
    )(*args)


def _pick(n, cands):
    for c in cands:
        if n % c == 0:
            return c
    return n


def _rms_fwd(h, w, name):
    rows, d = h.shape
    tm = _row_tile(rows)

    def body(h_ref, w_ref, u_ref):
        x = h_ref[...]
        r = lax.rsqrt(jnp.mean(x * x, axis=-1, keepdims=True) + EPS)
        u_ref[...] = (x * r * w_ref[...]).astype(ACT_DTYPE)

    return pl.pallas_call(
        body, name=name, grid=(rows // tm,),
        in_specs=[pl.BlockSpec((tm, d), lambda i: (i, 0)), pl.BlockSpec((1, d), lambda i: (0, 0))],
        out_specs=pl.BlockSpec((tm, d), lambda i: (i, 0)),
        out_shape=jax.ShapeDtypeStruct((rows, d), ACT_DTYPE),
        compiler_params=_params(("parallel",)),
    )(h, w)


def _rms_bwd(du, h, w, dres, name):
    rows, d = h.shape
    tm = _row_tile(rows)

    def body(du_ref, h_ref, w_ref, dres_ref, dh_ref, dw_ref):
        x = h_ref[...]
        dy = du_ref[...].astype(F32)
        r = lax.rsqrt(jnp.mean(x * x, axis=-1, keepdims=True) + EPS)
        xhat = x * r
        dxn = dy * w_ref[...]
        dx = r * (dxn - xhat * jnp.mean(dxn * xhat, axis=-1, keepdims=True))
        dh_ref[...] = dres_ref[...] + dx

        @pl.when(pl.program_id(0) == 0)
        def _():
            dw_ref[...] = jnp.zeros_like(dw_ref)

        dw_ref[...] += jnp.sum(dy * xhat, axis=0, keepdims=True)

    return pl.pallas_call(
        body, name=name, grid=(rows // tm,),
        in_specs=[pl.BlockSpec((tm, d), lambda i: (i, 0)), pl.BlockSpec((tm, d), lambda i: (i, 0)),
                  pl.BlockSpec((1, d), lambda i: (0, 0)), pl.BlockSpec((tm, d), lambda i: (i, 0))],
        out_specs=[pl.BlockSpec((tm, d), lambda i: (i, 0)), pl.BlockSpec((1, d), lambda i: (0, 0))],
        out_shape=[jax.ShapeDtypeStruct((rows, d), F32), jax.ShapeDtypeStruct((1, d), F32)],
        compiler_params=_params(("arbitrary",)),
    )(du, h, w, dres)


def _loss_head(h2, tgt, w):
    rows, d = h2.shape
    tm = _row_tile(rows)

    def body(h_ref, t_ref, w_ref, dh_ref, loss_ref, dw_ref):
        i = pl.program_id(0)
        x = h_ref[...]
        r = lax.rsqrt(jnp.mean(x * x, axis=-1, keepdims=True) + EPS)
        xhat = x * r
        wv = w_ref[...]
        row = i * tm + lax.broadcasted_iota(jnp.int32, (tm, 1), 0)
        live = row >= PAD_ROWS
        diff = jnp.where(live, xhat * wv - t_ref[...], 0.0)
        dy = diff * (1.0 / d)
        dxn = dy * wv
        dh_ref[...] = r * (dxn - xhat * jnp.mean(dxn * xhat, axis=-1, keepdims=True))

        @pl.when(i == 0)
        def _():
            loss_ref[...] = jnp.zeros_like(loss_ref)
            dw_ref[...] = jnp.zeros_like(dw_ref)

        loss_ref[...] += 0.5 * jnp.sum(jnp.mean(diff * diff, axis=-1, keepdims=True))
        dw_ref[...] += jnp.sum(dy * xhat, axis=0, keepdims=True)

    return pl.pallas_call(
        body, name="loss_head", grid=(rows // tm,),
        in_specs=[pl.BlockSpec((tm, d), lambda i: (i, 0)), pl.BlockSpec((tm, d), lambda i: (i, 0)),
                  pl.BlockSpec((1, d), lambda i: (0, 0))],
        out_specs=[pl.BlockSpec((tm, d), lambda i: (i, 0)), pl.BlockSpec((8, LANES), lambda i: (0, 0)),
                   pl.BlockSpec((1, d), lambda i: (0, 0))],
        out_shape=[jax.ShapeDtypeStruct((rows, d), F32), jax.ShapeDtypeStruct((8, LANES), F32),
                   jax.ShapeDtypeStruct((1, d), F32)],
        compiler_params=_params(("arbitrary",)),
    )(h2, tgt, w)


def _prev_halo_spec(tm, width, col):
    return pl.BlockSpec((HALO, width), lambda j, i: (jnp.maximum(i * (tm // HALO) - 1, 0), col(j)))


def _next_halo_spec(tm, rows, width, col):
    last = rows // HALO - 1
    return pl.BlockSpec((HALO, width), lambda j, i: (jnp.minimum((i + 1) * (tm // HALO), last), col(j)))


def _conv_taps(cat, w_ref, b_ref, kw):
    acc = b_ref[...] + w_ref[kw - 1:kw, :] * cat
    for s in range(1, kw):
        acc = acc + w_ref[kw - 1 - s:kw - s, :] * pltpu.roll(cat, s, 0)
    return acc


def _conv_back(dpre, w_ref, kw):
    n = dpre.shape[0]
    acc = w_ref[kw - 1:kw, :] * dpre
    for s in range(1, kw):
        acc = acc + w_ref[kw - 1 - s:kw - s, :] * pltpu.roll(dpre, n - s, 0)
    return acc


def _ssd_conv_fwd(xbc, w, b):
    rows, width = xbc.shape
    tm, tc = _row_tile(rows), 512

    def body(x_ref, xp_ref, w_ref, b_ref, o_ref):
        i = pl.program_id(1)
        xp = jnp.where(i == 0, 0.0, xp_ref[...].astype(F32))
        cat = jnp.concatenate([xp, x_ref[...].astype(F32)], axis=0)
        pre = _conv_taps(cat, w_ref, b_ref, SSD_CONV)[HALO:]
        row = i * tm + lax.broadcasted_iota(jnp.int32, (tm, 1), 0)
        o_ref[...] = jnp.where(row >= FRONT, _silu(pre), 0.0).astype(ACT_DTYPE)

    main = pl.BlockSpec((tm, tc), lambda j, i: (i, j))
    par = lambda r: pl.BlockSpec((r, tc), lambda j, i: (0, j))
    return pl.pallas_call(
        body, name="ssd_conv_fwd", grid=(width // tc, rows // tm),
        in_specs=[main, _prev_halo_spec(tm, tc, lambda j: j), par(SSD_CONV), par(1)],
        out_specs=main, out_shape=jax.ShapeDtypeStruct((rows, width), ACT_DTYPE),
        compiler_params=_params(("parallel", "parallel")),
    )(xbc, xbc, w, b)


def _ssd_conv_bwd(xbc, dxc, w, b):
    rows, width = xbc.shape
    tm, tc = _row_tile(rows), 512
    kw = SSD_CONV

    def body(x_ref, xp_ref, xn_ref, d_ref, dn_ref, w_ref, b_ref, dx_ref, dw_ref, db_ref):
        i = pl.program_id(1)
        xp = jnp.where(i == 0, 0.0, xp_ref[...].astype(F32))
        cat = jnp.concatenate([xp, x_ref[...].astype(F32), xn_ref[...].astype(F32)], axis=0)
        pre = _conv_taps(cat, w_ref, b_ref, kw)[HALO:]
        row = i * tm + lax.broadcasted_iota(jnp.int32, (tm + HALO, 1), 0)
        live = (row >= FRONT) & (row < rows)
        dout = jnp.concatenate([d_ref[...].astype(F32), dn_ref[...].astype(F32)], axis=0)
        dpre = jnp.where(live, dout * _dsilu(pre), 0.0)
        dx_ref[...] = _conv_back(dpre, w_ref, kw)[:tm].astype(ACT_DTYPE)

        @pl.when(i == 0)
        def _():
            dw_ref[...] = jnp.zeros_like(dw_ref)
            db_ref[...] = jnp.zeros_like(db_ref)

        dmain = dpre[:tm]
        db_ref[...] += jnp.sum(dmain, axis=0, keepdims=True)
        for k in range(kw):
            s = kw - 1 - k
            xs = (pltpu.roll(cat, s, 0) if s else cat)[HALO:HALO + tm]
            dw_ref[k:k + 1, :] += jnp.sum(dmain * xs, axis=0, keepdims=True)

    main = pl.BlockSpec((tm, tc), lambda j, i: (i, j))
    par = lambda r: pl.BlockSpec((r, tc), lambda j, i: (0, j))
    col = lambda j: j
    return pl.pallas_call(
        body, name="ssd_conv_bwd", grid=(width // tc, rows // tm),
        in_specs=[main, _prev_halo_spec(tm, tc, col), _next_halo_spec(tm, rows, tc, col),
                  main, _next_halo_spec(tm, rows, tc, col), par(kw), par(1)],
        out_specs=[main, par(kw), par(1)],
        out_shape=[jax.ShapeDtypeStruct((rows, width), ACT_DTYPE), jax.ShapeDtypeStruct((kw, width), F32),
                   jax.ShapeDtypeStruct((1, width), F32)],
        compiler_params=_params(("parallel", "arbitrary")),
    )(xbc, xbc, xbc, dxc, dxc, w, b)


def _ffn_conv_fwd(up, w, b):
    rows, width = up.shape
    dff = width // 2
    tm, tc = _row_tile(rows), _pick(dff, (256, 128))
    nb = dff // tc
    kw = FFN_CONV

    def body(g_ref, gp_ref, v_ref, vp_ref, wg_ref, bg_ref, wv_ref, bv_ref, o_ref):
        i = pl.program_id(1)

        def pre(x_ref, xp_ref, w_ref, b_ref):
            xp = jnp.where(i == 0, 0.0, xp_ref[...].astype(F32))
            cat = jnp.concatenate([xp, x_ref[...].astype(F32)], axis=0)
            return _conv_taps(cat, w_ref, b_ref, kw)[HALO:]

        o_ref[...] = (_silu(pre(g_ref, gp_ref, wg_ref, bg_ref)) * pre(v_ref, vp_ref, wv_ref, bv_ref)).astype(ACT_DTYPE)

    gcol, vcol = (lambda j: j), (lambda j: j + nb)
    main = lambda col: pl.BlockSpec((tm, tc), lambda j, i: (i, col(j)))
    par = lambda r, col: pl.BlockSpec((r, tc), lambda j, i: (0, col(j)))
    return pl.pallas_call(
        body, name="ffn_conv_fwd", grid=(nb, rows // tm),
        in_specs=[main(gcol), _prev_halo_spec(tm, tc, gcol), main(vcol), _prev_halo_spec(tm, tc, vcol),
                  par(kw, gcol), par(1, gcol), par(kw, vcol), par(1, vcol)],
        out_specs=pl.BlockSpec((tm, tc), lambda j, i: (i, j)),
        out_shape=jax.ShapeDtypeStruct((rows, dff), ACT_DTYPE),
        compiler_params=_params(("parallel", "parallel")),
    )(up, up, up, up, w, b, w, b)


def _ffn_conv_bwd(up, dact, w, b):
    rows, width = up.shape
    dff = width // 2
    tm, tc = _row_tile(rows), _pick(dff, (256, 128))
    nb = dff // tc
    kw = FFN_CONV

    def body(g_ref, gp_ref, gn_ref, v_ref, vp_ref, vn_ref, d_ref, dn_ref, wg_ref, bg_ref, wv_ref, bv_ref,
             dxg_ref, dxv_ref, dwg_ref, dbg_ref, dwv_ref, dbv_ref):
        i = pl.program_id(1)

        def shifted(x_ref, xp_ref, xn_ref):
            xp = jnp.where(i == 0, 0.0, xp_ref[...].astype(F32))
            cat = jnp.concatenate([xp, x_ref[...].astype(F32), xn_ref[...].astype(F32)], axis=0)
            return [cat] + [pltpu.roll(cat, s, 0) for s in range(1, kw)]

        def taps(sh, w_ref, b_ref):
            acc = b_ref[...] + w_ref[kw - 1:kw, :] * sh[0]
            for s in range(1, kw):
                acc = acc + w_ref[kw - 1 - s:kw - s, :] * sh[s]
            return acc[HALO:]

        sh_g, sh_v = shifted(g_ref, gp_ref, gn_ref), shifted(v_ref, vp_ref, vn_ref)
        ag, av = taps(sh_g, wg_ref, bg_ref), taps(sh_v, wv_ref, bv_ref)
        row = i * tm + lax.broadcasted_iota(jnp.int32, (tm + HALO, 1), 0)
        dout = jnp.concatenate([d_ref[...].astype(F32), dn_ref[...].astype(F32)], axis=0)
        dout = jnp.where(row < rows, dout, 0.0)
        s = jax.nn.sigmoid(ag)
        silu = ag * s
        dpre_v = dout * silu
        dpre_g = dout * av * (s + silu * (1.0 - s))

        @pl.when(i == 0)
        def _():
            for r in (dwg_ref, dbg_ref, dwv_ref, dbv_ref):
                r[...] = jnp.zeros_like(r)

        for dpre, sh, w_ref, dx_ref, dw_ref, db_ref in ((dpre_g, sh_g, wg_ref, dxg_ref, dwg_ref, dbg_ref),
                                                          (dpre_v, sh_v, wv_ref, dxv_ref, dwv_ref, dbv_ref)):
            dx_ref[...] = _conv_back(dpre, w_ref, kw)[:tm].astype(ACT_DTYPE)
            dmain = dpre[:tm]
            db_ref[...] += jnp.sum(dmain, axis=0, keepdims=True)
            for k in range(kw):
                dw_ref[k:k + 1, :] += jnp.sum(dmain * sh[kw - 1 - k][HALO:HALO + tm], axis=0, keepdims=True)

    gcol, vcol = (lambda j: j), (lambda j: j + nb)
    main = lambda col: pl.BlockSpec((tm, tc), lambda j, i: (i, col(j)))
    par = lambda r, col: pl.BlockSpec((r, tc), lambda j, i: (0, col(j)))
    halos = lambda col: [_prev_halo_spec(tm, tc, col), _next_halo_spec(tm, rows, tc, col)]
    act_shape = jax.ShapeDtypeStruct((rows, dff), ACT_DTYPE)
    par_shapes = [jax.ShapeDtypeStruct((kw, dff), F32), jax.ShapeDtypeStruct((1, dff), F32)]
    return pl.pallas_call(
        body, name="ffn_conv_bwd", grid=(nb, rows // tm),
        in_specs=[main(gcol)] + halos(gcol) + [main(vcol)] + halos(vcol) + [main(gcol), _next_halo_spec(tm, rows, tc, gcol),
                  par(kw, gcol), par(1, gcol), par(kw, vcol), par(1, vcol)],
        out_specs=[main(gcol), main(gcol), par(kw, gcol), par(1, gcol), par(kw, gcol), par(1, gcol)],
        out_shape=[act_shape, act_shape] + par_shapes + par_shapes,
        compiler_params=_params(("parallel", "arbitrary")),
    )(up, up, up, up, up, up, dact, dact, w, b, w, b)


def _ssd_scalars(dtr, dt_bias, a_log, live):
    q = CHUNK
    pre = dtr + dt_bias
    dt = jnp.where(live, jax.nn.softplus(pre), 0.0)
    a_neg = -jnp.exp(a_log)
    li = lax.broadcasted_iota(jnp.int32, (q, q), 0)
    si = lax.broadcasted_iota(jnp.int32, (q, q), 1)
    causal = li >= si
    tri = jnp.where(causal, 1.0, 0.0).astype(F32)
    a_cs = jnp.dot(tri, dt * a_neg, precision=lax.Precision.HIGHEST, preferred_element_type=F32)
    return pre, dt, a_neg, a_cs, causal, tri


def _head_cols(t):
    return jnp.concatenate([jnp.broadcast_to(t[:, hh:hh + 1], (t.shape[0], SSD_HEAD_DIM)) for hh in range(SSD_HPG)], axis=1)


def _head_sums(t):
    lane = lax.broadcasted_iota(jnp.int32, (t.shape[0], LANES), 1)
    out = jnp.zeros((t.shape[0], LANES), F32)
    for hh in range(SSD_HPG):
        s = jnp.sum(t[:, hh * SSD_HEAD_DIM:(hh + 1) * SSD_HEAD_DIM], axis=1, keepdims=True)
        out = jnp.where(lane == hh, s, out)
    return out


def _ssd_scan(xs, bm, cm, dtr, prev, dt_bias, a_log, d_skip, live):
    q = CHUNK
    _, dt, _, a_cs, causal, _ = _ssd_scalars(dtr, dt_bias, a_log, live)
    a_cs_t = a_cs.T
    a_end = a_cs[q - 1:q, :]
    xdt = xs * _head_cols(dt)
    cb = _dot(cm, bm, NT)
    y = _dot(cm, prev) * _head_cols(jnp.exp(a_cs)) + _head_cols(d_skip) * xs
    new = prev * _head_cols(jnp.exp(a_end)) + _dot(bm, xdt * _head_cols(jnp.exp(a_end - a_cs)), TN)
    ys = []
    for hh in range(SSD_HPG):
        decay = jnp.exp(jnp.where(causal, a_cs[:, hh:hh + 1] - a_cs_t[hh:hh + 1, :], -jnp.inf))
        ys.append(_dot(cb * decay, xdt[:, hh * SSD_HEAD_DIM:(hh + 1) * SSD_HEAD_DIM]))
    return y + jnp.concatenate(ys, axis=1), new


def _ssd_gate(y, z, nw):
    yz = y * _silu(z)
    return yz * lax.rsqrt(jnp.mean(yz * yz, axis=-1, keepdims=True) + EPS) * nw


def _ssd_scan_bwd(xs, bm, cm, dtr, prev, dt_bias, a_log, d_skip, live, dy, dnew):
    q = CHUNK
    pre, dt, a_neg, a_cs, causal, tri = _ssd_scalars(dtr, dt_bias, a_log, live)
    a_cs_t = a_cs.T
    a_end = a_cs[q - 1:q, :]
    dt_x, e_x, w_x = _head_cols(dt), _head_cols(jnp.exp(a_cs)), _head_cols(jnp.exp(a_end - a_cs))
    g_x, d_x = _head_cols(jnp.exp(a_end)), _head_cols(d_skip)
    xdt = xs * dt_x
    u = xdt * w_x
    cb = _mxu(cm, bm, NT)
    cs = _mxu(cm, prev, NN)
    dye = dy * e_x
    dcm = _mxu(dye, prev, NT)
    dprev = _mxu(cm, dye, TN) + dnew * g_x
    dacs_x = dye * cs
    dbm = _mxu(u, dnew, NT)
    du = _mxu(bm, dnew, NN)
    dw_x = du * u
    dacs_x = dacs_x - dw_x
    dend_x = jnp.sum(dw_x + dnew * prev * g_x, axis=0, keepdims=True)
    dxdt = du * w_x
    lane = lax.broadcasted_iota(jnp.int32, (q, LANES), 1)
    sub = lax.broadcasted_iota(jnp.int32, (q, LANES), 0)
    dcb = jnp.zeros((q, q), F32)
    dacs = jnp.zeros((q, LANES), F32)
    dacs_t = jnp.zeros((q, LANES), F32)
    dxdt_h = []
    for hh in range(SSD_HPG):
        hs = slice(hh * SSD_HEAD_DIM, (hh + 1) * SSD_HEAD_DIM)
        decay = jnp.exp(jnp.where(causal, a_cs[:, hh:hh + 1] - a_cs_t[hh:hh + 1, :], -jnp.inf))
        m = cb * decay
        dm = _mxu(dy[:, hs], xdt[:, hs], NT)
        dxdt_h.append(_mxu(m, dy[:, hs], TN))
        dcb = dcb + dm * decay
        p = dm * m
        dacs = jnp.where(lane == hh, jnp.sum(p, axis=1, keepdims=True), dacs)
        dacs_t = jnp.where(sub == hh, jnp.sum(p, axis=0, keepdims=True), dacs_t)
    dcm = dcm + _mxu(dcb, bm, NN)
    dbm = dbm + _mxu(dcb, cm, TN)
    dxdt = dxdt + jnp.concatenate(dxdt_h, axis=1)
    dxs = dy * d_x + dxdt * dt_x
    dacs = dacs - dacs_t.T + _head_sums(dacs_x)
    dacs = dacs + jnp.where(sub == q - 1, _head_sums(dend_x), 0.0)
    da = lax.dot_general(tri, dacs, TN, precision=lax.Precision.HIGHEST, preferred_element_type=F32)
    ddt = _head_sums(dxdt * xs) + da * a_neg
    dalog = jnp.sum(da * dt, axis=0, keepdims=True) * a_neg
    ddtr = jnp.where(live, ddt * jax.nn.sigmoid(pre), 0.0)
    dbias = jnp.sum(ddtr, axis=0, keepdims=True)
    ddsk = _head_sums(jnp.sum(dy * xs, axis=0, keepdims=True))
    return dxs, dbm, dcm, ddtr, dprev, dbias, dalog, ddsk


def _ssd_specs(rev, nc):
    cidx = (lambda c: nc - 1 - c) if rev else (lambda c: c)
    nb_b = SSD_D_INNER // SSD_STATE
    row = lambda width, col=lambda g: g: pl.BlockSpec((CHUNK, width), lambda g, c: (cidx(c), col(g)))
    par = lambda width: pl.BlockSpec((1, width), lambda g, c: (0, g))
    state = lambda: pl.BlockSpec((1, 1, SSD_STATE, SSD_GW), lambda g, c: (cidx(c), g, 0, 0))
    xbc = [row(SSD_GW), row(SSD_STATE, lambda g: nb_b + g), row(SSD_STATE, lambda g: nb_b + SSD_GROUPS + g)]
    return cidx, row, par, state, xbc


def _ssd_fwd(xbc_c, dtr, z, dt_bias, a_log, d_skip, nw):
    rows = z.shape[0]
    nc = rows // CHUNK
    _, row, par, state, xbc = _ssd_specs(False, nc)

    def body(xs_ref, b_ref, c_ref, dt_ref, z_ref, bias_ref, al_ref, dk_ref, nw_ref, o_ref, y_ref, st_ref, carry):
        c = pl.program_id(1)

        @pl.when(c == 0)
        def _():
            carry[...] = jnp.zeros_like(carry)

        live = c * CHUNK + lax.broadcasted_iota(jnp.int32, (CHUNK, 1), 0) >= FRONT
        prev = carry[...]
        st_ref[0, 0] = prev
        y, new = _ssd_scan(xs_ref[...].astype(F32), b_ref[...].astype(F32), c_ref[...].astype(F32), dt_ref[...], prev,
                           bias_ref[...], al_ref[...], dk_ref[...], live)
        y_ref[...] = y.astype(ACT_DTYPE)
        o_ref[...] = _ssd_gate(y, z_ref[...].astype(F32), nw_ref[...]).astype(ACT_DTYPE)
        carry[...] = new

    act = jax.ShapeDtypeStruct((rows, SSD_D_INNER), ACT_DTYPE)
    return pl.pallas_call(
        body, name="ssd_fwd", grid=(SSD_GROUPS, nc),
        in_specs=xbc + [row(LANES), row(SSD_GW), par(LANES), par(LANES), par(LANES), par(SSD_GW)],
        out_specs=[row(SSD_GW), row(SSD_GW), state()],
        out_shape=[act, act, jax.ShapeDtypeStruct((nc, SSD_GROUPS, SSD_STATE, SSD_GW), F32)],
        scratch_shapes=[pltpu.VMEM((SSD_STATE, SSD_GW), F32)],
        compiler_params=_params(("parallel", "arbitrary")),
    )(xbc_c, xbc_c, xbc_c, dtr, z, dt_bias, a_log, d_skip, nw)


def _ssd_gate_bwd(y, z, dout, nw):
    rows = y.shape[0]
    tm = _row_tile(rows)

    def body(y_ref, z_ref, do_ref, nw_ref, dy_ref, dz_ref, dnw_ref):
        yv, zv, dov = y_ref[...].astype(F32), z_ref[...].astype(F32), do_ref[...].astype(F32)
        s = jax.nn.sigmoid(zv)
        silu = zv * s
        yz = yv * silu
        r = lax.rsqrt(jnp.mean(yz * yz, axis=-1, keepdims=True) + EPS)
        yhat = yz * r
        dn = dov * nw_ref[...]
        dyz = r * (dn - yhat * jnp.mean(dn * yhat, axis=-1, keepdims=True))
        dy_ref[...] = (dyz * silu).astype(ACT_DTYPE)
        dz_ref[...] = (dyz * yv * (s + silu * (1.0 - s))).astype(ACT_DTYPE)

        @pl.when(pl.program_id(1) == 0)
        def _():
            dnw_ref[...] = jnp.zeros_like(dnw_ref)

        dnw_ref[...] += jnp.sum(dov * yhat, axis=0, keepdims=True)

    spec = pl.BlockSpec((tm, SSD_GW), lambda g, i: (i, g))
    par = pl.BlockSpec((1, SSD_GW), lambda g, i: (0, g))
    act = jax.ShapeDtypeStruct((rows, SSD_D_INNER), ACT_DTYPE)
    return pl.pallas_call(
        body, name="ssd_gate_bwd", grid=(SSD_GROUPS, rows // tm), in_specs=[spec, spec, spec, par],
        out_specs=[spec, spec, par], out_shape=[act, act, jax.ShapeDtypeStruct((1, SSD_D_INNER), F32)],
        compiler_params=_params(("parallel", "arbitrary")),
    )(y, z, dout, nw)


def _ssd_bwd(xbc_c, dtr, dt_bias, a_log, d_skip, states, dy):
    rows = dy.shape[0]
    nc = rows // CHUNK
    cidx, row, par, state, xbc = _ssd_specs(True, nc)

    def body(xs_ref, b_ref, c_ref, dt_ref, bias_ref, al_ref, dk_ref, st_ref, dy_ref,
             dxs_ref, db_ref, dc_ref, ddt_ref, dbias_ref, dal_ref, ddk_ref, carry):
        c = pl.program_id(1)

        @pl.when(c == 0)
        def _():
            carry[...] = jnp.zeros_like(carry)
            for r in (dbias_ref, dal_ref, ddk_ref):
                r[...] = jnp.zeros_like(r)

        live = cidx(c) * CHUNK + lax.broadcasted_iota(jnp.int32, (CHUNK, 1), 0) >= FRONT
        dxs, dbm, dcm, ddt, dprev, dbias, dal, ddk = _ssd_scan_bwd(
            xs_ref[...].astype(F32), b_ref[...].astype(F32), c_ref[...].astype(F32), dt_ref[...], st_ref[0, 0],
            bias_ref[...], al_ref[...], dk_ref[...], live, dy_ref[...].astype(F32), carry[...])
        dxs_ref[...] = dxs.astype(ACT_DTYPE)
        db_ref[...] = dbm.astype(ACT_DTYPE)
        dc_ref[...] = dcm.astype(ACT_DTYPE)
        ddt_ref[...] = ddt
        carry[...] = dprev
        dbias_ref[...] += dbias
        dal_ref[...] += dal
        ddk_ref[...] += ddk

    bc = jax.ShapeDtypeStruct((rows, SSD_GROUPS * SSD_STATE), ACT_DTYPE)
    head = jax.ShapeDtypeStruct((1, SSD_GROUPS * LANES), F32)
    return pl.pallas_call(
        body, name="ssd_bwd", grid=(SSD_GROUPS, nc),
        in_specs=xbc + [row(LANES), par(LANES), par(LANES), par(LANES), state(), row(SSD_GW)],
        out_specs=[row(SSD_GW), row(SSD_STATE), row(SSD_STATE), row(LANES), par(LANES), par(LANES), par(LANES)],
        out_shape=[jax.ShapeDtypeStruct((rows, SSD_D_INNER), ACT_DTYPE), bc, bc,
                   jax.ShapeDtypeStruct((rows, SSD_GROUPS * LANES), F32), head, head, head],
        scratch_shapes=[pltpu.VMEM((SSD_STATE, SSD_GW), F32)],
        compiler_params=_params(("parallel", "arbitrary")),
    )(xbc_c, xbc_c, xbc_c, dtr, dt_bias, a_log, d_skip, states, dy)


def _rotary(t, cos, sin):
    half = t.shape[-1] // 2
    t1, t2 = t[:, :half], t[:, half:]
    return jnp.concatenate([t1 * cos - t2 * sin, t2 * cos + t1 * sin], axis=1)


def _ret_chunk(qh, kh, vh, gh, prev, cos, sin, lg):
    q = CHUNK
    qr = _rotary(qh, cos, sin)
    kr = _rotary(kh, cos, sin) * (RET_QK ** -0.5)
    li = lax.broadcasted_iota(jnp.int32, (q, q), 0)
    si = lax.broadcasted_iota(jnp.int32, (q, q), 1)
    dist = (li - si).astype(F32)
    decay = jnp.exp(jnp.where(li >= si, dist * lg, -jnp.inf))
    idx = lax.broadcasted_iota(jnp.int32, (q, 1), 0).astype(F32)
    scores = _dot(qr, kr, NT) * decay
    out = _dot(scores, vh)
    kv = _dot(kr * jnp.exp((q - 1.0 - idx) * lg), vh, TN)
    out = out + _dot(qr, prev) * jnp.exp((idx + 1.0) * lg)
    new = prev * jnp.exp(q * lg) + kv
    out = out * lax.rsqrt(jnp.mean(out * out, axis=-1, keepdims=True) + EPS)
    return _silu(gh) * out, new


def _ret_specs(rev, nc):
    cidx = (lambda c: nc - 1 - c) if rev else (lambda c: c)
    row = lambda width: pl.BlockSpec((CHUNK, width), lambda h, c: (cidx(c), h))
    tab = lambda: pl.BlockSpec((CHUNK, RET_QK // 2), lambda h, c: (cidx(c), 0))
    lgs = lambda: pl.BlockSpec((1, 8, LANES), lambda h, c: (h, 0, 0))
    state = lambda: pl.BlockSpec((1, 1, RET_QK, RET_V), lambda h, c: (cidx(c), h, 0, 0))
    ins = [row(RET_QK), row(RET_QK), row(RET_V), row(RET_V), tab(), tab(), lgs()]
    return row, state, ins


def _ret_fwd(q, k, v, g, cos, sin, lgam):
    rows = q.shape[0]
    nc = rows // CHUNK
    row, state, ins = _ret_specs(False, nc)

    def body(q_ref, k_ref, v_ref, g_ref, cos_ref, sin_ref, lg_ref, y_ref, st_ref, carry):
        c = pl.program_id(1)

        @pl.when(c == 0)
        def _():
            carry[...] = jnp.zeros_like(carry)

        prev = carry[...]
        st_ref[0, 0] = prev.astype(ACT_DTYPE)
        out, new = _ret_chunk(q_ref[...].astype(F32), k_ref[...].astype(F32), v_ref[...].astype(F32),
                              g_ref[...].astype(F32), prev, cos_ref[...], sin_ref[...], lg_ref[0, 0:1, 0:1])
        y_ref[...] = out.astype(ACT_DTYPE)
        carry[...] = new

    return pl.pallas_call(
        body, name="ret_fwd", grid=(RET_HEADS, nc), in_specs=ins, out_specs=[row(RET_V), state()],
        out_shape=[jax.ShapeDtypeStruct((rows, RET_HEADS * RET_V), ACT_DTYPE),
                   jax.ShapeDtypeStruct((nc, RET_HEADS, RET_QK, RET_V), ACT_DTYPE)],
        scratch_shapes=[pltpu.VMEM((RET_QK, RET_V), F32)],
        compiler_params=_params(("parallel", "arbitrary")),
    )(q, k, v, g, cos, sin, lgam)


def _ret_bwd(q, k, v, g, cos, sin, lgam, states, dy):
    rows = q.shape[0]
    nc = rows // CHUNK
    row, state, ins = _ret_specs(True, nc)

    def body(q_ref, k_ref, v_ref, g_ref, cos_ref, sin_ref, lg_ref, st_ref, dy_ref, dq_ref, dk_ref, dv_ref, dg_ref, carry):
        c = pl.program_id(1)

        @pl.when(c == 0)
        def _():
            carry[...] = jnp.zeros_like(carry)

        fn = functools.partial(_ret_chunk, cos=cos_ref[...], sin=sin_ref[...], lg=lg_ref[0, 0:1, 0:1])
        _, vjp = jax.vjp(fn, q_ref[...].astype(F32), k_ref[...].astype(F32), v_ref[...].astype(F32),
                         g_ref[...].astype(F32), st_ref[0, 0].astype(F32))
        dq, dk, dv, dg, dprev = vjp((dy_ref[...].astype(F32), carry[...]))
        dq_ref[...] = dq.astype(ACT_DTYPE)
        dk_ref[...] = dk.astype(ACT_DTYPE)
        dv_ref[...] = dv.astype(ACT_DTYPE)
        dg_ref[...] = dg.astype(ACT_DTYPE)
        carry[...] = dprev

    shp = lambda width: jax.ShapeDtypeStruct((rows, RET_HEADS * width), ACT_DTYPE)
    return pl.pallas_call(
        body, name="ret_bwd", grid=(RET_HEADS, nc), in_specs=ins + [state(), row(RET_V)],
        out_specs=[row(RET_QK), row(RET_QK), row(RET_V), row(RET_V)],
        out_shape=[shp(RET_QK), shp(RET_QK), shp(RET_V), shp(RET_V)],
        scratch_shapes=[pltpu.VMEM((RET_QK, RET_V), F32)],
        compiler_params=_params(("parallel", "arbitrary")),
    )(q, k, v, g, cos, sin, lgam, states, dy)


def _merge_fwd(bs, br, gs, gr):
    rows, d = bs.shape
    tm = _row_tile(rows)

    def body(bs_ref, br_ref, gs_ref, gr_ref, o_ref):
        o_ref[...] = (jax.nn.sigmoid(gs_ref[...].astype(F32)) * bs_ref[...].astype(F32)
                      + jax.nn.sigmoid(gr_ref[...].astype(F32)) * br_ref[...].astype(F32)).astype(ACT_DTYPE)

    spec = pl.BlockSpec((tm, d), lambda i: (i, 0))
    return pl.pallas_call(
        body, name="merge_fwd", grid=(rows // tm,), in_specs=[spec] * 4, out_specs=spec,
        out_shape=jax.ShapeDtypeStruct((rows, d), ACT_DTYPE), compiler_params=_params(("parallel",)),
    )(bs, br, gs, gr)


def _merge_bwd(dm, bs, br, gs, gr):
    rows, d = bs.shape
    tm = _row_tile(rows)

    def body(dm_ref, bs_ref, br_ref, gs_ref, gr_ref, dbs_ref, dbr_ref, dgs_ref, dgr_ref):
        dmv = dm_ref[...].astype(F32)
        for b_ref, g_ref, db_ref, dg_ref in ((bs_ref, gs_ref, dbs_ref, dgs_ref), (br_ref, gr_ref, dbr_ref, dgr_ref)):
            s = jax.nn.sigmoid(g_ref[...].astype(F32))
            db_ref[...] = (dmv * s).astype(ACT_DTYPE)
            dg_ref[...] = (dmv * b_ref[...].astype(F32) * s * (1.0 - s)).astype(ACT_DTYPE)

    spec = pl.BlockSpec((tm, d), lambda i: (i, 0))
    shp = jax.ShapeDtypeStruct((rows, d), ACT_DTYPE)
    return pl.pallas_call(
        body, name="merge_bwd", grid=(rows // tm,), in_specs=[spec] * 5, out_specs=[spec] * 4,
        out_shape=[shp] * 4, compiler_params=_params(("parallel",)),
    )(dm, bs, br, gs, gr)


def _place():
    x, y, c = lax.axis_index("x"), lax.axis_index("y"), lax.axis_index("c")
    return x, y, c


def _slot(p):
    return 4 * p[0] + 2 * p[1] + p[2]


def _allgather(arrs, name):
    n = len(arrs)
    any_spec = pl.BlockSpec(memory_space=pl.ANY)

    def body(*refs):
        ins, outs = refs[:n], refs[n:2 * n]
        send_sems, recv_sems, local_sems = refs[2 * n:]
        x, y, c = _place()
        me, sibling = (x, y, c), (x, y, 1 - c)
        chips = [(1 - x, y), (x, 1 - y), (1 - x, 1 - y)]

        def copy(a, k, block, to, src=None):
            dst = outs[a].at[_slot(block)]
            return pltpu.make_async_remote_copy(
                src_ref=dst if src is None else src, dst_ref=dst, send_sem=send_sems.at[a * 7 + k],
                recv_sem=recv_sems.at[a * 7 + k], device_id=to, device_id_type=MESH)

        mine, first, passed = [], [], []
        for a in range(n):
            cp = pltpu.make_async_copy(ins[a], outs[a].at[_slot(me)], local_sems.at[a])
            cp.start()
            mine.append(cp)
            first.append(copy(a, 0, me, sibling, src=ins[a]))
            first += [copy(a, 1 + j, me, (*chip, c), src=ins[a]) for j, chip in enumerate(chips)]
        for cp in first:
            cp.start()
        for j, chip in enumerate(chips):
            for a in range(n):
                copy(a, 1 + j, (*chip, c), me).wait_recv()
                cp = copy(a, 4 + j, (*chip, c), sibling)
                cp.start()
                passed.append(cp)
        for a in range(n):
            copy(a, 0, sibling, me).wait_recv()
            for j, chip in enumerate(chips):
                copy(a, 4 + j, (*chip, 1 - c), me).wait_recv()
        for cp in first + passed:
            cp.wait_send()
        for cp in mine:
            cp.wait()

    return pl.pallas_call(
        body, name=name, in_specs=[any_spec] * n, out_specs=[any_spec] * n,
        out_shape=[jax.ShapeDtypeStruct((N_DEV,) + a.shape, a.dtype) for a in arrs],
        scratch_shapes=[pltpu.SemaphoreType.DMA((7 * n,)), pltpu.SemaphoreType.DMA((7 * n,)), pltpu.SemaphoreType.DMA((n,))],
    )(*arrs)


def _peers():
    x, y, c = _place()
    return (x, y, c), [(x ^ dx, y ^ dy, c ^ dc) for dx in (0, 1) for dy in (0, 1) for dc in (0, 1)][1:]


def _exchange_copies(srcs, lands, send_sems, recv_sems, scatter, sender):
    me, peers = _peers()
    out = []
    for a, (src, land) in enumerate(zip(srcs, lands, strict=True)):
        for k, peer in enumerate(peers):
            src_ref = src.at[_slot(peer)] if scatter else src
            out.append(pltpu.make_async_remote_copy(
                src_ref=src_ref, dst_ref=land.at[_slot(me if sender else peer)], send_sem=send_sems.at[a * 7 + k],
                recv_sem=recv_sems.at[a * 7 + k], device_id=peer, device_id_type=MESH))
    return out


_HBM = pl.BlockSpec(memory_space=pltpu.HBM)
_SEM = pl.BlockSpec(memory_space=pltpu.SEMAPHORE)
_EFFECT = pltpu.SideEffectType.DATAFLOW_SIDE_EFFECTING


def _exchange_start(srcs, scatter, name):
    n = len(srcs)
    land_shapes = [s.shape if scatter else (N_DEV,) + s.shape for s in srcs]

    def body(*refs):
        for cp in _exchange_copies(refs[:n], refs[n:2 * n], refs[2 * n], refs[2 * n + 1], scatter, True):
            cp.start()
        refs[-1][...] = jnp.zeros_like(refs[-1])

    args = [pltpu.with_memory_space_constraint(s, pltpu.HBM) for s in srcs]
    args += [pltpu.with_memory_space_constraint(lax.empty(shp, s.dtype), pltpu.HBM) for s, shp in zip(srcs, land_shapes)]
    outs = pl.pallas_call(
        body, name=name,
        out_shape=(pltpu.SemaphoreType.DMA((7 * n,)), pltpu.SemaphoreType.DMA((7 * n,)))
        + tuple(pltpu.HBM(a.shape, a.dtype) for a in args) + (jax.ShapeDtypeStruct((8, LANES), F32),),
        in_specs=[_HBM] * (2 * n), out_specs=(_SEM, _SEM) + (_HBM,) * (2 * n) + (pl.BlockSpec(memory_space=pltpu.VMEM),),
        input_output_aliases={i: 2 + i for i in range(2 * n)},
        compiler_params=pltpu.CompilerParams(has_side_effects=_EFFECT),
    )(*args)
    return outs[:-1], outs[-1]


def _exchange_wait(handle, scatter, after, name):
    n = (len(handle) - 2) // 2
    thru = handle[2:]

    def body(*refs):
        for cp in _exchange_copies(refs[:n], refs[n:2 * n], refs[2 * n], refs[2 * n + 1], scatter, False):
            cp.wait_send()
            cp.wait_recv()

    outs = pl.pallas_call(
        body, name=name, out_shape=tuple(pltpu.HBM(t.shape, t.dtype) for t in thru),
        in_specs=[_HBM] * (2 * n) + [_SEM, _SEM, pl.BlockSpec(memory_space=pl.ANY)], out_specs=(_HBM,) * (2 * n),
        input_output_aliases={i: i for i in range(2 * n)},
        compiler_params=pltpu.CompilerParams(has_side_effects=_EFFECT),
    )(*thru, handle[0], handle[1], after)
    return list(outs[:n]), list(outs[n:])


def _allreduce_small(pack):
    rows, lanes = pack.shape

    def body(x_ref, o_ref, buf, send_sems, recv_sems):
        x, y, c = _place()
        me, sibling = (x, y, c), (x, y, 1 - c)
        chips = [(1 - x, y), (x, 1 - y), (1 - x, 1 - y)]

        def copy(k, block, to, src=None):
            dst = buf.at[_slot(block)]
            return pltpu.make_async_remote_copy(
                src_ref=dst if src is None else src, dst_ref=dst, send_sem=send_sems.at[k], recv_sem=recv_sems.at[k],
                device_id=to, device_id_type=MESH)

        buf[_slot(me)] = x_ref[...]
        first = [copy(0, me, sibling, src=x_ref)]
        first += [copy(1 + j, me, (*chip, c), src=x_ref) for j, chip in enumerate(chips)]
        for cp in first:
            cp.start()
        passed = [copy(4 + j, (*chip, c), sibling) for j, chip in enumerate(chips)]
        for j, chip in enumerate(chips):
            copy(1 + j, (*chip, c), me).wait_recv()
            passed[j].start()
        copy(0, sibling, me).wait_recv()
        for j, chip in enumerate(chips):
            copy(4 + j, (*chip, 1 - c), me).wait_recv()
        for cp in first + passed:
            cp.wait_send()
        acc = buf[0]
        for i in range(1, N_DEV):
            acc = acc + buf[i]
        o_ref[...] = acc

    vmem = pl.BlockSpec(memory_space=pltpu.VMEM)
    return pl.pallas_call(
        body, name="allreduce_small", in_specs=[vmem], out_specs=vmem,
        out_shape=jax.ShapeDtypeStruct((rows, lanes), F32),
        scratch_shapes=[pltpu.VMEM((N_DEV, rows, lanes), F32), pltpu.SemaphoreType.DMA((7,)), pltpu.SemaphoreType.DMA((7,))],
        compiler_params=pltpu.CompilerParams(vmem_limit_bytes=VMEM_LIMIT),
    )(pack)


def _adamw(w, g, m, v):
    m = ADAM_B1 * m + (1.0 - ADAM_B1) * g
    v = ADAM_B2 * v + (1.0 - ADAM_B2) * jnp.square(g)
    m_hat = m / (1.0 - ADAM_B1 ** ADAM_STEP)
    v_hat = v / (1.0 - ADAM_B2 ** ADAM_STEP)
    delta = -ADAM_LR * (m_hat / (jnp.sqrt(v_hat) + ADAM_EPS) + ADAM_WD * w)
    return delta, m, v


def _adam_shard(own, parts, w, m, v, name):
    r, c = w.shape
    tr = _pick(r, (128, 64, 32, 16, 8))

    def body(own_ref, p_ref, w_ref, m_ref, v_ref, g_ref, d_ref, nm_ref, nv_ref):
        _, peers = _peers()
        g = own_ref[...].astype(F32)
        for peer in peers:
            g = g + p_ref[_slot(peer)].astype(F32)
        g_ref[...] = g
        d_ref[...], nm_ref[...], nv_ref[...] = _adamw(w_ref[...], g, m_ref[...], v_ref[...])

    spec = pl.BlockSpec((tr, c), lambda i: (i, 0))
    shp = jax.ShapeDtypeStruct((r, c), F32)
    return pl.pallas_call(
        body, name=name, grid=(r // tr,),
        in_specs=[spec, pl.BlockSpec((N_DEV, tr, c), lambda i: (0, i, 0)), spec, spec, spec], out_specs=[spec] * 4,
        out_shape=[shp] * 4, compiler_params=_params(("parallel",)),
    )(own, parts, w, m, v)


def _adam_small(w, g, m, v):
    r, c = w.shape

    def body(w_ref, g_ref, m_ref, v_ref, d_ref, nm_ref, nv_ref):
        d_ref[...], nm_ref[...], nv_ref[...] = _adamw(w_ref[...], g_ref[...], m_ref[...], v_ref[...])

    shp = jax.ShapeDtypeStruct((r, c), F32)
    return pl.pallas_call(body, name="adam_small", out_shape=[shp] * 3)(w, g, m, v)


def _pack(arrs):
    rows = []
    for a in arrs:
        flat = a.reshape(-1).astype(F32)
        rows.append(jnp.pad(flat, (0, (-flat.shape[0]) % (8 * LANES))).reshape(-1, LANES))
    return jnp.concatenate(rows, axis=0)


def _unpack(pack, shapes):
    out, r = [], 0
    for s in shapes:
        size = math.prod(s)
        nr = -(-size // (8 * LANES)) * 8
        out.append(pack[r:r + nr].reshape(-1)[:size].reshape(s))
        r += nr
    return out


def _group_lanes(t):
    lead = t.shape[:-1]
    t = t.reshape(lead + (SSD_GROUPS, SSD_HPG))
    t = jnp.pad(t, [(0, 0)] * len(lead) + [(0, 0), (0, LANES - SSD_HPG)])
    return t.reshape(lead + (SSD_GROUPS * LANES,))


def _ungroup_lanes(t):
    lead = t.shape[:-1]
    return t.reshape(lead + (SSD_GROUPS, LANES))[..., :SSD_HPG].reshape(lead + (SSD_HEADS,))


def kernel(x, meta_tokens, mix_norm_w, w_in, ssd_conv_w, ssd_conv_b, ssd_dt_bias, ssd_A_log, ssd_D, ssd_norm_w, w_branch_ssd, w_branch_ret, w_out, ffn_norm_w, w_up, ffn_conv_w, ffn_conv_b, w_down, final_norm_w, loss_target, m_meta_tokens, m_mix_norm_w, m_w_in, m_ssd_conv_w, m_ssd_conv_b, m_ssd_dt_bias, m_ssd_A_log, m_ssd_D, m_ssd_norm_w, m_w_branch_ssd, m_w_branch_ret, m_w_out, m_ffn_norm_w, m_w_up, m_ffn_conv_w, m_ffn_conv_b, m_w_down, m_final_norm_w, v_meta_tokens, v_mix_norm_w, v_w_in, v_ssd_conv_w, v_ssd_conv_b, v_ssd_dt_bias, v_ssd_A_log, v_ssd_D, v_ssd_norm_w, v_w_branch_ssd, v_w_branch_ret, v_w_out, v_ffn_norm_w, v_w_up, v_ffn_conv_w, v_ffn_conv_b, v_w_down, v_final_norm_w):
    seq, d = x.shape[1], x.shape[2]
    rows = seq + PAD_ROWS
    tm = _row_tile(rows)
    me = _slot(_place())
    d_ff = w_down.shape[1] * N_DEV

    big = [w_in[0], w_branch_ssd[0], w_branch_ret[0], w_out[0], w_up[0], w_down[0]]
    first = _allgather([w_in[0].astype(COMM_DTYPE), meta_tokens, ssd_conv_w[0], ffn_conv_w[0]], "gather_first")
    rest_src = [b.astype(COMM_DTYPE) for b in big[1:]]
    rest_handle, rest_token = _exchange_start(rest_src, False, "gather_rest_start")
    cols = lambda t: jnp.transpose(t, (1, 0, 2)).reshape(t.shape[1], -1)
    rws = lambda t: t.reshape(-1, t.shape[2])
    w_in_f, conv_w, fconv_w = cols(first[0]), cols(first[2]), cols(first[3])
    meta_full = cols(first[1]) + rest_token[0, 0]
    widths = [SSD_D_INNER, SSD_CONV_DIM, SSD_HEADS, RET_HEADS * RET_QK, RET_HEADS * RET_QK, RET_HEADS * RET_V,
              RET_HEADS * RET_V, d, d]
    offs = [0]
    for wd in widths:
        offs.append(offs[-1] + wd)
    seg = [w_in_f[:, offs[i]:offs[i + 1]] for i in range(9)]
    seg[2] = _group_lanes(seg[2])
    w_in_p = jnp.concatenate(seg, axis=1)
    widths_p = [s.shape[1] for s in seg]
    offs_p = [0]
    for wd in widths_p:
        offs_p.append(offs_p[-1] + wd)
    in_p = offs_p[-1]

    h0 = jnp.concatenate([jnp.zeros((FRONT, d), F32), meta_full, x[0]], axis=0)
    u1 = _rms_fwd(h0, mix_norm_w, "rms1")
    seg_dtype = [ACT_DTYPE, ACT_DTYPE, F32] + [ACT_DTYPE] * 6
    proj = [_mm(u1, seg[i], mode="nn", out_dtype=seg_dtype[i], tm=tm, tn=_pick(widths_p[i], (1024, 512)), tk=d,
                name=f"in_proj_{i}") for i in range(9)]
    z, xbc, dtr, q, k, v, g, gs, gr = proj
    xbc_c = _ssd_conv_fwd(xbc, conv_w, ssd_conv_b)
    bias_p, alog_p, dsk_p = _group_lanes(ssd_dt_bias), _group_lanes(ssd_A_log), _group_lanes(ssd_D)
    y_ssd, y_scan, ssd_states = _ssd_fwd(xbc_c, dtr, z, bias_p, alog_p, dsk_p, ssd_norm_w)
    pos = jnp.arange(rows, dtype=F32) - float(FRONT)
    inv_freq = ROPE_BASE ** (-jnp.linspace(0.0, 1.0, RET_QK // 2, dtype=F32))
    ang = pos[:, None] * inv_freq[None, :]
    cos, sin = jnp.cos(ang), jnp.sin(ang)
    lgam = jnp.log(1.0 - 2.0 ** (-5.0 - jnp.arange(RET_HEADS, dtype=F32)))
    lgam = jnp.broadcast_to(lgam[:, None, None], (RET_HEADS, 8, LANES))
    y_ret, ret_states = _ret_fwd(q, k, v, g, cos, sin, lgam)
    rest_own, rest = _exchange_wait(rest_handle, False, y_ret, "gather_rest_wait")
    rest = [lax.dynamic_update_index_in_dim(land, own, me, 0) for land, own in zip(rest, rest_own, strict=True)]
    w_bs, w_br, w_o, w_up_f, w_dn = rws(rest[0]), rws(rest[1]), rws(rest[2]), cols(rest[3]), rws(rest[4])
    bs = _mm(y_ssd, w_bs, mode="nn", out_dtype=ACT_DTYPE, tm=tm, tn=d, tk=SSD_D_INNER, name="branch_ssd")
    br = _mm(y_ret, w_br, mode="nn", out_dtype=ACT_DTYPE, tm=tm, tn=d, tk=RET_HEADS * RET_V, name="branch_ret")
    merged = _merge_fwd(bs, br, gs, gr)
    h1 = _mm(merged, w_o, mode="nn", out_dtype=F32, tm=tm, tn=d, tk=d, name="out_proj", add=h0)
    u2 = _rms_fwd(h1, ffn_norm_w, "rms2")
    up = _mm(u2, w_up_f, mode="nn", out_dtype=ACT_DTYPE, tm=tm, tn=_pick(2 * d_ff, (1408, 512)), tk=d, name="ffn_up")
    act = _ffn_conv_fwd(up, fconv_w, ffn_conv_b)
    h2 = _mm(act, w_dn, mode="nn", out_dtype=F32, tm=tm, tn=d, tk=d_ff, name="ffn_down", add=h1)
    tgt = jnp.pad(loss_target[0], ((PAD_ROWS, 0), (0, 0)))
    dh2, loss_acc, g_final = _loss_head(h2, tgt, final_norm_w.reshape(1, d))

    tff = _pick(d_ff, (1408, 256))
    tkr = _pick(rows, (1664, 128))
    cparts = lambda t, n=N_DEV: jnp.transpose(t.reshape(t.shape[0], n, -1), (1, 0, 2)).astype(COMM_DTYPE)
    rparts = lambda t: t.reshape(N_DEV, -1, t.shape[1]).astype(COMM_DTYPE)
    d_act = _mm(dh2, w_dn, mode="nt", out_dtype=ACT_DTYPE, tm=tm, tn=tff, tk=d, name="d_act")
    g_w_dn = _mm(act, dh2, mode="tn", out_dtype=F32, tm=tff, tn=d, tk=tkr, name="g_w_down")
    c_dn = [rparts(g_w_dn)]
    h_dn, t_dn = _exchange_start(c_dn, True, "scatter_down_start")
    d_up_g, d_up_v, g_fcw_g, g_fcb_g, g_fcw_v, g_fcb_v = _ffn_conv_bwd(up, d_act, fconv_w, ffn_conv_b + t_dn[0, 0])
    g_fconv_w = jnp.concatenate([g_fcw_g, g_fcw_v], axis=1)
    g_fconv_b = jnp.concatenate([g_fcb_g, g_fcb_v], axis=1)
    g_w_up_g = _mm(u2, d_up_g, mode="tn", out_dtype=F32, tm=d, tn=tff, tk=tkr, name="g_w_up_gate")
    g_w_up_v = _mm(u2, d_up_v, mode="tn", out_dtype=F32, tm=d, tn=tff, tk=tkr, name="g_w_up_value")
    c_up = [jnp.concatenate([cparts(g_w_up_g, N_DEV // 2), cparts(g_w_up_v, N_DEV // 2)], axis=0)]
    h_up, t_up = _exchange_start(c_up, True, "scatter_up_start")
    du2 = _mm(d_up_g, w_up_f[:, :d_ff], mode="nt", out_dtype=F32, tm=tm, tn=d, tk=d_ff, name="d_u2_gate", after=t_up)
    du2 = _mm(d_up_v, w_up_f[:, d_ff:], mode="nt", out_dtype=F32, tm=tm, tn=d, tk=d_ff, name="d_u2_value", add=du2)
    dh1, g_ffn_norm = _rms_bwd(du2, h1, ffn_norm_w, dh2, "rms2_bwd")
    d_merged = _mm(dh1, w_o, mode="nt", out_dtype=F32, tm=tm, tn=d, tk=d, name="d_merged")
    g_w_o = _mm(merged, dh1, mode="tn", out_dtype=F32, tm=d, tn=d, tk=tkr, name="g_w_out")
    d_bs, d_br, d_gs, d_gr = _merge_bwd(d_merged, bs, br, gs, gr)
    d_yssd = _mm(d_bs, w_bs, mode="nt", out_dtype=ACT_DTYPE, tm=tm, tn=1024, tk=d, name="d_y_ssd")
    g_w_bs = _mm(y_ssd, d_bs, mode="tn", out_dtype=F32, tm=1024, tn=d, tk=tkr, name="g_w_branch_ssd")
    d_yret = _mm(d_br, w_br, mode="nt", out_dtype=ACT_DTYPE, tm=tm, tn=1024, tk=d, name="d_y_ret")
    g_w_br = _mm(y_ret, d_br, mode="tn", out_dtype=F32, tm=1024, tn=d, tk=tkr, name="g_w_branch_ret")
    c_mid = [rparts(g_w_bs), rparts(g_w_br), rparts(g_w_o)]
    h_mid, t_mid = _exchange_start(c_mid, True, "scatter_mid_start")
    d_yscan, d_z, g_nw = _ssd_gate_bwd(y_scan, z, d_yssd, ssd_norm_w + t_mid[0, 0])
    dxs, d_bm, d_cm, d_dtr, g_bias_p, g_alog_p, g_dsk_p = _ssd_bwd(xbc_c, dtr, bias_p, alog_p, dsk_p, ssd_states, d_yscan)
    d_xbc, g_conv_w, g_conv_b = _ssd_conv_bwd(xbc, jnp.concatenate([dxs, d_bm, d_cm], axis=1), conv_w, ssd_conv_b)
    d_q, d_k, d_v, d_g = _ret_bwd(q, k, v, g, cos, sin, lgam, ret_states, d_yret)
    dproj = jnp.concatenate([d_z, d_xbc, d_dtr.astype(ACT_DTYPE), d_q, d_k, d_v, d_g, d_gs, d_gr], axis=1)
    g_w_in_p = _mm(u1, dproj, mode="tn", out_dtype=F32, tm=d, tn=_pick(in_p, (768, 512)), tk=tkr, name="g_w_in")
    gseg = [g_w_in_p[:, offs_p[i]:offs_p[i + 1]] for i in range(9)]
    gseg[2] = _ungroup_lanes(gseg[2])
    c_in = [cparts(jnp.concatenate(gseg, axis=1))]
    h_in, t_in = _exchange_start(c_in, True, "scatter_in_start")
    du1 = _mm(dproj, w_in_p, mode="nt", out_dtype=F32, tm=tm, tn=d, tk=_pick(in_p, (1536, 512)), name="d_u1", after=t_in)
    dh0, g_mix_norm = _rms_bwd(du1, h0, mix_norm_w, dh1, "rms1_bwd")
    grad_x = dh0[PAD_ROWS:][None]

    landed = {}
    for key, handle, names in (("in", h_in, ["w_in"]), ("mid", h_mid, ["w_branch_ssd", "w_branch_ret", "w_out"]),
                               ("up", h_up, ["w_up"]), ("down", h_dn, ["w_down"])):
        srcs, lands = _exchange_wait(handle, True, dh0, f"scatter_{key}_wait")
        for nm, land, src in zip(names, lands, srcs, strict=True):
            landed[nm] = (lax.dynamic_index_in_dim(src, me, 0, keepdims=False), land)
    big_m = [m_w_in, m_w_branch_ssd, m_w_branch_ret, m_w_out, m_w_up, m_w_down]
    big_v = [v_w_in, v_w_branch_ssd, v_w_branch_ret, v_w_out, v_w_up, v_w_down]
    big_names = ["w_in", "w_branch_ssd", "w_branch_ret", "w_out", "w_up", "w_down"]
    big_out = {}
    for nm, w, m, v_ in zip(big_names, big, big_m, big_v, strict=True):
        big_out[nm] = [t[None] for t in _adam_shard(*landed[nm], w, m[0], v_[0], "adam_" + nm)]

    small_g = [dh0[FRONT:PAD_ROWS], g_mix_norm, g_conv_w, g_conv_b, _ungroup_lanes(g_bias_p), _ungroup_lanes(g_alog_p),
               _ungroup_lanes(g_dsk_p), g_nw, g_ffn_norm, g_fconv_w, g_fconv_b, g_final, loss_acc[0:1, 0:1]]
    total = _unpack(_allreduce_small(_pack(small_g)), [t.shape for t in small_g])
    loss = total[12].reshape(())
    shard = lambda t, width: lax.dynamic_slice_in_dim(t, me * width, width, axis=1)
    small_names = ["meta_tokens", "mix_norm_w", "ssd_conv_w", "ssd_conv_b", "ssd_dt_bias", "ssd_A_log", "ssd_D", "ssd_norm_w",
                   "ffn_norm_w", "ffn_conv_w", "ffn_conv_b", "final_norm_w"]
    small_w = [meta_tokens, mix_norm_w, ssd_conv_w, ssd_conv_b, ssd_dt_bias, ssd_A_log, ssd_D, ssd_norm_w, ffn_norm_w,
               ffn_conv_w, ffn_conv_b, final_norm_w]
    small_m = [m_meta_tokens, m_mix_norm_w, m_ssd_conv_w, m_ssd_conv_b, m_ssd_dt_bias, m_ssd_A_log, m_ssd_D, m_ssd_norm_w,
               m_ffn_norm_w, m_ffn_conv_w, m_ffn_conv_b, m_final_norm_w]
    small_v = [v_meta_tokens, v_mix_norm_w, v_ssd_conv_w, v_ssd_conv_b, v_ssd_dt_bias, v_ssd_A_log, v_ssd_D, v_ssd_norm_w,
               v_ffn_norm_w, v_ffn_conv_w, v_ffn_conv_b, v_final_norm_w]
    grads = total[:12]
    grads[0] = shard(grads[0], meta_tokens.shape[1])
    grads[2] = shard(grads[2], ssd_conv_w.shape[2])
    grads[9] = shard(grads[9], ffn_conv_w.shape[2])
    grads = [t.reshape(w.shape) for t, w in zip(grads, small_w, strict=True)]
    shapes = [w.shape for w in small_w]
    upd = _adam_small(_pack(small_w), _pack(grads), _pack(small_m), _pack(small_v))
    small_out = {nm: [gr_] + [u[i] for u in (_unpack(t, shapes) for t in upd)]
                 for i, (nm, gr_) in enumerate(zip(small_names, grads, strict=True))}

    order = ["meta_tokens", "mix_norm_w", "w_in", "ssd_conv_w", "ssd_conv_b", "ssd_dt_bias", "ssd_A_log", "ssd_D", "ssd_norm_w",
             "w_branch_ssd", "w_branch_ret", "w_out", "ffn_norm_w", "w_up", "ffn_conv_w", "ffn_conv_b", "w_down", "final_norm_w"]
    res = {**big_out, **small_out}
    return (loss, grad_x, *[res[nm][0] for nm in order], *[res[nm][1] for nm in order], *[res[nm][2] for nm in order],
            *[res[nm][3] for nm in order])
```

```python
import functools
import math

import jax
import jax.numpy as jnp
from jax import lax
from jax.experimental import pallas as pl
from jax.experimental.pallas import tpu as pltpu

F32 = jnp.float32
MXU_DTYPE = jnp.bfloat16
ACT_DTYPE = jnp.bfloat16
COMM_DTYPE = jnp.bfloat16

N_META = 16
CHUNK = 128
FRONT = CHUNK - N_META
PAD_ROWS = FRONT + N_META
EPS = 1e-6
N_DEV = 8

SSD_D_INNER = 2048
SSD_HEAD_DIM = 64
SSD_HEADS = 32
SSD_GROUPS = 4
SSD_HPG = SSD_HEADS // SSD_GROUPS
SSD_STATE = 128
SSD_CONV = 4
SSD_CONV_DIM = SSD_D_INNER + 2 * SSD_GROUPS * SSD_STATE
SSD_GW = SSD_D_INNER // SSD_GROUPS
RET_HEADS = 4
RET_QK = 256
RET_V = 512
ROPE_BASE = 10000.0
FFN_CONV = 3
HALO = 16
LANES = 128

ADAM_LR = 0.001
ADAM_B1 = 0.9
ADAM_B2 = 0.999
ADAM_EPS = 1e-08
ADAM_WD = 0.01
ADAM_STEP = 10

VMEM_LIMIT = 56 * 1024 * 1024
MESH = pl.DeviceIdType.MESH

NN = (((1,), (0,)), ((), ()))
NT = (((1,), (1,)), ((), ()))
TN = (((0,), (0,)), ((), ()))


def _params(sem):
    return pltpu.CompilerParams(dimension_semantics=sem, vmem_limit_bytes=VMEM_LIMIT)


def _mxu(a, b, dn):
    return lax.dot_general(a.astype(MXU_DTYPE), b.astype(MXU_DTYPE), dn, preferred_element_type=F32)


@functools.partial(jax.custom_vjp, nondiff_argnums=(2,))
def _dot(a, b, dn=NN):
    return _mxu(a, b, dn)


def _dot_fwd(a, b, dn):
    return _mxu(a, b, dn), (a, b)


def _dot_bwd(dn, res, g):
    a, b = res
    if dn == NN:
        return _mxu(g, b, NT), _mxu(a, g, TN)
    if dn == NT:
        return _mxu(g, b, NN), _mxu(g, a, TN)
    assert dn == TN
    return _mxu(b, g, NT), _mxu(a, g, NN)


_dot.defvjp(_dot_fwd, _dot_bwd)


def _silu(x):
    return x * jax.nn.sigmoid(x)


def _dsilu(x):
    s = jax.nn.sigmoid(x)
    return s * (1.0 + x * (1.0 - s))


def _row_tile(rows):
    return 640 if rows % 640 == 0 else 128


def _mm(a, b, *, mode, out_dtype, tm, tn, tk, name, add=None, after=None):
    if mode == "nn":
        (m, k), (k2, n) = a.shape, b.shape
    elif mode == "nt":
        (m, k), (n, k2) = a.shape, b.shape
    else:
        (k, m), (k2, n) = a.shape, b.shape
    assert k == k2 and m % tm == 0 and n % tn == 0 and k % tk == 0, (name, a.shape, b.shape, tm, tn, tk)
    nk = k // tk
    dn = {"nn": NN, "nt": NT, "tn": TN}[mode]
    has_add = add is not None
    n_in = 2 + has_add + (after is not None)

    def body(*refs):
        a_ref, b_ref = refs[0], refs[1]
        add_ref = refs[2] if has_add else None
        o_ref = refs[n_in]
        p = _dot(a_ref[...], b_ref[...], dn)
        if nk == 1:
            if has_add:
                p = p + add_ref[...]
            o_ref[...] = p.astype(out_dtype)
        else:
            acc_ref = refs[n_in + 1]
            kk = pl.program_id(2)

            @pl.when(kk == 0)
            def _():
                acc_ref[...] = p

            @pl.when(kk > 0)
            def _():
                acc_ref[...] += p

            @pl.when(kk == nk - 1)
            def _():
                r = acc_ref[...]
                if has_add:
                    r = r + add_ref[...]
                o_ref[...] = r.astype(out_dtype)

    if mode == "tn":
        a_spec = pl.BlockSpec((tk, tm), lambda j, i, kk: (kk, i))
    else:
        a_spec = pl.BlockSpec((tm, tk), lambda j, i, kk: (i, kk))
    if mode == "nt":
        b_spec = pl.BlockSpec((tn, tk), lambda j, i, kk: (j, kk))
    else:
        b_spec = pl.BlockSpec((tk, tn), lambda j, i, kk: (kk, j))
    o_spec = pl.BlockSpec((tm, tn), lambda j, i, kk: (i, j))
    in_specs = [a_spec, b_spec] + ([o_spec] if has_add else [])
    args = (a, b) + ((add,) if has_add else ())
    if after is not None:
        in_specs.append(pl.BlockSpec(memory_space=pl.ANY))
        args += (after,)
    return pl.pallas_call(
        body, name=name, grid=(n // tn, m // tm, nk), in_specs=in_specs, out_specs=o_spec,
        out_shape=jax.ShapeDtypeStruct((m, n), out_dtype),
        scratch_shapes=[pltpu.VMEM((tm, tn), F32)] if nk > 1 else [],
        compiler_params=_params(("parallel", "parallel", "arbitrary")),
    )(*args)


def _pick(n, cands):
    for c in cands:
        if n % c == 0:
            return c
    return n


def _rms_fwd(h, w, name):
    rows, d = h.shape
    tm = _row_tile(rows)

    def body(h_ref, w_ref, u_ref):
        x = h_ref[...]
        r = lax.rsqrt(jnp.mean(x * x, axis=-1, keepdims=True) + EPS)
        u_ref[...] = (x * r * w_ref[...]).astype(ACT_DTYPE)

    return pl.pallas_call(
        body, name=name, grid=(rows // tm,),
        in_specs=[pl.BlockSpec((tm, d), lambda i: (i, 0)), pl.BlockSpec((1, d), lambda i: (0, 0))],
        out_specs=pl.BlockSpec((tm, d), lambda i: (i, 0)),
        out_shape=jax.ShapeDtypeStruct((rows, d), ACT_DTYPE),
        compiler_params=_params(("parallel",)),
    )(h, w)


def _rms_bwd(du, h, w, dres, name):
    rows, d = h.shape
    tm = _row_tile(rows)

    def body(du_ref, h_ref, w_ref, dres_ref, dh_ref, dw_ref):
        x = h_ref[...]
        dy = du_ref[...].astype(F32)
        r = lax.rsqrt(jnp.mean(x * x, axis=-1, keepdims=True) + EPS)
        xhat = x * r
        dxn = dy * w_ref[...]
        dx = r * (dxn - xhat * jnp.mean(dxn * xhat, axis=-1, keepdims=True))
        dh_ref[...] = dres_ref[...] + dx

        @pl.when(pl.program_id(0) == 0)
        def _():
            dw_ref[...] = jnp.zeros_like(dw_ref)

        dw_ref[...] += jnp.sum(dy * xhat, axis=0, keepdims=True)

    return pl.pallas_call(
        body, name=name, grid=(rows // tm,),
        in_specs=[pl.BlockSpec((tm, d), lambda i: (i, 0)), pl.BlockSpec((tm, d), lambda i: (i, 0)),
                  pl.BlockSpec((1, d), lambda i: (0, 0)), pl.BlockSpec((tm, d), lambda i: (i, 0))],
        out_specs=[pl.BlockSpec((tm, d), lambda i: (i, 0)), pl.BlockSpec((1, d), lambda i: (0, 0))],
        out_shape=[jax.ShapeDtypeStruct((rows, d), F32), jax.ShapeDtypeStruct((1, d), F32)],
        compiler_params=_params(("arbitrary",)),
    )(du, h, w, dres)


def _loss_head(h2, tgt, w):
    rows, d = h2.shape
    tm = _row_tile(rows)

    def body(h_ref, t_ref, w_ref, dh_ref, loss_ref, dw_ref):
        i = pl.program_id(0)
        x = h_ref[...]
        r = lax.rsqrt(jnp.mean(x * x, axis=-1, keepdims=True) + EPS)
        xhat = x * r
        wv = w_ref[...]
        row = i * tm + lax.broadcasted_iota(jnp.int32, (tm, 1), 0)
        live = row >= PAD_ROWS
        diff = jnp.where(live, xhat * wv - t_ref[...], 0.0)
        dy = diff * (1.0 / d)
        dxn = dy * wv
        dh_ref[...] = r * (dxn - xhat * jnp.mean(dxn * xhat, axis=-1, keepdims=True))

        @pl.when(i == 0)
        def _():
            loss_ref[...] = jnp.zeros_like(loss_ref)
            dw_ref[...] = jnp.zeros_like(dw_ref)

        loss_ref[...] += 0.5 * jnp.sum(jnp.mean(diff * diff, axis=-1, keepdims=True))
        dw_ref[...] += jnp.sum(dy * xhat, axis=0, keepdims=True)

    return pl.pallas_call(
        body, name="loss_head", grid=(rows // tm,),
        in_specs=[pl.BlockSpec((tm, d), lambda i: (i, 0)), pl.BlockSpec((tm, d), lambda i: (i, 0)),
                  pl.BlockSpec((1, d), lambda i: (0, 0))],
        out_specs=[pl.BlockSpec((tm, d), lambda i: (i, 0)), pl.BlockSpec((8, LANES), lambda i: (0, 0)),
                   pl.BlockSpec((1, d), lambda i: (0, 0))],
        out_shape=[jax.ShapeDtypeStruct((rows, d), F32), jax.ShapeDtypeStruct((8, LANES), F32),
                   jax.ShapeDtypeStruct((1, d), F32)],
        compiler_params=_params(("arbitrary",)),
    )(h2, tgt, w)


def _prev_halo_spec(tm, width, col):
    return pl.BlockSpec((HALO, width), lambda j, i: (jnp.maximum(i * (tm // HALO) - 1, 0), col(j)))


def _next_halo_spec(tm, rows, width, col):
    last = rows // HALO - 1
    return pl.BlockSpec((HALO, width), lambda j, i: (jnp.minimum((i + 1) * (tm // HALO), last), col(j)))


def _conv_taps(cat, w_ref, b_ref, kw):
    acc = b_ref[...] + w_ref[kw - 1:kw, :] * cat
    for s in range(1, kw):
        acc = acc + w_ref[kw - 1 - s:kw - s, :] * pltpu.roll(cat, s, 0)
    return acc


def _conv_back(dpre, w_ref, kw):
    n = dpre.shape[0]
    acc = w_ref[kw - 1:kw, :] * dpre
    for s in range(1, kw):
        acc = acc + w_ref[kw - 1 - s:kw - s, :] * pltpu.roll(dpre, n - s, 0)
    return acc


def _ssd_conv_fwd(xbc, w, b):
    rows, width = xbc.shape
    tm, tc = _row_tile(rows), 512

    def body(x_ref, xp_ref, w_ref, b_ref, o_ref):
        i = pl.program_id(1)
        xp = jnp.where(i == 0, 0.0, xp_ref[...].astype(F32))
        cat = jnp.concatenate([xp, x_ref[...].astype(F32)], axis=0)
        pre = _conv_taps(cat, w_ref, b_ref, SSD_CONV)[HALO:]
        row = i * tm + lax.broadcasted_iota(jnp.int32, (tm, 1), 0)
        o_ref[...] = jnp.where(row >= FRONT, _silu(pre), 0.0).astype(ACT_DTYPE)

    main = pl.BlockSpec((tm, tc), lambda j, i: (i, j))
    par = lambda r: pl.BlockSpec((r, tc), lambda j, i: (0, j))
    return pl.pallas_call(
        body, name="ssd_conv_fwd", grid=(width // tc, rows // tm),
        in_specs=[main, _prev_halo_spec(tm, tc, lambda j: j), par(SSD_CONV), par(1)],
        out_specs=main, out_shape=jax.ShapeDtypeStruct((rows, width), ACT_DTYPE),
        compiler_params=_params(("parallel", "parallel")),
    )(xbc, xbc, w, b)


def _ssd_conv_bwd(xbc, dxc, w, b):
    rows, width = xbc.shape
    tm, tc = _row_tile(rows), 512
    kw = SSD_CONV

    def body(x_ref, xp_ref, xn_ref, d_ref, dn_ref, w_ref, b_ref, dx_ref, dw_ref, db_ref):
        i = pl.program_id(1)
        xp = jnp.where(i == 0, 0.0, xp_ref[...].astype(F32))
        cat = jnp.concatenate([xp, x_ref[...].astype(F32), xn_ref[...].astype(F32)], axis=0)
        pre = _conv_taps(cat, w_ref, b_ref, kw)[HALO:]
        row = i * tm + lax.broadcasted_iota(jnp.int32, (tm + HALO, 1), 0)
        live = (row >= FRONT) & (row < rows)
        dout = jnp.concatenate([d_ref[...].astype(F32), dn_ref[...].astype(F32)], axis=0)
        dpre = jnp.where(live, dout * _dsilu(pre), 0.0)
        dx_ref[...] = _conv_back(dpre, w_ref, kw)[:tm].astype(ACT_DTYPE)

        @pl.when(i == 0)
        def _():
            dw_ref[...] = jnp.zeros_like(dw_ref)
            db_ref[...] = jnp.zeros_like(db_ref)

        dmain = dpre[:tm]
        db_ref[...] += jnp.sum(dmain, axis=0, keepdims=True)
        for k in range(kw):
            s = kw - 1 - k
            xs = (pltpu.roll(cat, s, 0) if s else cat)[HALO:HALO + tm]
            dw_ref[k:k + 1, :] += jnp.sum(dmain * xs, axis=0, keepdims=True)

    main = pl.BlockSpec((tm, tc), lambda j, i: (i, j))
    par = lambda r: pl.BlockSpec((r, tc), lambda j, i: (0, j))
    col = lambda j: j
    return pl.pallas_call(
        body, name="ssd_conv_bwd", grid=(width // tc, rows // tm),
        in_specs=[main, _prev_halo_spec(tm, tc, col), _next_halo_spec(tm, rows, tc, col),
                  main, _next_halo_spec(tm, rows, tc, col), par(kw), par(1)],
        out_specs=[main, par(kw), par(1)],
        out_shape=[jax.ShapeDtypeStruct((rows, width), ACT_DTYPE), jax.ShapeDtypeStruct((kw, width), F32),
                   jax.ShapeDtypeStruct((1, width), F32)],
        compiler_params=_params(("parallel", "arbitrary")),
    )(xbc, xbc, xbc, dxc, dxc, w, b)


def _ffn_conv_fwd(up, w, b):
    rows, width = up.shape
    dff = width // 2
    tm, tc = _row_tile(rows), _pick(dff, (256, 128))
    nb = dff // tc
    kw = FFN_CONV

    def body(g_ref, gp_ref, v_ref, vp_ref, wg_ref, bg_ref, wv_ref, bv_ref, o_ref):
        i = pl.program_id(1)

        def pre(x_ref, xp_ref, w_ref, b_ref):
            xp = jnp.where(i == 0, 0.0, xp_ref[...].astype(F32))
            cat = jnp.concatenate([xp, x_ref[...].astype(F32)], axis=0)
            return _conv_taps(cat, w_ref, b_ref, kw)[HALO:]

        o_ref[...] = (_silu(pre(g_ref, gp_ref, wg_ref, bg_ref)) * pre(v_ref, vp_ref, wv_ref, bv_ref)).astype(ACT_DTYPE)

    gcol, vcol = (lambda j: j), (lambda j: j + nb)
    main = lambda col: pl.BlockSpec((tm, tc), lambda j, i: (i, col(j)))
    par = lambda r, col: pl.BlockSpec((r, tc), lambda j, i: (0, col(j)))
    return pl.pallas_call(
        body, name="ffn_conv_fwd", grid=(nb, rows // tm),
        in_specs=[main(gcol), _prev_halo_spec(tm, tc, gcol), main(vcol), _prev_halo_spec(tm, tc, vcol),
                  par(kw, gcol), par(1, gcol), par(kw, vcol), par(1, vcol)],
        out_specs=pl.BlockSpec((tm, tc), lambda j, i: (i, j)),
        out_shape=jax.ShapeDtypeStruct((rows, dff), ACT_DTYPE),
        compiler_params=_params(("parallel", "parallel")),
    )(up, up, up, up, w, b, w, b)


def _ffn_conv_bwd(up, dact, w, b):
    rows, width = up.shape
    dff = width // 2
    tm, tc = _row_tile(rows), _pick(dff, (256, 128))
    nb = dff // tc
    kw = FFN_CONV

    def body(g_ref, gp_ref, gn_ref, v_ref, vp_ref, vn_ref, d_ref, dn_ref, wg_ref, bg_ref, wv_ref, bv_ref,
             dxg_ref, dxv_ref, dwg_ref, dbg_ref, dwv_ref, dbv_ref):
        i = pl.program_id(1)

        def shifted(x_ref, xp_ref, xn_ref):
            xp = jnp.where(i == 0, 0.0, xp_ref[...].astype(F32))
            cat = jnp.concatenate([xp, x_ref[...].astype(F32), xn_ref[...].astype(F32)], axis=0)
            return [cat] + [pltpu.roll(cat, s, 0) for s in range(1, kw)]

        def taps(sh, w_ref, b_ref):
            acc = b_ref[...] + w_ref[kw - 1:kw, :] * sh[0]
            for s in range(1, kw):
                acc = acc + w_ref[kw - 1 - s:kw - s, :] * sh[s]
            return acc[HALO:]

        sh_g, sh_v = shifted(g_ref, gp_ref, gn_ref), shifted(v_ref, vp_ref, vn_ref)
        ag, av = taps(sh_g, wg_ref, bg_ref), taps(sh_v, wv_ref, bv_ref)
        row = i * tm + lax.broadcasted_iota(jnp.int32, (tm + HALO, 1), 0)
        dout = jnp.concatenate([d_ref[...].astype(F32), dn_ref[...].astype(F32)], axis=0)
        dout = jnp.where(row < rows, dout, 0.0)
        s = jax.nn.sigmoid(ag)
        silu = ag * s
        dpre_v = dout * silu
        dpre_g = dout * av * (s + silu * (1.0 - s))

        @pl.when(i == 0)
        def _():
            for r in (dwg_ref, dbg_ref, dwv_ref, dbv_ref):
                r[...] = jnp.zeros_like(r)

        for dpre, sh, w_ref, dx_ref, dw_ref, db_ref in ((dpre_g, sh_g, wg_ref, dxg_ref, dwg_ref, dbg_ref),
                                                          (dpre_v, sh_v, wv_ref, dxv_ref, dwv_ref, dbv_ref)):
            dx_ref[...] = _conv_back(dpre, w_ref, kw)[:tm].astype(ACT_DTYPE)
            dmain = dpre[:tm]
            db_ref[...] += jnp.sum(dmain, axis=0, keepdims=True)
            for k in range(kw):
                dw_ref[k:k + 1, :] += jnp.sum(dmain * sh[kw - 1 - k][HALO:HALO + tm], axis=0, keepdims=True)

    gcol, vcol = (lambda j: j), (lambda j: j + nb)
    main = lambda col: pl.BlockSpec((tm, tc), lambda j, i: (i, col(j)))
    par = lambda r, col: pl.BlockSpec((r, tc), lambda j, i: (0, col(j)))
    halos = lambda col: [_prev_halo_spec(tm, tc, col), _next_halo_spec(tm, rows, tc, col)]
    act_shape = jax.ShapeDtypeStruct((rows, dff), ACT_DTYPE)
    par_shapes = [jax.ShapeDtypeStruct((kw, dff), F32), jax.ShapeDtypeStruct((1, dff), F32)]
    return pl.pallas_call(
        body, name="ffn_conv_bwd", grid=(nb, rows // tm),
        in_specs=[main(gcol)] + halos(gcol) + [main(vcol)] + halos(vcol) + [main(gcol), _next_halo_spec(tm, rows, tc, gcol),
                  par(kw, gcol), par(1, gcol), par(kw, vcol), par(1, vcol)],
        out_specs=[main(gcol), main(gcol), par(kw, gcol), par(1, gcol), par(kw, gcol), par(1, gcol)],
        out_shape=[act_shape, act_shape] + par_shapes + par_shapes,
        compiler_params=_params(("parallel", "arbitrary")),
    )(up, up, up, up, up, up, dact, dact, w, b, w, b)


def _ssd_scalars(dtr, dt_bias, a_log, live):
    q = CHUNK
    pre = dtr + dt_bias
    dt = jnp.where(live, jax.nn.softplus(pre), 0.0)
    a_neg = -jnp.exp(a_log)
    li = lax.broadcasted_iota(jnp.int32, (q, q), 0)
    si = lax.broadcasted_iota(jnp.int32, (q, q), 1)
    causal = li >= si
    tri = jnp.where(causal, 1.0, 0.0).astype(F32)
    a_cs = jnp.dot(tri, dt * a_neg, precision=lax.Precision.HIGHEST, preferred_element_type=F32)
    return pre, dt, a_neg, a_cs, causal, tri


def _head_select():
    r = lax.broadcasted_iota(jnp.int32, (LANES, SSD_GW), 0)
    c = lax.broadcasted_iota(jnp.int32, (LANES, SSD_GW), 1)
    return jnp.where(c // SSD_HEAD_DIM == r, 1.0, 0.0).astype(MXU_DTYPE)


def _split(t, parts):
    out, rem = [], t
    for _ in range(parts):
        p = rem.astype(MXU_DTYPE)
        out.append(p)
        rem = rem - p.astype(F32)
    return out


def _head_cols(t, sel):
    return sum(_mxu(p, sel, NN) for p in _split(t, 2))


def _head_sums(t, sel):
    return sum(_mxu(p, sel, NT) for p in _split(t, 3))


def _half_masks():
    lane = lax.broadcasted_iota(jnp.int32, (CHUNK, LANES), 1)
    return lane < SSD_HEAD_DIM, lane >= SSD_HEAD_DIM


def _ssd_scan(xs, bm, cm, dtr, prev, dt_bias, a_log, d_skip, live):
    q = CHUNK
    sel = _head_select()
    _, dt, _, a_cs, causal, _ = _ssd_scalars(dtr, dt_bias, a_log, live)
    a_cs_t = a_cs.T
    a_end = a_cs[q - 1:q, :]
    e_x = _head_cols(jnp.exp(a_cs), sel)
    xdt = xs * _head_cols(dt, sel)
    cb = _dot(cm, bm, NT)
    y = _dot(cm, prev) * e_x + _head_cols(jnp.broadcast_to(d_skip, (8, LANES)), sel)[0:1] * xs
    new = prev * e_x[q - 1:q, :] + _dot(bm, xdt * _head_cols(jnp.exp(a_end - a_cs), sel), TN)
    masks = _half_masks()
    ys = []
    for pp in range(SSD_HPG // 2):
        xpair = xdt[:, pp * LANES:(pp + 1) * LANES]
        acc = jnp.zeros((q, LANES), F32)
        for half in range(2):
            hh = 2 * pp + half
            decay = jnp.exp(jnp.where(causal, a_cs[:, hh:hh + 1] - a_cs_t[hh:hh + 1, :], -jnp.inf))
            acc = acc + _dot(cb * decay, jnp.where(masks[half], xpair, 0.0))
        ys.append(acc)
    return y + jnp.concatenate(ys, axis=1), new


def _ssd_gate(y, z, nw):
    yz = y * _silu(z)
    return yz * lax.rsqrt(jnp.mean(yz * yz, axis=-1, keepdims=True) + EPS) * nw


def _ssd_scan_bwd(xs, bm, cm, dtr, prev, dt_bias, a_log, d_skip, live, dy, dnew):
    q = CHUNK
    sel = _head_select()
    pre, dt, a_neg, a_cs, causal, tri = _ssd_scalars(dtr, dt_bias, a_log, live)
    a_cs_t = a_cs.T
    a_end = a_cs[q - 1:q, :]
    dt_x, e_x, w_x = _head_cols(dt, sel), _head_cols(jnp.exp(a_cs), sel), _head_cols(jnp.exp(a_end - a_cs), sel)
    g_x, d_x = e_x[q - 1:q, :], _head_cols(jnp.broadcast_to(d_skip, (8, LANES)), sel)[0:1]
    xdt = xs * dt_x
    u = xdt * w_x
    cb = _mxu(cm, bm, NT)
    cs = _mxu(cm, prev, NN)
    dye = dy * e_x
    dcm = _mxu(dye, prev, NT)
    dprev = _mxu(cm, dye, TN) + dnew * g_x
    dacs_x = dye * cs
    dbm = _mxu(u, dnew, NT)
    du = _mxu(bm, dnew, NN)
    dw_x = du * u
    dacs_x = dacs_x - dw_x
    dend_x = jnp.sum(dw_x + dnew * prev * g_x, axis=0, keepdims=True)
    dxdt = du * w_x
    lane = lax.broadcasted_iota(jnp.int32, (q, LANES), 1)
    sub = lax.broadcasted_iota(jnp.int32, (q, LANES), 0)
    dcb = jnp.zeros((q, q), F32)
    dacs = jnp.zeros((q, LANES), F32)
    dacs_t = jnp.zeros((q, LANES), F32)
    masks = _half_masks()
    dxdt_p = []
    for pp in range(SSD_HPG // 2):
        ps = slice(pp * LANES, (pp + 1) * LANES)
        acc = jnp.zeros((q, LANES), F32)
        for half in range(2):
            hh = 2 * pp + half
            decay = jnp.exp(jnp.where(causal, a_cs[:, hh:hh + 1] - a_cs_t[hh:hh + 1, :], -jnp.inf))
            m = cb * decay
            dyh = jnp.where(masks[half], dy[:, ps], 0.0)
            dm = _mxu(dyh, xdt[:, ps], NT)
            acc = acc + _mxu(m, dyh, TN)
            dcb = dcb + dm * decay
            p = dm * m
            dacs = jnp.where(lane == hh, jnp.sum(p, axis=1, keepdims=True), dacs)
            dacs_t = jnp.where(sub == hh, jnp.sum(p, axis=0, keepdims=True), dacs_t)
        dxdt_p.append(acc)
    dcm = dcm + _mxu(dcb, bm, NN)
    dbm = dbm + _mxu(dcb, cm, TN)
    dxdt = dxdt + jnp.concatenate(dxdt_p, axis=1)
    dxs = dy * d_x + dxdt * dt_x
    rows_x = jnp.concatenate([dend_x, jnp.sum(dy * xs, axis=0, keepdims=True), jnp.zeros((6, SSD_GW), F32)], axis=0)
    rows = _head_sums(rows_x, sel)
    dacs = dacs - dacs_t.T + _head_sums(dacs_x, sel)
    dacs = dacs + jnp.where(sub == q - 1, rows[0:1], 0.0)
    tri_t = jnp.where(causal, 0.0, 1.0).astype(F32) + jnp.where(lane == sub, 1.0, 0.0)
    da = jnp.dot(tri_t, dacs, precision=lax.Precision.HIGHEST, preferred_element_type=F32)
    ddt = _head_sums(dxdt * xs, sel) + da * a_neg
    dalog = jnp.sum(da * dt, axis=0, keepdims=True) * a_neg
    ddtr = jnp.where(live, ddt * jax.nn.sigmoid(pre), 0.0)
    dbias = jnp.sum(ddtr, axis=0, keepdims=True)
    return dxs, dbm, dcm, ddtr, dprev, dbias, dalog, rows[1:2]


def _ssd_specs(rev, nc):
    cidx = (lambda c: nc - 1 - c) if rev else (lambda c: c)
    nb_b = SSD_D_INNER // SSD_STATE
    row = lambda width, col=lambda g: g: pl.BlockSpec((CHUNK, width), lambda g, c: (cidx(c), col(g)))
    par = lambda width: pl.BlockSpec((1, width), lambda g, c: (0, g))
    state = lambda: pl.BlockSpec((1, 1, SSD_STATE, SSD_GW), lambda g, c: (cidx(c), g, 0, 0))
    xbc = [row(SSD_GW), row(SSD_STATE, lambda g: nb_b + g), row(SSD_STATE, lambda g: nb_b + SSD_GROUPS + g)]
    return cidx, row, par, state, xbc


def _ssd_fwd(xbc_c, dtr, z, dt_bias, a_log, d_skip, nw):
    rows = z.shape[0]
    nc = rows // CHUNK
    _, row, par, state, xbc = _ssd_specs(False, nc)

    def body(xs_ref, b_ref, c_ref, dt_ref, z_ref, bias_ref, al_ref, dk_ref, nw_ref, o_ref, y_ref, st_ref, carry):
        c = pl.program_id(1)

        @pl.when(c == 0)
        def _():
            carry[...] = jnp.zeros_like(carry)

        live = c * CHUNK + lax.broadcasted_iota(jnp.int32, (CHUNK, 1), 0) >= FRONT
        prev = carry[...]
        st_ref[0, 0] = prev
        y, new = _ssd_scan(xs_ref[...].astype(F32), b_ref[...].astype(F32), c_ref[...].astype(F32), dt_ref[...], prev,
                           bias_ref[...], al_ref[...], dk_ref[...], live)
        y_ref[...] = y.astype(ACT_DTYPE)
        o_ref[...] = _ssd_gate(y, z_ref[...].astype(F32), nw_ref[...]).astype(ACT_DTYPE)
        carry[...] = new

    act = jax.ShapeDtypeStruct((rows, SSD_D_INNER), ACT_DTYPE)
    return pl.pallas_call(
        body, name="ssd_fwd", grid=(SSD_GROUPS, nc),
        in_specs=xbc + [row(LANES), row(SSD_GW), par(LANES), par(LANES), par(LANES), par(SSD_GW)],
        out_specs=[row(SSD_GW), row(SSD_GW), state()],
        out_shape=[act, act, jax.ShapeDtypeStruct((nc, SSD_GROUPS, SSD_STATE, SSD_GW), F32)],
        scratch_shapes=[pltpu.VMEM((SSD_STATE, SSD_GW), F32)],
        compiler_params=_params(("parallel", "arbitrary")),
    )(xbc_c, xbc_c, xbc_c, dtr, z, dt_bias, a_log, d_skip, nw)


def _ssd_gate_bwd(y, z, dout, nw):
    rows = y.shape[0]
    tm = _row_tile(rows)

    def body(y_ref, z_ref, do_ref, nw_ref, dy_ref, dz_ref, dnw_ref):
        yv, zv, dov = y_ref[...].astype(F32), z_ref[...].astype(F32), do_ref[...].astype(F32)
        s = jax.nn.sigmoid(zv)
        silu = zv * s
        yz = yv * silu
        r = lax.rsqrt(jnp.mean(yz * yz, axis=-1, keepdims=True) + EPS)
        yhat = yz * r
        dn = dov * nw_ref[...]
        dyz = r * (dn - yhat * jnp.mean(dn * yhat, axis=-1, keepdims=True))
        dy_ref[...] = (dyz * silu).astype(ACT_DTYPE)
        dz_ref[...] = (dyz * yv * (s + silu * (1.0 - s))).astype(ACT_DTYPE)

        @pl.when(pl.program_id(1) == 0)
        def _():
            dnw_ref[...] = jnp.zeros_like(dnw_ref)

        dnw_ref[...] += jnp.sum(dov * yhat, axis=0, keepdims=True)

    spec = pl.BlockSpec((tm, SSD_GW), lambda g, i: (i, g))
    par = pl.BlockSpec((1, SSD_GW), lambda g, i: (0, g))
    act = jax.ShapeDtypeStruct((rows, SSD_D_INNER), ACT_DTYPE)
    return pl.pallas_call(
        body, name="ssd_gate_bwd", grid=(SSD_GROUPS, rows // tm), in_specs=[spec, spec, spec, par],
        out_specs=[spec, spec, par], out_shape=[act, act, jax.ShapeDtypeStruct((1, SSD_D_INNER), F32)],
        compiler_params=_params(("parallel", "arbitrary")),
    )(y, z, dout, nw)


def _ssd_bwd(xbc_c, dtr, dt_bias, a_log, d_skip, states, dy):
    rows = dy.shape[0]
    nc = rows // CHUNK
    cidx, row, par, state, xbc = _ssd_specs(True, nc)

    def body(xs_ref, b_ref, c_ref, dt_ref, bias_ref, al_ref, dk_ref, st_ref, dy_ref,
             dxs_ref, db_ref, dc_ref, ddt_ref, dbias_ref, dal_ref, ddk_ref, carry):
        c = pl.program_id(1)

        @pl.when(c == 0)
        def _():
            carry[...] = jnp.zeros_like(carry)
            for r in (dbias_ref, dal_ref, ddk_ref):
                r[...] = jnp.zeros_like(r)

        live = cidx(c) * CHUNK + lax.broadcasted_iota(jnp.int32, (CHUNK, 1), 0) >= FRONT
        dxs, dbm, dcm, ddt, dprev, dbias, dal, ddk = _ssd_scan_bwd(
            xs_ref[...].astype(F32), b_ref[...].astype(F32), c_ref[...].astype(F32), dt_ref[...], st_ref[0, 0],
            bias_ref[...], al_ref[...], dk_ref[...], live, dy_ref[...].astype(F32), carry[...])
        dxs_ref[...] = dxs.astype(ACT_DTYPE)
        db_ref[...] = dbm.astype(ACT_DTYPE)
        dc_ref[...] = dcm.astype(ACT_DTYPE)
        ddt_ref[...] = ddt
        carry[...] = dprev
        dbias_ref[...] += dbias
        dal_ref[...] += dal
        ddk_ref[...] += ddk

    bc = jax.ShapeDtypeStruct((rows, SSD_GROUPS * SSD_STATE), ACT_DTYPE)
    head = jax.ShapeDtypeStruct((1, SSD_GROUPS * LANES), F32)
    return pl.pallas_call(
        body, name="ssd_bwd", grid=(SSD_GROUPS, nc),
        in_specs=xbc + [row(LANES), par(LANES), par(LANES), par(LANES), state(), row(SSD_GW)],
        out_specs=[row(SSD_GW), row(SSD_STATE), row(SSD_STATE), row(LANES), par(LANES), par(LANES), par(LANES)],
        out_shape=[jax.ShapeDtypeStruct((rows, SSD_D_INNER), ACT_DTYPE), bc, bc,
                   jax.ShapeDtypeStruct((rows, SSD_GROUPS * LANES), F32), head, head, head],
        scratch_shapes=[pltpu.VMEM((SSD_STATE, SSD_GW), F32)],
        compiler_params=_params(("parallel", "arbitrary")),
    )(xbc_c, xbc_c, xbc_c, dtr, dt_bias, a_log, d_skip, states, dy)


def _rotary(t, cos, sin):
    half = t.shape[-1] // 2
    t1, t2 = t[:, :half], t[:, half:]
    return jnp.concatenate([t1 * cos - t2 * sin, t2 * cos + t1 * sin], axis=1)


def _ret_chunk(qh, kh, vh, gh, prev, cos, sin, lg):
    q = CHUNK
    qr = _rotary(qh, cos, sin)
    kr = _rotary(kh, cos, sin) * (RET_QK ** -0.5)
    li = lax.broadcasted_iota(jnp.int32, (q, q), 0)
    si = lax.broadcasted_iota(jnp.int32, (q, q), 1)
    dist = (li - si).astype(F32)
    decay = jnp.exp(jnp.where(li >= si, dist * lg, -jnp.inf))
    idx = lax.broadcasted_iota(jnp.int32, (q, 1), 0).astype(F32)
    scores = _dot(qr, kr, NT) * decay
    out = _dot(scores, vh)
    kv = _dot(kr * jnp.exp((q - 1.0 - idx) * lg), vh, TN)
    out = out + _dot(qr, prev) * jnp.exp((idx + 1.0) * lg)
    new = prev * jnp.exp(q * lg) + kv
    out = out * lax.rsqrt(jnp.mean(out * out, axis=-1, keepdims=True) + EPS)
    return _silu(gh) * out, new


def _ret_specs(rev, nc):
    cidx = (lambda c: nc - 1 - c) if rev else (lambda c: c)
    row = lambda width: pl.BlockSpec((CHUNK, width), lambda h, c: (cidx(c), h))
    tab = lambda: pl.BlockSpec((CHUNK, RET_QK // 2), lambda h, c: (cidx(c), 0))
    lgs = lambda: pl.BlockSpec((1, 8, LANES), lambda h, c: (h, 0, 0))
    state = lambda: pl.BlockSpec((1, 1, RET_QK, RET_V), lambda h, c: (cidx(c), h, 0, 0))
    ins = [row(RET_QK), row(RET_QK), row(RET_V), row(RET_V), tab(), tab(), lgs()]
    return row, state, ins


def _ret_fwd(q, k, v, g, cos, sin, lgam):
    rows = q.shape[0]
    nc = rows // CHUNK
    row, state, ins = _ret_specs(False, nc)

    def body(q_ref, k_ref, v_ref, g_ref, cos_ref, sin_ref, lg_ref, y_ref, st_ref, carry):
        c = pl.program_id(1)

        @pl.when(c == 0)
        def _():
            carry[...] = jnp.zeros_like(carry)

        prev = carry[...]
        st_ref[0, 0] = prev.astype(ACT_DTYPE)
        out, new = _ret_chunk(q_ref[...].astype(F32), k_ref[...].astype(F32), v_ref[...].astype(F32),
                              g_ref[...].astype(F32), prev, cos_ref[...], sin_ref[...], lg_ref[0, 0:1, 0:1])
        y_ref[...] = out.astype(ACT_DTYPE)
        carry[...] = new

    return pl.pallas_call(
        body, name="ret_fwd", grid=(RET_HEADS, nc), in_specs=ins, out_specs=[row(RET_V), state()],
        out_shape=[jax.ShapeDtypeStruct((rows, RET_HEADS * RET_V), ACT_DTYPE),
                   jax.ShapeDtypeStruct((nc, RET_HEADS, RET_QK, RET_V), ACT_DTYPE)],
        scratch_shapes=[pltpu.VMEM((RET_QK, RET_V), F32)],
        compiler_params=_params(("parallel", "arbitrary")),
    )(q, k, v, g, cos, sin, lgam)


def _ret_bwd(q, k, v, g, cos, sin, lgam, states, dy):
    rows = q.shape[0]
    nc = rows // CHUNK
    row, state, ins = _ret_specs(True, nc)

    def body(q_ref, k_ref, v_ref, g_ref, cos_ref, sin_ref, lg_ref, st_ref, dy_ref, dq_ref, dk_ref, dv_ref, dg_ref, carry):
        c = pl.program_id(1)

        @pl.when(c == 0)
        def _():
            carry[...] = jnp.zeros_like(carry)

        fn = functools.partial(_ret_chunk, cos=cos_ref[...], sin=sin_ref[...], lg=lg_ref[0, 0:1, 0:1])
        _, vjp = jax.vjp(fn, q_ref[...].astype(F32), k_ref[...].astype(F32), v_ref[...].astype(F32),
                         g_ref[...].astype(F32), st_ref[0, 0].astype(F32))
        dq, dk, dv, dg, dprev = vjp((dy_ref[...].astype(F32), carry[...]))
        dq_ref[...] = dq.astype(ACT_DTYPE)
        dk_ref[...] = dk.astype(ACT_DTYPE)
        dv_ref[...] = dv.astype(ACT_DTYPE)
        dg_ref[...] = dg.astype(ACT_DTYPE)
        carry[...] = dprev

    shp = lambda width: jax.ShapeDtypeStruct((rows, RET_HEADS * width), ACT_DTYPE)
    return pl.pallas_call(
        body, name="ret_bwd", grid=(RET_HEADS, nc), in_specs=ins + [state(), row(RET_V)],
        out_specs=[row(RET_QK), row(RET_QK), row(RET_V), row(RET_V)],
        out_shape=[shp(RET_QK), shp(RET_QK), shp(RET_V), shp(RET_V)],
        scratch_shapes=[pltpu.VMEM((RET_QK, RET_V), F32)],
        compiler_params=_params(("parallel", "arbitrary")),
    )(q, k, v, g, cos, sin, lgam, states, dy)


def _merge_fwd(bs, br, gs, gr):
    rows, d = bs.shape
    tm = _row_tile(rows)

    def body(bs_ref, br_ref, gs_ref, gr_ref, o_ref):
        o_ref[...] = (jax.nn.sigmoid(gs_ref[...].astype(F32)) * bs_ref[...].astype(F32)
                      + jax.nn.sigmoid(gr_ref[...].astype(F32)) * br_ref[...].astype(F32)).astype(ACT_DTYPE)

    spec = pl.BlockSpec((tm, d), lambda i: (i, 0))
    return pl.pallas_call(
        body, name="merge_fwd", grid=(rows // tm,), in_specs=[spec] * 4, out_specs=spec,
        out_shape=jax.ShapeDtypeStruct((rows, d), ACT_DTYPE), compiler_params=_params(("parallel",)),
    )(bs, br, gs, gr)


def _merge_bwd(dm, bs, br, gs, gr):
    rows, d = bs.shape
    tm = _row_tile(rows)

    def body(dm_ref, bs_ref, br_ref, gs_ref, gr_ref, dbs_ref, dbr_ref, dgs_ref, dgr_ref):
        dmv = dm_ref[...].astype(F32)
        for b_ref, g_ref, db_ref, dg_ref in ((bs_ref, gs_ref, dbs_ref, dgs_ref), (br_ref, gr_ref, dbr_ref, dgr_ref)):
            s = jax.nn.sigmoid(g_ref[...].astype(F32))
            db_ref[...] = (dmv * s).astype(ACT_DTYPE)
            dg_ref[...] = (dmv * b_ref[...].astype(F32) * s * (1.0 - s)).astype(ACT_DTYPE)

    spec = pl.BlockSpec((tm, d), lambda i: (i, 0))
    shp = jax.ShapeDtypeStruct((rows, d), ACT_DTYPE)
    return pl.pallas_call(
        body, name="merge_bwd", grid=(rows // tm,), in_specs=[spec] * 5, out_specs=[spec] * 4,
        out_shape=[shp] * 4, compiler_params=_params(("parallel",)),
    )(dm, bs, br, gs, gr)


def _place():
    x, y, c = lax.axis_index("x"), lax.axis_index("y"), lax.axis_index("c")
    return x, y, c


def _slot(p):
    return 4 * p[0] + 2 * p[1] + p[2]


def _allgather(arrs, name):
    n = len(arrs)
    any_spec = pl.BlockSpec(memory_space=pl.ANY)

    def body(*refs):
        ins, outs = refs[:n], refs[n:2 * n]
        send_sems, recv_sems, local_sems = refs[2 * n:]
        x, y, c = _place()
        me, sibling = (x, y, c), (x, y, 1 - c)
        chips = [(1 - x, y), (x, 1 - y), (1 - x, 1 - y)]

        def copy(a, k, block, to, src=None):
            dst = outs[a].at[_slot(block)]
            return pltpu.make_async_remote_copy(
                src_ref=dst if src is None else src, dst_ref=dst, send_sem=send_sems.at[a * 7 + k],
                recv_sem=recv_sems.at[a * 7 + k], device_id=to, device_id_type=MESH)

        mine, first, passed = [], [], []
        for a in range(n):
            cp = pltpu.make_async_copy(ins[a], outs[a].at[_slot(me)], local_sems.at[a])
            cp.start()
            mine.append(cp)
            first.append(copy(a, 0, me, sibling, src=ins[a]))
            first += [copy(a, 1 + j, me, (*chip, c), src=ins[a]) for j, chip in enumerate(chips)]
        for cp in first:
            cp.start()
        for j, chip in enumerate(chips):
            for a in range(n):
                copy(a, 1 + j, (*chip, c), me).wait_recv()
                cp = copy(a, 4 + j, (*chip, c), sibling)
                cp.start()
                passed.append(cp)
        for a in range(n):
            copy(a, 0, sibling, me).wait_recv()
            for j, chip in enumerate(chips):
                copy(a, 4 + j, (*chip, 1 - c), me).wait_recv()
        for cp in first + passed:
            cp.wait_send()
        for cp in mine:
            cp.wait()

    return pl.pallas_call(
        body, name=name, in_specs=[any_spec] * n, out_specs=[any_spec] * n,
        out_shape=[jax.ShapeDtypeStruct((N_DEV,) + a.shape, a.dtype) for a in arrs],
        scratch_shapes=[pltpu.SemaphoreType.DMA((7 * n,)), pltpu.SemaphoreType.DMA((7 * n,)), pltpu.SemaphoreType.DMA((n,))],
    )(*arrs)


def _peers():
    x, y, c = _place()
    return (x, y, c), [(x ^ dx, y ^ dy, c ^ dc) for dx in (0, 1) for dy in (0, 1) for dc in (0, 1)][1:]


def _exchange_copies(srcs, lands, send_sems, recv_sems, scatter, sender):
    me, peers = _peers()
    out = []
    for a, (src, land) in enumerate(zip(srcs, lands, strict=True)):
        for k, peer in enumerate(peers):
            src_ref = src.at[_slot(peer)] if scatter else src
            out.append(pltpu.make_async_remote_copy(
                src_ref=src_ref, dst_ref=land.at[_slot(me if sender else peer)], send_sem=send_sems.at[a * 7 + k],
                recv_sem=recv_sems.at[a * 7 + k], device_id=peer, device_id_type=MESH))
    return out


_HBM = pl.BlockSpec(memory_space=pltpu.HBM)
_SEM = pl.BlockSpec(memory_space=pltpu.SEMAPHORE)
_EFFECT = pltpu.SideEffectType.DATAFLOW_SIDE_EFFECTING


def _exchange_start(srcs, scatter, name):
    n = len(srcs)
    land_shapes = [s.shape if scatter else (N_DEV,) + s.shape for s in srcs]

    def body(*refs):
        for cp in _exchange_copies(refs[:n], refs[n:2 * n], refs[2 * n], refs[2 * n + 1], scatter, True):
            cp.start()
        refs[-1][...] = jnp.zeros_like(refs[-1])

    args = [pltpu.with_memory_space_constraint(s, pltpu.HBM) for s in srcs]
    args += [pltpu.with_memory_space_constraint(lax.empty(shp, s.dtype), pltpu.HBM) for s, shp in zip(srcs, land_shapes)]
    outs = pl.pallas_call(
        body, name=name,
        out_shape=(pltpu.SemaphoreType.DMA((7 * n,)), pltpu.SemaphoreType.DMA((7 * n,)))
        + tuple(pltpu.HBM(a.shape, a.dtype) for a in args) + (jax.ShapeDtypeStruct((8, LANES), F32),),
        in_specs=[_HBM] * (2 * n), out_specs=(_SEM, _SEM) + (_HBM,) * (2 * n) + (pl.BlockSpec(memory_space=pltpu.VMEM),),
        input_output_aliases={i: 2 + i for i in range(2 * n)},
        compiler_params=pltpu.CompilerParams(has_side_effects=_EFFECT),
    )(*args)
    return outs[:-1], outs[-1]


def _exchange_wait(handle, scatter, after, name):
    n = (len(handle) - 2) // 2
    thru = handle[2:]

    def body(*refs):
        for cp in _exchange_copies(refs[:n], refs[n:2 * n], refs[2 * n], refs[2 * n + 1], scatter, False):
            cp.wait_send()
            cp.wait_recv()

    outs = pl.pallas_call(
        body, name=name, out_shape=tuple(pltpu.HBM(t.shape, t.dtype) for t in thru),
        in_specs=[_HBM] * (2 * n) + [_SEM, _SEM, pl.BlockSpec(memory_space=pl.ANY)], out_specs=(_HBM,) * (2 * n),
        input_output_aliases={i: i for i in range(2 * n)},
        compiler_params=pltpu.CompilerParams(has_side_effects=_EFFECT),
    )(*thru, handle[0], handle[1], after)
    return list(outs[:n]), list(outs[n:])


def _allreduce_small(pack):
    rows, lanes = pack.shape

    def body(x_ref, o_ref, buf, send_sems, recv_sems):
        x, y, c = _place()
        me, sibling = (x, y, c), (x, y, 1 - c)
        chips = [(1 - x, y), (x, 1 - y), (1 - x, 1 - y)]

        def copy(k, block, to, src=None):
            dst = buf.at[_slot(block)]
            return pltpu.make_async_remote_copy(
                src_ref=dst if src is None else src, dst_ref=dst, send_sem=send_sems.at[k], recv_sem=recv_sems.at[k],
                device_id=to, device_id_type=MESH)

        buf[_slot(me)] = x_ref[...]
        first = [copy(0, me, sibling, src=x_ref)]
        first += [copy(1 + j, me, (*chip, c), src=x_ref) for j, chip in enumerate(chips)]
        for cp in first:
            cp.start()
        passed = [copy(4 + j, (*chip, c), sibling) for j, chip in enumerate(chips)]
        for j, chip in enumerate(chips):
            copy(1 + j, (*chip, c), me).wait_recv()
            passed[j].start()
        copy(0, sibling, me).wait_recv()
        for j, chip in enumerate(chips):
            copy(4 + j, (*chip, 1 - c), me).wait_recv()
        for cp in first + passed:
            cp.wait_send()
        acc = buf[0]
        for i in range(1, N_DEV):
            acc = acc + buf[i]
        o_ref[...] = acc

    vmem = pl.BlockSpec(memory_space=pltpu.VMEM)
    return pl.pallas_call(
        body, name="allreduce_small", in_specs=[vmem], out_specs=vmem,
        out_shape=jax.ShapeDtypeStruct((rows, lanes), F32),
        scratch_shapes=[pltpu.VMEM((N_DEV, rows, lanes), F32), pltpu.SemaphoreType.DMA((7,)), pltpu.SemaphoreType.DMA((7,))],
        compiler_params=pltpu.CompilerParams(vmem_limit_bytes=VMEM_LIMIT),
    )(pack)


def _adamw(w, g, m, v):
    m = ADAM_B1 * m + (1.0 - ADAM_B1) * g
    v = ADAM_B2 * v + (1.0 - ADAM_B2) * jnp.square(g)
    m_hat = m / (1.0 - ADAM_B1 ** ADAM_STEP)
    v_hat = v / (1.0 - ADAM_B2 ** ADAM_STEP)
    delta = -ADAM_LR * (m_hat / (jnp.sqrt(v_hat) + ADAM_EPS) + ADAM_WD * w)
    return delta, m, v


def _adam_shard(own, parts, w, m, v, name):
    r, c = w.shape
    tr = _pick(r, (128, 64, 32, 16, 8))

    def body(own_ref, p_ref, w_ref, m_ref, v_ref, g_ref, d_ref, nm_ref, nv_ref):
        _, peers = _peers()
        g = own_ref[...].astype(F32)
        for peer in peers:
            g = g + p_ref[_slot(peer)].astype(F32)
        g_ref[...] = g
        d_ref[...], nm_ref[...], nv_ref[...] = _adamw(w_ref[...], g, m_ref[...], v_ref[...])

    spec = pl.BlockSpec((tr, c), lambda i: (i, 0))
    shp = jax.ShapeDtypeStruct((r, c), F32)
    return pl.pallas_call(
        body, name=name, grid=(r // tr,),
        in_specs=[spec, pl.BlockSpec((N_DEV, tr, c), lambda i: (0, i, 0)), spec, spec, spec], out_specs=[spec] * 4,
        out_shape=[shp] * 4, compiler_params=_params(("parallel",)),
    )(own, parts, w, m, v)


def _adam_small(w, g, m, v):
    r, c = w.shape

    def body(w_ref, g_ref, m_ref, v_ref, d_ref, nm_ref, nv_ref):
        d_ref[...], nm_ref[...], nv_ref[...] = _adamw(w_ref[...], g_ref[...], m_ref[...], v_ref[...])

    shp = jax.ShapeDtypeStruct((r, c), F32)
    return pl.pallas_call(body, name="adam_small", out_shape=[shp] * 3)(w, g, m, v)


def _pack(arrs):
    rows = []
    for a in arrs:
        flat = a.reshape(-1).astype(F32)
        rows.append(jnp.pad(flat, (0, (-flat.shape[0]) % (8 * LANES))).reshape(-1, LANES))
    return jnp.concatenate(rows, axis=0)


def _unpack(pack, shapes):
    out, r = [], 0
    for s in shapes:
        size = math.prod(s)
        nr = -(-size // (8 * LANES)) * 8
        out.append(pack[r:r + nr].reshape(-1)[:size].reshape(s))
        r += nr
    return out


def _group_lanes(t):
    lead = t.shape[:-1]
    t = t.reshape(lead + (SSD_GROUPS, SSD_HPG))
    t = jnp.pad(t, [(0, 0)] * len(lead) + [(0, 0), (0, LANES - SSD_HPG)])
    return t.reshape(lead + (SSD_GROUPS * LANES,))


def _ungroup_lanes(t):
    lead = t.shape[:-1]
    return t.reshape(lead + (SSD_GROUPS, LANES))[..., :SSD_HPG].reshape(lead + (SSD_HEADS,))


def kernel(x, meta_tokens, mix_norm_w, w_in, ssd_conv_w, ssd_conv_b, ssd_dt_bias, ssd_A_log, ssd_D, ssd_norm_w, w_branch_ssd, w_branch_ret, w_out, ffn_norm_w, w_up, ffn_conv_w, ffn_conv_b, w_down, final_norm_w, loss_target, m_meta_tokens, m_mix_norm_w, m_w_in, m_ssd_conv_w, m_ssd_conv_b, m_ssd_dt_bias, m_ssd_A_log, m_ssd_D, m_ssd_norm_w, m_w_branch_ssd, m_w_branch_ret, m_w_out, m_ffn_norm_w, m_w_up, m_ffn_conv_w, m_ffn_conv_b, m_w_down, m_final_norm_w, v_meta_tokens, v_mix_norm_w, v_w_in, v_ssd_conv_w, v_ssd_conv_b, v_ssd_dt_bias, v_ssd_A_log, v_ssd_D, v_ssd_norm_w, v_w_branch_ssd, v_w_branch_ret, v_w_out, v_ffn_norm_w, v_w_up, v_ffn_conv_w, v_ffn_conv_b, v_w_down, v_final_norm_w):
    seq, d = x.shape[1], x.shape[2]
    rows = seq + PAD_ROWS
    tm = _row_tile(rows)
    me = _slot(_place())
    d_ff = w_down.shape[1] * N_DEV

    big = [w_in[0], w_branch_ssd[0], w_branch_ret[0], w_out[0], w_up[0], w_down[0]]
    first = _allgather([w_in[0].astype(COMM_DTYPE), meta_tokens, ssd_conv_w[0], ffn_conv_w[0]], "gather_first")
    rest_src = [b.astype(COMM_DTYPE) for b in big[1:]]
    rest_handle, rest_token = _exchange_start(rest_src, False, "gather_rest_start")
    cols = lambda t: jnp.transpose(t, (1, 0, 2)).reshape(t.shape[1], -1)
    rws = lambda t: t.reshape(-1, t.shape[2])
    w_in_f, conv_w, fconv_w = cols(first[0]), cols(first[2]), cols(first[3])
    meta_full = cols(first[1]) + rest_token[0, 0]
    widths = [SSD_D_INNER, SSD_CONV_DIM, SSD_HEADS, RET_HEADS * RET_QK, RET_HEADS * RET_QK, RET_HEADS * RET_V,
              RET_HEADS * RET_V, d, d]
    offs = [0]
    for wd in widths:
        offs.append(offs[-1] + wd)
    seg = [w_in_f[:, offs[i]:offs[i + 1]] for i in range(9)]
    seg[2] = _group_lanes(seg[2])
    w_in_p = jnp.concatenate(seg, axis=1)
    widths_p = [s.shape[1] for s in seg]
    offs_p = [0]
    for wd in widths_p:
        offs_p.append(offs_p[-1] + wd)
    in_p = offs_p[-1]

    h0 = jnp.concatenate([jnp.zeros((FRONT, d), F32), meta_full, x[0]], axis=0)
    u1 = _rms_fwd(h0, mix_norm_w, "rms1")
    seg_dtype = [ACT_DTYPE, ACT_DTYPE, F32] + [ACT_DTYPE] * 6
    proj = [_mm(u1, seg[i], mode="nn", out_dtype=seg_dtype[i], tm=tm, tn=_pick(widths_p[i], (1024, 512)), tk=d,
                name=f"in_proj_{i}") for i in range(9)]
    z, xbc, dtr, q, k, v, g, gs, gr = proj
    xbc_c = _ssd_conv_fwd(xbc, conv_w, ssd_conv_b)
    bias_p, alog_p, dsk_p = _group_lanes(ssd_dt_bias), _group_lanes(ssd_A_log), _group_lanes(ssd_D)
    y_ssd, y_scan, ssd_states = _ssd_fwd(xbc_c, dtr, z, bias_p, alog_p, dsk_p, ssd_norm_w)
    pos = jnp.arange(rows, dtype=F32) - float(FRONT)
    inv_freq = ROPE_BASE ** (-jnp.linspace(0.0, 1.0, RET_QK // 2, dtype=F32))
    ang = pos[:, None] * inv_freq[None, :]
    cos, sin = jnp.cos(ang), jnp.sin(ang)
    lgam = jnp.log(1.0 - 2.0 ** (-5.0 - jnp.arange(RET_HEADS, dtype=F32)))
    lgam = jnp.broadcast_to(lgam[:, None, None], (RET_HEADS, 8, LANES))
    y_ret, ret_states = _ret_fwd(q, k, v, g, cos, sin, lgam)
    rest_own, rest = _exchange_wait(rest_handle, False, y_ret, "gather_rest_wait")
    rest = [lax.dynamic_update_index_in_dim(land, own, me, 0) for land, own in zip(rest, rest_own, strict=True)]
    w_bs, w_br, w_o, w_up_f, w_dn = rws(rest[0]), rws(rest[1]), rws(rest[2]), cols(rest[3]), rws(rest[4])
    bs = _mm(y_ssd, w_bs, mode="nn", out_dtype=ACT_DTYPE, tm=tm, tn=d, tk=SSD_D_INNER, name="branch_ssd")
    br = _mm(y_ret, w_br, mode="nn", out_dtype=ACT_DTYPE, tm=tm, tn=d, tk=RET_HEADS * RET_V, name="branch_ret")
    merged = _merge_fwd(bs, br, gs, gr)
    h1 = _mm(merged, w_o, mode="nn", out_dtype=F32, tm=tm, tn=d, tk=d, name="out_proj", add=h0)
    u2 = _rms_fwd(h1, ffn_norm_w, "rms2")
    up = _mm(u2, w_up_f, mode="nn", out_dtype=ACT_DTYPE, tm=tm, tn=_pick(2 * d_ff, (1408, 512)), tk=d, name="ffn_up")
    act = _ffn_conv_fwd(up, fconv_w, ffn_conv_b)
    h2 = _mm(act, w_dn, mode="nn", out_dtype=F32, tm=tm, tn=d, tk=d_ff, name="ffn_down", add=h1)
    tgt = jnp.pad(loss_target[0], ((PAD_ROWS, 0), (0, 0)))
    dh2, loss_acc, g_final = _loss_head(h2, tgt, final_norm_w.reshape(1, d))

    tff = _pick(d_ff, (1408, 256))
    tkr = _pick(rows, (1664, 128))
    cparts = lambda t, n=N_DEV: jnp.transpose(t.reshape(t.shape[0], n, -1), (1, 0, 2)).astype(COMM_DTYPE)
    rparts = lambda t: t.reshape(N_DEV, -1, t.shape[1]).astype(COMM_DTYPE)
    d_act = _mm(dh2, w_dn, mode="nt", out_dtype=ACT_DTYPE, tm=tm, tn=tff, tk=d, name="d_act")
    g_w_dn = _mm(act, dh2, mode="tn", out_dtype=F32, tm=tff, tn=d, tk=tkr, name="g_w_down")
    c_dn = [rparts(g_w_dn)]
    h_dn, t_dn = _exchange_start(c_dn, True, "scatter_down_start")
    d_up_g, d_up_v, g_fcw_g, g_fcb_g, g_fcw_v, g_fcb_v = _ffn_conv_bwd(up, d_act, fconv_w, ffn_conv_b + t_dn[0, 0])
    g_fconv_w = jnp.concatenate([g_fcw_g, g_fcw_v], axis=1)
    g_fconv_b = jnp.concatenate([g_fcb_g, g_fcb_v], axis=1)
    g_w_up_g = _mm(u2, d_up_g, mode="tn", out_dtype=F32, tm=d, tn=tff, tk=tkr, name="g_w_up_gate")
    g_w_up_v = _mm(u2, d_up_v, mode="tn", out_dtype=F32, tm=d, tn=tff, tk=tkr, name="g_w_up_value")
    c_up = [jnp.concatenate([cparts(g_w_up_g, N_DEV // 2), cparts(g_w_up_v, N_DEV // 2)], axis=0)]
    h_up, t_up = _exchange_start(c_up, True, "scatter_up_start")
    du2 = _mm(d_up_g, w_up_f[:, :d_ff], mode="nt", out_dtype=F32, tm=tm, tn=d, tk=d_ff, name="d_u2_gate", after=t_up)
    du2 = _mm(d_up_v, w_up_f[:, d_ff:], mode="nt", out_dtype=F32, tm=tm, tn=d, tk=d_ff, name="d_u2_value", add=du2)
    dh1, g_ffn_norm = _rms_bwd(du2, h1, ffn_norm_w, dh2, "rms2_bwd")
    d_merged = _mm(dh1, w_o, mode="nt", out_dtype=F32, tm=tm, tn=d, tk=d, name="d_merged")
    g_w_o = _mm(merged, dh1, mode="tn", out_dtype=F32, tm=d, tn=d, tk=tkr, name="g_w_out")
    d_bs, d_br, d_gs, d_gr = _merge_bwd(d_merged, bs, br, gs, gr)
    d_yssd = _mm(d_bs, w_bs, mode="nt", out_dtype=ACT_DTYPE, tm=tm, tn=1024, tk=d, name="d_y_ssd")
    g_w_bs = _mm(y_ssd, d_bs, mode="tn", out_dtype=F32, tm=1024, tn=d, tk=tkr, name="g_w_branch_ssd")
    d_yret = _mm(d_br, w_br, mode="nt", out_dtype=ACT_DTYPE, tm=tm, tn=1024, tk=d, name="d_y_ret")
    g_w_br = _mm(y_ret, d_br, mode="tn", out_dtype=F32, tm=1024, tn=d, tk=tkr, name="g_w_branch_ret")
    c_mid = [rparts(g_w_bs), rparts(g_w_br), rparts(g_w_o)]
    h_mid, t_mid = _exchange_start(c_mid, True, "scatter_mid_start")
    d_yscan, d_z, g_nw = _ssd_gate_bwd(y_scan, z, d_yssd, ssd_norm_w + t_mid[0, 0])
    dxs, d_bm, d_cm, d_dtr, g_bias_p, g_alog_p, g_dsk_p = _ssd_bwd(xbc_c, dtr, bias_p, alog_p, dsk_p, ssd_states, d_yscan)
    d_xbc, g_conv_w, g_conv_b = _ssd_conv_bwd(xbc, jnp.concatenate([dxs, d_bm, d_cm], axis=1), conv_w, ssd_conv_b)
    d_q, d_k, d_v, d_g = _ret_bwd(q, k, v, g, cos, sin, lgam, ret_states, d_yret)
    dproj = jnp.concatenate([d_z, d_xbc, d_dtr.astype(ACT_DTYPE), d_q, d_k, d_v, d_g, d_gs, d_gr], axis=1)
    g_w_in_p = _mm(u1, dproj, mode="tn", out_dtype=F32, tm=d, tn=_pick(in_p, (768, 512)), tk=tkr, name="g_w_in")
    gseg = [g_w_in_p[:, offs_p[i]:offs_p[i + 1]] for i in range(9)]
    gseg[2] = _ungroup_lanes(gseg[2])
    c_in = [cparts(jnp.concatenate(gseg, axis=1))]
    h_in, t_in = _exchange_start(c_in, True, "scatter_in_start")
    du1 = _mm(dproj, w_in_p, mode="nt", out_dtype=F32, tm=tm, tn=d, tk=_pick(in_p, (1536, 512)), name="d_u1", after=t_in)
    dh0, g_mix_norm = _rms_bwd(du1, h0, mix_norm_w, dh1, "rms1_bwd")
    grad_x = dh0[PAD_ROWS:][None]

    landed = {}
    for key, handle, names in (("in", h_in, ["w_in"]), ("mid", h_mid, ["w_branch_ssd", "w_branch_ret", "w_out"]),
                               ("up", h_up, ["w_up"]), ("down", h_dn, ["w_down"])):
        srcs, lands = _exchange_wait(handle, True, dh0, f"scatter_{key}_wait")
        for nm, land, src in zip(names, lands, srcs, strict=True):
            landed[nm] = (lax.dynamic_index_in_dim(src, me, 0, keepdims=False), land)
    big_m = [m_w_in, m_w_branch_ssd, m_w_branch_ret, m_w_out, m_w_up, m_w_down]
    big_v = [v_w_in, v_w_branch_ssd, v_w_branch_ret, v_w_out, v_w_up, v_w_down]
    big_names = ["w_in", "w_branch_ssd", "w_branch_ret", "w_out", "w_up", "w_down"]
    big_out = {}
    for nm, w, m, v_ in zip(big_names, big, big_m, big_v, strict=True):
        big_out[nm] = [t[None] for t in _adam_shard(*landed[nm], w, m[0], v_[0], "adam_" + nm)]

    small_g = [dh0[FRONT:PAD_ROWS], g_mix_norm, g_conv_w, g_conv_b, _ungroup_lanes(g_bias_p), _ungroup_lanes(g_alog_p),
               _ungroup_lanes(g_dsk_p), g_nw, g_ffn_norm, g_fconv_w, g_fconv_b, g_final, loss_acc[0:1, 0:1]]
    total = _unpack(_allreduce_small(_pack(small_g)), [t.shape for t in small_g])
    loss = total[12].reshape(())
    shard = lambda t, width: lax.dynamic_slice_in_dim(t, me * width, width, axis=1)
    small_names = ["meta_tokens", "mix_norm_w", "ssd_conv_w", "ssd_conv_b", "ssd_dt_bias", "ssd_A_log", "ssd_D", "ssd_norm_w",
                   "ffn_norm_w", "ffn_conv_w", "ffn_conv_b", "final_norm_w"]
    small_w = [meta_tokens, mix_norm_w, ssd_conv_w, ssd_conv_b, ssd_dt_bias, ssd_A_log, ssd_D, ssd_norm_w, ffn_norm_w,
               ffn_conv_w, ffn_conv_b, final_norm_w]
    small_m = [m_meta_tokens, m_mix_norm_w, m_ssd_conv_w, m_ssd_conv_b, m_ssd_dt_bias, m_ssd_A_log, m_ssd_D, m_ssd_norm_w,
               m_ffn_norm_w, m_ffn_conv_w, m_ffn_conv_b, m_final_norm_w]
    small_v = [v_meta_tokens, v_mix_norm_w, v_ssd_conv_w, v_ssd_conv_b, v_ssd_dt_bias, v_ssd_A_log, v_ssd_D, v_ssd_norm_w,
               v_ffn_norm_w, v_ffn_conv_w, v_ffn_conv_b, v_final_norm_w]
    grads = total[:12]
    grads[0] = shard(grads[0], meta_tokens.shape[1])
    grads[2] = shard(grads[2], ssd_conv_w.shape[2])
    grads[9] = shard(grads[9], ffn_conv_w.shape[2])
    grads = [t.reshape(w.shape) for t, w in zip(grads, small_w, strict=True)]
    shapes = [w.shape for w in small_w]
    upd = _adam_small(_pack(small_w), _pack(grads), _pack(small_m), _pack(small_v))
    small_out = {nm: [gr_] + [u[i] for u in (_unpack(t, shapes) for t in upd)]
                 for i, (nm, gr_) in enumerate(zip(small_names, grads, strict=True))}

    order = ["meta_tokens", "mix_norm_w", "w_in", "ssd_conv_w", "ssd_conv_b", "ssd_dt_bias", "ssd_A_log", "ssd_D", "ssd_norm_w",
             "w_branch_ssd", "w_branch_ret", "w_out", "ffn_norm_w", "w_up", "ffn_conv_w", "ffn_conv_b", "w_down", "final_norm_w"]
    res = {**big_out, **small_out}
    return (loss, grad_x, *[res[nm][0] for nm in order], *[res[nm][1] for nm in order], *[res[nm][2] for nm in order],
            *[res[nm][3] for nm in order])
```

```python
import functools
import math

import jax
import jax.numpy as jnp
import numpy as np
from jax import lax
from jax.experimental import pallas as pl
from jax.experimental.pallas import tpu as pltpu

F32 = jnp.float32
MXU_DTYPE = jnp.bfloat16
ACT_DTYPE = jnp.bfloat16
COMM_DTYPE = jnp.bfloat16

N_META = 16
CHUNK = 128
FRONT = CHUNK - N_META
PAD_ROWS = FRONT + N_META
EPS = 1e-6
N_DEV = 8

SSD_D_INNER = 2048
SSD_HEAD_DIM = 64
SSD_HEADS = 32
SSD_GROUPS = 4
SSD_HPG = SSD_HEADS // SSD_GROUPS
SSD_STATE = 128
SSD_CONV = 4
SSD_CONV_DIM = SSD_D_INNER + 2 * SSD_GROUPS * SSD_STATE
SSD_GW = SSD_D_INNER // SSD_GROUPS
RET_HEADS = 4
RET_QK = 256
RET_V = 512
ROPE_BASE = 10000.0
FFN_CONV = 3
HALO = 16
LANES = 128

ADAM_LR = 0.001
ADAM_B1 = 0.9
ADAM_B2 = 0.999
ADAM_EPS = 1e-08
ADAM_WD = 0.01
ADAM_STEP = 10

VMEM_LIMIT = 56 * 1024 * 1024
MESH = pl.DeviceIdType.MESH

NN = (((1,), (0,)), ((), ()))
NT = (((1,), (1,)), ((), ()))
TN = (((0,), (0,)), ((), ()))


def _params(sem):
    return pltpu.CompilerParams(dimension_semantics=sem, vmem_limit_bytes=VMEM_LIMIT)


def _mxu(a, b, dn):
    return lax.dot_general(a.astype(MXU_DTYPE), b.astype(MXU_DTYPE), dn, preferred_element_type=F32)


@functools.partial(jax.custom_vjp, nondiff_argnums=(2,))
def _dot(a, b, dn=NN):
    return _mxu(a, b, dn)


def _dot_fwd(a, b, dn):
    return _mxu(a, b, dn), (a, b)


def _dot_bwd(dn, res, g):
    a, b = res
    if dn == NN:
        return _mxu(g, b, NT), _mxu(a, g, TN)
    if dn == NT:
        return _mxu(g, b, NN), _mxu(g, a, TN)
    assert dn == TN
    return _mxu(b, g, NT), _mxu(a, g, NN)


_dot.defvjp(_dot_fwd, _dot_bwd)


def _silu(x):
    return x * jax.nn.sigmoid(x)


def _dsilu(x):
    s = jax.nn.sigmoid(x)
    return s * (1.0 + x * (1.0 - s))


def _row_tile(rows):
    return 640 if rows % 640 == 0 else 128


def _mm(a, b, *, mode, out_dtype, tm, tn, tk, name, add=None, after=None, b_k0=0, b_n0=0, n_out=None):
    if mode == "nn":
        (m, k), k2 = a.shape, b.shape[0]
        n = b.shape[1] if n_out is None else n_out
        assert b_n0 % tn == 0 and b_n0 + n <= b.shape[1]
    elif mode == "nt":
        (m, k), n = a.shape, b.shape[0]
        k2 = k if b_k0 % tk == 0 and b_k0 + k <= b.shape[1] else None
    else:
        (k, m), (k2, n) = a.shape, b.shape
    assert (b_k0 == 0 or mode == "nt") and ((b_n0 == 0 and n_out is None) or mode == "nn")
    assert k == k2 and m % tm == 0 and n % tn == 0 and k % tk == 0, (name, a.shape, b.shape, tm, tn, tk)
    kb0, nb0 = b_k0 // tk, b_n0 // tn
    nk = k // tk
    dn = {"nn": NN, "nt": NT, "tn": TN}[mode]
    has_add = add is not None
    n_in = 2 + has_add + (after is not None)

    def body(*refs):
        a_ref, b_ref = refs[0], refs[1]
        add_ref = refs[2] if has_add else None
        o_ref = refs[n_in]
        p = _dot(a_ref[...], b_ref[...], dn)
        if nk == 1:
            if has_add:
                p = p + add_ref[...]
            o_ref[...] = p.astype(out_dtype)
        else:
            acc_ref = refs[n_in + 1]
            kk = pl.program_id(2)

            @pl.when(kk == 0)
            def _():
                acc_ref[...] = p

            @pl.when(kk > 0)
            def _():
                acc_ref[...] += p

            @pl.when(kk == nk - 1)
            def _():
                r = acc_ref[...]
                if has_add:
                    r = r + add_ref[...]
                o_ref[...] = r.astype(out_dtype)

    if mode == "tn":
        a_spec = pl.BlockSpec((tk, tm), lambda j, i, kk: (kk, i))
    else:
        a_spec = pl.BlockSpec((tm, tk), lambda j, i, kk: (i, kk))
    if mode == "nt":
        b_spec = pl.BlockSpec((tn, tk), lambda j, i, kk: (j, kk + kb0))
    else:
        b_spec = pl.BlockSpec((tk, tn), lambda j, i, kk: (kk, j + nb0))
    o_spec = pl.BlockSpec((tm, tn), lambda j, i, kk: (i, j))
    in_specs = [a_spec, b_spec] + ([o_spec] if has_add else [])
    args = (a, b) + ((add,) if has_add else ())
    if after is not None:
        in_specs.append(pl.BlockSpec(memory_space=pl.ANY))
        args += (after,)
    return pl.pallas_call(
        body, name=name, grid=(n // tn, m // tm, nk), in_specs=in_specs, out_specs=o_spec,
        out_shape=jax.ShapeDtypeStruct((m, n), out_dtype),
        scratch_shapes=[pltpu.VMEM((tm, tn), F32)] if nk > 1 else [],
        compiler_params=_params(("parallel", "parallel", "arbitrary")),
    )(*args)


def _pick(n, cands):
    for c in cands:
        if n % c == 0:
            return c
    return n


def _rms_fwd(h, w, name):
    rows, d = h.shape
    tm = _row_tile(rows)

    def body(h_ref, w_ref, u_ref):
        x = h_ref[...]
        r = lax.rsqrt(jnp.mean(x * x, axis=-1, keepdims=True) + EPS)
        u_ref[...] = (x * r * w_ref[...]).astype(ACT_DTYPE)

    return pl.pallas_call(
        body, name=name, grid=(rows // tm,),
        in_specs=[pl.BlockSpec((tm, d), lambda i: (i, 0)), pl.BlockSpec((1, d), lambda i: (0, 0))],
        out_specs=pl.BlockSpec((tm, d), lambda i: (i, 0)),
        out_shape=jax.ShapeDtypeStruct((rows, d), ACT_DTYPE),
        compiler_params=_params(("parallel",)),
    )(h, w)


def _rms_bwd(du, h, w, dres, name):
    rows, d = h.shape
    tm = _row_tile(rows)

    def body(du_ref, h_ref, w_ref, dres_ref, dh_ref, dw_ref):
        x = h_ref[...]
        dy = du_ref[...].astype(F32)
        r = lax.rsqrt(jnp.mean(x * x, axis=-1, keepdims=True) + EPS)
        xhat = x * r
        dxn = dy * w_ref[...]
        dx = r * (dxn - xhat * jnp.mean(dxn * xhat, axis=-1, keepdims=True))
        dh_ref[...] = dres_ref[...] + dx

        @pl.when(pl.program_id(0) == 0)
        def _():
            dw_ref[...] = jnp.zeros_like(dw_ref)

        dw_ref[...] += jnp.sum(dy * xhat, axis=0, keepdims=True)

    return pl.pallas_call(
        body, name=name, grid=(rows // tm,),
        in_specs=[pl.BlockSpec((tm, d), lambda i: (i, 0)), pl.BlockSpec((tm, d), lambda i: (i, 0)),
                  pl.BlockSpec((1, d), lambda i: (0, 0)), pl.BlockSpec((tm, d), lambda i: (i, 0))],
        out_specs=[pl.BlockSpec((tm, d), lambda i: (i, 0)), pl.BlockSpec((1, d), lambda i: (0, 0))],
        out_shape=[jax.ShapeDtypeStruct((rows, d), F32), jax.ShapeDtypeStruct((1, d), F32)],
        compiler_params=_params(("arbitrary",)),
    )(du, h, w, dres)


def _loss_head(h2, tgt, w):
    rows, d = h2.shape
    tm = _row_tile(rows)

    def body(h_ref, t_ref, w_ref, dh_ref, loss_ref, dw_ref):
        i = pl.program_id(0)
        x = h_ref[...]
        r = lax.rsqrt(jnp.mean(x * x, axis=-1, keepdims=True) + EPS)
        xhat = x * r
        wv = w_ref[...]
        row = i * tm + lax.broadcasted_iota(jnp.int32, (tm, 1), 0)
        live = row >= PAD_ROWS
        diff = jnp.where(live, xhat * wv - t_ref[...], 0.0)
        dy = diff * (1.0 / d)
        dxn = dy * wv
        dh_ref[...] = r * (dxn - xhat * jnp.mean(dxn * xhat, axis=-1, keepdims=True))

        @pl.when(i == 0)
        def _():
            loss_ref[...] = jnp.zeros_like(loss_ref)
            dw_ref[...] = jnp.zeros_like(dw_ref)

        loss_ref[...] += 0.5 * jnp.sum(jnp.mean(diff * diff, axis=-1, keepdims=True))
        dw_ref[...] += jnp.sum(dy * xhat, axis=0, keepdims=True)

    return pl.pallas_call(
        body, name="loss_head", grid=(rows // tm,),
        in_specs=[pl.BlockSpec((tm, d), lambda i: (i, 0)), pl.BlockSpec((tm, d), lambda i: (i, 0)),
                  pl.BlockSpec((1, d), lambda i: (0, 0))],
        out_specs=[pl.BlockSpec((tm, d), lambda i: (i, 0)), pl.BlockSpec((8, LANES), lambda i: (0, 0)),
                   pl.BlockSpec((1, d), lambda i: (0, 0))],
        out_shape=[jax.ShapeDtypeStruct((rows, d), F32), jax.ShapeDtypeStruct((8, LANES), F32),
                   jax.ShapeDtypeStruct((1, d), F32)],
        compiler_params=_params(("arbitrary",)),
    )(h2, tgt, w)


def _prev_halo_spec(tm, width, col):
    return pl.BlockSpec((HALO, width), lambda j, i: (jnp.maximum(i * (tm // HALO) - 1, 0), col(j)))


def _next_halo_spec(tm, rows, width, col):
    last = rows // HALO - 1
    return pl.BlockSpec((HALO, width), lambda j, i: (jnp.minimum((i + 1) * (tm // HALO), last), col(j)))


def _conv_taps(cat, w_ref, b_ref, kw):
    acc = b_ref[...] + w_ref[kw - 1:kw, :] * cat
    for s in range(1, kw):
        acc = acc + w_ref[kw - 1 - s:kw - s, :] * pltpu.roll(cat, s, 0)
    return acc


def _conv_back(dpre, w_ref, kw):
    n = dpre.shape[0]
    acc = w_ref[kw - 1:kw, :] * dpre
    for s in range(1, kw):
        acc = acc + w_ref[kw - 1 - s:kw - s, :] * pltpu.roll(dpre, n - s, 0)
    return acc


def _ssd_conv_fwd(xbc, w, b):
    rows, width = xbc.shape
    tm, tc = _row_tile(rows), 512

    def body(x_ref, xp_ref, w_ref, b_ref, o_ref):
        i = pl.program_id(1)
        xp = jnp.where(i == 0, 0.0, xp_ref[...].astype(F32))
        cat = jnp.concatenate([xp, x_ref[...].astype(F32)], axis=0)
        pre = _conv_taps(cat, w_ref, b_ref, SSD_CONV)[HALO:]
        row = i * tm + lax.broadcasted_iota(jnp.int32, (tm, 1), 0)
        o_ref[...] = jnp.where(row >= FRONT, _silu(pre), 0.0).astype(ACT_DTYPE)

    main = pl.BlockSpec((tm, tc), lambda j, i: (i, j))
    par = lambda r: pl.BlockSpec((r, tc), lambda j, i: (0, j))
    return pl.pallas_call(
        body, name="ssd_conv_fwd", grid=(width // tc, rows // tm),
        in_specs=[main, _prev_halo_spec(tm, tc, lambda j: j), par(SSD_CONV), par(1)],
        out_specs=main, out_shape=jax.ShapeDtypeStruct((rows, width), ACT_DTYPE),
        compiler_params=_params(("parallel", "parallel")),
    )(xbc, xbc, w, b)


def _ssd_conv_bwd(xbc, dxc, w, b):
    rows, width = xbc.shape
    tm, tc = _row_tile(rows), 512
    kw = SSD_CONV

    def body(x_ref, xp_ref, xn_ref, d_ref, dn_ref, w_ref, b_ref, dx_ref, dw_ref, db_ref):
        i = pl.program_id(1)
        xp = jnp.where(i == 0, 0.0, xp_ref[...].astype(F32))
        cat = jnp.concatenate([xp, x_ref[...].astype(F32), xn_ref[...].astype(F32)], axis=0)
        pre = _conv_taps(cat, w_ref, b_ref, kw)[HALO:]
        row = i * tm + lax.broadcasted_iota(jnp.int32, (tm + HALO, 1), 0)
        live = (row >= FRONT) & (row < rows)
        dout = jnp.concatenate([d_ref[...].astype(F32), dn_ref[...].astype(F32)], axis=0)
        dpre = jnp.where(live, dout * _dsilu(pre), 0.0)
        dx_ref[...] = _conv_back(dpre, w_ref, kw)[:tm].astype(ACT_DTYPE)

        @pl.when(i == 0)
        def _():
            dw_ref[...] = jnp.zeros_like(dw_ref)
            db_ref[...] = jnp.zeros_like(db_ref)

        dmain = dpre[:tm]
        db_ref[...] += jnp.sum(dmain, axis=0, keepdims=True)
        for k in range(kw):
            s = kw - 1 - k
            xs = (pltpu.roll(cat, s, 0) if s else cat)[HALO:HALO + tm]
            dw_ref[k:k + 1, :] += jnp.sum(dmain * xs, axis=0, keepdims=True)

    main = pl.BlockSpec((tm, tc), lambda j, i: (i, j))
    par = lambda r: pl.BlockSpec((r, tc), lambda j, i: (0, j))
    col = lambda j: j
    return pl.pallas_call(
        body, name="ssd_conv_bwd", grid=(width // tc, rows // tm),
        in_specs=[main, _prev_halo_spec(tm, tc, col), _next_halo_spec(tm, rows, tc, col),
                  main, _next_halo_spec(tm, rows, tc, col), par(kw), par(1)],
        out_specs=[main, par(kw), par(1)],
        out_shape=[jax.ShapeDtypeStruct((rows, width), ACT_DTYPE), jax.ShapeDtypeStruct((kw, width), F32),
                   jax.ShapeDtypeStruct((1, width), F32)],
        compiler_params=_params(("parallel", "arbitrary")),
    )(xbc, xbc, xbc, dxc, dxc, w, b)


def _ffn_conv_fwd(up, w, b):
    rows, width = up.shape
    dff = width // 2
    tm, tc = _row_tile(rows), _pick(dff, (256, 128))
    nb = dff // tc
    kw = FFN_CONV

    def body(g_ref, gp_ref, v_ref, vp_ref, wg_ref, bg_ref, wv_ref, bv_ref, o_ref):
        i = pl.program_id(1)

        def pre(x_ref, xp_ref, w_ref, b_ref):
            xp = jnp.where(i == 0, 0.0, xp_ref[...].astype(F32))
            cat = jnp.concatenate([xp, x_ref[...].astype(F32)], axis=0)
            return _conv_taps(cat, w_ref, b_ref, kw)[HALO:]

        o_ref[...] = (_silu(pre(g_ref, gp_ref, wg_ref, bg_ref)) * pre(v_ref, vp_ref, wv_ref, bv_ref)).astype(ACT_DTYPE)

    gcol, vcol = (lambda j: j), (lambda j: j + nb)
    main = lambda col: pl.BlockSpec((tm, tc), lambda j, i: (i, col(j)))
    par = lambda r, col: pl.BlockSpec((r, tc), lambda j, i: (0, col(j)))
    return pl.pallas_call(
        body, name="ffn_conv_fwd", grid=(nb, rows // tm),
        in_specs=[main(gcol), _prev_halo_spec(tm, tc, gcol), main(vcol), _prev_halo_spec(tm, tc, vcol),
                  par(kw, gcol), par(1, gcol), par(kw, vcol), par(1, vcol)],
        out_specs=pl.BlockSpec((tm, tc), lambda j, i: (i, j)),
        out_shape=jax.ShapeDtypeStruct((rows, dff), ACT_DTYPE),
        compiler_params=_params(("parallel", "parallel")),
    )(up, up, up, up, w, b, w, b)


def _ffn_conv_bwd(up, dact, w, b):
    rows, width = up.shape
    dff = width // 2
    tm, tc = _row_tile(rows), _pick(dff, (256, 128))
    nb = dff // tc
    kw = FFN_CONV

    def body(g_ref, gp_ref, gn_ref, v_ref, vp_ref, vn_ref, d_ref, dn_ref, wg_ref, bg_ref, wv_ref, bv_ref,
             dxg_ref, dxv_ref, dwg_ref, dbg_ref, dwv_ref, dbv_ref):
        i = pl.program_id(1)

        def shifted(x_ref, xp_ref, xn_ref):
            xp = jnp.where(i == 0, 0.0, xp_ref[...].astype(F32))
            cat = jnp.concatenate([xp, x_ref[...].astype(F32), xn_ref[...].astype(F32)], axis=0)
            return [cat] + [pltpu.roll(cat, s, 0) for s in range(1, kw)]

        def taps(sh, w_ref, b_ref):
            acc = b_ref[...] + w_ref[kw - 1:kw, :] * sh[0]
            for s in range(1, kw):
                acc = acc + w_ref[kw - 1 - s:kw - s, :] * sh[s]
            return acc[HALO:]

        sh_g, sh_v = shifted(g_ref, gp_ref, gn_ref), shifted(v_ref, vp_ref, vn_ref)
        ag, av = taps(sh_g, wg_ref, bg_ref), taps(sh_v, wv_ref, bv_ref)
        row = i * tm + lax.broadcasted_iota(jnp.int32, (tm + HALO, 1), 0)
        dout = jnp.concatenate([d_ref[...].astype(F32), dn_ref[...].astype(F32)], axis=0)
        dout = jnp.where(row < rows, dout, 0.0)
        s = jax.nn.sigmoid(ag)
        silu = ag * s
        dpre_v = dout * silu
        dpre_g = dout * av * (s + silu * (1.0 - s))

        @pl.when(i == 0)
        def _():
            for r in (dwg_ref, dbg_ref, dwv_ref, dbv_ref):
                r[...] = jnp.zeros_like(r)

        for dpre, sh, w_ref, dx_ref, dw_ref, db_ref in ((dpre_g, sh_g, wg_ref, dxg_ref, dwg_ref, dbg_ref),
                                                          (dpre_v, sh_v, wv_ref, dxv_ref, dwv_ref, dbv_ref)):
            dx_ref[...] = _conv_back(dpre, w_ref, kw)[:tm].astype(ACT_DTYPE)
            dmain = dpre[:tm]
            db_ref[...] += jnp.sum(dmain, axis=0, keepdims=True)
            for k in range(kw):
                dw_ref[k:k + 1, :] += jnp.sum(dmain * sh[kw - 1 - k][HALO:HALO + tm], axis=0, keepdims=True)

    gcol, vcol = (lambda j: j), (lambda j: j + nb)
    main = lambda col: pl.BlockSpec((tm, tc), lambda j, i: (i, col(j)))
    par = lambda r, col: pl.BlockSpec((r, tc), lambda j, i: (0, col(j)))
    halos = lambda col: [_prev_halo_spec(tm, tc, col), _next_halo_spec(tm, rows, tc, col)]
    act_shape = jax.ShapeDtypeStruct((rows, dff), ACT_DTYPE)
    par_shapes = [jax.ShapeDtypeStruct((kw, dff), F32), jax.ShapeDtypeStruct((1, dff), F32)]
    return pl.pallas_call(
        body, name="ffn_conv_bwd", grid=(nb, rows // tm),
        in_specs=[main(gcol)] + halos(gcol) + [main(vcol)] + halos(vcol) + [main(gcol), _next_halo_spec(tm, rows, tc, gcol),
                  par(kw, gcol), par(1, gcol), par(kw, vcol), par(1, vcol)],
        out_specs=[main(gcol), main(gcol), par(kw, gcol), par(1, gcol), par(kw, gcol), par(1, gcol)],
        out_shape=[act_shape, act_shape] + par_shapes + par_shapes,
        compiler_params=_params(("parallel", "arbitrary")),
    )(up, up, up, up, up, up, dact, dact, w, b, w, b)


def _ssd_scalars(dtr, dt_bias, a_log, live):
    q = CHUNK
    pre = dtr + dt_bias
    dt = jnp.where(live, jax.nn.softplus(pre), 0.0)
    a_neg = -jnp.exp(a_log)
    li = lax.broadcasted_iota(jnp.int32, (q, q), 0)
    si = lax.broadcasted_iota(jnp.int32, (q, q), 1)
    causal = li >= si
    tri = jnp.where(causal, 1.0, 0.0).astype(F32)
    a_cs = jnp.dot(tri, dt * a_neg, precision=lax.Precision.HIGHEST, preferred_element_type=F32)
    return pre, dt, a_neg, a_cs, causal, tri


def _head_select():
    r = lax.broadcasted_iota(jnp.int32, (LANES, SSD_GW), 0)
    c = lax.broadcasted_iota(jnp.int32, (LANES, SSD_GW), 1)
    return jnp.where(c // SSD_HEAD_DIM == r, 1.0, 0.0).astype(MXU_DTYPE)


def _split(t, parts):
    out, rem = [], t
    for _ in range(parts):
        p = rem.astype(MXU_DTYPE)
        out.append(p)
        rem = rem - p.astype(F32)
    return out


def _head_cols(t, sel):
    return sum(_mxu(p, sel, NN) for p in _split(t, 2))


def _head_sums(t, sel):
    return sum(_mxu(p, sel, NT) for p in _split(t, 3))


def _half_masks():
    lane = lax.broadcasted_iota(jnp.int32, (CHUNK, LANES), 1)
    return lane < SSD_HEAD_DIM, lane >= SSD_HEAD_DIM


def _ssd_scan(xs, bm, cm, dtr, prev, dt_bias, a_log, d_skip, live):
    q = CHUNK
    sel = _head_select()
    _, dt, _, a_cs, causal, _ = _ssd_scalars(dtr, dt_bias, a_log, live)
    a_cs_t = a_cs.T
    a_end = a_cs[q - 1:q, :]
    e_x = _head_cols(jnp.exp(a_cs), sel)
    xdt = xs * _head_cols(dt, sel)
    cb = _dot(cm, bm, NT)
    y = _dot(cm, prev) * e_x + _head_cols(jnp.broadcast_to(d_skip, (8, LANES)), sel)[0:1] * xs
    new = prev * e_x[q - 1:q, :] + _dot(bm, xdt * _head_cols(jnp.exp(a_end - a_cs), sel), TN)
    masks = _half_masks()
    ys = []
    for pp in range(SSD_HPG // 2):
        xpair = xdt[:, pp * LANES:(pp + 1) * LANES]
        acc = jnp.zeros((q, LANES), F32)
        for half in range(2):
            hh = 2 * pp + half
            decay = jnp.exp(jnp.where(causal, a_cs[:, hh:hh + 1] - a_cs_t[hh:hh + 1, :], -jnp.inf))
            acc = acc + _dot(cb * decay, jnp.where(masks[half], xpair, 0.0))
        ys.append(acc)
    return y + jnp.concatenate(ys, axis=1), new


def _ssd_gate(y, z, nw):
    yz = y * _silu(z)
    return yz * lax.rsqrt(jnp.mean(yz * yz, axis=-1, keepdims=True) + EPS) * nw


def _ssd_scan_bwd(xs, bm, cm, dtr, prev, dt_bias, a_log, d_skip, live, dy, dnew):
    q = CHUNK
    sel = _head_select()
    pre, dt, a_neg, a_cs, causal, tri = _ssd_scalars(dtr, dt_bias, a_log, live)
    a_cs_t = a_cs.T
    a_end = a_cs[q - 1:q, :]
    dt_x, e_x, w_x = _head_cols(dt, sel), _head_cols(jnp.exp(a_cs), sel), _head_cols(jnp.exp(a_end - a_cs), sel)
    g_x, d_x = e_x[q - 1:q, :], _head_cols(jnp.broadcast_to(d_skip, (8, LANES)), sel)[0:1]
    xdt = xs * dt_x
    u = xdt * w_x
    cb = _mxu(cm, bm, NT)
    cs = _mxu(cm, prev, NN)
    dye = dy * e_x
    dcm = _mxu(dye, prev, NT)
    dprev = _mxu(cm, dye, TN) + dnew * g_x
    dacs_x = dye * cs
    dbm = _mxu(u, dnew, NT)
    du = _mxu(bm, dnew, NN)
    dw_x = du * u
    dacs_x = dacs_x - dw_x
    dend_x = jnp.sum(dw_x + dnew * prev * g_x, axis=0, keepdims=True)
    dxdt = du * w_x
    lane = lax.broadcasted_iota(jnp.int32, (q, LANES), 1)
    sub = lax.broadcasted_iota(jnp.int32, (q, LANES), 0)
    dcb = jnp.zeros((q, q), F32)
    dacs = jnp.zeros((q, LANES), F32)
    dacs_t = jnp.zeros((q, LANES), F32)
    masks = _half_masks()
    dxdt_p = []
    for pp in range(SSD_HPG // 2):
        ps = slice(pp * LANES, (pp + 1) * LANES)
        acc = jnp.zeros((q, LANES), F32)
        for half in range(2):
            hh = 2 * pp + half
            decay = jnp.exp(jnp.where(causal, a_cs[:, hh:hh + 1] - a_cs_t[hh:hh + 1, :], -jnp.inf))
            m = cb * decay
            dyh = jnp.where(masks[half], dy[:, ps], 0.0)
            dm = _mxu(dyh, xdt[:, ps], NT)
            acc = acc + _mxu(m, dyh, TN)
            dcb = dcb + dm * decay
            p = dm * m
            dacs = jnp.where(lane == hh, jnp.sum(p, axis=1, keepdims=True), dacs)
            dacs_t = jnp.where(sub == hh, jnp.sum(p, axis=0, keepdims=True), dacs_t)
        dxdt_p.append(acc)
    dcm = dcm + _mxu(dcb, bm, NN)
    dbm = dbm + _mxu(dcb, cm, TN)
    dxdt = dxdt + jnp.concatenate(dxdt_p, axis=1)
    dxs = dy * d_x + dxdt * dt_x
    rows_x = jnp.concatenate([dend_x, jnp.sum(dy * xs, axis=0, keepdims=True), jnp.zeros((6, SSD_GW), F32)], axis=0)
    rows = _head_sums(rows_x, sel)
    dacs = dacs - dacs_t.T + _head_sums(dacs_x, sel)
    dacs = dacs + jnp.where(sub == q - 1, rows[0:1], 0.0)
    tri_t = jnp.where(causal, 0.0, 1.0).astype(F32) + jnp.where(lane == sub, 1.0, 0.0)
    da = jnp.dot(tri_t, dacs, precision=lax.Precision.HIGHEST, preferred_element_type=F32)
    ddt = _head_sums(dxdt * xs, sel) + da * a_neg
    dalog = jnp.sum(da * dt, axis=0, keepdims=True) * a_neg
    ddtr = jnp.where(live, ddt * jax.nn.sigmoid(pre), 0.0)
    dbias = jnp.sum(ddtr, axis=0, keepdims=True)
    return dxs, dbm, dcm, ddtr, dprev, dbias, dalog, rows[1:2]


def _chunks_per_step(nc):
    return 5 if nc % 5 == 0 else 1


def _ssd_specs(rev, nc):
    per = _chunks_per_step(nc)
    steps = nc // per
    sidx = (lambda s: steps - 1 - s) if rev else (lambda s: s)
    nb_b = SSD_D_INNER // SSD_STATE
    row = lambda width, col=lambda g: g: pl.BlockSpec((per * CHUNK, width), lambda g, s: (sidx(s), col(g)))
    par = lambda width: pl.BlockSpec((1, width), lambda g, s: (0, g))
    state = lambda: pl.BlockSpec((per, 1, SSD_STATE, SSD_GW), lambda g, s: (sidx(s), g, 0, 0))
    xbc = [row(SSD_GW), row(SSD_STATE, lambda g: nb_b + g), row(SSD_STATE, lambda g: nb_b + SSD_GROUPS + g)]
    return per, steps, sidx, row, par, state, xbc


def _ssd_fwd(xbc_c, dtr, z, dt_bias, a_log, d_skip, nw):
    rows = z.shape[0]
    nc = rows // CHUNK
    per, steps, _, row, par, state, xbc = _ssd_specs(False, nc)

    def body(xs_ref, b_ref, c_ref, dt_ref, z_ref, bias_ref, al_ref, dk_ref, nw_ref, o_ref, y_ref, st_ref, carry):
        s = pl.program_id(1)

        @pl.when(s == 0)
        def _():
            carry[...] = jnp.zeros_like(carry)

        for j in range(per):
            rs = pl.ds(j * CHUNK, CHUNK)
            live = (s * per + j) * CHUNK + lax.broadcasted_iota(jnp.int32, (CHUNK, 1), 0) >= FRONT
            prev = carry[...]
            st_ref[j, 0] = prev
            y, new = _ssd_scan(xs_ref[rs, :].astype(F32), b_ref[rs, :].astype(F32), c_ref[rs, :].astype(F32), dt_ref[rs, :],
                               prev, bias_ref[...], al_ref[...], dk_ref[...], live)
            y_ref[rs, :] = y.astype(ACT_DTYPE)
            o_ref[rs, :] = _ssd_gate(y, z_ref[rs, :].astype(F32), nw_ref[...]).astype(ACT_DTYPE)
            carry[...] = new

    act = jax.ShapeDtypeStruct((rows, SSD_D_INNER), ACT_DTYPE)
    return pl.pallas_call(
        body, name="ssd_fwd", grid=(SSD_GROUPS, steps),
        in_specs=xbc + [row(LANES), row(SSD_GW), par(LANES), par(LANES), par(LANES), par(SSD_GW)],
        out_specs=[row(SSD_GW), row(SSD_GW), state()],
        out_shape=[act, act, jax.ShapeDtypeStruct((nc, SSD_GROUPS, SSD_STATE, SSD_GW), F32)],
        scratch_shapes=[pltpu.VMEM((SSD_STATE, SSD_GW), F32)],
        compiler_params=_params(("parallel", "arbitrary")),
    )(xbc_c, xbc_c, xbc_c, dtr, z, dt_bias, a_log, d_skip, nw)


def _ssd_gate_bwd(y, z, dout, nw):
    rows = y.shape[0]
    tm = _row_tile(rows)

    def body(y_ref, z_ref, do_ref, nw_ref, dy_ref, dz_ref, dnw_ref):
        yv, zv, dov = y_ref[...].astype(F32), z_ref[...].astype(F32), do_ref[...].astype(F32)
        s = jax.nn.sigmoid(zv)
        silu = zv * s
        yz = yv * silu
        r = lax.rsqrt(jnp.mean(yz * yz, axis=-1, keepdims=True) + EPS)
        yhat = yz * r
        dn = dov * nw_ref[...]
        dyz = r * (dn - yhat * jnp.mean(dn * yhat, axis=-1, keepdims=True))
        dy_ref[...] = (dyz * silu).astype(ACT_DTYPE)
        dz_ref[...] = (dyz * yv * (s + silu * (1.0 - s))).astype(ACT_DTYPE)

        @pl.when(pl.program_id(1) == 0)
        def _():
            dnw_ref[...] = jnp.zeros_like(dnw_ref)

        dnw_ref[...] += jnp.sum(dov * yhat, axis=0, keepdims=True)

    spec = pl.BlockSpec((tm, SSD_GW), lambda g, i: (i, g))
    par = pl.BlockSpec((1, SSD_GW), lambda g, i: (0, g))
    act = jax.ShapeDtypeStruct((rows, SSD_D_INNER), ACT_DTYPE)
    return pl.pallas_call(
        body, name="ssd_gate_bwd", grid=(SSD_GROUPS, rows // tm), in_specs=[spec, spec, spec, par],
        out_specs=[spec, spec, par], out_shape=[act, act, jax.ShapeDtypeStruct((1, SSD_D_INNER), F32)],
        compiler_params=_params(("parallel", "arbitrary")),
    )(y, z, dout, nw)


def _ssd_bwd(xbc_c, dtr, dt_bias, a_log, d_skip, states, dy):
    rows = dy.shape[0]
    nc = rows // CHUNK
    per, steps, sidx, row, par, state, xbc = _ssd_specs(True, nc)

    def body(xs_ref, b_ref, c_ref, dt_ref, bias_ref, al_ref, dk_ref, st_ref, dy_ref,
             dxs_ref, db_ref, dc_ref, ddt_ref, dbias_ref, dal_ref, ddk_ref, carry):
        s = pl.program_id(1)

        @pl.when(s == 0)
        def _():
            carry[...] = jnp.zeros_like(carry)
            for r in (dbias_ref, dal_ref, ddk_ref):
                r[...] = jnp.zeros_like(r)

        for j in reversed(range(per)):
            rs = pl.ds(j * CHUNK, CHUNK)
            live = (sidx(s) * per + j) * CHUNK + lax.broadcasted_iota(jnp.int32, (CHUNK, 1), 0) >= FRONT
            dxs, dbm, dcm, ddt, dprev, dbias, dal, ddk = _ssd_scan_bwd(
                xs_ref[rs, :].astype(F32), b_ref[rs, :].astype(F32), c_ref[rs, :].astype(F32), dt_ref[rs, :], st_ref[j, 0],
                bias_ref[...], al_ref[...], dk_ref[...], live, dy_ref[rs, :].astype(F32), carry[...])
            dxs_ref[rs, :] = dxs.astype(ACT_DTYPE)
            db_ref[rs, :] = dbm.astype(ACT_DTYPE)
            dc_ref[rs, :] = dcm.astype(ACT_DTYPE)
            ddt_ref[rs, :] = ddt
            carry[...] = dprev
            dbias_ref[...] += dbias
            dal_ref[...] += dal
            ddk_ref[...] += ddk

    bc = jax.ShapeDtypeStruct((rows, SSD_GROUPS * SSD_STATE), ACT_DTYPE)
    head = jax.ShapeDtypeStruct((1, SSD_GROUPS * LANES), F32)
    return pl.pallas_call(
        body, name="ssd_bwd", grid=(SSD_GROUPS, steps),
        in_specs=xbc + [row(LANES), par(LANES), par(LANES), par(LANES), state(), row(SSD_GW)],
        out_specs=[row(SSD_GW), row(SSD_STATE), row(SSD_STATE), row(LANES), par(LANES), par(LANES), par(LANES)],
        out_shape=[jax.ShapeDtypeStruct((rows, SSD_D_INNER), ACT_DTYPE), bc, bc,
                   jax.ShapeDtypeStruct((rows, SSD_GROUPS * LANES), F32), head, head, head],
        scratch_shapes=[pltpu.VMEM((SSD_STATE, SSD_GW), F32)],
        compiler_params=_params(("parallel", "arbitrary")),
    )(xbc_c, xbc_c, xbc_c, dtr, dt_bias, a_log, d_skip, states, dy)


def _rotary_tables(rows):
    pos = np.arange(rows, dtype=np.float32) - np.float32(FRONT)
    inv_freq = np.float32(ROPE_BASE) ** (-np.linspace(0.0, 1.0, RET_QK // 2, dtype=np.float32))
    ang = (pos[:, None] * inv_freq[None, :]).astype(np.float32).astype(np.float64)
    lgam = np.log(1.0 - 2.0 ** (-5.0 - np.arange(RET_HEADS, dtype=np.float64))).astype(np.float32)
    lgam = np.broadcast_to(lgam[:, None, None], (RET_HEADS, 8, LANES))
    return jnp.asarray(np.cos(ang).astype(np.float32)), jnp.asarray(np.sin(ang).astype(np.float32)), jnp.asarray(lgam)


def _rotary(t, cos, sin):
    half = t.shape[-1] // 2
    t1, t2 = t[:, :half], t[:, half:]
    return jnp.concatenate([t1 * cos - t2 * sin, t2 * cos + t1 * sin], axis=1)


def _ret_chunk(qh, kh, vh, gh, prev, cos, sin, lg):
    q = CHUNK
    qr = _rotary(qh, cos, sin)
    kr = _rotary(kh, cos, sin) * (RET_QK ** -0.5)
    li = lax.broadcasted_iota(jnp.int32, (q, q), 0)
    si = lax.broadcasted_iota(jnp.int32, (q, q), 1)
    dist = (li - si).astype(F32)
    decay = jnp.exp(jnp.where(li >= si, dist * lg, -jnp.inf))
    idx = lax.broadcasted_iota(jnp.int32, (q, 1), 0).astype(F32)
    scores = _dot(qr, kr, NT) * decay
    out = _dot(scores, vh)
    kv = _dot(kr * jnp.exp((q - 1.0 - idx) * lg), vh, TN)
    out = out + _dot(qr, prev) * jnp.exp((idx + 1.0) * lg)
    new = prev * jnp.exp(q * lg) + kv
    out = out * lax.rsqrt(jnp.mean(out * out, axis=-1, keepdims=True) + EPS)
    return _silu(gh) * out, new


def _ret_specs(rev, nc):
    per = _chunks_per_step(nc)
    steps = nc // per
    sidx = (lambda s: steps - 1 - s) if rev else (lambda s: s)
    row = lambda width: pl.BlockSpec((per * CHUNK, width), lambda h, s: (sidx(s), h))
    tab = lambda: pl.BlockSpec((per * CHUNK, RET_QK // 2), lambda h, s: (sidx(s), 0))
    lgs = lambda: pl.BlockSpec((1, 8, LANES), lambda h, s: (h, 0, 0))
    state = lambda: pl.BlockSpec((per, 1, RET_QK, RET_V), lambda h, s: (sidx(s), h, 0, 0))
    ins = [row(RET_QK), row(RET_QK), row(RET_V), row(RET_V), tab(), tab(), lgs()]
    return per, steps, row, state, ins


def _ret_fwd(q, k, v, g, cos, sin, lgam):
    rows = q.shape[0]
    nc = rows // CHUNK
    per, steps, row, state, ins = _ret_specs(False, nc)

    def body(q_ref, k_ref, v_ref, g_ref, cos_ref, sin_ref, lg_ref, y_ref, st_ref, carry):
        @pl.when(pl.program_id(1) == 0)
        def _():
            carry[...] = jnp.zeros_like(carry)

        for j in range(per):
            rs = pl.ds(j * CHUNK, CHUNK)
            prev = carry[...]
            st_ref[j, 0] = prev.astype(ACT_DTYPE)
            out, new = _ret_chunk(q_ref[rs, :].astype(F32), k_ref[rs, :].astype(F32), v_ref[rs, :].astype(F32),
                                  g_ref[rs, :].astype(F32), prev, cos_ref[rs, :], sin_ref[rs, :], lg_ref[0, 0:1, 0:1])
            y_ref[rs, :] = out.astype(ACT_DTYPE)
            carry[...] = new

    return pl.pallas_call(
        body, name="ret_fwd", grid=(RET_HEADS, steps), in_specs=ins, out_specs=[row(RET_V), state()],
        out_shape=[jax.ShapeDtypeStruct((rows, RET_HEADS * RET_V), ACT_DTYPE),
                   jax.ShapeDtypeStruct((nc, RET_HEADS, RET_QK, RET_V), ACT_DTYPE)],
        scratch_shapes=[pltpu.VMEM((RET_QK, RET_V), F32)],
        compiler_params=_params(("parallel", "arbitrary")),
    )(q, k, v, g, cos, sin, lgam)


def _ret_bwd(q, k, v, g, cos, sin, lgam, states, dy):
    rows = q.shape[0]
    nc = rows // CHUNK
    per, steps, row, state, ins = _ret_specs(True, nc)

    def body(q_ref, k_ref, v_ref, g_ref, cos_ref, sin_ref, lg_ref, st_ref, dy_ref, dq_ref, dk_ref, dv_ref, dg_ref, carry):
        @pl.when(pl.program_id(1) == 0)
        def _():
            carry[...] = jnp.zeros_like(carry)

        for j in reversed(range(per)):
            rs = pl.ds(j * CHUNK, CHUNK)
            fn = functools.partial(_ret_chunk, cos=cos_ref[rs, :], sin=sin_ref[rs, :], lg=lg_ref[0, 0:1, 0:1])
            _, vjp = jax.vjp(fn, q_ref[rs, :].astype(F32), k_ref[rs, :].astype(F32), v_ref[rs, :].astype(F32),
                             g_ref[rs, :].astype(F32), st_ref[j, 0].astype(F32))
            dq, dk, dv, dg, dprev = vjp((dy_ref[rs, :].astype(F32), carry[...]))
            dq_ref[rs, :] = dq.astype(ACT_DTYPE)
            dk_ref[rs, :] = dk.astype(ACT_DTYPE)
            dv_ref[rs, :] = dv.astype(ACT_DTYPE)
            dg_ref[rs, :] = dg.astype(ACT_DTYPE)
            carry[...] = dprev

    shp = lambda width: jax.ShapeDtypeStruct((rows, RET_HEADS * width), ACT_DTYPE)
    return pl.pallas_call(
        body, name="ret_bwd", grid=(RET_HEADS, steps), in_specs=ins + [state(), row(RET_V)],
        out_specs=[row(RET_QK), row(RET_QK), row(RET_V), row(RET_V)],
        out_shape=[shp(RET_QK), shp(RET_QK), shp(RET_V), shp(RET_V)],
        scratch_shapes=[pltpu.VMEM((RET_QK, RET_V), F32)],
        compiler_params=_params(("parallel", "arbitrary")),
    )(q, k, v, g, cos, sin, lgam, states, dy)


def _merge_fwd(bs, br, gs, gr):
    rows, d = bs.shape
    tm = _row_tile(rows)

    def body(bs_ref, br_ref, gs_ref, gr_ref, o_ref):
        o_ref[...] = (jax.nn.sigmoid(gs_ref[...].astype(F32)) * bs_ref[...].astype(F32)
                      + jax.nn.sigmoid(gr_ref[...].astype(F32)) * br_ref[...].astype(F32)).astype(ACT_DTYPE)

    spec = pl.BlockSpec((tm, d), lambda i: (i, 0))
    return pl.pallas_call(
        body, name="merge_fwd", grid=(rows // tm,), in_specs=[spec] * 4, out_specs=spec,
        out_shape=jax.ShapeDtypeStruct((rows, d), ACT_DTYPE), compiler_params=_params(("parallel",)),
    )(bs, br, gs, gr)


def _merge_bwd(dm, bs, br, gs, gr):
    rows, d = bs.shape
    tm = _row_tile(rows)

    def body(dm_ref, bs_ref, br_ref, gs_ref, gr_ref, dbs_ref, dbr_ref, dgs_ref, dgr_ref):
        dmv = dm_ref[...].astype(F32)
        for b_ref, g_ref, db_ref, dg_ref in ((bs_ref, gs_ref, dbs_ref, dgs_ref), (br_ref, gr_ref, dbr_ref, dgr_ref)):
            s = jax.nn.sigmoid(g_ref[...].astype(F32))
            db_ref[...] = (dmv * s).astype(ACT_DTYPE)
            dg_ref[...] = (dmv * b_ref[...].astype(F32) * s * (1.0 - s)).astype(ACT_DTYPE)

    spec = pl.BlockSpec((tm, d), lambda i: (i, 0))
    shp = jax.ShapeDtypeStruct((rows, d), ACT_DTYPE)
    return pl.pallas_call(
        body, name="merge_bwd", grid=(rows // tm,), in_specs=[spec] * 5, out_specs=[spec] * 4,
        out_shape=[shp] * 4, compiler_params=_params(("parallel",)),
    )(dm, bs, br, gs, gr)


def _place():
    x, y, c = lax.axis_index("x"), lax.axis_index("y"), lax.axis_index("c")
    return x, y, c


def _slot(p):
    return 4 * p[0] + 2 * p[1] + p[2]


def _allgather(arrs, name):
    n = len(arrs)
    any_spec = pl.BlockSpec(memory_space=pl.ANY)

    def body(*refs):
        ins, outs = refs[:n], refs[n:2 * n]
        send_sems, recv_sems, local_sems = refs[2 * n:]
        x, y, c = _place()
        me, sibling = (x, y, c), (x, y, 1 - c)
        chips = [(1 - x, y), (x, 1 - y), (1 - x, 1 - y)]

        def copy(a, k, block, to, src=None):
            dst = outs[a].at[_slot(block)]
            return pltpu.make_async_remote_copy(
                src_ref=dst if src is None else src, dst_ref=dst, send_sem=send_sems.at[a * 7 + k],
                recv_sem=recv_sems.at[a * 7 + k], device_id=to, device_id_type=MESH)

        mine, first, passed = [], [], []
        for a in range(n):
            cp = pltpu.make_async_copy(ins[a], outs[a].at[_slot(me)], local_sems.at[a])
            cp.start()
            mine.append(cp)
            first.append(copy(a, 0, me, sibling, src=ins[a]))
            first += [copy(a, 1 + j, me, (*chip, c), src=ins[a]) for j, chip in enumerate(chips)]
        for cp in first:
            cp.start()
        for j, chip in enumerate(chips):
            for a in range(n):
                copy(a, 1 + j, (*chip, c), me).wait_recv()
                cp = copy(a, 4 + j, (*chip, c), sibling)
                cp.start()
                passed.append(cp)
        for a in range(n):
            copy(a, 0, sibling, me).wait_recv()
            for j, chip in enumerate(chips):
                copy(a, 4 + j, (*chip, 1 - c), me).wait_recv()
        for cp in first + passed:
            cp.wait_send()
        for cp in mine:
            cp.wait()

    return pl.pallas_call(
        body, name=name, in_specs=[any_spec] * n, out_specs=[any_spec] * n,
        out_shape=[jax.ShapeDtypeStruct((N_DEV,) + a.shape, a.dtype) for a in arrs],
        scratch_shapes=[pltpu.SemaphoreType.DMA((7 * n,)), pltpu.SemaphoreType.DMA((7 * n,)), pltpu.SemaphoreType.DMA((n,))],
    )(*arrs)


def _peers():
    x, y, c = _place()
    return (x, y, c), [(x ^ dx, y ^ dy, c ^ dc) for dx in (0, 1) for dy in (0, 1) for dc in (0, 1)][1:]


def _exchange_copies(srcs, lands, send_sems, recv_sems, scatter, sender):
    me, peers = _peers()
    out = []
    for a, (src, land) in enumerate(zip(srcs, lands, strict=True)):
        for k, peer in enumerate(peers):
            src_ref = src.at[_slot(peer)] if scatter else src
            out.append(pltpu.make_async_remote_copy(
                src_ref=src_ref, dst_ref=land.at[_slot(me if sender else peer)], send_sem=send_sems.at[a * 7 + k],
                recv_sem=recv_sems.at[a * 7 + k], device_id=peer, device_id_type=MESH))
    return out


_HBM = pl.BlockSpec(memory_space=pltpu.HBM)
_SEM = pl.BlockSpec(memory_space=pltpu.SEMAPHORE)
_EFFECT = pltpu.SideEffectType.DATAFLOW_SIDE_EFFECTING


def _exchange_start(srcs, scatter, name):
    n = len(srcs)
    land_shapes = [s.shape if scatter else (N_DEV,) + s.shape for s in srcs]

    def body(*refs):
        for cp in _exchange_copies(refs[:n], refs[n:2 * n], refs[2 * n], refs[2 * n + 1], scatter, True):
            cp.start()
        refs[-1][...] = jnp.zeros_like(refs[-1])

    args = [pltpu.with_memory_space_constraint(s, pltpu.HBM) for s in srcs]
    args += [pltpu.with_memory_space_constraint(lax.empty(shp, s.dtype), pltpu.HBM) for s, shp in zip(srcs, land_shapes)]
    outs = pl.pallas_call(
        body, name=name,
        out_shape=(pltpu.SemaphoreType.DMA((7 * n,)), pltpu.SemaphoreType.DMA((7 * n,)))
        + tuple(pltpu.HBM(a.shape, a.dtype) for a in args) + (jax.ShapeDtypeStruct((8, LANES), F32),),
        in_specs=[_HBM] * (2 * n), out_specs=(_SEM, _SEM) + (_HBM,) * (2 * n) + (pl.BlockSpec(memory_space=pltpu.VMEM),),
        input_output_aliases={i: 2 + i for i in range(2 * n)},
        compiler_params=pltpu.CompilerParams(has_side_effects=_EFFECT),
    )(*args)
    return outs[:-1], outs[-1]


def _exchange_wait(handle, scatter, after, name):
    n = (len(handle) - 2) // 2
    thru = handle[2:]

    def body(*refs):
        for cp in _exchange_copies(refs[:n], refs[n:2 * n], refs[2 * n], refs[2 * n + 1], scatter, False):
            cp.wait_send()
            cp.wait_recv()

    outs = pl.pallas_call(
        body, name=name, out_shape=tuple(pltpu.HBM(t.shape, t.dtype) for t in thru),
        in_specs=[_HBM] * (2 * n) + [_SEM, _SEM, pl.BlockSpec(memory_space=pl.ANY)], out_specs=(_HBM,) * (2 * n),
        input_output_aliases={i: i for i in range(2 * n)},
        compiler_params=pltpu.CompilerParams(has_side_effects=_EFFECT),
    )(*thru, handle[0], handle[1], after)
    return list(outs[:n]), list(outs[n:])


def _allreduce_small(pack):
    rows, lanes = pack.shape

    def body(x_ref, o_ref, buf, send_sems, recv_sems):
        x, y, c = _place()
        me, sibling = (x, y, c), (x, y, 1 - c)
        chips = [(1 - x, y), (x, 1 - y), (1 - x, 1 - y)]

        def copy(k, block, to, src=None):
            dst = buf.at[_slot(block)]
            return pltpu.make_async_remote_copy(
                src_ref=dst if src is None else src, dst_ref=dst, send_sem=send_sems.at[k], recv_sem=recv_sems.at[k],
                device_id=to, device_id_type=MESH)

        buf[_slot(me)] = x_ref[...]
        first = [copy(0, me, sibling, src=x_ref)]
        first += [copy(1 + j, me, (*chip, c), src=x_ref) for j, chip in enumerate(chips)]
        for cp in first:
            cp.start()
        passed = [copy(4 + j, (*chip, c), sibling) for j, chip in enumerate(chips)]
        for j, chip in enumerate(chips):
            copy(1 + j, (*chip, c), me).wait_recv()
            passed[j].start()
        copy(0, sibling, me).wait_recv()
        for j, chip in enumerate(chips):
            copy(4 + j, (*chip, 1 - c), me).wait_recv()
        for cp in first + passed:
            cp.wait_send()
        acc = buf[0]
        for i in range(1, N_DEV):
            acc = acc + buf[i]
        o_ref[...] = acc

    vmem = pl.BlockSpec(memory_space=pltpu.VMEM)
    return pl.pallas_call(
        body, name="allreduce_small", in_specs=[vmem], out_specs=vmem,
        out_shape=jax.ShapeDtypeStruct((rows, lanes), F32),
        scratch_shapes=[pltpu.VMEM((N_DEV, rows, lanes), F32), pltpu.SemaphoreType.DMA((7,)), pltpu.SemaphoreType.DMA((7,))],
        compiler_params=pltpu.CompilerParams(vmem_limit_bytes=VMEM_LIMIT),
    )(pack)


def _adamw(w, g, m, v):
    m = ADAM_B1 * m + (1.0 - ADAM_B1) * g
    v = ADAM_B2 * v + (1.0 - ADAM_B2) * jnp.square(g)
    m_hat = m / (1.0 - ADAM_B1 ** ADAM_STEP)
    v_hat = v / (1.0 - ADAM_B2 ** ADAM_STEP)
    delta = -ADAM_LR * (m_hat / (jnp.sqrt(v_hat) + ADAM_EPS) + ADAM_WD * w)
    return delta, m, v


def _adam_shard(own, parts, w, m, v, name):
    r, c = w.shape
    tr = _pick(r, (128, 64, 32, 16, 8))

    def body(own_ref, p_ref, w_ref, m_ref, v_ref, g_ref, d_ref, nm_ref, nv_ref):
        _, peers = _peers()
        g = own_ref[...].astype(F32)
        for peer in peers:
            g = g + p_ref[_slot(peer)].astype(F32)
        g_ref[...] = g
        d_ref[...], nm_ref[...], nv_ref[...] = _adamw(w_ref[...], g, m_ref[...], v_ref[...])

    spec = pl.BlockSpec((tr, c), lambda i: (i, 0))
    shp = jax.ShapeDtypeStruct((r, c), F32)
    return pl.pallas_call(
        body, name=name, grid=(r // tr,),
        in_specs=[spec, pl.BlockSpec((N_DEV, tr, c), lambda i: (0, i, 0)), spec, spec, spec], out_specs=[spec] * 4,
        out_shape=[shp] * 4, compiler_params=_params(("parallel",)),
    )(own, parts, w, m, v)


def _adam_small(w, g, m, v):
    r, c = w.shape

    def body(w_ref, g_ref, m_ref, v_ref, d_ref, nm_ref, nv_ref):
        d_ref[...], nm_ref[...], nv_ref[...] = _adamw(w_ref[...], g_ref[...], m_ref[...], v_ref[...])

    shp = jax.ShapeDtypeStruct((r, c), F32)
    return pl.pallas_call(body, name="adam_small", out_shape=[shp] * 3)(w, g, m, v)


def _pack(arrs):
    rows = []
    for a in arrs:
        flat = a.reshape(-1).astype(F32)
        rows.append(jnp.pad(flat, (0, (-flat.shape[0]) % (8 * LANES))).reshape(-1, LANES))
    return jnp.concatenate(rows, axis=0)


def _unpack(pack, shapes):
    out, r = [], 0
    for s in shapes:
        size = math.prod(s)
        nr = -(-size // (8 * LANES)) * 8
        out.append(pack[r:r + nr].reshape(-1)[:size].reshape(s))
        r += nr
    return out


def _group_lanes(t):
    lead = t.shape[:-1]
    t = t.reshape(lead + (SSD_GROUPS, SSD_HPG))
    t = jnp.pad(t, [(0, 0)] * len(lead) + [(0, 0), (0, LANES - SSD_HPG)])
    return t.reshape(lead + (SSD_GROUPS * LANES,))


def _ungroup_lanes(t):
    lead = t.shape[:-1]
    return t.reshape(lead + (SSD_GROUPS, LANES))[..., :SSD_HPG].reshape(lead + (SSD_HEADS,))


def kernel(x, meta_tokens, mix_norm_w, w_in, ssd_conv_w, ssd_conv_b, ssd_dt_bias, ssd_A_log, ssd_D, ssd_norm_w, w_branch_ssd, w_branch_ret, w_out, ffn_norm_w, w_up, ffn_conv_w, ffn_conv_b, w_down, final_norm_w, loss_target, m_meta_tokens, m_mix_norm_w, m_w_in, m_ssd_conv_w, m_ssd_conv_b, m_ssd_dt_bias, m_ssd_A_log, m_ssd_D, m_ssd_norm_w, m_w_branch_ssd, m_w_branch_ret, m_w_out, m_ffn_norm_w, m_w_up, m_ffn_conv_w, m_ffn_conv_b, m_w_down, m_final_norm_w, v_meta_tokens, v_mix_norm_w, v_w_in, v_ssd_conv_w, v_ssd_conv_b, v_ssd_dt_bias, v_ssd_A_log, v_ssd_D, v_ssd_norm_w, v_w_branch_ssd, v_w_branch_ret, v_w_out, v_ffn_norm_w, v_w_up, v_ffn_conv_w, v_ffn_conv_b, v_w_down, v_final_norm_w):
    seq, d = x.shape[1], x.shape[2]
    rows = seq + PAD_ROWS
    tm = _row_tile(rows)
    me = _slot(_place())
    d_ff = w_down.shape[1] * N_DEV

    big = [w_in[0], w_branch_ssd[0], w_branch_ret[0], w_out[0], w_up[0], w_down[0]]
    first = _allgather([w_in[0].astype(COMM_DTYPE), meta_tokens, ssd_conv_w[0], ffn_conv_w[0]], "gather_first")
    rest_src = [b.astype(COMM_DTYPE) for b in big[1:]]
    rest_handle, rest_token = _exchange_start(rest_src, False, "gather_rest_start")
    cols = lambda t: jnp.transpose(t, (1, 0, 2)).reshape(t.shape[1], -1)
    rws = lambda t: t.reshape(-1, t.shape[2])
    w_in_f, conv_w, fconv_w = cols(first[0]), cols(first[2]), cols(first[3])
    meta_full = cols(first[1]) + rest_token[0, 0]
    widths = [SSD_D_INNER, SSD_CONV_DIM, SSD_HEADS, RET_HEADS * RET_QK, RET_HEADS * RET_QK, RET_HEADS * RET_V,
              RET_HEADS * RET_V, d, d]
    offs = [0]
    for wd in widths:
        offs.append(offs[-1] + wd)
    seg = [w_in_f[:, offs[i]:offs[i + 1]] for i in range(9)]
    seg[2] = _group_lanes(seg[2])
    order_p = [0, 1, 3, 4, 5, 6, 7, 8, 2]
    w_in_p = jnp.concatenate([seg[i] for i in order_p], axis=1)
    offs_p = [0]
    for i in order_p:
        offs_p.append(offs_p[-1] + seg[i].shape[1])
    in_p = offs_p[-1]

    h0 = jnp.concatenate([jnp.zeros((FRONT, d), F32), meta_full, x[0]], axis=0)
    u1 = _rms_fwd(h0, mix_norm_w, "rms1")
    proj = [_mm(u1, w_in_p, mode="nn", out_dtype=F32 if i == 2 else ACT_DTYPE, tm=tm, tk=d, name=f"in_proj_{i}",
                tn=_pick(offs_p[j + 1] - offs_p[j], (1024, 512)), b_n0=offs_p[j], n_out=offs_p[j + 1] - offs_p[j])
            for j, i in enumerate(order_p)]
    z, xbc, q, k, v, g, gs, gr, dtr = proj
    xbc_c = _ssd_conv_fwd(xbc, conv_w, ssd_conv_b)
    bias_p, alog_p, dsk_p = _group_lanes(ssd_dt_bias), _group_lanes(ssd_A_log), _group_lanes(ssd_D)
    y_ssd, y_scan, ssd_states = _ssd_fwd(xbc_c, dtr, z, bias_p, alog_p, dsk_p, ssd_norm_w)
    cos, sin, lgam = _rotary_tables(rows)
    y_ret, ret_states = _ret_fwd(q, k, v, g, cos, sin, lgam)
    rest_own, rest = _exchange_wait(rest_handle, False, y_ret, "gather_rest_wait")
    rest = [lax.dynamic_update_index_in_dim(land, own, me, 0) for land, own in zip(rest, rest_own, strict=True)]
    w_bs, w_br, w_o, w_up_f, w_dn = rws(rest[0]), rws(rest[1]), rws(rest[2]), cols(rest[3]), rws(rest[4])
    bs = _mm(y_ssd, w_bs, mode="nn", out_dtype=ACT_DTYPE, tm=tm, tn=d, tk=SSD_D_INNER, name="branch_ssd")
    br = _mm(y_ret, w_br, mode="nn", out_dtype=ACT_DTYPE, tm=tm, tn=d, tk=RET_HEADS * RET_V, name="branch_ret")
    merged = _merge_fwd(bs, br, gs, gr)
    h1 = _mm(merged, w_o, mode="nn", out_dtype=F32, tm=tm, tn=d, tk=d, name="out_proj", add=h0)
    u2 = _rms_fwd(h1, ffn_norm_w, "rms2")
    up = _mm(u2, w_up_f, mode="nn", out_dtype=ACT_DTYPE, tm=tm, tn=_pick(2 * d_ff, (1408, 512)), tk=d, name="ffn_up")
    act = _ffn_conv_fwd(up, fconv_w, ffn_conv_b)
    h2 = _mm(act, w_dn, mode="nn", out_dtype=F32, tm=tm, tn=d, tk=d_ff, name="ffn_down", add=h1)
    tgt = jnp.pad(loss_target[0], ((PAD_ROWS, 0), (0, 0)))
    dh2, loss_acc, g_final = _loss_head(h2, tgt, final_norm_w.reshape(1, d))

    tff = _pick(d_ff, (1408, 256))
    tkr = _pick(rows, (1664, 128))
    cparts = lambda t, n=N_DEV: jnp.transpose(t.reshape(t.shape[0], n, -1), (1, 0, 2)).astype(COMM_DTYPE)
    rparts = lambda t: t.reshape(N_DEV, -1, t.shape[1]).astype(COMM_DTYPE)
    d_act = _mm(dh2, w_dn, mode="nt", out_dtype=ACT_DTYPE, tm=tm, tn=tff, tk=d, name="d_act")
    g_w_dn = _mm(act, dh2, mode="tn", out_dtype=F32, tm=tff, tn=d, tk=tkr, name="g_w_down")
    c_dn = [rparts(g_w_dn)]
    h_dn, t_dn = _exchange_start(c_dn, True, "scatter_down_start")
    d_up_g, d_up_v, g_fcw_g, g_fcb_g, g_fcw_v, g_fcb_v = _ffn_conv_bwd(up, d_act, fconv_w, ffn_conv_b + t_dn[0, 0])
    g_fconv_w = jnp.concatenate([g_fcw_g, g_fcw_v], axis=1)
    g_fconv_b = jnp.concatenate([g_fcb_g, g_fcb_v], axis=1)
    g_w_up_g = _mm(u2, d_up_g, mode="tn", out_dtype=F32, tm=d, tn=tff, tk=tkr, name="g_w_up_gate")
    g_w_up_v = _mm(u2, d_up_v, mode="tn", out_dtype=F32, tm=d, tn=tff, tk=tkr, name="g_w_up_value")
    c_up = [jnp.concatenate([cparts(g_w_up_g, N_DEV // 2), cparts(g_w_up_v, N_DEV // 2)], axis=0)]
    h_up, t_up = _exchange_start(c_up, True, "scatter_up_start")
    du2 = _mm(d_up_g, w_up_f, mode="nt", out_dtype=F32, tm=tm, tn=d, tk=d_ff, name="d_u2_gate", after=t_up)
    du2 = _mm(d_up_v, w_up_f, mode="nt", out_dtype=F32, tm=tm, tn=d, tk=d_ff, name="d_u2_value", add=du2, b_k0=d_ff)
    dh1, g_ffn_norm = _rms_bwd(du2, h1, ffn_norm_w, dh2, "rms2_bwd")
    d_merged = _mm(dh1, w_o, mode="nt", out_dtype=F32, tm=tm, tn=d, tk=d, name="d_merged")
    g_w_o = _mm(merged, dh1, mode="tn", out_dtype=F32, tm=d, tn=d, tk=tkr, name="g_w_out")
    d_bs, d_br, d_gs, d_gr = _merge_bwd(d_merged, bs, br, gs, gr)
    d_yssd = _mm(d_bs, w_bs, mode="nt", out_dtype=ACT_DTYPE, tm=tm, tn=1024, tk=d, name="d_y_ssd")
    g_w_bs = _mm(y_ssd, d_bs, mode="tn", out_dtype=F32, tm=1024, tn=d, tk=tkr, name="g_w_branch_ssd")
    d_yret = _mm(d_br, w_br, mode="nt", out_dtype=ACT_DTYPE, tm=tm, tn=1024, tk=d, name="d_y_ret")
    g_w_br = _mm(y_ret, d_br, mode="tn", out_dtype=F32, tm=1024, tn=d, tk=tkr, name="g_w_branch_ret")
    c_mid = [rparts(g_w_bs), rparts(g_w_br), rparts(g_w_o)]
    h_mid, t_mid = _exchange_start(c_mid, True, "scatter_mid_start")
    d_yscan, d_z, g_nw = _ssd_gate_bwd(y_scan, z, d_yssd, ssd_norm_w + t_mid[0, 0])
    dxs, d_bm, d_cm, d_dtr, g_bias_p, g_alog_p, g_dsk_p = _ssd_bwd(xbc_c, dtr, bias_p, alog_p, dsk_p, ssd_states, d_yscan)
    d_xbc, g_conv_w, g_conv_b = _ssd_conv_bwd(xbc, jnp.concatenate([dxs, d_bm, d_cm], axis=1), conv_w, ssd_conv_b)
    d_q, d_k, d_v, d_g = _ret_bwd(q, k, v, g, cos, sin, lgam, ret_states, d_yret)
    dproj = jnp.concatenate([d_z, d_xbc, d_q, d_k, d_v, d_g, d_gs, d_gr, d_dtr.astype(ACT_DTYPE)], axis=1)
    g_w_in_p = _mm(u1, dproj, mode="tn", out_dtype=F32, tm=d, tn=_pick(in_p, (768, 512)), tk=tkr, name="g_w_in")
    gseg = [None] * 9
    for j, i in enumerate(order_p):
        gseg[i] = g_w_in_p[:, offs_p[j]:offs_p[j + 1]]
    gseg[2] = _ungroup_lanes(gseg[2])
    c_in = [cparts(jnp.concatenate(gseg, axis=1))]
    h_in, t_in = _exchange_start(c_in, True, "scatter_in_start")
    du1 = _mm(dproj, w_in_p, mode="nt", out_dtype=F32, tm=tm, tn=d, tk=_pick(in_p, (4608, 512)), name="d_u1", after=t_in)
    dh0, g_mix_norm = _rms_bwd(du1, h0, mix_norm_w, dh1, "rms1_bwd")
    grad_x = dh0[PAD_ROWS:][None]

    landed = {}
    for key, handle, names in (("in", h_in, ["w_in"]), ("mid", h_mid, ["w_branch_ssd", "w_branch_ret", "w_out"]),
                               ("up", h_up, ["w_up"]), ("down", h_dn, ["w_down"])):
        srcs, lands = _exchange_wait(handle, True, dh0, f"scatter_{key}_wait")
        for nm, land, src in zip(names, lands, srcs, strict=True):
            landed[nm] = (lax.dynamic_index_in_dim(src, me, 0, keepdims=False), land)
    big_m = [m_w_in, m_w_branch_ssd, m_w_branch_ret, m_w_out, m_w_up, m_w_down]
    big_v = [v_w_in, v_w_branch_ssd, v_w_branch_ret, v_w_out, v_w_up, v_w_down]
    big_names = ["w_in", "w_branch_ssd", "w_branch_ret", "w_out", "w_up", "w_down"]
    big_out = {}
    for nm, w, m, v_ in zip(big_names, big, big_m, big_v, strict=True):
        big_out[nm] = [t[None] for t in _adam_shard(*landed[nm], w, m[0], v_[0], "adam_" + nm)]

    small_g = [dh0[FRONT:PAD_ROWS], g_mix_norm, g_conv_w, g_conv_b, _ungroup_lanes(g_bias_p), _ungroup_lanes(g_alog_p),
               _ungroup_lanes(g_dsk_p), g_nw, g_ffn_norm, g_fconv_w, g_fconv_b, g_final, loss_acc[0:1, 0:1]]
    total = _unpack(_allreduce_small(_pack(small_g)), [t.shape for t in small_g])
    loss = total[12].reshape(())
    shard = lambda t, width: lax.dynamic_slice_in_dim(t, me * width, width, axis=1)
    small_names = ["meta_tokens", "mix_norm_w", "ssd_conv_w", "ssd_conv_b", "ssd_dt_bias", "ssd_A_log", "ssd_D", "ssd_norm_w",
                   "ffn_norm_w", "ffn_conv_w", "ffn_conv_b", "final_norm_w"]
    small_w = [meta_tokens, mix_norm_w, ssd_conv_w, ssd_conv_b, ssd_dt_bias, ssd_A_log, ssd_D, ssd_norm_w, ffn_norm_w,
               ffn_conv_w, ffn_conv_b, final_norm_w]
    small_m = [m_meta_tokens, m_mix_norm_w, m_ssd_conv_w, m_ssd_conv_b, m_ssd_dt_bias, m_ssd_A_log, m_ssd_D, m_ssd_norm_w,
               m_ffn_norm_w, m_ffn_conv_w, m_ffn_conv_b, m_final_norm_w]
    small_v = [v_meta_tokens, v_mix_norm_w, v_ssd_conv_w, v_ssd_conv_b, v_ssd_dt_bias, v_ssd_A_log, v_ssd_D, v_ssd_norm_w,
               v_ffn_norm_w, v_ffn_conv_w, v_ffn_conv_b, v_final_norm_w]
    grads = total[:12]
    grads[0] = shard(grads[0], meta_tokens.shape[1])
    grads[2] = shard(grads[2], ssd_conv_w.shape[2])
    grads[9] = shard(grads[9], ffn_conv_w.shape[2])
    grads = [t.reshape(w.shape) for t, w in zip(grads, small_w, strict=True)]
    shapes = [w.shape for w in small_w]
    upd = _adam_small(_pack(small_w), _pack(grads), _pack(small_m), _pack(small_v))
    small_out = {nm: [gr_] + [u[i] for u in (_unpack(t, shapes) for t in upd)]
                 for i, (nm, gr_) in enumerate(zip(small_names, grads, strict=True))}

    order = ["meta_tokens", "mix_norm_w", "w_in", "ssd_conv_w", "ssd_conv_b", "ssd_dt_bias", "ssd_A_log", "ssd_D", "ssd_norm_w",
             "w_branch_ssd", "w_branch_ret", "w_out", "ffn_norm_w", "w_up", "ffn_conv_w", "ffn_conv_b", "w_down", "final_norm_w"]
    res = {**big_out, **small_out}
    return (loss, grad_x, *[res[nm][0] for nm in order], *[res[nm][1] for nm in order], *[res[nm][2] for nm in order],
            *[res[nm][3] for nm in order])
```

```python
import functools
import math

import jax
import jax.numpy as jnp
import numpy as np
from jax import lax
from jax.experimental import pallas as pl
from jax.experimental.pallas import tpu as pltpu

F32 = jnp.float32
MXU_DTYPE = jnp.bfloat16
ACT_DTYPE = jnp.bfloat16
COMM_DTYPE = jnp.bfloat16

N_META = 16
CHUNK = 128
FRONT = CHUNK - N_META
PAD_ROWS = FRONT + N_META
EPS = 1e-6
N_DEV = 8

SSD_D_INNER = 2048
SSD_HEAD_DIM = 64
SSD_HEADS = 32
SSD_GROUPS = 4
SSD_HPG = SSD_HEADS // SSD_GROUPS
SSD_STATE = 128
SSD_CONV = 4
SSD_CONV_DIM = SSD_D_INNER + 2 * SSD_GROUPS * SSD_STATE
SSD_GW = SSD_D_INNER // SSD_GROUPS
RET_HEADS = 4
RET_QK = 256
RET_V = 512
ROPE_BASE = 10000.0
FFN_CONV = 3
HALO = 16
LANES = 128

ADAM_LR = 0.001
ADAM_B1 = 0.9
ADAM_B2 = 0.999
ADAM_EPS = 1e-08
ADAM_WD = 0.01
ADAM_STEP = 10

VMEM_LIMIT = 56 * 1024 * 1024
MESH = pl.DeviceIdType.MESH

NN = (((1,), (0,)), ((), ()))
NT = (((1,), (1,)), ((), ()))
TN = (((0,), (0,)), ((), ()))


def _params(sem):
    return pltpu.CompilerParams(dimension_semantics=sem, vmem_limit_bytes=VMEM_LIMIT)


def _mxu(a, b, dn):
    return lax.dot_general(a.astype(MXU_DTYPE), b.astype(MXU_DTYPE), dn, preferred_element_type=F32)


@functools.partial(jax.custom_vjp, nondiff_argnums=(2,))
def _dot(a, b, dn=NN):
    return _mxu(a, b, dn)


def _dot_fwd(a, b, dn):
    return _mxu(a, b, dn), (a, b)


def _dot_bwd(dn, res, g):
    a, b = res
    if dn == NN:
        return _mxu(g, b, NT), _mxu(a, g, TN)
    if dn == NT:
        return _mxu(g, b, NN), _mxu(g, a, TN)
    assert dn == TN
    return _mxu(b, g, NT), _mxu(a, g, NN)


_dot.defvjp(_dot_fwd, _dot_bwd)


def _silu(x):
    return x * jax.nn.sigmoid(x)


def _dsilu(x):
    s = jax.nn.sigmoid(x)
    return s * (1.0 + x * (1.0 - s))


def _row_tile(rows):
    return 640 if rows % 640 == 0 else 128


def _mm(a, b, *, mode, out_dtype, tm, tn, tk, name, add=None, after=None, b_k0=0, b_n0=0, n_out=None):
    if mode == "nn":
        (m, k), k2 = a.shape, b.shape[0]
        n = b.shape[1] if n_out is None else n_out
        assert b_n0 % tn == 0 and b_n0 + n <= b.shape[1]
    elif mode == "nt":
        (m, k), n = a.shape, b.shape[0]
        k2 = k if b_k0 % tk == 0 and b_k0 + k <= b.shape[1] else None
    else:
        (k, m), (k2, n) = a.shape, b.shape
    assert (b_k0 == 0 or mode == "nt") and ((b_n0 == 0 and n_out is None) or mode == "nn")
    assert k == k2 and m % tm == 0 and n % tn == 0 and k % tk == 0, (name, a.shape, b.shape, tm, tn, tk)
    kb0, nb0 = b_k0 // tk, b_n0 // tn
    nk = k // tk
    dn = {"nn": NN, "nt": NT, "tn": TN}[mode]
    has_add = add is not None
    n_in = 2 + has_add + (after is not None)

    def body(*refs):
        a_ref, b_ref = refs[0], refs[1]
        add_ref = refs[2] if has_add else None
        o_ref = refs[n_in]
        p = _dot(a_ref[...], b_ref[...], dn)
        if nk == 1:
            if has_add:
                p = p + add_ref[...]
            o_ref[...] = p.astype(out_dtype)
        else:
            acc_ref = refs[n_in + 1]
            kk = pl.program_id(2)

            @pl.when(kk == 0)
            def _():
                acc_ref[...] = p

            @pl.when(kk > 0)
            def _():
                acc_ref[...] += p

            @pl.when(kk == nk - 1)
            def _():
                r = acc_ref[...]
                if has_add:
                    r = r + add_ref[...]
                o_ref[...] = r.astype(out_dtype)

    if mode == "tn":
        a_spec = pl.BlockSpec((tk, tm), lambda j, i, kk: (kk, i))
    else:
        a_spec = pl.BlockSpec((tm, tk), lambda j, i, kk: (i, kk))
    if mode == "nt":
        b_spec = pl.BlockSpec((tn, tk), lambda j, i, kk: (j, kk + kb0))
    else:
        b_spec = pl.BlockSpec((tk, tn), lambda j, i, kk: (kk, j + nb0))
    o_spec = pl.BlockSpec((tm, tn), lambda j, i, kk: (i, j))
    in_specs = [a_spec, b_spec] + ([o_spec] if has_add else [])
    args = (a, b) + ((add,) if has_add else ())
    if after is not None:
        in_specs.append(pl.BlockSpec(memory_space=pl.ANY))
        args += (after,)
    return pl.pallas_call(
        body, name=name, grid=(n // tn, m // tm, nk), in_specs=in_specs, out_specs=o_spec,
        out_shape=jax.ShapeDtypeStruct((m, n), out_dtype),
        scratch_shapes=[pltpu.VMEM((tm, tn), F32)] if nk > 1 else [],
        compiler_params=_params(("parallel", "parallel", "arbitrary")),
    )(*args)


def _pick(n, cands):
    for c in cands:
        if n % c == 0:
            return c
    return n


def _rms_fwd(h, w, name):
    rows, d = h.shape
    tm = _row_tile(rows)

    def body(h_ref, w_ref, u_ref):
        x = h_ref[...]
        r = lax.rsqrt(jnp.mean(x * x, axis=-1, keepdims=True) + EPS)
        u_ref[...] = (x * r * w_ref[...]).astype(ACT_DTYPE)

    return pl.pallas_call(
        body, name=name, grid=(rows // tm,),
        in_specs=[pl.BlockSpec((tm, d), lambda i: (i, 0)), pl.BlockSpec((1, d), lambda i: (0, 0))],
        out_specs=pl.BlockSpec((tm, d), lambda i: (i, 0)),
        out_shape=jax.ShapeDtypeStruct((rows, d), ACT_DTYPE),
        compiler_params=_params(("parallel",)),
    )(h, w)


def _rms_bwd(du, h, w, dres, name):
    rows, d = h.shape
    tm = _row_tile(rows)

    def body(du_ref, h_ref, w_ref, dres_ref, dh_ref, dw_ref):
        x = h_ref[...]
        dy = du_ref[...].astype(F32)
        r = lax.rsqrt(jnp.mean(x * x, axis=-1, keepdims=True) + EPS)
        xhat = x * r
        dxn = dy * w_ref[...]
        dx = r * (dxn - xhat * jnp.mean(dxn * xhat, axis=-1, keepdims=True))
        dh_ref[...] = dres_ref[...] + dx

        @pl.when(pl.program_id(0) == 0)
        def _():
            dw_ref[...] = jnp.zeros_like(dw_ref)

        dw_ref[...] += jnp.sum(dy * xhat, axis=0, keepdims=True)

    return pl.pallas_call(
        body, name=name, grid=(rows // tm,),
        in_specs=[pl.BlockSpec((tm, d), lambda i: (i, 0)), pl.BlockSpec((tm, d), lambda i: (i, 0)),
                  pl.BlockSpec((1, d), lambda i: (0, 0)), pl.BlockSpec((tm, d), lambda i: (i, 0))],
        out_specs=[pl.BlockSpec((tm, d), lambda i: (i, 0)), pl.BlockSpec((1, d), lambda i: (0, 0))],
        out_shape=[jax.ShapeDtypeStruct((rows, d), F32), jax.ShapeDtypeStruct((1, d), F32)],
        compiler_params=_params(("arbitrary",)),
    )(du, h, w, dres)


def _loss_head(h2, tgt, w):
    rows, d = h2.shape
    tm = _row_tile(rows)

    def body(h_ref, t_ref, w_ref, dh_ref, loss_ref, dw_ref):
        i = pl.program_id(0)
        x = h_ref[...]
        r = lax.rsqrt(jnp.mean(x * x, axis=-1, keepdims=True) + EPS)
        xhat = x * r
        wv = w_ref[...]
        row = i * tm + lax.broadcasted_iota(jnp.int32, (tm, 1), 0)
        live = row >= PAD_ROWS
        diff = jnp.where(live, xhat * wv - t_ref[...], 0.0)
        dy = diff * (1.0 / d)
        dxn = dy * wv
        dh_ref[...] = r * (dxn - xhat * jnp.mean(dxn * xhat, axis=-1, keepdims=True))

        @pl.when(i == 0)
        def _():
            loss_ref[...] = jnp.zeros_like(loss_ref)
            dw_ref[...] = jnp.zeros_like(dw_ref)

        loss_ref[...] += 0.5 * jnp.sum(jnp.mean(diff * diff, axis=-1, keepdims=True))
        dw_ref[...] += jnp.sum(dy * xhat, axis=0, keepdims=True)

    return pl.pallas_call(
        body, name="loss_head", grid=(rows // tm,),
        in_specs=[pl.BlockSpec((tm, d), lambda i: (i, 0)), pl.BlockSpec((tm, d), lambda i: (i, 0)),
                  pl.BlockSpec((1, d), lambda i: (0, 0))],
        out_specs=[pl.BlockSpec((tm, d), lambda i: (i, 0)), pl.BlockSpec((8, LANES), lambda i: (0, 0)),
                   pl.BlockSpec((1, d), lambda i: (0, 0))],
        out_shape=[jax.ShapeDtypeStruct((rows, d), F32), jax.ShapeDtypeStruct((8, LANES), F32),
                   jax.ShapeDtypeStruct((1, d), F32)],
        compiler_params=_params(("arbitrary",)),
    )(h2, tgt, w)


def _prev_halo_spec(tm, width, col):
    return pl.BlockSpec((HALO, width), lambda j, i: (jnp.maximum(i * (tm // HALO) - 1, 0), col(j)))


def _next_halo_spec(tm, rows, width, col):
    last = rows // HALO - 1
    return pl.BlockSpec((HALO, width), lambda j, i: (jnp.minimum((i + 1) * (tm // HALO), last), col(j)))


def _conv_taps(cat, w_ref, b_ref, kw):
    acc = b_ref[...] + w_ref[kw - 1:kw, :] * cat
    for s in range(1, kw):
        acc = acc + w_ref[kw - 1 - s:kw - s, :] * pltpu.roll(cat, s, 0)
    return acc


def _conv_back(dpre, w_ref, kw):
    n = dpre.shape[0]
    acc = w_ref[kw - 1:kw, :] * dpre
    for s in range(1, kw):
        acc = acc + w_ref[kw - 1 - s:kw - s, :] * pltpu.roll(dpre, n - s, 0)
    return acc


def _xbc_proj_conv(u1, w_in_p, col0, w, b):
    rows, d = u1.shape
    width = w.shape[1]
    tm, tc = _row_tile(rows), 512
    cb0 = col0 // tc
    assert col0 % tc == 0

    def body(u_ref, m_ref, w_ref, b_ref, x_ref, o_ref, carry):
        i = pl.program_id(1)

        @pl.when(i == 0)
        def _():
            carry[...] = jnp.zeros_like(carry)

        xb = _mxu(u_ref[...], m_ref[...], NN).astype(ACT_DTYPE)
        x_ref[...] = xb
        x = xb.astype(F32)
        cat = jnp.concatenate([carry[...], x], axis=0)
        carry[...] = x[tm - HALO:, :]
        pre = _conv_taps(cat, w_ref, b_ref, SSD_CONV)[HALO:]
        row = i * tm + lax.broadcasted_iota(jnp.int32, (tm, 1), 0)
        o_ref[...] = jnp.where(row >= FRONT, _silu(pre), 0.0).astype(ACT_DTYPE)

    main = pl.BlockSpec((tm, tc), lambda j, i: (i, j))
    par = lambda r: pl.BlockSpec((r, tc), lambda j, i: (0, j))
    act = jax.ShapeDtypeStruct((rows, width), ACT_DTYPE)
    return pl.pallas_call(
        body, name="xbc_proj_conv", grid=(width // tc, rows // tm),
        in_specs=[pl.BlockSpec((tm, d), lambda j, i: (i, 0)), pl.BlockSpec((d, tc), lambda j, i: (0, cb0 + j)),
                  par(SSD_CONV), par(1)],
        out_specs=[main, main], out_shape=[act, act], scratch_shapes=[pltpu.VMEM((HALO, tc), F32)],
        compiler_params=_params(("parallel", "arbitrary")),
    )(u1, w_in_p, w, b)


def _ssd_conv_bwd(xbc, dxc, w, b):
    rows, width = xbc.shape
    tm, tc = _row_tile(rows), 512
    kw = SSD_CONV

    def body(x_ref, xp_ref, xn_ref, d_ref, dn_ref, w_ref, b_ref, dx_ref, dw_ref, db_ref):
        i = pl.program_id(1)
        xp = jnp.where(i == 0, 0.0, xp_ref[...].astype(F32))
        cat = jnp.concatenate([xp, x_ref[...].astype(F32), xn_ref[...].astype(F32)], axis=0)
        pre = _conv_taps(cat, w_ref, b_ref, kw)[HALO:]
        row = i * tm + lax.broadcasted_iota(jnp.int32, (tm + HALO, 1), 0)
        live = (row >= FRONT) & (row < rows)
        dout = jnp.concatenate([d_ref[...].astype(F32), dn_ref[...].astype(F32)], axis=0)
        dpre = jnp.where(live, dout * _dsilu(pre), 0.0)
        dx_ref[...] = _conv_back(dpre, w_ref, kw)[:tm].astype(ACT_DTYPE)

        @pl.when(i == 0)
        def _():
            dw_ref[...] = jnp.zeros_like(dw_ref)
            db_ref[...] = jnp.zeros_like(db_ref)

        dmain = dpre[:tm]
        db_ref[...] += jnp.sum(dmain, axis=0, keepdims=True)
        for k in range(kw):
            s = kw - 1 - k
            xs = (pltpu.roll(cat, s, 0) if s else cat)[HALO:HALO + tm]
            dw_ref[k:k + 1, :] += jnp.sum(dmain * xs, axis=0, keepdims=True)

    main = pl.BlockSpec((tm, tc), lambda j, i: (i, j))
    par = lambda r: pl.BlockSpec((r, tc), lambda j, i: (0, j))
    col = lambda j: j
    return pl.pallas_call(
        body, name="ssd_conv_bwd", grid=(width // tc, rows // tm),
        in_specs=[main, _prev_halo_spec(tm, tc, col), _next_halo_spec(tm, rows, tc, col),
                  main, _next_halo_spec(tm, rows, tc, col), par(kw), par(1)],
        out_specs=[main, par(kw), par(1)],
        out_shape=[jax.ShapeDtypeStruct((rows, width), ACT_DTYPE), jax.ShapeDtypeStruct((kw, width), F32),
                   jax.ShapeDtypeStruct((1, width), F32)],
        compiler_params=_params(("parallel", "arbitrary")),
    )(xbc, xbc, xbc, dxc, dxc, w, b)


def _ffn_up_conv(u2, w_up, w, b):
    rows, d = u2.shape
    width = w_up.shape[1]
    dff = width // 2
    tm, tc = _row_tile(rows), _pick(dff, (256, 128))
    nb = dff // tc
    kw = FFN_CONV

    def body(u_ref, mg_ref, mv_ref, wg_ref, bg_ref, wv_ref, bv_ref, ug_ref, uv_ref, o_ref, cg, cv):
        i = pl.program_id(1)

        @pl.when(i == 0)
        def _():
            cg[...] = jnp.zeros_like(cg)
            cv[...] = jnp.zeros_like(cv)

        def pre(m_ref, up_ref, carry, w_ref, b_ref):
            upb = _mxu(u_ref[...], m_ref[...], NN).astype(ACT_DTYPE)
            up_ref[...] = upb
            x = upb.astype(F32)
            cat = jnp.concatenate([carry[...], x], axis=0)
            carry[...] = x[tm - HALO:, :]
            return _conv_taps(cat, w_ref, b_ref, kw)[HALO:]

        ag = pre(mg_ref, ug_ref, cg, wg_ref, bg_ref)
        av = pre(mv_ref, uv_ref, cv, wv_ref, bv_ref)
        o_ref[...] = (_silu(ag) * av).astype(ACT_DTYPE)

    gcol, vcol = (lambda j: j), (lambda j: j + nb)
    mat = lambda col: pl.BlockSpec((d, tc), lambda j, i: (0, col(j)))
    par = lambda r, col: pl.BlockSpec((r, tc), lambda j, i: (0, col(j)))
    out = pl.BlockSpec((tm, tc), lambda j, i: (i, j))
    act = jax.ShapeDtypeStruct((rows, dff), ACT_DTYPE)
    return pl.pallas_call(
        body, name="ffn_up_conv", grid=(nb, rows // tm),
        in_specs=[pl.BlockSpec((tm, d), lambda j, i: (i, 0)), mat(gcol), mat(vcol),
                  par(kw, gcol), par(1, gcol), par(kw, vcol), par(1, vcol)],
        out_specs=[out, out, out], out_shape=[act, act, act],
        scratch_shapes=[pltpu.VMEM((HALO, tc), F32), pltpu.VMEM((HALO, tc), F32)],
        compiler_params=_params(("parallel", "arbitrary")),
    )(u2, w_up, w_up, w, b, w, b)


def _ffn_conv_bwd(up_g, up_v, dact, w, b, u2):
    rows, dff = up_g.shape
    d = u2.shape[1]
    tm, tc = _row_tile(rows), _pick(dff, (256, 128))
    nb = dff // tc
    kw = FFN_CONV

    def body(g_ref, gp_ref, gn_ref, v_ref, vp_ref, vn_ref, d_ref, dn_ref, wg_ref, bg_ref, wv_ref, bv_ref, u_ref,
             dxg_ref, dxv_ref, dwg_ref, dbg_ref, dwv_ref, dbv_ref, gwg_ref, gwv_ref):
        i = pl.program_id(1)

        def shifted(x_ref, xp_ref, xn_ref):
            xp = jnp.where(i == 0, 0.0, xp_ref[...].astype(F32))
            cat = jnp.concatenate([xp, x_ref[...].astype(F32), xn_ref[...].astype(F32)], axis=0)
            return [cat] + [pltpu.roll(cat, s, 0) for s in range(1, kw)]

        def taps(sh, w_ref, b_ref):
            acc = b_ref[...] + w_ref[kw - 1:kw, :] * sh[0]
            for s in range(1, kw):
                acc = acc + w_ref[kw - 1 - s:kw - s, :] * sh[s]
            return acc[HALO:]

        sh_g, sh_v = shifted(g_ref, gp_ref, gn_ref), shifted(v_ref, vp_ref, vn_ref)
        ag, av = taps(sh_g, wg_ref, bg_ref), taps(sh_v, wv_ref, bv_ref)
        row = i * tm + lax.broadcasted_iota(jnp.int32, (tm + HALO, 1), 0)
        dout = jnp.concatenate([d_ref[...].astype(F32), dn_ref[...].astype(F32)], axis=0)
        dout = jnp.where(row < rows, dout, 0.0)
        s = jax.nn.sigmoid(ag)
        silu = ag * s
        dpre_v = dout * silu
        dpre_g = dout * av * (s + silu * (1.0 - s))

        @pl.when(i == 0)
        def _():
            for r in (dwg_ref, dbg_ref, dwv_ref, dbv_ref, gwg_ref, gwv_ref):
                r[...] = jnp.zeros_like(r)

        for dpre, sh, w_ref, dx_ref, dw_ref, db_ref, gw_ref in (
                (dpre_g, sh_g, wg_ref, dxg_ref, dwg_ref, dbg_ref, gwg_ref), (dpre_v, sh_v, wv_ref, dxv_ref, dwv_ref, dbv_ref, gwv_ref)):
            dx = _conv_back(dpre, w_ref, kw)[:tm].astype(ACT_DTYPE)
            dx_ref[...] = dx
            gw_ref[...] += _mxu(u_ref[...], dx, TN)
            dmain = dpre[:tm]
            db_ref[...] += jnp.sum(dmain, axis=0, keepdims=True)
            for k in range(kw):
                dw_ref[k:k + 1, :] += jnp.sum(dmain * sh[kw - 1 - k][HALO:HALO + tm], axis=0, keepdims=True)

    gcol, vcol = (lambda j: j), (lambda j: j + nb)
    main = lambda col: pl.BlockSpec((tm, tc), lambda j, i: (i, col(j)))
    par = lambda r, col: pl.BlockSpec((r, tc), lambda j, i: (0, col(j)))
    halos = lambda col: [_prev_halo_spec(tm, tc, col), _next_halo_spec(tm, rows, tc, col)]
    act_shape = jax.ShapeDtypeStruct((rows, dff), ACT_DTYPE)
    par_shapes = [jax.ShapeDtypeStruct((kw, dff), F32), jax.ShapeDtypeStruct((1, dff), F32)]
    gw_shape = jax.ShapeDtypeStruct((d, dff), F32)
    return pl.pallas_call(
        body, name="ffn_conv_bwd", grid=(nb, rows // tm),
        in_specs=[main(gcol)] + halos(gcol) + [main(gcol)] + halos(gcol) + [main(gcol), _next_halo_spec(tm, rows, tc, gcol),
                  par(kw, gcol), par(1, gcol), par(kw, vcol), par(1, vcol), pl.BlockSpec((tm, d), lambda j, i: (i, 0))],
        out_specs=[main(gcol), main(gcol), par(kw, gcol), par(1, gcol), par(kw, gcol), par(1, gcol), par(d, gcol), par(d, gcol)],
        out_shape=[act_shape, act_shape] + par_shapes + par_shapes + [gw_shape, gw_shape],
        compiler_params=_params(("parallel", "arbitrary")),
    )(up_g, up_g, up_g, up_v, up_v, up_v, dact, dact, w, b, w, b, u2)


def _ssd_scalars(dtr, dt_bias, a_log, live):
    q = CHUNK
    pre = dtr + dt_bias
    dt = jnp.where(live, jax.nn.softplus(pre), 0.0)
    a_neg = -jnp.exp(a_log)
    li = lax.broadcasted_iota(jnp.int32, (q, q), 0)
    si = lax.broadcasted_iota(jnp.int32, (q, q), 1)
    causal = li >= si
    tri = jnp.where(causal, 1.0, 0.0).astype(F32)
    a_cs = jnp.dot(tri, dt * a_neg, precision=lax.Precision.HIGHEST, preferred_element_type=F32)
    return pre, dt, a_neg, a_cs, causal, tri


def _head_select():
    r = lax.broadcasted_iota(jnp.int32, (LANES, SSD_GW), 0)
    c = lax.broadcasted_iota(jnp.int32, (LANES, SSD_GW), 1)
    return jnp.where(c // SSD_HEAD_DIM == r, 1.0, 0.0).astype(MXU_DTYPE)


def _split(t, parts):
    out, rem = [], t
    for _ in range(parts):
        p = rem.astype(MXU_DTYPE)
        out.append(p)
        rem = rem - p.astype(F32)
    return out


def _head_cols(t, sel):
    return sum(_mxu(p, sel, NN) for p in _split(t, 2))


def _head_sums(t, sel):
    return sum(_mxu(p, sel, NT) for p in _split(t, 3))


def _half_masks():
    lane = lax.broadcasted_iota(jnp.int32, (CHUNK, LANES), 1)
    return lane < SSD_HEAD_DIM, lane >= SSD_HEAD_DIM


def _ssd_scan(xs, bm, cm, dtr, prev, dt_bias, a_log, d_skip, live):
    q = CHUNK
    sel = _head_select()
    _, dt, _, a_cs, causal, _ = _ssd_scalars(dtr, dt_bias, a_log, live)
    a_cs_t = a_cs.T
    a_end = a_cs[q - 1:q, :]
    e_x = _head_cols(jnp.exp(a_cs), sel)
    xdt = xs * _head_cols(dt, sel)
    cb = _dot(cm, bm, NT)
    y = _dot(cm, prev) * e_x + _head_cols(jnp.broadcast_to(d_skip, (8, LANES)), sel)[0:1] * xs
    new = prev * e_x[q - 1:q, :] + _dot(bm, xdt * _head_cols(jnp.exp(a_end - a_cs), sel), TN)
    masks = _half_masks()
    ys = []
    for pp in range(SSD_HPG // 2):
        xpair = xdt[:, pp * LANES:(pp + 1) * LANES]
        acc = jnp.zeros((q, LANES), F32)
        for half in range(2):
            hh = 2 * pp + half
            decay = jnp.exp(jnp.where(causal, a_cs[:, hh:hh + 1] - a_cs_t[hh:hh + 1, :], -jnp.inf))
            acc = acc + _dot(cb * decay, jnp.where(masks[half], xpair, 0.0))
        ys.append(acc)
    return y + jnp.concatenate(ys, axis=1), new


def _ssd_gate(y, z, nw):
    yz = y * _silu(z)
    return yz * lax.rsqrt(jnp.mean(yz * yz, axis=-1, keepdims=True) + EPS) * nw


def _ssd_scan_bwd(xs, bm, cm, dtr, prev, dt_bias, a_log, d_skip, live, dy, dnew):
    q = CHUNK
    sel = _head_select()
    pre, dt, a_neg, a_cs, causal, tri = _ssd_scalars(dtr, dt_bias, a_log, live)
    a_cs_t = a_cs.T
    a_end = a_cs[q - 1:q, :]
    dt_x, e_x, w_x = _head_cols(dt, sel), _head_cols(jnp.exp(a_cs), sel), _head_cols(jnp.exp(a_end - a_cs), sel)
    g_x, d_x = e_x[q - 1:q, :], _head_cols(jnp.broadcast_to(d_skip, (8, LANES)), sel)[0:1]
    xdt = xs * dt_x
    u = xdt * w_x
    cb = _mxu(cm, bm, NT)
    cs = _mxu(cm, prev, NN)
    dye = dy * e_x
    dcm = _mxu(dye, prev, NT)
    dprev = _mxu(cm, dye, TN) + dnew * g_x
    dacs_x = dye * cs
    dbm = _mxu(u, dnew, NT)
    du = _mxu(bm, dnew, NN)
    dw_x = du * u
    dacs_x = dacs_x - dw_x
    dend_x = jnp.sum(dw_x + dnew * prev * g_x, axis=0, keepdims=True)
    dxdt = du * w_x
    lane = lax.broadcasted_iota(jnp.int32, (q, LANES), 1)
    sub = lax.broadcasted_iota(jnp.int32, (q, LANES), 0)
    dcb = jnp.zeros((q, q), F32)
    dacs = jnp.zeros((q, LANES), F32)
    dacs_t = jnp.zeros((q, LANES), F32)
    masks = _half_masks()
    dxdt_p = []
    for pp in range(SSD_HPG // 2):
        ps = slice(pp * LANES, (pp + 1) * LANES)
        acc = jnp.zeros((q, LANES), F32)
        for half in range(2):
            hh = 2 * pp + half
            decay = jnp.exp(jnp.where(causal, a_cs[:, hh:hh + 1] - a_cs_t[hh:hh + 1, :], -jnp.inf))
            m = cb * decay
            dyh = jnp.where(masks[half], dy[:, ps], 0.0)
            dm = _mxu(dyh, xdt[:, ps], NT)
            acc = acc + _mxu(m, dyh, TN)
            dcb = dcb + dm * decay
            p = dm * m
            dacs = jnp.where(lane == hh, jnp.sum(p, axis=1, keepdims=True), dacs)
            dacs_t = jnp.where(sub == hh, jnp.sum(p, axis=0, keepdims=True), dacs_t)
        dxdt_p.append(acc)
    dcm = dcm + _mxu(dcb, bm, NN)
    dbm = dbm + _mxu(dcb, cm, TN)
    dxdt = dxdt + jnp.concatenate(dxdt_p, axis=1)
    dxs = dy * d_x + dxdt * dt_x
    rows_x = jnp.concatenate([dend_x, jnp.sum(dy * xs, axis=0, keepdims=True), jnp.zeros((6, SSD_GW), F32)], axis=0)
    rows = _head_sums(rows_x, sel)
    dacs = dacs - dacs_t.T + _head_sums(dacs_x, sel)
    dacs = dacs + jnp.where(sub == q - 1, rows[0:1], 0.0)
    tri_t = jnp.where(causal, 0.0, 1.0).astype(F32) + jnp.where(lane == sub, 1.0, 0.0)
    da = jnp.dot(tri_t, dacs, precision=lax.Precision.HIGHEST, preferred_element_type=F32)
    ddt = _head_sums(dxdt * xs, sel) + da * a_neg
    dalog = jnp.sum(da * dt, axis=0, keepdims=True) * a_neg
    ddtr = jnp.where(live, ddt * jax.nn.sigmoid(pre), 0.0)
    dbias = jnp.sum(ddtr, axis=0, keepdims=True)
    return dxs, dbm, dcm, ddtr, dprev, dbias, dalog, rows[1:2]


def _chunks_per_step(nc):
    return 5 if nc % 5 == 0 else 1


def _ssd_specs(rev, nc):
    per = _chunks_per_step(nc)
    steps = nc // per
    sidx = (lambda s: steps - 1 - s) if rev else (lambda s: s)
    nb_b = SSD_D_INNER // SSD_STATE
    row = lambda width, col=lambda g: g: pl.BlockSpec((per * CHUNK, width), lambda g, s: (sidx(s), col(g)))
    par = lambda width: pl.BlockSpec((1, width), lambda g, s: (0, g))
    state = lambda: pl.BlockSpec((per, 1, SSD_STATE, SSD_GW), lambda g, s: (sidx(s), g, 0, 0))
    xbc = [row(SSD_GW), row(SSD_STATE, lambda g: nb_b + g), row(SSD_STATE, lambda g: nb_b + SSD_GROUPS + g)]
    return per, steps, sidx, row, par, state, xbc


def _ssd_fwd(xbc_c, dtr, z, dt_bias, a_log, d_skip, nw):
    rows = z.shape[0]
    nc = rows // CHUNK
    per, steps, _, row, par, state, xbc = _ssd_specs(False, nc)

    def body(xs_ref, b_ref, c_ref, dt_ref, z_ref, bias_ref, al_ref, dk_ref, nw_ref, o_ref, y_ref, st_ref, carry):
        s = pl.program_id(1)

        @pl.when(s == 0)
        def _():
            carry[...] = jnp.zeros_like(carry)

        for j in range(per):
            rs = pl.ds(j * CHUNK, CHUNK)
            live = (s * per + j) * CHUNK + lax.broadcasted_iota(jnp.int32, (CHUNK, 1), 0) >= FRONT
            prev = carry[...]
            st_ref[j, 0] = prev
            y, new = _ssd_scan(xs_ref[rs, :].astype(F32), b_ref[rs, :].astype(F32), c_ref[rs, :].astype(F32), dt_ref[rs, :],
                               prev, bias_ref[...], al_ref[...], dk_ref[...], live)
            y_ref[rs, :] = y.astype(ACT_DTYPE)
            o_ref[rs, :] = _ssd_gate(y, z_ref[rs, :].astype(F32), nw_ref[...]).astype(ACT_DTYPE)
            carry[...] = new

    act = jax.ShapeDtypeStruct((rows, SSD_D_INNER), ACT_DTYPE)
    return pl.pallas_call(
        body, name="ssd_fwd", grid=(SSD_GROUPS, steps),
        in_specs=xbc + [row(LANES), row(SSD_GW), par(LANES), par(LANES), par(LANES), par(SSD_GW)],
        out_specs=[row(SSD_GW), row(SSD_GW), state()],
        out_shape=[act, act, jax.ShapeDtypeStruct((nc, SSD_GROUPS, SSD_STATE, SSD_GW), F32)],
        scratch_shapes=[pltpu.VMEM((SSD_STATE, SSD_GW), F32)],
        compiler_params=_params(("parallel", "arbitrary")),
    )(xbc_c, xbc_c, xbc_c, dtr, z, dt_bias, a_log, d_skip, nw)


def _ssd_gate_bwd(y, z, dout, nw):
    rows = y.shape[0]
    tm = _row_tile(rows)

    def body(y_ref, z_ref, do_ref, nw_ref, dy_ref, dz_ref, dnw_ref):
        yv, zv, dov = y_ref[...].astype(F32), z_ref[...].astype(F32), do_ref[...].astype(F32)
        s = jax.nn.sigmoid(zv)
        silu = zv * s
        yz = yv * silu
        r = lax.rsqrt(jnp.mean(yz * yz, axis=-1, keepdims=True) + EPS)
        yhat = yz * r
        dn = dov * nw_ref[...]
        dyz = r * (dn - yhat * jnp.mean(dn * yhat, axis=-1, keepdims=True))
        dy_ref[...] = (dyz * silu).astype(ACT_DTYPE)
        dz_ref[...] = (dyz * yv * (s + silu * (1.0 - s))).astype(ACT_DTYPE)

        @pl.when(pl.program_id(1) == 0)
        def _():
            dnw_ref[...] = jnp.zeros_like(dnw_ref)

        dnw_ref[...] += jnp.sum(dov * yhat, axis=0, keepdims=True)

    spec = pl.BlockSpec((tm, SSD_GW), lambda g, i: (i, g))
    par = pl.BlockSpec((1, SSD_GW), lambda g, i: (0, g))
    act = jax.ShapeDtypeStruct((rows, SSD_D_INNER), ACT_DTYPE)
    return pl.pallas_call(
        body, name="ssd_gate_bwd", grid=(SSD_GROUPS, rows // tm), in_specs=[spec, spec, spec, par],
        out_specs=[spec, spec, par], out_shape=[act, act, jax.ShapeDtypeStruct((1, SSD_D_INNER), F32)],
        compiler_params=_params(("parallel", "arbitrary")),
    )(y, z, dout, nw)


def _ssd_bwd(xbc_c, dtr, dt_bias, a_log, d_skip, states, dy):
    rows = dy.shape[0]
    nc = rows // CHUNK
    per, steps, sidx, row, par, state, xbc = _ssd_specs(True, nc)

    def body(xs_ref, b_ref, c_ref, dt_ref, bias_ref, al_ref, dk_ref, st_ref, dy_ref,
             dxs_ref, db_ref, dc_ref, ddt_ref, dbias_ref, dal_ref, ddk_ref, carry):
        s = pl.program_id(1)

        @pl.when(s == 0)
        def _():
            carry[...] = jnp.zeros_like(carry)
            for r in (dbias_ref, dal_ref, ddk_ref):
                r[...] = jnp.zeros_like(r)

        for j in reversed(range(per)):
            rs = pl.ds(j * CHUNK, CHUNK)
            live = (sidx(s) * per + j) * CHUNK + lax.broadcasted_iota(jnp.int32, (CHUNK, 1), 0) >= FRONT
            dxs, dbm, dcm, ddt, dprev, dbias, dal, ddk = _ssd_scan_bwd(
                xs_ref[rs, :].astype(F32), b_ref[rs, :].astype(F32), c_ref[rs, :].astype(F32), dt_ref[rs, :], st_ref[j, 0],
                bias_ref[...], al_ref[...], dk_ref[...], live, dy_ref[rs, :].astype(F32), carry[...])
            dxs_ref[rs, :] = dxs.astype(ACT_DTYPE)
            db_ref[rs, :] = dbm.astype(ACT_DTYPE)
            dc_ref[rs, :] = dcm.astype(ACT_DTYPE)
            ddt_ref[rs, :] = ddt
            carry[...] = dprev
            dbias_ref[...] += dbias
            dal_ref[...] += dal
            ddk_ref[...] += ddk

    bc = jax.ShapeDtypeStruct((rows, SSD_GROUPS * SSD_STATE), ACT_DTYPE)
    head = jax.ShapeDtypeStruct((1, SSD_GROUPS * LANES), F32)
    return pl.pallas_call(
        body, name="ssd_bwd", grid=(SSD_GROUPS, steps),
        in_specs=xbc + [row(LANES), par(LANES), par(LANES), par(LANES), state(), row(SSD_GW)],
        out_specs=[row(SSD_GW), row(SSD_STATE), row(SSD_STATE), row(LANES), par(LANES), par(LANES), par(LANES)],
        out_shape=[jax.ShapeDtypeStruct((rows, SSD_D_INNER), ACT_DTYPE), bc, bc,
                   jax.ShapeDtypeStruct((rows, SSD_GROUPS * LANES), F32), head, head, head],
        scratch_shapes=[pltpu.VMEM((SSD_STATE, SSD_GW), F32)],
        compiler_params=_params(("parallel", "arbitrary")),
    )(xbc_c, xbc_c, xbc_c, dtr, dt_bias, a_log, d_skip, states, dy)


def _rotary_tables(rows):
    pos = np.arange(rows, dtype=np.float32) - np.float32(FRONT)
    inv_freq = np.float32(ROPE_BASE) ** (-np.linspace(0.0, 1.0, RET_QK // 2, dtype=np.float32))
    ang = (pos[:, None] * inv_freq[None, :]).astype(np.float32).astype(np.float64)
    lgam = np.log(1.0 - 2.0 ** (-5.0 - np.arange(RET_HEADS, dtype=np.float64))).astype(np.float32)
    lgam = np.broadcast_to(lgam[:, None, None], (RET_HEADS, 8, LANES))
    return jnp.asarray(np.cos(ang).astype(np.float32)), jnp.asarray(np.sin(ang).astype(np.float32)), jnp.asarray(lgam)


def _rotary(t, cos, sin):
    half = t.shape[-1] // 2
    t1, t2 = t[:, :half], t[:, half:]
    return jnp.concatenate([t1 * cos - t2 * sin, t2 * cos + t1 * sin], axis=1)


def _ret_chunk(qh, kh, vh, gh, prev, cos, sin, lg):
    q = CHUNK
    qr = _rotary(qh, cos, sin)
    kr = _rotary(kh, cos, sin) * (RET_QK ** -0.5)
    li = lax.broadcasted_iota(jnp.int32, (q, q), 0)
    si = lax.broadcasted_iota(jnp.int32, (q, q), 1)
    dist = (li - si).astype(F32)
    decay = jnp.exp(jnp.where(li >= si, dist * lg, -jnp.inf))
    idx = lax.broadcasted_iota(jnp.int32, (q, 1), 0).astype(F32)
    scores = _dot(qr, kr, NT) * decay
    out = _dot(scores, vh)
    kv = _dot(kr * jnp.exp((q - 1.0 - idx) * lg), vh, TN)
    out = out + _dot(qr, prev) * jnp.exp((idx + 1.0) * lg)
    new = prev * jnp.exp(q * lg) + kv
    out = out * lax.rsqrt(jnp.mean(out * out, axis=-1, keepdims=True) + EPS)
    return _silu(gh) * out, new


def _ret_specs(rev, nc):
    per = _chunks_per_step(nc)
    steps = nc // per
    sidx = (lambda s: steps - 1 - s) if rev else (lambda s: s)
    row = lambda width: pl.BlockSpec((per * CHUNK, width), lambda h, s: (sidx(s), h))
    tab = lambda: pl.BlockSpec((per * CHUNK, RET_QK // 2), lambda h, s: (sidx(s), 0))
    lgs = lambda: pl.BlockSpec((1, 8, LANES), lambda h, s: (h, 0, 0))
    state = lambda: pl.BlockSpec((per, 1, RET_QK, RET_V), lambda h, s: (sidx(s), h, 0, 0))
    ins = [row(RET_QK), row(RET_QK), row(RET_V), row(RET_V), tab(), tab(), lgs()]
    return per, steps, row, state, ins


def _ret_fwd(q, k, v, g, cos, sin, lgam):
    rows = q.shape[0]
    nc = rows // CHUNK
    per, steps, row, state, ins = _ret_specs(False, nc)

    def body(q_ref, k_ref, v_ref, g_ref, cos_ref, sin_ref, lg_ref, y_ref, st_ref, carry):
        @pl.when(pl.program_id(1) == 0)
        def _():
            carry[...] = jnp.zeros_like(carry)

        for j in range(per):
            rs = pl.ds(j * CHUNK, CHUNK)
            prev = carry[...]
            st_ref[j, 0] = prev.astype(ACT_DTYPE)
            out, new = _ret_chunk(q_ref[rs, :].astype(F32), k_ref[rs, :].astype(F32), v_ref[rs, :].astype(F32),
                                  g_ref[rs, :].astype(F32), prev, cos_ref[rs, :], sin_ref[rs, :], lg_ref[0, 0:1, 0:1])
            y_ref[rs, :] = out.astype(ACT_DTYPE)
            carry[...] = new

    return pl.pallas_call(
        body, name="ret_fwd", grid=(RET_HEADS, steps), in_specs=ins, out_specs=[row(RET_V), state()],
        out_shape=[jax.ShapeDtypeStruct((rows, RET_HEADS * RET_V), ACT_DTYPE),
                   jax.ShapeDtypeStruct((nc, RET_HEADS, RET_QK, RET_V), ACT_DTYPE)],
        scratch_shapes=[pltpu.VMEM((RET_QK, RET_V), F32)],
        compiler_params=_params(("parallel", "arbitrary")),
    )(q, k, v, g, cos, sin, lgam)


def _ret_bwd(q, k, v, g, cos, sin, lgam, states, dy):
    rows = q.shape[0]
    nc = rows // CHUNK
    per, steps, row, state, ins = _ret_specs(True, nc)

    def body(q_ref, k_ref, v_ref, g_ref, cos_ref, sin_ref, lg_ref, st_ref, dy_ref, dq_ref, dk_ref, dv_ref, dg_ref, carry):
        @pl.when(pl.program_id(1) == 0)
        def _():
            carry[...] = jnp.zeros_like(carry)

        for j in reversed(range(per)):
            rs = pl.ds(j * CHUNK, CHUNK)
            fn = functools.partial(_ret_chunk, cos=cos_ref[rs, :], sin=sin_ref[rs, :], lg=lg_ref[0, 0:1, 0:1])
            _, vjp = jax.vjp(fn, q_ref[rs, :].astype(F32), k_ref[rs, :].astype(F32), v_ref[rs, :].astype(F32),
                             g_ref[rs, :].astype(F32), st_ref[j, 0].astype(F32))
            dq, dk, dv, dg, dprev = vjp((dy_ref[rs, :].astype(F32), carry[...]))
            dq_ref[rs, :] = dq.astype(ACT_DTYPE)
            dk_ref[rs, :] = dk.astype(ACT_DTYPE)
            dv_ref[rs, :] = dv.astype(ACT_DTYPE)
            dg_ref[rs, :] = dg.astype(ACT_DTYPE)
            carry[...] = dprev

    shp = lambda width: jax.ShapeDtypeStruct((rows, RET_HEADS * width), ACT_DTYPE)
    return pl.pallas_call(
        body, name="ret_bwd", grid=(RET_HEADS, steps), in_specs=ins + [state(), row(RET_V)],
        out_specs=[row(RET_QK), row(RET_QK), row(RET_V), row(RET_V)],
        out_shape=[shp(RET_QK), shp(RET_QK), shp(RET_V), shp(RET_V)],
        scratch_shapes=[pltpu.VMEM((RET_QK, RET_V), F32)],
        compiler_params=_params(("parallel", "arbitrary")),
    )(q, k, v, g, cos, sin, lgam, states, dy)


def _merge_fwd(bs, br, gs, gr):
    rows, d = bs.shape
    tm = _row_tile(rows)

    def body(bs_ref, br_ref, gs_ref, gr_ref, o_ref):
        o_ref[...] = (jax.nn.sigmoid(gs_ref[...].astype(F32)) * bs_ref[...].astype(F32)
                      + jax.nn.sigmoid(gr_ref[...].astype(F32)) * br_ref[...].astype(F32)).astype(ACT_DTYPE)

    spec = pl.BlockSpec((tm, d), lambda i: (i, 0))
    return pl.pallas_call(
        body, name="merge_fwd", grid=(rows // tm,), in_specs=[spec] * 4, out_specs=spec,
        out_shape=jax.ShapeDtypeStruct((rows, d), ACT_DTYPE), compiler_params=_params(("parallel",)),
    )(bs, br, gs, gr)


def _merge_bwd(dm, bs, br, gs, gr):
    rows, d = bs.shape
    tm = _row_tile(rows)

    def body(dm_ref, bs_ref, br_ref, gs_ref, gr_ref, dbs_ref, dbr_ref, dgs_ref, dgr_ref):
        dmv = dm_ref[...].astype(F32)
        for b_ref, g_ref, db_ref, dg_ref in ((bs_ref, gs_ref, dbs_ref, dgs_ref), (br_ref, gr_ref, dbr_ref, dgr_ref)):
            s = jax.nn.sigmoid(g_ref[...].astype(F32))
            db_ref[...] = (dmv * s).astype(ACT_DTYPE)
            dg_ref[...] = (dmv * b_ref[...].astype(F32) * s * (1.0 - s)).astype(ACT_DTYPE)

    spec = pl.BlockSpec((tm, d), lambda i: (i, 0))
    shp = jax.ShapeDtypeStruct((rows, d), ACT_DTYPE)
    return pl.pallas_call(
        body, name="merge_bwd", grid=(rows // tm,), in_specs=[spec] * 5, out_specs=[spec] * 4,
        out_shape=[shp] * 4, compiler_params=_params(("parallel",)),
    )(dm, bs, br, gs, gr)


def _place():
    x, y, c = lax.axis_index("x"), lax.axis_index("y"), lax.axis_index("c")
    return x, y, c


def _slot(p):
    return 4 * p[0] + 2 * p[1] + p[2]


def _allgather(arrs, name):
    n = len(arrs)
    any_spec = pl.BlockSpec(memory_space=pl.ANY)

    def body(*refs):
        ins, outs = refs[:n], refs[n:2 * n]
        send_sems, recv_sems, local_sems = refs[2 * n:]
        x, y, c = _place()
        me, sibling = (x, y, c), (x, y, 1 - c)
        chips = [(1 - x, y), (x, 1 - y), (1 - x, 1 - y)]

        def copy(a, k, block, to, src=None):
            dst = outs[a].at[_slot(block)]
            return pltpu.make_async_remote_copy(
                src_ref=dst if src is None else src, dst_ref=dst, send_sem=send_sems.at[a * 7 + k],
                recv_sem=recv_sems.at[a * 7 + k], device_id=to, device_id_type=MESH)

        mine, first, passed = [], [], []
        for a in range(n):
            cp = pltpu.make_async_copy(ins[a], outs[a].at[_slot(me)], local_sems.at[a])
            cp.start()
            mine.append(cp)
            first.append(copy(a, 0, me, sibling, src=ins[a]))
            first += [copy(a, 1 + j, me, (*chip, c), src=ins[a]) for j, chip in enumerate(chips)]
        for cp in first:
            cp.start()
        for j, chip in enumerate(chips):
            for a in range(n):
                copy(a, 1 + j, (*chip, c), me).wait_recv()
                cp = copy(a, 4 + j, (*chip, c), sibling)
                cp.start()
                passed.append(cp)
        for a in range(n):
            copy(a, 0, sibling, me).wait_recv()
            for j, chip in enumerate(chips):
                copy(a, 4 + j, (*chip, 1 - c), me).wait_recv()
        for cp in first + passed:
            cp.wait_send()
        for cp in mine:
            cp.wait()

    return pl.pallas_call(
        body, name=name, in_specs=[any_spec] * n, out_specs=[any_spec] * n,
        out_shape=[jax.ShapeDtypeStruct((N_DEV,) + a.shape, a.dtype) for a in arrs],
        scratch_shapes=[pltpu.SemaphoreType.DMA((7 * n,)), pltpu.SemaphoreType.DMA((7 * n,)), pltpu.SemaphoreType.DMA((n,))],
    )(*arrs)


def _peers():
    x, y, c = _place()
    return (x, y, c), [(x ^ dx, y ^ dy, c ^ dc) for dx in (0, 1) for dy in (0, 1) for dc in (0, 1)][1:]


def _exchange_copies(srcs, lands, send_sems, recv_sems, scatter, sender):
    me, peers = _peers()
    out = []
    for a, (src, land) in enumerate(zip(srcs, lands, strict=True)):
        for k, peer in enumerate(peers):
            src_ref = src.at[_slot(peer)] if scatter else src
            out.append(pltpu.make_async_remote_copy(
                src_ref=src_ref, dst_ref=land.at[_slot(me if sender else peer)], send_sem=send_sems.at[a * 7 + k],
                recv_sem=recv_sems.at[a * 7 + k], device_id=peer, device_id_type=MESH))
    return out


_HBM = pl.BlockSpec(memory_space=pltpu.HBM)
_SEM = pl.BlockSpec(memory_space=pltpu.SEMAPHORE)
_EFFECT = pltpu.SideEffectType.DATAFLOW_SIDE_EFFECTING


def _exchange_start(srcs, scatter, name):
    n = len(srcs)
    land_shapes = [s.shape if scatter else (N_DEV,) + s.shape for s in srcs]

    def body(*refs):
        for cp in _exchange_copies(refs[:n], refs[n:2 * n], refs[2 * n], refs[2 * n + 1], scatter, True):
            cp.start()
        refs[-1][...] = jnp.zeros_like(refs[-1])

    args = [pltpu.with_memory_space_constraint(s, pltpu.HBM) for s in srcs]
    args += [pltpu.with_memory_space_constraint(lax.empty(shp, s.dtype), pltpu.HBM) for s, shp in zip(srcs, land_shapes)]
    outs = pl.pallas_call(
        body, name=name,
        out_shape=(pltpu.SemaphoreType.DMA((7 * n,)), pltpu.SemaphoreType.DMA((7 * n,)))
        + tuple(pltpu.HBM(a.shape, a.dtype) for a in args) + (jax.ShapeDtypeStruct((8, LANES), F32),),
        in_specs=[_HBM] * (2 * n), out_specs=(_SEM, _SEM) + (_HBM,) * (2 * n) + (pl.BlockSpec(memory_space=pltpu.VMEM),),
        input_output_aliases={i: 2 + i for i in range(2 * n)},
        compiler_params=pltpu.CompilerParams(has_side_effects=_EFFECT),
    )(*args)
    return outs[:-1], outs[-1]


def _exchange_wait(handle, scatter, after, name):
    n = (len(handle) - 2) // 2
    thru = handle[2:]

    def body(*refs):
        for cp in _exchange_copies(refs[:n], refs[n:2 * n], refs[2 * n], refs[2 * n + 1], scatter, False):
            cp.wait_send()
            cp.wait_recv()

    outs = pl.pallas_call(
        body, name=name, out_shape=tuple(pltpu.HBM(t.shape, t.dtype) for t in thru),
        in_specs=[_HBM] * (2 * n) + [_SEM, _SEM, pl.BlockSpec(memory_space=pl.ANY)], out_specs=(_HBM,) * (2 * n),
        input_output_aliases={i: i for i in range(2 * n)},
        compiler_params=pltpu.CompilerParams(has_side_effects=_EFFECT),
    )(*thru, handle[0], handle[1], after)
    return list(outs[:n]), list(outs[n:])


def _allreduce_small(pack):
    rows, lanes = pack.shape

    def body(x_ref, o_ref, buf, send_sems, recv_sems):
        x, y, c = _place()
        me, sibling = (x, y, c), (x, y, 1 - c)
        chips = [(1 - x, y), (x, 1 - y), (1 - x, 1 - y)]

        def copy(k, block, to, src=None):
            dst = buf.at[_slot(block)]
            return pltpu.make_async_remote_copy(
                src_ref=dst if src is None else src, dst_ref=dst, send_sem=send_sems.at[k], recv_sem=recv_sems.at[k],
                device_id=to, device_id_type=MESH)

        buf[_slot(me)] = x_ref[...]
        first = [copy(0, me, sibling, src=x_ref)]
        first += [copy(1 + j, me, (*chip, c), src=x_ref) for j, chip in enumerate(chips)]
        for cp in first:
            cp.start()
        passed = [copy(4 + j, (*chip, c), sibling) for j, chip in enumerate(chips)]
        for j, chip in enumerate(chips):
            copy(1 + j, (*chip, c), me).wait_recv()
            passed[j].start()
        copy(0, sibling, me).wait_recv()
        for j, chip in enumerate(chips):
            copy(4 + j, (*chip, 1 - c), me).wait_recv()
        for cp in first + passed:
            cp.wait_send()
        acc = buf[0]
        for i in range(1, N_DEV):
            acc = acc + buf[i]
        o_ref[...] = acc

    vmem = pl.BlockSpec(memory_space=pltpu.VMEM)
    return pl.pallas_call(
        body, name="allreduce_small", in_specs=[vmem], out_specs=vmem,
        out_shape=jax.ShapeDtypeStruct((rows, lanes), F32),
        scratch_shapes=[pltpu.VMEM((N_DEV, rows, lanes), F32), pltpu.SemaphoreType.DMA((7,)), pltpu.SemaphoreType.DMA((7,))],
        compiler_params=pltpu.CompilerParams(vmem_limit_bytes=VMEM_LIMIT),
    )(pack)


def _adamw(w, g, m, v):
    m = ADAM_B1 * m + (1.0 - ADAM_B1) * g
    v = ADAM_B2 * v + (1.0 - ADAM_B2) * jnp.square(g)
    m_hat = m / (1.0 - ADAM_B1 ** ADAM_STEP)
    v_hat = v / (1.0 - ADAM_B2 ** ADAM_STEP)
    delta = -ADAM_LR * (m_hat / (jnp.sqrt(v_hat) + ADAM_EPS) + ADAM_WD * w)
    return delta, m, v


def _adam_shard(own, parts, w, m, v, name):
    r, c = w.shape
    tr = _pick(r, (128, 64, 32, 16, 8))

    def body(own_ref, p_ref, w_ref, m_ref, v_ref, g_ref, d_ref, nm_ref, nv_ref):
        _, peers = _peers()
        g = own_ref[...].astype(F32)
        for peer in peers:
            g = g + p_ref[_slot(peer)].astype(F32)
        g_ref[...] = g
        d_ref[...], nm_ref[...], nv_ref[...] = _adamw(w_ref[...], g, m_ref[...], v_ref[...])

    spec = pl.BlockSpec((tr, c), lambda i: (i, 0))
    shp = jax.ShapeDtypeStruct((r, c), F32)
    return pl.pallas_call(
        body, name=name, grid=(r // tr,),
        in_specs=[spec, pl.BlockSpec((N_DEV, tr, c), lambda i: (0, i, 0)), spec, spec, spec], out_specs=[spec] * 4,
        out_shape=[shp] * 4, compiler_params=_params(("parallel",)),
    )(own, parts, w, m, v)


def _adam_small(w, g, m, v):
    r, c = w.shape

    def body(w_ref, g_ref, m_ref, v_ref, d_ref, nm_ref, nv_ref):
        d_ref[...], nm_ref[...], nv_ref[...] = _adamw(w_ref[...], g_ref[...], m_ref[...], v_ref[...])

    shp = jax.ShapeDtypeStruct((r, c), F32)
    return pl.pallas_call(body, name="adam_small", out_shape=[shp] * 3)(w, g, m, v)


def _pack(arrs):
    rows = []
    for a in arrs:
        flat = a.reshape(-1).astype(F32)
        rows.append(jnp.pad(flat, (0, (-flat.shape[0]) % (8 * LANES))).reshape(-1, LANES))
    return jnp.concatenate(rows, axis=0)


def _unpack(pack, shapes):
    out, r = [], 0
    for s in shapes:
        size = math.prod(s)
        nr = -(-size // (8 * LANES)) * 8
        out.append(pack[r:r + nr].reshape(-1)[:size].reshape(s))
        r += nr
    return out


def _group_lanes(t):
    lead = t.shape[:-1]
    t = t.reshape(lead + (SSD_GROUPS, SSD_HPG))
    t = jnp.pad(t, [(0, 0)] * len(lead) + [(0, 0), (0, LANES - SSD_HPG)])
    return t.reshape(lead + (SSD_GROUPS * LANES,))


def _ungroup_lanes(t):
    lead = t.shape[:-1]
    return t.reshape(lead + (SSD_GROUPS, LANES))[..., :SSD_HPG].reshape(lead + (SSD_HEADS,))


def kernel(x, meta_tokens, mix_norm_w, w_in, ssd_conv_w, ssd_conv_b, ssd_dt_bias, ssd_A_log, ssd_D, ssd_norm_w, w_branch_ssd, w_branch_ret, w_out, ffn_norm_w, w_up, ffn_conv_w, ffn_conv_b, w_down, final_norm_w, loss_target, m_meta_tokens, m_mix_norm_w, m_w_in, m_ssd_conv_w, m_ssd_conv_b, m_ssd_dt_bias, m_ssd_A_log, m_ssd_D, m_ssd_norm_w, m_w_branch_ssd, m_w_branch_ret, m_w_out, m_ffn_norm_w, m_w_up, m_ffn_conv_w, m_ffn_conv_b, m_w_down, m_final_norm_w, v_meta_tokens, v_mix_norm_w, v_w_in, v_ssd_conv_w, v_ssd_conv_b, v_ssd_dt_bias, v_ssd_A_log, v_ssd_D, v_ssd_norm_w, v_w_branch_ssd, v_w_branch_ret, v_w_out, v_ffn_norm_w, v_w_up, v_ffn_conv_w, v_ffn_conv_b, v_w_down, v_final_norm_w):
    seq, d = x.shape[1], x.shape[2]
    rows = seq + PAD_ROWS
    tm = _row_tile(rows)
    me = _slot(_place())
    d_ff = w_down.shape[1] * N_DEV

    big = [w_in[0], w_branch_ssd[0], w_branch_ret[0], w_out[0], w_up[0], w_down[0]]
    first = _allgather([w_in[0].astype(COMM_DTYPE), meta_tokens, ssd_conv_w[0], ffn_conv_w[0]], "gather_first")
    rest_src = [b.astype(COMM_DTYPE) for b in big[1:]]
    rest_handle, rest_token = _exchange_start(rest_src, False, "gather_rest_start")
    cols = lambda t: jnp.transpose(t, (1, 0, 2)).reshape(t.shape[1], -1)
    rws = lambda t: t.reshape(-1, t.shape[2])
    w_in_f, conv_w, fconv_w = cols(first[0]), cols(first[2]), cols(first[3])
    meta_full = cols(first[1]) + rest_token[0, 0]
    widths = [SSD_D_INNER, SSD_CONV_DIM, SSD_HEADS, RET_HEADS * RET_QK, RET_HEADS * RET_QK, RET_HEADS * RET_V,
              RET_HEADS * RET_V, d, d]
    offs = [0]
    for wd in widths:
        offs.append(offs[-1] + wd)
    seg = [w_in_f[:, offs[i]:offs[i + 1]] for i in range(9)]
    seg[2] = _group_lanes(seg[2])
    order_p = [0, 1, 3, 4, 5, 6, 7, 8, 2]
    w_in_p = jnp.concatenate([seg[i] for i in order_p], axis=1)
    offs_p = [0]
    for i in order_p:
        offs_p.append(offs_p[-1] + seg[i].shape[1])
    in_p = offs_p[-1]

    h0 = jnp.concatenate([jnp.zeros((FRONT, d), F32), meta_full, x[0]], axis=0)
    u1 = _rms_fwd(h0, mix_norm_w, "rms1")
    proj = [None if i == 1 else
            _mm(u1, w_in_p, mode="nn", out_dtype=F32 if i == 2 else ACT_DTYPE, tm=tm, tk=d, name=f"in_proj_{i}",
                tn=_pick(offs_p[j + 1] - offs_p[j], (1024, 512)), b_n0=offs_p[j], n_out=offs_p[j + 1] - offs_p[j])
            for j, i in enumerate(order_p)]
    z, _, q, k, v, g, gs, gr, dtr = proj
    xbc, xbc_c = _xbc_proj_conv(u1, w_in_p, offs_p[1], conv_w, ssd_conv_b)
    bias_p, alog_p, dsk_p = _group_lanes(ssd_dt_bias), _group_lanes(ssd_A_log), _group_lanes(ssd_D)
    y_ssd, y_scan, ssd_states = _ssd_fwd(xbc_c, dtr, z, bias_p, alog_p, dsk_p, ssd_norm_w)
    cos, sin, lgam = _rotary_tables(rows)
    y_ret, ret_states = _ret_fwd(q, k, v, g, cos, sin, lgam)
    rest_own, rest = _exchange_wait(rest_handle, False, y_ret, "gather_rest_wait")
    rest = [lax.dynamic_update_index_in_dim(land, own, me, 0) for land, own in zip(rest, rest_own, strict=True)]
    w_bs, w_br, w_o, w_up_f, w_dn = rws(rest[0]), rws(rest[1]), rws(rest[2]), cols(rest[3]), rws(rest[4])
    bs = _mm(y_ssd, w_bs, mode="nn", out_dtype=ACT_DTYPE, tm=tm, tn=d, tk=SSD_D_INNER, name="branch_ssd")
    br = _mm(y_ret, w_br, mode="nn", out_dtype=ACT_DTYPE, tm=tm, tn=d, tk=RET_HEADS * RET_V, name="branch_ret")
    merged = _merge_fwd(bs, br, gs, gr)
    h1 = _mm(merged, w_o, mode="nn", out_dtype=F32, tm=tm, tn=d, tk=d, name="out_proj", add=h0)
    u2 = _rms_fwd(h1, ffn_norm_w, "rms2")
    up_g, up_v, act = _ffn_up_conv(u2, w_up_f, fconv_w, ffn_conv_b)
    h2 = _mm(act, w_dn, mode="nn", out_dtype=F32, tm=tm, tn=d, tk=d_ff, name="ffn_down", add=h1)
    tgt = jnp.pad(loss_target[0], ((PAD_ROWS, 0), (0, 0)))
    dh2, loss_acc, g_final = _loss_head(h2, tgt, final_norm_w.reshape(1, d))

    tff = _pick(d_ff, (1408, 256))
    tkr = _pick(rows, (1664, 128))
    cparts = lambda t, n=N_DEV: jnp.transpose(t.reshape(t.shape[0], n, -1), (1, 0, 2)).astype(COMM_DTYPE)
    rparts = lambda t: t.reshape(N_DEV, -1, t.shape[1]).astype(COMM_DTYPE)
    d_act = _mm(dh2, w_dn, mode="nt", out_dtype=ACT_DTYPE, tm=tm, tn=tff, tk=d, name="d_act")
    g_w_dn = _mm(act, dh2, mode="tn", out_dtype=F32, tm=tff, tn=d, tk=tkr, name="g_w_down")
    c_dn = [rparts(g_w_dn)]
    h_dn, t_dn = _exchange_start(c_dn, True, "scatter_down_start")
    d_up_g, d_up_v, g_fcw_g, g_fcb_g, g_fcw_v, g_fcb_v, g_w_up_g, g_w_up_v = _ffn_conv_bwd(
        up_g, up_v, d_act, fconv_w, ffn_conv_b + t_dn[0, 0], u2)
    g_fconv_w = jnp.concatenate([g_fcw_g, g_fcw_v], axis=1)
    g_fconv_b = jnp.concatenate([g_fcb_g, g_fcb_v], axis=1)
    c_up = [jnp.concatenate([cparts(g_w_up_g, N_DEV // 2), cparts(g_w_up_v, N_DEV // 2)], axis=0)]
    h_up, t_up = _exchange_start(c_up, True, "scatter_up_start")
    du2 = _mm(d_up_g, w_up_f, mode="nt", out_dtype=F32, tm=tm, tn=d, tk=d_ff, name="d_u2_gate", after=t_up)
    du2 = _mm(d_up_v, w_up_f, mode="nt", out_dtype=F32, tm=tm, tn=d, tk=d_ff, name="d_u2_value", add=du2, b_k0=d_ff)
    dh1, g_ffn_norm = _rms_bwd(du2, h1, ffn_norm_w, dh2, "rms2_bwd")
    d_merged = _mm(dh1, w_o, mode="nt", out_dtype=F32, tm=tm, tn=d, tk=d, name="d_merged")
    g_w_o = _mm(merged, dh1, mode="tn", out_dtype=F32, tm=d, tn=d, tk=tkr, name="g_w_out")
    d_bs, d_br, d_gs, d_gr = _merge_bwd(d_merged, bs, br, gs, gr)
    d_yssd = _mm(d_bs, w_bs, mode="nt", out_dtype=ACT_DTYPE, tm=tm, tn=1024, tk=d, name="d_y_ssd")
    g_w_bs = _mm(y_ssd, d_bs, mode="tn", out_dtype=F32, tm=1024, tn=d, tk=tkr, name="g_w_branch_ssd")
    d_yret = _mm(d_br, w_br, mode="nt", out_dtype=ACT_DTYPE, tm=tm, tn=1024, tk=d, name="d_y_ret")
    g_w_br = _mm(y_ret, d_br, mode="tn", out_dtype=F32, tm=1024, tn=d, tk=tkr, name="g_w_branch_ret")
    c_mid = [rparts(g_w_bs), rparts(g_w_br), rparts(g_w_o)]
    h_mid, t_mid = _exchange_start(c_mid, True, "scatter_mid_start")
    d_yscan, d_z, g_nw = _ssd_gate_bwd(y_scan, z, d_yssd, ssd_norm_w + t_mid[0, 0])
    dxs, d_bm, d_cm, d_dtr, g_bias_p, g_alog_p, g_dsk_p = _ssd_bwd(xbc_c, dtr, bias_p, alog_p, dsk_p, ssd_states, d_yscan)
    d_xbc, g_conv_w, g_conv_b = _ssd_conv_bwd(xbc, jnp.concatenate([dxs, d_bm, d_cm], axis=1), conv_w, ssd_conv_b)
    d_q, d_k, d_v, d_g = _ret_bwd(q, k, v, g, cos, sin, lgam, ret_states, d_yret)
    dproj = jnp.concatenate([d_z, d_xbc, d_q, d_k, d_v, d_g, d_gs, d_gr, d_dtr.astype(ACT_DTYPE)], axis=1)
    g_w_in_p = _mm(u1, dproj, mode="tn", out_dtype=F32, tm=d, tn=_pick(in_p, (768, 512)), tk=tkr, name="g_w_in")
    gseg = [None] * 9
    for j, i in enumerate(order_p):
        gseg[i] = g_w_in_p[:, offs_p[j]:offs_p[j + 1]]
    gseg[2] = _ungroup_lanes(gseg[2])
    c_in = [cparts(jnp.concatenate(gseg, axis=1))]
    h_in, t_in = _exchange_start(c_in, True, "scatter_in_start")
    du1 = _mm(dproj, w_in_p, mode="nt", out_dtype=F32, tm=tm, tn=d, tk=_pick(in_p, (4608, 512)), name="d_u1", after=t_in)
    dh0, g_mix_norm = _rms_bwd(du1, h0, mix_norm_w, dh1, "rms1_bwd")
    grad_x = dh0[PAD_ROWS:][None]

    landed = {}
    for key, handle, names in (("in", h_in, ["w_in"]), ("mid", h_mid, ["w_branch_ssd", "w_branch_ret", "w_out"]),
                               ("up", h_up, ["w_up"]), ("down", h_dn, ["w_down"])):
        srcs, lands = _exchange_wait(handle, True, dh0, f"scatter_{key}_wait")
        for nm, land, src in zip(names, lands, srcs, strict=True):
            landed[nm] = (lax.dynamic_index_in_dim(src, me, 0, keepdims=False), land)
    big_m = [m_w_in, m_w_branch_ssd, m_w_branch_ret, m_w_out, m_w_up, m_w_down]
    big_v = [v_w_in, v_w_branch_ssd, v_w_branch_ret, v_w_out, v_w_up, v_w_down]
    big_names = ["w_in", "w_branch_ssd", "w_branch_ret", "w_out", "w_up", "w_down"]
    big_out = {}
    for nm, w, m, v_ in zip(big_names, big, big_m, big_v, strict=True):
        big_out[nm] = [t[None] for t in _adam_shard(*landed[nm], w, m[0], v_[0], "adam_" + nm)]

    small_g = [dh0[FRONT:PAD_ROWS], g_mix_norm, g_conv_w, g_conv_b, _ungroup_lanes(g_bias_p), _ungroup_lanes(g_alog_p),
               _ungroup_lanes(g_dsk_p), g_nw, g_ffn_norm, g_fconv_w, g_fconv_b, g_final, loss_acc[0:1, 0:1]]
    total = _unpack(_allreduce_small(_pack(small_g)), [t.shape for t in small_g])
    loss = total[12].reshape(())
    shard = lambda t, width: lax.dynamic_slice_in_dim(t, me * width, width, axis=1)
    small_names = ["meta_tokens", "mix_norm_w", "ssd_conv_w", "ssd_conv_b", "ssd_dt_bias", "ssd_A_log", "ssd_D", "ssd_norm_w",
                   "ffn_norm_w", "ffn_conv_w", "ffn_conv_b", "final_norm_w"]
    small_w = [meta_tokens, mix_norm_w, ssd_conv_w, ssd_conv_b, ssd_dt_bias, ssd_A_log, ssd_D, ssd_norm_w, ffn_norm_w,
               ffn_conv_w, ffn_conv_b, final_norm_w]
    small_m = [m_meta_tokens, m_mix_norm_w, m_ssd_conv_w, m_ssd_conv_b, m_ssd_dt_bias, m_ssd_A_log, m_ssd_D, m_ssd_norm_w,
               m_ffn_norm_w, m_ffn_conv_w, m_ffn_conv_b, m_final_norm_w]
    small_v = [v_meta_tokens, v_mix_norm_w, v_ssd_conv_w, v_ssd_conv_b, v_ssd_dt_bias, v_ssd_A_log, v_ssd_D, v_ssd_norm_w,
               v_ffn_norm_w, v_ffn_conv_w, v_ffn_conv_b, v_final_norm_w]
    grads = total[:12]
    grads[0] = shard(grads[0], meta_tokens.shape[1])
    grads[2] = shard(grads[2], ssd_conv_w.shape[2])
    grads[9] = shard(grads[9], ffn_conv_w.shape[2])
    grads = [t.reshape(w.shape) for t, w in zip(grads, small_w, strict=True)]
    shapes = [w.shape for w in small_w]
    upd = _adam_small(_pack(small_w), _pack(grads), _pack(small_m), _pack(small_v))
    small_out = {nm: [gr_] + [u[i] for u in (_unpack(t, shapes) for t in upd)]
                 for i, (nm, gr_) in enumerate(zip(small_names, grads, strict=True))}

    order = ["meta_tokens", "mix_norm_w", "w_in", "ssd_conv_w", "ssd_conv_b", "ssd_dt_bias", "ssd_A_log", "ssd_D", "ssd_norm_w",
             "w_branch_ssd", "w_branch_ret", "w_out", "ffn_norm_w", "w_up", "ffn_conv_w", "ffn_conv_b", "w_down", "final_norm_w"]
    res = {**big_out, **small_out}
    return (loss, grad_x, *[res[nm][0] for nm in order], *[res[nm][1] for nm in order], *[res[nm][2] for nm in order],
            *[res[nm][3] for nm in order])
```

```python
import functools
import math

import jax
import jax.numpy as jnp
import numpy as np
from jax import lax
from jax.experimental import pallas as pl
from jax.experimental.pallas import tpu as pltpu

F32 = jnp.float32
MXU_DTYPE = jnp.bfloat16
ACT_DTYPE = jnp.bfloat16
COMM_DTYPE = jnp.bfloat16

N_META = 16
CHUNK = 128
FRONT = CHUNK - N_META
PAD_ROWS = FRONT + N_META
EPS = 1e-6
N_DEV = 8

SSD_D_INNER = 2048
SSD_HEAD_DIM = 64
SSD_HEADS = 32
SSD_GROUPS = 4
SSD_HPG = SSD_HEADS // SSD_GROUPS
SSD_STATE = 128
SSD_CONV = 4
SSD_CONV_DIM = SSD_D_INNER + 2 * SSD_GROUPS * SSD_STATE
SSD_GW = SSD_D_INNER // SSD_GROUPS
RET_HEADS = 4
RET_QK = 256
RET_V = 512
ROPE_BASE = 10000.0
FFN_CONV = 3
HALO = 16
LANES = 128

ADAM_LR = 0.001
ADAM_B1 = 0.9
ADAM_B2 = 0.999
ADAM_EPS = 1e-08
ADAM_WD = 0.01
ADAM_STEP = 10

VMEM_LIMIT = 56 * 1024 * 1024
MESH = pl.DeviceIdType.MESH

NN = (((1,), (0,)), ((), ()))
NT = (((1,), (1,)), ((), ()))
TN = (((0,), (0,)), ((), ()))


def _params(sem):
    return pltpu.CompilerParams(dimension_semantics=sem, vmem_limit_bytes=VMEM_LIMIT)


def _mxu(a, b, dn):
    return lax.dot_general(a.astype(MXU_DTYPE), b.astype(MXU_DTYPE), dn, preferred_element_type=F32)


@functools.partial(jax.custom_vjp, nondiff_argnums=(2,))
def _dot(a, b, dn=NN):
    return _mxu(a, b, dn)


def _dot_fwd(a, b, dn):
    return _mxu(a, b, dn), (a, b)


def _dot_bwd(dn, res, g):
    a, b = res
    if dn == NN:
        return _mxu(g, b, NT), _mxu(a, g, TN)
    if dn == NT:
        return _mxu(g, b, NN), _mxu(g, a, TN)
    assert dn == TN
    return _mxu(b, g, NT), _mxu(a, g, NN)


_dot.defvjp(_dot_fwd, _dot_bwd)


def _silu(x):
    return x * jax.nn.sigmoid(x)


def _dsilu(x):
    s = jax.nn.sigmoid(x)
    return s * (1.0 + x * (1.0 - s))


def _row_tile(rows):
    return 640 if rows % 640 == 0 else 128


def _mm(a, b, *, mode, out_dtype, tm, tn, tk, name, add=None, after=None, b_k0=0, b_n0=0, n_out=None):
    if mode == "nn":
        (m, k), k2 = a.shape, b.shape[0]
        n = b.shape[1] if n_out is None else n_out
        assert b_n0 % tn == 0 and b_n0 + n <= b.shape[1]
    elif mode == "nt":
        (m, k), n = a.shape, b.shape[0]
        k2 = k if b_k0 % tk == 0 and b_k0 + k <= b.shape[1] else None
    else:
        (k, m), (k2, n) = a.shape, b.shape
    assert (b_k0 == 0 or mode == "nt") and ((b_n0 == 0 and n_out is None) or mode == "nn")
    assert k == k2 and m % tm == 0 and n % tn == 0 and k % tk == 0, (name, a.shape, b.shape, tm, tn, tk)
    kb0, nb0 = b_k0 // tk, b_n0 // tn
    nk = k // tk
    dn = {"nn": NN, "nt": NT, "tn": TN}[mode]
    has_add = add is not None
    n_in = 2 + has_add + (after is not None)

    def body(*refs):
        a_ref, b_ref = refs[0], refs[1]
        add_ref = refs[2] if has_add else None
        o_ref = refs[n_in]
        p = _dot(a_ref[...], b_ref[...], dn)
        if nk == 1:
            if has_add:
                p = p + add_ref[...]
            o_ref[...] = p.astype(out_dtype)
        else:
            acc_ref = refs[n_in + 1]
            kk = pl.program_id(2)

            @pl.when(kk == 0)
            def _():
                acc_ref[...] = p

            @pl.when(kk > 0)
            def _():
                acc_ref[...] += p

            @pl.when(kk == nk - 1)
            def _():
                r = acc_ref[...]
                if has_add:
                    r = r + add_ref[...]
                o_ref[...] = r.astype(out_dtype)

    if mode == "tn":
        a_spec = pl.BlockSpec((tk, tm), lambda j, i, kk: (kk, i))
    else:
        a_spec = pl.BlockSpec((tm, tk), lambda j, i, kk: (i, kk))
    if mode == "nt":
        b_spec = pl.BlockSpec((tn, tk), lambda j, i, kk: (j, kk + kb0))
    else:
        b_spec = pl.BlockSpec((tk, tn), lambda j, i, kk: (kk, j + nb0))
    o_spec = pl.BlockSpec((tm, tn), lambda j, i, kk: (i, j))
    in_specs = [a_spec, b_spec] + ([o_spec] if has_add else [])
    args = (a, b) + ((add,) if has_add else ())
    if after is not None:
        in_specs.append(pl.BlockSpec(memory_space=pl.ANY))
        args += (after,)
    return pl.pallas_call(
        body, name=name, grid=(n // tn, m // tm, nk), in_specs=in_specs, out_specs=o_spec,
        out_shape=jax.ShapeDtypeStruct((m, n), out_dtype),
        scratch_shapes=[pltpu.VMEM((tm, tn), F32)] if nk > 1 else [],
        compiler_params=_params(("parallel", "parallel", "arbitrary")),
    )(*args)


def _pick(n, cands):
    for c in cands:
        if n % c == 0:
            return c
    return n


def _rms_fwd(h, w, name):
    rows, d = h.shape
    tm = _row_tile(rows)

    def body(h_ref, w_ref, u_ref):
        x = h_ref[...]
        r = lax.rsqrt(jnp.mean(x * x, axis=-1, keepdims=True) + EPS)
        u_ref[...] = (x * r * w_ref[...]).astype(ACT_DTYPE)

    return pl.pallas_call(
        body, name=name, grid=(rows // tm,),
        in_specs=[pl.BlockSpec((tm, d), lambda i: (i, 0)), pl.BlockSpec((1, d), lambda i: (0, 0))],
        out_specs=pl.BlockSpec((tm, d), lambda i: (i, 0)),
        out_shape=jax.ShapeDtypeStruct((rows, d), ACT_DTYPE),
        compiler_params=_params(("parallel",)),
    )(h, w)


def _rms_bwd(du, h, w, dres, name):
    rows, d = h.shape
    tm = _row_tile(rows)

    def body(du_ref, h_ref, w_ref, dres_ref, dh_ref, dw_ref):
        x = h_ref[...]
        dy = du_ref[...].astype(F32)
        r = lax.rsqrt(jnp.mean(x * x, axis=-1, keepdims=True) + EPS)
        xhat = x * r
        dxn = dy * w_ref[...]
        dx = r * (dxn - xhat * jnp.mean(dxn * xhat, axis=-1, keepdims=True))
        dh_ref[...] = dres_ref[...] + dx

        @pl.when(pl.program_id(0) == 0)
        def _():
            dw_ref[...] = jnp.zeros_like(dw_ref)

        dw_ref[...] += jnp.sum(dy * xhat, axis=0, keepdims=True)

    return pl.pallas_call(
        body, name=name, grid=(rows // tm,),
        in_specs=[pl.BlockSpec((tm, d), lambda i: (i, 0)), pl.BlockSpec((tm, d), lambda i: (i, 0)),
                  pl.BlockSpec((1, d), lambda i: (0, 0)), pl.BlockSpec((tm, d), lambda i: (i, 0))],
        out_specs=[pl.BlockSpec((tm, d), lambda i: (i, 0)), pl.BlockSpec((1, d), lambda i: (0, 0))],
        out_shape=[jax.ShapeDtypeStruct((rows, d), F32), jax.ShapeDtypeStruct((1, d), F32)],
        compiler_params=_params(("arbitrary",)),
    )(du, h, w, dres)


def _loss_head(h2, tgt, w):
    rows, d = h2.shape
    tm = _row_tile(rows)

    def body(h_ref, t_ref, w_ref, dh_ref, loss_ref, dw_ref):
        i = pl.program_id(0)
        x = h_ref[...]
        r = lax.rsqrt(jnp.mean(x * x, axis=-1, keepdims=True) + EPS)
        xhat = x * r
        wv = w_ref[...]
        row = i * tm + lax.broadcasted_iota(jnp.int32, (tm, 1), 0)
        live = row >= PAD_ROWS
        diff = jnp.where(live, xhat * wv - t_ref[...], 0.0)
        dy = diff * (1.0 / d)
        dxn = dy * wv
        dh_ref[...] = r * (dxn - xhat * jnp.mean(dxn * xhat, axis=-1, keepdims=True))

        @pl.when(i == 0)
        def _():
            loss_ref[...] = jnp.zeros_like(loss_ref)
            dw_ref[...] = jnp.zeros_like(dw_ref)

        loss_ref[...] += 0.5 * jnp.sum(jnp.mean(diff * diff, axis=-1, keepdims=True))
        dw_ref[...] += jnp.sum(dy * xhat, axis=0, keepdims=True)

    return pl.pallas_call(
        body, name="loss_head", grid=(rows // tm,),
        in_specs=[pl.BlockSpec((tm, d), lambda i: (i, 0)), pl.BlockSpec((tm, d), lambda i: (i, 0)),
                  pl.BlockSpec((1, d), lambda i: (0, 0))],
        out_specs=[pl.BlockSpec((tm, d), lambda i: (i, 0)), pl.BlockSpec((8, LANES), lambda i: (0, 0)),
                   pl.BlockSpec((1, d), lambda i: (0, 0))],
        out_shape=[jax.ShapeDtypeStruct((rows, d), F32), jax.ShapeDtypeStruct((8, LANES), F32),
                   jax.ShapeDtypeStruct((1, d), F32)],
        compiler_params=_params(("arbitrary",)),
    )(h2, tgt, w)


def _prev_halo_spec(tm, width, col):
    return pl.BlockSpec((HALO, width), lambda j, i: (jnp.maximum(i * (tm // HALO) - 1, 0), col(j)))


def _next_halo_spec(tm, rows, width, col):
    last = rows // HALO - 1
    return pl.BlockSpec((HALO, width), lambda j, i: (jnp.minimum((i + 1) * (tm // HALO), last), col(j)))


def _conv_taps(cat, w_ref, b_ref, kw):
    acc = b_ref[...] + w_ref[kw - 1:kw, :] * cat
    for s in range(1, kw):
        acc = acc + w_ref[kw - 1 - s:kw - s, :] * pltpu.roll(cat, s, 0)
    return acc


def _conv_back(dpre, w_ref, kw):
    n = dpre.shape[0]
    acc = w_ref[kw - 1:kw, :] * dpre
    for s in range(1, kw):
        acc = acc + w_ref[kw - 1 - s:kw - s, :] * pltpu.roll(dpre, n - s, 0)
    return acc


def _xbc_proj_conv(u1, w_in_p, col0, w, b):
    rows, d = u1.shape
    width = w.shape[1]
    tm, tc = _row_tile(rows), 512
    cb0 = col0 // tc
    assert col0 % tc == 0

    def body(u_ref, m_ref, w_ref, b_ref, x_ref, o_ref, carry):
        i = pl.program_id(1)

        @pl.when(i == 0)
        def _():
            carry[...] = jnp.zeros_like(carry)

        xb = _mxu(u_ref[...], m_ref[...], NN).astype(ACT_DTYPE)
        x_ref[...] = xb
        x = xb.astype(F32)
        cat = jnp.concatenate([carry[...], x], axis=0)
        carry[...] = x[tm - HALO:, :]
        pre = _conv_taps(cat, w_ref, b_ref, SSD_CONV)[HALO:]
        row = i * tm + lax.broadcasted_iota(jnp.int32, (tm, 1), 0)
        o_ref[...] = jnp.where(row >= FRONT, _silu(pre), 0.0).astype(ACT_DTYPE)

    main = pl.BlockSpec((tm, tc), lambda j, i: (i, j))
    par = lambda r: pl.BlockSpec((r, tc), lambda j, i: (0, j))
    act = jax.ShapeDtypeStruct((rows, width), ACT_DTYPE)
    return pl.pallas_call(
        body, name="xbc_proj_conv", grid=(width // tc, rows // tm),
        in_specs=[pl.BlockSpec((tm, d), lambda j, i: (i, 0)), pl.BlockSpec((d, tc), lambda j, i: (0, cb0 + j)),
                  par(SSD_CONV), par(1)],
        out_specs=[main, main], out_shape=[act, act], scratch_shapes=[pltpu.VMEM((HALO, tc), F32)],
        compiler_params=_params(("parallel", "arbitrary")),
    )(u1, w_in_p, w, b)


def _ssd_conv_bwd(xbc, dxc, w, b):
    rows, width = xbc.shape
    tm, tc = _row_tile(rows), 512
    kw = SSD_CONV

    def body(x_ref, xp_ref, xn_ref, d_ref, dn_ref, w_ref, b_ref, dx_ref, dw_ref, db_ref):
        i = pl.program_id(1)
        xp = jnp.where(i == 0, 0.0, xp_ref[...].astype(F32))
        cat = jnp.concatenate([xp, x_ref[...].astype(F32), xn_ref[...].astype(F32)], axis=0)
        pre = _conv_taps(cat, w_ref, b_ref, kw)[HALO:]
        row = i * tm + lax.broadcasted_iota(jnp.int32, (tm + HALO, 1), 0)
        live = (row >= FRONT) & (row < rows)
        dout = jnp.concatenate([d_ref[...].astype(F32), dn_ref[...].astype(F32)], axis=0)
        dpre = jnp.where(live, dout * _dsilu(pre), 0.0)
        dx_ref[...] = _conv_back(dpre, w_ref, kw)[:tm].astype(ACT_DTYPE)

        @pl.when(i == 0)
        def _():
            dw_ref[...] = jnp.zeros_like(dw_ref)
            db_ref[...] = jnp.zeros_like(db_ref)

        dmain = dpre[:tm]
        db_ref[...] += jnp.sum(dmain, axis=0, keepdims=True)
        for k in range(kw):
            s = kw - 1 - k
            xs = (pltpu.roll(cat, s, 0) if s else cat)[HALO:HALO + tm]
            dw_ref[k:k + 1, :] += jnp.sum(dmain * xs, axis=0, keepdims=True)

    main = pl.BlockSpec((tm, tc), lambda j, i: (i, j))
    par = lambda r: pl.BlockSpec((r, tc), lambda j, i: (0, j))
    col = lambda j: j
    return pl.pallas_call(
        body, name="ssd_conv_bwd", grid=(width // tc, rows // tm),
        in_specs=[main, _prev_halo_spec(tm, tc, col), _next_halo_spec(tm, rows, tc, col),
                  main, _next_halo_spec(tm, rows, tc, col), par(kw), par(1)],
        out_specs=[main, par(kw), par(1)],
        out_shape=[jax.ShapeDtypeStruct((rows, width), ACT_DTYPE), jax.ShapeDtypeStruct((kw, width), F32),
                   jax.ShapeDtypeStruct((1, width), F32)],
        compiler_params=_params(("parallel", "arbitrary")),
    )(xbc, xbc, xbc, dxc, dxc, w, b)


def _ffn_up_conv(u2, w_up, w, b):
    rows, d = u2.shape
    width = w_up.shape[1]
    dff = width // 2
    tm, tc = _row_tile(rows), _pick(dff, (256, 128))
    nb = dff // tc
    kw = FFN_CONV

    def body(u_ref, mg_ref, mv_ref, wg_ref, bg_ref, wv_ref, bv_ref, ug_ref, uv_ref, o_ref, cg, cv):
        i = pl.program_id(1)

        @pl.when(i == 0)
        def _():
            cg[...] = jnp.zeros_like(cg)
            cv[...] = jnp.zeros_like(cv)

        def pre(m_ref, up_ref, carry, w_ref, b_ref):
            upb = _mxu(u_ref[...], m_ref[...], NN).astype(ACT_DTYPE)
            up_ref[...] = upb
            x = upb.astype(F32)
            cat = jnp.concatenate([carry[...], x], axis=0)
            carry[...] = x[tm - HALO:, :]
            return _conv_taps(cat, w_ref, b_ref, kw)[HALO:]

        ag = pre(mg_ref, ug_ref, cg, wg_ref, bg_ref)
        av = pre(mv_ref, uv_ref, cv, wv_ref, bv_ref)
        o_ref[...] = (_silu(ag) * av).astype(ACT_DTYPE)

    gcol, vcol = (lambda j: j), (lambda j: j + nb)
    mat = lambda col: pl.BlockSpec((d, tc), lambda j, i: (0, col(j)))
    par = lambda r, col: pl.BlockSpec((r, tc), lambda j, i: (0, col(j)))
    out = pl.BlockSpec((tm, tc), lambda j, i: (i, j))
    act = jax.ShapeDtypeStruct((rows, dff), ACT_DTYPE)
    return pl.pallas_call(
        body, name="ffn_up_conv", grid=(nb, rows // tm),
        in_specs=[pl.BlockSpec((tm, d), lambda j, i: (i, 0)), mat(gcol), mat(vcol),
                  par(kw, gcol), par(1, gcol), par(kw, vcol), par(1, vcol)],
        out_specs=[out, out, out], out_shape=[act, act, act],
        scratch_shapes=[pltpu.VMEM((HALO, tc), F32), pltpu.VMEM((HALO, tc), F32)],
        compiler_params=_params(("parallel", "arbitrary")),
    )(u2, w_up, w_up, w, b, w, b)


def _ffn_conv_bwd(up_g, up_v, dact, w, b, u2):
    rows, dff = up_g.shape
    d = u2.shape[1]
    tm, tc = _row_tile(rows), _pick(dff, (256, 128))
    nb = dff // tc
    kw = FFN_CONV

    def body(g_ref, gp_ref, gn_ref, v_ref, vp_ref, vn_ref, d_ref, dn_ref, wg_ref, bg_ref, wv_ref, bv_ref, u_ref,
             dxg_ref, dxv_ref, dwg_ref, dbg_ref, dwv_ref, dbv_ref, gwg_ref, gwv_ref):
        i = pl.program_id(1)

        def shifted(x_ref, xp_ref, xn_ref):
            xp = jnp.where(i == 0, 0.0, xp_ref[...].astype(F32))
            cat = jnp.concatenate([xp, x_ref[...].astype(F32), xn_ref[...].astype(F32)], axis=0)
            return [cat] + [pltpu.roll(cat, s, 0) for s in range(1, kw)]

        def taps(sh, w_ref, b_ref):
            acc = b_ref[...] + w_ref[kw - 1:kw, :] * sh[0]
            for s in range(1, kw):
                acc = acc + w_ref[kw - 1 - s:kw - s, :] * sh[s]
            return acc[HALO:]

        sh_g, sh_v = shifted(g_ref, gp_ref, gn_ref), shifted(v_ref, vp_ref, vn_ref)
        ag, av = taps(sh_g, wg_ref, bg_ref), taps(sh_v, wv_ref, bv_ref)
        row = i * tm + lax.broadcasted_iota(jnp.int32, (tm + HALO, 1), 0)
        dout = jnp.concatenate([d_ref[...].astype(F32), dn_ref[...].astype(F32)], axis=0)
        dout = jnp.where(row < rows, dout, 0.0)
        s = jax.nn.sigmoid(ag)
        silu = ag * s
        dpre_v = dout * silu
        dpre_g = dout * av * (s + silu * (1.0 - s))

        @pl.when(i == 0)
        def _():
            for r in (dwg_ref, dbg_ref, dwv_ref, dbv_ref, gwg_ref, gwv_ref):
                r[...] = jnp.zeros_like(r)

        for dpre, sh, w_ref, dx_ref, dw_ref, db_ref, gw_ref in (
                (dpre_g, sh_g, wg_ref, dxg_ref, dwg_ref, dbg_ref, gwg_ref), (dpre_v, sh_v, wv_ref, dxv_ref, dwv_ref, dbv_ref, gwv_ref)):
            dx = _conv_back(dpre, w_ref, kw)[:tm].astype(ACT_DTYPE)
            dx_ref[...] = dx
            gw_ref[...] += _mxu(u_ref[...], dx, TN)
            dmain = dpre[:tm]
            db_ref[...] += jnp.sum(dmain, axis=0, keepdims=True)
            for k in range(kw):
                dw_ref[k:k + 1, :] += jnp.sum(dmain * sh[kw - 1 - k][HALO:HALO + tm], axis=0, keepdims=True)

    gcol, vcol = (lambda j: j), (lambda j: j + nb)
    main = lambda col: pl.BlockSpec((tm, tc), lambda j, i: (i, col(j)))
    par = lambda r, col: pl.BlockSpec((r, tc), lambda j, i: (0, col(j)))
    halos = lambda col: [_prev_halo_spec(tm, tc, col), _next_halo_spec(tm, rows, tc, col)]
    act_shape = jax.ShapeDtypeStruct((rows, dff), ACT_DTYPE)
    par_shapes = [jax.ShapeDtypeStruct((kw, dff), F32), jax.ShapeDtypeStruct((1, dff), F32)]
    gw_shape = jax.ShapeDtypeStruct((d, dff), F32)
    return pl.pallas_call(
        body, name="ffn_conv_bwd", grid=(nb, rows // tm),
        in_specs=[main(gcol)] + halos(gcol) + [main(gcol)] + halos(gcol) + [main(gcol), _next_halo_spec(tm, rows, tc, gcol),
                  par(kw, gcol), par(1, gcol), par(kw, vcol), par(1, vcol), pl.BlockSpec((tm, d), lambda j, i: (i, 0))],
        out_specs=[main(gcol), main(gcol), par(kw, gcol), par(1, gcol), par(kw, gcol), par(1, gcol), par(d, gcol), par(d, gcol)],
        out_shape=[act_shape, act_shape] + par_shapes + par_shapes + [gw_shape, gw_shape],
        compiler_params=_params(("parallel", "arbitrary")),
    )(up_g, up_g, up_g, up_v, up_v, up_v, dact, dact, w, b, w, b, u2)


def _ssd_scalars(dtr, dt_bias, a_log, live):
    q = CHUNK
    pre = dtr + dt_bias
    dt = jnp.where(live, jax.nn.softplus(pre), 0.0)
    a_neg = -jnp.exp(a_log)
    li = lax.broadcasted_iota(jnp.int32, (q, q), 0)
    si = lax.broadcasted_iota(jnp.int32, (q, q), 1)
    causal = li >= si
    tri = jnp.where(causal, 1.0, 0.0).astype(F32)
    a_cs = jnp.dot(tri, dt * a_neg, precision=lax.Precision.HIGHEST, preferred_element_type=F32)
    return pre, dt, a_neg, a_cs, causal, tri


def _head_select():
    r = lax.broadcasted_iota(jnp.int32, (LANES, SSD_GW), 0)
    c = lax.broadcasted_iota(jnp.int32, (LANES, SSD_GW), 1)
    return jnp.where(c // SSD_HEAD_DIM == r, 1.0, 0.0).astype(MXU_DTYPE)


def _split(t, parts):
    out, rem = [], t
    for _ in range(parts):
        p = rem.astype(MXU_DTYPE)
        out.append(p)
        rem = rem - p.astype(F32)
    return out


def _head_cols(t, sel):
    return sum(_mxu(p, sel, NN) for p in _split(t, 2))


def _head_sums(t, sel):
    return sum(_mxu(p, sel, NT) for p in _split(t, 3))


def _half_masks():
    lane = lax.broadcasted_iota(jnp.int32, (CHUNK, LANES), 1)
    return lane < SSD_HEAD_DIM, lane >= SSD_HEAD_DIM


def _ssd_scan(xs, bm, cm, dtr, prev, dt_bias, a_log, d_skip, live):
    q = CHUNK
    sel = _head_select()
    _, dt, _, a_cs, causal, _ = _ssd_scalars(dtr, dt_bias, a_log, live)
    a_cs_t = a_cs.T
    a_end = a_cs[q - 1:q, :]
    e_x = _head_cols(jnp.exp(a_cs), sel)
    xdt = xs * _head_cols(dt, sel)
    cb = _dot(cm, bm, NT)
    y = _dot(cm, prev) * e_x + _head_cols(jnp.broadcast_to(d_skip, (8, LANES)), sel)[0:1] * xs
    new = prev * e_x[q - 1:q, :] + _dot(bm, xdt * _head_cols(jnp.exp(a_end - a_cs), sel), TN)
    masks = _half_masks()
    ys = []
    for pp in range(SSD_HPG // 2):
        xpair = xdt[:, pp * LANES:(pp + 1) * LANES]
        acc = jnp.zeros((q, LANES), F32)
        for half in range(2):
            hh = 2 * pp + half
            decay = jnp.exp(jnp.where(causal, a_cs[:, hh:hh + 1] - a_cs_t[hh:hh + 1, :], -jnp.inf))
            acc = acc + _dot(cb * decay, jnp.where(masks[half], xpair, 0.0))
        ys.append(acc)
    return y + jnp.concatenate(ys, axis=1), new


def _ssd_gate(y, z, nw):
    yz = y * _silu(z)
    return yz * lax.rsqrt(jnp.mean(yz * yz, axis=-1, keepdims=True) + EPS) * nw


def _ssd_scan_bwd(xs, bm, cm, dtr, prev, dt_bias, a_log, d_skip, live, dy, dnew):
    q = CHUNK
    sel = _head_select()
    pre, dt, a_neg, a_cs, causal, tri = _ssd_scalars(dtr, dt_bias, a_log, live)
    a_cs_t = a_cs.T
    a_end = a_cs[q - 1:q, :]
    dt_x, e_x, w_x = _head_cols(dt, sel), _head_cols(jnp.exp(a_cs), sel), _head_cols(jnp.exp(a_end - a_cs), sel)
    g_x, d_x = e_x[q - 1:q, :], _head_cols(jnp.broadcast_to(d_skip, (8, LANES)), sel)[0:1]
    xdt = xs * dt_x
    u = xdt * w_x
    cb = _mxu(cm, bm, NT)
    cs = _mxu(cm, prev, NN)
    dye = dy * e_x
    dcm = _mxu(dye, prev, NT)
    dprev = _mxu(cm, dye, TN) + dnew * g_x
    dacs_x = dye * cs
    dbm = _mxu(u, dnew, NT)
    du = _mxu(bm, dnew, NN)
    dw_x = du * u
    dacs_x = dacs_x - dw_x
    dend_x = jnp.sum(dw_x + dnew * prev * g_x, axis=0, keepdims=True)
    dxdt = du * w_x
    lane = lax.broadcasted_iota(jnp.int32, (q, LANES), 1)
    sub = lax.broadcasted_iota(jnp.int32, (q, LANES), 0)
    dcb = jnp.zeros((q, q), F32)
    dacs = jnp.zeros((q, LANES), F32)
    dacs_t = jnp.zeros((q, LANES), F32)
    masks = _half_masks()
    dxdt_p = []
    for pp in range(SSD_HPG // 2):
        ps = slice(pp * LANES, (pp + 1) * LANES)
        acc = jnp.zeros((q, LANES), F32)
        for half in range(2):
            hh = 2 * pp + half
            decay = jnp.exp(jnp.where(causal, a_cs[:, hh:hh + 1] - a_cs_t[hh:hh + 1, :], -jnp.inf))
            m = cb * decay
            dyh = jnp.where(masks[half], dy[:, ps], 0.0)
            dm = _mxu(dyh, xdt[:, ps], NT)
            acc = acc + _mxu(m, dyh, TN)
            dcb = dcb + dm * decay
            p = dm * m
            dacs = jnp.where(lane == hh, jnp.sum(p, axis=1, keepdims=True), dacs)
            dacs_t = jnp.where(sub == hh, jnp.sum(p, axis=0, keepdims=True), dacs_t)
        dxdt_p.append(acc)
    dcm = dcm + _mxu(dcb, bm, NN)
    dbm = dbm + _mxu(dcb, cm, TN)
    dxdt = dxdt + jnp.concatenate(dxdt_p, axis=1)
    dxs = dy * d_x + dxdt * dt_x
    rows_x = jnp.concatenate([dend_x, jnp.sum(dy * xs, axis=0, keepdims=True), jnp.zeros((6, SSD_GW), F32)], axis=0)
    rows = _head_sums(rows_x, sel)
    dacs = dacs - dacs_t.T + _head_sums(dacs_x, sel)
    dacs = dacs + jnp.where(sub == q - 1, rows[0:1], 0.0)
    tri_t = jnp.where(causal, 0.0, 1.0).astype(F32) + jnp.where(lane == sub, 1.0, 0.0)
    da = jnp.dot(tri_t, dacs, precision=lax.Precision.HIGHEST, preferred_element_type=F32)
    ddt = _head_sums(dxdt * xs, sel) + da * a_neg
    dalog = jnp.sum(da * dt, axis=0, keepdims=True) * a_neg
    ddtr = jnp.where(live, ddt * jax.nn.sigmoid(pre), 0.0)
    dbias = jnp.sum(ddtr, axis=0, keepdims=True)
    return dxs, dbm, dcm, ddtr, dprev, dbias, dalog, rows[1:2]


def _chunks_per_step(nc):
    return 5 if nc % 5 == 0 else 1


def _ssd_specs(rev, nc):
    per = _chunks_per_step(nc)
    steps = nc // per
    sidx = (lambda s: steps - 1 - s) if rev else (lambda s: s)
    nb_b = SSD_D_INNER // SSD_STATE
    row = lambda width, col=lambda g: g: pl.BlockSpec((per * CHUNK, width), lambda g, s: (sidx(s), col(g)))
    par = lambda width: pl.BlockSpec((1, width), lambda g, s: (0, g))
    state = lambda: pl.BlockSpec((per, 1, SSD_STATE, SSD_GW), lambda g, s: (sidx(s), g, 0, 0))
    xbc = [row(SSD_GW), row(SSD_STATE, lambda g: nb_b + g), row(SSD_STATE, lambda g: nb_b + SSD_GROUPS + g)]
    return per, steps, sidx, row, par, state, xbc


def _ssd_fwd(xbc_c, dtr, z, dt_bias, a_log, d_skip, nw):
    rows = z.shape[0]
    nc = rows // CHUNK
    per, steps, _, row, par, state, xbc = _ssd_specs(False, nc)

    def body(xs_ref, b_ref, c_ref, dt_ref, z_ref, bias_ref, al_ref, dk_ref, nw_ref, o_ref, y_ref, st_ref, carry):
        s = pl.program_id(1)

        @pl.when(s == 0)
        def _():
            carry[...] = jnp.zeros_like(carry)

        for j in range(per):
            rs = pl.ds(j * CHUNK, CHUNK)
            live = (s * per + j) * CHUNK + lax.broadcasted_iota(jnp.int32, (CHUNK, 1), 0) >= FRONT
            prev = carry[...]
            st_ref[j, 0] = prev
            y, new = _ssd_scan(xs_ref[rs, :].astype(F32), b_ref[rs, :].astype(F32), c_ref[rs, :].astype(F32), dt_ref[rs, :],
                               prev, bias_ref[...], al_ref[...], dk_ref[...], live)
            y_ref[rs, :] = y.astype(ACT_DTYPE)
            o_ref[rs, :] = _ssd_gate(y, z_ref[rs, :].astype(F32), nw_ref[...]).astype(ACT_DTYPE)
            carry[...] = new

    act = jax.ShapeDtypeStruct((rows, SSD_D_INNER), ACT_DTYPE)
    return pl.pallas_call(
        body, name="ssd_fwd", grid=(SSD_GROUPS, steps),
        in_specs=xbc + [row(LANES), row(SSD_GW), par(LANES), par(LANES), par(LANES), par(SSD_GW)],
        out_specs=[row(SSD_GW), row(SSD_GW), state()],
        out_shape=[act, act, jax.ShapeDtypeStruct((nc, SSD_GROUPS, SSD_STATE, SSD_GW), F32)],
        scratch_shapes=[pltpu.VMEM((SSD_STATE, SSD_GW), F32)],
        compiler_params=_params(("parallel", "arbitrary")),
    )(xbc_c, xbc_c, xbc_c, dtr, z, dt_bias, a_log, d_skip, nw)


def _ssd_gate_bwd(y, z, dout, nw):
    rows = y.shape[0]
    tm = _row_tile(rows)

    def body(y_ref, z_ref, do_ref, nw_ref, dy_ref, dz_ref, dnw_ref):
        yv, zv, dov = y_ref[...].astype(F32), z_ref[...].astype(F32), do_ref[...].astype(F32)
        s = jax.nn.sigmoid(zv)
        silu = zv * s
        yz = yv * silu
        r = lax.rsqrt(jnp.mean(yz * yz, axis=-1, keepdims=True) + EPS)
        yhat = yz * r
        dn = dov * nw_ref[...]
        dyz = r * (dn - yhat * jnp.mean(dn * yhat, axis=-1, keepdims=True))
        dy_ref[...] = (dyz * silu).astype(ACT_DTYPE)
        dz_ref[...] = (dyz * yv * (s + silu * (1.0 - s))).astype(ACT_DTYPE)

        @pl.when(pl.program_id(1) == 0)
        def _():
            dnw_ref[...] = jnp.zeros_like(dnw_ref)

        dnw_ref[...] += jnp.sum(dov * yhat, axis=0, keepdims=True)

    spec = pl.BlockSpec((tm, SSD_GW), lambda g, i: (i, g))
    par = pl.BlockSpec((1, SSD_GW), lambda g, i: (0, g))
    act = jax.ShapeDtypeStruct((rows, SSD_D_INNER), ACT_DTYPE)
    return pl.pallas_call(
        body, name="ssd_gate_bwd", grid=(SSD_GROUPS, rows // tm), in_specs=[spec, spec, spec, par],
        out_specs=[spec, spec, par], out_shape=[act, act, jax.ShapeDtypeStruct((1, SSD_D_INNER), F32)],
        compiler_params=_params(("parallel", "arbitrary")),
    )(y, z, dout, nw)


def _ssd_bwd(xbc_c, dtr, dt_bias, a_log, d_skip, states, dy):
    rows = dy.shape[0]
    nc = rows // CHUNK
    per, steps, sidx, row, par, state, xbc = _ssd_specs(True, nc)

    def body(xs_ref, b_ref, c_ref, dt_ref, bias_ref, al_ref, dk_ref, st_ref, dy_ref,
             dxs_ref, db_ref, dc_ref, ddt_ref, dbias_ref, dal_ref, ddk_ref, carry):
        s = pl.program_id(1)

        @pl.when(s == 0)
        def _():
            carry[...] = jnp.zeros_like(carry)
            for r in (dbias_ref, dal_ref, ddk_ref):
                r[...] = jnp.zeros_like(r)

        for j in reversed(range(per)):
            rs = pl.ds(j * CHUNK, CHUNK)
            live = (sidx(s) * per + j) * CHUNK + lax.broadcasted_iota(jnp.int32, (CHUNK, 1), 0) >= FRONT
            dxs, dbm, dcm, ddt, dprev, dbias, dal, ddk = _ssd_scan_bwd(
                xs_ref[rs, :].astype(F32), b_ref[rs, :].astype(F32), c_ref[rs, :].astype(F32), dt_ref[rs, :], st_ref[j, 0],
                bias_ref[...], al_ref[...], dk_ref[...], live, dy_ref[rs, :].astype(F32), carry[...])
            dxs_ref[rs, :] = dxs.astype(ACT_DTYPE)
            db_ref[rs, :] = dbm.astype(ACT_DTYPE)
            dc_ref[rs, :] = dcm.astype(ACT_DTYPE)
            ddt_ref[rs, :] = ddt
            carry[...] = dprev
            dbias_ref[...] += dbias
            dal_ref[...] += dal
            ddk_ref[...] += ddk

    bc = jax.ShapeDtypeStruct((rows, SSD_GROUPS * SSD_STATE), ACT_DTYPE)
    head = jax.ShapeDtypeStruct((1, SSD_GROUPS * LANES), F32)
    return pl.pallas_call(
        body, name="ssd_bwd", grid=(SSD_GROUPS, steps),
        in_specs=xbc + [row(LANES), par(LANES), par(LANES), par(LANES), state(), row(SSD_GW)],
        out_specs=[row(SSD_GW), row(SSD_STATE), row(SSD_STATE), row(LANES), par(LANES), par(LANES), par(LANES)],
        out_shape=[jax.ShapeDtypeStruct((rows, SSD_D_INNER), ACT_DTYPE), bc, bc,
                   jax.ShapeDtypeStruct((rows, SSD_GROUPS * LANES), F32), head, head, head],
        scratch_shapes=[pltpu.VMEM((SSD_STATE, SSD_GW), F32)],
        compiler_params=_params(("parallel", "arbitrary")),
    )(xbc_c, xbc_c, xbc_c, dtr, dt_bias, a_log, d_skip, states, dy)


def _rotary_tables(rows):
    pos = np.arange(rows, dtype=np.float32) - np.float32(FRONT)
    inv_freq = np.float32(ROPE_BASE) ** (-np.linspace(0.0, 1.0, RET_QK // 2, dtype=np.float32))
    ang = (pos[:, None] * inv_freq[None, :]).astype(np.float32).astype(np.float64)
    lgam = np.log(1.0 - 2.0 ** (-5.0 - np.arange(RET_HEADS, dtype=np.float64))).astype(np.float32)
    lgam = np.broadcast_to(lgam[:, None, None], (RET_HEADS, 8, LANES))
    return jnp.asarray(np.cos(ang).astype(np.float32)), jnp.asarray(np.sin(ang).astype(np.float32)), jnp.asarray(lgam)


def _rotary(t, cos, sin):
    half = t.shape[-1] // 2
    t1, t2 = t[:, :half], t[:, half:]
    return jnp.concatenate([t1 * cos - t2 * sin, t2 * cos + t1 * sin], axis=1)


def _ret_chunk(qh, kh, vh, gh, prev, cos, sin, lg):
    q = CHUNK
    qr = _rotary(qh, cos, sin)
    kr = _rotary(kh, cos, sin) * (RET_QK ** -0.5)
    li = lax.broadcasted_iota(jnp.int32, (q, q), 0)
    si = lax.broadcasted_iota(jnp.int32, (q, q), 1)
    dist = (li - si).astype(F32)
    decay = jnp.exp(jnp.where(li >= si, dist * lg, -jnp.inf))
    idx = lax.broadcasted_iota(jnp.int32, (q, 1), 0).astype(F32)
    scores = _dot(qr, kr, NT) * decay
    out = _dot(scores, vh)
    kv = _dot(kr * jnp.exp((q - 1.0 - idx) * lg), vh, TN)
    out = out + _dot(qr, prev) * jnp.exp((idx + 1.0) * lg)
    new = prev * jnp.exp(q * lg) + kv
    out = out * lax.rsqrt(jnp.mean(out * out, axis=-1, keepdims=True) + EPS)
    return _silu(gh) * out, new


def _ret_specs(rev, nc):
    per = _chunks_per_step(nc)
    steps = nc // per
    sidx = (lambda s: steps - 1 - s) if rev else (lambda s: s)
    row = lambda width: pl.BlockSpec((per * CHUNK, width), lambda h, s: (sidx(s), h))
    tab = lambda: pl.BlockSpec((per * CHUNK, RET_QK // 2), lambda h, s: (sidx(s), 0))
    lgs = lambda: pl.BlockSpec((1, 8, LANES), lambda h, s: (h, 0, 0))
    state = lambda: pl.BlockSpec((per, 1, RET_QK, RET_V), lambda h, s: (sidx(s), h, 0, 0))
    ins = [row(RET_QK), row(RET_QK), row(RET_V), row(RET_V), tab(), tab(), lgs()]
    return per, steps, row, state, ins


def _ret_fwd(q, k, v, g, cos, sin, lgam):
    rows = q.shape[0]
    nc = rows // CHUNK
    per, steps, row, state, ins = _ret_specs(False, nc)

    def body(q_ref, k_ref, v_ref, g_ref, cos_ref, sin_ref, lg_ref, y_ref, st_ref, carry):
        @pl.when(pl.program_id(1) == 0)
        def _():
            carry[...] = jnp.zeros_like(carry)

        for j in range(per):
            rs = pl.ds(j * CHUNK, CHUNK)
            prev = carry[...]
            st_ref[j, 0] = prev.astype(ACT_DTYPE)
            out, new = _ret_chunk(q_ref[rs, :].astype(F32), k_ref[rs, :].astype(F32), v_ref[rs, :].astype(F32),
                                  g_ref[rs, :].astype(F32), prev, cos_ref[rs, :], sin_ref[rs, :], lg_ref[0, 0:1, 0:1])
            y_ref[rs, :] = out.astype(ACT_DTYPE)
            carry[...] = new

    return pl.pallas_call(
        body, name="ret_fwd", grid=(RET_HEADS, steps), in_specs=ins, out_specs=[row(RET_V), state()],
        out_shape=[jax.ShapeDtypeStruct((rows, RET_HEADS * RET_V), ACT_DTYPE),
                   jax.ShapeDtypeStruct((nc, RET_HEADS, RET_QK, RET_V), ACT_DTYPE)],
        scratch_shapes=[pltpu.VMEM((RET_QK, RET_V), F32)],
        compiler_params=_params(("parallel", "arbitrary")),
    )(q, k, v, g, cos, sin, lgam)


def _ret_bwd(q, k, v, g, cos, sin, lgam, states, dy):
    rows = q.shape[0]
    nc = rows // CHUNK
    per, steps, row, state, ins = _ret_specs(True, nc)

    def body(q_ref, k_ref, v_ref, g_ref, cos_ref, sin_ref, lg_ref, st_ref, dy_ref, dq_ref, dk_ref, dv_ref, dg_ref, carry):
        @pl.when(pl.program_id(1) == 0)
        def _():
            carry[...] = jnp.zeros_like(carry)

        for j in reversed(range(per)):
            rs = pl.ds(j * CHUNK, CHUNK)
            fn = functools.partial(_ret_chunk, cos=cos_ref[rs, :], sin=sin_ref[rs, :], lg=lg_ref[0, 0:1, 0:1])
            _, vjp = jax.vjp(fn, q_ref[rs, :].astype(F32), k_ref[rs, :].astype(F32), v_ref[rs, :].astype(F32),
                             g_ref[rs, :].astype(F32), st_ref[j, 0].astype(F32))
            dq, dk, dv, dg, dprev = vjp((dy_ref[rs, :].astype(F32), carry[...]))
            dq_ref[rs, :] = dq.astype(ACT_DTYPE)
            dk_ref[rs, :] = dk.astype(ACT_DTYPE)
            dv_ref[rs, :] = dv.astype(ACT_DTYPE)
            dg_ref[rs, :] = dg.astype(ACT_DTYPE)
            carry[...] = dprev

    shp = lambda width: jax.ShapeDtypeStruct((rows, RET_HEADS * width), ACT_DTYPE)
    return pl.pallas_call(
        body, name="ret_bwd", grid=(RET_HEADS, steps), in_specs=ins + [state(), row(RET_V)],
        out_specs=[row(RET_QK), row(RET_QK), row(RET_V), row(RET_V)],
        out_shape=[shp(RET_QK), shp(RET_QK), shp(RET_V), shp(RET_V)],
        scratch_shapes=[pltpu.VMEM((RET_QK, RET_V), F32)],
        compiler_params=_params(("parallel", "arbitrary")),
    )(q, k, v, g, cos, sin, lgam, states, dy)


def _merge_fwd(bs, br, gs, gr):
    rows, d = bs.shape
    tm = _row_tile(rows)

    def body(bs_ref, br_ref, gs_ref, gr_ref, o_ref):
        o_ref[...] = (jax.nn.sigmoid(gs_ref[...].astype(F32)) * bs_ref[...].astype(F32)
                      + jax.nn.sigmoid(gr_ref[...].astype(F32)) * br_ref[...].astype(F32)).astype(ACT_DTYPE)

    spec = pl.BlockSpec((tm, d), lambda i: (i, 0))
    return pl.pallas_call(
        body, name="merge_fwd", grid=(rows // tm,), in_specs=[spec] * 4, out_specs=spec,
        out_shape=jax.ShapeDtypeStruct((rows, d), ACT_DTYPE), compiler_params=_params(("parallel",)),
    )(bs, br, gs, gr)


def _merge_bwd(dm, bs, br, gs, gr):
    rows, d = bs.shape
    tm = _row_tile(rows)

    def body(dm_ref, bs_ref, br_ref, gs_ref, gr_ref, dbs_ref, dbr_ref, dgs_ref, dgr_ref):
        dmv = dm_ref[...].astype(F32)
        for b_ref, g_ref, db_ref, dg_ref in ((bs_ref, gs_ref, dbs_ref, dgs_ref), (br_ref, gr_ref, dbr_ref, dgr_ref)):
            s = jax.nn.sigmoid(g_ref[...].astype(F32))
            db_ref[...] = (dmv * s).astype(ACT_DTYPE)
            dg_ref[...] = (dmv * b_ref[...].astype(F32) * s * (1.0 - s)).astype(ACT_DTYPE)

    spec = pl.BlockSpec((tm, d), lambda i: (i, 0))
    shp = jax.ShapeDtypeStruct((rows, d), ACT_DTYPE)
    return pl.pallas_call(
        body, name="merge_bwd", grid=(rows // tm,), in_specs=[spec] * 5, out_specs=[spec] * 4,
        out_shape=[shp] * 4, compiler_params=_params(("parallel",)),
    )(dm, bs, br, gs, gr)


def _place():
    x, y, c = lax.axis_index("x"), lax.axis_index("y"), lax.axis_index("c")
    return x, y, c


def _slot(p):
    return 4 * p[0] + 2 * p[1] + p[2]


def _allgather(arrs, name):
    n = len(arrs)
    any_spec = pl.BlockSpec(memory_space=pl.ANY)

    def body(*refs):
        ins, outs = refs[:n], refs[n:2 * n]
        send_sems, recv_sems, local_sems = refs[2 * n:]
        x, y, c = _place()
        me, sibling = (x, y, c), (x, y, 1 - c)
        chips = [(1 - x, y), (x, 1 - y), (1 - x, 1 - y)]

        def copy(a, k, block, to, src=None):
            dst = outs[a].at[_slot(block)]
            return pltpu.make_async_remote_copy(
                src_ref=dst if src is None else src, dst_ref=dst, send_sem=send_sems.at[a * 7 + k],
                recv_sem=recv_sems.at[a * 7 + k], device_id=to, device_id_type=MESH)

        mine, first, passed = [], [], []
        for a in range(n):
            cp = pltpu.make_async_copy(ins[a], outs[a].at[_slot(me)], local_sems.at[a])
            cp.start()
            mine.append(cp)
            first.append(copy(a, 0, me, sibling, src=ins[a]))
            first += [copy(a, 1 + j, me, (*chip, c), src=ins[a]) for j, chip in enumerate(chips)]
        for cp in first:
            cp.start()
        for j, chip in enumerate(chips):
            for a in range(n):
                copy(a, 1 + j, (*chip, c), me).wait_recv()
                cp = copy(a, 4 + j, (*chip, c), sibling)
                cp.start()
                passed.append(cp)
        for a in range(n):
            copy(a, 0, sibling, me).wait_recv()
            for j, chip in enumerate(chips):
                copy(a, 4 + j, (*chip, 1 - c), me).wait_recv()
        for cp in first + passed:
            cp.wait_send()
        for cp in mine:
            cp.wait()

    return pl.pallas_call(
        body, name=name, in_specs=[any_spec] * n, out_specs=[any_spec] * n,
        out_shape=[jax.ShapeDtypeStruct((N_DEV,) + a.shape, a.dtype) for a in arrs],
        scratch_shapes=[pltpu.SemaphoreType.DMA((7 * n,)), pltpu.SemaphoreType.DMA((7 * n,)), pltpu.SemaphoreType.DMA((n,))],
    )(*arrs)


def _peers():
    x, y, c = _place()
    return (x, y, c), [(x ^ dx, y ^ dy, c ^ dc) for dx in (0, 1) for dy in (0, 1) for dc in (0, 1)][1:]


def _exchange_copies(srcs, lands, send_sems, recv_sems, scatter, sender):
    me, peers = _peers()
    out = []
    for a, (src, land) in enumerate(zip(srcs, lands, strict=True)):
        for k, peer in enumerate(peers):
            src_ref = src.at[_slot(peer)] if scatter else src
            out.append(pltpu.make_async_remote_copy(
                src_ref=src_ref, dst_ref=land.at[_slot(me if sender else peer)], send_sem=send_sems.at[a * 7 + k],
                recv_sem=recv_sems.at[a * 7 + k], device_id=peer, device_id_type=MESH))
    return out


_HBM = pl.BlockSpec(memory_space=pltpu.HBM)
_SEM = pl.BlockSpec(memory_space=pltpu.SEMAPHORE)
_EFFECT = pltpu.SideEffectType.DATAFLOW_SIDE_EFFECTING


def _exchange_start(srcs, scatter, name):
    n = len(srcs)
    land_shapes = [s.shape if scatter else (N_DEV,) + s.shape for s in srcs]

    def body(*refs):
        for cp in _exchange_copies(refs[:n], refs[n:2 * n], refs[2 * n], refs[2 * n + 1], scatter, True):
            cp.start()
        refs[-1][...] = jnp.zeros_like(refs[-1])

    args = [pltpu.with_memory_space_constraint(s, pltpu.HBM) for s in srcs]
    args += [pltpu.with_memory_space_constraint(lax.empty(shp, s.dtype), pltpu.HBM) for s, shp in zip(srcs, land_shapes)]
    outs = pl.pallas_call(
        body, name=name,
        out_shape=(pltpu.SemaphoreType.DMA((7 * n,)), pltpu.SemaphoreType.DMA((7 * n,)))
        + tuple(pltpu.HBM(a.shape, a.dtype) for a in args) + (jax.ShapeDtypeStruct((8, LANES), F32),),
        in_specs=[_HBM] * (2 * n), out_specs=(_SEM, _SEM) + (_HBM,) * (2 * n) + (pl.BlockSpec(memory_space=pltpu.VMEM),),
        input_output_aliases={i: 2 + i for i in range(2 * n)},
        compiler_params=pltpu.CompilerParams(has_side_effects=_EFFECT),
    )(*args)
    return outs[:-1], outs[-1]


def _exchange_wait(handle, scatter, after, name):
    n = (len(handle) - 2) // 2
    thru = handle[2:]

    def body(*refs):
        for cp in _exchange_copies(refs[:n], refs[n:2 * n], refs[2 * n], refs[2 * n + 1], scatter, False):
            cp.wait_send()
            cp.wait_recv()

    outs = pl.pallas_call(
        body, name=name, out_shape=tuple(pltpu.HBM(t.shape, t.dtype) for t in thru),
        in_specs=[_HBM] * (2 * n) + [_SEM, _SEM, pl.BlockSpec(memory_space=pl.ANY)], out_specs=(_HBM,) * (2 * n),
        input_output_aliases={i: i for i in range(2 * n)},
        compiler_params=pltpu.CompilerParams(has_side_effects=_EFFECT),
    )(*thru, handle[0], handle[1], after)
    return list(outs[:n]), list(outs[n:])


def _allreduce_small(pack):
    rows, lanes = pack.shape

    def body(x_ref, o_ref, buf, send_sems, recv_sems):
        x, y, c = _place()
        me, sibling = (x, y, c), (x, y, 1 - c)
        chips = [(1 - x, y), (x, 1 - y), (1 - x, 1 - y)]

        def copy(k, block, to, src=None):
            dst = buf.at[_slot(block)]
            return pltpu.make_async_remote_copy(
                src_ref=dst if src is None else src, dst_ref=dst, send_sem=send_sems.at[k], recv_sem=recv_sems.at[k],
                device_id=to, device_id_type=MESH)

        buf[_slot(me)] = x_ref[...]
        first = [copy(0, me, sibling, src=x_ref)]
        first += [copy(1 + j, me, (*chip, c), src=x_ref) for j, chip in enumerate(chips)]
        for cp in first:
            cp.start()
        passed = [copy(4 + j, (*chip, c), sibling) for j, chip in enumerate(chips)]
        for j, chip in enumerate(chips):
            copy(1 + j, (*chip, c), me).wait_recv()
            passed[j].start()
        copy(0, sibling, me).wait_recv()
        for j, chip in enumerate(chips):
            copy(4 + j, (*chip, 1 - c), me).wait_recv()
        for cp in first + passed:
            cp.wait_send()
        acc = buf[0]
        for i in range(1, N_DEV):
            acc = acc + buf[i]
        o_ref[...] = acc

    vmem = pl.BlockSpec(memory_space=pltpu.VMEM)
    return pl.pallas_call(
        body, name="allreduce_small", in_specs=[vmem], out_specs=vmem,
        out_shape=jax.ShapeDtypeStruct((rows, lanes), F32),
        scratch_shapes=[pltpu.VMEM((N_DEV, rows, lanes), F32), pltpu.SemaphoreType.DMA((7,)), pltpu.SemaphoreType.DMA((7,))],
        compiler_params=pltpu.CompilerParams(vmem_limit_bytes=VMEM_LIMIT),
    )(pack)


def _adamw(w, g, m, v):
    m = ADAM_B1 * m + (1.0 - ADAM_B1) * g
    v = ADAM_B2 * v + (1.0 - ADAM_B2) * jnp.square(g)
    m_hat = m / (1.0 - ADAM_B1 ** ADAM_STEP)
    v_hat = v / (1.0 - ADAM_B2 ** ADAM_STEP)
    delta = -ADAM_LR * (m_hat / (jnp.sqrt(v_hat) + ADAM_EPS) + ADAM_WD * w)
    return delta, m, v


def _adam_shard(own, parts, w, m, v, name):
    r, c = w.shape
    tr = _pick(r, (128, 64, 32, 16, 8))

    def body(own_ref, p_ref, w_ref, m_ref, v_ref, g_ref, d_ref, nm_ref, nv_ref):
        _, peers = _peers()
        g = own_ref[...].astype(F32)
        for peer in peers:
            g = g + p_ref[_slot(peer)].astype(F32)
        g_ref[...] = g
        d_ref[...], nm_ref[...], nv_ref[...] = _adamw(w_ref[...], g, m_ref[...], v_ref[...])

    spec = pl.BlockSpec((tr, c), lambda i: (i, 0))
    shp = jax.ShapeDtypeStruct((r, c), F32)
    return pl.pallas_call(
        body, name=name, grid=(r // tr,),
        in_specs=[spec, pl.BlockSpec((N_DEV, tr, c), lambda i: (0, i, 0)), spec, spec, spec], out_specs=[spec] * 4,
        out_shape=[shp] * 4, compiler_params=_params(("parallel",)),
    )(own, parts, w, m, v)


def _adam_small(w, g, m, v):
    r, c = w.shape

    def body(w_ref, g_ref, m_ref, v_ref, d_ref, nm_ref, nv_ref):
        d_ref[...], nm_ref[...], nv_ref[...] = _adamw(w_ref[...], g_ref[...], m_ref[...], v_ref[...])

    shp = jax.ShapeDtypeStruct((r, c), F32)
    return pl.pallas_call(body, name="adam_small", out_shape=[shp] * 3)(w, g, m, v)


def _column_plan(pieces, shard_w):
    plan = []
    for c0, width, d0 in pieces:
        c = c0
        while c < c0 + width:
            s, a = divmod(c, shard_w)
            w = min(c0 + width - c, shard_w - a)
            plan.append((s, a, w, d0 + c - c0))
            c += w
    return plan


def _cols_from_shards(g, plan, out_w, zero, name):
    _, r, sw = g.shape
    tr = _pick(r, (128,))

    def body(x_ref, o_ref):
        for d0, w in zero:
            o_ref[:, d0:d0 + w] = jnp.zeros((tr, w), g.dtype)
        for s, a, w, d0 in plan:
            o_ref[:, d0:d0 + w] = x_ref[s, :, a:a + w]

    return pl.pallas_call(
        body, name=name, grid=(r // tr,), in_specs=[pl.BlockSpec((N_DEV, tr, sw), lambda i: (0, i, 0))],
        out_specs=pl.BlockSpec((tr, out_w), lambda i: (i, 0)), out_shape=jax.ShapeDtypeStruct((r, out_w), g.dtype),
        compiler_params=_params(("parallel",)),
    )(g)


def _shards_from_cols(srcs, plans, shard_w, name):
    r = srcs[0].shape[0]
    tr = _pick(r, (128,))
    n = len(srcs)

    def body(*refs):
        o_ref = refs[n]
        for x_ref, plan in zip(refs[:n], plans, strict=True):
            for s, a, w, d0 in plan:
                o_ref[s, :, a:a + w] = x_ref[:, d0:d0 + w].astype(COMM_DTYPE)

    return pl.pallas_call(
        body, name=name, grid=(r // tr,), in_specs=[pl.BlockSpec((tr, t.shape[1]), lambda i: (i, 0)) for t in srcs],
        out_specs=pl.BlockSpec((N_DEV, tr, shard_w), lambda i: (0, i, 0)),
        out_shape=jax.ShapeDtypeStruct((N_DEV, r, shard_w), COMM_DTYPE), compiler_params=_params(("parallel",)),
    )(*srcs)


def _pack(arrs):
    rows = []
    for a in arrs:
        flat = a.reshape(-1).astype(F32)
        rows.append(jnp.pad(flat, (0, (-flat.shape[0]) % (8 * LANES))).reshape(-1, LANES))
    return jnp.concatenate(rows, axis=0)


def _unpack(pack, shapes):
    out, r = [], 0
    for s in shapes:
        size = math.prod(s)
        nr = -(-size // (8 * LANES)) * 8
        out.append(pack[r:r + nr].reshape(-1)[:size].reshape(s))
        r += nr
    return out


def _group_lanes(t):
    lead = t.shape[:-1]
    t = t.reshape(lead + (SSD_GROUPS, SSD_HPG))
    t = jnp.pad(t, [(0, 0)] * len(lead) + [(0, 0), (0, LANES - SSD_HPG)])
    return t.reshape(lead + (SSD_GROUPS * LANES,))


def _ungroup_lanes(t):
    lead = t.shape[:-1]
    return t.reshape(lead + (SSD_GROUPS, LANES))[..., :SSD_HPG].reshape(lead + (SSD_HEADS,))


def kernel(x, meta_tokens, mix_norm_w, w_in, ssd_conv_w, ssd_conv_b, ssd_dt_bias, ssd_A_log, ssd_D, ssd_norm_w, w_branch_ssd, w_branch_ret, w_out, ffn_norm_w, w_up, ffn_conv_w, ffn_conv_b, w_down, final_norm_w, loss_target, m_meta_tokens, m_mix_norm_w, m_w_in, m_ssd_conv_w, m_ssd_conv_b, m_ssd_dt_bias, m_ssd_A_log, m_ssd_D, m_ssd_norm_w, m_w_branch_ssd, m_w_branch_ret, m_w_out, m_ffn_norm_w, m_w_up, m_ffn_conv_w, m_ffn_conv_b, m_w_down, m_final_norm_w, v_meta_tokens, v_mix_norm_w, v_w_in, v_ssd_conv_w, v_ssd_conv_b, v_ssd_dt_bias, v_ssd_A_log, v_ssd_D, v_ssd_norm_w, v_w_branch_ssd, v_w_branch_ret, v_w_out, v_ffn_norm_w, v_w_up, v_ffn_conv_w, v_ffn_conv_b, v_w_down, v_final_norm_w):
    seq, d = x.shape[1], x.shape[2]
    rows = seq + PAD_ROWS
    tm = _row_tile(rows)
    me = _slot(_place())
    d_ff = w_down.shape[1] * N_DEV

    big = [w_in[0], w_branch_ssd[0], w_branch_ret[0], w_out[0], w_up[0], w_down[0]]
    first = _allgather([w_in[0].astype(COMM_DTYPE), meta_tokens, ssd_conv_w[0], ffn_conv_w[0]], "gather_first")
    rest_src = [b.astype(COMM_DTYPE) for b in big[1:]]
    rest_handle, rest_token = _exchange_start(rest_src, False, "gather_rest_start")
    cols = lambda t: jnp.transpose(t, (1, 0, 2)).reshape(t.shape[1], -1)
    rws = lambda t: t.reshape(-1, t.shape[2])
    conv_w, fconv_w = cols(first[2]), cols(first[3])
    meta_full = cols(first[1]) + rest_token[0, 0]
    widths = [SSD_D_INNER, SSD_CONV_DIM, SSD_HEADS, RET_HEADS * RET_QK, RET_HEADS * RET_QK, RET_HEADS * RET_V,
              RET_HEADS * RET_V, d, d]
    offs = [0]
    for wd in widths:
        offs.append(offs[-1] + wd)
    order_p = [0, 1, 3, 4, 5, 6, 7, 8, 2]
    offs_p = [0]
    for i in order_p:
        offs_p.append(offs_p[-1] + (SSD_GROUPS * LANES if i == 2 else widths[i]))
    in_p = offs_p[-1]
    dt0 = offs_p[order_p.index(2)]
    pieces = [(offs[i], widths[i], offs_p[j]) for j, i in enumerate(order_p) if i != 2]
    pieces += [(offs[2] + SSD_HPG * grp, SSD_HPG, dt0 + LANES * grp) for grp in range(SSD_GROUPS)]
    in_plan = _column_plan(pieces, w_in.shape[2])
    w_in_p = _cols_from_shards(first[0], in_plan, in_p, [(dt0, SSD_GROUPS * LANES)], "w_in_columns")

    h0 = jnp.concatenate([jnp.zeros((FRONT, d), F32), meta_full, x[0]], axis=0)
    u1 = _rms_fwd(h0, mix_norm_w, "rms1")
    proj = [None if i == 1 else
            _mm(u1, w_in_p, mode="nn", out_dtype=F32 if i == 2 else ACT_DTYPE, tm=tm, tk=d, name=f"in_proj_{i}",
                tn=_pick(offs_p[j + 1] - offs_p[j], (1024, 512)), b_n0=offs_p[j], n_out=offs_p[j + 1] - offs_p[j])
            for j, i in enumerate(order_p)]
    z, _, q, k, v, g, gs, gr, dtr = proj
    xbc, xbc_c = _xbc_proj_conv(u1, w_in_p, offs_p[1], conv_w, ssd_conv_b)
    bias_p, alog_p, dsk_p = _group_lanes(ssd_dt_bias), _group_lanes(ssd_A_log), _group_lanes(ssd_D)
    y_ssd, y_scan, ssd_states = _ssd_fwd(xbc_c, dtr, z, bias_p, alog_p, dsk_p, ssd_norm_w)
    cos, sin, lgam = _rotary_tables(rows)
    y_ret, ret_states = _ret_fwd(q, k, v, g, cos, sin, lgam)
    rest_own, rest = _exchange_wait(rest_handle, False, y_ret, "gather_rest_wait")
    rest = [lax.dynamic_update_index_in_dim(land, own, me, 0) for land, own in zip(rest, rest_own, strict=True)]
    w_bs, w_br, w_o, w_dn = rws(rest[0]), rws(rest[1]), rws(rest[2]), rws(rest[4])
    w_up_f = _cols_from_shards(rest[3], _column_plan([(0, 2 * d_ff, 0)], w_up.shape[2]), 2 * d_ff, [], "w_up_columns")
    bs = _mm(y_ssd, w_bs, mode="nn", out_dtype=ACT_DTYPE, tm=tm, tn=d, tk=SSD_D_INNER, name="branch_ssd")
    br = _mm(y_ret, w_br, mode="nn", out_dtype=ACT_DTYPE, tm=tm, tn=d, tk=RET_HEADS * RET_V, name="branch_ret")
    merged = _merge_fwd(bs, br, gs, gr)
    h1 = _mm(merged, w_o, mode="nn", out_dtype=F32, tm=tm, tn=d, tk=d, name="out_proj", add=h0)
    u2 = _rms_fwd(h1, ffn_norm_w, "rms2")
    up_g, up_v, act = _ffn_up_conv(u2, w_up_f, fconv_w, ffn_conv_b)
    h2 = _mm(act, w_dn, mode="nn", out_dtype=F32, tm=tm, tn=d, tk=d_ff, name="ffn_down", add=h1)
    tgt = jnp.pad(loss_target[0], ((PAD_ROWS, 0), (0, 0)))
    dh2, loss_acc, g_final = _loss_head(h2, tgt, final_norm_w.reshape(1, d))

    tff = _pick(d_ff, (1408, 256))
    tkr = _pick(rows, (1664, 128))
    rparts = lambda t: t.reshape(N_DEV, -1, t.shape[1])
    d_act = _mm(dh2, w_dn, mode="nt", out_dtype=ACT_DTYPE, tm=tm, tn=tff, tk=d, name="d_act")
    g_w_dn = _mm(act, dh2, mode="tn", out_dtype=COMM_DTYPE, tm=tff, tn=d, tk=tkr, name="g_w_down")
    c_dn = [rparts(g_w_dn)]
    h_dn, t_dn = _exchange_start(c_dn, True, "scatter_down_start")
    d_up_g, d_up_v, g_fcw_g, g_fcb_g, g_fcw_v, g_fcb_v, g_w_up_g, g_w_up_v = _ffn_conv_bwd(
        up_g, up_v, d_act, fconv_w, ffn_conv_b + t_dn[0, 0], u2)
    g_fconv_w = jnp.concatenate([g_fcw_g, g_fcw_v], axis=1)
    g_fconv_b = jnp.concatenate([g_fcb_g, g_fcb_v], axis=1)
    c_up = [_shards_from_cols([g_w_up_g, g_w_up_v], [_column_plan([(0, d_ff, 0)], w_up.shape[2]),
                                                     _column_plan([(d_ff, d_ff, 0)], w_up.shape[2])], w_up.shape[2], "g_w_up_shards")]
    h_up, t_up = _exchange_start(c_up, True, "scatter_up_start")
    du2 = _mm(d_up_g, w_up_f, mode="nt", out_dtype=F32, tm=tm, tn=d, tk=d_ff, name="d_u2_gate", after=t_up)
    du2 = _mm(d_up_v, w_up_f, mode="nt", out_dtype=F32, tm=tm, tn=d, tk=d_ff, name="d_u2_value", add=du2, b_k0=d_ff)
    dh1, g_ffn_norm = _rms_bwd(du2, h1, ffn_norm_w, dh2, "rms2_bwd")
    d_merged = _mm(dh1, w_o, mode="nt", out_dtype=F32, tm=tm, tn=d, tk=d, name="d_merged")
    g_w_o = _mm(merged, dh1, mode="tn", out_dtype=COMM_DTYPE, tm=d, tn=d, tk=tkr, name="g_w_out")
    d_bs, d_br, d_gs, d_gr = _merge_bwd(d_merged, bs, br, gs, gr)
    d_yssd = _mm(d_bs, w_bs, mode="nt", out_dtype=ACT_DTYPE, tm=tm, tn=1024, tk=d, name="d_y_ssd")
    g_w_bs = _mm(y_ssd, d_bs, mode="tn", out_dtype=COMM_DTYPE, tm=1024, tn=d, tk=tkr, name="g_w_branch_ssd")
    d_yret = _mm(d_br, w_br, mode="nt", out_dtype=ACT_DTYPE, tm=tm, tn=1024, tk=d, name="d_y_ret")
    g_w_br = _mm(y_ret, d_br, mode="tn", out_dtype=COMM_DTYPE, tm=1024, tn=d, tk=tkr, name="g_w_branch_ret")
    c_mid = [rparts(g_w_bs), rparts(g_w_br), rparts(g_w_o)]
    h_mid, t_mid = _exchange_start(c_mid, True, "scatter_mid_start")
    d_yscan, d_z, g_nw = _ssd_gate_bwd(y_scan, z, d_yssd, ssd_norm_w + t_mid[0, 0])
    dxs, d_bm, d_cm, d_dtr, g_bias_p, g_alog_p, g_dsk_p = _ssd_bwd(xbc_c, dtr, bias_p, alog_p, dsk_p, ssd_states, d_yscan)
    d_xbc, g_conv_w, g_conv_b = _ssd_conv_bwd(xbc, jnp.concatenate([dxs, d_bm, d_cm], axis=1), conv_w, ssd_conv_b)
    d_q, d_k, d_v, d_g = _ret_bwd(q, k, v, g, cos, sin, lgam, ret_states, d_yret)
    dproj = jnp.concatenate([d_z, d_xbc, d_q, d_k, d_v, d_g, d_gs, d_gr, d_dtr.astype(ACT_DTYPE)], axis=1)
    g_w_in_p = _mm(u1, dproj, mode="tn", out_dtype=F32, tm=d, tn=_pick(in_p, (768, 512)), tk=tkr, name="g_w_in")
    c_in = [_shards_from_cols([g_w_in_p], [in_plan], w_in.shape[2], "g_w_in_shards")]
    h_in, t_in = _exchange_start(c_in, True, "scatter_in_start")
    du1 = _mm(dproj, w_in_p, mode="nt", out_dtype=F32, tm=tm, tn=d, tk=_pick(in_p, (4608, 512)), name="d_u1", after=t_in)
    dh0, g_mix_norm = _rms_bwd(du1, h0, mix_norm_w, dh1, "rms1_bwd")
    grad_x = dh0[PAD_ROWS:][None]

    landed = {}
    for key, handle, names in (("in", h_in, ["w_in"]), ("mid", h_mid, ["w_branch_ssd", "w_branch_ret", "w_out"]),
                               ("up", h_up, ["w_up"]), ("down", h_dn, ["w_down"])):
        srcs, lands = _exchange_wait(handle, True, dh0, f"scatter_{key}_wait")
        for nm, land, src in zip(names, lands, srcs, strict=True):
            landed[nm] = (lax.dynamic_index_in_dim(src, me, 0, keepdims=False), land)
    big_m = [m_w_in, m_w_branch_ssd, m_w_branch_ret, m_w_out, m_w_up, m_w_down]
    big_v = [v_w_in, v_w_branch_ssd, v_w_branch_ret, v_w_out, v_w_up, v_w_down]
    big_names = ["w_in", "w_branch_ssd", "w_branch_ret", "w_out", "w_up", "w_down"]
    big_out = {}
    for nm, w, m, v_ in zip(big_names, big, big_m, big_v, strict=True):
        big_out[nm] = [t[None] for t in _adam_shard(*landed[nm], w, m[0], v_[0], "adam_" + nm)]

    small_g = [dh0[FRONT:PAD_ROWS], g_mix_norm, g_conv_w, g_conv_b, _ungroup_lanes(g_bias_p), _ungroup_lanes(g_alog_p),
               _ungroup_lanes(g_dsk_p), g_nw, g_ffn_norm, g_fconv_w, g_fconv_b, g_final, loss_acc[0:1, 0:1]]
    total = _unpack(_allreduce_small(_pack(small_g)), [t.shape for t in small_g])
    loss = total[12].reshape(())
    shard = lambda t, width: lax.dynamic_slice_in_dim(t, me * width, width, axis=1)
    small_names = ["meta_tokens", "mix_norm_w", "ssd_conv_w", "ssd_conv_b", "ssd_dt_bias", "ssd_A_log", "ssd_D", "ssd_norm_w",
                   "ffn_norm_w", "ffn_conv_w", "ffn_conv_b", "final_norm_w"]
    small_w = [meta_tokens, mix_norm_w, ssd_conv_w, ssd_conv_b, ssd_dt_bias, ssd_A_log, ssd_D, ssd_norm_w, ffn_norm_w,
               ffn_conv_w, ffn_conv_b, final_norm_w]
    small_m = [m_meta_tokens, m_mix_norm_w, m_ssd_conv_w, m_ssd_conv_b, m_ssd_dt_bias, m_ssd_A_log, m_ssd_D, m_ssd_norm_w,
               m_ffn_norm_w, m_ffn_conv_w, m_ffn_conv_b, m_final_norm_w]
    small_v = [v_meta_tokens, v_mix_norm_w, v_ssd_conv_w, v_ssd_conv_b, v_ssd_dt_bias, v_ssd_A_log, v_ssd_D, v_ssd_norm_w,
               v_ffn_norm_w, v_ffn_conv_w, v_ffn_conv_b, v_final_norm_w]
    grads = total[:12]
    grads[0] = shard(grads[0], meta_tokens.shape[1])
    grads[2] = shard(grads[2], ssd_conv_w.shape[2])
    grads[9] = shard(grads[9], ffn_conv_w.shape[2])
    grads = [t.reshape(w.shape) for t, w in zip(grads, small_w, strict=True)]
    shapes = [w.shape for w in small_w]
    upd = _adam_small(_pack(small_w), _pack(grads), _pack(small_m), _pack(small_v))
    small_out = {nm: [gr_] + [u[i] for u in (_unpack(t, shapes) for t in upd)]
                 for i, (nm, gr_) in enumerate(zip(small_names, grads, strict=True))}

    order = ["meta_tokens", "mix_norm_w", "w_in", "ssd_conv_w", "ssd_conv_b", "ssd_dt_bias", "ssd_A_log", "ssd_D", "ssd_norm_w",
             "w_branch_ssd", "w_branch_ret", "w_out", "ffn_norm_w", "w_up", "ffn_conv_w", "ffn_conv_b", "w_down", "final_norm_w"]
    res = {**big_out, **small_out}
    return (loss, grad_x, *[res[nm][0] for nm in order], *[res[nm][1] for nm in order], *[res[nm][2] for nm in order],
            *[res[nm][3] for nm in order])
```

```python
import functools
import math

import jax
import jax.numpy as jnp
import numpy as np
from jax import lax
from jax.experimental import pallas as pl
from jax.experimental.pallas import tpu as pltpu

F32 = jnp.float32
MXU_DTYPE = jnp.bfloat16
ACT_DTYPE = jnp.bfloat16
COMM_DTYPE = jnp.bfloat16

N_META = 16
CHUNK = 128
FRONT = CHUNK - N_META
PAD_ROWS = FRONT + N_META
EPS = 1e-6
N_DEV = 8

SSD_D_INNER = 2048
SSD_HEAD_DIM = 64
SSD_HEADS = 32
SSD_GROUPS = 4
SSD_HPG = SSD_HEADS // SSD_GROUPS
SSD_STATE = 128
SSD_CONV = 4
SSD_CONV_DIM = SSD_D_INNER + 2 * SSD_GROUPS * SSD_STATE
SSD_GW = SSD_D_INNER // SSD_GROUPS
RET_HEADS = 4
RET_QK = 256
RET_V = 512
ROPE_BASE = 10000.0
FFN_CONV = 3
HALO = 16
LANES = 128

ADAM_LR = 0.001
ADAM_B1 = 0.9
ADAM_B2 = 0.999
ADAM_EPS = 1e-08
ADAM_WD = 0.01
ADAM_STEP = 10

VMEM_LIMIT = 56 * 1024 * 1024
MESH = pl.DeviceIdType.MESH

NN = (((1,), (0,)), ((), ()))
NT = (((1,), (1,)), ((), ()))
TN = (((0,), (0,)), ((), ()))


def _params(sem):
    return pltpu.CompilerParams(dimension_semantics=sem, vmem_limit_bytes=VMEM_LIMIT)


def _mxu(a, b, dn):
    return lax.dot_general(a.astype(MXU_DTYPE), b.astype(MXU_DTYPE), dn, preferred_element_type=F32)


@functools.partial(jax.custom_vjp, nondiff_argnums=(2,))
def _dot(a, b, dn=NN):
    return _mxu(a, b, dn)


def _dot_fwd(a, b, dn):
    return _mxu(a, b, dn), (a, b)


def _dot_bwd(dn, res, g):
    a, b = res
    if dn == NN:
        return _mxu(g, b, NT), _mxu(a, g, TN)
    if dn == NT:
        return _mxu(g, b, NN), _mxu(g, a, TN)
    assert dn == TN
    return _mxu(b, g, NT), _mxu(a, g, NN)


_dot.defvjp(_dot_fwd, _dot_bwd)


def _silu(x):
    return x * jax.nn.sigmoid(x)


def _dsilu(x):
    s = jax.nn.sigmoid(x)
    return s * (1.0 + x * (1.0 - s))


def _row_tile(rows):
    return 640 if rows % 640 == 0 else 128


def _mm(a, b, *, mode, out_dtype, tm, tn, tk, name, add=None, after=None, b_k0=0, b_n0=0, n_out=None):
    if mode == "nn":
        (m, k), k2 = a.shape, b.shape[0]
        n = b.shape[1] if n_out is None else n_out
        assert b_n0 % tn == 0 and b_n0 + n <= b.shape[1]
    elif mode == "nt":
        (m, k), n = a.shape, b.shape[0]
        k2 = k if b_k0 % tk == 0 and b_k0 + k <= b.shape[1] else None
    else:
        (k, m), (k2, n) = a.shape, b.shape
    assert (b_k0 == 0 or mode == "nt") and ((b_n0 == 0 and n_out is None) or mode == "nn")
    assert k == k2 and m % tm == 0 and n % tn == 0 and k % tk == 0, (name, a.shape, b.shape, tm, tn, tk)
    kb0, nb0 = b_k0 // tk, b_n0 // tn
    nk = k // tk
    dn = {"nn": NN, "nt": NT, "tn": TN}[mode]
    has_add = add is not None
    n_in = 2 + has_add + (after is not None)

    def body(*refs):
        a_ref, b_ref = refs[0], refs[1]
        add_ref = refs[2] if has_add else None
        o_ref = refs[n_in]
        p = _dot(a_ref[...], b_ref[...], dn)
        if nk == 1:
            if has_add:
                p = p + add_ref[...]
            o_ref[...] = p.astype(out_dtype)
        else:
            acc_ref = refs[n_in + 1]
            kk = pl.program_id(2)

            @pl.when(kk == 0)
            def _():
                acc_ref[...] = p

            @pl.when(kk > 0)
            def _():
                acc_ref[...] += p

            @pl.when(kk == nk - 1)
            def _():
                r = acc_ref[...]
                if has_add:
                    r = r + add_ref[...]
                o_ref[...] = r.astype(out_dtype)

    if mode == "tn":
        a_spec = pl.BlockSpec((tk, tm), lambda j, i, kk: (kk, i))
    else:
        a_spec = pl.BlockSpec((tm, tk), lambda j, i, kk: (i, kk))
    if mode == "nt":
        b_spec = pl.BlockSpec((tn, tk), lambda j, i, kk: (j, kk + kb0))
    else:
        b_spec = pl.BlockSpec((tk, tn), lambda j, i, kk: (kk, j + nb0))
    o_spec = pl.BlockSpec((tm, tn), lambda j, i, kk: (i, j))
    in_specs = [a_spec, b_spec] + ([o_spec] if has_add else [])
    args = (a, b) + ((add,) if has_add else ())
    if after is not None:
        in_specs.append(pl.BlockSpec(memory_space=pl.ANY))
        args += (after,)
    return pl.pallas_call(
        body, name=name, grid=(n // tn, m // tm, nk), in_specs=in_specs, out_specs=o_spec,
        out_shape=jax.ShapeDtypeStruct((m, n), out_dtype),
        scratch_shapes=[pltpu.VMEM((tm, tn), F32)] if nk > 1 else [],
        compiler_params=_params(("parallel", "parallel", "arbitrary")),
    )(*args)


def _pick(n, cands):
    for c in cands:
        if n % c == 0:
            return c
    return n


def _rms_fwd(h, w, name):
    rows, d = h.shape
    tm = _row_tile(rows)

    def body(h_ref, w_ref, u_ref):
        x = h_ref[...]
        r = lax.rsqrt(jnp.mean(x * x, axis=-1, keepdims=True) + EPS)
        u_ref[...] = (x * r * w_ref[...]).astype(ACT_DTYPE)

    return pl.pallas_call(
        body, name=name, grid=(rows // tm,),
        in_specs=[pl.BlockSpec((tm, d), lambda i: (i, 0)), pl.BlockSpec((1, d), lambda i: (0, 0))],
        out_specs=pl.BlockSpec((tm, d), lambda i: (i, 0)),
        out_shape=jax.ShapeDtypeStruct((rows, d), ACT_DTYPE),
        compiler_params=_params(("parallel",)),
    )(h, w)


def _rms_bwd(du, h, w, dres, name):
    rows, d = h.shape
    tm = _row_tile(rows)

    def body(du_ref, h_ref, w_ref, dres_ref, dh_ref, dw_ref):
        x = h_ref[...]
        dy = du_ref[...].astype(F32)
        r = lax.rsqrt(jnp.mean(x * x, axis=-1, keepdims=True) + EPS)
        xhat = x * r
        dxn = dy * w_ref[...]
        dx = r * (dxn - xhat * jnp.mean(dxn * xhat, axis=-1, keepdims=True))
        dh_ref[...] = dres_ref[...] + dx

        @pl.when(pl.program_id(0) == 0)
        def _():
            dw_ref[...] = jnp.zeros_like(dw_ref)

        dw_ref[...] += jnp.sum(dy * xhat, axis=0, keepdims=True)

    return pl.pallas_call(
        body, name=name, grid=(rows // tm,),
        in_specs=[pl.BlockSpec((tm, d), lambda i: (i, 0)), pl.BlockSpec((tm, d), lambda i: (i, 0)),
                  pl.BlockSpec((1, d), lambda i: (0, 0)), pl.BlockSpec((tm, d), lambda i: (i, 0))],
        out_specs=[pl.BlockSpec((tm, d), lambda i: (i, 0)), pl.BlockSpec((1, d), lambda i: (0, 0))],
        out_shape=[jax.ShapeDtypeStruct((rows, d), F32), jax.ShapeDtypeStruct((1, d), F32)],
        compiler_params=_params(("arbitrary",)),
    )(du, h, w, dres)


def _loss_head(h2, tgt, w):
    rows, d = h2.shape
    tm = _row_tile(rows)

    def body(h_ref, t_ref, w_ref, dh_ref, loss_ref, dw_ref):
        i = pl.program_id(0)
        x = h_ref[...]
        r = lax.rsqrt(jnp.mean(x * x, axis=-1, keepdims=True) + EPS)
        xhat = x * r
        wv = w_ref[...]
        row = i * tm + lax.broadcasted_iota(jnp.int32, (tm, 1), 0)
        live = row >= PAD_ROWS
        diff = jnp.where(live, xhat * wv - t_ref[...], 0.0)
        dy = diff * (1.0 / d)
        dxn = dy * wv
        dh_ref[...] = r * (dxn - xhat * jnp.mean(dxn * xhat, axis=-1, keepdims=True))

        @pl.when(i == 0)
        def _():
            loss_ref[...] = jnp.zeros_like(loss_ref)
            dw_ref[...] = jnp.zeros_like(dw_ref)

        loss_ref[...] += 0.5 * jnp.sum(jnp.mean(diff * diff, axis=-1, keepdims=True))
        dw_ref[...] += jnp.sum(dy * xhat, axis=0, keepdims=True)

    return pl.pallas_call(
        body, name="loss_head", grid=(rows // tm,),
        in_specs=[pl.BlockSpec((tm, d), lambda i: (i, 0)), pl.BlockSpec((tm, d), lambda i: (i, 0)),
                  pl.BlockSpec((1, d), lambda i: (0, 0))],
        out_specs=[pl.BlockSpec((tm, d), lambda i: (i, 0)), pl.BlockSpec((8, LANES), lambda i: (0, 0)),
                   pl.BlockSpec((1, d), lambda i: (0, 0))],
        out_shape=[jax.ShapeDtypeStruct((rows, d), F32), jax.ShapeDtypeStruct((8, LANES), F32),
                   jax.ShapeDtypeStruct((1, d), F32)],
        compiler_params=_params(("arbitrary",)),
    )(h2, tgt, w)


def _prev_halo_spec(tm, width, col):
    return pl.BlockSpec((HALO, width), lambda j, i: (jnp.maximum(i * (tm // HALO) - 1, 0), col(j)))


def _next_halo_spec(tm, rows, width, col):
    last = rows // HALO - 1
    return pl.BlockSpec((HALO, width), lambda j, i: (jnp.minimum((i + 1) * (tm // HALO), last), col(j)))


def _conv_taps(cat, w_ref, b_ref, kw):
    acc = b_ref[...] + w_ref[kw - 1:kw, :] * cat
    for s in range(1, kw):
        acc = acc + w_ref[kw - 1 - s:kw - s, :] * pltpu.roll(cat, s, 0)
    return acc


def _conv_back(dpre, w_ref, kw):
    n = dpre.shape[0]
    acc = w_ref[kw - 1:kw, :] * dpre
    for s in range(1, kw):
        acc = acc + w_ref[kw - 1 - s:kw - s, :] * pltpu.roll(dpre, n - s, 0)
    return acc


def _xbc_proj_conv(u1, w_in_p, col0, w, b):
    rows, d = u1.shape
    width = w.shape[1]
    tm, tc = _row_tile(rows), 512
    cb0 = col0 // tc
    assert col0 % tc == 0

    def body(u_ref, m_ref, w_ref, b_ref, x_ref, o_ref, carry):
        i = pl.program_id(1)

        @pl.when(i == 0)
        def _():
            carry[...] = jnp.zeros_like(carry)

        xb = _mxu(u_ref[...], m_ref[...], NN).astype(ACT_DTYPE)
        x_ref[...] = xb
        x = xb.astype(F32)
        cat = jnp.concatenate([carry[...], x], axis=0)
        carry[...] = x[tm - HALO:, :]
        pre = _conv_taps(cat, w_ref, b_ref, SSD_CONV)[HALO:]
        row = i * tm + lax.broadcasted_iota(jnp.int32, (tm, 1), 0)
        o_ref[...] = jnp.where(row >= FRONT, _silu(pre), 0.0).astype(ACT_DTYPE)

    main = pl.BlockSpec((tm, tc), lambda j, i: (i, j))
    par = lambda r: pl.BlockSpec((r, tc), lambda j, i: (0, j))
    act = jax.ShapeDtypeStruct((rows, width), ACT_DTYPE)
    return pl.pallas_call(
        body, name="xbc_proj_conv", grid=(width // tc, rows // tm),
        in_specs=[pl.BlockSpec((tm, d), lambda j, i: (i, 0)), pl.BlockSpec((d, tc), lambda j, i: (0, cb0 + j)),
                  par(SSD_CONV), par(1)],
        out_specs=[main, main], out_shape=[act, act], scratch_shapes=[pltpu.VMEM((HALO, tc), F32)],
        compiler_params=_params(("parallel", "arbitrary")),
    )(u1, w_in_p, w, b)


def _ssd_conv_bwd(xbc, dxs, dbm, dcm, w, b):
    rows, width = xbc.shape
    tm, tc = _row_tile(rows), 512
    kw = SSD_CONV
    nx = dxs.shape[1] // tc
    assert dbm.shape[1] == tc and dcm.shape[1] == tc and width == (nx + 2) * tc

    def body(x_ref, xp_ref, xn_ref, d0_ref, d0n_ref, d1_ref, d1n_ref, d2_ref, d2n_ref, w_ref, b_ref, dx_ref, dw_ref, db_ref):
        j, i = pl.program_id(0), pl.program_id(1)
        xp = jnp.where(i == 0, 0.0, xp_ref[...].astype(F32))
        cat = jnp.concatenate([xp, x_ref[...].astype(F32), xn_ref[...].astype(F32)], axis=0)
        sh = [cat] + [pltpu.roll(cat, s, 0) for s in range(1, kw)]
        pre = b_ref[...] + w_ref[kw - 1:kw, :] * sh[0]
        for s in range(1, kw):
            pre = pre + w_ref[kw - 1 - s:kw - s, :] * sh[s]
        pre = pre[HALO:]
        row = i * tm + lax.broadcasted_iota(jnp.int32, (tm + HALO, 1), 0)
        live = (row >= FRONT) & (row < rows)
        pick = lambda a, bb, c: jnp.where(j < nx, a[...], jnp.where(j == nx, bb[...], c[...])).astype(F32)
        dout = jnp.concatenate([pick(d0_ref, d1_ref, d2_ref), pick(d0n_ref, d1n_ref, d2n_ref)], axis=0)
        dpre = jnp.where(live, dout * _dsilu(pre), 0.0)
        dx_ref[...] = _conv_back(dpre, w_ref, kw)[:tm].astype(ACT_DTYPE)

        @pl.when(i == 0)
        def _():
            dw_ref[...] = jnp.zeros_like(dw_ref)
            db_ref[...] = jnp.zeros_like(db_ref)

        dmain = dpre[:tm]
        db_ref[...] += jnp.sum(dmain, axis=0, keepdims=True)
        for k in range(kw):
            dw_ref[k:k + 1, :] += jnp.sum(dmain * sh[kw - 1 - k][HALO:HALO + tm], axis=0, keepdims=True)

    main = pl.BlockSpec((tm, tc), lambda j, i: (i, j))
    par = lambda r: pl.BlockSpec((r, tc), lambda j, i: (0, j))
    col = lambda j: j
    xcol, zero = (lambda j: jnp.minimum(j, nx - 1)), (lambda j: 0)
    dspecs = lambda c: [pl.BlockSpec((tm, tc), lambda j, i: (i, c(j))), _next_halo_spec(tm, rows, tc, c)]
    return pl.pallas_call(
        body, name="ssd_conv_bwd", grid=(width // tc, rows // tm),
        in_specs=[main, _prev_halo_spec(tm, tc, col), _next_halo_spec(tm, rows, tc, col)]
        + dspecs(xcol) + dspecs(zero) + dspecs(zero) + [par(kw), par(1)],
        out_specs=[main, par(kw), par(1)],
        out_shape=[jax.ShapeDtypeStruct((rows, width), ACT_DTYPE), jax.ShapeDtypeStruct((kw, width), F32),
                   jax.ShapeDtypeStruct((1, width), F32)],
        compiler_params=_params(("parallel", "arbitrary")),
    )(xbc, xbc, xbc, dxs, dxs, dbm, dbm, dcm, dcm, w, b)


def _ffn_up_conv(u2, w_up, w, b):
    rows, d = u2.shape
    width = w_up.shape[1]
    dff = width // 2
    tm, tc = _row_tile(rows), _pick(dff, (256, 128))
    nb = dff // tc
    kw = FFN_CONV

    def body(u_ref, mg_ref, mv_ref, wg_ref, bg_ref, wv_ref, bv_ref, ug_ref, uv_ref, o_ref, cg, cv):
        i = pl.program_id(1)

        @pl.when(i == 0)
        def _():
            cg[...] = jnp.zeros_like(cg)
            cv[...] = jnp.zeros_like(cv)

        def pre(m_ref, up_ref, carry, w_ref, b_ref):
            upb = _mxu(u_ref[...], m_ref[...], NN).astype(ACT_DTYPE)
            up_ref[...] = upb
            x = upb.astype(F32)
            cat = jnp.concatenate([carry[...], x], axis=0)
            carry[...] = x[tm - HALO:, :]
            return _conv_taps(cat, w_ref, b_ref, kw)[HALO:]

        ag = pre(mg_ref, ug_ref, cg, wg_ref, bg_ref)
        av = pre(mv_ref, uv_ref, cv, wv_ref, bv_ref)
        o_ref[...] = (_silu(ag) * av).astype(ACT_DTYPE)

    gcol, vcol = (lambda j: j), (lambda j: j + nb)
    mat = lambda col: pl.BlockSpec((d, tc), lambda j, i: (0, col(j)))
    par = lambda r, col: pl.BlockSpec((r, tc), lambda j, i: (0, col(j)))
    out = pl.BlockSpec((tm, tc), lambda j, i: (i, j))
    act = jax.ShapeDtypeStruct((rows, dff), ACT_DTYPE)
    return pl.pallas_call(
        body, name="ffn_up_conv", grid=(nb, rows // tm),
        in_specs=[pl.BlockSpec((tm, d), lambda j, i: (i, 0)), mat(gcol), mat(vcol),
                  par(kw, gcol), par(1, gcol), par(kw, vcol), par(1, vcol)],
        out_specs=[out, out, out], out_shape=[act, act, act],
        scratch_shapes=[pltpu.VMEM((HALO, tc), F32), pltpu.VMEM((HALO, tc), F32)],
        compiler_params=_params(("parallel", "arbitrary")),
    )(u2, w_up, w_up, w, b, w, b)


def _ffn_conv_bwd(up_g, up_v, dact, w, b, u2):
    rows, dff = up_g.shape
    d = u2.shape[1]
    tm, tc = _row_tile(rows), _pick(dff, (256, 128))
    nb = dff // tc
    kw = FFN_CONV

    def body(g_ref, gp_ref, gn_ref, v_ref, vp_ref, vn_ref, d_ref, dn_ref, wg_ref, bg_ref, wv_ref, bv_ref, u_ref,
             dxg_ref, dxv_ref, dwg_ref, dbg_ref, dwv_ref, dbv_ref, gwg_ref, gwv_ref):
        i = pl.program_id(1)

        def shifted(x_ref, xp_ref, xn_ref):
            xp = jnp.where(i == 0, 0.0, xp_ref[...].astype(F32))
            cat = jnp.concatenate([xp, x_ref[...].astype(F32), xn_ref[...].astype(F32)], axis=0)
            return [cat] + [pltpu.roll(cat, s, 0) for s in range(1, kw)]

        def taps(sh, w_ref, b_ref):
            acc = b_ref[...] + w_ref[kw - 1:kw, :] * sh[0]
            for s in range(1, kw):
                acc = acc + w_ref[kw - 1 - s:kw - s, :] * sh[s]
            return acc[HALO:]

        sh_g, sh_v = shifted(g_ref, gp_ref, gn_ref), shifted(v_ref, vp_ref, vn_ref)
        ag, av = taps(sh_g, wg_ref, bg_ref), taps(sh_v, wv_ref, bv_ref)
        row = i * tm + lax.broadcasted_iota(jnp.int32, (tm + HALO, 1), 0)
        dout = jnp.concatenate([d_ref[...].astype(F32), dn_ref[...].astype(F32)], axis=0)
        dout = jnp.where(row < rows, dout, 0.0)
        s = jax.nn.sigmoid(ag)
        silu = ag * s
        dpre_v = dout * silu
        dpre_g = dout * av * (s + silu * (1.0 - s))

        @pl.when(i == 0)
        def _():
            for r in (dwg_ref, dbg_ref, dwv_ref, dbv_ref, gwg_ref, gwv_ref):
                r[...] = jnp.zeros_like(r)

        for dpre, sh, w_ref, dx_ref, dw_ref, db_ref, gw_ref in (
                (dpre_g, sh_g, wg_ref, dxg_ref, dwg_ref, dbg_ref, gwg_ref), (dpre_v, sh_v, wv_ref, dxv_ref, dwv_ref, dbv_ref, gwv_ref)):
            dx = _conv_back(dpre, w_ref, kw)[:tm].astype(ACT_DTYPE)
            dx_ref[...] = dx
            gw_ref[...] += _mxu(u_ref[...], dx, TN)
            dmain = dpre[:tm]
            db_ref[...] += jnp.sum(dmain, axis=0, keepdims=True)
            for k in range(kw):
                dw_ref[k:k + 1, :] += jnp.sum(dmain * sh[kw - 1 - k][HALO:HALO + tm], axis=0, keepdims=True)

    gcol, vcol = (lambda j: j), (lambda j: j + nb)
    main = lambda col: pl.BlockSpec((tm, tc), lambda j, i: (i, col(j)))
    par = lambda r, col: pl.BlockSpec((r, tc), lambda j, i: (0, col(j)))
    halos = lambda col: [_prev_halo_spec(tm, tc, col), _next_halo_spec(tm, rows, tc, col)]
    act_shape = jax.ShapeDtypeStruct((rows, dff), ACT_DTYPE)
    par_shapes = [jax.ShapeDtypeStruct((kw, dff), F32), jax.ShapeDtypeStruct((1, dff), F32)]
    gw_shape = jax.ShapeDtypeStruct((d, dff), F32)
    return pl.pallas_call(
        body, name="ffn_conv_bwd", grid=(nb, rows // tm),
        in_specs=[main(gcol)] + halos(gcol) + [main(gcol)] + halos(gcol) + [main(gcol), _next_halo_spec(tm, rows, tc, gcol),
                  par(kw, gcol), par(1, gcol), par(kw, vcol), par(1, vcol), pl.BlockSpec((tm, d), lambda j, i: (i, 0))],
        out_specs=[main(gcol), main(gcol), par(kw, gcol), par(1, gcol), par(kw, gcol), par(1, gcol), par(d, gcol), par(d, gcol)],
        out_shape=[act_shape, act_shape] + par_shapes + par_shapes + [gw_shape, gw_shape],
        compiler_params=_params(("parallel", "arbitrary")),
    )(up_g, up_g, up_g, up_v, up_v, up_v, dact, dact, w, b, w, b, u2)


def _ssd_scalars(dtr, dt_bias, a_log, live):
    q = CHUNK
    pre = dtr + dt_bias
    dt = jnp.where(live, jax.nn.softplus(pre), 0.0)
    a_neg = -jnp.exp(a_log)
    li = lax.broadcasted_iota(jnp.int32, (q, q), 0)
    si = lax.broadcasted_iota(jnp.int32, (q, q), 1)
    causal = li >= si
    tri = jnp.where(causal, 1.0, 0.0).astype(F32)
    a_cs = jnp.dot(tri, dt * a_neg, precision=lax.Precision.HIGHEST, preferred_element_type=F32)
    return pre, dt, a_neg, a_cs, causal, tri


def _head_select():
    r = lax.broadcasted_iota(jnp.int32, (LANES, SSD_GW), 0)
    c = lax.broadcasted_iota(jnp.int32, (LANES, SSD_GW), 1)
    return jnp.where(c // SSD_HEAD_DIM == r, 1.0, 0.0).astype(MXU_DTYPE)


def _split(t, parts):
    out, rem = [], t
    for _ in range(parts):
        p = rem.astype(MXU_DTYPE)
        out.append(p)
        rem = rem - p.astype(F32)
    return out


def _head_cols(t, sel):
    return sum(_mxu(p, sel, NN) for p in _split(t, 2))


def _head_sums(t, sel):
    return sum(_mxu(p, sel, NT) for p in _split(t, 3))


def _half_masks():
    lane = lax.broadcasted_iota(jnp.int32, (CHUNK, LANES), 1)
    return lane < SSD_HEAD_DIM, lane >= SSD_HEAD_DIM


def _ssd_scan(xs, bm, cm, dtr, prev, dt_bias, a_log, d_skip, live):
    q = CHUNK
    sel = _head_select()
    _, dt, _, a_cs, causal, _ = _ssd_scalars(dtr, dt_bias, a_log, live)
    a_cs_t = a_cs.T
    a_end = a_cs[q - 1:q, :]
    e_x = _head_cols(jnp.exp(a_cs), sel)
    xdt = xs * _head_cols(dt, sel)
    cb = _dot(cm, bm, NT)
    y = _dot(cm, prev) * e_x + _head_cols(jnp.broadcast_to(d_skip, (8, LANES)), sel)[0:1] * xs
    new = prev * e_x[q - 1:q, :] + _dot(bm, xdt * _head_cols(jnp.exp(a_end - a_cs), sel), TN)
    masks = _half_masks()
    ys = []
    for pp in range(SSD_HPG // 2):
        xpair = xdt[:, pp * LANES:(pp + 1) * LANES]
        acc = jnp.zeros((q, LANES), F32)
        for half in range(2):
            hh = 2 * pp + half
            decay = jnp.exp(jnp.where(causal, a_cs[:, hh:hh + 1] - a_cs_t[hh:hh + 1, :], -jnp.inf))
            acc = acc + _dot(cb * decay, jnp.where(masks[half], xpair, 0.0))
        ys.append(acc)
    return y + jnp.concatenate(ys, axis=1), new


def _ssd_gate(y, z, nw):
    yz = y * _silu(z)
    return yz * lax.rsqrt(jnp.mean(yz * yz, axis=-1, keepdims=True) + EPS) * nw


def _ssd_scan_bwd(xs, bm, cm, dtr, prev, dt_bias, a_log, d_skip, live, dy, dnew):
    q = CHUNK
    sel = _head_select()
    pre, dt, a_neg, a_cs, causal, tri = _ssd_scalars(dtr, dt_bias, a_log, live)
    a_cs_t = a_cs.T
    a_end = a_cs[q - 1:q, :]
    dt_x, e_x, w_x = _head_cols(dt, sel), _head_cols(jnp.exp(a_cs), sel), _head_cols(jnp.exp(a_end - a_cs), sel)
    g_x, d_x = e_x[q - 1:q, :], _head_cols(jnp.broadcast_to(d_skip, (8, LANES)), sel)[0:1]
    xdt = xs * dt_x
    u = xdt * w_x
    cb = _mxu(cm, bm, NT)
    cs = _mxu(cm, prev, NN)
    dye = dy * e_x
    dcm = _mxu(dye, prev, NT)
    dprev = _mxu(cm, dye, TN) + dnew * g_x
    dacs_x = dye * cs
    dbm = _mxu(u, dnew, NT)
    du = _mxu(bm, dnew, NN)
    dw_x = du * u
    dacs_x = dacs_x - dw_x
    dend_x = jnp.sum(dw_x + dnew * prev * g_x, axis=0, keepdims=True)
    dxdt = du * w_x
    lane = lax.broadcasted_iota(jnp.int32, (q, LANES), 1)
    sub = lax.broadcasted_iota(jnp.int32, (q, LANES), 0)
    dcb = jnp.zeros((q, q), F32)
    dacs = jnp.zeros((q, LANES), F32)
    dacs_t = jnp.zeros((q, LANES), F32)
    masks = _half_masks()
    dxdt_p = []
    for pp in range(SSD_HPG // 2):
        ps = slice(pp * LANES, (pp + 1) * LANES)
        acc = jnp.zeros((q, LANES), F32)
        for half in range(2):
            hh = 2 * pp + half
            decay = jnp.exp(jnp.where(causal, a_cs[:, hh:hh + 1] - a_cs_t[hh:hh + 1, :], -jnp.inf))
            m = cb * decay
            dyh = jnp.where(masks[half], dy[:, ps], 0.0)
            dm = _mxu(dyh, xdt[:, ps], NT)
            acc = acc + _mxu(m, dyh, TN)
            dcb = dcb + dm * decay
            p = dm * m
            dacs = jnp.where(lane == hh, jnp.sum(p, axis=1, keepdims=True), dacs)
            dacs_t = jnp.where(sub == hh, jnp.sum(p, axis=0, keepdims=True), dacs_t)
        dxdt_p.append(acc)
    dcm = dcm + _mxu(dcb, bm, NN)
    dbm = dbm + _mxu(dcb, cm, TN)
    dxdt = dxdt + jnp.concatenate(dxdt_p, axis=1)
    dxs = dy * d_x + dxdt * dt_x
    rows_x = jnp.concatenate([dend_x, jnp.sum(dy * xs, axis=0, keepdims=True), jnp.zeros((6, SSD_GW), F32)], axis=0)
    rows = _head_sums(rows_x, sel)
    dacs = dacs - dacs_t.T + _head_sums(dacs_x, sel)
    dacs = dacs + jnp.where(sub == q - 1, rows[0:1], 0.0)
    tri_t = jnp.where(causal, 0.0, 1.0).astype(F32) + jnp.where(lane == sub, 1.0, 0.0)
    da = jnp.dot(tri_t, dacs, precision=lax.Precision.HIGHEST, preferred_element_type=F32)
    ddt = _head_sums(dxdt * xs, sel) + da * a_neg
    dalog = jnp.sum(da * dt, axis=0, keepdims=True) * a_neg
    ddtr = jnp.where(live, ddt * jax.nn.sigmoid(pre), 0.0)
    dbias = jnp.sum(ddtr, axis=0, keepdims=True)
    return dxs, dbm, dcm, ddtr, dprev, dbias, dalog, rows[1:2]


def _chunks_per_step(nc):
    return 5 if nc % 5 == 0 else 1


def _ssd_specs(rev, nc):
    per = _chunks_per_step(nc)
    steps = nc // per
    sidx = (lambda s: steps - 1 - s) if rev else (lambda s: s)
    nb_b = SSD_D_INNER // SSD_STATE
    row = lambda width, col=lambda g: g: pl.BlockSpec((per * CHUNK, width), lambda g, s: (sidx(s), col(g)))
    par = lambda width: pl.BlockSpec((1, width), lambda g, s: (0, g))
    state = lambda: pl.BlockSpec((per, 1, SSD_STATE, SSD_GW), lambda g, s: (sidx(s), g, 0, 0))
    xbc = [row(SSD_GW), row(SSD_STATE, lambda g: nb_b + g), row(SSD_STATE, lambda g: nb_b + SSD_GROUPS + g)]
    return per, steps, sidx, row, par, state, xbc


def _ssd_fwd(xbc_c, dtr, z, dt_bias, a_log, d_skip, nw):
    rows = z.shape[0]
    nc = rows // CHUNK
    per, steps, _, row, par, state, xbc = _ssd_specs(False, nc)

    def body(xs_ref, b_ref, c_ref, dt_ref, z_ref, bias_ref, al_ref, dk_ref, nw_ref, o_ref, y_ref, st_ref, carry):
        s = pl.program_id(1)

        @pl.when(s == 0)
        def _():
            carry[...] = jnp.zeros_like(carry)

        for j in range(per):
            rs = pl.ds(j * CHUNK, CHUNK)
            live = (s * per + j) * CHUNK + lax.broadcasted_iota(jnp.int32, (CHUNK, 1), 0) >= FRONT
            prev = carry[...]
            st_ref[j, 0] = prev
            y, new = _ssd_scan(xs_ref[rs, :].astype(F32), b_ref[rs, :].astype(F32), c_ref[rs, :].astype(F32), dt_ref[rs, :],
                               prev, bias_ref[...], al_ref[...], dk_ref[...], live)
            y_ref[rs, :] = y.astype(ACT_DTYPE)
            o_ref[rs, :] = _ssd_gate(y, z_ref[rs, :].astype(F32), nw_ref[...]).astype(ACT_DTYPE)
            carry[...] = new

    act = jax.ShapeDtypeStruct((rows, SSD_D_INNER), ACT_DTYPE)
    return pl.pallas_call(
        body, name="ssd_fwd", grid=(SSD_GROUPS, steps),
        in_specs=xbc + [row(LANES), row(SSD_GW), par(LANES), par(LANES), par(LANES), par(SSD_GW)],
        out_specs=[row(SSD_GW), row(SSD_GW), state()],
        out_shape=[act, act, jax.ShapeDtypeStruct((nc, SSD_GROUPS, SSD_STATE, SSD_GW), F32)],
        scratch_shapes=[pltpu.VMEM((SSD_STATE, SSD_GW), F32)],
        compiler_params=_params(("parallel", "arbitrary")),
    )(xbc_c, xbc_c, xbc_c, dtr, z, dt_bias, a_log, d_skip, nw)


def _ssd_gate_bwd(y, z, dout, nw):
    rows = y.shape[0]
    tm = _row_tile(rows)

    def body(y_ref, z_ref, do_ref, nw_ref, dy_ref, dz_ref, dnw_ref):
        yv, zv, dov = y_ref[...].astype(F32), z_ref[...].astype(F32), do_ref[...].astype(F32)
        s = jax.nn.sigmoid(zv)
        silu = zv * s
        yz = yv * silu
        r = lax.rsqrt(jnp.mean(yz * yz, axis=-1, keepdims=True) + EPS)
        yhat = yz * r
        dn = dov * nw_ref[...]
        dyz = r * (dn - yhat * jnp.mean(dn * yhat, axis=-1, keepdims=True))
        dy_ref[...] = (dyz * silu).astype(ACT_DTYPE)
        dz_ref[...] = (dyz * yv * (s + silu * (1.0 - s))).astype(ACT_DTYPE)

        @pl.when(pl.program_id(1) == 0)
        def _():
            dnw_ref[...] = jnp.zeros_like(dnw_ref)

        dnw_ref[...] += jnp.sum(dov * yhat, axis=0, keepdims=True)

    spec = pl.BlockSpec((tm, SSD_GW), lambda g, i: (i, g))
    par = pl.BlockSpec((1, SSD_GW), lambda g, i: (0, g))
    act = jax.ShapeDtypeStruct((rows, SSD_D_INNER), ACT_DTYPE)
    return pl.pallas_call(
        body, name="ssd_gate_bwd", grid=(SSD_GROUPS, rows // tm), in_specs=[spec, spec, spec, par],
        out_specs=[spec, spec, par], out_shape=[act, act, jax.ShapeDtypeStruct((1, SSD_D_INNER), F32)],
        compiler_params=_params(("parallel", "arbitrary")),
    )(y, z, dout, nw)


def _ssd_bwd(xbc_c, dtr, dt_bias, a_log, d_skip, states, dy):
    rows = dy.shape[0]
    nc = rows // CHUNK
    per, steps, sidx, row, par, state, xbc = _ssd_specs(True, nc)

    def body(xs_ref, b_ref, c_ref, dt_ref, bias_ref, al_ref, dk_ref, st_ref, dy_ref,
             dxs_ref, db_ref, dc_ref, ddt_ref, dbias_ref, dal_ref, ddk_ref, carry):
        s = pl.program_id(1)

        @pl.when(s == 0)
        def _():
            carry[...] = jnp.zeros_like(carry)
            for r in (dbias_ref, dal_ref, ddk_ref):
                r[...] = jnp.zeros_like(r)

        for j in reversed(range(per)):
            rs = pl.ds(j * CHUNK, CHUNK)
            live = (sidx(s) * per + j) * CHUNK + lax.broadcasted_iota(jnp.int32, (CHUNK, 1), 0) >= FRONT
            dxs, dbm, dcm, ddt, dprev, dbias, dal, ddk = _ssd_scan_bwd(
                xs_ref[rs, :].astype(F32), b_ref[rs, :].astype(F32), c_ref[rs, :].astype(F32), dt_ref[rs, :], st_ref[j, 0],
                bias_ref[...], al_ref[...], dk_ref[...], live, dy_ref[rs, :].astype(F32), carry[...])
            dxs_ref[rs, :] = dxs.astype(ACT_DTYPE)
            db_ref[rs, :] = dbm.astype(ACT_DTYPE)
            dc_ref[rs, :] = dcm.astype(ACT_DTYPE)
            ddt_ref[rs, :] = ddt
            carry[...] = dprev
            dbias_ref[...] += dbias
            dal_ref[...] += dal
            ddk_ref[...] += ddk

    bc = jax.ShapeDtypeStruct((rows, SSD_GROUPS * SSD_STATE), ACT_DTYPE)
    head = jax.ShapeDtypeStruct((1, SSD_GROUPS * LANES), F32)
    return pl.pallas_call(
        body, name="ssd_bwd", grid=(SSD_GROUPS, steps),
        in_specs=xbc + [row(LANES), par(LANES), par(LANES), par(LANES), state(), row(SSD_GW)],
        out_specs=[row(SSD_GW), row(SSD_STATE), row(SSD_STATE), row(LANES), par(LANES), par(LANES), par(LANES)],
        out_shape=[jax.ShapeDtypeStruct((rows, SSD_D_INNER), ACT_DTYPE), bc, bc,
                   jax.ShapeDtypeStruct((rows, SSD_GROUPS * LANES), F32), head, head, head],
        scratch_shapes=[pltpu.VMEM((SSD_STATE, SSD_GW), F32)],
        compiler_params=_params(("parallel", "arbitrary")),
    )(xbc_c, xbc_c, xbc_c, dtr, dt_bias, a_log, d_skip, states, dy)


def _rotary_tables(rows):
    pos = np.arange(rows, dtype=np.float32) - np.float32(FRONT)
    inv_freq = np.float32(ROPE_BASE) ** (-np.linspace(0.0, 1.0, RET_QK // 2, dtype=np.float32))
    ang = (pos[:, None] * inv_freq[None, :]).astype(np.float32).astype(np.float64)
    lgam = np.log(1.0 - 2.0 ** (-5.0 - np.arange(RET_HEADS, dtype=np.float64))).astype(np.float32)
    lgam = np.broadcast_to(lgam[:, None, None], (RET_HEADS, 8, LANES))
    return jnp.asarray(np.cos(ang).astype(np.float32)), jnp.asarray(np.sin(ang).astype(np.float32)), jnp.asarray(lgam)


def _rotary(t, cos, sin):
    half = t.shape[-1] // 2
    t1, t2 = t[:, :half], t[:, half:]
    return jnp.concatenate([t1 * cos - t2 * sin, t2 * cos + t1 * sin], axis=1)


def _ret_chunk(qh, kh, vh, gh, prev, cos, sin, lg):
    q = CHUNK
    qr = _rotary(qh, cos, sin)
    kr = _rotary(kh, cos, sin) * (RET_QK ** -0.5)
    li = lax.broadcasted_iota(jnp.int32, (q, q), 0)
    si = lax.broadcasted_iota(jnp.int32, (q, q), 1)
    dist = (li - si).astype(F32)
    decay = jnp.exp(jnp.where(li >= si, dist * lg, -jnp.inf))
    idx = lax.broadcasted_iota(jnp.int32, (q, 1), 0).astype(F32)
    scores = _dot(qr, kr, NT) * decay
    out = _dot(scores, vh)
    kv = _dot(kr * jnp.exp((q - 1.0 - idx) * lg), vh, TN)
    out = out + _dot(qr, prev) * jnp.exp((idx + 1.0) * lg)
    new = prev * jnp.exp(q * lg) + kv
    out = out * lax.rsqrt(jnp.mean(out * out, axis=-1, keepdims=True) + EPS)
    return _silu(gh) * out, new


def _ret_specs(rev, nc):
    per = _chunks_per_step(nc)
    steps = nc // per
    sidx = (lambda s: steps - 1 - s) if rev else (lambda s: s)
    row = lambda width: pl.BlockSpec((per * CHUNK, width), lambda h, s: (sidx(s), h))
    tab = lambda: pl.BlockSpec((per * CHUNK, RET_QK // 2), lambda h, s: (sidx(s), 0))
    lgs = lambda: pl.BlockSpec((1, 8, LANES), lambda h, s: (h, 0, 0))
    state = lambda: pl.BlockSpec((per, 1, RET_QK, RET_V), lambda h, s: (sidx(s), h, 0, 0))
    ins = [row(RET_QK), row(RET_QK), row(RET_V), row(RET_V), tab(), tab(), lgs()]
    return per, steps, row, state, ins


def _ret_fwd(q, k, v, g, cos, sin, lgam):
    rows = q.shape[0]
    nc = rows // CHUNK
    per, steps, row, state, ins = _ret_specs(False, nc)

    def body(q_ref, k_ref, v_ref, g_ref, cos_ref, sin_ref, lg_ref, y_ref, st_ref, carry):
        @pl.when(pl.program_id(1) == 0)
        def _():
            carry[...] = jnp.zeros_like(carry)

        for j in range(per):
            rs = pl.ds(j * CHUNK, CHUNK)
            prev = carry[...]
            st_ref[j, 0] = prev.astype(ACT_DTYPE)
            out, new = _ret_chunk(q_ref[rs, :].astype(F32), k_ref[rs, :].astype(F32), v_ref[rs, :].astype(F32),
                                  g_ref[rs, :].astype(F32), prev, cos_ref[rs, :], sin_ref[rs, :], lg_ref[0, 0:1, 0:1])
            y_ref[rs, :] = out.astype(ACT_DTYPE)
            carry[...] = new

    return pl.pallas_call(
        body, name="ret_fwd", grid=(RET_HEADS, steps), in_specs=ins, out_specs=[row(RET_V), state()],
        out_shape=[jax.ShapeDtypeStruct((rows, RET_HEADS * RET_V), ACT_DTYPE),
                   jax.ShapeDtypeStruct((nc, RET_HEADS, RET_QK, RET_V), ACT_DTYPE)],
        scratch_shapes=[pltpu.VMEM((RET_QK, RET_V), F32)],
        compiler_params=_params(("parallel", "arbitrary")),
    )(q, k, v, g, cos, sin, lgam)


def _ret_bwd(q, k, v, g, cos, sin, lgam, states, dy):
    rows = q.shape[0]
    nc = rows // CHUNK
    per, steps, row, state, ins = _ret_specs(True, nc)

    def body(q_ref, k_ref, v_ref, g_ref, cos_ref, sin_ref, lg_ref, st_ref, dy_ref, dq_ref, dk_ref, dv_ref, dg_ref, carry):
        @pl.when(pl.program_id(1) == 0)
        def _():
            carry[...] = jnp.zeros_like(carry)

        for j in reversed(range(per)):
            rs = pl.ds(j * CHUNK, CHUNK)
            fn = functools.partial(_ret_chunk, cos=cos_ref[rs, :], sin=sin_ref[rs, :], lg=lg_ref[0, 0:1, 0:1])
            _, vjp = jax.vjp(fn, q_ref[rs, :].astype(F32), k_ref[rs, :].astype(F32), v_ref[rs, :].astype(F32),
                             g_ref[rs, :].astype(F32), st_ref[j, 0].astype(F32))
            dq, dk, dv, dg, dprev = vjp((dy_ref[rs, :].astype(F32), carry[...]))
            dq_ref[rs, :] = dq.astype(ACT_DTYPE)
            dk_ref[rs, :] = dk.astype(ACT_DTYPE)
            dv_ref[rs, :] = dv.astype(ACT_DTYPE)
            dg_ref[rs, :] = dg.astype(ACT_DTYPE)
            carry[...] = dprev

    shp = lambda width: jax.ShapeDtypeStruct((rows, RET_HEADS * width), ACT_DTYPE)
    return pl.pallas_call(
        body, name="ret_bwd", grid=(RET_HEADS, steps), in_specs=ins + [state(), row(RET_V)],
        out_specs=[row(RET_QK), row(RET_QK), row(RET_V), row(RET_V)],
        out_shape=[shp(RET_QK), shp(RET_QK), shp(RET_V), shp(RET_V)],
        scratch_shapes=[pltpu.VMEM((RET_QK, RET_V), F32)],
        compiler_params=_params(("parallel", "arbitrary")),
    )(q, k, v, g, cos, sin, lgam, states, dy)


def _merge_fwd(bs, br, gs, gr):
    rows, d = bs.shape
    tm = _row_tile(rows)

    def body(bs_ref, br_ref, gs_ref, gr_ref, o_ref):
        o_ref[...] = (jax.nn.sigmoid(gs_ref[...].astype(F32)) * bs_ref[...].astype(F32)
                      + jax.nn.sigmoid(gr_ref[...].astype(F32)) * br_ref[...].astype(F32)).astype(ACT_DTYPE)

    spec = pl.BlockSpec((tm, d), lambda i: (i, 0))
    return pl.pallas_call(
        body, name="merge_fwd", grid=(rows // tm,), in_specs=[spec] * 4, out_specs=spec,
        out_shape=jax.ShapeDtypeStruct((rows, d), ACT_DTYPE), compiler_params=_params(("parallel",)),
    )(bs, br, gs, gr)


def _merge_bwd(dm, bs, br, gs, gr):
    rows, d = bs.shape
    tm = _row_tile(rows)

    def body(dm_ref, bs_ref, br_ref, gs_ref, gr_ref, dbs_ref, dbr_ref, dgs_ref, dgr_ref):
        dmv = dm_ref[...].astype(F32)
        for b_ref, g_ref, db_ref, dg_ref in ((bs_ref, gs_ref, dbs_ref, dgs_ref), (br_ref, gr_ref, dbr_ref, dgr_ref)):
            s = jax.nn.sigmoid(g_ref[...].astype(F32))
            db_ref[...] = (dmv * s).astype(ACT_DTYPE)
            dg_ref[...] = (dmv * b_ref[...].astype(F32) * s * (1.0 - s)).astype(ACT_DTYPE)

    spec = pl.BlockSpec((tm, d), lambda i: (i, 0))
    shp = jax.ShapeDtypeStruct((rows, d), ACT_DTYPE)
    return pl.pallas_call(
        body, name="merge_bwd", grid=(rows // tm,), in_specs=[spec] * 5, out_specs=[spec] * 4,
        out_shape=[shp] * 4, compiler_params=_params(("parallel",)),
    )(dm, bs, br, gs, gr)


def _place():
    x, y, c = lax.axis_index("x"), lax.axis_index("y"), lax.axis_index("c")
    return x, y, c


def _slot(p):
    return 4 * p[0] + 2 * p[1] + p[2]


def _allgather(arrs, name):
    n = len(arrs)
    any_spec = pl.BlockSpec(memory_space=pl.ANY)

    def body(*refs):
        ins, outs = refs[:n], refs[n:2 * n]
        send_sems, recv_sems, local_sems = refs[2 * n:]
        x, y, c = _place()
        me, sibling = (x, y, c), (x, y, 1 - c)
        chips = [(1 - x, y), (x, 1 - y), (1 - x, 1 - y)]

        def copy(a, k, block, to, src=None):
            dst = outs[a].at[_slot(block)]
            return pltpu.make_async_remote_copy(
                src_ref=dst if src is None else src, dst_ref=dst, send_sem=send_sems.at[a * 7 + k],
                recv_sem=recv_sems.at[a * 7 + k], device_id=to, device_id_type=MESH)

        mine, first, passed = [], [], []
        for a in range(n):
            cp = pltpu.make_async_copy(ins[a], outs[a].at[_slot(me)], local_sems.at[a])
            cp.start()
            mine.append(cp)
            first.append(copy(a, 0, me, sibling, src=ins[a]))
            first += [copy(a, 1 + j, me, (*chip, c), src=ins[a]) for j, chip in enumerate(chips)]
        for cp in first:
            cp.start()
        for j, chip in enumerate(chips):
            for a in range(n):
                copy(a, 1 + j, (*chip, c), me).wait_recv()
                cp = copy(a, 4 + j, (*chip, c), sibling)
                cp.start()
                passed.append(cp)
        for a in range(n):
            copy(a, 0, sibling, me).wait_recv()
            for j, chip in enumerate(chips):
                copy(a, 4 + j, (*chip, 1 - c), me).wait_recv()
        for cp in first + passed:
            cp.wait_send()
        for cp in mine:
            cp.wait()

    return pl.pallas_call(
        body, name=name, in_specs=[any_spec] * n, out_specs=[any_spec] * n,
        out_shape=[jax.ShapeDtypeStruct((N_DEV,) + a.shape, a.dtype) for a in arrs],
        scratch_shapes=[pltpu.SemaphoreType.DMA((7 * n,)), pltpu.SemaphoreType.DMA((7 * n,)), pltpu.SemaphoreType.DMA((n,))],
    )(*arrs)


def _peers():
    x, y, c = _place()
    return (x, y, c), [(x ^ dx, y ^ dy, c ^ dc) for dx in (0, 1) for dy in (0, 1) for dc in (0, 1)][1:]


def _exchange_copies(srcs, lands, send_sems, recv_sems, scatter, sender):
    me, peers = _peers()
    out = []
    for a, (src, land) in enumerate(zip(srcs, lands, strict=True)):
        for k, peer in enumerate(peers):
            src_ref = src.at[_slot(peer)] if scatter else src
            out.append(pltpu.make_async_remote_copy(
                src_ref=src_ref, dst_ref=land.at[_slot(me if sender else peer)], send_sem=send_sems.at[a * 7 + k],
                recv_sem=recv_sems.at[a * 7 + k], device_id=peer, device_id_type=MESH))
    return out


_HBM = pl.BlockSpec(memory_space=pltpu.HBM)
_SEM = pl.BlockSpec(memory_space=pltpu.SEMAPHORE)
_EFFECT = pltpu.SideEffectType.DATAFLOW_SIDE_EFFECTING


def _exchange_start(srcs, scatter, name, after=None):
    n = len(srcs)
    land_shapes = [s.shape if scatter else (N_DEV,) + s.shape for s in srcs]
    n_in = 2 * n + (after is not None)

    def body(*refs):
        for cp in _exchange_copies(refs[:n], refs[n:2 * n], refs[n_in], refs[n_in + 1], scatter, True):
            cp.start()
        refs[-1][...] = jnp.zeros_like(refs[-1])

    args = [pltpu.with_memory_space_constraint(s, pltpu.HBM) for s in srcs]
    args += [pltpu.with_memory_space_constraint(lax.empty(shp, s.dtype), pltpu.HBM) for s, shp in zip(srcs, land_shapes)]
    thru_shapes = tuple(pltpu.HBM(a.shape, a.dtype) for a in args)
    extra = [] if after is None else [after]
    outs = pl.pallas_call(
        body, name=name,
        out_shape=(pltpu.SemaphoreType.DMA((7 * n,)), pltpu.SemaphoreType.DMA((7 * n,))) + thru_shapes
        + (jax.ShapeDtypeStruct((8, LANES), F32),),
        in_specs=[_HBM] * (2 * n) + [pl.BlockSpec(memory_space=pl.ANY)] * len(extra),
        out_specs=(_SEM, _SEM) + (_HBM,) * (2 * n) + (pl.BlockSpec(memory_space=pltpu.VMEM),),
        input_output_aliases={i: 2 + i for i in range(2 * n)},
        compiler_params=pltpu.CompilerParams(has_side_effects=_EFFECT),
    )(*args, *extra)
    return outs[:-1], outs[-1]


def _exchange_wait(handle, scatter, after, name):
    n = (len(handle) - 2) // 2
    thru = handle[2:]

    def body(*refs):
        for cp in _exchange_copies(refs[:n], refs[n:2 * n], refs[2 * n], refs[2 * n + 1], scatter, False):
            cp.wait_send()
            cp.wait_recv()

    outs = pl.pallas_call(
        body, name=name, out_shape=tuple(pltpu.HBM(t.shape, t.dtype) for t in thru),
        in_specs=[_HBM] * (2 * n) + [_SEM, _SEM, pl.BlockSpec(memory_space=pl.ANY)], out_specs=(_HBM,) * (2 * n),
        input_output_aliases={i: i for i in range(2 * n)},
        compiler_params=pltpu.CompilerParams(has_side_effects=_EFFECT),
    )(*thru, handle[0], handle[1], after)
    return list(outs[:n]), list(outs[n:])


def _allreduce_small(pack):
    rows, lanes = pack.shape

    def body(x_ref, o_ref, buf, send_sems, recv_sems):
        x, y, c = _place()
        me, sibling = (x, y, c), (x, y, 1 - c)
        chips = [(1 - x, y), (x, 1 - y), (1 - x, 1 - y)]

        def copy(k, block, to, src=None):
            dst = buf.at[_slot(block)]
            return pltpu.make_async_remote_copy(
                src_ref=dst if src is None else src, dst_ref=dst, send_sem=send_sems.at[k], recv_sem=recv_sems.at[k],
                device_id=to, device_id_type=MESH)

        buf[_slot(me)] = x_ref[...]
        first = [copy(0, me, sibling, src=x_ref)]
        first += [copy(1 + j, me, (*chip, c), src=x_ref) for j, chip in enumerate(chips)]
        for cp in first:
            cp.start()
        passed = [copy(4 + j, (*chip, c), sibling) for j, chip in enumerate(chips)]
        for j, chip in enumerate(chips):
            copy(1 + j, (*chip, c), me).wait_recv()
            passed[j].start()
        copy(0, sibling, me).wait_recv()
        for j, chip in enumerate(chips):
            copy(4 + j, (*chip, 1 - c), me).wait_recv()
        for cp in first + passed:
            cp.wait_send()
        acc = buf[0]
        for i in range(1, N_DEV):
            acc = acc + buf[i]
        o_ref[...] = acc

    vmem = pl.BlockSpec(memory_space=pltpu.VMEM)
    return pl.pallas_call(
        body, name="allreduce_small", in_specs=[vmem], out_specs=vmem,
        out_shape=jax.ShapeDtypeStruct((rows, lanes), F32),
        scratch_shapes=[pltpu.VMEM((N_DEV, rows, lanes), F32), pltpu.SemaphoreType.DMA((7,)), pltpu.SemaphoreType.DMA((7,))],
        compiler_params=pltpu.CompilerParams(vmem_limit_bytes=VMEM_LIMIT),
    )(pack)


def _adamw(w, g, m, v):
    m = ADAM_B1 * m + (1.0 - ADAM_B1) * g
    v = ADAM_B2 * v + (1.0 - ADAM_B2) * jnp.square(g)
    m_hat = m / (1.0 - ADAM_B1 ** ADAM_STEP)
    v_hat = v / (1.0 - ADAM_B2 ** ADAM_STEP)
    delta = -ADAM_LR * (m_hat / (jnp.sqrt(v_hat) + ADAM_EPS) + ADAM_WD * w)
    return delta, m, v


def _adam_shard(own, parts, w, m, v, name):
    r, c = w.shape
    tr = _pick(r, (128, 64, 32, 16, 8))

    def body(own_ref, p_ref, w_ref, m_ref, v_ref, g_ref, d_ref, nm_ref, nv_ref):
        _, peers = _peers()
        g = own_ref[...].astype(F32)
        for peer in peers:
            g = g + p_ref[_slot(peer)].astype(F32)
        g_ref[...] = g
        d_ref[...], nm_ref[...], nv_ref[...] = _adamw(w_ref[...], g, m_ref[...], v_ref[...])

    spec = pl.BlockSpec((tr, c), lambda i: (i, 0))
    shp = jax.ShapeDtypeStruct((r, c), F32)
    return pl.pallas_call(
        body, name=name, grid=(r // tr,),
        in_specs=[spec, pl.BlockSpec((N_DEV, tr, c), lambda i: (0, i, 0)), spec, spec, spec], out_specs=[spec] * 4,
        out_shape=[shp] * 4, compiler_params=_params(("parallel",)),
    )(own, parts, w, m, v)


def _adam_small(w, g, m, v):
    r, c = w.shape

    def body(w_ref, g_ref, m_ref, v_ref, d_ref, nm_ref, nv_ref):
        d_ref[...], nm_ref[...], nv_ref[...] = _adamw(w_ref[...], g_ref[...], m_ref[...], v_ref[...])

    shp = jax.ShapeDtypeStruct((r, c), F32)
    return pl.pallas_call(body, name="adam_small", out_shape=[shp] * 3)(w, g, m, v)


def _column_plan(pieces, shard_w):
    plan = []
    for c0, width, d0 in pieces:
        c = c0
        while c < c0 + width:
            s, a = divmod(c, shard_w)
            w = min(c0 + width - c, shard_w - a)
            plan.append((s, a, w, d0 + c - c0))
            c += w
    return plan


def _cols_from_shards(g, plan, out_w, zero, name):
    _, r, sw = g.shape
    tr = _pick(r, (128,))

    def body(x_ref, o_ref):
        for d0, w in zero:
            o_ref[:, d0:d0 + w] = jnp.zeros((tr, w), g.dtype)
        for s, a, w, d0 in plan:
            o_ref[:, d0:d0 + w] = x_ref[s, :, a:a + w]

    return pl.pallas_call(
        body, name=name, grid=(r // tr,), in_specs=[pl.BlockSpec((N_DEV, tr, sw), lambda i: (0, i, 0))],
        out_specs=pl.BlockSpec((tr, out_w), lambda i: (i, 0)), out_shape=jax.ShapeDtypeStruct((r, out_w), g.dtype),
        compiler_params=_params(("parallel",)),
    )(g)


def _shards_from_cols(srcs, plans, shard_w, name):
    r = srcs[0].shape[0]
    tr = _pick(r, (128,))
    n = len(srcs)

    def body(*refs):
        o_ref = refs[n]
        for x_ref, plan in zip(refs[:n], plans, strict=True):
            for s, a, w, d0 in plan:
                o_ref[s, :, a:a + w] = x_ref[:, d0:d0 + w].astype(COMM_DTYPE)

    return pl.pallas_call(
        body, name=name, grid=(r // tr,), in_specs=[pl.BlockSpec((tr, t.shape[1]), lambda i: (i, 0)) for t in srcs],
        out_specs=pl.BlockSpec((N_DEV, tr, shard_w), lambda i: (0, i, 0)),
        out_shape=jax.ShapeDtypeStruct((N_DEV, r, shard_w), COMM_DTYPE), compiler_params=_params(("parallel",)),
    )(*srcs)


def _pack(arrs):
    rows = []
    for a in arrs:
        flat = a.reshape(-1).astype(F32)
        rows.append(jnp.pad(flat, (0, (-flat.shape[0]) % (8 * LANES))).reshape(-1, LANES))
    return jnp.concatenate(rows, axis=0)


def _unpack(pack, shapes):
    out, r = [], 0
    for s in shapes:
        size = math.prod(s)
        nr = -(-size // (8 * LANES)) * 8
        out.append(pack[r:r + nr].reshape(-1)[:size].reshape(s))
        r += nr
    return out


def _group_lanes(t):
    lead = t.shape[:-1]
    t = t.reshape(lead + (SSD_GROUPS, SSD_HPG))
    t = jnp.pad(t, [(0, 0)] * len(lead) + [(0, 0), (0, LANES - SSD_HPG)])
    return t.reshape(lead + (SSD_GROUPS * LANES,))


def _ungroup_lanes(t):
    lead = t.shape[:-1]
    return t.reshape(lead + (SSD_GROUPS, LANES))[..., :SSD_HPG].reshape(lead + (SSD_HEADS,))


def kernel(x, meta_tokens, mix_norm_w, w_in, ssd_conv_w, ssd_conv_b, ssd_dt_bias, ssd_A_log, ssd_D, ssd_norm_w, w_branch_ssd, w_branch_ret, w_out, ffn_norm_w, w_up, ffn_conv_w, ffn_conv_b, w_down, final_norm_w, loss_target, m_meta_tokens, m_mix_norm_w, m_w_in, m_ssd_conv_w, m_ssd_conv_b, m_ssd_dt_bias, m_ssd_A_log, m_ssd_D, m_ssd_norm_w, m_w_branch_ssd, m_w_branch_ret, m_w_out, m_ffn_norm_w, m_w_up, m_ffn_conv_w, m_ffn_conv_b, m_w_down, m_final_norm_w, v_meta_tokens, v_mix_norm_w, v_w_in, v_ssd_conv_w, v_ssd_conv_b, v_ssd_dt_bias, v_ssd_A_log, v_ssd_D, v_ssd_norm_w, v_w_branch_ssd, v_w_branch_ret, v_w_out, v_ffn_norm_w, v_w_up, v_ffn_conv_w, v_ffn_conv_b, v_w_down, v_final_norm_w):
    seq, d = x.shape[1], x.shape[2]
    rows = seq + PAD_ROWS
    tm = _row_tile(rows)
    me = _slot(_place())
    d_ff = w_down.shape[1] * N_DEV

    big = [w_in[0], w_branch_ssd[0], w_branch_ret[0], w_out[0], w_up[0], w_down[0]]
    first = _allgather([w_in[0].astype(COMM_DTYPE), meta_tokens, ssd_conv_w[0], ffn_conv_w[0]], "gather_first")
    rest_src = [b.astype(COMM_DTYPE) for b in big[1:]]
    rest_handle, rest_token = _exchange_start(rest_src, False, "gather_rest_start", after=first[0])
    cols = lambda t: jnp.transpose(t, (1, 0, 2)).reshape(t.shape[1], -1)
    rws = lambda t: t.reshape(-1, t.shape[2])
    conv_w, fconv_w = cols(first[2]), cols(first[3])
    meta_full = cols(first[1]) + rest_token[0, 0]
    widths = [SSD_D_INNER, SSD_CONV_DIM, SSD_HEADS, RET_HEADS * RET_QK, RET_HEADS * RET_QK, RET_HEADS * RET_V,
              RET_HEADS * RET_V, d, d]
    offs = [0]
    for wd in widths:
        offs.append(offs[-1] + wd)
    order_p = [0, 1, 3, 4, 5, 6, 7, 8, 2]
    offs_p = [0]
    for i in order_p:
        offs_p.append(offs_p[-1] + (SSD_GROUPS * LANES if i == 2 else widths[i]))
    in_p = offs_p[-1]
    dt0 = offs_p[order_p.index(2)]
    pieces = [(offs[i], widths[i], offs_p[j]) for j, i in enumerate(order_p) if i != 2]
    pieces += [(offs[2] + SSD_HPG * grp, SSD_HPG, dt0 + LANES * grp) for grp in range(SSD_GROUPS)]
    in_plan = _column_plan(pieces, w_in.shape[2])
    w_in_p = _cols_from_shards(first[0], in_plan, in_p, [(dt0, SSD_GROUPS * LANES)], "w_in_columns")

    h0 = jnp.concatenate([jnp.zeros((FRONT, d), F32), meta_full, x[0]], axis=0)
    u1 = _rms_fwd(h0, mix_norm_w, "rms1")
    proj = [None if i == 1 else
            _mm(u1, w_in_p, mode="nn", out_dtype=F32 if i == 2 else ACT_DTYPE, tm=tm, tk=d, name=f"in_proj_{i}",
                tn=_pick(offs_p[j + 1] - offs_p[j], (1024, 512)), b_n0=offs_p[j], n_out=offs_p[j + 1] - offs_p[j])
            for j, i in enumerate(order_p)]
    z, _, q, k, v, g, gs, gr, dtr = proj
    xbc, xbc_c = _xbc_proj_conv(u1, w_in_p, offs_p[1], conv_w, ssd_conv_b)
    bias_p, alog_p, dsk_p = _group_lanes(ssd_dt_bias), _group_lanes(ssd_A_log), _group_lanes(ssd_D)
    y_ssd, y_scan, ssd_states = _ssd_fwd(xbc_c, dtr, z, bias_p, alog_p, dsk_p, ssd_norm_w)
    cos, sin, lgam = _rotary_tables(rows)
    y_ret, ret_states = _ret_fwd(q, k, v, g, cos, sin, lgam)
    rest_own, rest = _exchange_wait(rest_handle, False, y_ret, "gather_rest_wait")
    rest = [lax.dynamic_update_index_in_dim(land, own, me, 0) for land, own in zip(rest, rest_own, strict=True)]
    w_bs, w_br, w_o, w_dn = rws(rest[0]), rws(rest[1]), rws(rest[2]), rws(rest[4])
    w_up_f = _cols_from_shards(rest[3], _column_plan([(0, 2 * d_ff, 0)], w_up.shape[2]), 2 * d_ff, [], "w_up_columns")
    bs = _mm(y_ssd, w_bs, mode="nn", out_dtype=ACT_DTYPE, tm=tm, tn=d, tk=SSD_D_INNER, name="branch_ssd")
    br = _mm(y_ret, w_br, mode="nn", out_dtype=ACT_DTYPE, tm=tm, tn=d, tk=RET_HEADS * RET_V, name="branch_ret")
    merged = _merge_fwd(bs, br, gs, gr)
    h1 = _mm(merged, w_o, mode="nn", out_dtype=F32, tm=tm, tn=d, tk=d, name="out_proj", add=h0)
    u2 = _rms_fwd(h1, ffn_norm_w, "rms2")
    up_g, up_v, act = _ffn_up_conv(u2, w_up_f, fconv_w, ffn_conv_b)
    h2 = _mm(act, w_dn, mode="nn", out_dtype=F32, tm=tm, tn=d, tk=d_ff, name="ffn_down", add=h1)
    tgt = jnp.pad(loss_target[0], ((PAD_ROWS, 0), (0, 0)))
    dh2, loss_acc, g_final = _loss_head(h2, tgt, final_norm_w.reshape(1, d))

    tff = _pick(d_ff, (1408, 256))
    tkr = _pick(rows, (1664, 128))
    rparts = lambda t: t.reshape(N_DEV, -1, t.shape[1])
    d_act = _mm(dh2, w_dn, mode="nt", out_dtype=ACT_DTYPE, tm=tm, tn=tff, tk=d, name="d_act")
    g_w_dn = _mm(act, dh2, mode="tn", out_dtype=COMM_DTYPE, tm=tff, tn=d, tk=tkr, name="g_w_down")
    c_dn = [rparts(g_w_dn)]
    h_dn, t_dn = _exchange_start(c_dn, True, "scatter_down_start")
    d_up_g, d_up_v, g_fcw_g, g_fcb_g, g_fcw_v, g_fcb_v, g_w_up_g, g_w_up_v = _ffn_conv_bwd(
        up_g, up_v, d_act, fconv_w, ffn_conv_b + t_dn[0, 0], u2)
    g_fconv_w = jnp.concatenate([g_fcw_g, g_fcw_v], axis=1)
    g_fconv_b = jnp.concatenate([g_fcb_g, g_fcb_v], axis=1)
    c_up = [_shards_from_cols([g_w_up_g, g_w_up_v], [_column_plan([(0, d_ff, 0)], w_up.shape[2]),
                                                     _column_plan([(d_ff, d_ff, 0)], w_up.shape[2])], w_up.shape[2], "g_w_up_shards")]
    h_up, t_up = _exchange_start(c_up, True, "scatter_up_start")
    du2 = _mm(d_up_g, w_up_f, mode="nt", out_dtype=F32, tm=tm, tn=d, tk=d_ff, name="d_u2_gate", after=t_up)
    du2 = _mm(d_up_v, w_up_f, mode="nt", out_dtype=F32, tm=tm, tn=d, tk=d_ff, name="d_u2_value", add=du2, b_k0=d_ff)
    dh1, g_ffn_norm = _rms_bwd(du2, h1, ffn_norm_w, dh2, "rms2_bwd")
    d_merged = _mm(dh1, w_o, mode="nt", out_dtype=F32, tm=tm, tn=d, tk=d, name="d_merged")
    g_w_o = _mm(merged, dh1, mode="tn", out_dtype=COMM_DTYPE, tm=d, tn=d, tk=tkr, name="g_w_out")
    d_bs, d_br, d_gs, d_gr = _merge_bwd(d_merged, bs, br, gs, gr)
    d_yssd = _mm(d_bs, w_bs, mode="nt", out_dtype=ACT_DTYPE, tm=tm, tn=1024, tk=d, name="d_y_ssd")
    g_w_bs = _mm(y_ssd, d_bs, mode="tn", out_dtype=COMM_DTYPE, tm=1024, tn=d, tk=tkr, name="g_w_branch_ssd")
    d_yret = _mm(d_br, w_br, mode="nt", out_dtype=ACT_DTYPE, tm=tm, tn=1024, tk=d, name="d_y_ret")
    g_w_br = _mm(y_ret, d_br, mode="tn", out_dtype=COMM_DTYPE, tm=1024, tn=d, tk=tkr, name="g_w_branch_ret")
    c_mid = [rparts(g_w_bs), rparts(g_w_br), rparts(g_w_o)]
    h_mid, t_mid = _exchange_start(c_mid, True, "scatter_mid_start")
    d_yscan, d_z, g_nw = _ssd_gate_bwd(y_scan, z, d_yssd, ssd_norm_w + t_mid[0, 0])
    dxs, d_bm, d_cm, d_dtr, g_bias_p, g_alog_p, g_dsk_p = _ssd_bwd(xbc_c, dtr, bias_p, alog_p, dsk_p, ssd_states, d_yscan)
    d_xbc, g_conv_w, g_conv_b = _ssd_conv_bwd(xbc, dxs, d_bm, d_cm, conv_w, ssd_conv_b)
    d_q, d_k, d_v, d_g = _ret_bwd(q, k, v, g, cos, sin, lgam, ret_states, d_yret)
    dproj = jnp.concatenate([d_z, d_xbc, d_q, d_k, d_v, d_g, d_gs, d_gr, d_dtr.astype(ACT_DTYPE)], axis=1)
    g_w_in_p = _mm(u1, dproj, mode="tn", out_dtype=F32, tm=d, tn=_pick(in_p, (768, 512)), tk=tkr, name="g_w_in")
    c_in = [_shards_from_cols([g_w_in_p], [in_plan], w_in.shape[2], "g_w_in_shards")]
    h_in, t_in = _exchange_start(c_in, True, "scatter_in_start")
    du1 = _mm(dproj, w_in_p, mode="nt", out_dtype=F32, tm=tm, tn=d, tk=_pick(in_p, (4608, 512)), name="d_u1", after=t_in)
    dh0, g_mix_norm = _rms_bwd(du1, h0, mix_norm_w, dh1, "rms1_bwd")
    grad_x = dh0[PAD_ROWS:][None]

    landed = {}
    for key, handle, names in (("in", h_in, ["w_in"]), ("mid", h_mid, ["w_branch_ssd", "w_branch_ret", "w_out"]),
                               ("up", h_up, ["w_up"]), ("down", h_dn, ["w_down"])):
        srcs, lands = _exchange_wait(handle, True, dh0, f"scatter_{key}_wait")
        for nm, land, src in zip(names, lands, srcs, strict=True):
            landed[nm] = (lax.dynamic_index_in_dim(src, me, 0, keepdims=False), land)
    big_m = [m_w_in, m_w_branch_ssd, m_w_branch_ret, m_w_out, m_w_up, m_w_down]
    big_v = [v_w_in, v_w_branch_ssd, v_w_branch_ret, v_w_out, v_w_up, v_w_down]
    big_names = ["w_in", "w_branch_ssd", "w_branch_ret", "w_out", "w_up", "w_down"]
    big_out = {}
    for nm, w, m, v_ in zip(big_names, big, big_m, big_v, strict=True):
        big_out[nm] = [t[None] for t in _adam_shard(*landed[nm], w, m[0], v_[0], "adam_" + nm)]

    small_g = [dh0[FRONT:PAD_ROWS], g_mix_norm, g_conv_w, g_conv_b, _ungroup_lanes(g_bias_p), _ungroup_lanes(g_alog_p),
               _ungroup_lanes(g_dsk_p), g_nw, g_ffn_norm, g_fconv_w, g_fconv_b, g_final, loss_acc[0:1, 0:1]]
    total = _unpack(_allreduce_small(_pack(small_g)), [t.shape for t in small_g])
    loss = total[12].reshape(())
    shard = lambda t, width: lax.dynamic_slice_in_dim(t, me * width, width, axis=1)
    small_names = ["meta_tokens", "mix_norm_w", "ssd_conv_w", "ssd_conv_b", "ssd_dt_bias", "ssd_A_log", "ssd_D", "ssd_norm_w",
                   "ffn_norm_w", "ffn_conv_w", "ffn_conv_b", "final_norm_w"]
    small_w = [meta_tokens, mix_norm_w, ssd_conv_w, ssd_conv_b, ssd_dt_bias, ssd_A_log, ssd_D, ssd_norm_w, ffn_norm_w,
               ffn_conv_w, ffn_conv_b, final_norm_w]
    small_m = [m_meta_tokens, m_mix_norm_w, m_ssd_conv_w, m_ssd_conv_b, m_ssd_dt_bias, m_ssd_A_log, m_ssd_D, m_ssd_norm_w,
               m_ffn_norm_w, m_ffn_conv_w, m_ffn_conv_b, m_final_norm_w]
    small_v = [v_meta_tokens, v_mix_norm_w, v_ssd_conv_w, v_ssd_conv_b, v_ssd_dt_bias, v_ssd_A_log, v_ssd_D, v_ssd_norm_w,
               v_ffn_norm_w, v_ffn_conv_w, v_ffn_conv_b, v_final_norm_w]
    grads = total[:12]
    grads[0] = shard(grads[0], meta_tokens.shape[1])
    grads[2] = shard(grads[2], ssd_conv_w.shape[2])
    grads[9] = shard(grads[9], ffn_conv_w.shape[2])
    grads = [t.reshape(w.shape) for t, w in zip(grads, small_w, strict=True)]
    shapes = [w.shape for w in small_w]
    upd = _adam_small(_pack(small_w), _pack(grads), _pack(small_m), _pack(small_v))
    small_out = {nm: [gr_] + [u[i] for u in (_unpack(t, shapes) for t in upd)]
                 for i, (nm, gr_) in enumerate(zip(small_names, grads, strict=True))}

    order = ["meta_tokens", "mix_norm_w", "w_in", "ssd_conv_w", "ssd_conv_b", "ssd_dt_bias", "ssd_A_log", "ssd_D", "ssd_norm_w",
             "w_branch_ssd", "w_branch_ret", "w_out", "ffn_norm_w", "w_up", "ffn_conv_w", "ffn_conv_b", "w_down", "final_norm_w"]
    res = {**big_out, **small_out}
    return (loss, grad_x, *[res[nm][0] for nm in order], *[res[nm][1] for nm in order], *[res[nm][2] for nm in order],
            *[res[nm][3] for nm in order])
```

```python
import functools
import math

import jax
import jax.numpy as jnp
import numpy as np
from jax import lax
from jax.experimental import pallas as pl
from jax.experimental.pallas import tpu as pltpu

F32 = jnp.float32
MXU_DTYPE = jnp.bfloat16
ACT_DTYPE = jnp.bfloat16
COMM_DTYPE = jnp.bfloat16

N_META = 16
CHUNK = 128
FRONT = CHUNK - N_META
PAD_ROWS = FRONT + N_META
EPS = 1e-6
N_DEV = 8

SSD_D_INNER = 2048
SSD_HEAD_DIM = 64
SSD_HEADS = 32
SSD_GROUPS = 4
SSD_HPG = SSD_HEADS // SSD_GROUPS
SSD_STATE = 128
SSD_CONV = 4
SSD_CONV_DIM = SSD_D_INNER + 2 * SSD_GROUPS * SSD_STATE
SSD_GW = SSD_D_INNER // SSD_GROUPS
RET_HEADS = 4
RET_QK = 256
RET_V = 512
RET_HW = 2 * RET_QK + 2 * RET_V
ROPE_BASE = 10000.0
FFN_CONV = 3
HALO = 16
LANES = 128

ADAM_LR = 0.001
ADAM_B1 = 0.9
ADAM_B2 = 0.999
ADAM_EPS = 1e-08
ADAM_WD = 0.01
ADAM_STEP = 10

VMEM_LIMIT = 56 * 1024 * 1024
MESH = pl.DeviceIdType.MESH

NN = (((1,), (0,)), ((), ()))
NT = (((1,), (1,)), ((), ()))
TN = (((0,), (0,)), ((), ()))


def _params(sem):
    return pltpu.CompilerParams(dimension_semantics=sem, vmem_limit_bytes=VMEM_LIMIT)


def _mxu(a, b, dn):
    return lax.dot_general(a.astype(MXU_DTYPE), b.astype(MXU_DTYPE), dn, preferred_element_type=F32)


@functools.partial(jax.custom_vjp, nondiff_argnums=(2,))
def _dot(a, b, dn=NN):
    return _mxu(a, b, dn)


def _dot_fwd(a, b, dn):
    return _mxu(a, b, dn), (a, b)


def _dot_bwd(dn, res, g):
    a, b = res
    if dn == NN:
        return _mxu(g, b, NT), _mxu(a, g, TN)
    if dn == NT:
        return _mxu(g, b, NN), _mxu(g, a, TN)
    assert dn == TN
    return _mxu(b, g, NT), _mxu(a, g, NN)


_dot.defvjp(_dot_fwd, _dot_bwd)


def _silu(x):
    return x * jax.nn.sigmoid(x)


def _dsilu(x):
    s = jax.nn.sigmoid(x)
    return s * (1.0 + x * (1.0 - s))


def _row_tile(rows):
    return 640 if rows % 640 == 0 else 128


def _mm(a, b, *, mode, out_dtype, tm, tn, tk, name, add=None, after=None, b_k0=0, b_n0=0, n_out=None):
    if mode == "nn":
        (m, k), k2 = a.shape, b.shape[0]
        n = b.shape[1] if n_out is None else n_out
        assert b_n0 % tn == 0 and b_n0 + n <= b.shape[1]
    elif mode == "nt":
        (m, k), n = a.shape, b.shape[0]
        k2 = k if b_k0 % tk == 0 and b_k0 + k <= b.shape[1] else None
    else:
        (k, m), (k2, n) = a.shape, b.shape
    assert (b_k0 == 0 or mode == "nt") and ((b_n0 == 0 and n_out is None) or mode == "nn")
    assert k == k2 and m % tm == 0 and n % tn == 0 and k % tk == 0, (name, a.shape, b.shape, tm, tn, tk)
    kb0, nb0 = b_k0 // tk, b_n0 // tn
    nk = k // tk
    dn = {"nn": NN, "nt": NT, "tn": TN}[mode]
    has_add = add is not None
    n_in = 2 + has_add + (after is not None)

    def body(*refs):
        a_ref, b_ref = refs[0], refs[1]
        add_ref = refs[2] if has_add else None
        o_ref = refs[n_in]
        p = _dot(a_ref[...], b_ref[...], dn)
        if nk == 1:
            if has_add:
                p = p + add_ref[...]
            o_ref[...] = p.astype(out_dtype)
        else:
            acc_ref = refs[n_in + 1]
            kk = pl.program_id(2)

            @pl.when(kk == 0)
            def _():
                acc_ref[...] = p

            @pl.when(kk > 0)
            def _():
                acc_ref[...] += p

            @pl.when(kk == nk - 1)
            def _():
                r = acc_ref[...]
                if has_add:
                    r = r + add_ref[...]
                o_ref[...] = r.astype(out_dtype)

    if mode == "tn":
        a_spec = pl.BlockSpec((tk, tm), lambda j, i, kk: (kk, i))
    else:
        a_spec = pl.BlockSpec((tm, tk), lambda j, i, kk: (i, kk))
    if mode == "nt":
        b_spec = pl.BlockSpec((tn, tk), lambda j, i, kk: (j, kk + kb0))
    else:
        b_spec = pl.BlockSpec((tk, tn), lambda j, i, kk: (kk, j + nb0))
    o_spec = pl.BlockSpec((tm, tn), lambda j, i, kk: (i, j))
    in_specs = [a_spec, b_spec] + ([o_spec] if has_add else [])
    args = (a, b) + ((add,) if has_add else ())
    if after is not None:
        in_specs.append(pl.BlockSpec(memory_space=pl.ANY))
        args += (after,)
    return pl.pallas_call(
        body, name=name, grid=(n // tn, m // tm, nk), in_specs=in_specs, out_specs=o_spec,
        out_shape=jax.ShapeDtypeStruct((m, n), out_dtype),
        scratch_shapes=[pltpu.VMEM((tm, tn), F32)] if nk > 1 else [],
        compiler_params=_params(("parallel", "parallel", "arbitrary")),
    )(*args)


def _pick(n, cands):
    for c in cands:
        if n % c == 0:
            return c
    return n


def _rms_fwd(h, w, name):
    rows, d = h.shape
    tm = _row_tile(rows)

    def body(h_ref, w_ref, u_ref):
        x = h_ref[...]
        r = lax.rsqrt(jnp.mean(x * x, axis=-1, keepdims=True) + EPS)
        u_ref[...] = (x * r * w_ref[...]).astype(ACT_DTYPE)

    return pl.pallas_call(
        body, name=name, grid=(rows // tm,),
        in_specs=[pl.BlockSpec((tm, d), lambda i: (i, 0)), pl.BlockSpec((1, d), lambda i: (0, 0))],
        out_specs=pl.BlockSpec((tm, d), lambda i: (i, 0)),
        out_shape=jax.ShapeDtypeStruct((rows, d), ACT_DTYPE),
        compiler_params=_params(("parallel",)),
    )(h, w)


def _rms_bwd(du, h, w, dres, name):
    rows, d = h.shape
    tm = _row_tile(rows)

    def body(du_ref, h_ref, w_ref, dres_ref, dh_ref, dw_ref):
        x = h_ref[...]
        dy = du_ref[...].astype(F32)
        r = lax.rsqrt(jnp.mean(x * x, axis=-1, keepdims=True) + EPS)
        xhat = x * r
        dxn = dy * w_ref[...]
        dx = r * (dxn - xhat * jnp.mean(dxn * xhat, axis=-1, keepdims=True))
        dh_ref[...] = dres_ref[...] + dx

        @pl.when(pl.program_id(0) == 0)
        def _():
            dw_ref[...] = jnp.zeros_like(dw_ref)

        dw_ref[...] += jnp.sum(dy * xhat, axis=0, keepdims=True)

    return pl.pallas_call(
        body, name=name, grid=(rows // tm,),
        in_specs=[pl.BlockSpec((tm, d), lambda i: (i, 0)), pl.BlockSpec((tm, d), lambda i: (i, 0)),
                  pl.BlockSpec((1, d), lambda i: (0, 0)), pl.BlockSpec((tm, d), lambda i: (i, 0))],
        out_specs=[pl.BlockSpec((tm, d), lambda i: (i, 0)), pl.BlockSpec((1, d), lambda i: (0, 0))],
        out_shape=[jax.ShapeDtypeStruct((rows, d), F32), jax.ShapeDtypeStruct((1, d), F32)],
        compiler_params=_params(("arbitrary",)),
    )(du, h, w, dres)


def _loss_head(h2, tgt, w):
    rows, d = h2.shape
    tm = _row_tile(rows)

    def body(h_ref, t_ref, w_ref, dh_ref, loss_ref, dw_ref):
        i = pl.program_id(0)
        x = h_ref[...]
        r = lax.rsqrt(jnp.mean(x * x, axis=-1, keepdims=True) + EPS)
        xhat = x * r
        wv = w_ref[...]
        row = i * tm + lax.broadcasted_iota(jnp.int32, (tm, 1), 0)
        live = row >= PAD_ROWS
        diff = jnp.where(live, xhat * wv - t_ref[...], 0.0)
        dy = diff * (1.0 / d)
        dxn = dy * wv
        dh_ref[...] = r * (dxn - xhat * jnp.mean(dxn * xhat, axis=-1, keepdims=True))

        @pl.when(i == 0)
        def _():
            loss_ref[...] = jnp.zeros_like(loss_ref)
            dw_ref[...] = jnp.zeros_like(dw_ref)

        loss_ref[...] += 0.5 * jnp.sum(jnp.mean(diff * diff, axis=-1, keepdims=True))
        dw_ref[...] += jnp.sum(dy * xhat, axis=0, keepdims=True)

    return pl.pallas_call(
        body, name="loss_head", grid=(rows // tm,),
        in_specs=[pl.BlockSpec((tm, d), lambda i: (i, 0)), pl.BlockSpec((tm, d), lambda i: (i, 0)),
                  pl.BlockSpec((1, d), lambda i: (0, 0))],
        out_specs=[pl.BlockSpec((tm, d), lambda i: (i, 0)), pl.BlockSpec((8, LANES), lambda i: (0, 0)),
                   pl.BlockSpec((1, d), lambda i: (0, 0))],
        out_shape=[jax.ShapeDtypeStruct((rows, d), F32), jax.ShapeDtypeStruct((8, LANES), F32),
                   jax.ShapeDtypeStruct((1, d), F32)],
        compiler_params=_params(("arbitrary",)),
    )(h2, tgt, w)


def _prev_halo_spec(tm, width, col):
    return pl.BlockSpec((HALO, width), lambda j, i: (jnp.maximum(i * (tm // HALO) - 1, 0), col(j)))


def _next_halo_spec(tm, rows, width, col):
    last = rows // HALO - 1
    return pl.BlockSpec((HALO, width), lambda j, i: (jnp.minimum((i + 1) * (tm // HALO), last), col(j)))


def _conv_taps(cat, w_ref, b_ref, kw):
    acc = b_ref[...] + w_ref[kw - 1:kw, :] * cat
    for s in range(1, kw):
        acc = acc + w_ref[kw - 1 - s:kw - s, :] * pltpu.roll(cat, s, 0)
    return acc


def _conv_back(dpre, w_ref, kw):
    n = dpre.shape[0]
    acc = w_ref[kw - 1:kw, :] * dpre
    for s in range(1, kw):
        acc = acc + w_ref[kw - 1 - s:kw - s, :] * pltpu.roll(dpre, n - s, 0)
    return acc


def _xbc_proj_conv(u1, w_in_p, col0, w, b):
    rows, d = u1.shape
    width = w.shape[1]
    tm, tc = _row_tile(rows), 512
    cb0 = col0 // tc
    assert col0 % tc == 0

    def body(u_ref, m_ref, w_ref, b_ref, x_ref, o_ref, carry):
        i = pl.program_id(1)

        @pl.when(i == 0)
        def _():
            carry[...] = jnp.zeros_like(carry)

        xb = _mxu(u_ref[...], m_ref[...], NN).astype(ACT_DTYPE)
        x_ref[...] = xb
        x = xb.astype(F32)
        cat = jnp.concatenate([carry[...], x], axis=0)
        carry[...] = x[tm - HALO:, :]
        pre = _conv_taps(cat, w_ref, b_ref, SSD_CONV)[HALO:]
        row = i * tm + lax.broadcasted_iota(jnp.int32, (tm, 1), 0)
        o_ref[...] = jnp.where(row >= FRONT, _silu(pre), 0.0).astype(ACT_DTYPE)

    main = pl.BlockSpec((tm, tc), lambda j, i: (i, j))
    par = lambda r: pl.BlockSpec((r, tc), lambda j, i: (0, j))
    act = jax.ShapeDtypeStruct((rows, width), ACT_DTYPE)
    return pl.pallas_call(
        body, name="xbc_proj_conv", grid=(width // tc, rows // tm),
        in_specs=[pl.BlockSpec((tm, d), lambda j, i: (i, 0)), pl.BlockSpec((d, tc), lambda j, i: (0, cb0 + j)),
                  par(SSD_CONV), par(1)],
        out_specs=[main, main], out_shape=[act, act], scratch_shapes=[pltpu.VMEM((HALO, tc), F32)],
        compiler_params=_params(("parallel", "arbitrary")),
    )(u1, w_in_p, w, b)


def _ssd_conv_bwd(xbc, dxs, dbm, dcm, w, b, dproj, col0):
    rows, width = xbc.shape
    tm, tc = _row_tile(rows), 512
    kw = SSD_CONV
    nx = dxs.shape[1] // tc
    assert dbm.shape[1] == tc and dcm.shape[1] == tc and width == (nx + 2) * tc and col0 % tc == 0

    def body(x_ref, xp_ref, xn_ref, d0_ref, d0n_ref, d1_ref, d1n_ref, d2_ref, d2n_ref, w_ref, b_ref, dp_ref,
             dx_ref, dw_ref, db_ref):
        j, i = pl.program_id(0), pl.program_id(1)
        xp = jnp.where(i == 0, 0.0, xp_ref[...].astype(F32))
        cat = jnp.concatenate([xp, x_ref[...].astype(F32), xn_ref[...].astype(F32)], axis=0)
        sh = [cat] + [pltpu.roll(cat, s, 0) for s in range(1, kw)]
        pre = b_ref[...] + w_ref[kw - 1:kw, :] * sh[0]
        for s in range(1, kw):
            pre = pre + w_ref[kw - 1 - s:kw - s, :] * sh[s]
        pre = pre[HALO:]
        row = i * tm + lax.broadcasted_iota(jnp.int32, (tm + HALO, 1), 0)
        live = (row >= FRONT) & (row < rows)
        pick = lambda a, bb, c: jnp.where(j < nx, a[...], jnp.where(j == nx, bb[...], c[...])).astype(F32)
        dout = jnp.concatenate([pick(d0_ref, d1_ref, d2_ref), pick(d0n_ref, d1n_ref, d2n_ref)], axis=0)
        dpre = jnp.where(live, dout * _dsilu(pre), 0.0)
        dx_ref[...] = _conv_back(dpre, w_ref, kw)[:tm].astype(ACT_DTYPE)

        @pl.when(i == 0)
        def _():
            dw_ref[...] = jnp.zeros_like(dw_ref)
            db_ref[...] = jnp.zeros_like(db_ref)

        dmain = dpre[:tm]
        db_ref[...] += jnp.sum(dmain, axis=0, keepdims=True)
        for k in range(kw):
            dw_ref[k:k + 1, :] += jnp.sum(dmain * sh[kw - 1 - k][HALO:HALO + tm], axis=0, keepdims=True)

    main = pl.BlockSpec((tm, tc), lambda j, i: (i, j))
    par = lambda r: pl.BlockSpec((r, tc), lambda j, i: (0, j))
    col = lambda j: j
    xcol, zero = (lambda j: jnp.minimum(j, nx - 1)), (lambda j: 0)
    dspecs = lambda c: [pl.BlockSpec((tm, tc), lambda j, i: (i, c(j))), _next_halo_spec(tm, rows, tc, c)]
    return pl.pallas_call(
        body, name="ssd_conv_bwd", grid=(width // tc, rows // tm),
        in_specs=[main, _prev_halo_spec(tm, tc, col), _next_halo_spec(tm, rows, tc, col)]
        + dspecs(xcol) + dspecs(zero) + dspecs(zero) + [par(kw), par(1), pl.BlockSpec(memory_space=pl.ANY)],
        out_specs=[pl.BlockSpec((tm, tc), lambda j, i: (i, col0 // tc + j)), par(kw), par(1)],
        out_shape=[jax.ShapeDtypeStruct(dproj.shape, dproj.dtype), jax.ShapeDtypeStruct((kw, width), F32),
                   jax.ShapeDtypeStruct((1, width), F32)],
        input_output_aliases={11: 0}, compiler_params=_params(("parallel", "arbitrary")),
    )(xbc, xbc, xbc, dxs, dxs, dbm, dbm, dcm, dcm, w, b, dproj)


def _ffn_up_conv(u2, w_up, w, b):
    rows, d = u2.shape
    width = w_up.shape[1]
    dff = width // 2
    tm, tc = _row_tile(rows), _pick(dff, (256, 128))
    nb = dff // tc
    kw = FFN_CONV

    def body(u_ref, mg_ref, mv_ref, wg_ref, bg_ref, wv_ref, bv_ref, ug_ref, uv_ref, o_ref, cg, cv):
        i = pl.program_id(1)

        @pl.when(i == 0)
        def _():
            cg[...] = jnp.zeros_like(cg)
            cv[...] = jnp.zeros_like(cv)

        def pre(m_ref, up_ref, carry, w_ref, b_ref):
            upb = _mxu(u_ref[...], m_ref[...], NN).astype(ACT_DTYPE)
            up_ref[...] = upb
            x = upb.astype(F32)
            cat = jnp.concatenate([carry[...], x], axis=0)
            carry[...] = x[tm - HALO:, :]
            return _conv_taps(cat, w_ref, b_ref, kw)[HALO:]

        ag = pre(mg_ref, ug_ref, cg, wg_ref, bg_ref)
        av = pre(mv_ref, uv_ref, cv, wv_ref, bv_ref)
        o_ref[...] = (_silu(ag) * av).astype(ACT_DTYPE)

    gcol, vcol = (lambda j: j), (lambda j: j + nb)
    mat = lambda col: pl.BlockSpec((d, tc), lambda j, i: (0, col(j)))
    par = lambda r, col: pl.BlockSpec((r, tc), lambda j, i: (0, col(j)))
    out = pl.BlockSpec((tm, tc), lambda j, i: (i, j))
    act = jax.ShapeDtypeStruct((rows, dff), ACT_DTYPE)
    return pl.pallas_call(
        body, name="ffn_up_conv", grid=(nb, rows // tm),
        in_specs=[pl.BlockSpec((tm, d), lambda j, i: (i, 0)), mat(gcol), mat(vcol),
                  par(kw, gcol), par(1, gcol), par(kw, vcol), par(1, vcol)],
        out_specs=[out, out, out], out_shape=[act, act, act],
        scratch_shapes=[pltpu.VMEM((HALO, tc), F32), pltpu.VMEM((HALO, tc), F32)],
        compiler_params=_params(("parallel", "arbitrary")),
    )(u2, w_up, w_up, w, b, w, b)


def _ffn_conv_bwd(up_g, up_v, dact, w, b, u2):
    rows, dff = up_g.shape
    d = u2.shape[1]
    tm, tc = _row_tile(rows), _pick(dff, (256, 128))
    nb = dff // tc
    kw = FFN_CONV

    def body(g_ref, gp_ref, gn_ref, v_ref, vp_ref, vn_ref, d_ref, dn_ref, wg_ref, bg_ref, wv_ref, bv_ref, u_ref,
             dxg_ref, dxv_ref, dwg_ref, dbg_ref, dwv_ref, dbv_ref, gwg_ref, gwv_ref):
        i = pl.program_id(1)

        def shifted(x_ref, xp_ref, xn_ref):
            xp = jnp.where(i == 0, 0.0, xp_ref[...].astype(F32))
            cat = jnp.concatenate([xp, x_ref[...].astype(F32), xn_ref[...].astype(F32)], axis=0)
            return [cat] + [pltpu.roll(cat, s, 0) for s in range(1, kw)]

        def taps(sh, w_ref, b_ref):
            acc = b_ref[...] + w_ref[kw - 1:kw, :] * sh[0]
            for s in range(1, kw):
                acc = acc + w_ref[kw - 1 - s:kw - s, :] * sh[s]
            return acc[HALO:]

        sh_g, sh_v = shifted(g_ref, gp_ref, gn_ref), shifted(v_ref, vp_ref, vn_ref)
        ag, av = taps(sh_g, wg_ref, bg_ref), taps(sh_v, wv_ref, bv_ref)
        row = i * tm + lax.broadcasted_iota(jnp.int32, (tm + HALO, 1), 0)
        dout = jnp.concatenate([d_ref[...].astype(F32), dn_ref[...].astype(F32)], axis=0)
        dout = jnp.where(row < rows, dout, 0.0)
        s = jax.nn.sigmoid(ag)
        silu = ag * s
        dpre_v = dout * silu
        dpre_g = dout * av * (s + silu * (1.0 - s))

        @pl.when(i == 0)
        def _():
            for r in (dwg_ref, dbg_ref, dwv_ref, dbv_ref, gwg_ref, gwv_ref):
                r[...] = jnp.zeros_like(r)

        for dpre, sh, w_ref, dx_ref, dw_ref, db_ref, gw_ref in (
                (dpre_g, sh_g, wg_ref, dxg_ref, dwg_ref, dbg_ref, gwg_ref), (dpre_v, sh_v, wv_ref, dxv_ref, dwv_ref, dbv_ref, gwv_ref)):
            dx = _conv_back(dpre, w_ref, kw)[:tm].astype(ACT_DTYPE)
            dx_ref[...] = dx
            gw_ref[...] += _mxu(u_ref[...], dx, TN)
            dmain = dpre[:tm]
            db_ref[...] += jnp.sum(dmain, axis=0, keepdims=True)
            for k in range(kw):
                dw_ref[k:k + 1, :] += jnp.sum(dmain * sh[kw - 1 - k][HALO:HALO + tm], axis=0, keepdims=True)

    gcol, vcol = (lambda j: j), (lambda j: j + nb)
    main = lambda col: pl.BlockSpec((tm, tc), lambda j, i: (i, col(j)))
    par = lambda r, col: pl.BlockSpec((r, tc), lambda j, i: (0, col(j)))
    halos = lambda col: [_prev_halo_spec(tm, tc, col), _next_halo_spec(tm, rows, tc, col)]
    act_shape = jax.ShapeDtypeStruct((rows, dff), ACT_DTYPE)
    par_shapes = [jax.ShapeDtypeStruct((kw, dff), F32), jax.ShapeDtypeStruct((1, dff), F32)]
    gw_shape = jax.ShapeDtypeStruct((d, dff), F32)
    return pl.pallas_call(
        body, name="ffn_conv_bwd", grid=(nb, rows // tm),
        in_specs=[main(gcol)] + halos(gcol) + [main(gcol)] + halos(gcol) + [main(gcol), _next_halo_spec(tm, rows, tc, gcol),
                  par(kw, gcol), par(1, gcol), par(kw, vcol), par(1, vcol), pl.BlockSpec((tm, d), lambda j, i: (i, 0))],
        out_specs=[main(gcol), main(gcol), par(kw, gcol), par(1, gcol), par(kw, gcol), par(1, gcol), par(d, gcol), par(d, gcol)],
        out_shape=[act_shape, act_shape] + par_shapes + par_shapes + [gw_shape, gw_shape],
        compiler_params=_params(("parallel", "arbitrary")),
    )(up_g, up_g, up_g, up_v, up_v, up_v, dact, dact, w, b, w, b, u2)


def _ssd_scalars(dtr, dt_bias, a_log, live):
    q = CHUNK
    pre = dtr + dt_bias
    dt = jnp.where(live, jax.nn.softplus(pre), 0.0)
    a_neg = -jnp.exp(a_log)
    li = lax.broadcasted_iota(jnp.int32, (q, q), 0)
    si = lax.broadcasted_iota(jnp.int32, (q, q), 1)
    causal = li >= si
    tri = jnp.where(causal, 1.0, 0.0).astype(F32)
    a_cs = jnp.dot(tri, dt * a_neg, precision=lax.Precision.HIGHEST, preferred_element_type=F32)
    return pre, dt, a_neg, a_cs, causal, tri


def _head_select():
    r = lax.broadcasted_iota(jnp.int32, (LANES, SSD_GW), 0)
    c = lax.broadcasted_iota(jnp.int32, (LANES, SSD_GW), 1)
    return jnp.where(c // SSD_HEAD_DIM == r, 1.0, 0.0).astype(MXU_DTYPE)


def _split(t, parts):
    out, rem = [], t
    for _ in range(parts):
        p = rem.astype(MXU_DTYPE)
        out.append(p)
        rem = rem - p.astype(F32)
    return out


def _head_cols(t, sel):
    return sum(_mxu(p, sel, NN) for p in _split(t, 2))


def _head_sums(t, sel):
    return sum(_mxu(p, sel, NT) for p in _split(t, 3))


def _half_masks():
    lane = lax.broadcasted_iota(jnp.int32, (CHUNK, LANES), 1)
    return lane < SSD_HEAD_DIM, lane >= SSD_HEAD_DIM


def _ssd_scan(xs, bm, cm, dtr, prev, dt_bias, a_log, d_skip, live):
    q = CHUNK
    sel = _head_select()
    _, dt, _, a_cs, causal, _ = _ssd_scalars(dtr, dt_bias, a_log, live)
    a_cs_t = a_cs.T
    a_end = a_cs[q - 1:q, :]
    e_x = _head_cols(jnp.exp(a_cs), sel)
    xdt = xs * _head_cols(dt, sel)
    cb = _dot(cm, bm, NT)
    y = _dot(cm, prev) * e_x + _head_cols(jnp.broadcast_to(d_skip, (8, LANES)), sel)[0:1] * xs
    new = prev * e_x[q - 1:q, :] + _dot(bm, xdt * _head_cols(jnp.exp(a_end - a_cs), sel), TN)
    masks = _half_masks()
    ys = []
    for pp in range(SSD_HPG // 2):
        xpair = xdt[:, pp * LANES:(pp + 1) * LANES]
        acc = jnp.zeros((q, LANES), F32)
        for half in range(2):
            hh = 2 * pp + half
            decay = jnp.exp(jnp.where(causal, a_cs[:, hh:hh + 1] - a_cs_t[hh:hh + 1, :], -jnp.inf))
            acc = acc + _dot(cb * decay, jnp.where(masks[half], xpair, 0.0))
        ys.append(acc)
    return y + jnp.concatenate(ys, axis=1), new


def _ssd_gate(y, z, nw):
    yz = y * _silu(z)
    return yz * lax.rsqrt(jnp.mean(yz * yz, axis=-1, keepdims=True) + EPS) * nw


def _ssd_scan_bwd(xs, bm, cm, dtr, prev, dt_bias, a_log, d_skip, live, dy, dnew):
    q = CHUNK
    sel = _head_select()
    pre, dt, a_neg, a_cs, causal, tri = _ssd_scalars(dtr, dt_bias, a_log, live)
    a_cs_t = a_cs.T
    a_end = a_cs[q - 1:q, :]
    dt_x, e_x, w_x = _head_cols(dt, sel), _head_cols(jnp.exp(a_cs), sel), _head_cols(jnp.exp(a_end - a_cs), sel)
    g_x, d_x = e_x[q - 1:q, :], _head_cols(jnp.broadcast_to(d_skip, (8, LANES)), sel)[0:1]
    xdt = xs * dt_x
    u = xdt * w_x
    cb = _mxu(cm, bm, NT)
    cs = _mxu(cm, prev, NN)
    dye = dy * e_x
    dcm = _mxu(dye, prev, NT)
    dprev = _mxu(cm, dye, TN) + dnew * g_x
    dacs_x = dye * cs
    dbm = _mxu(u, dnew, NT)
    du = _mxu(bm, dnew, NN)
    dw_x = du * u
    dacs_x = dacs_x - dw_x
    dend_x = jnp.sum(dw_x + dnew * prev * g_x, axis=0, keepdims=True)
    dxdt = du * w_x
    lane = lax.broadcasted_iota(jnp.int32, (q, LANES), 1)
    sub = lax.broadcasted_iota(jnp.int32, (q, LANES), 0)
    dcb = jnp.zeros((q, q), F32)
    dacs = jnp.zeros((q, LANES), F32)
    dacs_t = jnp.zeros((q, LANES), F32)
    masks = _half_masks()
    dxdt_p = []
    for pp in range(SSD_HPG // 2):
        ps = slice(pp * LANES, (pp + 1) * LANES)
        acc = jnp.zeros((q, LANES), F32)
        for half in range(2):
            hh = 2 * pp + half
            decay = jnp.exp(jnp.where(causal, a_cs[:, hh:hh + 1] - a_cs_t[hh:hh + 1, :], -jnp.inf))
            m = cb * decay
            dyh = jnp.where(masks[half], dy[:, ps], 0.0)
            dm = _mxu(dyh, xdt[:, ps], NT)
            acc = acc + _mxu(m, dyh, TN)
            dcb = dcb + dm * decay
            p = dm * m
            dacs = jnp.where(lane == hh, jnp.sum(p, axis=1, keepdims=True), dacs)
            dacs_t = jnp.where(sub == hh, jnp.sum(p, axis=0, keepdims=True), dacs_t)
        dxdt_p.append(acc)
    dcm = dcm + _mxu(dcb, bm, NN)
    dbm = dbm + _mxu(dcb, cm, TN)
    dxdt = dxdt + jnp.concatenate(dxdt_p, axis=1)
    dxs = dy * d_x + dxdt * dt_x
    rows_x = jnp.concatenate([dend_x, jnp.sum(dy * xs, axis=0, keepdims=True), jnp.zeros((6, SSD_GW), F32)], axis=0)
    rows = _head_sums(rows_x, sel)
    dacs = dacs - dacs_t.T + _head_sums(dacs_x, sel)
    dacs = dacs + jnp.where(sub == q - 1, rows[0:1], 0.0)
    tri_t = jnp.where(causal, 0.0, 1.0).astype(F32) + jnp.where(lane == sub, 1.0, 0.0)
    da = jnp.dot(tri_t, dacs, precision=lax.Precision.HIGHEST, preferred_element_type=F32)
    ddt = _head_sums(dxdt * xs, sel) + da * a_neg
    dalog = jnp.sum(da * dt, axis=0, keepdims=True) * a_neg
    ddtr = jnp.where(live, ddt * jax.nn.sigmoid(pre), 0.0)
    dbias = jnp.sum(ddtr, axis=0, keepdims=True)
    return dxs, dbm, dcm, ddtr, dprev, dbias, dalog, rows[1:2]


def _chunks_per_step(nc):
    return 5 if nc % 5 == 0 else 1


def _ssd_specs(rev, nc):
    per = _chunks_per_step(nc)
    steps = nc // per
    sidx = (lambda s: steps - 1 - s) if rev else (lambda s: s)
    nb_b = SSD_D_INNER // SSD_STATE
    row = lambda width, col=lambda g: g: pl.BlockSpec((per * CHUNK, width), lambda g, s: (sidx(s), col(g)))
    par = lambda width: pl.BlockSpec((1, width), lambda g, s: (0, g))
    state = lambda: pl.BlockSpec((per, 1, SSD_STATE, SSD_GW), lambda g, s: (sidx(s), g, 0, 0))
    xbc = [row(SSD_GW), row(SSD_STATE, lambda g: nb_b + g), row(SSD_STATE, lambda g: nb_b + SSD_GROUPS + g)]
    return per, steps, sidx, row, par, state, xbc


def _ssd_fwd(xbc_c, dtr, z, dt_bias, a_log, d_skip, nw):
    rows = z.shape[0]
    nc = rows // CHUNK
    per, steps, _, row, par, state, xbc = _ssd_specs(False, nc)

    def body(xs_ref, b_ref, c_ref, dt_ref, z_ref, bias_ref, al_ref, dk_ref, nw_ref, o_ref, y_ref, st_ref, carry):
        s = pl.program_id(1)

        @pl.when(s == 0)
        def _():
            carry[...] = jnp.zeros_like(carry)

        for j in range(per):
            rs = pl.ds(j * CHUNK, CHUNK)
            live = (s * per + j) * CHUNK + lax.broadcasted_iota(jnp.int32, (CHUNK, 1), 0) >= FRONT
            prev = carry[...]
            st_ref[j, 0] = prev
            y, new = _ssd_scan(xs_ref[rs, :].astype(F32), b_ref[rs, :].astype(F32), c_ref[rs, :].astype(F32), dt_ref[rs, :],
                               prev, bias_ref[...], al_ref[...], dk_ref[...], live)
            y_ref[rs, :] = y.astype(ACT_DTYPE)
            o_ref[rs, :] = _ssd_gate(y, z_ref[rs, :].astype(F32), nw_ref[...]).astype(ACT_DTYPE)
            carry[...] = new

    act = jax.ShapeDtypeStruct((rows, SSD_D_INNER), ACT_DTYPE)
    return pl.pallas_call(
        body, name="ssd_fwd", grid=(SSD_GROUPS, steps),
        in_specs=xbc + [row(LANES), row(SSD_GW), par(LANES), par(LANES), par(LANES), par(SSD_GW)],
        out_specs=[row(SSD_GW), row(SSD_GW), state()],
        out_shape=[act, act, jax.ShapeDtypeStruct((nc, SSD_GROUPS, SSD_STATE, SSD_GW), F32)],
        scratch_shapes=[pltpu.VMEM((SSD_STATE, SSD_GW), F32)],
        compiler_params=_params(("parallel", "arbitrary")),
    )(xbc_c, xbc_c, xbc_c, dtr, z, dt_bias, a_log, d_skip, nw)


def _ssd_gate_bwd(y, z, dout, nw, dproj, col0):
    rows = y.shape[0]
    tm = _row_tile(rows)
    assert col0 % SSD_GW == 0

    def body(y_ref, z_ref, do_ref, nw_ref, dp_ref, dy_ref, dz_ref, dnw_ref):
        yv, zv, dov = y_ref[...].astype(F32), z_ref[...].astype(F32), do_ref[...].astype(F32)
        s = jax.nn.sigmoid(zv)
        silu = zv * s
        yz = yv * silu
        r = lax.rsqrt(jnp.mean(yz * yz, axis=-1, keepdims=True) + EPS)
        yhat = yz * r
        dn = dov * nw_ref[...]
        dyz = r * (dn - yhat * jnp.mean(dn * yhat, axis=-1, keepdims=True))
        dy_ref[...] = (dyz * silu).astype(ACT_DTYPE)
        dz_ref[...] = (dyz * yv * (s + silu * (1.0 - s))).astype(ACT_DTYPE)

        @pl.when(pl.program_id(1) == 0)
        def _():
            dnw_ref[...] = jnp.zeros_like(dnw_ref)

        dnw_ref[...] += jnp.sum(dov * yhat, axis=0, keepdims=True)

    spec = pl.BlockSpec((tm, SSD_GW), lambda g, i: (i, g))
    par = pl.BlockSpec((1, SSD_GW), lambda g, i: (0, g))
    act = jax.ShapeDtypeStruct((rows, SSD_D_INNER), ACT_DTYPE)
    return pl.pallas_call(
        body, name="ssd_gate_bwd", grid=(SSD_GROUPS, rows // tm),
        in_specs=[spec, spec, spec, par, pl.BlockSpec(memory_space=pl.ANY)],
        out_specs=[spec, pl.BlockSpec((tm, SSD_GW), lambda g, i: (i, col0 // SSD_GW + g)), par],
        out_shape=[act, jax.ShapeDtypeStruct(dproj.shape, dproj.dtype), jax.ShapeDtypeStruct((1, SSD_D_INNER), F32)],
        input_output_aliases={4: 1}, compiler_params=_params(("parallel", "arbitrary")),
    )(y, z, dout, nw, dproj)


def _ssd_bwd(xbc_c, dtr, dt_bias, a_log, d_skip, states, dy, dproj, col0):
    rows = dy.shape[0]
    nc = rows // CHUNK
    per, steps, sidx, row, par, state, xbc = _ssd_specs(True, nc)
    assert col0 % LANES == 0

    def body(xs_ref, b_ref, c_ref, dt_ref, bias_ref, al_ref, dk_ref, st_ref, dy_ref, dp_ref,
             dxs_ref, db_ref, dc_ref, ddt_ref, dbias_ref, dal_ref, ddk_ref, carry):
        s = pl.program_id(1)

        @pl.when(s == 0)
        def _():
            carry[...] = jnp.zeros_like(carry)
            for r in (dbias_ref, dal_ref, ddk_ref):
                r[...] = jnp.zeros_like(r)

        for j in reversed(range(per)):
            rs = pl.ds(j * CHUNK, CHUNK)
            live = (sidx(s) * per + j) * CHUNK + lax.broadcasted_iota(jnp.int32, (CHUNK, 1), 0) >= FRONT
            dxs, dbm, dcm, ddt, dprev, dbias, dal, ddk = _ssd_scan_bwd(
                xs_ref[rs, :].astype(F32), b_ref[rs, :].astype(F32), c_ref[rs, :].astype(F32), dt_ref[rs, :], st_ref[j, 0],
                bias_ref[...], al_ref[...], dk_ref[...], live, dy_ref[rs, :].astype(F32), carry[...])
            dxs_ref[rs, :] = dxs.astype(ACT_DTYPE)
            db_ref[rs, :] = dbm.astype(ACT_DTYPE)
            dc_ref[rs, :] = dcm.astype(ACT_DTYPE)
            ddt_ref[rs, :] = ddt.astype(ACT_DTYPE)
            carry[...] = dprev
            dbias_ref[...] += dbias
            dal_ref[...] += dal
            ddk_ref[...] += ddk

    bc = jax.ShapeDtypeStruct((rows, SSD_GROUPS * SSD_STATE), ACT_DTYPE)
    head = jax.ShapeDtypeStruct((1, SSD_GROUPS * LANES), F32)
    return pl.pallas_call(
        body, name="ssd_bwd", grid=(SSD_GROUPS, steps),
        in_specs=xbc + [row(LANES), par(LANES), par(LANES), par(LANES), state(), row(SSD_GW), pl.BlockSpec(memory_space=pl.ANY)],
        out_specs=[row(SSD_GW), row(SSD_STATE), row(SSD_STATE), row(LANES, lambda g: col0 // LANES + g),
                   par(LANES), par(LANES), par(LANES)],
        out_shape=[jax.ShapeDtypeStruct((rows, SSD_D_INNER), ACT_DTYPE), bc, bc,
                   jax.ShapeDtypeStruct(dproj.shape, dproj.dtype), head, head, head],
        input_output_aliases={9: 3}, scratch_shapes=[pltpu.VMEM((SSD_STATE, SSD_GW), F32)],
        compiler_params=_params(("parallel", "arbitrary")),
    )(xbc_c, xbc_c, xbc_c, dtr, dt_bias, a_log, d_skip, states, dy, dproj)


def _rotary_tables(rows):
    pos = np.arange(rows, dtype=np.float32) - np.float32(FRONT)
    inv_freq = np.float32(ROPE_BASE) ** (-np.linspace(0.0, 1.0, RET_QK // 2, dtype=np.float32))
    ang = (pos[:, None] * inv_freq[None, :]).astype(np.float32).astype(np.float64)
    lgam = np.log(1.0 - 2.0 ** (-5.0 - np.arange(RET_HEADS, dtype=np.float64))).astype(np.float32)
    lgam = np.broadcast_to(lgam[:, None, None], (RET_HEADS, 8, LANES))
    return jnp.asarray(np.cos(ang).astype(np.float32)), jnp.asarray(np.sin(ang).astype(np.float32)), jnp.asarray(lgam)


def _rotary(t, cos, sin):
    half = t.shape[-1] // 2
    t1, t2 = t[:, :half], t[:, half:]
    return jnp.concatenate([t1 * cos - t2 * sin, t2 * cos + t1 * sin], axis=1)


def _ret_chunk(qh, kh, vh, gh, prev, cos, sin, lg):
    q = CHUNK
    qr = _rotary(qh, cos, sin)
    kr = _rotary(kh, cos, sin) * (RET_QK ** -0.5)
    li = lax.broadcasted_iota(jnp.int32, (q, q), 0)
    si = lax.broadcasted_iota(jnp.int32, (q, q), 1)
    dist = (li - si).astype(F32)
    decay = jnp.exp(jnp.where(li >= si, dist * lg, -jnp.inf))
    idx = lax.broadcasted_iota(jnp.int32, (q, 1), 0).astype(F32)
    scores = _dot(qr, kr, NT) * decay
    out = _dot(scores, vh)
    kv = _dot(kr * jnp.exp((q - 1.0 - idx) * lg), vh, TN)
    out = out + _dot(qr, prev) * jnp.exp((idx + 1.0) * lg)
    new = prev * jnp.exp(q * lg) + kv
    out = out * lax.rsqrt(jnp.mean(out * out, axis=-1, keepdims=True) + EPS)
    return _silu(gh) * out, new


def _ret_specs(rev, nc):
    per = _chunks_per_step(nc)
    steps = nc // per
    sidx = (lambda s: steps - 1 - s) if rev else (lambda s: s)
    row = lambda width: pl.BlockSpec((per * CHUNK, width), lambda h, s: (sidx(s), h))
    tab = lambda: pl.BlockSpec((per * CHUNK, RET_QK // 2), lambda h, s: (sidx(s), 0))
    lgs = lambda: pl.BlockSpec((1, 8, LANES), lambda h, s: (h, 0, 0))
    state = lambda: pl.BlockSpec((per, 1, RET_QK, RET_V), lambda h, s: (sidx(s), h, 0, 0))
    part = lambda width, k: pl.BlockSpec((per * CHUNK, width), lambda h, s: (sidx(s), h * (RET_HW // width) + k))
    ins = [part(RET_QK, 0), part(RET_QK, 1), part(RET_V, 1), part(RET_V, 2), tab(), tab(), lgs()]
    return per, steps, sidx, row, state, ins


def _ret_fwd(qkvg, cos, sin, lgam):
    rows = qkvg.shape[0]
    nc = rows // CHUNK
    per, steps, _, row, state, ins = _ret_specs(False, nc)
    q = k = v = g = qkvg

    def body(q_ref, k_ref, v_ref, g_ref, cos_ref, sin_ref, lg_ref, y_ref, st_ref, carry):
        @pl.when(pl.program_id(1) == 0)
        def _():
            carry[...] = jnp.zeros_like(carry)

        for j in range(per):
            rs = pl.ds(j * CHUNK, CHUNK)
            prev = carry[...]
            st_ref[j, 0] = prev.astype(ACT_DTYPE)
            out, new = _ret_chunk(q_ref[rs, :].astype(F32), k_ref[rs, :].astype(F32), v_ref[rs, :].astype(F32),
                                  g_ref[rs, :].astype(F32), prev, cos_ref[rs, :], sin_ref[rs, :], lg_ref[0, 0:1, 0:1])
            y_ref[rs, :] = out.astype(ACT_DTYPE)
            carry[...] = new

    return pl.pallas_call(
        body, name="ret_fwd", grid=(RET_HEADS, steps), in_specs=ins, out_specs=[row(RET_V), state()],
        out_shape=[jax.ShapeDtypeStruct((rows, RET_HEADS * RET_V), ACT_DTYPE),
                   jax.ShapeDtypeStruct((nc, RET_HEADS, RET_QK, RET_V), ACT_DTYPE)],
        scratch_shapes=[pltpu.VMEM((RET_QK, RET_V), F32)],
        compiler_params=_params(("parallel", "arbitrary")),
    )(q, k, v, g, cos, sin, lgam)


def _ret_bwd(qkvg, cos, sin, lgam, states, dy, dproj):
    rows = qkvg.shape[0]
    nc = rows // CHUNK
    per, steps, sidx, row, state, ins = _ret_specs(True, nc)

    def body(q_ref, k_ref, v_ref, g_ref, cos_ref, sin_ref, lg_ref, st_ref, dy_ref, dp_ref, o_ref, carry):
        @pl.when(pl.program_id(1) == 0)
        def _():
            carry[...] = jnp.zeros_like(carry)

        for j in reversed(range(per)):
            rs = pl.ds(j * CHUNK, CHUNK)
            fn = functools.partial(_ret_chunk, cos=cos_ref[rs, :], sin=sin_ref[rs, :], lg=lg_ref[0, 0:1, 0:1])
            _, vjp = jax.vjp(fn, q_ref[rs, :].astype(F32), k_ref[rs, :].astype(F32), v_ref[rs, :].astype(F32),
                             g_ref[rs, :].astype(F32), st_ref[j, 0].astype(F32))
            dq, dk, dv, dg, dprev = vjp((dy_ref[rs, :].astype(F32), carry[...]))
            o_ref[rs, 0:RET_QK] = dq.astype(ACT_DTYPE)
            o_ref[rs, RET_QK:2 * RET_QK] = dk.astype(ACT_DTYPE)
            o_ref[rs, 2 * RET_QK:2 * RET_QK + RET_V] = dv.astype(ACT_DTYPE)
            o_ref[rs, 2 * RET_QK + RET_V:RET_HW] = dg.astype(ACT_DTYPE)
            carry[...] = dprev

    return pl.pallas_call(
        body, name="ret_bwd", grid=(RET_HEADS, steps),
        in_specs=ins + [state(), row(RET_V), pl.BlockSpec(memory_space=pl.ANY)],
        out_specs=pl.BlockSpec((per * CHUNK, RET_HW), lambda h, s: (sidx(s), h)),
        out_shape=jax.ShapeDtypeStruct(dproj.shape, dproj.dtype), input_output_aliases={9: 0},
        scratch_shapes=[pltpu.VMEM((RET_QK, RET_V), F32)],
        compiler_params=_params(("parallel", "arbitrary")),
    )(qkvg, qkvg, qkvg, qkvg, cos, sin, lgam, states, dy, dproj)


def _merge_fwd(bs, br, gates):
    rows, d = bs.shape
    tm = _row_tile(rows)

    def body(bs_ref, br_ref, gs_ref, gr_ref, o_ref):
        o_ref[...] = (jax.nn.sigmoid(gs_ref[...].astype(F32)) * bs_ref[...].astype(F32)
                      + jax.nn.sigmoid(gr_ref[...].astype(F32)) * br_ref[...].astype(F32)).astype(ACT_DTYPE)

    spec = pl.BlockSpec((tm, d), lambda i: (i, 0))
    return pl.pallas_call(
        body, name="merge_fwd", grid=(rows // tm,), in_specs=[spec, spec, spec, pl.BlockSpec((tm, d), lambda i: (i, 1))],
        out_specs=spec, out_shape=jax.ShapeDtypeStruct((rows, d), ACT_DTYPE), compiler_params=_params(("parallel",)),
    )(bs, br, gates, gates)


def _merge_bwd(dm, bs, br, gates, dproj, col0):
    rows, d = bs.shape
    tm = _row_tile(rows)
    assert col0 % (2 * d) == 0

    def body(dm_ref, bs_ref, br_ref, gs_ref, gr_ref, dp_ref, dbs_ref, dbr_ref, dg_ref):
        dmv = dm_ref[...].astype(F32)
        for k, (b_ref, g_ref, db_ref) in enumerate(((bs_ref, gs_ref, dbs_ref), (br_ref, gr_ref, dbr_ref))):
            s = jax.nn.sigmoid(g_ref[...].astype(F32))
            db_ref[...] = (dmv * s).astype(ACT_DTYPE)
            dg_ref[:, k * d:(k + 1) * d] = (dmv * b_ref[...].astype(F32) * s * (1.0 - s)).astype(ACT_DTYPE)

    spec = pl.BlockSpec((tm, d), lambda i: (i, 0))
    shp = jax.ShapeDtypeStruct((rows, d), ACT_DTYPE)
    return pl.pallas_call(
        body, name="merge_bwd", grid=(rows // tm,),
        in_specs=[spec, spec, spec, spec, pl.BlockSpec((tm, d), lambda i: (i, 1)), pl.BlockSpec(memory_space=pl.ANY)],
        out_specs=[spec, spec, pl.BlockSpec((tm, 2 * d), lambda i: (i, col0 // (2 * d)))],
        out_shape=[shp, shp, jax.ShapeDtypeStruct(dproj.shape, dproj.dtype)], input_output_aliases={5: 2},
        compiler_params=_params(("parallel",)),
    )(dm, bs, br, gates, gates, dproj)


def _place():
    x, y, c = lax.axis_index("x"), lax.axis_index("y"), lax.axis_index("c")
    return x, y, c


def _slot(p):
    return 4 * p[0] + 2 * p[1] + p[2]


def _allgather(arrs, name):
    n = len(arrs)
    any_spec = pl.BlockSpec(memory_space=pl.ANY)

    def body(*refs):
        ins, outs = refs[:n], refs[n:2 * n]
        send_sems, recv_sems, local_sems = refs[2 * n:]
        x, y, c = _place()
        me, sibling = (x, y, c), (x, y, 1 - c)
        chips = [(1 - x, y), (x, 1 - y), (1 - x, 1 - y)]

        def copy(a, k, block, to, src=None):
            dst = outs[a].at[_slot(block)]
            return pltpu.make_async_remote_copy(
                src_ref=dst if src is None else src, dst_ref=dst, send_sem=send_sems.at[a * 7 + k],
                recv_sem=recv_sems.at[a * 7 + k], device_id=to, device_id_type=MESH)

        mine, first, passed = [], [], []
        for a in range(n):
            cp = pltpu.make_async_copy(ins[a], outs[a].at[_slot(me)], local_sems.at[a])
            cp.start()
            mine.append(cp)
            first.append(copy(a, 0, me, sibling, src=ins[a]))
            first += [copy(a, 1 + j, me, (*chip, c), src=ins[a]) for j, chip in enumerate(chips)]
        for cp in first:
            cp.start()
        for j, chip in enumerate(chips):
            for a in range(n):
                copy(a, 1 + j, (*chip, c), me).wait_recv()
                cp = copy(a, 4 + j, (*chip, c), sibling)
                cp.start()
                passed.append(cp)
        for a in range(n):
            copy(a, 0, sibling, me).wait_recv()
            for j, chip in enumerate(chips):
                copy(a, 4 + j, (*chip, 1 - c), me).wait_recv()
        for cp in first + passed:
            cp.wait_send()
        for cp in mine:
            cp.wait()

    return pl.pallas_call(
        body, name=name, in_specs=[any_spec] * n, out_specs=[any_spec] * n,
        out_shape=[jax.ShapeDtypeStruct((N_DEV,) + a.shape, a.dtype) for a in arrs],
        scratch_shapes=[pltpu.SemaphoreType.DMA((7 * n,)), pltpu.SemaphoreType.DMA((7 * n,)), pltpu.SemaphoreType.DMA((n,))],
    )(*arrs)


def _peers():
    x, y, c = _place()
    return (x, y, c), [(x ^ dx, y ^ dy, c ^ dc) for dx in (0, 1) for dy in (0, 1) for dc in (0, 1)][1:]


def _exchange_copies(srcs, lands, send_sems, recv_sems, scatter, sender):
    me, peers = _peers()
    out = []
    for a, (src, land) in enumerate(zip(srcs, lands, strict=True)):
        for k, peer in enumerate(peers):
            src_ref = src.at[_slot(peer)] if scatter else src
            out.append(pltpu.make_async_remote_copy(
                src_ref=src_ref, dst_ref=land.at[_slot(me if sender else peer)], send_sem=send_sems.at[a * 7 + k],
                recv_sem=recv_sems.at[a * 7 + k], device_id=peer, device_id_type=MESH))
    return out


_HBM = pl.BlockSpec(memory_space=pltpu.HBM)
_SEM = pl.BlockSpec(memory_space=pltpu.SEMAPHORE)
_EFFECT = pltpu.SideEffectType.DATAFLOW_SIDE_EFFECTING


def _exchange_start(srcs, scatter, name, after=None):
    n = len(srcs)
    land_shapes = [s.shape if scatter else (N_DEV,) + s.shape for s in srcs]
    n_in = 2 * n + (after is not None)

    def body(*refs):
        for cp in _exchange_copies(refs[:n], refs[n:2 * n], refs[n_in], refs[n_in + 1], scatter, True):
            cp.start()
        refs[-1][...] = jnp.zeros_like(refs[-1])

    args = [pltpu.with_memory_space_constraint(s, pltpu.HBM) for s in srcs]
    args += [pltpu.with_memory_space_constraint(lax.empty(shp, s.dtype), pltpu.HBM) for s, shp in zip(srcs, land_shapes)]
    thru_shapes = tuple(pltpu.HBM(a.shape, a.dtype) for a in args)
    extra = [] if after is None else [after]
    outs = pl.pallas_call(
        body, name=name,
        out_shape=(pltpu.SemaphoreType.DMA((7 * n,)), pltpu.SemaphoreType.DMA((7 * n,))) + thru_shapes
        + (jax.ShapeDtypeStruct((8, LANES), F32),),
        in_specs=[_HBM] * (2 * n) + [pl.BlockSpec(memory_space=pl.ANY)] * len(extra),
        out_specs=(_SEM, _SEM) + (_HBM,) * (2 * n) + (pl.BlockSpec(memory_space=pltpu.VMEM),),
        input_output_aliases={i: 2 + i for i in range(2 * n)},
        compiler_params=pltpu.CompilerParams(has_side_effects=_EFFECT),
    )(*args, *extra)
    return outs[:-1], outs[-1]


def _exchange_wait(handle, scatter, after, name):
    n = (len(handle) - 2) // 2
    thru = handle[2:]

    def body(*refs):
        for cp in _exchange_copies(refs[:n], refs[n:2 * n], refs[2 * n], refs[2 * n + 1], scatter, False):
            cp.wait_send()
            cp.wait_recv()

    outs = pl.pallas_call(
        body, name=name, out_shape=tuple(pltpu.HBM(t.shape, t.dtype) for t in thru),
        in_specs=[_HBM] * (2 * n) + [_SEM, _SEM, pl.BlockSpec(memory_space=pl.ANY)], out_specs=(_HBM,) * (2 * n),
        input_output_aliases={i: i for i in range(2 * n)},
        compiler_params=pltpu.CompilerParams(has_side_effects=_EFFECT),
    )(*thru, handle[0], handle[1], after)
    return list(outs[:n]), list(outs[n:])


def _allreduce_small(pack):
    rows, lanes = pack.shape

    def body(x_ref, o_ref, buf, send_sems, recv_sems):
        x, y, c = _place()
        me, sibling = (x, y, c), (x, y, 1 - c)
        chips = [(1 - x, y), (x, 1 - y), (1 - x, 1 - y)]

        def copy(k, block, to, src=None):
            dst = buf.at[_slot(block)]
            return pltpu.make_async_remote_copy(
                src_ref=dst if src is None else src, dst_ref=dst, send_sem=send_sems.at[k], recv_sem=recv_sems.at[k],
                device_id=to, device_id_type=MESH)

        buf[_slot(me)] = x_ref[...]
        first = [copy(0, me, sibling, src=x_ref)]
        first += [copy(1 + j, me, (*chip, c), src=x_ref) for j, chip in enumerate(chips)]
        for cp in first:
            cp.start()
        passed = [copy(4 + j, (*chip, c), sibling) for j, chip in enumerate(chips)]
        for j, chip in enumerate(chips):
            copy(1 + j, (*chip, c), me).wait_recv()
            passed[j].start()
        copy(0, sibling, me).wait_recv()
        for j, chip in enumerate(chips):
            copy(4 + j, (*chip, 1 - c), me).wait_recv()
        for cp in first + passed:
            cp.wait_send()
        acc = buf[0]
        for i in range(1, N_DEV):
            acc = acc + buf[i]
        o_ref[...] = acc

    vmem = pl.BlockSpec(memory_space=pltpu.VMEM)
    return pl.pallas_call(
        body, name="allreduce_small", in_specs=[vmem], out_specs=vmem,
        out_shape=jax.ShapeDtypeStruct((rows, lanes), F32),
        scratch_shapes=[pltpu.VMEM((N_DEV, rows, lanes), F32), pltpu.SemaphoreType.DMA((7,)), pltpu.SemaphoreType.DMA((7,))],
        compiler_params=pltpu.CompilerParams(vmem_limit_bytes=VMEM_LIMIT),
    )(pack)


def _adamw(w, g, m, v):
    m = ADAM_B1 * m + (1.0 - ADAM_B1) * g
    v = ADAM_B2 * v + (1.0 - ADAM_B2) * jnp.square(g)
    m_hat = m / (1.0 - ADAM_B1 ** ADAM_STEP)
    v_hat = v / (1.0 - ADAM_B2 ** ADAM_STEP)
    delta = -ADAM_LR * (m_hat / (jnp.sqrt(v_hat) + ADAM_EPS) + ADAM_WD * w)
    return delta, m, v


def _adam_shard(own, parts, w, m, v, name):
    r, c = w.shape
    tr = _pick(r, (128, 64, 32, 16, 8))

    def body(own_ref, p_ref, w_ref, m_ref, v_ref, g_ref, d_ref, nm_ref, nv_ref):
        _, peers = _peers()
        g = own_ref[...].astype(F32)
        for peer in peers:
            g = g + p_ref[_slot(peer)].astype(F32)
        g_ref[...] = g
        d_ref[...], nm_ref[...], nv_ref[...] = _adamw(w_ref[...], g, m_ref[...], v_ref[...])

    spec = pl.BlockSpec((tr, c), lambda i: (i, 0))
    shp = jax.ShapeDtypeStruct((r, c), F32)
    return pl.pallas_call(
        body, name=name, grid=(r // tr,),
        in_specs=[spec, pl.BlockSpec((N_DEV, tr, c), lambda i: (0, i, 0)), spec, spec, spec], out_specs=[spec] * 4,
        out_shape=[shp] * 4, compiler_params=_params(("parallel",)),
    )(own, parts, w, m, v)


def _adam_small(w, g, m, v):
    r, c = w.shape

    def body(w_ref, g_ref, m_ref, v_ref, d_ref, nm_ref, nv_ref):
        d_ref[...], nm_ref[...], nv_ref[...] = _adamw(w_ref[...], g_ref[...], m_ref[...], v_ref[...])

    shp = jax.ShapeDtypeStruct((r, c), F32)
    return pl.pallas_call(body, name="adam_small", out_shape=[shp] * 3)(w, g, m, v)


def _column_plan(pieces, shard_w):
    plan = []
    for c0, width, d0 in pieces:
        c = c0
        while c < c0 + width:
            s, a = divmod(c, shard_w)
            w = min(c0 + width - c, shard_w - a)
            plan.append((s, a, w, d0 + c - c0))
            c += w
    return plan


def _cols_from_shards(g, plan, out_w, zero, name):
    _, r, sw = g.shape
    tr = _pick(r, (128,))

    def body(x_ref, o_ref):
        for d0, w in zero:
            o_ref[:, d0:d0 + w] = jnp.zeros((tr, w), g.dtype)
        for s, a, w, d0 in plan:
            o_ref[:, d0:d0 + w] = x_ref[s, :, a:a + w]

    return pl.pallas_call(
        body, name=name, grid=(r // tr,), in_specs=[pl.BlockSpec((N_DEV, tr, sw), lambda i: (0, i, 0))],
        out_specs=pl.BlockSpec((tr, out_w), lambda i: (i, 0)), out_shape=jax.ShapeDtypeStruct((r, out_w), g.dtype),
        compiler_params=_params(("parallel",)),
    )(g)


def _shards_from_cols(srcs, plans, shard_w, name):
    r = srcs[0].shape[0]
    tr = _pick(r, (128,))
    n = len(srcs)

    def body(*refs):
        o_ref = refs[n]
        for x_ref, plan in zip(refs[:n], plans, strict=True):
            for s, a, w, d0 in plan:
                o_ref[s, :, a:a + w] = x_ref[:, d0:d0 + w].astype(COMM_DTYPE)

    return pl.pallas_call(
        body, name=name, grid=(r // tr,), in_specs=[pl.BlockSpec((tr, t.shape[1]), lambda i: (i, 0)) for t in srcs],
        out_specs=pl.BlockSpec((N_DEV, tr, shard_w), lambda i: (0, i, 0)),
        out_shape=jax.ShapeDtypeStruct((N_DEV, r, shard_w), COMM_DTYPE), compiler_params=_params(("parallel",)),
    )(*srcs)


def _pack(arrs):
    rows = []
    for a in arrs:
        flat = a.reshape(-1).astype(F32)
        rows.append(jnp.pad(flat, (0, (-flat.shape[0]) % (8 * LANES))).reshape(-1, LANES))
    return jnp.concatenate(rows, axis=0)


def _unpack(pack, shapes):
    out, r = [], 0
    for s in shapes:
        size = math.prod(s)
        nr = -(-size // (8 * LANES)) * 8
        out.append(pack[r:r + nr].reshape(-1)[:size].reshape(s))
        r += nr
    return out


def _group_lanes(t):
    lead = t.shape[:-1]
    t = t.reshape(lead + (SSD_GROUPS, SSD_HPG))
    t = jnp.pad(t, [(0, 0)] * len(lead) + [(0, 0), (0, LANES - SSD_HPG)])
    return t.reshape(lead + (SSD_GROUPS * LANES,))


def _ungroup_lanes(t):
    lead = t.shape[:-1]
    return t.reshape(lead + (SSD_GROUPS, LANES))[..., :SSD_HPG].reshape(lead + (SSD_HEADS,))


def kernel(x, meta_tokens, mix_norm_w, w_in, ssd_conv_w, ssd_conv_b, ssd_dt_bias, ssd_A_log, ssd_D, ssd_norm_w, w_branch_ssd, w_branch_ret, w_out, ffn_norm_w, w_up, ffn_conv_w, ffn_conv_b, w_down, final_norm_w, loss_target, m_meta_tokens, m_mix_norm_w, m_w_in, m_ssd_conv_w, m_ssd_conv_b, m_ssd_dt_bias, m_ssd_A_log, m_ssd_D, m_ssd_norm_w, m_w_branch_ssd, m_w_branch_ret, m_w_out, m_ffn_norm_w, m_w_up, m_ffn_conv_w, m_ffn_conv_b, m_w_down, m_final_norm_w, v_meta_tokens, v_mix_norm_w, v_w_in, v_ssd_conv_w, v_ssd_conv_b, v_ssd_dt_bias, v_ssd_A_log, v_ssd_D, v_ssd_norm_w, v_w_branch_ssd, v_w_branch_ret, v_w_out, v_ffn_norm_w, v_w_up, v_ffn_conv_w, v_ffn_conv_b, v_w_down, v_final_norm_w):
    seq, d = x.shape[1], x.shape[2]
    rows = seq + PAD_ROWS
    tm = _row_tile(rows)
    me = _slot(_place())
    d_ff = w_down.shape[1] * N_DEV

    big = [w_in[0], w_branch_ssd[0], w_branch_ret[0], w_out[0], w_up[0], w_down[0]]
    first = _allgather([w_in[0].astype(COMM_DTYPE), meta_tokens, ssd_conv_w[0], ffn_conv_w[0]], "gather_first")
    rest_src = [b.astype(COMM_DTYPE) for b in big[1:]]
    rest_handle, rest_token = _exchange_start(rest_src, False, "gather_rest_start", after=first[0])
    cols = lambda t: jnp.transpose(t, (1, 0, 2)).reshape(t.shape[1], -1)
    rws = lambda t: t.reshape(-1, t.shape[2])
    conv_w, fconv_w = cols(first[2]), cols(first[3])
    meta_full = cols(first[1]) + rest_token[0, 0]
    widths = [SSD_D_INNER, SSD_CONV_DIM, SSD_HEADS, RET_HEADS * RET_QK, RET_HEADS * RET_QK, RET_HEADS * RET_V,
              RET_HEADS * RET_V, d, d]
    offs = [0]
    for wd in widths:
        offs.append(offs[-1] + wd)
    r0, z0 = 0, RET_HEADS * RET_HW
    g0 = z0 + widths[0]
    x0 = g0 + 2 * d
    dt0 = x0 + widths[1]
    in_p = dt0 + SSD_GROUPS * LANES
    pieces = []
    for hd in range(RET_HEADS):
        base = r0 + hd * RET_HW
        pieces += [(offs[3] + hd * RET_QK, RET_QK, base), (offs[4] + hd * RET_QK, RET_QK, base + RET_QK),
                   (offs[5] + hd * RET_V, RET_V, base + 2 * RET_QK), (offs[6] + hd * RET_V, RET_V, base + 2 * RET_QK + RET_V)]
    pieces += [(offs[0], widths[0], z0), (offs[7], d, g0), (offs[8], d, g0 + d), (offs[1], widths[1], x0)]
    pieces += [(offs[2] + SSD_HPG * grp, SSD_HPG, dt0 + LANES * grp) for grp in range(SSD_GROUPS)]
    in_plan = _column_plan(pieces, w_in.shape[2])
    w_in_p = _cols_from_shards(first[0], in_plan, in_p, [(dt0, SSD_GROUPS * LANES)], "w_in_columns")

    h0 = jnp.concatenate([jnp.zeros((FRONT, d), F32), meta_full, x[0]], axis=0)
    u1 = _rms_fwd(h0, mix_norm_w, "rms1")
    in_proj = lambda c0, width, dtype, nm: _mm(u1, w_in_p, mode="nn", out_dtype=dtype, tm=tm, tk=d, name="in_proj_" + nm,
                                               tn=_pick(width, (1024, 512)), b_n0=c0, n_out=width)
    qkvg = in_proj(r0, RET_HEADS * RET_HW, ACT_DTYPE, "qkvg")
    z = in_proj(z0, widths[0], ACT_DTYPE, "z")
    gates = in_proj(g0, 2 * d, ACT_DTYPE, "gates")
    dtr = in_proj(dt0, SSD_GROUPS * LANES, F32, "dt")
    xbc, xbc_c = _xbc_proj_conv(u1, w_in_p, x0, conv_w, ssd_conv_b)
    bias_p, alog_p, dsk_p = _group_lanes(ssd_dt_bias), _group_lanes(ssd_A_log), _group_lanes(ssd_D)
    y_ssd, y_scan, ssd_states = _ssd_fwd(xbc_c, dtr, z, bias_p, alog_p, dsk_p, ssd_norm_w)
    cos, sin, lgam = _rotary_tables(rows)
    y_ret, ret_states = _ret_fwd(qkvg, cos, sin, lgam)
    rest_own, rest = _exchange_wait(rest_handle, False, y_ret, "gather_rest_wait")
    rest = [lax.dynamic_update_index_in_dim(land, own, me, 0) for land, own in zip(rest, rest_own, strict=True)]
    w_bs, w_br, w_o, w_dn = rws(rest[0]), rws(rest[1]), rws(rest[2]), rws(rest[4])
    w_up_f = _cols_from_shards(rest[3], _column_plan([(0, 2 * d_ff, 0)], w_up.shape[2]), 2 * d_ff, [], "w_up_columns")
    bs = _mm(y_ssd, w_bs, mode="nn", out_dtype=ACT_DTYPE, tm=tm, tn=d, tk=SSD_D_INNER, name="branch_ssd")
    br = _mm(y_ret, w_br, mode="nn", out_dtype=ACT_DTYPE, tm=tm, tn=d, tk=RET_HEADS * RET_V, name="branch_ret")
    merged = _merge_fwd(bs, br, gates)
    h1 = _mm(merged, w_o, mode="nn", out_dtype=F32, tm=tm, tn=d, tk=d, name="out_proj", add=h0)
    u2 = _rms_fwd(h1, ffn_norm_w, "rms2")
    up_g, up_v, act = _ffn_up_conv(u2, w_up_f, fconv_w, ffn_conv_b)
    h2 = _mm(act, w_dn, mode="nn", out_dtype=F32, tm=tm, tn=d, tk=d_ff, name="ffn_down", add=h1)
    tgt = jnp.pad(loss_target[0], ((PAD_ROWS, 0), (0, 0)))
    dh2, loss_acc, g_final = _loss_head(h2, tgt, final_norm_w.reshape(1, d))

    tff = _pick(d_ff, (1408, 256))
    tkr = _pick(rows, (1664, 128))
    tkr2 = _pick(rows, (4160, 128))
    rparts = lambda t: t.reshape(N_DEV, -1, t.shape[1])
    d_act = _mm(dh2, w_dn, mode="nt", out_dtype=ACT_DTYPE, tm=tm, tn=tff, tk=d, name="d_act")
    g_w_dn = _mm(act, dh2, mode="tn", out_dtype=COMM_DTYPE, tm=tff, tn=d, tk=tkr, name="g_w_down")
    c_dn = [rparts(g_w_dn)]
    h_dn, t_dn = _exchange_start(c_dn, True, "scatter_down_start")
    d_up_g, d_up_v, g_fcw_g, g_fcb_g, g_fcw_v, g_fcb_v, g_w_up_g, g_w_up_v = _ffn_conv_bwd(
        up_g, up_v, d_act, fconv_w, ffn_conv_b + t_dn[0, 0], u2)
    g_fconv_w = jnp.concatenate([g_fcw_g, g_fcw_v], axis=1)
    g_fconv_b = jnp.concatenate([g_fcb_g, g_fcb_v], axis=1)
    c_up = [_shards_from_cols([g_w_up_g, g_w_up_v], [_column_plan([(0, d_ff, 0)], w_up.shape[2]),
                                                     _column_plan([(d_ff, d_ff, 0)], w_up.shape[2])], w_up.shape[2], "g_w_up_shards")]
    h_up, t_up = _exchange_start(c_up, True, "scatter_up_start")
    du2 = _mm(d_up_g, w_up_f, mode="nt", out_dtype=F32, tm=tm, tn=d, tk=d_ff, name="d_u2_gate", after=t_up)
    du2 = _mm(d_up_v, w_up_f, mode="nt", out_dtype=F32, tm=tm, tn=d, tk=d_ff, name="d_u2_value", add=du2, b_k0=d_ff)
    dh1, g_ffn_norm = _rms_bwd(du2, h1, ffn_norm_w, dh2, "rms2_bwd")
    d_merged = _mm(dh1, w_o, mode="nt", out_dtype=F32, tm=tm, tn=d, tk=d, name="d_merged")
    g_w_o = _mm(merged, dh1, mode="tn", out_dtype=COMM_DTYPE, tm=d, tn=d, tk=tkr, name="g_w_out")
    dproj = lax.empty((rows, in_p), ACT_DTYPE)
    d_bs, d_br, dproj = _merge_bwd(d_merged, bs, br, gates, dproj, g0)
    d_yssd = _mm(d_bs, w_bs, mode="nt", out_dtype=ACT_DTYPE, tm=tm, tn=1024, tk=d, name="d_y_ssd")
    g_w_bs = _mm(y_ssd, d_bs, mode="tn", out_dtype=COMM_DTYPE, tm=1024, tn=d, tk=tkr2, name="g_w_branch_ssd")
    d_yret = _mm(d_br, w_br, mode="nt", out_dtype=ACT_DTYPE, tm=tm, tn=1024, tk=d, name="d_y_ret")
    g_w_br = _mm(y_ret, d_br, mode="tn", out_dtype=COMM_DTYPE, tm=1024, tn=d, tk=tkr2, name="g_w_branch_ret")
    c_mid = [rparts(g_w_bs), rparts(g_w_br), rparts(g_w_o)]
    h_mid, t_mid = _exchange_start(c_mid, True, "scatter_mid_start")
    d_yscan, dproj, g_nw = _ssd_gate_bwd(y_scan, z, d_yssd, ssd_norm_w + t_mid[0, 0], dproj, z0)
    dxs, d_bm, d_cm, dproj, g_bias_p, g_alog_p, g_dsk_p = _ssd_bwd(
        xbc_c, dtr, bias_p, alog_p, dsk_p, ssd_states, d_yscan, dproj, dt0)
    dproj, g_conv_w, g_conv_b = _ssd_conv_bwd(xbc, dxs, d_bm, d_cm, conv_w, ssd_conv_b, dproj, x0)
    dproj = _ret_bwd(qkvg, cos, sin, lgam, ret_states, d_yret, dproj)
    g_w_in_p = _mm(u1, dproj, mode="tn", out_dtype=F32, tm=d, tn=_pick(in_p, (768, 512)), tk=tkr2, name="g_w_in")
    c_in = [_shards_from_cols([g_w_in_p], [in_plan], w_in.shape[2], "g_w_in_shards")]
    h_in, t_in = _exchange_start(c_in, True, "scatter_in_start")
    du1 = _mm(dproj, w_in_p, mode="nt", out_dtype=F32, tm=tm, tn=d, tk=_pick(in_p, (4608, 512)), name="d_u1", after=t_in)
    dh0, g_mix_norm = _rms_bwd(du1, h0, mix_norm_w, dh1, "rms1_bwd")
    grad_x = dh0[PAD_ROWS:][None]

    landed = {}
    for key, handle, names in (("in", h_in, ["w_in"]), ("mid", h_mid, ["w_branch_ssd", "w_branch_ret", "w_out"]),
                               ("up", h_up, ["w_up"]), ("down", h_dn, ["w_down"])):
        srcs, lands = _exchange_wait(handle, True, dh0, f"scatter_{key}_wait")
        for nm, land, src in zip(names, lands, srcs, strict=True):
            landed[nm] = (lax.dynamic_index_in_dim(src, me, 0, keepdims=False), land)
    big_m = [m_w_in, m_w_branch_ssd, m_w_branch_ret, m_w_out, m_w_up, m_w_down]
    big_v = [v_w_in, v_w_branch_ssd, v_w_branch_ret, v_w_out, v_w_up, v_w_down]
    big_names = ["w_in", "w_branch_ssd", "w_branch_ret", "w_out", "w_up", "w_down"]
    big_out = {}
    for nm, w, m, v_ in zip(big_names, big, big_m, big_v, strict=True):
        big_out[nm] = [t[None] for t in _adam_shard(*landed[nm], w, m[0], v_[0], "adam_" + nm)]

    small_g = [dh0[FRONT:PAD_ROWS], g_mix_norm, g_conv_w, g_conv_b, _ungroup_lanes(g_bias_p), _ungroup_lanes(g_alog_p),
               _ungroup_lanes(g_dsk_p), g_nw, g_ffn_norm, g_fconv_w, g_fconv_b, g_final, loss_acc[0:1, 0:1]]
    total = _unpack(_allreduce_small(_pack(small_g)), [t.shape for t in small_g])
    loss = total[12].reshape(())
    shard = lambda t, width: lax.dynamic_slice_in_dim(t, me * width, width, axis=1)
    small_names = ["meta_tokens", "mix_norm_w", "ssd_conv_w", "ssd_conv_b", "ssd_dt_bias", "ssd_A_log", "ssd_D", "ssd_norm_w",
                   "ffn_norm_w", "ffn_conv_w", "ffn_conv_b", "final_norm_w"]
    small_w = [meta_tokens, mix_norm_w, ssd_conv_w, ssd_conv_b, ssd_dt_bias, ssd_A_log, ssd_D, ssd_norm_w, ffn_norm_w,
               ffn_conv_w, ffn_conv_b, final_norm_w]
    small_m = [m_meta_tokens, m_mix_norm_w, m_ssd_conv_w, m_ssd_conv_b, m_ssd_dt_bias, m_ssd_A_log, m_ssd_D, m_ssd_norm_w,
               m_ffn_norm_w, m_ffn_conv_w, m_ffn_conv_b, m_final_norm_w]
    small_v = [v_meta_tokens, v_mix_norm_w, v_ssd_conv_w, v_ssd_conv_b, v_ssd_dt_bias, v_ssd_A_log, v_ssd_D, v_ssd_norm_w,
               v_ffn_norm_w, v_ffn_conv_w, v_ffn_conv_b, v_final_norm_w]
    grads = total[:12]
    grads[0] = shard(grads[0], meta_tokens.shape[1])
    grads[2] = shard(grads[2], ssd_conv_w.shape[2])
    grads[9] = shard(grads[9], ffn_conv_w.shape[2])
    grads = [t.reshape(w.shape) for t, w in zip(grads, small_w, strict=True)]
    shapes = [w.shape for w in small_w]
    upd = _adam_small(_pack(small_w), _pack(grads), _pack(small_m), _pack(small_v))
    small_out = {nm: [gr_] + [u[i] for u in (_unpack(t, shapes) for t in upd)]
                 for i, (nm, gr_) in enumerate(zip(small_names, grads, strict=True))}

    order = ["meta_tokens", "mix_norm_w", "w_in", "ssd_conv_w", "ssd_conv_b", "ssd_dt_bias", "ssd_A_log", "ssd_D", "ssd_norm_w",
             "w_branch_ssd", "w_branch_ret", "w_out", "ffn_norm_w", "w_up", "ffn_conv_w", "ffn_conv_b", "w_down", "final_norm_w"]
    res = {**big_out, **small_out}
    return (loss, grad_x, *[res[nm][0] for nm in order], *[res[nm][1] for nm in order], *[res[nm][2] for nm in order],
            *[res[nm][3] for nm in order])
```

```python
import functools
import math

import jax
import jax.numpy as jnp
import numpy as np
from jax import lax
from jax.experimental import pallas as pl
from jax.experimental.pallas import tpu as pltpu

F32 = jnp.float32
MXU_DTYPE = jnp.bfloat16
ACT_DTYPE = jnp.bfloat16
COMM_DTYPE = jnp.bfloat16

N_META = 16
CHUNK = 128
FRONT = CHUNK - N_META
PAD_ROWS = FRONT + N_META
EPS = 1e-6
N_DEV = 8

SSD_D_INNER = 2048
SSD_HEAD_DIM = 64
SSD_HEADS = 32
SSD_GROUPS = 4
SSD_HPG = SSD_HEADS // SSD_GROUPS
SSD_STATE = 128
SSD_CONV = 4
SSD_CONV_DIM = SSD_D_INNER + 2 * SSD_GROUPS * SSD_STATE
SSD_GW = SSD_D_INNER // SSD_GROUPS
RET_HEADS = 4
RET_QK = 256
RET_V = 512
RET_HW = 2 * RET_QK + 2 * RET_V
ROPE_BASE = 10000.0
FFN_CONV = 3
HALO = 16
LANES = 128

ADAM_LR = 0.001
ADAM_B1 = 0.9
ADAM_B2 = 0.999
ADAM_EPS = 1e-08
ADAM_WD = 0.01
ADAM_STEP = 10

VMEM_LIMIT = 56 * 1024 * 1024
MESH = pl.DeviceIdType.MESH

NN = (((1,), (0,)), ((), ()))
NT = (((1,), (1,)), ((), ()))
TN = (((0,), (0,)), ((), ()))


def _params(sem):
    return pltpu.CompilerParams(dimension_semantics=sem, vmem_limit_bytes=VMEM_LIMIT)


def _mxu(a, b, dn):
    return lax.dot_general(a.astype(MXU_DTYPE), b.astype(MXU_DTYPE), dn, preferred_element_type=F32)


@functools.partial(jax.custom_vjp, nondiff_argnums=(2,))
def _dot(a, b, dn=NN):
    return _mxu(a, b, dn)


def _dot_fwd(a, b, dn):
    return _mxu(a, b, dn), (a, b)


def _dot_bwd(dn, res, g):
    a, b = res
    if dn == NN:
        return _mxu(g, b, NT), _mxu(a, g, TN)
    if dn == NT:
        return _mxu(g, b, NN), _mxu(g, a, TN)
    assert dn == TN
    return _mxu(b, g, NT), _mxu(a, g, NN)


_dot.defvjp(_dot_fwd, _dot_bwd)


def _silu(x):
    return x * jax.nn.sigmoid(x)


def _dsilu(x):
    s = jax.nn.sigmoid(x)
    return s * (1.0 + x * (1.0 - s))


def _row_tile(rows):
    return 640 if rows % 640 == 0 else 128


def _mm(a, b, *, mode, out_dtype, tm, tn, tk, name, add=None, after=None, b_k0=0, b_n0=0, n_out=None):
    if mode == "nn":
        (m, k), k2 = a.shape, b.shape[0]
        n = b.shape[1] if n_out is None else n_out
        assert b_n0 % tn == 0 and b_n0 + n <= b.shape[1]
    elif mode == "nt":
        (m, k), n = a.shape, b.shape[0]
        k2 = k if b_k0 % tk == 0 and b_k0 + k <= b.shape[1] else None
    else:
        (k, m), (k2, n) = a.shape, b.shape
    assert (b_k0 == 0 or mode == "nt") and ((b_n0 == 0 and n_out is None) or mode == "nn")
    assert k == k2 and m % tm == 0 and n % tn == 0 and k % tk == 0, (name, a.shape, b.shape, tm, tn, tk)
    kb0, nb0 = b_k0 // tk, b_n0 // tn
    nk = k // tk
    dn = {"nn": NN, "nt": NT, "tn": TN}[mode]
    has_add = add is not None
    n_in = 2 + has_add + (after is not None)

    def body(*refs):
        a_ref, b_ref = refs[0], refs[1]
        add_ref = refs[2] if has_add else None
        o_ref = refs[n_in]
        p = _dot(a_ref[...], b_ref[...], dn)
        if nk == 1:
            if has_add:
                p = p + add_ref[...]
            o_ref[...] = p.astype(out_dtype)
        else:
            acc_ref = refs[n_in + 1]
            kk = pl.program_id(2)

            @pl.when(kk == 0)
            def _():
                acc_ref[...] = p

            @pl.when(kk > 0)
            def _():
                acc_ref[...] += p

            @pl.when(kk == nk - 1)
            def _():
                r = acc_ref[...]
                if has_add:
                    r = r + add_ref[...]
                o_ref[...] = r.astype(out_dtype)

    if mode == "tn":
        a_spec = pl.BlockSpec((tk, tm), lambda j, i, kk: (kk, i))
    else:
        a_spec = pl.BlockSpec((tm, tk), lambda j, i, kk: (i, kk))
    if mode == "nt":
        b_spec = pl.BlockSpec((tn, tk), lambda j, i, kk: (j, kk + kb0))
    else:
        b_spec = pl.BlockSpec((tk, tn), lambda j, i, kk: (kk, j + nb0))
    o_spec = pl.BlockSpec((tm, tn), lambda j, i, kk: (i, j))
    in_specs = [a_spec, b_spec] + ([o_spec] if has_add else [])
    args = (a, b) + ((add,) if has_add else ())
    if after is not None:
        in_specs.append(pl.BlockSpec(memory_space=pl.ANY))
        args += (after,)
    return pl.pallas_call(
        body, name=name, grid=(n // tn, m // tm, nk), in_specs=in_specs, out_specs=o_spec,
        out_shape=jax.ShapeDtypeStruct((m, n), out_dtype),
        scratch_shapes=[pltpu.VMEM((tm, tn), F32)] if nk > 1 else [],
        compiler_params=_params(("parallel", "parallel", "arbitrary")),
    )(*args)


def _pick(n, cands):
    for c in cands:
        if n % c == 0:
            return c
    return n


def _rms_fwd(h, w, name):
    rows, d = h.shape
    tm = _row_tile(rows)

    def body(h_ref, w_ref, u_ref):
        x = h_ref[...]
        r = lax.rsqrt(jnp.mean(x * x, axis=-1, keepdims=True) + EPS)
        u_ref[...] = (x * r * w_ref[...]).astype(ACT_DTYPE)

    return pl.pallas_call(
        body, name=name, grid=(rows // tm,),
        in_specs=[pl.BlockSpec((tm, d), lambda i: (i, 0)), pl.BlockSpec((1, d), lambda i: (0, 0))],
        out_specs=pl.BlockSpec((tm, d), lambda i: (i, 0)),
        out_shape=jax.ShapeDtypeStruct((rows, d), ACT_DTYPE),
        compiler_params=_params(("parallel",)),
    )(h, w)


def _rms_bwd(du, h, w, dres, name):
    rows, d = h.shape
    tm = _row_tile(rows)

    def body(du_ref, h_ref, w_ref, dres_ref, dh_ref, dw_ref):
        x = h_ref[...]
        dy = du_ref[...].astype(F32)
        r = lax.rsqrt(jnp.mean(x * x, axis=-1, keepdims=True) + EPS)
        xhat = x * r
        dxn = dy * w_ref[...]
        dx = r * (dxn - xhat * jnp.mean(dxn * xhat, axis=-1, keepdims=True))
        dh_ref[...] = dres_ref[...] + dx

        @pl.when(pl.program_id(0) == 0)
        def _():
            dw_ref[...] = jnp.zeros_like(dw_ref)

        dw_ref[...] += jnp.sum(dy * xhat, axis=0, keepdims=True)

    return pl.pallas_call(
        body, name=name, grid=(rows // tm,),
        in_specs=[pl.BlockSpec((tm, d), lambda i: (i, 0)), pl.BlockSpec((tm, d), lambda i: (i, 0)),
                  pl.BlockSpec((1, d), lambda i: (0, 0)), pl.BlockSpec((tm, d), lambda i: (i, 0))],
        out_specs=[pl.BlockSpec((tm, d), lambda i: (i, 0)), pl.BlockSpec((1, d), lambda i: (0, 0))],
        out_shape=[jax.ShapeDtypeStruct((rows, d), F32), jax.ShapeDtypeStruct((1, d), F32)],
        compiler_params=_params(("arbitrary",)),
    )(du, h, w, dres)


def _loss_head(h2, tgt, w):
    rows, d = h2.shape
    tm = _row_tile(rows)

    def body(h_ref, t_ref, w_ref, dh_ref, loss_ref, dw_ref):
        i = pl.program_id(0)
        x = h_ref[...]
        r = lax.rsqrt(jnp.mean(x * x, axis=-1, keepdims=True) + EPS)
        xhat = x * r
        wv = w_ref[...]
        row = i * tm + lax.broadcasted_iota(jnp.int32, (tm, 1), 0)
        live = row >= PAD_ROWS
        diff = jnp.where(live, xhat * wv - t_ref[...], 0.0)
        dy = diff * (1.0 / d)
        dxn = dy * wv
        dh_ref[...] = r * (dxn - xhat * jnp.mean(dxn * xhat, axis=-1, keepdims=True))

        @pl.when(i == 0)
        def _():
            loss_ref[...] = jnp.zeros_like(loss_ref)
            dw_ref[...] = jnp.zeros_like(dw_ref)

        loss_ref[...] += 0.5 * jnp.sum(jnp.mean(diff * diff, axis=-1, keepdims=True))
        dw_ref[...] += jnp.sum(dy * xhat, axis=0, keepdims=True)

    return pl.pallas_call(
        body, name="loss_head", grid=(rows // tm,),
        in_specs=[pl.BlockSpec((tm, d), lambda i: (i, 0)), pl.BlockSpec((tm, d), lambda i: (i, 0)),
                  pl.BlockSpec((1, d), lambda i: (0, 0))],
        out_specs=[pl.BlockSpec((tm, d), lambda i: (i, 0)), pl.BlockSpec((8, LANES), lambda i: (0, 0)),
                   pl.BlockSpec((1, d), lambda i: (0, 0))],
        out_shape=[jax.ShapeDtypeStruct((rows, d), F32), jax.ShapeDtypeStruct((8, LANES), F32),
                   jax.ShapeDtypeStruct((1, d), F32)],
        compiler_params=_params(("arbitrary",)),
    )(h2, tgt, w)


def _prev_halo_spec(tm, width, col):
    return pl.BlockSpec((HALO, width), lambda j, i: (jnp.maximum(i * (tm // HALO) - 1, 0), col(j)))


def _next_halo_spec(tm, rows, width, col):
    last = rows // HALO - 1
    return pl.BlockSpec((HALO, width), lambda j, i: (jnp.minimum((i + 1) * (tm // HALO), last), col(j)))


def _conv_taps(cat, w_ref, b_ref, kw):
    acc = b_ref[...] + w_ref[kw - 1:kw, :] * cat
    for s in range(1, kw):
        acc = acc + w_ref[kw - 1 - s:kw - s, :] * pltpu.roll(cat, s, 0)
    return acc


def _conv_back(dpre, w_ref, kw):
    n = dpre.shape[0]
    acc = w_ref[kw - 1:kw, :] * dpre
    for s in range(1, kw):
        acc = acc + w_ref[kw - 1 - s:kw - s, :] * pltpu.roll(dpre, n - s, 0)
    return acc


def _xbc_proj_conv(u1, w_in_p, col0, w, b):
    rows, d = u1.shape
    width = w.shape[1]
    tm, tc = _row_tile(rows), 512
    cb0 = col0 // tc
    assert col0 % tc == 0

    def body(u_ref, m_ref, w_ref, b_ref, x_ref, o_ref, carry):
        i = pl.program_id(1)

        @pl.when(i == 0)
        def _():
            carry[...] = jnp.zeros_like(carry)

        xb = _mxu(u_ref[...], m_ref[...], NN).astype(ACT_DTYPE)
        x_ref[...] = xb
        x = xb.astype(F32)
        cat = jnp.concatenate([carry[...], x], axis=0)
        carry[...] = x[tm - HALO:, :]
        pre = _conv_taps(cat, w_ref, b_ref, SSD_CONV)[HALO:]
        row = i * tm + lax.broadcasted_iota(jnp.int32, (tm, 1), 0)
        o_ref[...] = jnp.where(row >= FRONT, _silu(pre), 0.0).astype(ACT_DTYPE)

    main = pl.BlockSpec((tm, tc), lambda j, i: (i, j))
    par = lambda r: pl.BlockSpec((r, tc), lambda j, i: (0, j))
    act = jax.ShapeDtypeStruct((rows, width), ACT_DTYPE)
    return pl.pallas_call(
        body, name="xbc_proj_conv", grid=(width // tc, rows // tm),
        in_specs=[pl.BlockSpec((tm, d), lambda j, i: (i, 0)), pl.BlockSpec((d, tc), lambda j, i: (0, cb0 + j)),
                  par(SSD_CONV), par(1)],
        out_specs=[main, main], out_shape=[act, act], scratch_shapes=[pltpu.VMEM((HALO, tc), F32)],
        compiler_params=_params(("parallel", "arbitrary")),
    )(u1, w_in_p, w, b)


def _ssd_conv_bwd(xbc, dxs, dbm, dcm, w, b, dproj, col0):
    rows, width = xbc.shape
    tm, tc = _row_tile(rows), 512
    kw = SSD_CONV
    nx = dxs.shape[1] // tc
    assert dbm.shape[1] == tc and dcm.shape[1] == tc and width == (nx + 2) * tc and col0 % tc == 0

    def body(x_ref, xp_ref, xn_ref, d0_ref, d0n_ref, d1_ref, d1n_ref, d2_ref, d2n_ref, w_ref, b_ref, dp_ref,
             dx_ref, dw_ref, db_ref):
        j, i = pl.program_id(0), pl.program_id(1)
        xp = jnp.where(i == 0, 0.0, xp_ref[...].astype(F32))
        cat = jnp.concatenate([xp, x_ref[...].astype(F32), xn_ref[...].astype(F32)], axis=0)
        sh = [cat] + [pltpu.roll(cat, s, 0) for s in range(1, kw)]
        pre = b_ref[...] + w_ref[kw - 1:kw, :] * sh[0]
        for s in range(1, kw):
            pre = pre + w_ref[kw - 1 - s:kw - s, :] * sh[s]
        pre = pre[HALO:]
        row = i * tm + lax.broadcasted_iota(jnp.int32, (tm + HALO, 1), 0)
        live = (row >= FRONT) & (row < rows)
        pick = lambda a, bb, c: jnp.where(j < nx, a[...], jnp.where(j == nx, bb[...], c[...])).astype(F32)
        dout = jnp.concatenate([pick(d0_ref, d1_ref, d2_ref), pick(d0n_ref, d1n_ref, d2n_ref)], axis=0)
        dpre = jnp.where(live, dout * _dsilu(pre), 0.0)
        dx_ref[...] = _conv_back(dpre, w_ref, kw)[:tm].astype(ACT_DTYPE)

        @pl.when(i == 0)
        def _():
            dw_ref[...] = jnp.zeros_like(dw_ref)
            db_ref[...] = jnp.zeros_like(db_ref)

        dmain = dpre[:tm]
        db_ref[...] += jnp.sum(dmain, axis=0, keepdims=True)
        for k in range(kw):
            dw_ref[k:k + 1, :] += jnp.sum(dmain * sh[kw - 1 - k][HALO:HALO + tm], axis=0, keepdims=True)

    main = pl.BlockSpec((tm, tc), lambda j, i: (i, j))
    par = lambda r: pl.BlockSpec((r, tc), lambda j, i: (0, j))
    col = lambda j: j
    xcol, zero = (lambda j: jnp.minimum(j, nx - 1)), (lambda j: 0)
    dspecs = lambda c: [pl.BlockSpec((tm, tc), lambda j, i: (i, c(j))), _next_halo_spec(tm, rows, tc, c)]
    return pl.pallas_call(
        body, name="ssd_conv_bwd", grid=(width // tc, rows // tm),
        in_specs=[main, _prev_halo_spec(tm, tc, col), _next_halo_spec(tm, rows, tc, col)]
        + dspecs(xcol) + dspecs(zero) + dspecs(zero) + [par(kw), par(1), pl.BlockSpec(memory_space=pl.ANY)],
        out_specs=[pl.BlockSpec((tm, tc), lambda j, i: (i, col0 // tc + j)), par(kw), par(1)],
        out_shape=[jax.ShapeDtypeStruct(dproj.shape, dproj.dtype), jax.ShapeDtypeStruct((kw, width), F32),
                   jax.ShapeDtypeStruct((1, width), F32)],
        input_output_aliases={11: 0}, compiler_params=_params(("parallel", "arbitrary")),
    )(xbc, xbc, xbc, dxs, dxs, dbm, dbm, dcm, dcm, w, b, dproj)


def _ffn_up_conv(u2, w_up, w, b):
    rows, d = u2.shape
    width = w_up.shape[1]
    dff = width // 2
    tm, tc = _row_tile(rows), _pick(dff, (256, 128))
    nb = dff // tc
    kw = FFN_CONV

    def body(u_ref, mg_ref, mv_ref, wg_ref, bg_ref, wv_ref, bv_ref, ug_ref, uv_ref, o_ref, cg, cv):
        i = pl.program_id(1)

        @pl.when(i == 0)
        def _():
            cg[...] = jnp.zeros_like(cg)
            cv[...] = jnp.zeros_like(cv)

        def pre(m_ref, up_ref, carry, w_ref, b_ref):
            upb = _mxu(u_ref[...], m_ref[...], NN).astype(ACT_DTYPE)
            up_ref[...] = upb
            x = upb.astype(F32)
            cat = jnp.concatenate([carry[...], x], axis=0)
            carry[...] = x[tm - HALO:, :]
            return _conv_taps(cat, w_ref, b_ref, kw)[HALO:]

        ag = pre(mg_ref, ug_ref, cg, wg_ref, bg_ref)
        av = pre(mv_ref, uv_ref, cv, wv_ref, bv_ref)
        o_ref[...] = (_silu(ag) * av).astype(ACT_DTYPE)

    gcol, vcol = (lambda j: j), (lambda j: j + nb)
    mat = lambda col: pl.BlockSpec((d, tc), lambda j, i: (0, col(j)))
    par = lambda r, col: pl.BlockSpec((r, tc), lambda j, i: (0, col(j)))
    out = pl.BlockSpec((tm, tc), lambda j, i: (i, j))
    act = jax.ShapeDtypeStruct((rows, dff), ACT_DTYPE)
    return pl.pallas_call(
        body, name="ffn_up_conv", grid=(nb, rows // tm),
        in_specs=[pl.BlockSpec((tm, d), lambda j, i: (i, 0)), mat(gcol), mat(vcol),
                  par(kw, gcol), par(1, gcol), par(kw, vcol), par(1, vcol)],
        out_specs=[out, out, out], out_shape=[act, act, act],
        scratch_shapes=[pltpu.VMEM((HALO, tc), F32), pltpu.VMEM((HALO, tc), F32)],
        compiler_params=_params(("parallel", "arbitrary")),
    )(u2, w_up, w_up, w, b, w, b)


def _ffn_conv_bwd(up_g, up_v, dact, w, b, u2):
    rows, dff = up_g.shape
    d = u2.shape[1]
    tm, tc = _row_tile(rows), _pick(dff, (256, 128))
    nb = dff // tc
    kw = FFN_CONV

    sb = 16

    def body(g_ref, gp_ref, gn_ref, v_ref, vp_ref, vn_ref, d_ref, dn_ref, wg_ref, bg_ref, wv_ref, bv_ref, u_ref,
             dxg_ref, dxv_ref, dwg_ref, dbg_ref, dwv_ref, dbv_ref, gwg_ref, gwv_ref, xg_s, xv_s, dd_s, og_s, ov_s):
        i = pl.program_id(1)
        last = i == rows // tm - 1
        for x_s, x_ref, xp_ref, xn_ref in ((xg_s, g_ref, gp_ref, gn_ref), (xv_s, v_ref, vp_ref, vn_ref)):
            x_s[0:HALO, :] = jnp.where(i == 0, 0.0, xp_ref[...].astype(F32))
            x_s[HALO:HALO + tm, :] = x_ref[...].astype(F32)
            x_s[HALO + tm:, :] = xn_ref[...].astype(F32)
        dd_s[0:tm, :] = d_ref[...].astype(F32)
        dd_s[tm:, :] = jnp.where(last, 0.0, dn_ref[...].astype(F32))

        wg = [wg_ref[k:k + 1, :] for k in range(kw)]
        wv = [wv_ref[k:k + 1, :] for k in range(kw)]
        bg, bv = bg_ref[...], bv_ref[...]

        def taps(x_s, e0, w, bias):
            win = x_s[pl.ds(e0 + HALO - sb, 2 * sb), :]
            sh = [win[sb:], pltpu.roll(win, 1, 0)[sb:], pltpu.roll(win, 2, 0)[sb:]]
            return bias + w[2] * sh[0] + w[1] * sh[1] + w[0] * sh[2], sh

        def dpre_of(e0):
            ag, sh_g = taps(xg_s, e0, wg, bg)
            av, sh_v = taps(xv_s, e0, wv, bv)
            dout = dd_s[pl.ds(e0, sb), :]
            s = jax.nn.sigmoid(ag)
            silu = ag * s
            return dout * av * (s + silu * (1.0 - s)), dout * silu, sh_g, sh_v

        def back(dp, nxt, w):
            cat = jnp.concatenate([dp, nxt], axis=0)
            return w[2] * dp + w[1] * pltpu.roll(cat, 2 * sb - 1, 0)[:sb] + w[0] * pltpu.roll(cat, 2 * sb - 2, 0)[:sb]

        nxt_g, nxt_v, _, _ = dpre_of(tm)
        acc_g = acc_v = tuple(jnp.zeros((sb, tc), F32) for _ in range(kw + 1))
        for e0 in range(tm - sb, -1, -sb):
            dpg, dpv, sh_g, sh_v = dpre_of(e0)
            og_s[e0:e0 + sb, :] = back(dpg, nxt_g, wg)
            ov_s[e0:e0 + sb, :] = back(dpv, nxt_v, wv)
            acc_g = tuple(a + dpg * t for a, t in zip(acc_g, (sh_g[2], sh_g[1], sh_g[0], 1.0)))
            acc_v = tuple(a + dpv * t for a, t in zip(acc_v, (sh_v[2], sh_v[1], sh_v[0], 1.0)))
            nxt_g, nxt_v = dpg, dpv

        @pl.when(i == 0)
        def _():
            for r in (dwg_ref, dbg_ref, dwv_ref, dbv_ref, gwg_ref, gwv_ref):
                r[...] = jnp.zeros_like(r)

        for acc, o_s, dx_ref, dw_ref, db_ref, gw_ref in ((acc_g, og_s, dxg_ref, dwg_ref, dbg_ref, gwg_ref),
                                                        (acc_v, ov_s, dxv_ref, dwv_ref, dbv_ref, gwv_ref)):
            dx = o_s[...].astype(ACT_DTYPE)
            dx_ref[...] = dx
            gw_ref[...] += _mxu(u_ref[...], dx, TN)
            for k in range(kw):
                dw_ref[k:k + 1, :] += jnp.sum(acc[k], axis=0, keepdims=True)
            db_ref[...] += jnp.sum(acc[kw], axis=0, keepdims=True)

    gcol, vcol = (lambda j: j), (lambda j: j + nb)
    main = lambda col: pl.BlockSpec((tm, tc), lambda j, i: (i, col(j)))
    par = lambda r, col: pl.BlockSpec((r, tc), lambda j, i: (0, col(j)))
    halos = lambda col: [_prev_halo_spec(tm, tc, col), _next_halo_spec(tm, rows, tc, col)]
    act_shape = jax.ShapeDtypeStruct((rows, dff), ACT_DTYPE)
    par_shapes = [jax.ShapeDtypeStruct((kw, dff), F32), jax.ShapeDtypeStruct((1, dff), F32)]
    gw_shape = jax.ShapeDtypeStruct((d, dff), F32)
    return pl.pallas_call(
        body, name="ffn_conv_bwd", grid=(nb, rows // tm),
        in_specs=[main(gcol)] + halos(gcol) + [main(gcol)] + halos(gcol) + [main(gcol), _next_halo_spec(tm, rows, tc, gcol),
                  par(kw, gcol), par(1, gcol), par(kw, vcol), par(1, vcol), pl.BlockSpec((tm, d), lambda j, i: (i, 0))],
        out_specs=[main(gcol), main(gcol), par(kw, gcol), par(1, gcol), par(kw, gcol), par(1, gcol), par(d, gcol), par(d, gcol)],
        out_shape=[act_shape, act_shape] + par_shapes + par_shapes + [gw_shape, gw_shape],
        scratch_shapes=[pltpu.VMEM((tm + 2 * HALO, tc), F32), pltpu.VMEM((tm + 2 * HALO, tc), F32),
                        pltpu.VMEM((tm + HALO, tc), F32), pltpu.VMEM((tm, tc), F32), pltpu.VMEM((tm, tc), F32)],
        compiler_params=_params(("parallel", "arbitrary")),
    )(up_g, up_g, up_g, up_v, up_v, up_v, dact, dact, w, b, w, b, u2)


def _ssd_scalars(dtr, dt_bias, a_log, live):
    q = CHUNK
    pre = dtr + dt_bias
    dt = jnp.where(live, jax.nn.softplus(pre), 0.0)
    a_neg = -jnp.exp(a_log)
    li = lax.broadcasted_iota(jnp.int32, (q, q), 0)
    si = lax.broadcasted_iota(jnp.int32, (q, q), 1)
    causal = li >= si
    tri = jnp.where(causal, 1.0, 0.0).astype(F32)
    a_cs = sum(_mxu(tri, p, NN) for p in _split(dt * a_neg, 3))
    return pre, dt, a_neg, a_cs, causal, tri


def _head_select():
    r = lax.broadcasted_iota(jnp.int32, (LANES, SSD_GW), 0)
    c = lax.broadcasted_iota(jnp.int32, (LANES, SSD_GW), 1)
    return jnp.where(c // SSD_HEAD_DIM == r, 1.0, 0.0).astype(MXU_DTYPE)


def _split(t, parts):
    out, rem = [], t
    for _ in range(parts):
        p = rem.astype(MXU_DTYPE)
        out.append(p)
        rem = rem - p.astype(F32)
    return out


def _head_cols(t, sel):
    return sum(_mxu(p, sel, NN) for p in _split(t, 2))


def _head_sums(t, sel):
    return sum(_mxu(p, sel, NT) for p in _split(t, 3))


def _half_masks():
    lane = lax.broadcasted_iota(jnp.int32, (CHUNK, LANES), 1)
    return lane < SSD_HEAD_DIM, lane >= SSD_HEAD_DIM


def _ssd_scan(xs, bm, cm, dtr, prev, dt_bias, a_log, d_skip, live):
    q = CHUNK
    sel = _head_select()
    _, dt, _, a_cs, causal, _ = _ssd_scalars(dtr, dt_bias, a_log, live)
    a_cs_t = a_cs.T
    a_end = a_cs[q - 1:q, :]
    e_x = _head_cols(jnp.exp(a_cs), sel)
    xdt = xs * _head_cols(dt, sel)
    cb = _dot(cm, bm, NT)
    y = _dot(cm, prev) * e_x + _head_cols(jnp.broadcast_to(d_skip, (8, LANES)), sel)[0:1] * xs
    new = prev * e_x[q - 1:q, :] + _dot(bm, xdt * _head_cols(jnp.exp(a_end - a_cs), sel), TN)
    masks = _half_masks()
    ys = []
    for pp in range(SSD_HPG // 2):
        xpair = xdt[:, pp * LANES:(pp + 1) * LANES]
        acc = jnp.zeros((q, LANES), F32)
        for half in range(2):
            hh = 2 * pp + half
            decay = jnp.exp(jnp.where(causal, a_cs[:, hh:hh + 1] - a_cs_t[hh:hh + 1, :], -jnp.inf))
            acc = acc + _dot(cb * decay, jnp.where(masks[half], xpair, 0.0))
        ys.append(acc)
    return y + jnp.concatenate(ys, axis=1), new


def _ssd_gate(y, z, nw):
    yz = y * _silu(z)
    return yz * lax.rsqrt(jnp.mean(yz * yz, axis=-1, keepdims=True) + EPS) * nw


def _ssd_scan_bwd(xs, bm, cm, dtr, prev, dt_bias, a_log, d_skip, live, dy, dnew):
    q = CHUNK
    sel = _head_select()
    pre, dt, a_neg, a_cs, causal, tri = _ssd_scalars(dtr, dt_bias, a_log, live)
    a_cs_t = a_cs.T
    a_end = a_cs[q - 1:q, :]
    dt_x, e_x, w_x = _head_cols(dt, sel), _head_cols(jnp.exp(a_cs), sel), _head_cols(jnp.exp(a_end - a_cs), sel)
    g_x, d_x = e_x[q - 1:q, :], _head_cols(jnp.broadcast_to(d_skip, (8, LANES)), sel)[0:1]
    xdt = xs * dt_x
    u = xdt * w_x
    cb = _mxu(cm, bm, NT)
    cs = _mxu(cm, prev, NN)
    dye = dy * e_x
    dcm = _mxu(dye, prev, NT)
    dprev = _mxu(cm, dye, TN) + dnew * g_x
    dacs_x = dye * cs
    dbm = _mxu(u, dnew, NT)
    du = _mxu(bm, dnew, NN)
    dw_x = du * u
    dacs_x = dacs_x - dw_x
    dend_x = jnp.sum(dw_x + dnew * prev * g_x, axis=0, keepdims=True)
    dxdt = du * w_x
    lane = lax.broadcasted_iota(jnp.int32, (q, LANES), 1)
    sub = lax.broadcasted_iota(jnp.int32, (q, LANES), 0)
    dcb = jnp.zeros((q, q), F32)
    dacs = jnp.zeros((q, LANES), F32)
    dacs_t = jnp.zeros((q, LANES), F32)
    masks = _half_masks()
    dxdt_p = []
    for pp in range(SSD_HPG // 2):
        ps = slice(pp * LANES, (pp + 1) * LANES)
        acc = jnp.zeros((q, LANES), F32)
        for half in range(2):
            hh = 2 * pp + half
            decay = jnp.exp(jnp.where(causal, a_cs[:, hh:hh + 1] - a_cs_t[hh:hh + 1, :], -jnp.inf))
            m = cb * decay
            dyh = jnp.where(masks[half], dy[:, ps], 0.0)
            dm = _mxu(dyh, xdt[:, ps], NT)
            acc = acc + _mxu(m, dyh, TN)
            dcb = dcb + dm * decay
            p = dm * m
            dacs = jnp.where(lane == hh, jnp.sum(p, axis=1, keepdims=True), dacs)
            dacs_t = jnp.where(sub == hh, jnp.sum(p, axis=0, keepdims=True), dacs_t)
        dxdt_p.append(acc)
    dcm = dcm + _mxu(dcb, bm, NN)
    dbm = dbm + _mxu(dcb, cm, TN)
    dxdt = dxdt + jnp.concatenate(dxdt_p, axis=1)
    dxs = dy * d_x + dxdt * dt_x
    rows_x = jnp.concatenate([dend_x, jnp.sum(dy * xs, axis=0, keepdims=True), jnp.zeros((6, SSD_GW), F32)], axis=0)
    rows = _head_sums(rows_x, sel)
    dacs = dacs - dacs_t.T + _head_sums(dacs_x, sel)
    dacs = dacs + jnp.where(sub == q - 1, rows[0:1], 0.0)
    tri_t = jnp.where(causal, 0.0, 1.0).astype(F32) + jnp.where(lane == sub, 1.0, 0.0)
    da = sum(_mxu(tri_t, p, NN) for p in _split(dacs, 3))
    ddt = _head_sums(dxdt * xs, sel) + da * a_neg
    dalog = jnp.sum(da * dt, axis=0, keepdims=True) * a_neg
    ddtr = jnp.where(live, ddt * jax.nn.sigmoid(pre), 0.0)
    dbias = jnp.sum(ddtr, axis=0, keepdims=True)
    return dxs, dbm, dcm, ddtr, dprev, dbias, dalog, rows[1:2]


def _chunks_per_step(nc):
    return 5 if nc % 5 == 0 else 1


def _ssd_specs(rev, nc):
    per = _chunks_per_step(nc)
    steps = nc // per
    sidx = (lambda s: steps - 1 - s) if rev else (lambda s: s)
    nb_b = SSD_D_INNER // SSD_STATE
    row = lambda width, col=lambda g: g: pl.BlockSpec((per * CHUNK, width), lambda g, s: (sidx(s), col(g)))
    par = lambda width: pl.BlockSpec((1, width), lambda g, s: (0, g))
    state = lambda: pl.BlockSpec((per, 1, SSD_STATE, SSD_GW), lambda g, s: (sidx(s), g, 0, 0))
    xbc = [row(SSD_GW), row(SSD_STATE, lambda g: nb_b + g), row(SSD_STATE, lambda g: nb_b + SSD_GROUPS + g)]
    return per, steps, sidx, row, par, state, xbc


def _ssd_fwd(xbc_c, dtr, z, dt_bias, a_log, d_skip, nw):
    rows = z.shape[0]
    nc = rows // CHUNK
    per, steps, _, row, par, state, xbc = _ssd_specs(False, nc)

    def body(xs_ref, b_ref, c_ref, dt_ref, z_ref, bias_ref, al_ref, dk_ref, nw_ref, o_ref, y_ref, st_ref, carry):
        s = pl.program_id(1)

        @pl.when(s == 0)
        def _():
            carry[...] = jnp.zeros_like(carry)

        for j in range(per):
            rs = pl.ds(j * CHUNK, CHUNK)
            live = (s * per + j) * CHUNK + lax.broadcasted_iota(jnp.int32, (CHUNK, 1), 0) >= FRONT
            prev = carry[...]
            st_ref[j, 0] = prev
            y, new = _ssd_scan(xs_ref[rs, :].astype(F32), b_ref[rs, :].astype(F32), c_ref[rs, :].astype(F32), dt_ref[rs, :],
                               prev, bias_ref[...], al_ref[...], dk_ref[...], live)
            y_ref[rs, :] = y.astype(ACT_DTYPE)
            o_ref[rs, :] = _ssd_gate(y, z_ref[rs, :].astype(F32), nw_ref[...]).astype(ACT_DTYPE)
            carry[...] = new

    act = jax.ShapeDtypeStruct((rows, SSD_D_INNER), ACT_DTYPE)
    return pl.pallas_call(
        body, name="ssd_fwd", grid=(SSD_GROUPS, steps),
        in_specs=xbc + [row(LANES), row(SSD_GW), par(LANES), par(LANES), par(LANES), par(SSD_GW)],
        out_specs=[row(SSD_GW), row(SSD_GW), state()],
        out_shape=[act, act, jax.ShapeDtypeStruct((nc, SSD_GROUPS, SSD_STATE, SSD_GW), F32)],
        scratch_shapes=[pltpu.VMEM((SSD_STATE, SSD_GW), F32)],
        compiler_params=_params(("parallel", "arbitrary")),
    )(xbc_c, xbc_c, xbc_c, dtr, z, dt_bias, a_log, d_skip, nw)


def _ssd_gate_bwd(y, z, dout, nw, dproj, col0):
    rows = y.shape[0]
    tm = _row_tile(rows)
    assert col0 % SSD_GW == 0

    def body(y_ref, z_ref, do_ref, nw_ref, dp_ref, dy_ref, dz_ref, dnw_ref):
        yv, zv, dov = y_ref[...].astype(F32), z_ref[...].astype(F32), do_ref[...].astype(F32)
        s = jax.nn.sigmoid(zv)
        silu = zv * s
        yz = yv * silu
        r = lax.rsqrt(jnp.mean(yz * yz, axis=-1, keepdims=True) + EPS)
        yhat = yz * r
        dn = dov * nw_ref[...]
        dyz = r * (dn - yhat * jnp.mean(dn * yhat, axis=-1, keepdims=True))
        dy_ref[...] = (dyz * silu).astype(ACT_DTYPE)
        dz_ref[...] = (dyz * yv * (s + silu * (1.0 - s))).astype(ACT_DTYPE)

        @pl.when(pl.program_id(1) == 0)
        def _():
            dnw_ref[...] = jnp.zeros_like(dnw_ref)

        dnw_ref[...] += jnp.sum(dov * yhat, axis=0, keepdims=True)

    spec = pl.BlockSpec((tm, SSD_GW), lambda g, i: (i, g))
    par = pl.BlockSpec((1, SSD_GW), lambda g, i: (0, g))
    act = jax.ShapeDtypeStruct((rows, SSD_D_INNER), ACT_DTYPE)
    return pl.pallas_call(
        body, name="ssd_gate_bwd", grid=(SSD_GROUPS, rows // tm),
        in_specs=[spec, spec, spec, par, pl.BlockSpec(memory_space=pl.ANY)],
        out_specs=[spec, pl.BlockSpec((tm, SSD_GW), lambda g, i: (i, col0 // SSD_GW + g)), par],
        out_shape=[act, jax.ShapeDtypeStruct(dproj.shape, dproj.dtype), jax.ShapeDtypeStruct((1, SSD_D_INNER), F32)],
        input_output_aliases={4: 1}, compiler_params=_params(("parallel", "arbitrary")),
    )(y, z, dout, nw, dproj)


def _ssd_bwd(xbc_c, dtr, dt_bias, a_log, d_skip, states, dy, dproj, col0):
    rows = dy.shape[0]
    nc = rows // CHUNK
    per, steps, sidx, row, par, state, xbc = _ssd_specs(True, nc)
    assert col0 % LANES == 0

    def body(xs_ref, b_ref, c_ref, dt_ref, bias_ref, al_ref, dk_ref, st_ref, dy_ref, dp_ref,
             dxs_ref, db_ref, dc_ref, ddt_ref, dbias_ref, dal_ref, ddk_ref, carry):
        s = pl.program_id(1)

        @pl.when(s == 0)
        def _():
            carry[...] = jnp.zeros_like(carry)
            for r in (dbias_ref, dal_ref, ddk_ref):
                r[...] = jnp.zeros_like(r)

        for j in reversed(range(per)):
            rs = pl.ds(j * CHUNK, CHUNK)
            live = (sidx(s) * per + j) * CHUNK + lax.broadcasted_iota(jnp.int32, (CHUNK, 1), 0) >= FRONT
            dxs, dbm, dcm, ddt, dprev, dbias, dal, ddk = _ssd_scan_bwd(
                xs_ref[rs, :].astype(F32), b_ref[rs, :].astype(F32), c_ref[rs, :].astype(F32), dt_ref[rs, :], st_ref[j, 0],
                bias_ref[...], al_ref[...], dk_ref[...], live, dy_ref[rs, :].astype(F32), carry[...])
            dxs_ref[rs, :] = dxs.astype(ACT_DTYPE)
            db_ref[rs, :] = dbm.astype(ACT_DTYPE)
            dc_ref[rs, :] = dcm.astype(ACT_DTYPE)
            ddt_ref[rs, :] = ddt.astype(ACT_DTYPE)
            carry[...] = dprev
            dbias_ref[...] += dbias
            dal_ref[...] += dal
            ddk_ref[...] += ddk

    bc = jax.ShapeDtypeStruct((rows, SSD_GROUPS * SSD_STATE), ACT_DTYPE)
    head = jax.ShapeDtypeStruct((1, SSD_GROUPS * LANES), F32)
    return pl.pallas_call(
        body, name="ssd_bwd", grid=(SSD_GROUPS, steps),
        in_specs=xbc + [row(LANES), par(LANES), par(LANES), par(LANES), state(), row(SSD_GW), pl.BlockSpec(memory_space=pl.ANY)],
        out_specs=[row(SSD_GW), row(SSD_STATE), row(SSD_STATE), row(LANES, lambda g: col0 // LANES + g),
                   par(LANES), par(LANES), par(LANES)],
        out_shape=[jax.ShapeDtypeStruct((rows, SSD_D_INNER), ACT_DTYPE), bc, bc,
                   jax.ShapeDtypeStruct(dproj.shape, dproj.dtype), head, head, head],
        input_output_aliases={9: 3}, scratch_shapes=[pltpu.VMEM((SSD_STATE, SSD_GW), F32)],
        compiler_params=_params(("parallel", "arbitrary")),
    )(xbc_c, xbc_c, xbc_c, dtr, dt_bias, a_log, d_skip, states, dy, dproj)


def _rotary_tables(rows):
    pos = np.arange(rows, dtype=np.float32) - np.float32(FRONT)
    inv_freq = np.float32(ROPE_BASE) ** (-np.linspace(0.0, 1.0, RET_QK // 2, dtype=np.float32))
    ang = (pos[:, None] * inv_freq[None, :]).astype(np.float32).astype(np.float64)
    lgam = np.log(1.0 - 2.0 ** (-5.0 - np.arange(RET_HEADS, dtype=np.float64))).astype(np.float32)
    lgam = np.broadcast_to(lgam[:, None, None], (RET_HEADS, 8, LANES))
    return jnp.asarray(np.cos(ang).astype(np.float32)), jnp.asarray(np.sin(ang).astype(np.float32)), jnp.asarray(lgam)


def _rotary(t, cos, sin):
    half = t.shape[-1] // 2
    t1, t2 = t[:, :half], t[:, half:]
    return jnp.concatenate([t1 * cos - t2 * sin, t2 * cos + t1 * sin], axis=1)


def _ret_chunk(qh, kh, vh, gh, prev, cos, sin, lg):
    q = CHUNK
    qr = _rotary(qh, cos, sin)
    kr = _rotary(kh, cos, sin) * (RET_QK ** -0.5)
    li = lax.broadcasted_iota(jnp.int32, (q, q), 0)
    si = lax.broadcasted_iota(jnp.int32, (q, q), 1)
    dist = (li - si).astype(F32)
    decay = jnp.exp(jnp.where(li >= si, dist * lg, -jnp.inf))
    idx = lax.broadcasted_iota(jnp.int32, (q, 1), 0).astype(F32)
    scores = _dot(qr, kr, NT) * decay
    out = _dot(scores, vh)
    kv = _dot(kr * jnp.exp((q - 1.0 - idx) * lg), vh, TN)
    out = out + _dot(qr, prev) * jnp.exp((idx + 1.0) * lg)
    new = prev * jnp.exp(q * lg) + kv
    out = out * lax.rsqrt(jnp.mean(out * out, axis=-1, keepdims=True) + EPS)
    return _silu(gh) * out, new


def _ret_specs(rev, nc):
    per = _chunks_per_step(nc)
    steps = nc // per
    sidx = (lambda s: steps - 1 - s) if rev else (lambda s: s)
    row = lambda width: pl.BlockSpec((per * CHUNK, width), lambda h, s: (sidx(s), h))
    tab = lambda: pl.BlockSpec((per * CHUNK, RET_QK // 2), lambda h, s: (sidx(s), 0))
    lgs = lambda: pl.BlockSpec((1, 8, LANES), lambda h, s: (h, 0, 0))
    state = lambda: pl.BlockSpec((per, 1, RET_QK, RET_V), lambda h, s: (sidx(s), h, 0, 0))
    part = lambda width, k: pl.BlockSpec((per * CHUNK, width), lambda h, s: (sidx(s), h * (RET_HW // width) + k))
    ins = [part(RET_QK, 0), part(RET_QK, 1), part(RET_V, 1), part(RET_V, 2), tab(), tab(), lgs()]
    return per, steps, sidx, row, state, ins


def _ret_fwd(qkvg, cos, sin, lgam):
    rows = qkvg.shape[0]
    nc = rows // CHUNK
    per, steps, _, row, state, ins = _ret_specs(False, nc)
    q = k = v = g = qkvg

    def body(q_ref, k_ref, v_ref, g_ref, cos_ref, sin_ref, lg_ref, y_ref, st_ref, carry):
        @pl.when(pl.program_id(1) == 0)
        def _():
            carry[...] = jnp.zeros_like(carry)

        for j in range(per):
            rs = pl.ds(j * CHUNK, CHUNK)
            prev = carry[...]
            st_ref[j, 0] = prev.astype(ACT_DTYPE)
            out, new = _ret_chunk(q_ref[rs, :].astype(F32), k_ref[rs, :].astype(F32), v_ref[rs, :].astype(F32),
                                  g_ref[rs, :].astype(F32), prev, cos_ref[rs, :], sin_ref[rs, :], lg_ref[0, 0:1, 0:1])
            y_ref[rs, :] = out.astype(ACT_DTYPE)
            carry[...] = new

    return pl.pallas_call(
        body, name="ret_fwd", grid=(RET_HEADS, steps), in_specs=ins, out_specs=[row(RET_V), state()],
        out_shape=[jax.ShapeDtypeStruct((rows, RET_HEADS * RET_V), ACT_DTYPE),
                   jax.ShapeDtypeStruct((nc, RET_HEADS, RET_QK, RET_V), ACT_DTYPE)],
        scratch_shapes=[pltpu.VMEM((RET_QK, RET_V), F32)],
        compiler_params=_params(("parallel", "arbitrary")),
    )(q, k, v, g, cos, sin, lgam)


def _ret_bwd(qkvg, cos, sin, lgam, states, dy, dproj):
    rows = qkvg.shape[0]
    nc = rows // CHUNK
    per, steps, sidx, row, state, ins = _ret_specs(True, nc)

    def body(q_ref, k_ref, v_ref, g_ref, cos_ref, sin_ref, lg_ref, st_ref, dy_ref, dp_ref, o_ref, carry):
        @pl.when(pl.program_id(1) == 0)
        def _():
            carry[...] = jnp.zeros_like(carry)

        for j in reversed(range(per)):
            rs = pl.ds(j * CHUNK, CHUNK)
            fn = functools.partial(_ret_chunk, cos=cos_ref[rs, :], sin=sin_ref[rs, :], lg=lg_ref[0, 0:1, 0:1])
            _, vjp = jax.vjp(fn, q_ref[rs, :].astype(F32), k_ref[rs, :].astype(F32), v_ref[rs, :].astype(F32),
                             g_ref[rs, :].astype(F32), st_ref[j, 0].astype(F32))
            dq, dk, dv, dg, dprev = vjp((dy_ref[rs, :].astype(F32), carry[...]))
            o_ref[rs, 0:RET_QK] = dq.astype(ACT_DTYPE)
            o_ref[rs, RET_QK:2 * RET_QK] = dk.astype(ACT_DTYPE)
            o_ref[rs, 2 * RET_QK:2 * RET_QK + RET_V] = dv.astype(ACT_DTYPE)
            o_ref[rs, 2 * RET_QK + RET_V:RET_HW] = dg.astype(ACT_DTYPE)
            carry[...] = dprev

    return pl.pallas_call(
        body, name="ret_bwd", grid=(RET_HEADS, steps),
        in_specs=ins + [state(), row(RET_V), pl.BlockSpec(memory_space=pl.ANY)],
        out_specs=pl.BlockSpec((per * CHUNK, RET_HW), lambda h, s: (sidx(s), h)),
        out_shape=jax.ShapeDtypeStruct(dproj.shape, dproj.dtype), input_output_aliases={9: 0},
        scratch_shapes=[pltpu.VMEM((RET_QK, RET_V), F32)],
        compiler_params=_params(("parallel", "arbitrary")),
    )(qkvg, qkvg, qkvg, qkvg, cos, sin, lgam, states, dy, dproj)


def _merge_fwd(bs, br, gates):
    rows, d = bs.shape
    tm = _row_tile(rows)

    def body(bs_ref, br_ref, gs_ref, gr_ref, o_ref):
        o_ref[...] = (jax.nn.sigmoid(gs_ref[...].astype(F32)) * bs_ref[...].astype(F32)
                      + jax.nn.sigmoid(gr_ref[...].astype(F32)) * br_ref[...].astype(F32)).astype(ACT_DTYPE)

    spec = pl.BlockSpec((tm, d), lambda i: (i, 0))
    return pl.pallas_call(
        body, name="merge_fwd", grid=(rows // tm,), in_specs=[spec, spec, spec, pl.BlockSpec((tm, d), lambda i: (i, 1))],
        out_specs=spec, out_shape=jax.ShapeDtypeStruct((rows, d), ACT_DTYPE), compiler_params=_params(("parallel",)),
    )(bs, br, gates, gates)


def _merge_bwd(dm, bs, br, gates, dproj, col0):
    rows, d = bs.shape
    tm = _row_tile(rows)
    assert col0 % (2 * d) == 0

    def body(dm_ref, bs_ref, br_ref, gs_ref, gr_ref, dp_ref, dbs_ref, dbr_ref, dg_ref):
        dmv = dm_ref[...].astype(F32)
        for k, (b_ref, g_ref, db_ref) in enumerate(((bs_ref, gs_ref, dbs_ref), (br_ref, gr_ref, dbr_ref))):
            s = jax.nn.sigmoid(g_ref[...].astype(F32))
            db_ref[...] = (dmv * s).astype(ACT_DTYPE)
            dg_ref[:, k * d:(k + 1) * d] = (dmv * b_ref[...].astype(F32) * s * (1.0 - s)).astype(ACT_DTYPE)

    spec = pl.BlockSpec((tm, d), lambda i: (i, 0))
    shp = jax.ShapeDtypeStruct((rows, d), ACT_DTYPE)
    return pl.pallas_call(
        body, name="merge_bwd", grid=(rows // tm,),
        in_specs=[spec, spec, spec, spec, pl.BlockSpec((tm, d), lambda i: (i, 1)), pl.BlockSpec(memory_space=pl.ANY)],
        out_specs=[spec, spec, pl.BlockSpec((tm, 2 * d), lambda i: (i, col0 // (2 * d)))],
        out_shape=[shp, shp, jax.ShapeDtypeStruct(dproj.shape, dproj.dtype)], input_output_aliases={5: 2},
        compiler_params=_params(("parallel",)),
    )(dm, bs, br, gates, gates, dproj)


def _place():
    x, y, c = lax.axis_index("x"), lax.axis_index("y"), lax.axis_index("c")
    return x, y, c


def _slot(p):
    return 4 * p[0] + 2 * p[1] + p[2]


def _allgather(arrs, name):
    n = len(arrs)
    any_spec = pl.BlockSpec(memory_space=pl.ANY)

    def body(*refs):
        ins, outs = refs[:n], refs[n:2 * n]
        send_sems, recv_sems, local_sems = refs[2 * n:]
        x, y, c = _place()
        me, sibling = (x, y, c), (x, y, 1 - c)
        chips = [(1 - x, y), (x, 1 - y), (1 - x, 1 - y)]

        def copy(a, k, block, to, src=None):
            dst = outs[a].at[_slot(block)]
            return pltpu.make_async_remote_copy(
                src_ref=dst if src is None else src, dst_ref=dst, send_sem=send_sems.at[a * 7 + k],
                recv_sem=recv_sems.at[a * 7 + k], device_id=to, device_id_type=MESH)

        mine, first, passed = [], [], []
        for a in range(n):
            cp = pltpu.make_async_copy(ins[a], outs[a].at[_slot(me)], local_sems.at[a])
            cp.start()
            mine.append(cp)
            first.append(copy(a, 0, me, sibling, src=ins[a]))
            first += [copy(a, 1 + j, me, (*chip, c), src=ins[a]) for j, chip in enumerate(chips)]
        for cp in first:
            cp.start()
        for j, chip in enumerate(chips):
            for a in range(n):
                copy(a, 1 + j, (*chip, c), me).wait_recv()
                cp = copy(a, 4 + j, (*chip, c), sibling)
                cp.start()
                passed.append(cp)
        for a in range(n):
            copy(a, 0, sibling, me).wait_recv()
            for j, chip in enumerate(chips):
                copy(a, 4 + j, (*chip, 1 - c), me).wait_recv()
        for cp in first + passed:
            cp.wait_send()
        for cp in mine:
            cp.wait()

    return pl.pallas_call(
        body, name=name, in_specs=[any_spec] * n, out_specs=[any_spec] * n,
        out_shape=[jax.ShapeDtypeStruct((N_DEV,) + a.shape, a.dtype) for a in arrs],
        scratch_shapes=[pltpu.SemaphoreType.DMA((7 * n,)), pltpu.SemaphoreType.DMA((7 * n,)), pltpu.SemaphoreType.DMA((n,))],
    )(*arrs)


def _peers():
    x, y, c = _place()
    return (x, y, c), [(x ^ dx, y ^ dy, c ^ dc) for dx in (0, 1) for dy in (0, 1) for dc in (0, 1)][1:]


def _exchange_copies(srcs, lands, send_sems, recv_sems, scatter, sender):
    me, peers = _peers()
    out = []
    for a, (src, land) in enumerate(zip(srcs, lands, strict=True)):
        for k, peer in enumerate(peers):
            src_ref = src.at[_slot(peer)] if scatter else src
            out.append(pltpu.make_async_remote_copy(
                src_ref=src_ref, dst_ref=land.at[_slot(me if sender else peer)], send_sem=send_sems.at[a * 7 + k],
                recv_sem=recv_sems.at[a * 7 + k], device_id=peer, device_id_type=MESH))
    return out


_HBM = pl.BlockSpec(memory_space=pltpu.HBM)
_SEM = pl.BlockSpec(memory_space=pltpu.SEMAPHORE)
_EFFECT = pltpu.SideEffectType.DATAFLOW_SIDE_EFFECTING


def _exchange_start(srcs, scatter, name, after=None):
    n = len(srcs)
    land_shapes = [s.shape if scatter else (N_DEV,) + s.shape for s in srcs]
    n_in = 2 * n + (after is not None)

    def body(*refs):
        for cp in _exchange_copies(refs[:n], refs[n:2 * n], refs[n_in], refs[n_in + 1], scatter, True):
            cp.start()
        refs[-1][...] = jnp.zeros_like(refs[-1])

    args = [pltpu.with_memory_space_constraint(s, pltpu.HBM) for s in srcs]
    args += [pltpu.with_memory_space_constraint(lax.empty(shp, s.dtype), pltpu.HBM) for s, shp in zip(srcs, land_shapes)]
    thru_shapes = tuple(pltpu.HBM(a.shape, a.dtype) for a in args)
    extra = [] if after is None else [after]
    outs = pl.pallas_call(
        body, name=name,
        out_shape=(pltpu.SemaphoreType.DMA((7 * n,)), pltpu.SemaphoreType.DMA((7 * n,))) + thru_shapes
        + (jax.ShapeDtypeStruct((8, LANES), F32),),
        in_specs=[_HBM] * (2 * n) + [pl.BlockSpec(memory_space=pl.ANY)] * len(extra),
        out_specs=(_SEM, _SEM) + (_HBM,) * (2 * n) + (pl.BlockSpec(memory_space=pltpu.VMEM),),
        input_output_aliases={i: 2 + i for i in range(2 * n)},
        compiler_params=pltpu.CompilerParams(has_side_effects=_EFFECT),
    )(*args, *extra)
    return outs[:-1], outs[-1]


def _exchange_wait(handle, scatter, after, name):
    n = (len(handle) - 2) // 2
    thru = handle[2:]

    def body(*refs):
        for cp in _exchange_copies(refs[:n], refs[n:2 * n], refs[2 * n], refs[2 * n + 1], scatter, False):
            cp.wait_send()
            cp.wait_recv()

    outs = pl.pallas_call(
        body, name=name, out_shape=tuple(pltpu.HBM(t.shape, t.dtype) for t in thru),
        in_specs=[_HBM] * (2 * n) + [_SEM, _SEM, pl.BlockSpec(memory_space=pl.ANY)], out_specs=(_HBM,) * (2 * n),
        input_output_aliases={i: i for i in range(2 * n)},
        compiler_params=pltpu.CompilerParams(has_side_effects=_EFFECT),
    )(*thru, handle[0], handle[1], after)
    return list(outs[:n]), list(outs[n:])


def _allreduce_small(pack):
    rows, lanes = pack.shape

    def body(x_ref, o_ref, buf, send_sems, recv_sems):
        x, y, c = _place()
        me, sibling = (x, y, c), (x, y, 1 - c)
        chips = [(1 - x, y), (x, 1 - y), (1 - x, 1 - y)]

        def copy(k, block, to, src=None):
            dst = buf.at[_slot(block)]
            return pltpu.make_async_remote_copy(
                src_ref=dst if src is None else src, dst_ref=dst, send_sem=send_sems.at[k], recv_sem=recv_sems.at[k],
                device_id=to, device_id_type=MESH)

        buf[_slot(me)] = x_ref[...]
        first = [copy(0, me, sibling, src=x_ref)]
        first += [copy(1 + j, me, (*chip, c), src=x_ref) for j, chip in enumerate(chips)]
        for cp in first:
            cp.start()
        passed = [copy(4 + j, (*chip, c), sibling) for j, chip in enumerate(chips)]
        for j, chip in enumerate(chips):
            copy(1 + j, (*chip, c), me).wait_recv()
            passed[j].start()
        copy(0, sibling, me).wait_recv()
        for j, chip in enumerate(chips):
            copy(4 + j, (*chip, 1 - c), me).wait_recv()
        for cp in first + passed:
            cp.wait_send()
        acc = buf[0]
        for i in range(1, N_DEV):
            acc = acc + buf[i]
        o_ref[...] = acc

    vmem = pl.BlockSpec(memory_space=pltpu.VMEM)
    return pl.pallas_call(
        body, name="allreduce_small", in_specs=[vmem], out_specs=vmem,
        out_shape=jax.ShapeDtypeStruct((rows, lanes), F32),
        scratch_shapes=[pltpu.VMEM((N_DEV, rows, lanes), F32), pltpu.SemaphoreType.DMA((7,)), pltpu.SemaphoreType.DMA((7,))],
        compiler_params=pltpu.CompilerParams(vmem_limit_bytes=VMEM_LIMIT),
    )(pack)


def _adamw(w, g, m, v):
    m = ADAM_B1 * m + (1.0 - ADAM_B1) * g
    v = ADAM_B2 * v + (1.0 - ADAM_B2) * jnp.square(g)
    m_hat = m / (1.0 - ADAM_B1 ** ADAM_STEP)
    v_hat = v / (1.0 - ADAM_B2 ** ADAM_STEP)
    delta = -ADAM_LR * (m_hat / (jnp.sqrt(v_hat) + ADAM_EPS) + ADAM_WD * w)
    return delta, m, v


def _adam_shard(own, parts, w, m, v, name):
    r, c = w.shape
    tr = _pick(r, (128, 64, 32, 16, 8))

    def body(own_ref, p_ref, w_ref, m_ref, v_ref, g_ref, d_ref, nm_ref, nv_ref):
        _, peers = _peers()
        g = own_ref[...].astype(F32)
        for peer in peers:
            g = g + p_ref[_slot(peer)].astype(F32)
        g_ref[...] = g
        d_ref[...], nm_ref[...], nv_ref[...] = _adamw(w_ref[...], g, m_ref[...], v_ref[...])

    spec = pl.BlockSpec((tr, c), lambda i: (i, 0))
    shp = jax.ShapeDtypeStruct((r, c), F32)
    return pl.pallas_call(
        body, name=name, grid=(r // tr,),
        in_specs=[spec, pl.BlockSpec((N_DEV, tr, c), lambda i: (0, i, 0)), spec, spec, spec], out_specs=[spec] * 4,
        out_shape=[shp] * 4, compiler_params=_params(("parallel",)),
    )(own, parts, w, m, v)


def _adam_small(w, g, m, v):
    r, c = w.shape

    def body(w_ref, g_ref, m_ref, v_ref, d_ref, nm_ref, nv_ref):
        d_ref[...], nm_ref[...], nv_ref[...] = _adamw(w_ref[...], g_ref[...], m_ref[...], v_ref[...])

    shp = jax.ShapeDtypeStruct((r, c), F32)
    return pl.pallas_call(body, name="adam_small", out_shape=[shp] * 3)(w, g, m, v)


def _column_plan(pieces, shard_w):
    plan = []
    for c0, width, d0 in pieces:
        c = c0
        while c < c0 + width:
            s, a = divmod(c, shard_w)
            w = min(c0 + width - c, shard_w - a)
            plan.append((s, a, w, d0 + c - c0))
            c += w
    return plan


def _cols_from_shards(g, plan, out_w, zero, name):
    _, r, sw = g.shape
    tr = _pick(r, (128,))

    def body(x_ref, o_ref):
        for d0, w in zero:
            o_ref[:, d0:d0 + w] = jnp.zeros((tr, w), g.dtype)
        for s, a, w, d0 in plan:
            o_ref[:, d0:d0 + w] = x_ref[s, :, a:a + w]

    return pl.pallas_call(
        body, name=name, grid=(r // tr,), in_specs=[pl.BlockSpec((N_DEV, tr, sw), lambda i: (0, i, 0))],
        out_specs=pl.BlockSpec((tr, out_w), lambda i: (i, 0)), out_shape=jax.ShapeDtypeStruct((r, out_w), g.dtype),
        compiler_params=_params(("parallel",)),
    )(g)


def _shards_from_cols(srcs, plans, shard_w, name):
    r = srcs[0].shape[0]
    tr = _pick(r, (128,))
    n = len(srcs)

    def body(*refs):
        o_ref = refs[n]
        for x_ref, plan in zip(refs[:n], plans, strict=True):
            for s, a, w, d0 in plan:
                o_ref[s, :, a:a + w] = x_ref[:, d0:d0 + w].astype(COMM_DTYPE)

    return pl.pallas_call(
        body, name=name, grid=(r // tr,), in_specs=[pl.BlockSpec((tr, t.shape[1]), lambda i: (i, 0)) for t in srcs],
        out_specs=pl.BlockSpec((N_DEV, tr, shard_w), lambda i: (0, i, 0)),
        out_shape=jax.ShapeDtypeStruct((N_DEV, r, shard_w), COMM_DTYPE), compiler_params=_params(("parallel",)),
    )(*srcs)


def _pack(arrs):
    rows = []
    for a in arrs:
        flat = a.reshape(-1).astype(F32)
        rows.append(jnp.pad(flat, (0, (-flat.shape[0]) % (8 * LANES))).reshape(-1, LANES))
    return jnp.concatenate(rows, axis=0)


def _unpack(pack, shapes):
    out, r = [], 0
    for s in shapes:
        size = math.prod(s)
        nr = -(-size // (8 * LANES)) * 8
        out.append(pack[r:r + nr].reshape(-1)[:size].reshape(s))
        r += nr
    return out


def _group_lanes(t):
    lead = t.shape[:-1]
    t = t.reshape(lead + (SSD_GROUPS, SSD_HPG))
    t = jnp.pad(t, [(0, 0)] * len(lead) + [(0, 0), (0, LANES - SSD_HPG)])
    return t.reshape(lead + (SSD_GROUPS * LANES,))


def _ungroup_lanes(t):
    lead = t.shape[:-1]
    return t.reshape(lead + (SSD_GROUPS, LANES))[..., :SSD_HPG].reshape(lead + (SSD_HEADS,))


def kernel(x, meta_tokens, mix_norm_w, w_in, ssd_conv_w, ssd_conv_b, ssd_dt_bias, ssd_A_log, ssd_D, ssd_norm_w, w_branch_ssd, w_branch_ret, w_out, ffn_norm_w, w_up, ffn_conv_w, ffn_conv_b, w_down, final_norm_w, loss_target, m_meta_tokens, m_mix_norm_w, m_w_in, m_ssd_conv_w, m_ssd_conv_b, m_ssd_dt_bias, m_ssd_A_log, m_ssd_D, m_ssd_norm_w, m_w_branch_ssd, m_w_branch_ret, m_w_out, m_ffn_norm_w, m_w_up, m_ffn_conv_w, m_ffn_conv_b, m_w_down, m_final_norm_w, v_meta_tokens, v_mix_norm_w, v_w_in, v_ssd_conv_w, v_ssd_conv_b, v_ssd_dt_bias, v_ssd_A_log, v_ssd_D, v_ssd_norm_w, v_w_branch_ssd, v_w_branch_ret, v_w_out, v_ffn_norm_w, v_w_up, v_ffn_conv_w, v_ffn_conv_b, v_w_down, v_final_norm_w):
    seq, d = x.shape[1], x.shape[2]
    rows = seq + PAD_ROWS
    tm = _row_tile(rows)
    me = _slot(_place())
    d_ff = w_down.shape[1] * N_DEV

    big = [w_in[0], w_branch_ssd[0], w_branch_ret[0], w_out[0], w_up[0], w_down[0]]
    first = _allgather([w_in[0].astype(COMM_DTYPE), meta_tokens, ssd_conv_w[0], ffn_conv_w[0]], "gather_first")
    rest_src = [b.astype(COMM_DTYPE) for b in big[1:]]
    rest_handle, rest_token = _exchange_start(rest_src, False, "gather_rest_start", after=first[0])
    cols = lambda t: jnp.transpose(t, (1, 0, 2)).reshape(t.shape[1], -1)
    rws = lambda t: t.reshape(-1, t.shape[2])
    conv_w, fconv_w = cols(first[2]), cols(first[3])
    meta_full = cols(first[1]) + rest_token[0, 0]
    widths = [SSD_D_INNER, SSD_CONV_DIM, SSD_HEADS, RET_HEADS * RET_QK, RET_HEADS * RET_QK, RET_HEADS * RET_V,
              RET_HEADS * RET_V, d, d]
    offs = [0]
    for wd in widths:
        offs.append(offs[-1] + wd)
    r0, z0 = 0, RET_HEADS * RET_HW
    g0 = z0 + widths[0]
    x0 = g0 + 2 * d
    dt0 = x0 + widths[1]
    in_p = dt0 + SSD_GROUPS * LANES
    pieces = []
    for hd in range(RET_HEADS):
        base = r0 + hd * RET_HW
        pieces += [(offs[3] + hd * RET_QK, RET_QK, base), (offs[4] + hd * RET_QK, RET_QK, base + RET_QK),
                   (offs[5] + hd * RET_V, RET_V, base + 2 * RET_QK), (offs[6] + hd * RET_V, RET_V, base + 2 * RET_QK + RET_V)]
    pieces += [(offs[0], widths[0], z0), (offs[7], d, g0), (offs[8], d, g0 + d), (offs[1], widths[1], x0)]
    pieces += [(offs[2] + SSD_HPG * grp, SSD_HPG, dt0 + LANES * grp) for grp in range(SSD_GROUPS)]
    in_plan = _column_plan(pieces, w_in.shape[2])
    w_in_p = _cols_from_shards(first[0], in_plan, in_p, [(dt0, SSD_GROUPS * LANES)], "w_in_columns")

    h0 = jnp.concatenate([jnp.zeros((FRONT, d), F32), meta_full, x[0]], axis=0)
    u1 = _rms_fwd(h0, mix_norm_w, "rms1")
    in_proj = lambda c0, width, dtype, nm: _mm(u1, w_in_p, mode="nn", out_dtype=dtype, tm=tm, tk=d, name="in_proj_" + nm,
                                               tn=_pick(width, (1024, 512)), b_n0=c0, n_out=width)
    qkvg = in_proj(r0, RET_HEADS * RET_HW, ACT_DTYPE, "qkvg")
    z = in_proj(z0, widths[0], ACT_DTYPE, "z")
    gates = in_proj(g0, 2 * d, ACT_DTYPE, "gates")
    dtr = in_proj(dt0, SSD_GROUPS * LANES, F32, "dt")
    xbc, xbc_c = _xbc_proj_conv(u1, w_in_p, x0, conv_w, ssd_conv_b)
    bias_p, alog_p, dsk_p = _group_lanes(ssd_dt_bias), _group_lanes(ssd_A_log), _group_lanes(ssd_D)
    y_ssd, y_scan, ssd_states = _ssd_fwd(xbc_c, dtr, z, bias_p, alog_p, dsk_p, ssd_norm_w)
    cos, sin, lgam = _rotary_tables(rows)
    y_ret, ret_states = _ret_fwd(qkvg, cos, sin, lgam)
    rest_own, rest = _exchange_wait(rest_handle, False, y_ret, "gather_rest_wait")
    rest = [lax.dynamic_update_index_in_dim(land, own, me, 0) for land, own in zip(rest, rest_own, strict=True)]
    w_bs, w_br, w_o, w_dn = rws(rest[0]), rws(rest[1]), rws(rest[2]), rws(rest[4])
    w_up_f = _cols_from_shards(rest[3], _column_plan([(0, 2 * d_ff, 0)], w_up.shape[2]), 2 * d_ff, [], "w_up_columns")
    bs = _mm(y_ssd, w_bs, mode="nn", out_dtype=ACT_DTYPE, tm=tm, tn=d, tk=SSD_D_INNER, name="branch_ssd")
    br = _mm(y_ret, w_br, mode="nn", out_dtype=ACT_DTYPE, tm=tm, tn=d, tk=RET_HEADS * RET_V, name="branch_ret")
    merged = _merge_fwd(bs, br, gates)
    h1 = _mm(merged, w_o, mode="nn", out_dtype=F32, tm=tm, tn=d, tk=d, name="out_proj", add=h0)
    u2 = _rms_fwd(h1, ffn_norm_w, "rms2")
    up_g, up_v, act = _ffn_up_conv(u2, w_up_f, fconv_w, ffn_conv_b)
    h2 = _mm(act, w_dn, mode="nn", out_dtype=F32, tm=tm, tn=d, tk=d_ff, name="ffn_down", add=h1)
    tgt = jnp.pad(loss_target[0], ((PAD_ROWS, 0), (0, 0)))
    dh2, loss_acc, g_final = _loss_head(h2, tgt, final_norm_w.reshape(1, d))

    tff = _pick(d_ff, (1408, 256))
    tkr = _pick(rows, (1664, 128))
    tkr2 = _pick(rows, (4160, 128))
    rparts = lambda t: t.reshape(N_DEV, -1, t.shape[1])
    d_act = _mm(dh2, w_dn, mode="nt", out_dtype=ACT_DTYPE, tm=tm, tn=tff, tk=d, name="d_act")
    g_w_dn = _mm(act, dh2, mode="tn", out_dtype=COMM_DTYPE, tm=tff, tn=d, tk=tkr, name="g_w_down")
    c_dn = [rparts(g_w_dn)]
    h_dn, t_dn = _exchange_start(c_dn, True, "scatter_down_start")
    d_up_g, d_up_v, g_fcw_g, g_fcb_g, g_fcw_v, g_fcb_v, g_w_up_g, g_w_up_v = _ffn_conv_bwd(
        up_g, up_v, d_act, fconv_w, ffn_conv_b + t_dn[0, 0], u2)
    g_fconv_w = jnp.concatenate([g_fcw_g, g_fcw_v], axis=1)
    g_fconv_b = jnp.concatenate([g_fcb_g, g_fcb_v], axis=1)
    c_up = [_shards_from_cols([g_w_up_g, g_w_up_v], [_column_plan([(0, d_ff, 0)], w_up.shape[2]),
                                                     _column_plan([(d_ff, d_ff, 0)], w_up.shape[2])], w_up.shape[2], "g_w_up_shards")]
    h_up, t_up = _exchange_start(c_up, True, "scatter_up_start")
    du2 = _mm(d_up_g, w_up_f, mode="nt", out_dtype=F32, tm=tm, tn=d, tk=d_ff, name="d_u2_gate", after=t_up)
    du2 = _mm(d_up_v, w_up_f, mode="nt", out_dtype=F32, tm=tm, tn=d, tk=d_ff, name="d_u2_value", add=du2, b_k0=d_ff)
    dh1, g_ffn_norm = _rms_bwd(du2, h1, ffn_norm_w, dh2, "rms2_bwd")
    d_merged = _mm(dh1, w_o, mode="nt", out_dtype=F32, tm=tm, tn=d, tk=d, name="d_merged")
    g_w_o = _mm(merged, dh1, mode="tn", out_dtype=COMM_DTYPE, tm=d, tn=d, tk=tkr, name="g_w_out")
    dproj = lax.empty((rows, in_p), ACT_DTYPE)
    d_bs, d_br, dproj = _merge_bwd(d_merged, bs, br, gates, dproj, g0)
    d_yssd = _mm(d_bs, w_bs, mode="nt", out_dtype=ACT_DTYPE, tm=tm, tn=1024, tk=d, name="d_y_ssd")
    g_w_bs = _mm(y_ssd, d_bs, mode="tn", out_dtype=COMM_DTYPE, tm=1024, tn=d, tk=tkr2, name="g_w_branch_ssd")
    d_yret = _mm(d_br, w_br, mode="nt", out_dtype=ACT_DTYPE, tm=tm, tn=1024, tk=d, name="d_y_ret")
    g_w_br = _mm(y_ret, d_br, mode="tn", out_dtype=COMM_DTYPE, tm=1024, tn=d, tk=tkr2, name="g_w_branch_ret")
    c_mid = [rparts(g_w_bs), rparts(g_w_br), rparts(g_w_o)]
    h_mid, t_mid = _exchange_start(c_mid, True, "scatter_mid_start")
    d_yscan, dproj, g_nw = _ssd_gate_bwd(y_scan, z, d_yssd, ssd_norm_w + t_mid[0, 0], dproj, z0)
    dxs, d_bm, d_cm, dproj, g_bias_p, g_alog_p, g_dsk_p = _ssd_bwd(
        xbc_c, dtr, bias_p, alog_p, dsk_p, ssd_states, d_yscan, dproj, dt0)
    dproj, g_conv_w, g_conv_b = _ssd_conv_bwd(xbc, dxs, d_bm, d_cm, conv_w, ssd_conv_b, dproj, x0)
    dproj = _ret_bwd(qkvg, cos, sin, lgam, ret_states, d_yret, dproj)
    g_w_in_p = _mm(u1, dproj, mode="tn", out_dtype=F32, tm=d, tn=_pick(in_p, (768, 512)), tk=tkr2, name="g_w_in")
    c_in = [_shards_from_cols([g_w_in_p], [in_plan], w_in.shape[2], "g_w_in_shards")]
    h_in, t_in = _exchange_start(c_in, True, "scatter_in_start")
    du1 = _mm(dproj, w_in_p, mode="nt", out_dtype=F32, tm=tm, tn=d, tk=_pick(in_p, (4608, 512)), name="d_u1", after=t_in)
    dh0, g_mix_norm = _rms_bwd(du1, h0, mix_norm_w, dh1, "rms1_bwd")
    grad_x = dh0[PAD_ROWS:][None]

    landed = {}
    for key, handle, names in (("in", h_in, ["w_in"]), ("mid", h_mid, ["w_branch_ssd", "w_branch_ret", "w_out"]),
                               ("up", h_up, ["w_up"]), ("down", h_dn, ["w_down"])):
        srcs, lands = _exchange_wait(handle, True, dh0, f"scatter_{key}_wait")
        for nm, land, src in zip(names, lands, srcs, strict=True):
            landed[nm] = (lax.dynamic_index_in_dim(src, me, 0, keepdims=False), land)
    big_m = [m_w_in, m_w_branch_ssd, m_w_branch_ret, m_w_out, m_w_up, m_w_down]
    big_v = [v_w_in, v_w_branch_ssd, v_w_branch_ret, v_w_out, v_w_up, v_w_down]
    big_names = ["w_in", "w_branch_ssd", "w_branch_ret", "w_out", "w_up", "w_down"]
    big_out = {}
    for nm, w, m, v_ in zip(big_names, big, big_m, big_v, strict=True):
        big_out[nm] = [t[None] for t in _adam_shard(*landed[nm], w, m[0], v_[0], "adam_" + nm)]

    small_g = [dh0[FRONT:PAD_ROWS], g_mix_norm, g_conv_w, g_conv_b, _ungroup_lanes(g_bias_p), _ungroup_lanes(g_alog_p),
               _ungroup_lanes(g_dsk_p), g_nw, g_ffn_norm, g_fconv_w, g_fconv_b, g_final, loss_acc[0:1, 0:1]]
    total = _unpack(_allreduce_small(_pack(small_g)), [t.shape for t in small_g])
    loss = total[12].reshape(())
    shard = lambda t, width: lax.dynamic_slice_in_dim(t, me * width, width, axis=1)
    small_names = ["meta_tokens", "mix_norm_w", "ssd_conv_w", "ssd_conv_b", "ssd_dt_bias", "ssd_A_log", "ssd_D", "ssd_norm_w",
                   "ffn_norm_w", "ffn_conv_w", "ffn_conv_b", "final_norm_w"]
    small_w = [meta_tokens, mix_norm_w, ssd_conv_w, ssd_conv_b, ssd_dt_bias, ssd_A_log, ssd_D, ssd_norm_w, ffn_norm_w,
               ffn_conv_w, ffn_conv_b, final_norm_w]
    small_m = [m_meta_tokens, m_mix_norm_w, m_ssd_conv_w, m_ssd_conv_b, m_ssd_dt_bias, m_ssd_A_log, m_ssd_D, m_ssd_norm_w,
               m_ffn_norm_w, m_ffn_conv_w, m_ffn_conv_b, m_final_norm_w]
    small_v = [v_meta_tokens, v_mix_norm_w, v_ssd_conv_w, v_ssd_conv_b, v_ssd_dt_bias, v_ssd_A_log, v_ssd_D, v_ssd_norm_w,
               v_ffn_norm_w, v_ffn_conv_w, v_ffn_conv_b, v_final_norm_w]
    grads = total[:12]
    grads[0] = shard(grads[0], meta_tokens.shape[1])
    grads[2] = shard(grads[2], ssd_conv_w.shape[2])
    grads[9] = shard(grads[9], ffn_conv_w.shape[2])
    grads = [t.reshape(w.shape) for t, w in zip(grads, small_w, strict=True)]
    shapes = [w.shape for w in small_w]
    upd = _adam_small(_pack(small_w), _pack(grads), _pack(small_m), _pack(small_v))
    small_out = {nm: [gr_] + [u[i] for u in (_unpack(t, shapes) for t in upd)]
                 for i, (nm, gr_) in enumerate(zip(small_names, grads, strict=True))}

    order = ["meta_tokens", "mix_norm_w", "w_in", "ssd_conv_w", "ssd_conv_b", "ssd_dt_bias", "ssd_A_log", "ssd_D", "ssd_norm_w",
             "w_branch_ssd", "w_branch_ret", "w_out", "ffn_norm_w", "w_up", "ffn_conv_w", "ffn_conv_b", "w_down", "final_norm_w"]
    res = {**big_out, **small_out}
    return (loss, grad_x, *[res[nm][0] for nm in order], *[res[nm][1] for nm in order], *[res[nm][2] for nm in order],
            *[res[nm][3] for nm in order])
```

```python
import functools
import math

import jax
import jax.numpy as jnp
import numpy as np
from jax import lax
from jax.experimental import pallas as pl
from jax.experimental.pallas import tpu as pltpu

F32 = jnp.float32
MXU_DTYPE = jnp.bfloat16
ACT_DTYPE = jnp.bfloat16
COMM_DTYPE = jnp.bfloat16

N_META = 16
CHUNK = 128
FRONT = CHUNK - N_META
PAD_ROWS = FRONT + N_META
EPS = 1e-6
N_DEV = 8

SSD_D_INNER = 2048
SSD_HEAD_DIM = 64
SSD_HEADS = 32
SSD_GROUPS = 4
SSD_HPG = SSD_HEADS // SSD_GROUPS
SSD_STATE = 128
SSD_CONV = 4
SSD_CONV_DIM = SSD_D_INNER + 2 * SSD_GROUPS * SSD_STATE
SSD_GW = SSD_D_INNER // SSD_GROUPS
RET_HEADS = 4
RET_QK = 256
RET_V = 512
RET_HW = 2 * RET_QK + 2 * RET_V
ROPE_BASE = 10000.0
FFN_CONV = 3
HALO = 16
LANES = 128

ADAM_LR = 0.001
ADAM_B1 = 0.9
ADAM_B2 = 0.999
ADAM_EPS = 1e-08
ADAM_WD = 0.01
ADAM_STEP = 10

VMEM_LIMIT = 56 * 1024 * 1024
MESH = pl.DeviceIdType.MESH

NN = (((1,), (0,)), ((), ()))
NT = (((1,), (1,)), ((), ()))
TN = (((0,), (0,)), ((), ()))


def _params(sem):
    return pltpu.CompilerParams(dimension_semantics=sem, vmem_limit_bytes=VMEM_LIMIT)


def _mxu(a, b, dn):
    return lax.dot_general(a.astype(MXU_DTYPE), b.astype(MXU_DTYPE), dn, preferred_element_type=F32)


@functools.partial(jax.custom_vjp, nondiff_argnums=(2,))
def _dot(a, b, dn=NN):
    return _mxu(a, b, dn)


def _dot_fwd(a, b, dn):
    return _mxu(a, b, dn), (a, b)


def _dot_bwd(dn, res, g):
    a, b = res
    if dn == NN:
        return _mxu(g, b, NT), _mxu(a, g, TN)
    if dn == NT:
        return _mxu(g, b, NN), _mxu(g, a, TN)
    assert dn == TN
    return _mxu(b, g, NT), _mxu(a, g, NN)


_dot.defvjp(_dot_fwd, _dot_bwd)


def _silu(x):
    return x * jax.nn.sigmoid(x)


def _dsilu(x):
    s = jax.nn.sigmoid(x)
    return s * (1.0 + x * (1.0 - s))


def _row_tile(rows):
    return 640 if rows % 640 == 0 else 128


def _mm(a, b, *, mode, out_dtype, tm, tn, tk, name, add=None, after=None, b_k0=0, b_n0=0, n_out=None):
    if mode == "nn":
        (m, k), k2 = a.shape, b.shape[0]
        n = b.shape[1] if n_out is None else n_out
        assert b_n0 % tn == 0 and b_n0 + n <= b.shape[1]
    elif mode == "nt":
        (m, k), n = a.shape, b.shape[0]
        k2 = k if b_k0 % tk == 0 and b_k0 + k <= b.shape[1] else None
    else:
        (k, m), (k2, n) = a.shape, b.shape
    assert (b_k0 == 0 or mode == "nt") and ((b_n0 == 0 and n_out is None) or mode == "nn")
    assert k == k2 and m % tm == 0 and n % tn == 0 and k % tk == 0, (name, a.shape, b.shape, tm, tn, tk)
    kb0, nb0 = b_k0 // tk, b_n0 // tn
    nk = k // tk
    dn = {"nn": NN, "nt": NT, "tn": TN}[mode]
    has_add = add is not None
    n_in = 2 + has_add + (after is not None)

    def body(*refs):
        a_ref, b_ref = refs[0], refs[1]
        add_ref = refs[2] if has_add else None
        o_ref = refs[n_in]
        p = _dot(a_ref[...], b_ref[...], dn)
        if nk == 1:
            if has_add:
                p = p + add_ref[...]
            o_ref[...] = p.astype(out_dtype)
        else:
            acc_ref = refs[n_in + 1]
            kk = pl.program_id(2)

            @pl.when(kk == 0)
            def _():
                acc_ref[...] = p

            @pl.when(kk > 0)
            def _():
                acc_ref[...] += p

            @pl.when(kk == nk - 1)
            def _():
                r = acc_ref[...]
                if has_add:
                    r = r + add_ref[...]
                o_ref[...] = r.astype(out_dtype)

    if mode == "tn":
        a_spec = pl.BlockSpec((tk, tm), lambda j, i, kk: (kk, i))
    else:
        a_spec = pl.BlockSpec((tm, tk), lambda j, i, kk: (i, kk))
    if mode == "nt":
        b_spec = pl.BlockSpec((tn, tk), lambda j, i, kk: (j, kk + kb0))
    else:
        b_spec = pl.BlockSpec((tk, tn), lambda j, i, kk: (kk, j + nb0))
    o_spec = pl.BlockSpec((tm, tn), lambda j, i, kk: (i, j))
    in_specs = [a_spec, b_spec] + ([o_spec] if has_add else [])
    args = (a, b) + ((add,) if has_add else ())
    if after is not None:
        in_specs.append(pl.BlockSpec(memory_space=pl.ANY))
        args += (after,)
    return pl.pallas_call(
        body, name=name, grid=(n // tn, m // tm, nk), in_specs=in_specs, out_specs=o_spec,
        out_shape=jax.ShapeDtypeStruct((m, n), out_dtype),
        scratch_shapes=[pltpu.VMEM((tm, tn), F32)] if nk > 1 else [],
        compiler_params=_params(("parallel", "parallel", "arbitrary")),
    )(*args)


def _pick(n, cands):
    for c in cands:
        if n % c == 0:
            return c
    return n


def _rms_fwd(h, w, name):
    rows, d = h.shape
    tm = _row_tile(rows)

    def body(h_ref, w_ref, u_ref):
        x = h_ref[...]
        r = lax.rsqrt(jnp.mean(x * x, axis=-1, keepdims=True) + EPS)
        u_ref[...] = (x * r * w_ref[...]).astype(ACT_DTYPE)

    return pl.pallas_call(
        body, name=name, grid=(rows // tm,),
        in_specs=[pl.BlockSpec((tm, d), lambda i: (i, 0)), pl.BlockSpec((1, d), lambda i: (0, 0))],
        out_specs=pl.BlockSpec((tm, d), lambda i: (i, 0)),
        out_shape=jax.ShapeDtypeStruct((rows, d), ACT_DTYPE),
        compiler_params=_params(("parallel",)),
    )(h, w)


def _rms_bwd(du, h, w, dres, name):
    rows, d = h.shape
    tm = _row_tile(rows)

    def body(du_ref, h_ref, w_ref, dres_ref, dh_ref, dw_ref):
        x = h_ref[...]
        dy = du_ref[...].astype(F32)
        r = lax.rsqrt(jnp.mean(x * x, axis=-1, keepdims=True) + EPS)
        xhat = x * r
        dxn = dy * w_ref[...]
        dx = r * (dxn - xhat * jnp.mean(dxn * xhat, axis=-1, keepdims=True))
        dh_ref[...] = dres_ref[...] + dx

        @pl.when(pl.program_id(0) == 0)
        def _():
            dw_ref[...] = jnp.zeros_like(dw_ref)

        dw_ref[...] += jnp.sum(dy * xhat, axis=0, keepdims=True)

    return pl.pallas_call(
        body, name=name, grid=(rows // tm,),
        in_specs=[pl.BlockSpec((tm, d), lambda i: (i, 0)), pl.BlockSpec((tm, d), lambda i: (i, 0)),
                  pl.BlockSpec((1, d), lambda i: (0, 0)), pl.BlockSpec((tm, d), lambda i: (i, 0))],
        out_specs=[pl.BlockSpec((tm, d), lambda i: (i, 0)), pl.BlockSpec((1, d), lambda i: (0, 0))],
        out_shape=[jax.ShapeDtypeStruct((rows, d), F32), jax.ShapeDtypeStruct((1, d), F32)],
        compiler_params=_params(("arbitrary",)),
    )(du, h, w, dres)


def _loss_head(h2, tgt, w):
    rows, d = h2.shape
    tm = _row_tile(rows)

    def body(h_ref, t_ref, w_ref, dh_ref, loss_ref, dw_ref):
        i = pl.program_id(0)
        x = h_ref[...]
        r = lax.rsqrt(jnp.mean(x * x, axis=-1, keepdims=True) + EPS)
        xhat = x * r
        wv = w_ref[...]
        row = i * tm + lax.broadcasted_iota(jnp.int32, (tm, 1), 0)
        live = row >= PAD_ROWS
        diff = jnp.where(live, xhat * wv - t_ref[...], 0.0)
        dy = diff * (1.0 / d)
        dxn = dy * wv
        dh_ref[...] = r * (dxn - xhat * jnp.mean(dxn * xhat, axis=-1, keepdims=True))

        @pl.when(i == 0)
        def _():
            loss_ref[...] = jnp.zeros_like(loss_ref)
            dw_ref[...] = jnp.zeros_like(dw_ref)

        loss_ref[...] += 0.5 * jnp.sum(jnp.mean(diff * diff, axis=-1, keepdims=True))
        dw_ref[...] += jnp.sum(dy * xhat, axis=0, keepdims=True)

    return pl.pallas_call(
        body, name="loss_head", grid=(rows // tm,),
        in_specs=[pl.BlockSpec((tm, d), lambda i: (i, 0)), pl.BlockSpec((tm, d), lambda i: (i, 0)),
                  pl.BlockSpec((1, d), lambda i: (0, 0))],
        out_specs=[pl.BlockSpec((tm, d), lambda i: (i, 0)), pl.BlockSpec((8, LANES), lambda i: (0, 0)),
                   pl.BlockSpec((1, d), lambda i: (0, 0))],
        out_shape=[jax.ShapeDtypeStruct((rows, d), F32), jax.ShapeDtypeStruct((8, LANES), F32),
                   jax.ShapeDtypeStruct((1, d), F32)],
        compiler_params=_params(("arbitrary",)),
    )(h2, tgt, w)


def _prev_halo_spec(tm, width, col):
    return pl.BlockSpec((HALO, width), lambda j, i: (jnp.maximum(i * (tm // HALO) - 1, 0), col(j)))


def _next_halo_spec(tm, rows, width, col):
    last = rows // HALO - 1
    return pl.BlockSpec((HALO, width), lambda j, i: (jnp.minimum((i + 1) * (tm // HALO), last), col(j)))


def _conv_taps(cat, w_ref, b_ref, kw):
    acc = b_ref[...] + w_ref[kw - 1:kw, :] * cat
    for s in range(1, kw):
        acc = acc + w_ref[kw - 1 - s:kw - s, :] * pltpu.roll(cat, s, 0)
    return acc


def _conv_back(dpre, w_ref, kw):
    n = dpre.shape[0]
    acc = w_ref[kw - 1:kw, :] * dpre
    for s in range(1, kw):
        acc = acc + w_ref[kw - 1 - s:kw - s, :] * pltpu.roll(dpre, n - s, 0)
    return acc


def _xbc_proj_conv(u1, w_in_p, col0, w, b):
    rows, d = u1.shape
    width = w.shape[1]
    tm, tc = _row_tile(rows), 512
    cb0 = col0 // tc
    assert col0 % tc == 0

    def body(u_ref, m_ref, w_ref, b_ref, x_ref, o_ref, carry):
        i = pl.program_id(1)

        @pl.when(i == 0)
        def _():
            carry[...] = jnp.zeros_like(carry)

        xb = _mxu(u_ref[...], m_ref[...], NN).astype(ACT_DTYPE)
        x_ref[...] = xb
        x = xb.astype(F32)
        cat = jnp.concatenate([carry[...], x], axis=0)
        carry[...] = x[tm - HALO:, :]
        pre = _conv_taps(cat, w_ref, b_ref, SSD_CONV)[HALO:]
        row = i * tm + lax.broadcasted_iota(jnp.int32, (tm, 1), 0)
        o_ref[...] = jnp.where(row >= FRONT, _silu(pre), 0.0).astype(ACT_DTYPE)

    main = pl.BlockSpec((tm, tc), lambda j, i: (i, j))
    par = lambda r: pl.BlockSpec((r, tc), lambda j, i: (0, j))
    act = jax.ShapeDtypeStruct((rows, width), ACT_DTYPE)
    return pl.pallas_call(
        body, name="xbc_proj_conv", grid=(width // tc, rows // tm),
        in_specs=[pl.BlockSpec((tm, d), lambda j, i: (i, 0)), pl.BlockSpec((d, tc), lambda j, i: (0, cb0 + j)),
                  par(SSD_CONV), par(1)],
        out_specs=[main, main], out_shape=[act, act], scratch_shapes=[pltpu.VMEM((HALO, tc), F32)],
        compiler_params=_params(("parallel", "arbitrary")),
    )(u1, w_in_p, w, b)


def _ssd_conv_bwd(xbc, dxs, dbm, dcm, w, b, dproj, col0):
    rows, width = xbc.shape
    tm, tc = _row_tile(rows), 512
    kw = SSD_CONV
    nx = dxs.shape[1] // tc
    assert dbm.shape[1] == tc and dcm.shape[1] == tc and width == (nx + 2) * tc and col0 % tc == 0

    def body(x_ref, xp_ref, xn_ref, d0_ref, d0n_ref, d1_ref, d1n_ref, d2_ref, d2n_ref, w_ref, b_ref, dp_ref,
             dx_ref, dw_ref, db_ref):
        j, i = pl.program_id(0), pl.program_id(1)
        xp = jnp.where(i == 0, 0.0, xp_ref[...].astype(F32))
        cat = jnp.concatenate([xp, x_ref[...].astype(F32), xn_ref[...].astype(F32)], axis=0)
        sh = [cat] + [pltpu.roll(cat, s, 0) for s in range(1, kw)]
        pre = b_ref[...] + w_ref[kw - 1:kw, :] * sh[0]
        for s in range(1, kw):
            pre = pre + w_ref[kw - 1 - s:kw - s, :] * sh[s]
        pre = pre[HALO:]
        row = i * tm + lax.broadcasted_iota(jnp.int32, (tm + HALO, 1), 0)
        live = (row >= FRONT) & (row < rows)
        pick = lambda a, bb, c: jnp.where(j < nx, a[...], jnp.where(j == nx, bb[...], c[...])).astype(F32)
        dout = jnp.concatenate([pick(d0_ref, d1_ref, d2_ref), pick(d0n_ref, d1n_ref, d2n_ref)], axis=0)
        dpre = jnp.where(live, dout * _dsilu(pre), 0.0)
        dx_ref[...] = _conv_back(dpre, w_ref, kw)[:tm].astype(ACT_DTYPE)

        @pl.when(i == 0)
        def _():
            dw_ref[...] = jnp.zeros_like(dw_ref)
            db_ref[...] = jnp.zeros_like(db_ref)

        dmain = dpre[:tm]
        db_ref[...] += jnp.sum(dmain, axis=0, keepdims=True)
        for k in range(kw):
            dw_ref[k:k + 1, :] += jnp.sum(dmain * sh[kw - 1 - k][HALO:HALO + tm], axis=0, keepdims=True)

    main = pl.BlockSpec((tm, tc), lambda j, i: (i, j))
    par = lambda r: pl.BlockSpec((r, tc), lambda j, i: (0, j))
    col = lambda j: j
    xcol, zero = (lambda j: jnp.minimum(j, nx - 1)), (lambda j: 0)
    dspecs = lambda c: [pl.BlockSpec((tm, tc), lambda j, i: (i, c(j))), _next_halo_spec(tm, rows, tc, c)]
    return pl.pallas_call(
        body, name="ssd_conv_bwd", grid=(width // tc, rows // tm),
        in_specs=[main, _prev_halo_spec(tm, tc, col), _next_halo_spec(tm, rows, tc, col)]
        + dspecs(xcol) + dspecs(zero) + dspecs(zero) + [par(kw), par(1), pl.BlockSpec(memory_space=pl.ANY)],
        out_specs=[pl.BlockSpec((tm, tc), lambda j, i: (i, col0 // tc + j)), par(kw), par(1)],
        out_shape=[jax.ShapeDtypeStruct(dproj.shape, dproj.dtype), jax.ShapeDtypeStruct((kw, width), F32),
                   jax.ShapeDtypeStruct((1, width), F32)],
        input_output_aliases={11: 0}, compiler_params=_params(("parallel", "arbitrary")),
    )(xbc, xbc, xbc, dxs, dxs, dbm, dbm, dcm, dcm, w, b, dproj)


def _ffn_up_conv(u2, w_up, w, b):
    rows, d = u2.shape
    width = w_up.shape[1]
    dff = width // 2
    tm, tc = _row_tile(rows), _pick(dff, (256, 128))
    nb = dff // tc
    kw = FFN_CONV

    def body(u_ref, mg_ref, mv_ref, wg_ref, bg_ref, wv_ref, bv_ref, ug_ref, uv_ref, o_ref, cg, cv):
        i = pl.program_id(1)

        @pl.when(i == 0)
        def _():
            cg[...] = jnp.zeros_like(cg)
            cv[...] = jnp.zeros_like(cv)

        def pre(m_ref, up_ref, carry, w_ref, b_ref):
            upb = _mxu(u_ref[...], m_ref[...], NN).astype(ACT_DTYPE)
            up_ref[...] = upb
            x = upb.astype(F32)
            cat = jnp.concatenate([carry[...], x], axis=0)
            carry[...] = x[tm - HALO:, :]
            return _conv_taps(cat, w_ref, b_ref, kw)[HALO:]

        ag = pre(mg_ref, ug_ref, cg, wg_ref, bg_ref)
        av = pre(mv_ref, uv_ref, cv, wv_ref, bv_ref)
        o_ref[...] = (_silu(ag) * av).astype(ACT_DTYPE)

    gcol, vcol = (lambda j: j), (lambda j: j + nb)
    mat = lambda col: pl.BlockSpec((d, tc), lambda j, i: (0, col(j)))
    par = lambda r, col: pl.BlockSpec((r, tc), lambda j, i: (0, col(j)))
    out = pl.BlockSpec((tm, tc), lambda j, i: (i, j))
    act = jax.ShapeDtypeStruct((rows, dff), ACT_DTYPE)
    return pl.pallas_call(
        body, name="ffn_up_conv", grid=(nb, rows // tm),
        in_specs=[pl.BlockSpec((tm, d), lambda j, i: (i, 0)), mat(gcol), mat(vcol),
                  par(kw, gcol), par(1, gcol), par(kw, vcol), par(1, vcol)],
        out_specs=[out, out, out], out_shape=[act, act, act],
        scratch_shapes=[pltpu.VMEM((HALO, tc), F32), pltpu.VMEM((HALO, tc), F32)],
        compiler_params=_params(("parallel", "arbitrary")),
    )(u2, w_up, w_up, w, b, w, b)


def _ffn_conv_bwd(up_g, up_v, dact, w, b, u2):
    rows, dff = up_g.shape
    d = u2.shape[1]
    tm, tc = _row_tile(rows), _pick(dff, (256, 128))
    nb = dff // tc
    kw = FFN_CONV

    sb = 16

    def body(g_ref, gp_ref, gn_ref, v_ref, vp_ref, vn_ref, d_ref, dn_ref, wg_ref, bg_ref, wv_ref, bv_ref, u_ref,
             dxg_ref, dxv_ref, dwg_ref, dbg_ref, dwv_ref, dbv_ref, gwg_ref, gwv_ref, xg_s, xv_s, dd_s, og_s, ov_s):
        i = pl.program_id(1)
        last = i == rows // tm - 1
        for x_s, x_ref, xp_ref, xn_ref in ((xg_s, g_ref, gp_ref, gn_ref), (xv_s, v_ref, vp_ref, vn_ref)):
            x_s[0:HALO, :] = jnp.where(i == 0, 0.0, xp_ref[...].astype(F32))
            x_s[HALO:HALO + tm, :] = x_ref[...].astype(F32)
            x_s[HALO + tm:, :] = xn_ref[...].astype(F32)
        dd_s[0:tm, :] = d_ref[...].astype(F32)
        dd_s[tm:, :] = jnp.where(last, 0.0, dn_ref[...].astype(F32))

        wg = [wg_ref[k:k + 1, :] for k in range(kw)]
        wv = [wv_ref[k:k + 1, :] for k in range(kw)]
        bg, bv = bg_ref[...], bv_ref[...]

        def taps(x_s, e0, w, bias):
            win = x_s[pl.ds(e0 + HALO - sb, 2 * sb), :]
            sh = [win[sb:], pltpu.roll(win, 1, 0)[sb:], pltpu.roll(win, 2, 0)[sb:]]
            return bias + w[2] * sh[0] + w[1] * sh[1] + w[0] * sh[2], sh

        def dpre_of(e0):
            ag, sh_g = taps(xg_s, e0, wg, bg)
            av, sh_v = taps(xv_s, e0, wv, bv)
            dout = dd_s[pl.ds(e0, sb), :]
            s = jax.nn.sigmoid(ag)
            silu = ag * s
            return dout * av * (s + silu * (1.0 - s)), dout * silu, sh_g, sh_v

        def back(dp, nxt, w):
            cat = jnp.concatenate([dp, nxt], axis=0)
            return w[2] * dp + w[1] * pltpu.roll(cat, 2 * sb - 1, 0)[:sb] + w[0] * pltpu.roll(cat, 2 * sb - 2, 0)[:sb]

        nxt_g, nxt_v, _, _ = dpre_of(tm)
        acc_g = acc_v = tuple(jnp.zeros((sb, tc), F32) for _ in range(kw + 1))
        for e0 in range(tm - sb, -1, -sb):
            dpg, dpv, sh_g, sh_v = dpre_of(e0)
            og_s[e0:e0 + sb, :] = back(dpg, nxt_g, wg)
            ov_s[e0:e0 + sb, :] = back(dpv, nxt_v, wv)
            acc_g = tuple(a + dpg * t for a, t in zip(acc_g, (sh_g[2], sh_g[1], sh_g[0], 1.0)))
            acc_v = tuple(a + dpv * t for a, t in zip(acc_v, (sh_v[2], sh_v[1], sh_v[0], 1.0)))
            nxt_g, nxt_v = dpg, dpv

        @pl.when(i == 0)
        def _():
            for r in (dwg_ref, dbg_ref, dwv_ref, dbv_ref, gwg_ref, gwv_ref):
                r[...] = jnp.zeros_like(r)

        for acc, o_s, dx_ref, dw_ref, db_ref, gw_ref in ((acc_g, og_s, dxg_ref, dwg_ref, dbg_ref, gwg_ref),
                                                        (acc_v, ov_s, dxv_ref, dwv_ref, dbv_ref, gwv_ref)):
            dx = o_s[...].astype(ACT_DTYPE)
            dx_ref[...] = dx
            gw_ref[...] += _mxu(u_ref[...], dx, TN)
            for k in range(kw):
                dw_ref[k:k + 1, :] += jnp.sum(acc[k], axis=0, keepdims=True)
            db_ref[...] += jnp.sum(acc[kw], axis=0, keepdims=True)

    gcol, vcol = (lambda j: j), (lambda j: j + nb)
    main = lambda col: pl.BlockSpec((tm, tc), lambda j, i: (i, col(j)))
    par = lambda r, col: pl.BlockSpec((r, tc), lambda j, i: (0, col(j)))
    halos = lambda col: [_prev_halo_spec(tm, tc, col), _next_halo_spec(tm, rows, tc, col)]
    act_shape = jax.ShapeDtypeStruct((rows, dff), ACT_DTYPE)
    par_shapes = [jax.ShapeDtypeStruct((kw, dff), F32), jax.ShapeDtypeStruct((1, dff), F32)]
    gw_shape = jax.ShapeDtypeStruct((d, dff), F32)
    return pl.pallas_call(
        body, name="ffn_conv_bwd", grid=(nb, rows // tm),
        in_specs=[main(gcol)] + halos(gcol) + [main(gcol)] + halos(gcol) + [main(gcol), _next_halo_spec(tm, rows, tc, gcol),
                  par(kw, gcol), par(1, gcol), par(kw, vcol), par(1, vcol), pl.BlockSpec((tm, d), lambda j, i: (i, 0))],
        out_specs=[main(gcol), main(gcol), par(kw, gcol), par(1, gcol), par(kw, gcol), par(1, gcol), par(d, gcol), par(d, gcol)],
        out_shape=[act_shape, act_shape] + par_shapes + par_shapes + [gw_shape, gw_shape],
        scratch_shapes=[pltpu.VMEM((tm + 2 * HALO, tc), F32), pltpu.VMEM((tm + 2 * HALO, tc), F32),
                        pltpu.VMEM((tm + HALO, tc), F32), pltpu.VMEM((tm, tc), F32), pltpu.VMEM((tm, tc), F32)],
        compiler_params=_params(("parallel", "arbitrary")),
    )(up_g, up_g, up_g, up_v, up_v, up_v, dact, dact, w, b, w, b, u2)


def _ssd_scalars(dtr, dt_bias, a_log, live):
    q = CHUNK
    pre = dtr + dt_bias
    dt = jnp.where(live, jax.nn.softplus(pre), 0.0)
    a_neg = -jnp.exp(a_log)
    li = lax.broadcasted_iota(jnp.int32, (q, q), 0)
    si = lax.broadcasted_iota(jnp.int32, (q, q), 1)
    causal = li >= si
    tri = jnp.where(causal, 1.0, 0.0).astype(F32)
    a_cs = sum(_mxu(tri, p, NN) for p in _split(dt * a_neg, 3))
    return pre, dt, a_neg, a_cs, causal, tri


def _head_select():
    r = lax.broadcasted_iota(jnp.int32, (LANES, SSD_GW), 0)
    c = lax.broadcasted_iota(jnp.int32, (LANES, SSD_GW), 1)
    return jnp.where(c // SSD_HEAD_DIM == r, 1.0, 0.0).astype(MXU_DTYPE)


def _split(t, parts):
    out, rem = [], t
    for _ in range(parts):
        p = rem.astype(MXU_DTYPE)
        out.append(p)
        rem = rem - p.astype(F32)
    return out


def _stacked(ts, parts, sel, dn):
    out = _mxu(jnp.concatenate([p for t in ts for p in _split(t, parts)], axis=0), sel, dn)
    res, r0 = [], 0
    for t in ts:
        r = t.shape[0]
        res.append(sum(out[r0 + k * r:r0 + (k + 1) * r] for k in range(parts)))
        r0 += parts * r
    return res


def _head_cols(ts, sel):
    return _stacked(ts, 2, sel, NN)


def _head_sums(ts, sel):
    return _stacked(ts, 3, sel, NT)


def _half_masks():
    lane = lax.broadcasted_iota(jnp.int32, (CHUNK, LANES), 1)
    return lane < SSD_HEAD_DIM, lane >= SSD_HEAD_DIM


def _ssd_scan(xs, bm, cm, dtr, prev, dt_bias, a_log, d_skip, live):
    q = CHUNK
    sel = _head_select()
    _, dt, _, a_cs, causal, _ = _ssd_scalars(dtr, dt_bias, a_log, live)
    a_cs_t = a_cs.T
    a_end = a_cs[q - 1:q, :]
    dt_x, e_x, w_x, d_x = _head_cols([dt, jnp.exp(a_cs), jnp.exp(a_end - a_cs), jnp.broadcast_to(d_skip, (16, LANES))], sel)
    xdt = xs * dt_x
    cb = _dot(cm, bm, NT)
    y = _dot(cm, prev) * e_x + d_x[0:1] * xs
    new = prev * e_x[q - 1:q, :] + _dot(bm, xdt * w_x, TN)
    masks = _half_masks()
    ys = []
    for pp in range(SSD_HPG // 2):
        xpair = xdt[:, pp * LANES:(pp + 1) * LANES]
        acc = jnp.zeros((q, LANES), F32)
        for half in range(2):
            hh = 2 * pp + half
            decay = jnp.exp(jnp.where(causal, a_cs[:, hh:hh + 1] - a_cs_t[hh:hh + 1, :], -jnp.inf))
            acc = acc + _dot(cb * decay, jnp.where(masks[half], xpair, 0.0))
        ys.append(acc)
    return y + jnp.concatenate(ys, axis=1), new


def _ssd_gate(y, z, nw):
    yz = y * _silu(z)
    return yz * lax.rsqrt(jnp.mean(yz * yz, axis=-1, keepdims=True) + EPS) * nw


def _ssd_scan_bwd(xs, bm, cm, dtr, prev, dt_bias, a_log, d_skip, live, dy, dnew):
    q = CHUNK
    sel = _head_select()
    pre, dt, a_neg, a_cs, causal, tri = _ssd_scalars(dtr, dt_bias, a_log, live)
    a_cs_t = a_cs.T
    a_end = a_cs[q - 1:q, :]
    dt_x, e_x, w_x, d_x = _head_cols([dt, jnp.exp(a_cs), jnp.exp(a_end - a_cs), jnp.broadcast_to(d_skip, (16, LANES))], sel)
    g_x, d_x = e_x[q - 1:q, :], d_x[0:1]
    xdt = xs * dt_x
    u = xdt * w_x
    cb = _mxu(cm, bm, NT)
    cs = _mxu(cm, prev, NN)
    dye = dy * e_x
    dcm = _mxu(dye, prev, NT)
    dprev = _mxu(cm, dye, TN) + dnew * g_x
    dacs_x = dye * cs
    dbm = _mxu(u, dnew, NT)
    du = _mxu(bm, dnew, NN)
    dw_x = du * u
    dacs_x = dacs_x - dw_x
    dend_x = jnp.sum(dw_x + dnew * prev * g_x, axis=0, keepdims=True)
    dxdt = du * w_x
    lane = lax.broadcasted_iota(jnp.int32, (q, LANES), 1)
    sub = lax.broadcasted_iota(jnp.int32, (q, LANES), 0)
    dcb = jnp.zeros((q, q), F32)
    dacs = jnp.zeros((q, LANES), F32)
    dacs_t = jnp.zeros((q, LANES), F32)
    masks = _half_masks()
    dxdt_p = []
    for pp in range(SSD_HPG // 2):
        ps = slice(pp * LANES, (pp + 1) * LANES)
        acc = jnp.zeros((q, LANES), F32)
        for half in range(2):
            hh = 2 * pp + half
            decay = jnp.exp(jnp.where(causal, a_cs[:, hh:hh + 1] - a_cs_t[hh:hh + 1, :], -jnp.inf))
            m = cb * decay
            dyh = jnp.where(masks[half], dy[:, ps], 0.0)
            dm = _mxu(dyh, xdt[:, ps], NT)
            acc = acc + _mxu(m, dyh, TN)
            dcb = dcb + dm * decay
            p = dm * m
            dacs = jnp.where(lane == hh, jnp.sum(p, axis=1, keepdims=True), dacs)
            dacs_t = jnp.where(sub == hh, jnp.sum(p, axis=0, keepdims=True), dacs_t)
        dxdt_p.append(acc)
    dcm = dcm + _mxu(dcb, bm, NN)
    dbm = dbm + _mxu(dcb, cm, TN)
    dxdt = dxdt + jnp.concatenate(dxdt_p, axis=1)
    dxs = dy * d_x + dxdt * dt_x
    rows_x = jnp.concatenate([dend_x, jnp.sum(dy * xs, axis=0, keepdims=True), jnp.zeros((14, SSD_GW), F32)], axis=0)
    dacs_h, ddt_h, rows = _head_sums([dacs_x, dxdt * xs, rows_x], sel)
    dacs = dacs - dacs_t.T + dacs_h
    dacs = dacs + jnp.where(sub == q - 1, rows[0:1], 0.0)
    tri_t = jnp.where(causal, 0.0, 1.0).astype(F32) + jnp.where(lane == sub, 1.0, 0.0)
    da = sum(_mxu(tri_t, p, NN) for p in _split(dacs, 3))
    ddt = ddt_h + da * a_neg
    dalog = jnp.sum(da * dt, axis=0, keepdims=True) * a_neg
    ddtr = jnp.where(live, ddt * jax.nn.sigmoid(pre), 0.0)
    dbias = jnp.sum(ddtr, axis=0, keepdims=True)
    return dxs, dbm, dcm, ddtr, dprev, dbias, dalog, rows[1:2]


def _chunks_per_step(nc):
    return 13 if nc % 13 == 0 else 1


def _ssd_specs(rev, nc):
    per = _chunks_per_step(nc)
    steps = nc // per
    sidx = (lambda s: steps - 1 - s) if rev else (lambda s: s)
    nb_b = SSD_D_INNER // SSD_STATE
    row = lambda width, col=lambda g: g: pl.BlockSpec((per * CHUNK, width), lambda g, s: (sidx(s), col(g)))
    par = lambda width: pl.BlockSpec((1, width), lambda g, s: (0, g))
    state = lambda: pl.BlockSpec((per, 1, SSD_STATE, SSD_GW), lambda g, s: (sidx(s), g, 0, 0))
    xbc = [row(SSD_GW), row(SSD_STATE, lambda g: nb_b + g), row(SSD_STATE, lambda g: nb_b + SSD_GROUPS + g)]
    return per, steps, sidx, row, par, state, xbc


def _ssd_fwd(xbc_c, dtr, z, dt_bias, a_log, d_skip, nw):
    rows = z.shape[0]
    nc = rows // CHUNK
    per, steps, _, row, par, state, xbc = _ssd_specs(False, nc)

    def body(xs_ref, b_ref, c_ref, dt_ref, z_ref, bias_ref, al_ref, dk_ref, nw_ref, o_ref, y_ref, st_ref, carry):
        s = pl.program_id(1)

        @pl.when(s == 0)
        def _():
            carry[...] = jnp.zeros_like(carry)

        for j in range(per):
            rs = pl.ds(j * CHUNK, CHUNK)
            live = (s * per + j) * CHUNK + lax.broadcasted_iota(jnp.int32, (CHUNK, 1), 0) >= FRONT
            prev = carry[...]
            st_ref[j, 0] = prev
            y, new = _ssd_scan(xs_ref[rs, :].astype(F32), b_ref[rs, :].astype(F32), c_ref[rs, :].astype(F32), dt_ref[rs, :],
                               prev, bias_ref[...], al_ref[...], dk_ref[...], live)
            y_ref[rs, :] = y.astype(ACT_DTYPE)
            o_ref[rs, :] = _ssd_gate(y, z_ref[rs, :].astype(F32), nw_ref[...]).astype(ACT_DTYPE)
            carry[...] = new

    act = jax.ShapeDtypeStruct((rows, SSD_D_INNER), ACT_DTYPE)
    return pl.pallas_call(
        body, name="ssd_fwd", grid=(SSD_GROUPS, steps),
        in_specs=xbc + [row(LANES), row(SSD_GW), par(LANES), par(LANES), par(LANES), par(SSD_GW)],
        out_specs=[row(SSD_GW), row(SSD_GW), state()],
        out_shape=[act, act, jax.ShapeDtypeStruct((nc, SSD_GROUPS, SSD_STATE, SSD_GW), F32)],
        scratch_shapes=[pltpu.VMEM((SSD_STATE, SSD_GW), F32)],
        compiler_params=_params(("parallel", "arbitrary")),
    )(xbc_c, xbc_c, xbc_c, dtr, z, dt_bias, a_log, d_skip, nw)


def _ssd_gate_bwd(y, z, dout, nw, dproj, col0):
    rows = y.shape[0]
    tm = _row_tile(rows)
    assert col0 % SSD_GW == 0

    def body(y_ref, z_ref, do_ref, nw_ref, dp_ref, dy_ref, dz_ref, dnw_ref):
        yv, zv, dov = y_ref[...].astype(F32), z_ref[...].astype(F32), do_ref[...].astype(F32)
        s = jax.nn.sigmoid(zv)
        silu = zv * s
        yz = yv * silu
        r = lax.rsqrt(jnp.mean(yz * yz, axis=-1, keepdims=True) + EPS)
        yhat = yz * r
        dn = dov * nw_ref[...]
        dyz = r * (dn - yhat * jnp.mean(dn * yhat, axis=-1, keepdims=True))
        dy_ref[...] = (dyz * silu).astype(ACT_DTYPE)
        dz_ref[...] = (dyz * yv * (s + silu * (1.0 - s))).astype(ACT_DTYPE)

        @pl.when(pl.program_id(1) == 0)
        def _():
            dnw_ref[...] = jnp.zeros_like(dnw_ref)

        dnw_ref[...] += jnp.sum(dov * yhat, axis=0, keepdims=True)

    spec = pl.BlockSpec((tm, SSD_GW), lambda g, i: (i, g))
    par = pl.BlockSpec((1, SSD_GW), lambda g, i: (0, g))
    act = jax.ShapeDtypeStruct((rows, SSD_D_INNER), ACT_DTYPE)
    return pl.pallas_call(
        body, name="ssd_gate_bwd", grid=(SSD_GROUPS, rows // tm),
        in_specs=[spec, spec, spec, par, pl.BlockSpec(memory_space=pl.ANY)],
        out_specs=[spec, pl.BlockSpec((tm, SSD_GW), lambda g, i: (i, col0 // SSD_GW + g)), par],
        out_shape=[act, jax.ShapeDtypeStruct(dproj.shape, dproj.dtype), jax.ShapeDtypeStruct((1, SSD_D_INNER), F32)],
        input_output_aliases={4: 1}, compiler_params=_params(("parallel", "arbitrary")),
    )(y, z, dout, nw, dproj)


def _ssd_bwd(xbc_c, dtr, dt_bias, a_log, d_skip, states, dy, dproj, col0):
    rows = dy.shape[0]
    nc = rows // CHUNK
    per, steps, sidx, row, par, state, xbc = _ssd_specs(True, nc)
    assert col0 % LANES == 0

    def body(xs_ref, b_ref, c_ref, dt_ref, bias_ref, al_ref, dk_ref, st_ref, dy_ref, dp_ref,
             dxs_ref, db_ref, dc_ref, ddt_ref, dbias_ref, dal_ref, ddk_ref, carry):
        s = pl.program_id(1)

        @pl.when(s == 0)
        def _():
            carry[...] = jnp.zeros_like(carry)
            for r in (dbias_ref, dal_ref, ddk_ref):
                r[...] = jnp.zeros_like(r)

        for j in reversed(range(per)):
            rs = pl.ds(j * CHUNK, CHUNK)
            live = (sidx(s) * per + j) * CHUNK + lax.broadcasted_iota(jnp.int32, (CHUNK, 1), 0) >= FRONT
            dxs, dbm, dcm, ddt, dprev, dbias, dal, ddk = _ssd_scan_bwd(
                xs_ref[rs, :].astype(F32), b_ref[rs, :].astype(F32), c_ref[rs, :].astype(F32), dt_ref[rs, :], st_ref[j, 0],
                bias_ref[...], al_ref[...], dk_ref[...], live, dy_ref[rs, :].astype(F32), carry[...])
            dxs_ref[rs, :] = dxs.astype(ACT_DTYPE)
            db_ref[rs, :] = dbm.astype(ACT_DTYPE)
            dc_ref[rs, :] = dcm.astype(ACT_DTYPE)
            ddt_ref[rs, :] = ddt.astype(ACT_DTYPE)
            carry[...] = dprev
            dbias_ref[...] += dbias
            dal_ref[...] += dal
            ddk_ref[...] += ddk

    bc = jax.ShapeDtypeStruct((rows, SSD_GROUPS * SSD_STATE), ACT_DTYPE)
    head = jax.ShapeDtypeStruct((1, SSD_GROUPS * LANES), F32)
    return pl.pallas_call(
        body, name="ssd_bwd", grid=(SSD_GROUPS, steps),
        in_specs=xbc + [row(LANES), par(LANES), par(LANES), par(LANES), state(), row(SSD_GW), pl.BlockSpec(memory_space=pl.ANY)],
        out_specs=[row(SSD_GW), row(SSD_STATE), row(SSD_STATE), row(LANES, lambda g: col0 // LANES + g),
                   par(LANES), par(LANES), par(LANES)],
        out_shape=[jax.ShapeDtypeStruct((rows, SSD_D_INNER), ACT_DTYPE), bc, bc,
                   jax.ShapeDtypeStruct(dproj.shape, dproj.dtype), head, head, head],
        input_output_aliases={9: 3}, scratch_shapes=[pltpu.VMEM((SSD_STATE, SSD_GW), F32)],
        compiler_params=_params(("parallel", "arbitrary")),
    )(xbc_c, xbc_c, xbc_c, dtr, dt_bias, a_log, d_skip, states, dy, dproj)


def _rotary_tables(rows):
    pos = np.arange(rows, dtype=np.float32) - np.float32(FRONT)
    inv_freq = np.float32(ROPE_BASE) ** (-np.linspace(0.0, 1.0, RET_QK // 2, dtype=np.float32))
    ang = (pos[:, None] * inv_freq[None, :]).astype(np.float32).astype(np.float64)
    lgam = np.log(1.0 - 2.0 ** (-5.0 - np.arange(RET_HEADS, dtype=np.float64))).astype(np.float32)
    lgam = np.broadcast_to(lgam[:, None, None], (RET_HEADS, 8, LANES))
    return jnp.asarray(np.cos(ang).astype(np.float32)), jnp.asarray(np.sin(ang).astype(np.float32)), jnp.asarray(lgam)


def _rotary(t, cos, sin):
    half = t.shape[-1] // 2
    t1, t2 = t[:, :half], t[:, half:]
    return jnp.concatenate([t1 * cos - t2 * sin, t2 * cos + t1 * sin], axis=1)


def _ret_chunk(qh, kh, vh, gh, prev, cos, sin, lg):
    q = CHUNK
    qr = _rotary(qh, cos, sin)
    kr = _rotary(kh, cos, sin) * (RET_QK ** -0.5)
    li = lax.broadcasted_iota(jnp.int32, (q, q), 0)
    si = lax.broadcasted_iota(jnp.int32, (q, q), 1)
    dist = (li - si).astype(F32)
    decay = jnp.exp(jnp.where(li >= si, dist * lg, -jnp.inf))
    idx = lax.broadcasted_iota(jnp.int32, (q, 1), 0).astype(F32)
    scores = _dot(qr, kr, NT) * decay
    out = _dot(scores, vh)
    kv = _dot(kr * jnp.exp((q - 1.0 - idx) * lg), vh, TN)
    out = out + _dot(qr, prev) * jnp.exp((idx + 1.0) * lg)
    new = prev * jnp.exp(q * lg) + kv
    out = out * lax.rsqrt(jnp.mean(out * out, axis=-1, keepdims=True) + EPS)
    return _silu(gh) * out, new


def _ret_specs(rev, nc):
    per = _chunks_per_step(nc)
    steps = nc // per
    sidx = (lambda s: steps - 1 - s) if rev else (lambda s: s)
    row = lambda width: pl.BlockSpec((per * CHUNK, width), lambda h, s: (sidx(s), h))
    tab = lambda: pl.BlockSpec((per * CHUNK, RET_QK // 2), lambda h, s: (sidx(s), 0))
    lgs = lambda: pl.BlockSpec((1, 8, LANES), lambda h, s: (h, 0, 0))
    state = lambda: pl.BlockSpec((per, 1, RET_QK, RET_V), lambda h, s: (sidx(s), h, 0, 0))
    part = lambda width, k: pl.BlockSpec((per * CHUNK, width), lambda h, s: (sidx(s), h * (RET_HW // width) + k))
    ins = [part(RET_QK, 0), part(RET_QK, 1), part(RET_V, 1), part(RET_V, 2), tab(), tab(), lgs()]
    return per, steps, sidx, row, state, ins


def _ret_fwd(qkvg, cos, sin, lgam):
    rows = qkvg.shape[0]
    nc = rows // CHUNK
    per, steps, _, row, state, ins = _ret_specs(False, nc)
    q = k = v = g = qkvg

    def body(q_ref, k_ref, v_ref, g_ref, cos_ref, sin_ref, lg_ref, y_ref, st_ref, carry):
        @pl.when(pl.program_id(1) == 0)
        def _():
            carry[...] = jnp.zeros_like(carry)

        for j in range(per):
            rs = pl.ds(j * CHUNK, CHUNK)
            prev = carry[...]
            st_ref[j, 0] = prev.astype(ACT_DTYPE)
            out, new = _ret_chunk(q_ref[rs, :].astype(F32), k_ref[rs, :].astype(F32), v_ref[rs, :].astype(F32),
                                  g_ref[rs, :].astype(F32), prev, cos_ref[rs, :], sin_ref[rs, :], lg_ref[0, 0:1, 0:1])
            y_ref[rs, :] = out.astype(ACT_DTYPE)
            carry[...] = new

    return pl.pallas_call(
        body, name="ret_fwd", grid=(RET_HEADS, steps), in_specs=ins, out_specs=[row(RET_V), state()],
        out_shape=[jax.ShapeDtypeStruct((rows, RET_HEADS * RET_V), ACT_DTYPE),
                   jax.ShapeDtypeStruct((nc, RET_HEADS, RET_QK, RET_V), ACT_DTYPE)],
        scratch_shapes=[pltpu.VMEM((RET_QK, RET_V), F32)],
        compiler_params=_params(("parallel", "arbitrary")),
    )(q, k, v, g, cos, sin, lgam)


def _ret_bwd(qkvg, cos, sin, lgam, states, dy, dproj):
    rows = qkvg.shape[0]
    nc = rows // CHUNK
    per, steps, sidx, row, state, ins = _ret_specs(True, nc)

    def body(q_ref, k_ref, v_ref, g_ref, cos_ref, sin_ref, lg_ref, st_ref, dy_ref, dp_ref, o_ref, carry):
        @pl.when(pl.program_id(1) == 0)
        def _():
            carry[...] = jnp.zeros_like(carry)

        for j in reversed(range(per)):
            rs = pl.ds(j * CHUNK, CHUNK)
            fn = functools.partial(_ret_chunk, cos=cos_ref[rs, :], sin=sin_ref[rs, :], lg=lg_ref[0, 0:1, 0:1])
            _, vjp = jax.vjp(fn, q_ref[rs, :].astype(F32), k_ref[rs, :].astype(F32), v_ref[rs, :].astype(F32),
                             g_ref[rs, :].astype(F32), st_ref[j, 0].astype(F32))
            dq, dk, dv, dg, dprev = vjp((dy_ref[rs, :].astype(F32), carry[...]))
            o_ref[rs, 0:RET_QK] = dq.astype(ACT_DTYPE)
            o_ref[rs, RET_QK:2 * RET_QK] = dk.astype(ACT_DTYPE)
            o_ref[rs, 2 * RET_QK:2 * RET_QK + RET_V] = dv.astype(ACT_DTYPE)
            o_ref[rs, 2 * RET_QK + RET_V:RET_HW] = dg.astype(ACT_DTYPE)
            carry[...] = dprev

    return pl.pallas_call(
        body, name="ret_bwd", grid=(RET_HEADS, steps),
        in_specs=ins + [state(), row(RET_V), pl.BlockSpec(memory_space=pl.ANY)],
        out_specs=pl.BlockSpec((per * CHUNK, RET_HW), lambda h, s: (sidx(s), h)),
        out_shape=jax.ShapeDtypeStruct(dproj.shape, dproj.dtype), input_output_aliases={9: 0},
        scratch_shapes=[pltpu.VMEM((RET_QK, RET_V), F32)],
        compiler_params=_params(("parallel", "arbitrary")),
    )(qkvg, qkvg, qkvg, qkvg, cos, sin, lgam, states, dy, dproj)


def _merge_fwd(bs, br, gates):
    rows, d = bs.shape
    tm = _row_tile(rows)

    def body(bs_ref, br_ref, gs_ref, gr_ref, o_ref):
        o_ref[...] = (jax.nn.sigmoid(gs_ref[...].astype(F32)) * bs_ref[...].astype(F32)
                      + jax.nn.sigmoid(gr_ref[...].astype(F32)) * br_ref[...].astype(F32)).astype(ACT_DTYPE)

    spec = pl.BlockSpec((tm, d), lambda i: (i, 0))
    return pl.pallas_call(
        body, name="merge_fwd", grid=(rows // tm,), in_specs=[spec, spec, spec, pl.BlockSpec((tm, d), lambda i: (i, 1))],
        out_specs=spec, out_shape=jax.ShapeDtypeStruct((rows, d), ACT_DTYPE), compiler_params=_params(("parallel",)),
    )(bs, br, gates, gates)


def _merge_bwd(dm, bs, br, gates, dproj, col0):
    rows, d = bs.shape
    tm = _row_tile(rows)
    assert col0 % (2 * d) == 0

    def body(dm_ref, bs_ref, br_ref, gs_ref, gr_ref, dp_ref, dbs_ref, dbr_ref, dg_ref):
        dmv = dm_ref[...].astype(F32)
        for k, (b_ref, g_ref, db_ref) in enumerate(((bs_ref, gs_ref, dbs_ref), (br_ref, gr_ref, dbr_ref))):
            s = jax.nn.sigmoid(g_ref[...].astype(F32))
            db_ref[...] = (dmv * s).astype(ACT_DTYPE)
            dg_ref[:, k * d:(k + 1) * d] = (dmv * b_ref[...].astype(F32) * s * (1.0 - s)).astype(ACT_DTYPE)

    spec = pl.BlockSpec((tm, d), lambda i: (i, 0))
    shp = jax.ShapeDtypeStruct((rows, d), ACT_DTYPE)
    return pl.pallas_call(
        body, name="merge_bwd", grid=(rows // tm,),
        in_specs=[spec, spec, spec, spec, pl.BlockSpec((tm, d), lambda i: (i, 1)), pl.BlockSpec(memory_space=pl.ANY)],
        out_specs=[spec, spec, pl.BlockSpec((tm, 2 * d), lambda i: (i, col0 // (2 * d)))],
        out_shape=[shp, shp, jax.ShapeDtypeStruct(dproj.shape, dproj.dtype)], input_output_aliases={5: 2},
        compiler_params=_params(("parallel",)),
    )(dm, bs, br, gates, gates, dproj)


def _place():
    x, y, c = lax.axis_index("x"), lax.axis_index("y"), lax.axis_index("c")
    return x, y, c


def _slot(p):
    return 4 * p[0] + 2 * p[1] + p[2]


def _allgather(arrs, name):
    n = len(arrs)
    any_spec = pl.BlockSpec(memory_space=pl.ANY)

    def body(*refs):
        ins, outs = refs[:n], refs[n:2 * n]
        send_sems, recv_sems, local_sems = refs[2 * n:]
        x, y, c = _place()
        me, sibling = (x, y, c), (x, y, 1 - c)
        chips = [(1 - x, y), (x, 1 - y), (1 - x, 1 - y)]

        def copy(a, k, block, to, src=None):
            dst = outs[a].at[_slot(block)]
            return pltpu.make_async_remote_copy(
                src_ref=dst if src is None else src, dst_ref=dst, send_sem=send_sems.at[a * 7 + k],
                recv_sem=recv_sems.at[a * 7 + k], device_id=to, device_id_type=MESH)

        mine, first, passed = [], [], []
        for a in range(n):
            cp = pltpu.make_async_copy(ins[a], outs[a].at[_slot(me)], local_sems.at[a])
            cp.start()
            mine.append(cp)
            first.append(copy(a, 0, me, sibling, src=ins[a]))
            first += [copy(a, 1 + j, me, (*chip, c), src=ins[a]) for j, chip in enumerate(chips)]
        for cp in first:
            cp.start()
        for j, chip in enumerate(chips):
            for a in range(n):
                copy(a, 1 + j, (*chip, c), me).wait_recv()
                cp = copy(a, 4 + j, (*chip, c), sibling)
                cp.start()
                passed.append(cp)
        for a in range(n):
            copy(a, 0, sibling, me).wait_recv()
            for j, chip in enumerate(chips):
                copy(a, 4 + j, (*chip, 1 - c), me).wait_recv()
        for cp in first + passed:
            cp.wait_send()
        for cp in mine:
            cp.wait()

    return pl.pallas_call(
        body, name=name, in_specs=[any_spec] * n, out_specs=[any_spec] * n,
        out_shape=[jax.ShapeDtypeStruct((N_DEV,) + a.shape, a.dtype) for a in arrs],
        scratch_shapes=[pltpu.SemaphoreType.DMA((7 * n,)), pltpu.SemaphoreType.DMA((7 * n,)), pltpu.SemaphoreType.DMA((n,))],
    )(*arrs)


def _peers():
    x, y, c = _place()
    return (x, y, c), [(x ^ dx, y ^ dy, c ^ dc) for dx in (0, 1) for dy in (0, 1) for dc in (0, 1)][1:]


def _exchange_copies(srcs, lands, send_sems, recv_sems, scatter, sender):
    me, peers = _peers()
    out = []
    for a, (src, land) in enumerate(zip(srcs, lands, strict=True)):
        for k, peer in enumerate(peers):
            src_ref = src.at[_slot(peer)] if scatter else src
            out.append(pltpu.make_async_remote_copy(
                src_ref=src_ref, dst_ref=land.at[_slot(me if sender else peer)], send_sem=send_sems.at[a * 7 + k],
                recv_sem=recv_sems.at[a * 7 + k], device_id=peer, device_id_type=MESH))
    return out


_HBM = pl.BlockSpec(memory_space=pltpu.HBM)
_SEM = pl.BlockSpec(memory_space=pltpu.SEMAPHORE)
_EFFECT = pltpu.SideEffectType.DATAFLOW_SIDE_EFFECTING


def _exchange_start(srcs, scatter, name, after=None):
    n = len(srcs)
    land_shapes = [s.shape if scatter else (N_DEV,) + s.shape for s in srcs]
    n_in = 2 * n + (after is not None)

    def body(*refs):
        for cp in _exchange_copies(refs[:n], refs[n:2 * n], refs[n_in], refs[n_in + 1], scatter, True):
            cp.start()
        refs[-1][...] = jnp.zeros_like(refs[-1])

    args = [pltpu.with_memory_space_constraint(s, pltpu.HBM) for s in srcs]
    args += [pltpu.with_memory_space_constraint(lax.empty(shp, s.dtype), pltpu.HBM) for s, shp in zip(srcs, land_shapes)]
    thru_shapes = tuple(pltpu.HBM(a.shape, a.dtype) for a in args)
    extra = [] if after is None else [after]
    outs = pl.pallas_call(
        body, name=name,
        out_shape=(pltpu.SemaphoreType.DMA((7 * n,)), pltpu.SemaphoreType.DMA((7 * n,))) + thru_shapes
        + (jax.ShapeDtypeStruct((8, LANES), F32),),
        in_specs=[_HBM] * (2 * n) + [pl.BlockSpec(memory_space=pl.ANY)] * len(extra),
        out_specs=(_SEM, _SEM) + (_HBM,) * (2 * n) + (pl.BlockSpec(memory_space=pltpu.VMEM),),
        input_output_aliases={i: 2 + i for i in range(2 * n)},
        compiler_params=pltpu.CompilerParams(has_side_effects=_EFFECT),
    )(*args, *extra)
    return outs[:-1], outs[-1]


def _exchange_wait(handle, scatter, after, name):
    n = (len(handle) - 2) // 2
    thru = handle[2:]

    def body(*refs):
        for cp in _exchange_copies(refs[:n], refs[n:2 * n], refs[2 * n], refs[2 * n + 1], scatter, False):
            cp.wait_send()
            cp.wait_recv()

    outs = pl.pallas_call(
        body, name=name, out_shape=tuple(pltpu.HBM(t.shape, t.dtype) for t in thru),
        in_specs=[_HBM] * (2 * n) + [_SEM, _SEM, pl.BlockSpec(memory_space=pl.ANY)], out_specs=(_HBM,) * (2 * n),
        input_output_aliases={i: i for i in range(2 * n)},
        compiler_params=pltpu.CompilerParams(has_side_effects=_EFFECT),
    )(*thru, handle[0], handle[1], after)
    return list(outs[:n]), list(outs[n:])


def _allreduce_small(pack):
    rows, lanes = pack.shape

    def body(x_ref, o_ref, buf, send_sems, recv_sems):
        x, y, c = _place()
        me, sibling = (x, y, c), (x, y, 1 - c)
        chips = [(1 - x, y), (x, 1 - y), (1 - x, 1 - y)]

        def copy(k, block, to, src=None):
            dst = buf.at[_slot(block)]
            return pltpu.make_async_remote_copy(
                src_ref=dst if src is None else src, dst_ref=dst, send_sem=send_sems.at[k], recv_sem=recv_sems.at[k],
                device_id=to, device_id_type=MESH)

        buf[_slot(me)] = x_ref[...]
        first = [copy(0, me, sibling, src=x_ref)]
        first += [copy(1 + j, me, (*chip, c), src=x_ref) for j, chip in enumerate(chips)]
        for cp in first:
            cp.start()
        passed = [copy(4 + j, (*chip, c), sibling) for j, chip in enumerate(chips)]
        for j, chip in enumerate(chips):
            copy(1 + j, (*chip, c), me).wait_recv()
            passed[j].start()
        copy(0, sibling, me).wait_recv()
        for j, chip in enumerate(chips):
            copy(4 + j, (*chip, 1 - c), me).wait_recv()
        for cp in first + passed:
            cp.wait_send()
        acc = buf[0]
        for i in range(1, N_DEV):
            acc = acc + buf[i]
        o_ref[...] = acc

    vmem = pl.BlockSpec(memory_space=pltpu.VMEM)
    return pl.pallas_call(
        body, name="allreduce_small", in_specs=[vmem], out_specs=vmem,
        out_shape=jax.ShapeDtypeStruct((rows, lanes), F32),
        scratch_shapes=[pltpu.VMEM((N_DEV, rows, lanes), F32), pltpu.SemaphoreType.DMA((7,)), pltpu.SemaphoreType.DMA((7,))],
        compiler_params=pltpu.CompilerParams(vmem_limit_bytes=VMEM_LIMIT),
    )(pack)


def _adamw(w, g, m, v):
    m = ADAM_B1 * m + (1.0 - ADAM_B1) * g
    v = ADAM_B2 * v + (1.0 - ADAM_B2) * jnp.square(g)
    m_hat = m / (1.0 - ADAM_B1 ** ADAM_STEP)
    v_hat = v / (1.0 - ADAM_B2 ** ADAM_STEP)
    delta = -ADAM_LR * (m_hat / (jnp.sqrt(v_hat) + ADAM_EPS) + ADAM_WD * w)
    return delta, m, v


def _adam_shard(own, parts, w, m, v, name):
    r, c = w.shape
    tr = _pick(r, (128, 64, 32, 16, 8))

    def body(own_ref, p_ref, w_ref, m_ref, v_ref, g_ref, d_ref, nm_ref, nv_ref):
        _, peers = _peers()
        g = own_ref[...].astype(F32)
        for peer in peers:
            g = g + p_ref[_slot(peer)].astype(F32)
        g_ref[...] = g
        d_ref[...], nm_ref[...], nv_ref[...] = _adamw(w_ref[...], g, m_ref[...], v_ref[...])

    spec = pl.BlockSpec((tr, c), lambda i: (i, 0))
    shp = jax.ShapeDtypeStruct((r, c), F32)
    return pl.pallas_call(
        body, name=name, grid=(r // tr,),
        in_specs=[spec, pl.BlockSpec((N_DEV, tr, c), lambda i: (0, i, 0)), spec, spec, spec], out_specs=[spec] * 4,
        out_shape=[shp] * 4, compiler_params=_params(("parallel",)),
    )(own, parts, w, m, v)


def _adam_small(w, g, m, v):
    r, c = w.shape

    def body(w_ref, g_ref, m_ref, v_ref, d_ref, nm_ref, nv_ref):
        d_ref[...], nm_ref[...], nv_ref[...] = _adamw(w_ref[...], g_ref[...], m_ref[...], v_ref[...])

    shp = jax.ShapeDtypeStruct((r, c), F32)
    return pl.pallas_call(body, name="adam_small", out_shape=[shp] * 3)(w, g, m, v)


def _column_plan(pieces, shard_w):
    plan = []
    for c0, width, d0 in pieces:
        c = c0
        while c < c0 + width:
            s, a = divmod(c, shard_w)
            w = min(c0 + width - c, shard_w - a)
            plan.append((s, a, w, d0 + c - c0))
            c += w
    return plan


def _cols_from_shards(g, plan, out_w, zero, name):
    _, r, sw = g.shape
    tr = _pick(r, (128,))

    def body(x_ref, o_ref):
        for d0, w in zero:
            o_ref[:, d0:d0 + w] = jnp.zeros((tr, w), g.dtype)
        for s, a, w, d0 in plan:
            o_ref[:, d0:d0 + w] = x_ref[s, :, a:a + w]

    return pl.pallas_call(
        body, name=name, grid=(r // tr,), in_specs=[pl.BlockSpec((N_DEV, tr, sw), lambda i: (0, i, 0))],
        out_specs=pl.BlockSpec((tr, out_w), lambda i: (i, 0)), out_shape=jax.ShapeDtypeStruct((r, out_w), g.dtype),
        compiler_params=_params(("parallel",)),
    )(g)


def _shards_from_cols(srcs, plans, shard_w, name):
    r = srcs[0].shape[0]
    tr = _pick(r, (128,))
    n = len(srcs)

    def body(*refs):
        o_ref = refs[n]
        for x_ref, plan in zip(refs[:n], plans, strict=True):
            for s, a, w, d0 in plan:
                o_ref[s, :, a:a + w] = x_ref[:, d0:d0 + w].astype(COMM_DTYPE)

    return pl.pallas_call(
        body, name=name, grid=(r // tr,), in_specs=[pl.BlockSpec((tr, t.shape[1]), lambda i: (i, 0)) for t in srcs],
        out_specs=pl.BlockSpec((N_DEV, tr, shard_w), lambda i: (0, i, 0)),
        out_shape=jax.ShapeDtypeStruct((N_DEV, r, shard_w), COMM_DTYPE), compiler_params=_params(("parallel",)),
    )(*srcs)


def _pack(arrs):
    rows = []
    for a in arrs:
        flat = a.reshape(-1).astype(F32)
        rows.append(jnp.pad(flat, (0, (-flat.shape[0]) % (8 * LANES))).reshape(-1, LANES))
    return jnp.concatenate(rows, axis=0)


def _unpack(pack, shapes):
    out, r = [], 0
    for s in shapes:
        size = math.prod(s)
        nr = -(-size // (8 * LANES)) * 8
        out.append(pack[r:r + nr].reshape(-1)[:size].reshape(s))
        r += nr
    return out


def _group_lanes(t):
    lead = t.shape[:-1]
    t = t.reshape(lead + (SSD_GROUPS, SSD_HPG))
    t = jnp.pad(t, [(0, 0)] * len(lead) + [(0, 0), (0, LANES - SSD_HPG)])
    return t.reshape(lead + (SSD_GROUPS * LANES,))


def _ungroup_lanes(t):
    lead = t.shape[:-1]
    return t.reshape(lead + (SSD_GROUPS, LANES))[..., :SSD_HPG].reshape(lead + (SSD_HEADS,))


def kernel(x, meta_tokens, mix_norm_w, w_in, ssd_conv_w, ssd_conv_b, ssd_dt_bias, ssd_A_log, ssd_D, ssd_norm_w, w_branch_ssd, w_branch_ret, w_out, ffn_norm_w, w_up, ffn_conv_w, ffn_conv_b, w_down, final_norm_w, loss_target, m_meta_tokens, m_mix_norm_w, m_w_in, m_ssd_conv_w, m_ssd_conv_b, m_ssd_dt_bias, m_ssd_A_log, m_ssd_D, m_ssd_norm_w, m_w_branch_ssd, m_w_branch_ret, m_w_out, m_ffn_norm_w, m_w_up, m_ffn_conv_w, m_ffn_conv_b, m_w_down, m_final_norm_w, v_meta_tokens, v_mix_norm_w, v_w_in, v_ssd_conv_w, v_ssd_conv_b, v_ssd_dt_bias, v_ssd_A_log, v_ssd_D, v_ssd_norm_w, v_w_branch_ssd, v_w_branch_ret, v_w_out, v_ffn_norm_w, v_w_up, v_ffn_conv_w, v_ffn_conv_b, v_w_down, v_final_norm_w):
    seq, d = x.shape[1], x.shape[2]
    rows = seq + PAD_ROWS
    tm = _row_tile(rows)
    me = _slot(_place())
    d_ff = w_down.shape[1] * N_DEV

    big = [w_in[0], w_branch_ssd[0], w_branch_ret[0], w_out[0], w_up[0], w_down[0]]
    first = _allgather([w_in[0].astype(COMM_DTYPE), meta_tokens, ssd_conv_w[0], ffn_conv_w[0]], "gather_first")
    rest_src = [b.astype(COMM_DTYPE) for b in big[1:]]
    rest_handle, rest_token = _exchange_start(rest_src, False, "gather_rest_start", after=first[0])
    cols = lambda t: jnp.transpose(t, (1, 0, 2)).reshape(t.shape[1], -1)
    rws = lambda t: t.reshape(-1, t.shape[2])
    conv_w, fconv_w = cols(first[2]), cols(first[3])
    meta_full = cols(first[1]) + rest_token[0, 0]
    widths = [SSD_D_INNER, SSD_CONV_DIM, SSD_HEADS, RET_HEADS * RET_QK, RET_HEADS * RET_QK, RET_HEADS * RET_V,
              RET_HEADS * RET_V, d, d]
    offs = [0]
    for wd in widths:
        offs.append(offs[-1] + wd)
    r0, z0 = 0, RET_HEADS * RET_HW
    g0 = z0 + widths[0]
    x0 = g0 + 2 * d
    dt0 = x0 + widths[1]
    in_p = dt0 + SSD_GROUPS * LANES
    pieces = []
    for hd in range(RET_HEADS):
        base = r0 + hd * RET_HW
        pieces += [(offs[3] + hd * RET_QK, RET_QK, base), (offs[4] + hd * RET_QK, RET_QK, base + RET_QK),
                   (offs[5] + hd * RET_V, RET_V, base + 2 * RET_QK), (offs[6] + hd * RET_V, RET_V, base + 2 * RET_QK + RET_V)]
    pieces += [(offs[0], widths[0], z0), (offs[7], d, g0), (offs[8], d, g0 + d), (offs[1], widths[1], x0)]
    pieces += [(offs[2] + SSD_HPG * grp, SSD_HPG, dt0 + LANES * grp) for grp in range(SSD_GROUPS)]
    in_plan = _column_plan(pieces, w_in.shape[2])
    w_in_p = _cols_from_shards(first[0], in_plan, in_p, [(dt0, SSD_GROUPS * LANES)], "w_in_columns")

    h0 = jnp.concatenate([jnp.zeros((FRONT, d), F32), meta_full, x[0]], axis=0)
    u1 = _rms_fwd(h0, mix_norm_w, "rms1")
    in_proj = lambda c0, width, dtype, nm: _mm(u1, w_in_p, mode="nn", out_dtype=dtype, tm=tm, tk=d, name="in_proj_" + nm,
                                               tn=_pick(width, (1024, 512)), b_n0=c0, n_out=width)
    qkvg = in_proj(r0, RET_HEADS * RET_HW, ACT_DTYPE, "qkvg")
    z = in_proj(z0, widths[0], ACT_DTYPE, "z")
    gates = in_proj(g0, 2 * d, ACT_DTYPE, "gates")
    dtr = in_proj(dt0, SSD_GROUPS * LANES, F32, "dt")
    xbc, xbc_c = _xbc_proj_conv(u1, w_in_p, x0, conv_w, ssd_conv_b)
    bias_p, alog_p, dsk_p = _group_lanes(ssd_dt_bias), _group_lanes(ssd_A_log), _group_lanes(ssd_D)
    y_ssd, y_scan, ssd_states = _ssd_fwd(xbc_c, dtr, z, bias_p, alog_p, dsk_p, ssd_norm_w)
    cos, sin, lgam = _rotary_tables(rows)
    y_ret, ret_states = _ret_fwd(qkvg, cos, sin, lgam)
    rest_own, rest = _exchange_wait(rest_handle, False, y_ret, "gather_rest_wait")
    rest = [lax.dynamic_update_index_in_dim(land, own, me, 0) for land, own in zip(rest, rest_own, strict=True)]
    w_bs, w_br, w_o, w_dn = rws(rest[0]), rws(rest[1]), rws(rest[2]), rws(rest[4])
    w_up_f = _cols_from_shards(rest[3], _column_plan([(0, 2 * d_ff, 0)], w_up.shape[2]), 2 * d_ff, [], "w_up_columns")
    bs = _mm(y_ssd, w_bs, mode="nn", out_dtype=ACT_DTYPE, tm=tm, tn=d, tk=SSD_D_INNER, name="branch_ssd")
    br = _mm(y_ret, w_br, mode="nn", out_dtype=ACT_DTYPE, tm=tm, tn=d, tk=RET_HEADS * RET_V, name="branch_ret")
    merged = _merge_fwd(bs, br, gates)
    h1 = _mm(merged, w_o, mode="nn", out_dtype=F32, tm=tm, tn=d, tk=d, name="out_proj", add=h0)
    u2 = _rms_fwd(h1, ffn_norm_w, "rms2")
    up_g, up_v, act = _ffn_up_conv(u2, w_up_f, fconv_w, ffn_conv_b)
    h2 = _mm(act, w_dn, mode="nn", out_dtype=F32, tm=tm, tn=d, tk=d_ff, name="ffn_down", add=h1)
    tgt = jnp.pad(loss_target[0], ((PAD_ROWS, 0), (0, 0)))
    dh2, loss_acc, g_final = _loss_head(h2, tgt, final_norm_w.reshape(1, d))

    tff = _pick(d_ff, (1408, 256))
    tkr = _pick(rows, (1664, 128))
    tkr2 = _pick(rows, (4160, 128))
    rparts = lambda t: t.reshape(N_DEV, -1, t.shape[1])
    d_act = _mm(dh2, w_dn, mode="nt", out_dtype=ACT_DTYPE, tm=tm, tn=tff, tk=d, name="d_act")
    g_w_dn = _mm(act, dh2, mode="tn", out_dtype=COMM_DTYPE, tm=tff, tn=d, tk=tkr, name="g_w_down")
    c_dn = [rparts(g_w_dn)]
    h_dn, t_dn = _exchange_start(c_dn, True, "scatter_down_start")
    d_up_g, d_up_v, g_fcw_g, g_fcb_g, g_fcw_v, g_fcb_v, g_w_up_g, g_w_up_v = _ffn_conv_bwd(
        up_g, up_v, d_act, fconv_w, ffn_conv_b + t_dn[0, 0], u2)
    g_fconv_w = jnp.concatenate([g_fcw_g, g_fcw_v], axis=1)
    g_fconv_b = jnp.concatenate([g_fcb_g, g_fcb_v], axis=1)
    c_up = [_shards_from_cols([g_w_up_g, g_w_up_v], [_column_plan([(0, d_ff, 0)], w_up.shape[2]),
                                                     _column_plan([(d_ff, d_ff, 0)], w_up.shape[2])], w_up.shape[2], "g_w_up_shards")]
    h_up, t_up = _exchange_start(c_up, True, "scatter_up_start")
    du2 = _mm(d_up_g, w_up_f, mode="nt", out_dtype=F32, tm=tm, tn=d, tk=d_ff, name="d_u2_gate", after=t_up)
    du2 = _mm(d_up_v, w_up_f, mode="nt", out_dtype=F32, tm=tm, tn=d, tk=d_ff, name="d_u2_value", add=du2, b_k0=d_ff)
    dh1, g_ffn_norm = _rms_bwd(du2, h1, ffn_norm_w, dh2, "rms2_bwd")
    d_merged = _mm(dh1, w_o, mode="nt", out_dtype=F32, tm=tm, tn=d, tk=d, name="d_merged")
    g_w_o = _mm(merged, dh1, mode="tn", out_dtype=COMM_DTYPE, tm=d, tn=d, tk=tkr, name="g_w_out")
    dproj = lax.empty((rows, in_p), ACT_DTYPE)
    d_bs, d_br, dproj = _merge_bwd(d_merged, bs, br, gates, dproj, g0)
    d_yssd = _mm(d_bs, w_bs, mode="nt", out_dtype=ACT_DTYPE, tm=tm, tn=1024, tk=d, name="d_y_ssd")
    g_w_bs = _mm(y_ssd, d_bs, mode="tn", out_dtype=COMM_DTYPE, tm=1024, tn=d, tk=tkr2, name="g_w_branch_ssd")
    d_yret = _mm(d_br, w_br, mode="nt", out_dtype=ACT_DTYPE, tm=tm, tn=1024, tk=d, name="d_y_ret")
    g_w_br = _mm(y_ret, d_br, mode="tn", out_dtype=COMM_DTYPE, tm=1024, tn=d, tk=tkr2, name="g_w_branch_ret")
    c_mid = [rparts(g_w_bs), rparts(g_w_br), rparts(g_w_o)]
    h_mid, t_mid = _exchange_start(c_mid, True, "scatter_mid_start")
    d_yscan, dproj, g_nw = _ssd_gate_bwd(y_scan, z, d_yssd, ssd_norm_w + t_mid[0, 0], dproj, z0)
    dxs, d_bm, d_cm, dproj, g_bias_p, g_alog_p, g_dsk_p = _ssd_bwd(
        xbc_c, dtr, bias_p, alog_p, dsk_p, ssd_states, d_yscan, dproj, dt0)
    dproj, g_conv_w, g_conv_b = _ssd_conv_bwd(xbc, dxs, d_bm, d_cm, conv_w, ssd_conv_b, dproj, x0)
    dproj = _ret_bwd(qkvg, cos, sin, lgam, ret_states, d_yret, dproj)
    g_w_in_p = _mm(u1, dproj, mode="tn", out_dtype=F32, tm=d, tn=_pick(in_p, (768, 512)), tk=tkr2, name="g_w_in")
    c_in = [_shards_from_cols([g_w_in_p], [in_plan], w_in.shape[2], "g_w_in_shards")]
    h_in, t_in = _exchange_start(c_in, True, "scatter_in_start")
    du1 = _mm(dproj, w_in_p, mode="nt", out_dtype=F32, tm=tm, tn=d, tk=_pick(in_p, (4608, 512)), name="d_u1", after=t_in)
    dh0, g_mix_norm = _rms_bwd(du1, h0, mix_norm_w, dh1, "rms1_bwd")
    grad_x = dh0[PAD_ROWS:][None]

    landed = {}
    for key, handle, names in (("in", h_in, ["w_in"]), ("mid", h_mid, ["w_branch_ssd", "w_branch_ret", "w_out"]),
                               ("up", h_up, ["w_up"]), ("down", h_dn, ["w_down"])):
        srcs, lands = _exchange_wait(handle, True, dh0, f"scatter_{key}_wait")
        for nm, land, src in zip(names, lands, srcs, strict=True):
            landed[nm] = (lax.dynamic_index_in_dim(src, me, 0, keepdims=False), land)
    big_m = [m_w_in, m_w_branch_ssd, m_w_branch_ret, m_w_out, m_w_up, m_w_down]
    big_v = [v_w_in, v_w_branch_ssd, v_w_branch_ret, v_w_out, v_w_up, v_w_down]
    big_names = ["w_in", "w_branch_ssd", "w_branch_ret", "w_out", "w_up", "w_down"]
    big_out = {}
    for nm, w, m, v_ in zip(big_names, big, big_m, big_v, strict=True):
        big_out[nm] = [t[None] for t in _adam_shard(*landed[nm], w, m[0], v_[0], "adam_" + nm)]

    small_g = [dh0[FRONT:PAD_ROWS], g_mix_norm, g_conv_w, g_conv_b, _ungroup_lanes(g_bias_p), _ungroup_lanes(g_alog_p),
               _ungroup_lanes(g_dsk_p), g_nw, g_ffn_norm, g_fconv_w, g_fconv_b, g_final, loss_acc[0:1, 0:1]]
    total = _unpack(_allreduce_small(_pack(small_g)), [t.shape for t in small_g])
    loss = total[12].reshape(())
    shard = lambda t, width: lax.dynamic_slice_in_dim(t, me * width, width, axis=1)
    small_names = ["meta_tokens", "mix_norm_w", "ssd_conv_w", "ssd_conv_b", "ssd_dt_bias", "ssd_A_log", "ssd_D", "ssd_norm_w",
                   "ffn_norm_w", "ffn_conv_w", "ffn_conv_b", "final_norm_w"]
    small_w = [meta_tokens, mix_norm_w, ssd_conv_w, ssd_conv_b, ssd_dt_bias, ssd_A_log, ssd_D, ssd_norm_w, ffn_norm_w,
               ffn_conv_w, ffn_conv_b, final_norm_w]
    small_m = [m_meta_tokens, m_mix_norm_w, m_ssd_conv_w, m_ssd_conv_b, m_ssd_dt_bias, m_ssd_A_log, m_ssd_D, m_ssd_norm_w,
               m_ffn_norm_w, m_ffn_conv_w, m_ffn_conv_b, m_final_norm_w]
    small_v = [v_meta_tokens, v_mix_norm_w, v_ssd_conv_w, v_ssd_conv_b, v_ssd_dt_bias, v_ssd_A_log, v_ssd_D, v_ssd_norm_w,
               v_ffn_norm_w, v_ffn_conv_w, v_ffn_conv_b, v_final_norm_w]
    grads = total[:12]
    grads[0] = shard(grads[0], meta_tokens.shape[1])
    grads[2] = shard(grads[2], ssd_conv_w.shape[2])
    grads[9] = shard(grads[9], ffn_conv_w.shape[2])
    grads = [t.reshape(w.shape) for t, w in zip(grads, small_w, strict=True)]
    shapes = [w.shape for w in small_w]
    upd = _adam_small(_pack(small_w), _pack(grads), _pack(small_m), _pack(small_v))
    small_out = {nm: [gr_] + [u[i] for u in (_unpack(t, shapes) for t in upd)]
                 for i, (nm, gr_) in enumerate(zip(small_names, grads, strict=True))}

    order = ["meta_tokens", "mix_norm_w", "w_in", "ssd_conv_w", "ssd_conv_b", "ssd_dt_bias", "ssd_A_log", "ssd_D", "ssd_norm_w",
             "w_branch_ssd", "w_branch_ret", "w_out", "ffn_norm_w", "w_up", "ffn_conv_w", "ffn_conv_b", "w_down", "final_norm_w"]
    res = {**big_out, **small_out}
    return (loss, grad_x, *[res[nm][0] for nm in order], *[res[nm][1] for nm in order], *[res[nm][2] for nm in order],
            *[res[nm][3] for nm in order])
```

```python
import functools
import math

import jax
import jax.numpy as jnp
import numpy as np
from jax import lax
from jax.experimental import pallas as pl
from jax.experimental.pallas import tpu as pltpu

F32 = jnp.float32
MXU_DTYPE = jnp.bfloat16
ACT_DTYPE = jnp.bfloat16
COMM_DTYPE = jnp.bfloat16

N_META = 16
CHUNK = 128
FRONT = CHUNK - N_META
PAD_ROWS = FRONT + N_META
EPS = 1e-6
N_DEV = 8

SSD_D_INNER = 2048
SSD_HEAD_DIM = 64
SSD_HEADS = 32
SSD_GROUPS = 4
SSD_HPG = SSD_HEADS // SSD_GROUPS
SSD_STATE = 128
SSD_CONV = 4
SSD_CONV_DIM = SSD_D_INNER + 2 * SSD_GROUPS * SSD_STATE
SSD_GW = SSD_D_INNER // SSD_GROUPS
RET_HEADS = 4
RET_QK = 256
RET_V = 512
RET_HW = 2 * RET_QK + 2 * RET_V
ROPE_BASE = 10000.0
FFN_CONV = 3
HALO = 16
LANES = 128

ADAM_LR = 0.001
ADAM_B1 = 0.9
ADAM_B2 = 0.999
ADAM_EPS = 1e-08
ADAM_WD = 0.01
ADAM_STEP = 10

VMEM_LIMIT = 56 * 1024 * 1024
MESH = pl.DeviceIdType.MESH

NN = (((1,), (0,)), ((), ()))
NT = (((1,), (1,)), ((), ()))
TN = (((0,), (0,)), ((), ()))


def _params(sem):
    return pltpu.CompilerParams(dimension_semantics=sem, vmem_limit_bytes=VMEM_LIMIT)


def _mxu(a, b, dn):
    return lax.dot_general(a.astype(MXU_DTYPE), b.astype(MXU_DTYPE), dn, preferred_element_type=F32)


@functools.partial(jax.custom_vjp, nondiff_argnums=(2,))
def _dot(a, b, dn=NN):
    return _mxu(a, b, dn)


def _dot_fwd(a, b, dn):
    return _mxu(a, b, dn), (a, b)


def _dot_bwd(dn, res, g):
    a, b = res
    if dn == NN:
        return _mxu(g, b, NT), _mxu(a, g, TN)
    if dn == NT:
        return _mxu(g, b, NN), _mxu(g, a, TN)
    assert dn == TN
    return _mxu(b, g, NT), _mxu(a, g, NN)


_dot.defvjp(_dot_fwd, _dot_bwd)


def _silu(x):
    return x * jax.nn.sigmoid(x)


def _dsilu(x):
    s = jax.nn.sigmoid(x)
    return s * (1.0 + x * (1.0 - s))


def _row_tile(rows):
    return 640 if rows % 640 == 0 else 128


def _mm(a, b, *, mode, out_dtype, tm, tn, tk, name, add=None, after=None, b_k0=0, b_n0=0, n_out=None):
    if mode == "nt":
        (m, k), n = a.shape, b.shape[0]
        k2 = k if b_k0 % tk == 0 and b_k0 + k <= b.shape[1] else None
    else:
        (m, k) = a.shape if mode == "nn" else a.shape[::-1]
        k2 = b.shape[0]
        n = b.shape[1] if n_out is None else n_out
        assert b_n0 % tn == 0 and b_n0 + n <= b.shape[1]
    assert (b_k0 == 0 or mode == "nt") and ((b_n0 == 0 and n_out is None) or mode != "nt")
    assert k == k2 and m % tm == 0 and n % tn == 0 and k % tk == 0, (name, a.shape, b.shape, tm, tn, tk)
    kb0, nb0 = b_k0 // tk, b_n0 // tn
    nk = k // tk
    dn = {"nn": NN, "nt": NT, "tn": TN}[mode]
    has_add = add is not None
    n_in = 2 + has_add + (after is not None)

    def body(*refs):
        a_ref, b_ref = refs[0], refs[1]
        add_ref = refs[2] if has_add else None
        o_ref = refs[n_in]
        p = _dot(a_ref[...], b_ref[...], dn)
        if nk == 1:
            if has_add:
                p = p + add_ref[...]
            o_ref[...] = p.astype(out_dtype)
        else:
            acc_ref = refs[n_in + 1]
            kk = pl.program_id(2)

            @pl.when(kk == 0)
            def _():
                acc_ref[...] = p

            @pl.when(kk > 0)
            def _():
                acc_ref[...] += p

            @pl.when(kk == nk - 1)
            def _():
                r = acc_ref[...]
                if has_add:
                    r = r + add_ref[...]
                o_ref[...] = r.astype(out_dtype)

    if mode == "tn":
        a_spec = pl.BlockSpec((tk, tm), lambda j, i, kk: (kk, i))
    else:
        a_spec = pl.BlockSpec((tm, tk), lambda j, i, kk: (i, kk))
    if mode == "nt":
        b_spec = pl.BlockSpec((tn, tk), lambda j, i, kk: (j, kk + kb0))
    else:
        b_spec = pl.BlockSpec((tk, tn), lambda j, i, kk: (kk, j + nb0))
    o_spec = pl.BlockSpec((tm, tn), lambda j, i, kk: (i, j))
    in_specs = [a_spec, b_spec] + ([o_spec] if has_add else [])
    args = (a, b) + ((add,) if has_add else ())
    if after is not None:
        in_specs.append(pl.BlockSpec(memory_space=pl.ANY))
        args += (after,)
    return pl.pallas_call(
        body, name=name, grid=(n // tn, m // tm, nk), in_specs=in_specs, out_specs=o_spec,
        out_shape=jax.ShapeDtypeStruct((m, n), out_dtype),
        scratch_shapes=[pltpu.VMEM((tm, tn), F32)] if nk > 1 else [],
        compiler_params=_params(("parallel", "parallel", "arbitrary")),
    )(*args)


def _pick(n, cands):
    for c in cands:
        if n % c == 0:
            return c
    return n


def _rms_fwd(h, w, name):
    rows, d = h.shape
    tm = _row_tile(rows)

    def body(h_ref, w_ref, u_ref):
        x = h_ref[...]
        r = lax.rsqrt(jnp.mean(x * x, axis=-1, keepdims=True) + EPS)
        u_ref[...] = (x * r * w_ref[...]).astype(ACT_DTYPE)

    return pl.pallas_call(
        body, name=name, grid=(rows // tm,),
        in_specs=[pl.BlockSpec((tm, d), lambda i: (i, 0)), pl.BlockSpec((1, d), lambda i: (0, 0))],
        out_specs=pl.BlockSpec((tm, d), lambda i: (i, 0)),
        out_shape=jax.ShapeDtypeStruct((rows, d), ACT_DTYPE),
        compiler_params=_params(("parallel",)),
    )(h, w)


def _rms_bwd(du, h, w, dres, name):
    rows, d = h.shape
    tm = _row_tile(rows)

    def body(du_ref, h_ref, w_ref, dres_ref, dh_ref, dw_ref):
        x = h_ref[...]
        dy = du_ref[...].astype(F32)
        r = lax.rsqrt(jnp.mean(x * x, axis=-1, keepdims=True) + EPS)
        xhat = x * r
        dxn = dy * w_ref[...]
        dx = r * (dxn - xhat * jnp.mean(dxn * xhat, axis=-1, keepdims=True))
        dh_ref[...] = dres_ref[...] + dx

        @pl.when(pl.program_id(0) == 0)
        def _():
            dw_ref[...] = jnp.zeros_like(dw_ref)

        dw_ref[...] += jnp.sum(dy * xhat, axis=0, keepdims=True)

    return pl.pallas_call(
        body, name=name, grid=(rows // tm,),
        in_specs=[pl.BlockSpec((tm, d), lambda i: (i, 0)), pl.BlockSpec((tm, d), lambda i: (i, 0)),
                  pl.BlockSpec((1, d), lambda i: (0, 0)), pl.BlockSpec((tm, d), lambda i: (i, 0))],
        out_specs=[pl.BlockSpec((tm, d), lambda i: (i, 0)), pl.BlockSpec((1, d), lambda i: (0, 0))],
        out_shape=[jax.ShapeDtypeStruct((rows, d), F32), jax.ShapeDtypeStruct((1, d), F32)],
        compiler_params=_params(("arbitrary",)),
    )(du, h, w, dres)


def _loss_head(h2, tgt, w):
    rows, d = h2.shape
    tm = _row_tile(rows)

    def body(h_ref, t_ref, w_ref, dh_ref, loss_ref, dw_ref):
        i = pl.program_id(0)
        x = h_ref[...]
        r = lax.rsqrt(jnp.mean(x * x, axis=-1, keepdims=True) + EPS)
        xhat = x * r
        wv = w_ref[...]
        row = i * tm + lax.broadcasted_iota(jnp.int32, (tm, 1), 0)
        live = row >= PAD_ROWS
        diff = jnp.where(live, xhat * wv - t_ref[...], 0.0)
        dy = diff * (1.0 / d)
        dxn = dy * wv
        dh_ref[...] = r * (dxn - xhat * jnp.mean(dxn * xhat, axis=-1, keepdims=True))

        @pl.when(i == 0)
        def _():
            loss_ref[...] = jnp.zeros_like(loss_ref)
            dw_ref[...] = jnp.zeros_like(dw_ref)

        loss_ref[...] += 0.5 * jnp.sum(jnp.mean(diff * diff, axis=-1, keepdims=True))
        dw_ref[...] += jnp.sum(dy * xhat, axis=0, keepdims=True)

    return pl.pallas_call(
        body, name="loss_head", grid=(rows // tm,),
        in_specs=[pl.BlockSpec((tm, d), lambda i: (i, 0)), pl.BlockSpec((tm, d), lambda i: (i, 0)),
                  pl.BlockSpec((1, d), lambda i: (0, 0))],
        out_specs=[pl.BlockSpec((tm, d), lambda i: (i, 0)), pl.BlockSpec((8, LANES), lambda i: (0, 0)),
                   pl.BlockSpec((1, d), lambda i: (0, 0))],
        out_shape=[jax.ShapeDtypeStruct((rows, d), F32), jax.ShapeDtypeStruct((8, LANES), F32),
                   jax.ShapeDtypeStruct((1, d), F32)],
        compiler_params=_params(("arbitrary",)),
    )(h2, tgt, w)


def _prev_halo_spec(tm, width, col):
    return pl.BlockSpec((HALO, width), lambda j, i: (jnp.maximum(i * (tm // HALO) - 1, 0), col(j)))


def _next_halo_spec(tm, rows, width, col):
    last = rows // HALO - 1
    return pl.BlockSpec((HALO, width), lambda j, i: (jnp.minimum((i + 1) * (tm // HALO), last), col(j)))


def _conv_taps(cat, w_ref, b_ref, kw):
    acc = b_ref[...] + w_ref[kw - 1:kw, :] * cat
    for s in range(1, kw):
        acc = acc + w_ref[kw - 1 - s:kw - s, :] * pltpu.roll(cat, s, 0)
    return acc


def _conv_back(dpre, w_ref, kw):
    n = dpre.shape[0]
    acc = w_ref[kw - 1:kw, :] * dpre
    for s in range(1, kw):
        acc = acc + w_ref[kw - 1 - s:kw - s, :] * pltpu.roll(dpre, n - s, 0)
    return acc


def _xbc_proj_conv(u1, w_in_p, col0, w, b):
    rows, d = u1.shape
    width = w.shape[1]
    tm, tc = _row_tile(rows), 512
    cb0 = col0 // tc
    assert col0 % tc == 0

    def body(u_ref, m_ref, w_ref, b_ref, x_ref, o_ref, carry):
        i = pl.program_id(1)

        @pl.when(i == 0)
        def _():
            carry[...] = jnp.zeros_like(carry)

        xb = _mxu(u_ref[...], m_ref[...], NN).astype(ACT_DTYPE)
        x_ref[...] = xb
        x = xb.astype(F32)
        cat = jnp.concatenate([carry[...], x], axis=0)
        carry[...] = x[tm - HALO:, :]
        pre = _conv_taps(cat, w_ref, b_ref, SSD_CONV)[HALO:]
        row = i * tm + lax.broadcasted_iota(jnp.int32, (tm, 1), 0)
        o_ref[...] = jnp.where(row >= FRONT, _silu(pre), 0.0).astype(ACT_DTYPE)

    main = pl.BlockSpec((tm, tc), lambda j, i: (i, j))
    par = lambda r: pl.BlockSpec((r, tc), lambda j, i: (0, j))
    act = jax.ShapeDtypeStruct((rows, width), ACT_DTYPE)
    return pl.pallas_call(
        body, name="xbc_proj_conv", grid=(width // tc, rows // tm),
        in_specs=[pl.BlockSpec((tm, d), lambda j, i: (i, 0)), pl.BlockSpec((d, tc), lambda j, i: (0, cb0 + j)),
                  par(SSD_CONV), par(1)],
        out_specs=[main, main], out_shape=[act, act], scratch_shapes=[pltpu.VMEM((HALO, tc), F32)],
        compiler_params=_params(("parallel", "arbitrary")),
    )(u1, w_in_p, w, b)


def _ssd_conv_bwd(xbc, dxs, dbm, dcm, w, b, dproj, col0):
    rows, width = xbc.shape
    tm, tc = _row_tile(rows), 512
    kw = SSD_CONV
    nx = dxs.shape[1] // tc
    assert dbm.shape[1] == tc and dcm.shape[1] == tc and width == (nx + 2) * tc and col0 % tc == 0

    def body(x_ref, xp_ref, xn_ref, d0_ref, d0n_ref, d1_ref, d1n_ref, d2_ref, d2n_ref, w_ref, b_ref, dp_ref,
             dx_ref, dw_ref, db_ref):
        j, i = pl.program_id(0), pl.program_id(1)
        xp = jnp.where(i == 0, 0.0, xp_ref[...].astype(F32))
        cat = jnp.concatenate([xp, x_ref[...].astype(F32), xn_ref[...].astype(F32)], axis=0)
        sh = [cat] + [pltpu.roll(cat, s, 0) for s in range(1, kw)]
        pre = b_ref[...] + w_ref[kw - 1:kw, :] * sh[0]
        for s in range(1, kw):
            pre = pre + w_ref[kw - 1 - s:kw - s, :] * sh[s]
        pre = pre[HALO:]
        row = i * tm + lax.broadcasted_iota(jnp.int32, (tm + HALO, 1), 0)
        live = (row >= FRONT) & (row < rows)
        pick = lambda a, bb, c: jnp.where(j < nx, a[...], jnp.where(j == nx, bb[...], c[...])).astype(F32)
        dout = jnp.concatenate([pick(d0_ref, d1_ref, d2_ref), pick(d0n_ref, d1n_ref, d2n_ref)], axis=0)
        dpre = jnp.where(live, dout * _dsilu(pre), 0.0)
        dx_ref[...] = _conv_back(dpre, w_ref, kw)[:tm].astype(ACT_DTYPE)

        @pl.when(i == 0)
        def _():
            dw_ref[...] = jnp.zeros_like(dw_ref)
            db_ref[...] = jnp.zeros_like(db_ref)

        dmain = dpre[:tm]
        db_ref[...] += jnp.sum(dmain, axis=0, keepdims=True)
        for k in range(kw):
            dw_ref[k:k + 1, :] += jnp.sum(dmain * sh[kw - 1 - k][HALO:HALO + tm], axis=0, keepdims=True)

    main = pl.BlockSpec((tm, tc), lambda j, i: (i, j))
    par = lambda r: pl.BlockSpec((r, tc), lambda j, i: (0, j))
    col = lambda j: j
    xcol, zero = (lambda j: jnp.minimum(j, nx - 1)), (lambda j: 0)
    dspecs = lambda c: [pl.BlockSpec((tm, tc), lambda j, i: (i, c(j))), _next_halo_spec(tm, rows, tc, c)]
    return pl.pallas_call(
        body, name="ssd_conv_bwd", grid=(width // tc, rows // tm),
        in_specs=[main, _prev_halo_spec(tm, tc, col), _next_halo_spec(tm, rows, tc, col)]
        + dspecs(xcol) + dspecs(zero) + dspecs(zero) + [par(kw), par(1), pl.BlockSpec(memory_space=pl.ANY)],
        out_specs=[pl.BlockSpec((tm, tc), lambda j, i: (i, col0 // tc + j)), par(kw), par(1)],
        out_shape=[jax.ShapeDtypeStruct(dproj.shape, dproj.dtype), jax.ShapeDtypeStruct((kw, width), F32),
                   jax.ShapeDtypeStruct((1, width), F32)],
        input_output_aliases={11: 0}, compiler_params=_params(("parallel", "arbitrary")),
    )(xbc, xbc, xbc, dxs, dxs, dbm, dbm, dcm, dcm, w, b, dproj)


def _ffn_up_conv(u2, w_up, w, b):
    rows, d = u2.shape
    width = w_up.shape[1]
    dff = width // 2
    tm, tc = _row_tile(rows), _pick(dff, (256, 128))
    nb = dff // tc
    kw = FFN_CONV

    def body(u_ref, mg_ref, mv_ref, wg_ref, bg_ref, wv_ref, bv_ref, ug_ref, uv_ref, o_ref, cg, cv):
        i = pl.program_id(1)

        @pl.when(i == 0)
        def _():
            cg[...] = jnp.zeros_like(cg)
            cv[...] = jnp.zeros_like(cv)

        def pre(m_ref, up_ref, carry, w_ref, b_ref):
            upb = _mxu(u_ref[...], m_ref[...], NN).astype(ACT_DTYPE)
            up_ref[...] = upb
            x = upb.astype(F32)
            cat = jnp.concatenate([carry[...], x], axis=0)
            carry[...] = x[tm - HALO:, :]
            return _conv_taps(cat, w_ref, b_ref, kw)[HALO:]

        ag = pre(mg_ref, ug_ref, cg, wg_ref, bg_ref)
        av = pre(mv_ref, uv_ref, cv, wv_ref, bv_ref)
        o_ref[...] = (_silu(ag) * av).astype(ACT_DTYPE)

    gcol, vcol = (lambda j: j), (lambda j: j + nb)
    mat = lambda col: pl.BlockSpec((d, tc), lambda j, i: (0, col(j)))
    par = lambda r, col: pl.BlockSpec((r, tc), lambda j, i: (0, col(j)))
    out = pl.BlockSpec((tm, tc), lambda j, i: (i, j))
    act = jax.ShapeDtypeStruct((rows, dff), ACT_DTYPE)
    return pl.pallas_call(
        body, name="ffn_up_conv", grid=(nb, rows // tm),
        in_specs=[pl.BlockSpec((tm, d), lambda j, i: (i, 0)), mat(gcol), mat(vcol),
                  par(kw, gcol), par(1, gcol), par(kw, vcol), par(1, vcol)],
        out_specs=[out, out, out], out_shape=[act, act, act],
        scratch_shapes=[pltpu.VMEM((HALO, tc), F32), pltpu.VMEM((HALO, tc), F32)],
        compiler_params=_params(("parallel", "arbitrary")),
    )(u2, w_up, w_up, w, b, w, b)


def _ffn_conv_bwd(up_g, up_v, dact, w, b, u2):
    rows, dff = up_g.shape
    d = u2.shape[1]
    tm, tc = _row_tile(rows), _pick(dff, (256, 128))
    nb = dff // tc
    kw = FFN_CONV

    sb = 16

    def body(g_ref, gp_ref, gn_ref, v_ref, vp_ref, vn_ref, d_ref, dn_ref, wg_ref, bg_ref, wv_ref, bv_ref, u_ref,
             dxg_ref, dxv_ref, dwg_ref, dbg_ref, dwv_ref, dbv_ref, gwg_ref, gwv_ref, xg_s, xv_s, dd_s, og_s, ov_s):
        i = pl.program_id(1)
        last = i == rows // tm - 1
        for x_s, x_ref, xp_ref, xn_ref in ((xg_s, g_ref, gp_ref, gn_ref), (xv_s, v_ref, vp_ref, vn_ref)):
            x_s[0:HALO, :] = jnp.where(i == 0, 0.0, xp_ref[...].astype(F32))
            x_s[HALO:HALO + tm, :] = x_ref[...].astype(F32)
            x_s[HALO + tm:, :] = xn_ref[...].astype(F32)
        dd_s[0:tm, :] = d_ref[...].astype(F32)
        dd_s[tm:, :] = jnp.where(last, 0.0, dn_ref[...].astype(F32))

        wg = [wg_ref[k:k + 1, :] for k in range(kw)]
        wv = [wv_ref[k:k + 1, :] for k in range(kw)]
        bg, bv = bg_ref[...], bv_ref[...]

        def taps(x_s, e0, w, bias):
            win = x_s[pl.ds(e0 + HALO - sb, 2 * sb), :]
            sh = [win[sb:], pltpu.roll(win, 1, 0)[sb:], pltpu.roll(win, 2, 0)[sb:]]
            return bias + w[2] * sh[0] + w[1] * sh[1] + w[0] * sh[2], sh

        def dpre_of(e0):
            ag, sh_g = taps(xg_s, e0, wg, bg)
            av, sh_v = taps(xv_s, e0, wv, bv)
            dout = dd_s[pl.ds(e0, sb), :]
            s = jax.nn.sigmoid(ag)
            silu = ag * s
            return dout * av * (s + silu * (1.0 - s)), dout * silu, sh_g, sh_v

        def back(dp, nxt, w):
            cat = jnp.concatenate([dp, nxt], axis=0)
            return w[2] * dp + w[1] * pltpu.roll(cat, 2 * sb - 1, 0)[:sb] + w[0] * pltpu.roll(cat, 2 * sb - 2, 0)[:sb]

        nxt_g, nxt_v, _, _ = dpre_of(tm)
        acc_g = acc_v = tuple(jnp.zeros((sb, tc), F32) for _ in range(kw + 1))
        for e0 in range(tm - sb, -1, -sb):
            dpg, dpv, sh_g, sh_v = dpre_of(e0)
            og_s[e0:e0 + sb, :] = back(dpg, nxt_g, wg)
            ov_s[e0:e0 + sb, :] = back(dpv, nxt_v, wv)
            acc_g = tuple(a + dpg * t for a, t in zip(acc_g, (sh_g[2], sh_g[1], sh_g[0], 1.0)))
            acc_v = tuple(a + dpv * t for a, t in zip(acc_v, (sh_v[2], sh_v[1], sh_v[0], 1.0)))
            nxt_g, nxt_v = dpg, dpv

        @pl.when(i == 0)
        def _():
            for r in (dwg_ref, dbg_ref, dwv_ref, dbv_ref, gwg_ref, gwv_ref):
                r[...] = jnp.zeros_like(r)

        for acc, o_s, dx_ref, dw_ref, db_ref, gw_ref in ((acc_g, og_s, dxg_ref, dwg_ref, dbg_ref, gwg_ref),
                                                        (acc_v, ov_s, dxv_ref, dwv_ref, dbv_ref, gwv_ref)):
            dx = o_s[...].astype(ACT_DTYPE)
            dx_ref[...] = dx
            gw_ref[...] += _mxu(u_ref[...], dx, TN)
            for k in range(kw):
                dw_ref[k:k + 1, :] += jnp.sum(acc[k], axis=0, keepdims=True)
            db_ref[...] += jnp.sum(acc[kw], axis=0, keepdims=True)

    gcol, vcol = (lambda j: j), (lambda j: j + nb)
    main = lambda col: pl.BlockSpec((tm, tc), lambda j, i: (i, col(j)))
    par = lambda r, col: pl.BlockSpec((r, tc), lambda j, i: (0, col(j)))
    halos = lambda col: [_prev_halo_spec(tm, tc, col), _next_halo_spec(tm, rows, tc, col)]
    act_shape = jax.ShapeDtypeStruct((rows, dff), ACT_DTYPE)
    par_shapes = [jax.ShapeDtypeStruct((kw, dff), F32), jax.ShapeDtypeStruct((1, dff), F32)]
    gw_shape = jax.ShapeDtypeStruct((d, dff), F32)
    return pl.pallas_call(
        body, name="ffn_conv_bwd", grid=(nb, rows // tm),
        in_specs=[main(gcol)] + halos(gcol) + [main(gcol)] + halos(gcol) + [main(gcol), _next_halo_spec(tm, rows, tc, gcol),
                  par(kw, gcol), par(1, gcol), par(kw, vcol), par(1, vcol), pl.BlockSpec((tm, d), lambda j, i: (i, 0))],
        out_specs=[main(gcol), main(gcol), par(kw, gcol), par(1, gcol), par(kw, gcol), par(1, gcol), par(d, gcol), par(d, gcol)],
        out_shape=[act_shape, act_shape] + par_shapes + par_shapes + [gw_shape, gw_shape],
        scratch_shapes=[pltpu.VMEM((tm + 2 * HALO, tc), F32), pltpu.VMEM((tm + 2 * HALO, tc), F32),
                        pltpu.VMEM((tm + HALO, tc), F32), pltpu.VMEM((tm, tc), F32), pltpu.VMEM((tm, tc), F32)],
        compiler_params=_params(("parallel", "arbitrary")),
    )(up_g, up_g, up_g, up_v, up_v, up_v, dact, dact, w, b, w, b, u2)


def _ssd_scalars(dtr, dt_bias, a_log, live):
    q = CHUNK
    pre = dtr + dt_bias
    dt = jnp.where(live, jax.nn.softplus(pre), 0.0)
    a_neg = -jnp.exp(a_log)
    li = lax.broadcasted_iota(jnp.int32, (q, q), 0)
    si = lax.broadcasted_iota(jnp.int32, (q, q), 1)
    causal = li >= si
    tri = jnp.where(causal, 1.0, 0.0).astype(F32)
    a_cs = sum(_mxu(tri, p, NN) for p in _split(dt * a_neg, 3))
    return pre, dt, a_neg, a_cs, causal, tri


def _head_select():
    r = lax.broadcasted_iota(jnp.int32, (LANES, SSD_GW), 0)
    c = lax.broadcasted_iota(jnp.int32, (LANES, SSD_GW), 1)
    return jnp.where(c // SSD_HEAD_DIM == r, 1.0, 0.0).astype(MXU_DTYPE)


def _split(t, parts):
    out, rem = [], t
    for _ in range(parts):
        p = rem.astype(MXU_DTYPE)
        out.append(p)
        rem = rem - p.astype(F32)
    return out


def _stacked(ts, parts, sel, dn):
    out = _mxu(jnp.concatenate([p for t in ts for p in _split(t, parts)], axis=0), sel, dn)
    res, r0 = [], 0
    for t in ts:
        r = t.shape[0]
        res.append(sum(out[r0 + k * r:r0 + (k + 1) * r] for k in range(parts)))
        r0 += parts * r
    return res


def _head_cols(ts, sel):
    return _stacked(ts, 2, sel, NN)


def _head_sums(ts, sel):
    return _stacked(ts, 3, sel, NT)


def _half_masks():
    lane = lax.broadcasted_iota(jnp.int32, (CHUNK, LANES), 1)
    return lane < SSD_HEAD_DIM, lane >= SSD_HEAD_DIM


def _ssd_scan(xs, bm, cm, dtr, prev, dt_bias, a_log, d_skip, live):
    q = CHUNK
    sel = _head_select()
    _, dt, _, a_cs, causal, _ = _ssd_scalars(dtr, dt_bias, a_log, live)
    a_cs_t = a_cs.T
    a_end = a_cs[q - 1:q, :]
    dt_x, e_x, w_x, d_x = _head_cols([dt, jnp.exp(a_cs), jnp.exp(a_end - a_cs), jnp.broadcast_to(d_skip, (16, LANES))], sel)
    xdt = xs * dt_x
    cb = _dot(cm, bm, NT)
    y = _dot(cm, prev) * e_x + d_x[0:1] * xs
    new = prev * e_x[q - 1:q, :] + _dot(bm, xdt * w_x, TN)
    masks = _half_masks()
    ys = []
    for pp in range(SSD_HPG // 2):
        xpair = xdt[:, pp * LANES:(pp + 1) * LANES]
        acc = jnp.zeros((q, LANES), F32)
        for half in range(2):
            hh = 2 * pp + half
            decay = jnp.exp(jnp.where(causal, a_cs[:, hh:hh + 1] - a_cs_t[hh:hh + 1, :], -jnp.inf))
            acc = acc + _dot(cb * decay, jnp.where(masks[half], xpair, 0.0))
        ys.append(acc)
    return y + jnp.concatenate(ys, axis=1), new


def _ssd_gate(y, z, nw):
    yz = y * _silu(z)
    return yz * lax.rsqrt(jnp.mean(yz * yz, axis=-1, keepdims=True) + EPS) * nw


def _ssd_scan_bwd(xs, bm, cm, dtr, prev, dt_bias, a_log, d_skip, live, dy, dnew):
    q = CHUNK
    sel = _head_select()
    pre, dt, a_neg, a_cs, causal, tri = _ssd_scalars(dtr, dt_bias, a_log, live)
    a_cs_t = a_cs.T
    a_end = a_cs[q - 1:q, :]
    dt_x, e_x, w_x, d_x = _head_cols([dt, jnp.exp(a_cs), jnp.exp(a_end - a_cs), jnp.broadcast_to(d_skip, (16, LANES))], sel)
    g_x, d_x = e_x[q - 1:q, :], d_x[0:1]
    xdt = xs * dt_x
    u = xdt * w_x
    cb = _mxu(cm, bm, NT)
    cs = _mxu(cm, prev, NN)
    dye = dy * e_x
    dcm = _mxu(dye, prev, NT)
    dprev = _mxu(cm, dye, TN) + dnew * g_x
    dacs_x = dye * cs
    dbm = _mxu(u, dnew, NT)
    du = _mxu(bm, dnew, NN)
    dw_x = du * u
    dacs_x = dacs_x - dw_x
    dend_x = jnp.sum(dw_x + dnew * prev * g_x, axis=0, keepdims=True)
    dxdt = du * w_x
    lane = lax.broadcasted_iota(jnp.int32, (q, LANES), 1)
    sub = lax.broadcasted_iota(jnp.int32, (q, LANES), 0)
    dcb = jnp.zeros((q, q), F32)
    dacs = jnp.zeros((q, LANES), F32)
    dacs_t = jnp.zeros((q, LANES), F32)
    masks = _half_masks()
    dxdt_p = []
    for pp in range(SSD_HPG // 2):
        ps = slice(pp * LANES, (pp + 1) * LANES)
        acc = jnp.zeros((q, LANES), F32)
        for half in range(2):
            hh = 2 * pp + half
            decay = jnp.exp(jnp.where(causal, a_cs[:, hh:hh + 1] - a_cs_t[hh:hh + 1, :], -jnp.inf))
            m = cb * decay
            dyh = jnp.where(masks[half], dy[:, ps], 0.0)
            dm = _mxu(dyh, xdt[:, ps], NT)
            acc = acc + _mxu(m, dyh, TN)
            dcb = dcb + dm * decay
            p = dm * m
            dacs = jnp.where(lane == hh, jnp.sum(p, axis=1, keepdims=True), dacs)
            dacs_t = jnp.where(sub == hh, jnp.sum(p, axis=0, keepdims=True), dacs_t)
        dxdt_p.append(acc)
    dcm = dcm + _mxu(dcb, bm, NN)
    dbm = dbm + _mxu(dcb, cm, TN)
    dxdt = dxdt + jnp.concatenate(dxdt_p, axis=1)
    dxs = dy * d_x + dxdt * dt_x
    rows_x = jnp.concatenate([dend_x, jnp.sum(dy * xs, axis=0, keepdims=True), jnp.zeros((14, SSD_GW), F32)], axis=0)
    dacs_h, ddt_h, rows = _head_sums([dacs_x, dxdt * xs, rows_x], sel)
    dacs = dacs - dacs_t.T + dacs_h
    dacs = dacs + jnp.where(sub == q - 1, rows[0:1], 0.0)
    tri_t = jnp.where(causal, 0.0, 1.0).astype(F32) + jnp.where(lane == sub, 1.0, 0.0)
    da = sum(_mxu(tri_t, p, NN) for p in _split(dacs, 3))
    ddt = ddt_h + da * a_neg
    dalog = jnp.sum(da * dt, axis=0, keepdims=True) * a_neg
    ddtr = jnp.where(live, ddt * jax.nn.sigmoid(pre), 0.0)
    dbias = jnp.sum(ddtr, axis=0, keepdims=True)
    return dxs, dbm, dcm, ddtr, dprev, dbias, dalog, rows[1:2]


def _chunks_per_step(nc):
    return 13 if nc % 13 == 0 else 1


def _ssd_specs(rev, nc):
    per = _chunks_per_step(nc)
    steps = nc // per
    sidx = (lambda s: steps - 1 - s) if rev else (lambda s: s)
    nb_b = SSD_D_INNER // SSD_STATE
    row = lambda width, col=lambda g: g: pl.BlockSpec((per * CHUNK, width), lambda g, s: (sidx(s), col(g)))
    par = lambda width: pl.BlockSpec((1, width), lambda g, s: (0, g))
    state = lambda: pl.BlockSpec((per, 1, SSD_STATE, SSD_GW), lambda g, s: (sidx(s), g, 0, 0))
    xbc = [row(SSD_GW), row(SSD_STATE, lambda g: nb_b + g), row(SSD_STATE, lambda g: nb_b + SSD_GROUPS + g)]
    return per, steps, sidx, row, par, state, xbc


def _ssd_fwd(xbc_c, dtr, z, dt_bias, a_log, d_skip, nw):
    rows = z.shape[0]
    nc = rows // CHUNK
    per, steps, _, row, par, state, xbc = _ssd_specs(False, nc)

    def body(xs_ref, b_ref, c_ref, dt_ref, z_ref, bias_ref, al_ref, dk_ref, nw_ref, o_ref, y_ref, st_ref, carry):
        s = pl.program_id(1)

        @pl.when(s == 0)
        def _():
            carry[...] = jnp.zeros_like(carry)

        for j in range(per):
            rs = pl.ds(j * CHUNK, CHUNK)
            live = (s * per + j) * CHUNK + lax.broadcasted_iota(jnp.int32, (CHUNK, 1), 0) >= FRONT
            prev = carry[...]
            st_ref[j, 0] = prev
            y, new = _ssd_scan(xs_ref[rs, :].astype(F32), b_ref[rs, :].astype(F32), c_ref[rs, :].astype(F32), dt_ref[rs, :],
                               prev, bias_ref[...], al_ref[...], dk_ref[...], live)
            y_ref[rs, :] = y.astype(ACT_DTYPE)
            o_ref[rs, :] = _ssd_gate(y, z_ref[rs, :].astype(F32), nw_ref[...]).astype(ACT_DTYPE)
            carry[...] = new

    act = jax.ShapeDtypeStruct((rows, SSD_D_INNER), ACT_DTYPE)
    return pl.pallas_call(
        body, name="ssd_fwd", grid=(SSD_GROUPS, steps),
        in_specs=xbc + [row(LANES), row(SSD_GW), par(LANES), par(LANES), par(LANES), par(SSD_GW)],
        out_specs=[row(SSD_GW), row(SSD_GW), state()],
        out_shape=[act, act, jax.ShapeDtypeStruct((nc, SSD_GROUPS, SSD_STATE, SSD_GW), F32)],
        scratch_shapes=[pltpu.VMEM((SSD_STATE, SSD_GW), F32)],
        compiler_params=_params(("parallel", "arbitrary")),
    )(xbc_c, xbc_c, xbc_c, dtr, z, dt_bias, a_log, d_skip, nw)


def _ssd_gate_bwd(y, z, d_bs, w_bs, nw, dproj, col0):
    rows = y.shape[0]
    d = d_bs.shape[1]
    tm = _row_tile(rows)
    assert col0 % SSD_GW == 0

    def body(y_ref, z_ref, db_ref, wb_ref, nw_ref, dp_ref, dy_ref, dz_ref, dnw_ref):
        yv, zv = y_ref[...].astype(F32), z_ref[...].astype(F32)
        dov = _mxu(db_ref[...], wb_ref[...], NT)
        s = jax.nn.sigmoid(zv)
        silu = zv * s
        yz = yv * silu
        r = lax.rsqrt(jnp.mean(yz * yz, axis=-1, keepdims=True) + EPS)
        yhat = yz * r
        dn = dov * nw_ref[...]
        dyz = r * (dn - yhat * jnp.mean(dn * yhat, axis=-1, keepdims=True))
        dy_ref[...] = (dyz * silu).astype(ACT_DTYPE)
        dz_ref[...] = (dyz * yv * (s + silu * (1.0 - s))).astype(ACT_DTYPE)

        @pl.when(pl.program_id(1) == 0)
        def _():
            dnw_ref[...] = jnp.zeros_like(dnw_ref)

        dnw_ref[...] += jnp.sum(dov * yhat, axis=0, keepdims=True)

    spec = pl.BlockSpec((tm, SSD_GW), lambda g, i: (i, g))
    par = pl.BlockSpec((1, SSD_GW), lambda g, i: (0, g))
    act = jax.ShapeDtypeStruct((rows, SSD_D_INNER), ACT_DTYPE)
    return pl.pallas_call(
        body, name="ssd_gate_bwd", grid=(SSD_GROUPS, rows // tm),
        in_specs=[spec, spec, pl.BlockSpec((tm, d), lambda g, i: (i, 0)), pl.BlockSpec((SSD_GW, d), lambda g, i: (g, 0)),
                  par, pl.BlockSpec(memory_space=pl.ANY)],
        out_specs=[spec, pl.BlockSpec((tm, SSD_GW), lambda g, i: (i, col0 // SSD_GW + g)), par],
        out_shape=[act, jax.ShapeDtypeStruct(dproj.shape, dproj.dtype), jax.ShapeDtypeStruct((1, SSD_D_INNER), F32)],
        input_output_aliases={5: 1}, compiler_params=_params(("parallel", "arbitrary")),
    )(y, z, d_bs, w_bs, nw, dproj)


def _ssd_bwd(xbc_c, dtr, dt_bias, a_log, d_skip, states, dy, dproj, col0):
    rows = dy.shape[0]
    nc = rows // CHUNK
    per, steps, sidx, row, par, state, xbc = _ssd_specs(True, nc)
    assert col0 % LANES == 0

    def body(xs_ref, b_ref, c_ref, dt_ref, bias_ref, al_ref, dk_ref, st_ref, dy_ref, dp_ref,
             dxs_ref, db_ref, dc_ref, ddt_ref, dbias_ref, dal_ref, ddk_ref, carry):
        s = pl.program_id(1)

        @pl.when(s == 0)
        def _():
            carry[...] = jnp.zeros_like(carry)
            for r in (dbias_ref, dal_ref, ddk_ref):
                r[...] = jnp.zeros_like(r)

        for j in reversed(range(per)):
            rs = pl.ds(j * CHUNK, CHUNK)
            live = (sidx(s) * per + j) * CHUNK + lax.broadcasted_iota(jnp.int32, (CHUNK, 1), 0) >= FRONT
            dxs, dbm, dcm, ddt, dprev, dbias, dal, ddk = _ssd_scan_bwd(
                xs_ref[rs, :].astype(F32), b_ref[rs, :].astype(F32), c_ref[rs, :].astype(F32), dt_ref[rs, :], st_ref[j, 0],
                bias_ref[...], al_ref[...], dk_ref[...], live, dy_ref[rs, :].astype(F32), carry[...])
            dxs_ref[rs, :] = dxs.astype(ACT_DTYPE)
            db_ref[rs, :] = dbm.astype(ACT_DTYPE)
            dc_ref[rs, :] = dcm.astype(ACT_DTYPE)
            ddt_ref[rs, :] = ddt.astype(ACT_DTYPE)
            carry[...] = dprev
            dbias_ref[...] += dbias
            dal_ref[...] += dal
            ddk_ref[...] += ddk

    bc = jax.ShapeDtypeStruct((rows, SSD_GROUPS * SSD_STATE), ACT_DTYPE)
    head = jax.ShapeDtypeStruct((1, SSD_GROUPS * LANES), F32)
    return pl.pallas_call(
        body, name="ssd_bwd", grid=(SSD_GROUPS, steps),
        in_specs=xbc + [row(LANES), par(LANES), par(LANES), par(LANES), state(), row(SSD_GW), pl.BlockSpec(memory_space=pl.ANY)],
        out_specs=[row(SSD_GW), row(SSD_STATE), row(SSD_STATE), row(LANES, lambda g: col0 // LANES + g),
                   par(LANES), par(LANES), par(LANES)],
        out_shape=[jax.ShapeDtypeStruct((rows, SSD_D_INNER), ACT_DTYPE), bc, bc,
                   jax.ShapeDtypeStruct(dproj.shape, dproj.dtype), head, head, head],
        input_output_aliases={9: 3}, scratch_shapes=[pltpu.VMEM((SSD_STATE, SSD_GW), F32)],
        compiler_params=_params(("parallel", "arbitrary")),
    )(xbc_c, xbc_c, xbc_c, dtr, dt_bias, a_log, d_skip, states, dy, dproj)


def _rotary_tables(rows):
    pos = np.arange(rows, dtype=np.float32) - np.float32(FRONT)
    inv_freq = np.float32(ROPE_BASE) ** (-np.linspace(0.0, 1.0, RET_QK // 2, dtype=np.float32))
    ang = (pos[:, None] * inv_freq[None, :]).astype(np.float32).astype(np.float64)
    lgam = np.log(1.0 - 2.0 ** (-5.0 - np.arange(RET_HEADS, dtype=np.float64))).astype(np.float32)
    lgam = np.broadcast_to(lgam[:, None, None], (RET_HEADS, 8, LANES))
    return jnp.asarray(np.cos(ang).astype(np.float32)), jnp.asarray(np.sin(ang).astype(np.float32)), jnp.asarray(lgam)


def _rotary(t, cos, sin):
    half = t.shape[-1] // 2
    t1, t2 = t[:, :half], t[:, half:]
    return jnp.concatenate([t1 * cos - t2 * sin, t2 * cos + t1 * sin], axis=1)


def _ret_chunk(qh, kh, vh, gh, prev, cos, sin, lg):
    q = CHUNK
    qr = _rotary(qh, cos, sin)
    kr = _rotary(kh, cos, sin) * (RET_QK ** -0.5)
    li = lax.broadcasted_iota(jnp.int32, (q, q), 0)
    si = lax.broadcasted_iota(jnp.int32, (q, q), 1)
    dist = (li - si).astype(F32)
    decay = jnp.exp(jnp.where(li >= si, dist * lg, -jnp.inf))
    idx = lax.broadcasted_iota(jnp.int32, (q, 1), 0).astype(F32)
    scores = _dot(qr, kr, NT) * decay
    out = _dot(scores, vh)
    kv = _dot(kr * jnp.exp((q - 1.0 - idx) * lg), vh, TN)
    out = out + _dot(qr, prev) * jnp.exp((idx + 1.0) * lg)
    new = prev * jnp.exp(q * lg) + kv
    out = out * lax.rsqrt(jnp.mean(out * out, axis=-1, keepdims=True) + EPS)
    return _silu(gh) * out, new


def _ret_specs(rev, nc):
    per = _chunks_per_step(nc)
    steps = nc // per
    sidx = (lambda s: steps - 1 - s) if rev else (lambda s: s)
    row = lambda width: pl.BlockSpec((per * CHUNK, width), lambda h, s: (sidx(s), h))
    tab = lambda: pl.BlockSpec((per * CHUNK, RET_QK // 2), lambda h, s: (sidx(s), 0))
    lgs = lambda: pl.BlockSpec((1, 8, LANES), lambda h, s: (h, 0, 0))
    state = lambda: pl.BlockSpec((per, 1, RET_QK, RET_V), lambda h, s: (sidx(s), h, 0, 0))
    part = lambda width, k: pl.BlockSpec((per * CHUNK, width), lambda h, s: (sidx(s), h * (RET_HW // width) + k))
    ins = [part(RET_QK, 0), part(RET_QK, 1), part(RET_V, 1), part(RET_V, 2), tab(), tab(), lgs()]
    return per, steps, sidx, row, state, ins


def _ret_fwd(qkvg, cos, sin, lgam):
    rows = qkvg.shape[0]
    nc = rows // CHUNK
    per, steps, _, row, state, ins = _ret_specs(False, nc)
    q = k = v = g = qkvg

    def body(q_ref, k_ref, v_ref, g_ref, cos_ref, sin_ref, lg_ref, y_ref, st_ref, carry):
        @pl.when(pl.program_id(1) == 0)
        def _():
            carry[...] = jnp.zeros_like(carry)

        for j in range(per):
            rs = pl.ds(j * CHUNK, CHUNK)
            prev = carry[...]
            st_ref[j, 0] = prev.astype(ACT_DTYPE)
            out, new = _ret_chunk(q_ref[rs, :].astype(F32), k_ref[rs, :].astype(F32), v_ref[rs, :].astype(F32),
                                  g_ref[rs, :].astype(F32), prev, cos_ref[rs, :], sin_ref[rs, :], lg_ref[0, 0:1, 0:1])
            y_ref[rs, :] = out.astype(ACT_DTYPE)
            carry[...] = new

    return pl.pallas_call(
        body, name="ret_fwd", grid=(RET_HEADS, steps), in_specs=ins, out_specs=[row(RET_V), state()],
        out_shape=[jax.ShapeDtypeStruct((rows, RET_HEADS * RET_V), ACT_DTYPE),
                   jax.ShapeDtypeStruct((nc, RET_HEADS, RET_QK, RET_V), ACT_DTYPE)],
        scratch_shapes=[pltpu.VMEM((RET_QK, RET_V), F32)],
        compiler_params=_params(("parallel", "arbitrary")),
    )(q, k, v, g, cos, sin, lgam)


def _ret_bwd(qkvg, cos, sin, lgam, states, d_br, w_br, dproj):
    rows = qkvg.shape[0]
    d = d_br.shape[1]
    nc = rows // CHUNK
    per, steps, sidx, row, state, ins = _ret_specs(True, nc)

    def body(q_ref, k_ref, v_ref, g_ref, cos_ref, sin_ref, lg_ref, st_ref, db_ref, wb_ref, dp_ref, o_ref, carry):
        @pl.when(pl.program_id(1) == 0)
        def _():
            carry[...] = jnp.zeros_like(carry)

        for j in reversed(range(per)):
            rs = pl.ds(j * CHUNK, CHUNK)
            fn = functools.partial(_ret_chunk, cos=cos_ref[rs, :], sin=sin_ref[rs, :], lg=lg_ref[0, 0:1, 0:1])
            _, vjp = jax.vjp(fn, q_ref[rs, :].astype(F32), k_ref[rs, :].astype(F32), v_ref[rs, :].astype(F32),
                             g_ref[rs, :].astype(F32), st_ref[j, 0].astype(F32))
            dq, dk, dv, dg, dprev = vjp((_mxu(db_ref[rs, :], wb_ref[...], NT), carry[...]))
            o_ref[rs, 0:RET_QK] = dq.astype(ACT_DTYPE)
            o_ref[rs, RET_QK:2 * RET_QK] = dk.astype(ACT_DTYPE)
            o_ref[rs, 2 * RET_QK:2 * RET_QK + RET_V] = dv.astype(ACT_DTYPE)
            o_ref[rs, 2 * RET_QK + RET_V:RET_HW] = dg.astype(ACT_DTYPE)
            carry[...] = dprev

    return pl.pallas_call(
        body, name="ret_bwd", grid=(RET_HEADS, steps),
        in_specs=ins + [state(), pl.BlockSpec((per * CHUNK, d), lambda h, s: (sidx(s), 0)),
                        pl.BlockSpec((RET_V, d), lambda h, s: (h, 0)), pl.BlockSpec(memory_space=pl.ANY)],
        out_specs=pl.BlockSpec((per * CHUNK, RET_HW), lambda h, s: (sidx(s), h)),
        out_shape=jax.ShapeDtypeStruct(dproj.shape, dproj.dtype), input_output_aliases={10: 0},
        scratch_shapes=[pltpu.VMEM((RET_QK, RET_V), F32)],
        compiler_params=_params(("parallel", "arbitrary")),
    )(qkvg, qkvg, qkvg, qkvg, cos, sin, lgam, states, d_br, w_br, dproj)


def _merge_fwd(bs, br, gates):
    rows, d = bs.shape
    tm = _row_tile(rows)

    def body(bs_ref, br_ref, gs_ref, gr_ref, o_ref):
        o_ref[...] = (jax.nn.sigmoid(gs_ref[...].astype(F32)) * bs_ref[...].astype(F32)
                      + jax.nn.sigmoid(gr_ref[...].astype(F32)) * br_ref[...].astype(F32)).astype(ACT_DTYPE)

    spec = pl.BlockSpec((tm, d), lambda i: (i, 0))
    return pl.pallas_call(
        body, name="merge_fwd", grid=(rows // tm,), in_specs=[spec, spec, spec, pl.BlockSpec((tm, d), lambda i: (i, 1))],
        out_specs=spec, out_shape=jax.ShapeDtypeStruct((rows, d), ACT_DTYPE), compiler_params=_params(("parallel",)),
    )(bs, br, gates, gates)


def _merge_bwd(dh1, w_o, bs, br, gates, dproj, col0):
    rows, d = bs.shape
    tm = _row_tile(rows)
    assert col0 % (2 * d) == 0

    def body(dh_ref, wo_ref, bs_ref, br_ref, gs_ref, gr_ref, dp_ref, dbs_ref, dbr_ref, dg_ref):
        dmv = _mxu(dh_ref[...], wo_ref[...], NT)
        for k, (b_ref, g_ref, db_ref) in enumerate(((bs_ref, gs_ref, dbs_ref), (br_ref, gr_ref, dbr_ref))):
            s = jax.nn.sigmoid(g_ref[...].astype(F32))
            db_ref[...] = (dmv * s).astype(ACT_DTYPE)
            dg_ref[:, k * d:(k + 1) * d] = (dmv * b_ref[...].astype(F32) * s * (1.0 - s)).astype(ACT_DTYPE)

    spec = pl.BlockSpec((tm, d), lambda i: (i, 0))
    shp = jax.ShapeDtypeStruct((rows, d), ACT_DTYPE)
    return pl.pallas_call(
        body, name="merge_bwd", grid=(rows // tm,),
        in_specs=[spec, pl.BlockSpec(w_o.shape, lambda i: (0, 0)), spec, spec, spec, pl.BlockSpec((tm, d), lambda i: (i, 1)),
                  pl.BlockSpec(memory_space=pl.ANY)],
        out_specs=[spec, spec, pl.BlockSpec((tm, 2 * d), lambda i: (i, col0 // (2 * d)))],
        out_shape=[shp, shp, jax.ShapeDtypeStruct(dproj.shape, dproj.dtype)], input_output_aliases={6: 2},
        compiler_params=_params(("parallel",)),
    )(dh1, w_o, bs, br, gates, gates, dproj)


def _place():
    x, y, c = lax.axis_index("x"), lax.axis_index("y"), lax.axis_index("c")
    return x, y, c


def _slot(p):
    return 4 * p[0] + 2 * p[1] + p[2]


def _allgather(arrs, name):
    n = len(arrs)
    any_spec = pl.BlockSpec(memory_space=pl.ANY)

    def body(*refs):
        ins, outs = refs[:n], refs[n:2 * n]
        send_sems, recv_sems, local_sems = refs[2 * n:]
        x, y, c = _place()
        me, sibling = (x, y, c), (x, y, 1 - c)
        chips = [(1 - x, y), (x, 1 - y), (1 - x, 1 - y)]

        def copy(a, k, block, to, src=None):
            dst = outs[a].at[_slot(block)]
            return pltpu.make_async_remote_copy(
                src_ref=dst if src is None else src, dst_ref=dst, send_sem=send_sems.at[a * 7 + k],
                recv_sem=recv_sems.at[a * 7 + k], device_id=to, device_id_type=MESH)

        mine, first, passed = [], [], []
        for a in range(n):
            cp = pltpu.make_async_copy(ins[a], outs[a].at[_slot(me)], local_sems.at[a])
            cp.start()
            mine.append(cp)
            first.append(copy(a, 0, me, sibling, src=ins[a]))
            first += [copy(a, 1 + j, me, (*chip, c), src=ins[a]) for j, chip in enumerate(chips)]
        for cp in first:
            cp.start()
        for j, chip in enumerate(chips):
            for a in range(n):
                copy(a, 1 + j, (*chip, c), me).wait_recv()
                cp = copy(a, 4 + j, (*chip, c), sibling)
                cp.start()
                passed.append(cp)
        for a in range(n):
            copy(a, 0, sibling, me).wait_recv()
            for j, chip in enumerate(chips):
                copy(a, 4 + j, (*chip, 1 - c), me).wait_recv()
        for cp in first + passed:
            cp.wait_send()
        for cp in mine:
            cp.wait()

    return pl.pallas_call(
        body, name=name, in_specs=[any_spec] * n, out_specs=[any_spec] * n,
        out_shape=[jax.ShapeDtypeStruct((N_DEV,) + a.shape, a.dtype) for a in arrs],
        scratch_shapes=[pltpu.SemaphoreType.DMA((7 * n,)), pltpu.SemaphoreType.DMA((7 * n,)), pltpu.SemaphoreType.DMA((n,))],
    )(*arrs)


def _peers():
    x, y, c = _place()
    return (x, y, c), [(x ^ dx, y ^ dy, c ^ dc) for dx in (0, 1) for dy in (0, 1) for dc in (0, 1)][1:]


def _exchange_copies(srcs, lands, send_sems, recv_sems, scatter, sender):
    me, peers = _peers()
    out = []
    for a, (src, land) in enumerate(zip(srcs, lands, strict=True)):
        for k, peer in enumerate(peers):
            src_ref = src.at[_slot(peer)] if scatter else src
            out.append(pltpu.make_async_remote_copy(
                src_ref=src_ref, dst_ref=land.at[_slot(me if sender else peer)], send_sem=send_sems.at[a * 7 + k],
                recv_sem=recv_sems.at[a * 7 + k], device_id=peer, device_id_type=MESH))
    return out


_HBM = pl.BlockSpec(memory_space=pltpu.HBM)
_SEM = pl.BlockSpec(memory_space=pltpu.SEMAPHORE)
_EFFECT = pltpu.SideEffectType.DATAFLOW_SIDE_EFFECTING


def _exchange_start(srcs, scatter, name, after=None):
    n = len(srcs)
    land_shapes = [s.shape if scatter else (N_DEV,) + s.shape for s in srcs]
    n_in = 2 * n + (after is not None)

    def body(*refs):
        for cp in _exchange_copies(refs[:n], refs[n:2 * n], refs[n_in], refs[n_in + 1], scatter, True):
            cp.start()
        refs[-1][...] = jnp.zeros_like(refs[-1])

    args = [pltpu.with_memory_space_constraint(s, pltpu.HBM) for s in srcs]
    args += [pltpu.with_memory_space_constraint(lax.empty(shp, s.dtype), pltpu.HBM) for s, shp in zip(srcs, land_shapes)]
    thru_shapes = tuple(pltpu.HBM(a.shape, a.dtype) for a in args)
    extra = [] if after is None else [after]
    outs = pl.pallas_call(
        body, name=name,
        out_shape=(pltpu.SemaphoreType.DMA((7 * n,)), pltpu.SemaphoreType.DMA((7 * n,))) + thru_shapes
        + (jax.ShapeDtypeStruct((8, LANES), F32),),
        in_specs=[_HBM] * (2 * n) + [pl.BlockSpec(memory_space=pl.ANY)] * len(extra),
        out_specs=(_SEM, _SEM) + (_HBM,) * (2 * n) + (pl.BlockSpec(memory_space=pltpu.VMEM),),
        input_output_aliases={i: 2 + i for i in range(2 * n)},
        compiler_params=pltpu.CompilerParams(has_side_effects=_EFFECT),
    )(*args, *extra)
    return outs[:-1], outs[-1]


def _exchange_wait(handle, scatter, after, name):
    n = (len(handle) - 2) // 2
    thru = handle[2:]

    def body(*refs):
        for cp in _exchange_copies(refs[:n], refs[n:2 * n], refs[2 * n], refs[2 * n + 1], scatter, False):
            cp.wait_send()
            cp.wait_recv()

    outs = pl.pallas_call(
        body, name=name, out_shape=tuple(pltpu.HBM(t.shape, t.dtype) for t in thru),
        in_specs=[_HBM] * (2 * n) + [_SEM, _SEM, pl.BlockSpec(memory_space=pl.ANY)], out_specs=(_HBM,) * (2 * n),
        input_output_aliases={i: i for i in range(2 * n)},
        compiler_params=pltpu.CompilerParams(has_side_effects=_EFFECT),
    )(*thru, handle[0], handle[1], after)
    return list(outs[:n]), list(outs[n:])


def _allreduce_small(pack):
    rows, lanes = pack.shape

    def body(x_ref, o_ref, buf, send_sems, recv_sems):
        x, y, c = _place()
        me, sibling = (x, y, c), (x, y, 1 - c)
        chips = [(1 - x, y), (x, 1 - y), (1 - x, 1 - y)]

        def copy(k, block, to, src=None):
            dst = buf.at[_slot(block)]
            return pltpu.make_async_remote_copy(
                src_ref=dst if src is None else src, dst_ref=dst, send_sem=send_sems.at[k], recv_sem=recv_sems.at[k],
                device_id=to, device_id_type=MESH)

        buf[_slot(me)] = x_ref[...]
        first = [copy(0, me, sibling, src=x_ref)]
        first += [copy(1 + j, me, (*chip, c), src=x_ref) for j, chip in enumerate(chips)]
        for cp in first:
            cp.start()
        passed = [copy(4 + j, (*chip, c), sibling) for j, chip in enumerate(chips)]
        for j, chip in enumerate(chips):
            copy(1 + j, (*chip, c), me).wait_recv()
            passed[j].start()
        copy(0, sibling, me).wait_recv()
        for j, chip in enumerate(chips):
            copy(4 + j, (*chip, 1 - c), me).wait_recv()
        for cp in first + passed:
            cp.wait_send()
        acc = buf[0]
        for i in range(1, N_DEV):
            acc = acc + buf[i]
        o_ref[...] = acc

    vmem = pl.BlockSpec(memory_space=pltpu.VMEM)
    return pl.pallas_call(
        body, name="allreduce_small", in_specs=[vmem], out_specs=vmem,
        out_shape=jax.ShapeDtypeStruct((rows, lanes), F32),
        scratch_shapes=[pltpu.VMEM((N_DEV, rows, lanes), F32), pltpu.SemaphoreType.DMA((7,)), pltpu.SemaphoreType.DMA((7,))],
        compiler_params=pltpu.CompilerParams(vmem_limit_bytes=VMEM_LIMIT),
    )(pack)


def _adamw(w, g, m, v):
    m = ADAM_B1 * m + (1.0 - ADAM_B1) * g
    v = ADAM_B2 * v + (1.0 - ADAM_B2) * jnp.square(g)
    m_hat = m / (1.0 - ADAM_B1 ** ADAM_STEP)
    v_hat = v / (1.0 - ADAM_B2 ** ADAM_STEP)
    delta = -ADAM_LR * (m_hat / (jnp.sqrt(v_hat) + ADAM_EPS) + ADAM_WD * w)
    return delta, m, v


def _adam_shard(own, parts, w, m, v, name):
    r, c = w.shape
    tr = _pick(r, (128, 64, 32, 16, 8))

    def body(own_ref, p_ref, w_ref, m_ref, v_ref, g_ref, d_ref, nm_ref, nv_ref):
        _, peers = _peers()
        g = own_ref[...].astype(F32)
        for peer in peers:
            g = g + p_ref[_slot(peer)].astype(F32)
        g_ref[...] = g
        d_ref[...], nm_ref[...], nv_ref[...] = _adamw(w_ref[...], g, m_ref[...], v_ref[...])

    spec = pl.BlockSpec((tr, c), lambda i: (i, 0))
    shp = jax.ShapeDtypeStruct((r, c), F32)
    return pl.pallas_call(
        body, name=name, grid=(r // tr,),
        in_specs=[spec, pl.BlockSpec((N_DEV, tr, c), lambda i: (0, i, 0)), spec, spec, spec], out_specs=[spec] * 4,
        out_shape=[shp] * 4, compiler_params=_params(("parallel",)),
    )(own, parts, w, m, v)


def _adam_small(w, g, m, v):
    r, c = w.shape

    def body(w_ref, g_ref, m_ref, v_ref, d_ref, nm_ref, nv_ref):
        d_ref[...], nm_ref[...], nv_ref[...] = _adamw(w_ref[...], g_ref[...], m_ref[...], v_ref[...])

    shp = jax.ShapeDtypeStruct((r, c), F32)
    return pl.pallas_call(body, name="adam_small", out_shape=[shp] * 3)(w, g, m, v)


def _column_plan(pieces, shard_w):
    plan = []
    for c0, width, d0 in pieces:
        c = c0
        while c < c0 + width:
            s, a = divmod(c, shard_w)
            w = min(c0 + width - c, shard_w - a)
            plan.append((s, a, w, d0 + c - c0))
            c += w
    return plan


def _cols_from_shards(g, plan, out_w, zero, name):
    _, r, sw = g.shape
    tr = _pick(r, (128,))

    def body(x_ref, o_ref):
        for d0, w in zero:
            o_ref[:, d0:d0 + w] = jnp.zeros((tr, w), g.dtype)
        for s, a, w, d0 in plan:
            o_ref[:, d0:d0 + w] = x_ref[s, :, a:a + w]

    return pl.pallas_call(
        body, name=name, grid=(r // tr,), in_specs=[pl.BlockSpec((N_DEV, tr, sw), lambda i: (0, i, 0))],
        out_specs=pl.BlockSpec((tr, out_w), lambda i: (i, 0)), out_shape=jax.ShapeDtypeStruct((r, out_w), g.dtype),
        compiler_params=_params(("parallel",)),
    )(g)


def _shards_from_cols(srcs, plans, shard_w, name):
    r = srcs[0].shape[0]
    tr = _pick(r, (128,))
    n = len(srcs)

    def body(*refs):
        o_ref = refs[n]
        for x_ref, plan in zip(refs[:n], plans, strict=True):
            for s, a, w, d0 in plan:
                o_ref[s, :, a:a + w] = x_ref[:, d0:d0 + w].astype(COMM_DTYPE)

    return pl.pallas_call(
        body, name=name, grid=(r // tr,), in_specs=[pl.BlockSpec((tr, t.shape[1]), lambda i: (i, 0)) for t in srcs],
        out_specs=pl.BlockSpec((N_DEV, tr, shard_w), lambda i: (0, i, 0)),
        out_shape=jax.ShapeDtypeStruct((N_DEV, r, shard_w), COMM_DTYPE), compiler_params=_params(("parallel",)),
    )(*srcs)


def _pack(arrs):
    rows = []
    for a in arrs:
        flat = a.reshape(-1).astype(F32)
        rows.append(jnp.pad(flat, (0, (-flat.shape[0]) % (8 * LANES))).reshape(-1, LANES))
    return jnp.concatenate(rows, axis=0)


def _unpack(pack, shapes):
    out, r = [], 0
    for s in shapes:
        size = math.prod(s)
        nr = -(-size // (8 * LANES)) * 8
        out.append(pack[r:r + nr].reshape(-1)[:size].reshape(s))
        r += nr
    return out


def _group_lanes(t):
    lead = t.shape[:-1]
    t = t.reshape(lead + (SSD_GROUPS, SSD_HPG))
    t = jnp.pad(t, [(0, 0)] * len(lead) + [(0, 0), (0, LANES - SSD_HPG)])
    return t.reshape(lead + (SSD_GROUPS * LANES,))


def _ungroup_lanes(t):
    lead = t.shape[:-1]
    return t.reshape(lead + (SSD_GROUPS, LANES))[..., :SSD_HPG].reshape(lead + (SSD_HEADS,))


def kernel(x, meta_tokens, mix_norm_w, w_in, ssd_conv_w, ssd_conv_b, ssd_dt_bias, ssd_A_log, ssd_D, ssd_norm_w, w_branch_ssd, w_branch_ret, w_out, ffn_norm_w, w_up, ffn_conv_w, ffn_conv_b, w_down, final_norm_w, loss_target, m_meta_tokens, m_mix_norm_w, m_w_in, m_ssd_conv_w, m_ssd_conv_b, m_ssd_dt_bias, m_ssd_A_log, m_ssd_D, m_ssd_norm_w, m_w_branch_ssd, m_w_branch_ret, m_w_out, m_ffn_norm_w, m_w_up, m_ffn_conv_w, m_ffn_conv_b, m_w_down, m_final_norm_w, v_meta_tokens, v_mix_norm_w, v_w_in, v_ssd_conv_w, v_ssd_conv_b, v_ssd_dt_bias, v_ssd_A_log, v_ssd_D, v_ssd_norm_w, v_w_branch_ssd, v_w_branch_ret, v_w_out, v_ffn_norm_w, v_w_up, v_ffn_conv_w, v_ffn_conv_b, v_w_down, v_final_norm_w):
    seq, d = x.shape[1], x.shape[2]
    rows = seq + PAD_ROWS
    tm = _row_tile(rows)
    me = _slot(_place())
    d_ff = w_down.shape[1] * N_DEV

    big = [w_in[0], w_branch_ssd[0], w_branch_ret[0], w_out[0], w_up[0], w_down[0]]
    first = _allgather([w_in[0].astype(COMM_DTYPE), meta_tokens, ssd_conv_w[0], ffn_conv_w[0]], "gather_first")
    rest_src = [b.astype(COMM_DTYPE) for b in big[1:]]
    rest_handle, rest_token = _exchange_start(rest_src, False, "gather_rest_start", after=first[0])
    cols = lambda t: jnp.transpose(t, (1, 0, 2)).reshape(t.shape[1], -1)
    rws = lambda t: t.reshape(-1, t.shape[2])
    conv_w, fconv_w = cols(first[2]), cols(first[3])
    meta_full = cols(first[1]) + rest_token[0, 0]
    widths = [SSD_D_INNER, SSD_CONV_DIM, SSD_HEADS, RET_HEADS * RET_QK, RET_HEADS * RET_QK, RET_HEADS * RET_V,
              RET_HEADS * RET_V, d, d]
    offs = [0]
    for wd in widths:
        offs.append(offs[-1] + wd)
    r0, z0 = 0, RET_HEADS * RET_HW
    g0 = z0 + widths[0]
    x0 = g0 + 2 * d
    dt0 = x0 + widths[1]
    in_p = dt0 + SSD_GROUPS * LANES
    pieces = []
    for hd in range(RET_HEADS):
        base = r0 + hd * RET_HW
        pieces += [(offs[3] + hd * RET_QK, RET_QK, base), (offs[4] + hd * RET_QK, RET_QK, base + RET_QK),
                   (offs[5] + hd * RET_V, RET_V, base + 2 * RET_QK), (offs[6] + hd * RET_V, RET_V, base + 2 * RET_QK + RET_V)]
    pieces += [(offs[0], widths[0], z0), (offs[7], d, g0), (offs[8], d, g0 + d), (offs[1], widths[1], x0)]
    pieces += [(offs[2] + SSD_HPG * grp, SSD_HPG, dt0 + LANES * grp) for grp in range(SSD_GROUPS)]
    in_plan = _column_plan(pieces, w_in.shape[2])
    w_in_p = _cols_from_shards(first[0], in_plan, in_p, [(dt0, SSD_GROUPS * LANES)], "w_in_columns")

    h0 = jnp.concatenate([jnp.zeros((FRONT, d), F32), meta_full, x[0]], axis=0)
    u1 = _rms_fwd(h0, mix_norm_w, "rms1")
    in_proj = lambda c0, width, dtype, nm: _mm(u1, w_in_p, mode="nn", out_dtype=dtype, tm=tm, tk=d, name="in_proj_" + nm,
                                               tn=_pick(width, (1024, 512)), b_n0=c0, n_out=width)
    qkvg = in_proj(r0, RET_HEADS * RET_HW, ACT_DTYPE, "qkvg")
    z = in_proj(z0, widths[0], ACT_DTYPE, "z")
    gates = in_proj(g0, 2 * d, ACT_DTYPE, "gates")
    dtr = in_proj(dt0, SSD_GROUPS * LANES, F32, "dt")
    xbc, xbc_c = _xbc_proj_conv(u1, w_in_p, x0, conv_w, ssd_conv_b)
    bias_p, alog_p, dsk_p = _group_lanes(ssd_dt_bias), _group_lanes(ssd_A_log), _group_lanes(ssd_D)
    y_ssd, y_scan, ssd_states = _ssd_fwd(xbc_c, dtr, z, bias_p, alog_p, dsk_p, ssd_norm_w)
    cos, sin, lgam = _rotary_tables(rows)
    y_ret, ret_states = _ret_fwd(qkvg, cos, sin, lgam)
    rest_own, rest = _exchange_wait(rest_handle, False, y_ret, "gather_rest_wait")
    rest = [lax.dynamic_update_index_in_dim(land, own, me, 0) for land, own in zip(rest, rest_own, strict=True)]
    w_bs, w_br, w_o, w_dn = rws(rest[0]), rws(rest[1]), rws(rest[2]), rws(rest[4])
    w_up_f = _cols_from_shards(rest[3], _column_plan([(0, 2 * d_ff, 0)], w_up.shape[2]), 2 * d_ff, [], "w_up_columns")
    bs = _mm(y_ssd, w_bs, mode="nn", out_dtype=ACT_DTYPE, tm=tm, tn=d, tk=SSD_D_INNER, name="branch_ssd")
    br = _mm(y_ret, w_br, mode="nn", out_dtype=ACT_DTYPE, tm=tm, tn=d, tk=RET_HEADS * RET_V, name="branch_ret")
    merged = _merge_fwd(bs, br, gates)
    h1 = _mm(merged, w_o, mode="nn", out_dtype=F32, tm=tm, tn=d, tk=d, name="out_proj", add=h0)
    u2 = _rms_fwd(h1, ffn_norm_w, "rms2")
    up_g, up_v, act = _ffn_up_conv(u2, w_up_f, fconv_w, ffn_conv_b)
    h2 = _mm(act, w_dn, mode="nn", out_dtype=F32, tm=tm, tn=d, tk=d_ff, name="ffn_down", add=h1)
    tgt = jnp.pad(loss_target[0], ((PAD_ROWS, 0), (0, 0)))
    dh2, loss_acc, g_final = _loss_head(h2, tgt, final_norm_w.reshape(1, d))

    tff = _pick(d_ff, (1408, 256))
    tkr = _pick(rows, (1664, 128))
    tkr2 = _pick(rows, (4160, 128))
    rparts = lambda t: t.reshape(N_DEV, -1, t.shape[1])
    d_act = _mm(dh2, w_dn, mode="nt", out_dtype=ACT_DTYPE, tm=tm, tn=tff, tk=d, name="d_act")
    g_w_dn = _mm(act, dh2, mode="tn", out_dtype=COMM_DTYPE, tm=tff, tn=d, tk=tkr, name="g_w_down")
    c_dn = [rparts(g_w_dn)]
    h_dn, t_dn = _exchange_start(c_dn, True, "scatter_down_start")
    d_up_g, d_up_v, g_fcw_g, g_fcb_g, g_fcw_v, g_fcb_v, g_w_up_g, g_w_up_v = _ffn_conv_bwd(
        up_g, up_v, d_act, fconv_w, ffn_conv_b + t_dn[0, 0], u2)
    g_fconv_w = jnp.concatenate([g_fcw_g, g_fcw_v], axis=1)
    g_fconv_b = jnp.concatenate([g_fcb_g, g_fcb_v], axis=1)
    c_up = [_shards_from_cols([g_w_up_g, g_w_up_v], [_column_plan([(0, d_ff, 0)], w_up.shape[2]),
                                                     _column_plan([(d_ff, d_ff, 0)], w_up.shape[2])], w_up.shape[2], "g_w_up_shards")]
    h_up, t_up = _exchange_start(c_up, True, "scatter_up_start")
    du2 = _mm(d_up_g, w_up_f, mode="nt", out_dtype=F32, tm=tm, tn=d, tk=d_ff, name="d_u2_gate", after=t_up)
    du2 = _mm(d_up_v, w_up_f, mode="nt", out_dtype=F32, tm=tm, tn=d, tk=d_ff, name="d_u2_value", add=du2, b_k0=d_ff)
    dh1, g_ffn_norm = _rms_bwd(du2, h1, ffn_norm_w, dh2, "rms2_bwd")
    g_w_o =_mm(merged, dh1, mode="tn", out_dtype=COMM_DTYPE, tm=d, tn=d, tk=tkr, name="g_w_out")
    dproj = lax.empty((rows, in_p), ACT_DTYPE)
    d_bs, d_br, dproj = _merge_bwd(dh1, w_o, bs, br, gates, dproj, g0)
    g_w_bs = _mm(y_ssd, d_bs, mode="tn", out_dtype=COMM_DTYPE, tm=1024, tn=d, tk=tkr2, name="g_w_branch_ssd")
    g_w_br = _mm(y_ret, d_br, mode="tn", out_dtype=COMM_DTYPE, tm=1024, tn=d, tk=tkr2, name="g_w_branch_ret")
    c_mid = [rparts(g_w_bs), rparts(g_w_br), rparts(g_w_o)]
    h_mid, t_mid = _exchange_start(c_mid, True, "scatter_mid_start")
    d_yscan, dproj, g_nw = _ssd_gate_bwd(y_scan, z, d_bs, w_bs, ssd_norm_w + t_mid[0, 0], dproj, z0)
    dxs, d_bm, d_cm, dproj, g_bias_p, g_alog_p, g_dsk_p = _ssd_bwd(
        xbc_c, dtr, bias_p, alog_p, dsk_p, ssd_states, d_yscan, dproj, dt0)
    dproj, g_conv_w, g_conv_b = _ssd_conv_bwd(xbc, dxs, d_bm, d_cm, conv_w, ssd_conv_b, dproj, x0)
    dproj = _ret_bwd(qkvg, cos, sin, lgam, ret_states, d_br, w_br, dproj)
    g_w_in_p = _mm(u1, dproj, mode="tn", out_dtype=F32, tm=d, tn=_pick(in_p, (768, 512)), tk=tkr2, name="g_w_in")
    c_in = [_shards_from_cols([g_w_in_p], [in_plan], w_in.shape[2], "g_w_in_shards")]
    h_in, t_in = _exchange_start(c_in, True, "scatter_in_start")
    du1 = _mm(dproj, w_in_p, mode="nt", out_dtype=F32, tm=tm, tn=d, tk=_pick(in_p, (4608, 512)), name="d_u1", after=t_in)
    dh0, g_mix_norm = _rms_bwd(du1, h0, mix_norm_w, dh1, "rms1_bwd")
    grad_x = dh0[PAD_ROWS:][None]

    landed = {}
    for key, handle, names in (("in", h_in, ["w_in"]), ("mid", h_mid, ["w_branch_ssd", "w_branch_ret", "w_out"]),
                               ("up", h_up, ["w_up"]), ("down", h_dn, ["w_down"])):
        srcs, lands = _exchange_wait(handle, True, dh0, f"scatter_{key}_wait")
        for nm, land, src in zip(names, lands, srcs, strict=True):
            landed[nm] = (lax.dynamic_index_in_dim(src, me, 0, keepdims=False), land)
    big_m = [m_w_in, m_w_branch_ssd, m_w_branch_ret, m_w_out, m_w_up, m_w_down]
    big_v = [v_w_in, v_w_branch_ssd, v_w_branch_ret, v_w_out, v_w_up, v_w_down]
    big_names = ["w_in", "w_branch_ssd", "w_branch_ret", "w_out", "w_up", "w_down"]
    big_out = {}
    for nm, w, m, v_ in zip(big_names, big, big_m, big_v, strict=True):
        big_out[nm] = [t[None] for t in _adam_shard(*landed[nm], w, m[0], v_[0], "adam_" + nm)]

    small_g = [dh0[FRONT:PAD_ROWS], g_mix_norm, g_conv_w, g_conv_b, _ungroup_lanes(g_bias_p), _ungroup_lanes(g_alog_p),
               _ungroup_lanes(g_dsk_p), g_nw, g_ffn_norm, g_fconv_w, g_fconv_b, g_final, loss_acc[0:1, 0:1]]
    total = _unpack(_allreduce_small(_pack(small_g)), [t.shape for t in small_g])
    loss = total[12].reshape(())
    shard = lambda t, width: lax.dynamic_slice_in_dim(t, me * width, width, axis=1)
    small_names = ["meta_tokens", "mix_norm_w", "ssd_conv_w", "ssd_conv_b", "ssd_dt_bias", "ssd_A_log", "ssd_D", "ssd_norm_w",
                   "ffn_norm_w", "ffn_conv_w", "ffn_conv_b", "final_norm_w"]
    small_w = [meta_tokens, mix_norm_w, ssd_conv_w, ssd_conv_b, ssd_dt_bias, ssd_A_log, ssd_D, ssd_norm_w, ffn_norm_w,
               ffn_conv_w, ffn_conv_b, final_norm_w]
    small_m = [m_meta_tokens, m_mix_norm_w, m_ssd_conv_w, m_ssd_conv_b, m_ssd_dt_bias, m_ssd_A_log, m_ssd_D, m_ssd_norm_w,
               m_ffn_norm_w, m_ffn_conv_w, m_ffn_conv_b, m_final_norm_w]
    small_v = [v_meta_tokens, v_mix_norm_w, v_ssd_conv_w, v_ssd_conv_b, v_ssd_dt_bias, v_ssd_A_log, v_ssd_D, v_ssd_norm_w,
               v_ffn_norm_w, v_ffn_conv_w, v_ffn_conv_b, v_final_norm_w]
    grads = total[:12]
    grads[0] = shard(grads[0], meta_tokens.shape[1])
    grads[2] = shard(grads[2], ssd_conv_w.shape[2])
    grads[9] = shard(grads[9], ffn_conv_w.shape[2])
    grads = [t.reshape(w.shape) for t, w in zip(grads, small_w, strict=True)]
    shapes = [w.shape for w in small_w]
    upd = _adam_small(_pack(small_w), _pack(grads), _pack(small_m), _pack(small_v))
    small_out = {nm: [gr_] + [u[i] for u in (_unpack(t, shapes) for t in upd)]
                 for i, (nm, gr_) in enumerate(zip(small_names, grads, strict=True))}

    order = ["meta_tokens", "mix_norm_w", "w_in", "ssd_conv_w", "ssd_conv_b", "ssd_dt_bias", "ssd_A_log", "ssd_D", "ssd_norm_w",
             "w_branch_ssd", "w_branch_ret", "w_out", "ffn_norm_w", "w_up", "ffn_conv_w", "ffn_conv_b", "w_down", "final_norm_w"]
    res = {**big_out, **small_out}
    return (loss, grad_x, *[res[nm][0] for nm in order], *[res[nm][1] for nm in order], *[res[nm][2] for nm in order],
            *[res[nm][3] for nm in order])
```

```python
import functools
import math

import jax
import jax.numpy as jnp
import numpy as np
from jax import lax
from jax.experimental import pallas as pl
from jax.experimental.pallas import tpu as pltpu

F32 = jnp.float32
MXU_DTYPE = jnp.bfloat16
ACT_DTYPE = jnp.bfloat16
COMM_DTYPE = jnp.bfloat16

N_META = 16
CHUNK = 128
FRONT = CHUNK - N_META
PAD_ROWS = FRONT + N_META
EPS = 1e-6
N_DEV = 8

SSD_D_INNER = 2048
SSD_HEAD_DIM = 64
SSD_HEADS = 32
SSD_GROUPS = 4
SSD_HPG = SSD_HEADS // SSD_GROUPS
SSD_STATE = 128
SSD_CONV = 4
SSD_CONV_DIM = SSD_D_INNER + 2 * SSD_GROUPS * SSD_STATE
SSD_GW = SSD_D_INNER // SSD_GROUPS
RET_HEADS = 4
RET_QK = 256
RET_V = 512
RET_HW = 2 * RET_QK + 2 * RET_V
ROPE_BASE = 10000.0
FFN_CONV = 3
HALO = 16
LANES = 128

ADAM_LR = 0.001
ADAM_B1 = 0.9
ADAM_B2 = 0.999
ADAM_EPS = 1e-08
ADAM_WD = 0.01
ADAM_STEP = 10

VMEM_LIMIT = 56 * 1024 * 1024
MESH = pl.DeviceIdType.MESH

NN = (((1,), (0,)), ((), ()))
NT = (((1,), (1,)), ((), ()))
TN = (((0,), (0,)), ((), ()))


def _params(sem):
    return pltpu.CompilerParams(dimension_semantics=sem, vmem_limit_bytes=VMEM_LIMIT)


def _mxu(a, b, dn):
    return lax.dot_general(a.astype(MXU_DTYPE), b.astype(MXU_DTYPE), dn, preferred_element_type=F32)


@functools.partial(jax.custom_vjp, nondiff_argnums=(2,))
def _dot(a, b, dn=NN):
    return _mxu(a, b, dn)


def _dot_fwd(a, b, dn):
    return _mxu(a, b, dn), (a, b)


def _dot_bwd(dn, res, g):
    a, b = res
    if dn == NN:
        return _mxu(g, b, NT), _mxu(a, g, TN)
    if dn == NT:
        return _mxu(g, b, NN), _mxu(g, a, TN)
    assert dn == TN
    return _mxu(b, g, NT), _mxu(a, g, NN)


_dot.defvjp(_dot_fwd, _dot_bwd)


def _silu(x):
    return x * jax.nn.sigmoid(x)


def _dsilu(x):
    s = jax.nn.sigmoid(x)
    return s * (1.0 + x * (1.0 - s))


def _row_tile(rows):
    return 640 if rows % 640 == 0 else 128


def _mm(a, b, *, mode, out_dtype, tm, tn, tk, name, add=None, after=None, b_k0=0, b_n0=0, n_out=None):
    if mode == "nt":
        (m, k), n = a.shape, b.shape[0]
        k2 = k if b_k0 % tk == 0 and b_k0 + k <= b.shape[1] else None
    else:
        (m, k) = a.shape if mode == "nn" else a.shape[::-1]
        k2 = b.shape[0]
        n = b.shape[1] if n_out is None else n_out
        assert b_n0 % tn == 0 and b_n0 + n <= b.shape[1]
    assert (b_k0 == 0 or mode == "nt") and ((b_n0 == 0 and n_out is None) or mode != "nt")
    assert k == k2 and m % tm == 0 and n % tn == 0 and k % tk == 0, (name, a.shape, b.shape, tm, tn, tk)
    kb0, nb0 = b_k0 // tk, b_n0 // tn
    nk = k // tk
    dn = {"nn": NN, "nt": NT, "tn": TN}[mode]
    has_add = add is not None
    n_in = 2 + has_add + (after is not None)

    def body(*refs):
        a_ref, b_ref = refs[0], refs[1]
        add_ref = refs[2] if has_add else None
        o_ref = refs[n_in]
        p = _dot(a_ref[...], b_ref[...], dn)
        if nk == 1:
            if has_add:
                p = p + add_ref[...]
            o_ref[...] = p.astype(out_dtype)
        else:
            acc_ref = refs[n_in + 1]
            kk = pl.program_id(2)

            @pl.when(kk == 0)
            def _():
                acc_ref[...] = p

            @pl.when(kk > 0)
            def _():
                acc_ref[...] += p

            @pl.when(kk == nk - 1)
            def _():
                r = acc_ref[...]
                if has_add:
                    r = r + add_ref[...]
                o_ref[...] = r.astype(out_dtype)

    if mode == "tn":
        a_spec = pl.BlockSpec((tk, tm), lambda j, i, kk: (kk, i))
    else:
        a_spec = pl.BlockSpec((tm, tk), lambda j, i, kk: (i, kk))
    if mode == "nt":
        b_spec = pl.BlockSpec((tn, tk), lambda j, i, kk: (j, kk + kb0))
    else:
        b_spec = pl.BlockSpec((tk, tn), lambda j, i, kk: (kk, j + nb0))
    o_spec = pl.BlockSpec((tm, tn), lambda j, i, kk: (i, j))
    in_specs = [a_spec, b_spec] + ([o_spec] if has_add else [])
    args = (a, b) + ((add,) if has_add else ())
    if after is not None:
        in_specs.append(pl.BlockSpec(memory_space=pl.ANY))
        args += (after,)
    return pl.pallas_call(
        body, name=name, grid=(n // tn, m // tm, nk), in_specs=in_specs, out_specs=o_spec,
        out_shape=jax.ShapeDtypeStruct((m, n), out_dtype),
        scratch_shapes=[pltpu.VMEM((tm, tn), F32)] if nk > 1 else [],
        compiler_params=_params(("parallel", "parallel", "arbitrary")),
    )(*args)


def _pick(n, cands):
    for c in cands:
        if n % c == 0:
            return c
    return n


def _rms_fwd(h, w, name):
    rows, d = h.shape
    tm = _row_tile(rows)

    def body(h_ref, w_ref, u_ref):
        x = h_ref[...]
        r = lax.rsqrt(jnp.mean(x * x, axis=-1, keepdims=True) + EPS)
        u_ref[...] = (x * r * w_ref[...]).astype(ACT_DTYPE)

    return pl.pallas_call(
        body, name=name, grid=(rows // tm,),
        in_specs=[pl.BlockSpec((tm, d), lambda i: (i, 0)), pl.BlockSpec((1, d), lambda i: (0, 0))],
        out_specs=pl.BlockSpec((tm, d), lambda i: (i, 0)),
        out_shape=jax.ShapeDtypeStruct((rows, d), ACT_DTYPE),
        compiler_params=_params(("parallel",)),
    )(h, w)


def _rms_bwd(du, h, w, dres, name):
    rows, d = h.shape
    tm = _row_tile(rows)

    def body(du_ref, h_ref, w_ref, dres_ref, dh_ref, dw_ref):
        x = h_ref[...]
        dy = du_ref[...].astype(F32)
        r = lax.rsqrt(jnp.mean(x * x, axis=-1, keepdims=True) + EPS)
        xhat = x * r
        dxn = dy * w_ref[...]
        dx = r * (dxn - xhat * jnp.mean(dxn * xhat, axis=-1, keepdims=True))
        dh_ref[...] = dres_ref[...] + dx

        @pl.when(pl.program_id(0) == 0)
        def _():
            dw_ref[...] = jnp.zeros_like(dw_ref)

        dw_ref[...] += jnp.sum(dy * xhat, axis=0, keepdims=True)

    return pl.pallas_call(
        body, name=name, grid=(rows // tm,),
        in_specs=[pl.BlockSpec((tm, d), lambda i: (i, 0)), pl.BlockSpec((tm, d), lambda i: (i, 0)),
                  pl.BlockSpec((1, d), lambda i: (0, 0)), pl.BlockSpec((tm, d), lambda i: (i, 0))],
        out_specs=[pl.BlockSpec((tm, d), lambda i: (i, 0)), pl.BlockSpec((1, d), lambda i: (0, 0))],
        out_shape=[jax.ShapeDtypeStruct((rows, d), F32), jax.ShapeDtypeStruct((1, d), F32)],
        compiler_params=_params(("arbitrary",)),
    )(du, h, w, dres)


def _loss_head(h2, tgt, w):
    rows, d = h2.shape
    tm = _row_tile(rows)

    def body(h_ref, t_ref, w_ref, dh_ref, loss_ref, dw_ref):
        i = pl.program_id(0)
        x = h_ref[...]
        r = lax.rsqrt(jnp.mean(x * x, axis=-1, keepdims=True) + EPS)
        xhat = x * r
        wv = w_ref[...]
        row = i * tm + lax.broadcasted_iota(jnp.int32, (tm, 1), 0)
        live = row >= PAD_ROWS
        diff = jnp.where(live, xhat * wv - t_ref[...], 0.0)
        dy = diff * (1.0 / d)
        dxn = dy * wv
        dh_ref[...] = r * (dxn - xhat * jnp.mean(dxn * xhat, axis=-1, keepdims=True))

        @pl.when(i == 0)
        def _():
            loss_ref[...] = jnp.zeros_like(loss_ref)
            dw_ref[...] = jnp.zeros_like(dw_ref)

        loss_ref[...] += 0.5 * jnp.sum(jnp.mean(diff * diff, axis=-1, keepdims=True))
        dw_ref[...] += jnp.sum(dy * xhat, axis=0, keepdims=True)

    return pl.pallas_call(
        body, name="loss_head", grid=(rows // tm,),
        in_specs=[pl.BlockSpec((tm, d), lambda i: (i, 0)), pl.BlockSpec((tm, d), lambda i: (i, 0)),
                  pl.BlockSpec((1, d), lambda i: (0, 0))],
        out_specs=[pl.BlockSpec((tm, d), lambda i: (i, 0)), pl.BlockSpec((8, LANES), lambda i: (0, 0)),
                   pl.BlockSpec((1, d), lambda i: (0, 0))],
        out_shape=[jax.ShapeDtypeStruct((rows, d), F32), jax.ShapeDtypeStruct((8, LANES), F32),
                   jax.ShapeDtypeStruct((1, d), F32)],
        compiler_params=_params(("arbitrary",)),
    )(h2, tgt, w)


def _prev_halo_spec(tm, width, col):
    return pl.BlockSpec((HALO, width), lambda j, i: (jnp.maximum(i * (tm // HALO) - 1, 0), col(j)))


def _next_halo_spec(tm, rows, width, col):
    last = rows // HALO - 1
    return pl.BlockSpec((HALO, width), lambda j, i: (jnp.minimum((i + 1) * (tm // HALO), last), col(j)))


def _conv_taps(cat, w_ref, b_ref, kw):
    acc = b_ref[...] + w_ref[kw - 1:kw, :] * cat
    for s in range(1, kw):
        acc = acc + w_ref[kw - 1 - s:kw - s, :] * pltpu.roll(cat, s, 0)
    return acc


def _conv_back(dpre, w_ref, kw):
    n = dpre.shape[0]
    acc = w_ref[kw - 1:kw, :] * dpre
    for s in range(1, kw):
        acc = acc + w_ref[kw - 1 - s:kw - s, :] * pltpu.roll(dpre, n - s, 0)
    return acc


def _xbc_proj_conv(u1, w_in_p, col0, w, b):
    rows, d = u1.shape
    width = w.shape[1]
    tm, tc = _row_tile(rows), 512
    cb0 = col0 // tc
    assert col0 % tc == 0

    def body(u_ref, m_ref, w_ref, b_ref, x_ref, o_ref, carry):
        i = pl.program_id(1)

        @pl.when(i == 0)
        def _():
            carry[...] = jnp.zeros_like(carry)

        xb = _mxu(u_ref[...], m_ref[...], NN).astype(ACT_DTYPE)
        x_ref[...] = xb
        x = xb.astype(F32)
        cat = jnp.concatenate([carry[...], x], axis=0)
        carry[...] = x[tm - HALO:, :]
        pre = _conv_taps(cat, w_ref, b_ref, SSD_CONV)[HALO:]
        row = i * tm + lax.broadcasted_iota(jnp.int32, (tm, 1), 0)
        o_ref[...] = jnp.where(row >= FRONT, _silu(pre), 0.0).astype(ACT_DTYPE)

    main = pl.BlockSpec((tm, tc), lambda j, i: (i, j))
    par = lambda r: pl.BlockSpec((r, tc), lambda j, i: (0, j))
    act = jax.ShapeDtypeStruct((rows, width), ACT_DTYPE)
    return pl.pallas_call(
        body, name="xbc_proj_conv", grid=(width // tc, rows // tm),
        in_specs=[pl.BlockSpec((tm, d), lambda j, i: (i, 0)), pl.BlockSpec((d, tc), lambda j, i: (0, cb0 + j)),
                  par(SSD_CONV), par(1)],
        out_specs=[main, main], out_shape=[act, act], scratch_shapes=[pltpu.VMEM((HALO, tc), F32)],
        compiler_params=_params(("parallel", "arbitrary")),
    )(u1, w_in_p, w, b)


def _ssd_conv_bwd(xbc, dxs, dbm, dcm, w, b, dproj, col0):
    rows, width = xbc.shape
    tm, tc = _row_tile(rows), 512
    kw = SSD_CONV
    nx = dxs.shape[1] // tc
    assert dbm.shape[1] == tc and dcm.shape[1] == tc and width == (nx + 2) * tc and col0 % tc == 0

    def body(x_ref, xp_ref, xn_ref, d0_ref, d0n_ref, d1_ref, d1n_ref, d2_ref, d2n_ref, w_ref, b_ref, dp_ref,
             dx_ref, dw_ref, db_ref):
        j, i = pl.program_id(0), pl.program_id(1)
        xp = jnp.where(i == 0, 0.0, xp_ref[...].astype(F32))
        cat = jnp.concatenate([xp, x_ref[...].astype(F32), xn_ref[...].astype(F32)], axis=0)
        sh = [cat] + [pltpu.roll(cat, s, 0) for s in range(1, kw)]
        pre = b_ref[...] + w_ref[kw - 1:kw, :] * sh[0]
        for s in range(1, kw):
            pre = pre + w_ref[kw - 1 - s:kw - s, :] * sh[s]
        pre = pre[HALO:]
        row = i * tm + lax.broadcasted_iota(jnp.int32, (tm + HALO, 1), 0)
        live = (row >= FRONT) & (row < rows)
        pick = lambda a, bb, c: jnp.where(j < nx, a[...], jnp.where(j == nx, bb[...], c[...])).astype(F32)
        dout = jnp.concatenate([pick(d0_ref, d1_ref, d2_ref), pick(d0n_ref, d1n_ref, d2n_ref)], axis=0)
        dpre = jnp.where(live, dout * _dsilu(pre), 0.0)
        dx_ref[...] = _conv_back(dpre, w_ref, kw)[:tm].astype(ACT_DTYPE)

        @pl.when(i == 0)
        def _():
            dw_ref[...] = jnp.zeros_like(dw_ref)
            db_ref[...] = jnp.zeros_like(db_ref)

        dmain = dpre[:tm]
        db_ref[...] += jnp.sum(dmain, axis=0, keepdims=True)
        for k in range(kw):
            dw_ref[k:k + 1, :] += jnp.sum(dmain * sh[kw - 1 - k][HALO:HALO + tm], axis=0, keepdims=True)

    main = pl.BlockSpec((tm, tc), lambda j, i: (i, j))
    par = lambda r: pl.BlockSpec((r, tc), lambda j, i: (0, j))
    col = lambda j: j
    xcol, zero = (lambda j: jnp.minimum(j, nx - 1)), (lambda j: 0)
    dspecs = lambda c: [pl.BlockSpec((tm, tc), lambda j, i: (i, c(j))), _next_halo_spec(tm, rows, tc, c)]
    return pl.pallas_call(
        body, name="ssd_conv_bwd", grid=(width // tc, rows // tm),
        in_specs=[main, _prev_halo_spec(tm, tc, col), _next_halo_spec(tm, rows, tc, col)]
        + dspecs(xcol) + dspecs(zero) + dspecs(zero) + [par(kw), par(1), pl.BlockSpec(memory_space=pl.ANY)],
        out_specs=[pl.BlockSpec((tm, tc), lambda j, i: (i, col0 // tc + j)), par(kw), par(1)],
        out_shape=[jax.ShapeDtypeStruct(dproj.shape, dproj.dtype), jax.ShapeDtypeStruct((kw, width), F32),
                   jax.ShapeDtypeStruct((1, width), F32)],
        input_output_aliases={11: 0}, compiler_params=_params(("parallel", "arbitrary")),
    )(xbc, xbc, xbc, dxs, dxs, dbm, dbm, dcm, dcm, w, b, dproj)


def _ffn_up_conv(u2, w_up, w, b):
    rows, d = u2.shape
    width = w_up.shape[1]
    dff = width // 2
    tm, tc = _row_tile(rows), _pick(dff, (256, 128))
    nb = dff // tc
    kw = FFN_CONV

    def body(u_ref, mg_ref, mv_ref, wg_ref, bg_ref, wv_ref, bv_ref, ug_ref, uv_ref, o_ref, cg, cv):
        i = pl.program_id(1)

        @pl.when(i == 0)
        def _():
            cg[...] = jnp.zeros_like(cg)
            cv[...] = jnp.zeros_like(cv)

        def pre(m_ref, up_ref, carry, w_ref, b_ref):
            upb = _mxu(u_ref[...], m_ref[...], NN).astype(ACT_DTYPE)
            up_ref[...] = upb
            x = upb.astype(F32)
            cat = jnp.concatenate([carry[...], x], axis=0)
            carry[...] = x[tm - HALO:, :]
            return _conv_taps(cat, w_ref, b_ref, kw)[HALO:]

        ag = pre(mg_ref, ug_ref, cg, wg_ref, bg_ref)
        av = pre(mv_ref, uv_ref, cv, wv_ref, bv_ref)
        o_ref[...] = (_silu(ag) * av).astype(ACT_DTYPE)

    gcol, vcol = (lambda j: j), (lambda j: j + nb)
    mat = lambda col: pl.BlockSpec((d, tc), lambda j, i: (0, col(j)))
    par = lambda r, col: pl.BlockSpec((r, tc), lambda j, i: (0, col(j)))
    out = pl.BlockSpec((tm, tc), lambda j, i: (i, j))
    act = jax.ShapeDtypeStruct((rows, dff), ACT_DTYPE)
    return pl.pallas_call(
        body, name="ffn_up_conv", grid=(nb, rows // tm),
        in_specs=[pl.BlockSpec((tm, d), lambda j, i: (i, 0)), mat(gcol), mat(vcol),
                  par(kw, gcol), par(1, gcol), par(kw, vcol), par(1, vcol)],
        out_specs=[out, out, out], out_shape=[act, act, act],
        scratch_shapes=[pltpu.VMEM((HALO, tc), F32), pltpu.VMEM((HALO, tc), F32)],
        compiler_params=_params(("parallel", "arbitrary")),
    )(u2, w_up, w_up, w, b, w, b)


def _ffn_conv_bwd(up_g, up_v, dact, w, b, u2):
    rows, dff = up_g.shape
    d = u2.shape[1]
    tm, tc = _row_tile(rows), _pick(dff, (256, 128))
    nb = dff // tc
    kw = FFN_CONV

    sb = 16

    def body(g_ref, gp_ref, gn_ref, v_ref, vp_ref, vn_ref, d_ref, dn_ref, wg_ref, bg_ref, wv_ref, bv_ref, u_ref,
             dxg_ref, dxv_ref, dwg_ref, dbg_ref, dwv_ref, dbv_ref, gwg_ref, gwv_ref, xg_s, xv_s, dd_s, og_s, ov_s):
        i = pl.program_id(1)
        last = i == rows // tm - 1
        for x_s, x_ref, xp_ref, xn_ref in ((xg_s, g_ref, gp_ref, gn_ref), (xv_s, v_ref, vp_ref, vn_ref)):
            x_s[0:HALO, :] = jnp.where(i == 0, 0.0, xp_ref[...].astype(F32))
            x_s[HALO:HALO + tm, :] = x_ref[...].astype(F32)
            x_s[HALO + tm:, :] = xn_ref[...].astype(F32)
        dd_s[0:tm, :] = d_ref[...].astype(F32)
        dd_s[tm:, :] = jnp.where(last, 0.0, dn_ref[...].astype(F32))

        wg = [wg_ref[k:k + 1, :] for k in range(kw)]
        wv = [wv_ref[k:k + 1, :] for k in range(kw)]
        bg, bv = bg_ref[...], bv_ref[...]

        def taps(x_s, e0, w, bias):
            win = x_s[pl.ds(e0 + HALO - sb, 2 * sb), :]
            sh = [win[sb:], pltpu.roll(win, 1, 0)[sb:], pltpu.roll(win, 2, 0)[sb:]]
            return bias + w[2] * sh[0] + w[1] * sh[1] + w[0] * sh[2], sh

        def dpre_of(e0):
            ag, sh_g = taps(xg_s, e0, wg, bg)
            av, sh_v = taps(xv_s, e0, wv, bv)
            dout = dd_s[pl.ds(e0, sb), :]
            s = jax.nn.sigmoid(ag)
            silu = ag * s
            return dout * av * (s + silu * (1.0 - s)), dout * silu, sh_g, sh_v

        def back(dp, nxt, w):
            cat = jnp.concatenate([dp, nxt], axis=0)
            return w[2] * dp + w[1] * pltpu.roll(cat, 2 * sb - 1, 0)[:sb] + w[0] * pltpu.roll(cat, 2 * sb - 2, 0)[:sb]

        nxt_g, nxt_v, _, _ = dpre_of(tm)
        acc_g = acc_v = tuple(jnp.zeros((sb, tc), F32) for _ in range(kw + 1))
        for e0 in range(tm - sb, -1, -sb):
            dpg, dpv, sh_g, sh_v = dpre_of(e0)
            og_s[e0:e0 + sb, :] = back(dpg, nxt_g, wg)
            ov_s[e0:e0 + sb, :] = back(dpv, nxt_v, wv)
            acc_g = tuple(a + dpg * t for a, t in zip(acc_g, (sh_g[2], sh_g[1], sh_g[0], 1.0)))
            acc_v = tuple(a + dpv * t for a, t in zip(acc_v, (sh_v[2], sh_v[1], sh_v[0], 1.0)))
            nxt_g, nxt_v = dpg, dpv

        @pl.when(i == 0)
        def _():
            for r in (dwg_ref, dbg_ref, dwv_ref, dbv_ref, gwg_ref, gwv_ref):
                r[...] = jnp.zeros_like(r)

        for acc, o_s, dx_ref, dw_ref, db_ref, gw_ref in ((acc_g, og_s, dxg_ref, dwg_ref, dbg_ref, gwg_ref),
                                                        (acc_v, ov_s, dxv_ref, dwv_ref, dbv_ref, gwv_ref)):
            dx = o_s[...].astype(ACT_DTYPE)
            dx_ref[...] = dx
            gw_ref[...] += _mxu(u_ref[...], dx, TN)
            for k in range(kw):
                dw_ref[k:k + 1, :] += jnp.sum(acc[k], axis=0, keepdims=True)
            db_ref[...] += jnp.sum(acc[kw], axis=0, keepdims=True)

    gcol, vcol = (lambda j: j), (lambda j: j + nb)
    main = lambda col: pl.BlockSpec((tm, tc), lambda j, i: (i, col(j)))
    par = lambda r, col: pl.BlockSpec((r, tc), lambda j, i: (0, col(j)))
    halos = lambda col: [_prev_halo_spec(tm, tc, col), _next_halo_spec(tm, rows, tc, col)]
    act_shape = jax.ShapeDtypeStruct((rows, dff), ACT_DTYPE)
    par_shapes = [jax.ShapeDtypeStruct((kw, dff), F32), jax.ShapeDtypeStruct((1, dff), F32)]
    gw_shape = jax.ShapeDtypeStruct((d, dff), F32)
    return pl.pallas_call(
        body, name="ffn_conv_bwd", grid=(nb, rows // tm),
        in_specs=[main(gcol)] + halos(gcol) + [main(gcol)] + halos(gcol) + [main(gcol), _next_halo_spec(tm, rows, tc, gcol),
                  par(kw, gcol), par(1, gcol), par(kw, vcol), par(1, vcol), pl.BlockSpec((tm, d), lambda j, i: (i, 0))],
        out_specs=[main(gcol), main(gcol), par(kw, gcol), par(1, gcol), par(kw, gcol), par(1, gcol), par(d, gcol), par(d, gcol)],
        out_shape=[act_shape, act_shape] + par_shapes + par_shapes + [gw_shape, gw_shape],
        scratch_shapes=[pltpu.VMEM((tm + 2 * HALO, tc), F32), pltpu.VMEM((tm + 2 * HALO, tc), F32),
                        pltpu.VMEM((tm + HALO, tc), F32), pltpu.VMEM((tm, tc), F32), pltpu.VMEM((tm, tc), F32)],
        compiler_params=_params(("parallel", "arbitrary")),
    )(up_g, up_g, up_g, up_v, up_v, up_v, dact, dact, w, b, w, b, u2)


def _ssd_scalars(dtr, dt_bias, a_log, live):
    q = CHUNK
    pre = dtr + dt_bias
    dt = jnp.where(live, jax.nn.softplus(pre), 0.0)
    a_neg = -jnp.exp(a_log)
    li = lax.broadcasted_iota(jnp.int32, (q, q), 0)
    si = lax.broadcasted_iota(jnp.int32, (q, q), 1)
    causal = li >= si
    tri = jnp.where(causal, 1.0, 0.0).astype(F32)
    a_cs = sum(_mxu(tri, p, NN) for p in _split(dt * a_neg, 3))
    return pre, dt, a_neg, a_cs, causal, tri


def _head_select():
    r = lax.broadcasted_iota(jnp.int32, (LANES, SSD_GW), 0)
    c = lax.broadcasted_iota(jnp.int32, (LANES, SSD_GW), 1)
    return jnp.where(c // SSD_HEAD_DIM == r, 1.0, 0.0).astype(MXU_DTYPE)


def _split(t, parts):
    out, rem = [], t
    for _ in range(parts):
        p = rem.astype(MXU_DTYPE)
        out.append(p)
        rem = rem - p.astype(F32)
    return out


def _stacked(ts, parts, sel, dn):
    out = _mxu(jnp.concatenate([p for t in ts for p in _split(t, parts)], axis=0), sel, dn)
    res, r0 = [], 0
    for t in ts:
        r = t.shape[0]
        res.append(sum(out[r0 + k * r:r0 + (k + 1) * r] for k in range(parts)))
        r0 += parts * r
    return res


def _head_cols(ts, sel):
    return _stacked(ts, 2, sel, NN)


def _head_sums(ts, sel):
    return _stacked(ts, 3, sel, NT)


def _half_masks():
    lane = lax.broadcasted_iota(jnp.int32, (CHUNK, LANES), 1)
    return lane < SSD_HEAD_DIM, lane >= SSD_HEAD_DIM


def _ssd_scan(xs, bm, cm, dtr, prev, dt_bias, a_log, d_skip, live):
    q = CHUNK
    sel = _head_select()
    _, dt, _, a_cs, causal, _ = _ssd_scalars(dtr, dt_bias, a_log, live)
    a_cs_t = a_cs.T
    a_end = a_cs[q - 1:q, :]
    dt_x, e_x, w_x, d_x = _head_cols([dt, jnp.exp(a_cs), jnp.exp(a_end - a_cs), jnp.broadcast_to(d_skip, (16, LANES))], sel)
    xdt = xs * dt_x
    cb = _dot(cm, bm, NT)
    y = _dot(cm, prev) * e_x + d_x[0:1] * xs
    new = prev * e_x[q - 1:q, :] + _dot(bm, xdt * w_x, TN)
    masks = _half_masks()
    ys = []
    for pp in range(SSD_HPG // 2):
        xpair = xdt[:, pp * LANES:(pp + 1) * LANES]
        acc = jnp.zeros((q, LANES), F32)
        for half in range(2):
            hh = 2 * pp + half
            decay = jnp.exp(jnp.where(causal, a_cs[:, hh:hh + 1] - a_cs_t[hh:hh + 1, :], -jnp.inf))
            acc = acc + _dot(cb * decay, jnp.where(masks[half], xpair, 0.0))
        ys.append(acc)
    return y + jnp.concatenate(ys, axis=1), new


def _ssd_gate(y, z, nw):
    yz = y * _silu(z)
    return yz * lax.rsqrt(jnp.mean(yz * yz, axis=-1, keepdims=True) + EPS) * nw


def _ssd_scan_bwd(xs, bm, cm, dtr, prev, dt_bias, a_log, d_skip, live, dy, dnew):
    q = CHUNK
    sel = _head_select()
    pre, dt, a_neg, a_cs, causal, tri = _ssd_scalars(dtr, dt_bias, a_log, live)
    a_cs_t = a_cs.T
    a_end = a_cs[q - 1:q, :]
    dt_x, e_x, w_x, d_x = _head_cols([dt, jnp.exp(a_cs), jnp.exp(a_end - a_cs), jnp.broadcast_to(d_skip, (16, LANES))], sel)
    g_x, d_x = e_x[q - 1:q, :], d_x[0:1]
    xdt = xs * dt_x
    u = xdt * w_x
    cb = _mxu(cm, bm, NT)
    cs = _mxu(cm, prev, NN)
    dye = dy * e_x
    dcm = _mxu(dye, prev, NT)
    dprev = _mxu(cm, dye, TN) + dnew * g_x
    dacs_x = dye * cs
    dbm = _mxu(u, dnew, NT)
    du = _mxu(bm, dnew, NN)
    dw_x = du * u
    dacs_x = dacs_x - dw_x
    dend_x = jnp.sum(dw_x + dnew * prev * g_x, axis=0, keepdims=True)
    dxdt = du * w_x
    lane = lax.broadcasted_iota(jnp.int32, (q, LANES), 1)
    sub = lax.broadcasted_iota(jnp.int32, (q, LANES), 0)
    dcb = jnp.zeros((q, q), F32)
    dacs = jnp.zeros((q, LANES), F32)
    dacs_t = jnp.zeros((q, LANES), F32)
    masks = _half_masks()
    dxdt_p = []
    for pp in range(SSD_HPG // 2):
        ps = slice(pp * LANES, (pp + 1) * LANES)
        acc = jnp.zeros((q, LANES), F32)
        for half in range(2):
            hh = 2 * pp + half
            decay = jnp.exp(jnp.where(causal, a_cs[:, hh:hh + 1] - a_cs_t[hh:hh + 1, :], -jnp.inf))
            m = cb * decay
            dyh = jnp.where(masks[half], dy[:, ps], 0.0)
            dm = _mxu(dyh, xdt[:, ps], NT)
            acc = acc + _mxu(m, dyh, TN)
            dcb = dcb + dm * decay
            p = dm * m
            dacs = jnp.where(lane == hh, jnp.sum(p, axis=1, keepdims=True), dacs)
            dacs_t = jnp.where(sub == hh, jnp.sum(p, axis=0, keepdims=True), dacs_t)
        dxdt_p.append(acc)
    dcm = dcm + _mxu(dcb, bm, NN)
    dbm = dbm + _mxu(dcb, cm, TN)
    dxdt = dxdt + jnp.concatenate(dxdt_p, axis=1)
    dxs = dy * d_x + dxdt * dt_x
    rows_x = jnp.concatenate([dend_x, jnp.sum(dy * xs, axis=0, keepdims=True), jnp.zeros((14, SSD_GW), F32)], axis=0)
    dacs_h, ddt_h, rows = _head_sums([dacs_x, dxdt * xs, rows_x], sel)
    dacs = dacs - dacs_t.T + dacs_h
    dacs = dacs + jnp.where(sub == q - 1, rows[0:1], 0.0)
    tri_t = jnp.where(causal, 0.0, 1.0).astype(F32) + jnp.where(lane == sub, 1.0, 0.0)
    da = jnp.dot(tri_t, dacs, precision=lax.Precision.HIGHEST, preferred_element_type=F32)
    ddt = ddt_h + da * a_neg
    dalog = jnp.sum(da * dt, axis=0, keepdims=True) * a_neg
    ddtr = jnp.where(live, ddt * jax.nn.sigmoid(pre), 0.0)
    dbias = jnp.sum(ddtr, axis=0, keepdims=True)
    return dxs, dbm, dcm, ddtr, dprev, dbias, dalog, rows[1:2]


def _chunks_per_step(nc):
    return 13 if nc % 13 == 0 else 1


def _ssd_specs(rev, nc):
    per = _chunks_per_step(nc)
    steps = nc // per
    sidx = (lambda s: steps - 1 - s) if rev else (lambda s: s)
    nb_b = SSD_D_INNER // SSD_STATE
    row = lambda width, col=lambda g: g: pl.BlockSpec((per * CHUNK, width), lambda g, s: (sidx(s), col(g)))
    par = lambda width: pl.BlockSpec((1, width), lambda g, s: (0, g))
    state = lambda: pl.BlockSpec((per, 1, SSD_STATE, SSD_GW), lambda g, s: (sidx(s), g, 0, 0))
    xbc = [row(SSD_GW), row(SSD_STATE, lambda g: nb_b + g), row(SSD_STATE, lambda g: nb_b + SSD_GROUPS + g)]
    return per, steps, sidx, row, par, state, xbc


def _ssd_fwd(xbc_c, dtr, z, dt_bias, a_log, d_skip, nw):
    rows = z.shape[0]
    nc = rows // CHUNK
    per, steps, _, row, par, state, xbc = _ssd_specs(False, nc)

    def body(xs_ref, b_ref, c_ref, dt_ref, z_ref, bias_ref, al_ref, dk_ref, nw_ref, o_ref, y_ref, st_ref, carry):
        s = pl.program_id(1)

        @pl.when(s == 0)
        def _():
            carry[...] = jnp.zeros_like(carry)

        for j in range(per):
            rs = pl.ds(j * CHUNK, CHUNK)
            live = (s * per + j) * CHUNK + lax.broadcasted_iota(jnp.int32, (CHUNK, 1), 0) >= FRONT
            prev = carry[...]
            st_ref[j, 0] = prev
            y, new = _ssd_scan(xs_ref[rs, :].astype(F32), b_ref[rs, :].astype(F32), c_ref[rs, :].astype(F32), dt_ref[rs, :],
                               prev, bias_ref[...], al_ref[...], dk_ref[...], live)
            y_ref[rs, :] = y.astype(ACT_DTYPE)
            o_ref[rs, :] = _ssd_gate(y, z_ref[rs, :].astype(F32), nw_ref[...]).astype(ACT_DTYPE)
            carry[...] = new

    act = jax.ShapeDtypeStruct((rows, SSD_D_INNER), ACT_DTYPE)
    return pl.pallas_call(
        body, name="ssd_fwd", grid=(SSD_GROUPS, steps),
        in_specs=xbc + [row(LANES), row(SSD_GW), par(LANES), par(LANES), par(LANES), par(SSD_GW)],
        out_specs=[row(SSD_GW), row(SSD_GW), state()],
        out_shape=[act, act, jax.ShapeDtypeStruct((nc, SSD_GROUPS, SSD_STATE, SSD_GW), F32)],
        scratch_shapes=[pltpu.VMEM((SSD_STATE, SSD_GW), F32)],
        compiler_params=_params(("parallel", "arbitrary")),
    )(xbc_c, xbc_c, xbc_c, dtr, z, dt_bias, a_log, d_skip, nw)


def _ssd_gate_bwd(y, z, d_bs, w_bs, nw, dproj, col0):
    rows = y.shape[0]
    d = d_bs.shape[1]
    tm = _row_tile(rows)
    assert col0 % SSD_GW == 0

    def body(y_ref, z_ref, db_ref, wb_ref, nw_ref, dp_ref, dy_ref, dz_ref, dnw_ref):
        yv, zv = y_ref[...].astype(F32), z_ref[...].astype(F32)
        dov = _mxu(db_ref[...], wb_ref[...], NT)
        s = jax.nn.sigmoid(zv)
        silu = zv * s
        yz = yv * silu
        r = lax.rsqrt(jnp.mean(yz * yz, axis=-1, keepdims=True) + EPS)
        yhat = yz * r
        dn = dov * nw_ref[...]
        dyz = r * (dn - yhat * jnp.mean(dn * yhat, axis=-1, keepdims=True))
        dy_ref[...] = (dyz * silu).astype(ACT_DTYPE)
        dz_ref[...] = (dyz * yv * (s + silu * (1.0 - s))).astype(ACT_DTYPE)

        @pl.when(pl.program_id(1) == 0)
        def _():
            dnw_ref[...] = jnp.zeros_like(dnw_ref)

        dnw_ref[...] += jnp.sum(dov * yhat, axis=0, keepdims=True)

    spec = pl.BlockSpec((tm, SSD_GW), lambda g, i: (i, g))
    par = pl.BlockSpec((1, SSD_GW), lambda g, i: (0, g))
    act = jax.ShapeDtypeStruct((rows, SSD_D_INNER), ACT_DTYPE)
    return pl.pallas_call(
        body, name="ssd_gate_bwd", grid=(SSD_GROUPS, rows // tm),
        in_specs=[spec, spec, pl.BlockSpec((tm, d), lambda g, i: (i, 0)), pl.BlockSpec((SSD_GW, d), lambda g, i: (g, 0)),
                  par, pl.BlockSpec(memory_space=pl.ANY)],
        out_specs=[spec, pl.BlockSpec((tm, SSD_GW), lambda g, i: (i, col0 // SSD_GW + g)), par],
        out_shape=[act, jax.ShapeDtypeStruct(dproj.shape, dproj.dtype), jax.ShapeDtypeStruct((1, SSD_D_INNER), F32)],
        input_output_aliases={5: 1}, compiler_params=_params(("parallel", "arbitrary")),
    )(y, z, d_bs, w_bs, nw, dproj)


def _ssd_bwd(xbc_c, dtr, dt_bias, a_log, d_skip, states, dy, dproj, col0):
    rows = dy.shape[0]
    nc = rows // CHUNK
    per, steps, sidx, row, par, state, xbc = _ssd_specs(True, nc)
    assert col0 % LANES == 0

    def body(xs_ref, b_ref, c_ref, dt_ref, bias_ref, al_ref, dk_ref, st_ref, dy_ref, dp_ref,
             dxs_ref, db_ref, dc_ref, ddt_ref, dbias_ref, dal_ref, ddk_ref, carry):
        s = pl.program_id(1)

        @pl.when(s == 0)
        def _():
            carry[...] = jnp.zeros_like(carry)
            for r in (dbias_ref, dal_ref, ddk_ref):
                r[...] = jnp.zeros_like(r)

        for j in reversed(range(per)):
            rs = pl.ds(j * CHUNK, CHUNK)
            live = (sidx(s) * per + j) * CHUNK + lax.broadcasted_iota(jnp.int32, (CHUNK, 1), 0) >= FRONT
            dxs, dbm, dcm, ddt, dprev, dbias, dal, ddk = _ssd_scan_bwd(
                xs_ref[rs, :].astype(F32), b_ref[rs, :].astype(F32), c_ref[rs, :].astype(F32), dt_ref[rs, :], st_ref[j, 0],
                bias_ref[...], al_ref[...], dk_ref[...], live, dy_ref[rs, :].astype(F32), carry[...])
            dxs_ref[rs, :] = dxs.astype(ACT_DTYPE)
            db_ref[rs, :] = dbm.astype(ACT_DTYPE)
            dc_ref[rs, :] = dcm.astype(ACT_DTYPE)
            ddt_ref[rs, :] = ddt.astype(ACT_DTYPE)
            carry[...] = dprev
            dbias_ref[...] += dbias
            dal_ref[...] += dal
            ddk_ref[...] += ddk

    bc = jax.ShapeDtypeStruct((rows, SSD_GROUPS * SSD_STATE), ACT_DTYPE)
    head = jax.ShapeDtypeStruct((1, SSD_GROUPS * LANES), F32)
    return pl.pallas_call(
        body, name="ssd_bwd", grid=(SSD_GROUPS, steps),
        in_specs=xbc + [row(LANES), par(LANES), par(LANES), par(LANES), state(), row(SSD_GW), pl.BlockSpec(memory_space=pl.ANY)],
        out_specs=[row(SSD_GW), row(SSD_STATE), row(SSD_STATE), row(LANES, lambda g: col0 // LANES + g),
                   par(LANES), par(LANES), par(LANES)],
        out_shape=[jax.ShapeDtypeStruct((rows, SSD_D_INNER), ACT_DTYPE), bc, bc,
                   jax.ShapeDtypeStruct(dproj.shape, dproj.dtype), head, head, head],
        input_output_aliases={9: 3}, scratch_shapes=[pltpu.VMEM((SSD_STATE, SSD_GW), F32)],
        compiler_params=_params(("parallel", "arbitrary")),
    )(xbc_c, xbc_c, xbc_c, dtr, dt_bias, a_log, d_skip, states, dy, dproj)


def _rotary_tables(rows):
    pos = np.arange(rows, dtype=np.float32) - np.float32(FRONT)
    inv_freq = np.float32(ROPE_BASE) ** (-np.linspace(0.0, 1.0, RET_QK // 2, dtype=np.float32))
    ang = (pos[:, None] * inv_freq[None, :]).astype(np.float32).astype(np.float64)
    lgam = np.log(1.0 - 2.0 ** (-5.0 - np.arange(RET_HEADS, dtype=np.float64))).astype(np.float32)
    lgam = np.broadcast_to(lgam[:, None, None], (RET_HEADS, 8, LANES))
    return jnp.asarray(np.cos(ang).astype(np.float32)), jnp.asarray(np.sin(ang).astype(np.float32)), jnp.asarray(lgam)


def _rotary(t, cos, sin):
    half = t.shape[-1] // 2
    t1, t2 = t[:, :half], t[:, half:]
    return jnp.concatenate([t1 * cos - t2 * sin, t2 * cos + t1 * sin], axis=1)


def _ret_chunk(qh, kh, vh, gh, prev, cos, sin, lg):
    q = CHUNK
    qr = _rotary(qh, cos, sin)
    kr = _rotary(kh, cos, sin) * (RET_QK ** -0.5)
    li = lax.broadcasted_iota(jnp.int32, (q, q), 0)
    si = lax.broadcasted_iota(jnp.int32, (q, q), 1)
    dist = (li - si).astype(F32)
    decay = jnp.exp(jnp.where(li >= si, dist * lg, -jnp.inf))
    idx = lax.broadcasted_iota(jnp.int32, (q, 1), 0).astype(F32)
    scores = _dot(qr, kr, NT) * decay
    out = _dot(scores, vh)
    kv = _dot(kr * jnp.exp((q - 1.0 - idx) * lg), vh, TN)
    out = out + _dot(qr, prev) * jnp.exp((idx + 1.0) * lg)
    new = prev * jnp.exp(q * lg) + kv
    out = out * lax.rsqrt(jnp.mean(out * out, axis=-1, keepdims=True) + EPS)
    return _silu(gh) * out, new


def _ret_specs(rev, nc):
    per = _chunks_per_step(nc)
    steps = nc // per
    sidx = (lambda s: steps - 1 - s) if rev else (lambda s: s)
    row = lambda width: pl.BlockSpec((per * CHUNK, width), lambda h, s: (sidx(s), h))
    tab = lambda: pl.BlockSpec((per * CHUNK, RET_QK // 2), lambda h, s: (sidx(s), 0))
    lgs = lambda: pl.BlockSpec((1, 8, LANES), lambda h, s: (h, 0, 0))
    state = lambda: pl.BlockSpec((per, 1, RET_QK, RET_V), lambda h, s: (sidx(s), h, 0, 0))
    part = lambda width, k: pl.BlockSpec((per * CHUNK, width), lambda h, s: (sidx(s), h * (RET_HW // width) + k))
    ins = [part(RET_QK, 0), part(RET_QK, 1), part(RET_V, 1), part(RET_V, 2), tab(), tab(), lgs()]
    return per, steps, sidx, row, state, ins


def _ret_fwd(qkvg, cos, sin, lgam):
    rows = qkvg.shape[0]
    nc = rows // CHUNK
    per, steps, _, row, state, ins = _ret_specs(False, nc)
    q = k = v = g = qkvg

    def body(q_ref, k_ref, v_ref, g_ref, cos_ref, sin_ref, lg_ref, y_ref, st_ref, carry):
        @pl.when(pl.program_id(1) == 0)
        def _():
            carry[...] = jnp.zeros_like(carry)

        for j in range(per):
            rs = pl.ds(j * CHUNK, CHUNK)
            prev = carry[...]
            st_ref[j, 0] = prev.astype(ACT_DTYPE)
            out, new = _ret_chunk(q_ref[rs, :].astype(F32), k_ref[rs, :].astype(F32), v_ref[rs, :].astype(F32),
                                  g_ref[rs, :].astype(F32), prev, cos_ref[rs, :], sin_ref[rs, :], lg_ref[0, 0:1, 0:1])
            y_ref[rs, :] = out.astype(ACT_DTYPE)
            carry[...] = new

    return pl.pallas_call(
        body, name="ret_fwd", grid=(RET_HEADS, steps), in_specs=ins, out_specs=[row(RET_V), state()],
        out_shape=[jax.ShapeDtypeStruct((rows, RET_HEADS * RET_V), ACT_DTYPE),
                   jax.ShapeDtypeStruct((nc, RET_HEADS, RET_QK, RET_V), ACT_DTYPE)],
        scratch_shapes=[pltpu.VMEM((RET_QK, RET_V), F32)],
        compiler_params=_params(("parallel", "arbitrary")),
    )(q, k, v, g, cos, sin, lgam)


def _ret_bwd(qkvg, cos, sin, lgam, states, dy, dproj):
    rows = qkvg.shape[0]
    nc = rows // CHUNK
    per, steps, sidx, row, state, ins = _ret_specs(True, nc)

    def body(q_ref, k_ref, v_ref, g_ref, cos_ref, sin_ref, lg_ref, st_ref, dy_ref, dp_ref, o_ref, carry):
        @pl.when(pl.program_id(1) == 0)
        def _():
            carry[...] = jnp.zeros_like(carry)

        for j in reversed(range(per)):
            rs = pl.ds(j * CHUNK, CHUNK)
            fn = functools.partial(_ret_chunk, cos=cos_ref[rs, :], sin=sin_ref[rs, :], lg=lg_ref[0, 0:1, 0:1])
            _, vjp = jax.vjp(fn, q_ref[rs, :].astype(F32), k_ref[rs, :].astype(F32), v_ref[rs, :].astype(F32),
                             g_ref[rs, :].astype(F32), st_ref[j, 0].astype(F32))
            dq, dk, dv, dg, dprev = vjp((dy_ref[rs, :].astype(F32), carry[...]))
            o_ref[rs, 0:RET_QK] = dq.astype(ACT_DTYPE)
            o_ref[rs, RET_QK:2 * RET_QK] = dk.astype(ACT_DTYPE)
            o_ref[rs, 2 * RET_QK:2 * RET_QK + RET_V] = dv.astype(ACT_DTYPE)
            o_ref[rs, 2 * RET_QK + RET_V:RET_HW] = dg.astype(ACT_DTYPE)
            carry[...] = dprev

    return pl.pallas_call(
        body, name="ret_bwd", grid=(RET_HEADS, steps),
        in_specs=ins + [state(), row(RET_V), pl.BlockSpec(memory_space=pl.ANY)],
        out_specs=pl.BlockSpec((per * CHUNK, RET_HW), lambda h, s: (sidx(s), h)),
        out_shape=jax.ShapeDtypeStruct(dproj.shape, dproj.dtype), input_output_aliases={9: 0},
        scratch_shapes=[pltpu.VMEM((RET_QK, RET_V), F32)],
        compiler_params=_params(("parallel", "arbitrary")),
    )(qkvg, qkvg, qkvg, qkvg, cos, sin, lgam, states, dy, dproj)


def _merge_fwd(bs, br, gates):
    rows, d = bs.shape
    tm = _row_tile(rows)

    def body(bs_ref, br_ref, gs_ref, gr_ref, o_ref):
        o_ref[...] = (jax.nn.sigmoid(gs_ref[...].astype(F32)) * bs_ref[...].astype(F32)
                      + jax.nn.sigmoid(gr_ref[...].astype(F32)) * br_ref[...].astype(F32)).astype(ACT_DTYPE)

    spec = pl.BlockSpec((tm, d), lambda i: (i, 0))
    return pl.pallas_call(
        body, name="merge_fwd", grid=(rows // tm,), in_specs=[spec, spec, spec, pl.BlockSpec((tm, d), lambda i: (i, 1))],
        out_specs=spec, out_shape=jax.ShapeDtypeStruct((rows, d), ACT_DTYPE), compiler_params=_params(("parallel",)),
    )(bs, br, gates, gates)


def _merge_bwd(dh1, w_o, bs, br, gates, dproj, col0):
    rows, d = bs.shape
    tm = _row_tile(rows)
    assert col0 % (2 * d) == 0

    def body(dh_ref, wo_ref, bs_ref, br_ref, gs_ref, gr_ref, dp_ref, dbs_ref, dbr_ref, dg_ref):
        dmv = _mxu(dh_ref[...], wo_ref[...], NT)
        for k, (b_ref, g_ref, db_ref) in enumerate(((bs_ref, gs_ref, dbs_ref), (br_ref, gr_ref, dbr_ref))):
            s = jax.nn.sigmoid(g_ref[...].astype(F32))
            db_ref[...] = (dmv * s).astype(ACT_DTYPE)
            dg_ref[:, k * d:(k + 1) * d] = (dmv * b_ref[...].astype(F32) * s * (1.0 - s)).astype(ACT_DTYPE)

    spec = pl.BlockSpec((tm, d), lambda i: (i, 0))
    shp = jax.ShapeDtypeStruct((rows, d), ACT_DTYPE)
    return pl.pallas_call(
        body, name="merge_bwd", grid=(rows // tm,),
        in_specs=[spec, pl.BlockSpec(w_o.shape, lambda i: (0, 0)), spec, spec, spec, pl.BlockSpec((tm, d), lambda i: (i, 1)),
                  pl.BlockSpec(memory_space=pl.ANY)],
        out_specs=[spec, spec, pl.BlockSpec((tm, 2 * d), lambda i: (i, col0 // (2 * d)))],
        out_shape=[shp, shp, jax.ShapeDtypeStruct(dproj.shape, dproj.dtype)], input_output_aliases={6: 2},
        compiler_params=_params(("parallel",)),
    )(dh1, w_o, bs, br, gates, gates, dproj)


def _place():
    x, y, c = lax.axis_index("x"), lax.axis_index("y"), lax.axis_index("c")
    return x, y, c


def _slot(p):
    return 4 * p[0] + 2 * p[1] + p[2]


def _allgather(arrs, name):
    n = len(arrs)
    any_spec = pl.BlockSpec(memory_space=pl.ANY)

    def body(*refs):
        ins, outs = refs[:n], refs[n:2 * n]
        send_sems, recv_sems, local_sems = refs[2 * n:]
        x, y, c = _place()
        me, sibling = (x, y, c), (x, y, 1 - c)
        chips = [(1 - x, y), (x, 1 - y), (1 - x, 1 - y)]

        def copy(a, k, block, to, src=None):
            dst = outs[a].at[_slot(block)]
            return pltpu.make_async_remote_copy(
                src_ref=dst if src is None else src, dst_ref=dst, send_sem=send_sems.at[a * 7 + k],
                recv_sem=recv_sems.at[a * 7 + k], device_id=to, device_id_type=MESH)

        mine, first, passed = [], [], []
        for a in range(n):
            cp = pltpu.make_async_copy(ins[a], outs[a].at[_slot(me)], local_sems.at[a])
            cp.start()
            mine.append(cp)
            first.append(copy(a, 0, me, sibling, src=ins[a]))
            first += [copy(a, 1 + j, me, (*chip, c), src=ins[a]) for j, chip in enumerate(chips)]
        for cp in first:
            cp.start()
        for j, chip in enumerate(chips):
            for a in range(n):
                copy(a, 1 + j, (*chip, c), me).wait_recv()
                cp = copy(a, 4 + j, (*chip, c), sibling)
                cp.start()
                passed.append(cp)
        for a in range(n):
            copy(a, 0, sibling, me).wait_recv()
            for j, chip in enumerate(chips):
                copy(a, 4 + j, (*chip, 1 - c), me).wait_recv()
        for cp in first + passed:
            cp.wait_send()
        for cp in mine:
            cp.wait()

    return pl.pallas_call(
        body, name=name, in_specs=[any_spec] * n, out_specs=[any_spec] * n,
        out_shape=[jax.ShapeDtypeStruct((N_DEV,) + a.shape, a.dtype) for a in arrs],
        scratch_shapes=[pltpu.SemaphoreType.DMA((7 * n,)), pltpu.SemaphoreType.DMA((7 * n,)), pltpu.SemaphoreType.DMA((n,))],
    )(*arrs)


def _peers():
    x, y, c = _place()
    return (x, y, c), [(x ^ dx, y ^ dy, c ^ dc) for dx in (0, 1) for dy in (0, 1) for dc in (0, 1)][1:]


def _exchange_copies(srcs, lands, send_sems, recv_sems, scatter, sender):
    me, peers = _peers()
    out = []
    for a, (src, land) in enumerate(zip(srcs, lands, strict=True)):
        for k, peer in enumerate(peers):
            src_ref = src.at[_slot(peer)] if scatter else src
            out.append(pltpu.make_async_remote_copy(
                src_ref=src_ref, dst_ref=land.at[_slot(me if sender else peer)], send_sem=send_sems.at[a * 7 + k],
                recv_sem=recv_sems.at[a * 7 + k], device_id=peer, device_id_type=MESH))
    return out


_HBM = pl.BlockSpec(memory_space=pltpu.HBM)
_SEM = pl.BlockSpec(memory_space=pltpu.SEMAPHORE)
_EFFECT = pltpu.SideEffectType.DATAFLOW_SIDE_EFFECTING


def _exchange_start(srcs, scatter, name, after=None):
    n = len(srcs)
    land_shapes = [s.shape if scatter else (N_DEV,) + s.shape for s in srcs]
    n_in = 2 * n + (after is not None)

    def body(*refs):
        for cp in _exchange_copies(refs[:n], refs[n:2 * n], refs[n_in], refs[n_in + 1], scatter, True):
            cp.start()
        refs[-1][...] = jnp.zeros_like(refs[-1])

    args = [pltpu.with_memory_space_constraint(s, pltpu.HBM) for s in srcs]
    args += [pltpu.with_memory_space_constraint(lax.empty(shp, s.dtype), pltpu.HBM) for s, shp in zip(srcs, land_shapes)]
    thru_shapes = tuple(pltpu.HBM(a.shape, a.dtype) for a in args)
    extra = [] if after is None else [after]
    outs = pl.pallas_call(
        body, name=name,
        out_shape=(pltpu.SemaphoreType.DMA((7 * n,)), pltpu.SemaphoreType.DMA((7 * n,))) + thru_shapes
        + (jax.ShapeDtypeStruct((8, LANES), F32),),
        in_specs=[_HBM] * (2 * n) + [pl.BlockSpec(memory_space=pl.ANY)] * len(extra),
        out_specs=(_SEM, _SEM) + (_HBM,) * (2 * n) + (pl.BlockSpec(memory_space=pltpu.VMEM),),
        input_output_aliases={i: 2 + i for i in range(2 * n)},
        compiler_params=pltpu.CompilerParams(has_side_effects=_EFFECT),
    )(*args, *extra)
    return outs[:-1], outs[-1]


def _exchange_wait(handle, scatter, after, name):
    n = (len(handle) - 2) // 2
    thru = handle[2:]

    def body(*refs):
        for cp in _exchange_copies(refs[:n], refs[n:2 * n], refs[2 * n], refs[2 * n + 1], scatter, False):
            cp.wait_send()
            cp.wait_recv()

    outs = pl.pallas_call(
        body, name=name, out_shape=tuple(pltpu.HBM(t.shape, t.dtype) for t in thru),
        in_specs=[_HBM] * (2 * n) + [_SEM, _SEM, pl.BlockSpec(memory_space=pl.ANY)], out_specs=(_HBM,) * (2 * n),
        input_output_aliases={i: i for i in range(2 * n)},
        compiler_params=pltpu.CompilerParams(has_side_effects=_EFFECT),
    )(*thru, handle[0], handle[1], after)
    return list(outs[:n]), list(outs[n:])


def _allreduce_small(pack):
    rows, lanes = pack.shape

    def body(x_ref, o_ref, buf, send_sems, recv_sems):
        x, y, c = _place()
        me, sibling = (x, y, c), (x, y, 1 - c)
        chips = [(1 - x, y), (x, 1 - y), (1 - x, 1 - y)]

        def copy(k, block, to, src=None):
            dst = buf.at[_slot(block)]
            return pltpu.make_async_remote_copy(
                src_ref=dst if src is None else src, dst_ref=dst, send_sem=send_sems.at[k], recv_sem=recv_sems.at[k],
                device_id=to, device_id_type=MESH)

        buf[_slot(me)] = x_ref[...]
        first = [copy(0, me, sibling, src=x_ref)]
        first += [copy(1 + j, me, (*chip, c), src=x_ref) for j, chip in enumerate(chips)]
        for cp in first:
            cp.start()
        passed = [copy(4 + j, (*chip, c), sibling) for j, chip in enumerate(chips)]
        for j, chip in enumerate(chips):
            copy(1 + j, (*chip, c), me).wait_recv()
            passed[j].start()
        copy(0, sibling, me).wait_recv()
        for j, chip in enumerate(chips):
            copy(4 + j, (*chip, 1 - c), me).wait_recv()
        for cp in first + passed:
            cp.wait_send()
        acc = buf[0]
        for i in range(1, N_DEV):
            acc = acc + buf[i]
        o_ref[...] = acc

    vmem = pl.BlockSpec(memory_space=pltpu.VMEM)
    return pl.pallas_call(
        body, name="allreduce_small", in_specs=[vmem], out_specs=vmem,
        out_shape=jax.ShapeDtypeStruct((rows, lanes), F32),
        scratch_shapes=[pltpu.VMEM((N_DEV, rows, lanes), F32), pltpu.SemaphoreType.DMA((7,)), pltpu.SemaphoreType.DMA((7,))],
        compiler_params=pltpu.CompilerParams(vmem_limit_bytes=VMEM_LIMIT),
    )(pack)


def _adamw(w, g, m, v):
    m = ADAM_B1 * m + (1.0 - ADAM_B1) * g
    v = ADAM_B2 * v + (1.0 - ADAM_B2) * jnp.square(g)
    m_hat = m / (1.0 - ADAM_B1 ** ADAM_STEP)
    v_hat = v / (1.0 - ADAM_B2 ** ADAM_STEP)
    delta = -ADAM_LR * (m_hat / (jnp.sqrt(v_hat) + ADAM_EPS) + ADAM_WD * w)
    return delta, m, v


def _adam_shard(own, parts, w, m, v, name):
    r, c = w.shape
    tr = _pick(r, (128, 64, 32, 16, 8))

    def body(own_ref, p_ref, w_ref, m_ref, v_ref, g_ref, d_ref, nm_ref, nv_ref):
        _, peers = _peers()
        g = own_ref[...].astype(F32)
        for peer in peers:
            g = g + p_ref[_slot(peer)].astype(F32)
        g_ref[...] = g
        d_ref[...], nm_ref[...], nv_ref[...] = _adamw(w_ref[...], g, m_ref[...], v_ref[...])

    spec = pl.BlockSpec((tr, c), lambda i: (i, 0))
    shp = jax.ShapeDtypeStruct((r, c), F32)
    return pl.pallas_call(
        body, name=name, grid=(r // tr,),
        in_specs=[spec, pl.BlockSpec((N_DEV, tr, c), lambda i: (0, i, 0)), spec, spec, spec], out_specs=[spec] * 4,
        out_shape=[shp] * 4, compiler_params=_params(("parallel",)),
    )(own, parts, w, m, v)


def _adam_small(w, g, m, v):
    r, c = w.shape

    def body(w_ref, g_ref, m_ref, v_ref, d_ref, nm_ref, nv_ref):
        d_ref[...], nm_ref[...], nv_ref[...] = _adamw(w_ref[...], g_ref[...], m_ref[...], v_ref[...])

    shp = jax.ShapeDtypeStruct((r, c), F32)
    return pl.pallas_call(body, name="adam_small", out_shape=[shp] * 3)(w, g, m, v)


def _column_plan(pieces, shard_w):
    plan = []
    for c0, width, d0 in pieces:
        c = c0
        while c < c0 + width:
            s, a = divmod(c, shard_w)
            w = min(c0 + width - c, shard_w - a)
            plan.append((s, a, w, d0 + c - c0))
            c += w
    return plan


def _cols_from_shards(g, plan, out_w, zero, name):
    _, r, sw = g.shape
    tr = _pick(r, (128,))

    def body(x_ref, o_ref):
        for d0, w in zero:
            o_ref[:, d0:d0 + w] = jnp.zeros((tr, w), g.dtype)
        for s, a, w, d0 in plan:
            o_ref[:, d0:d0 + w] = x_ref[s, :, a:a + w]

    return pl.pallas_call(
        body, name=name, grid=(r // tr,), in_specs=[pl.BlockSpec((N_DEV, tr, sw), lambda i: (0, i, 0))],
        out_specs=pl.BlockSpec((tr, out_w), lambda i: (i, 0)), out_shape=jax.ShapeDtypeStruct((r, out_w), g.dtype),
        compiler_params=_params(("parallel",)),
    )(g)


def _shards_from_cols(srcs, plans, shard_w, name):
    r = srcs[0].shape[0]
    tr = _pick(r, (128,))
    n = len(srcs)

    def body(*refs):
        o_ref = refs[n]
        for x_ref, plan in zip(refs[:n], plans, strict=True):
            for s, a, w, d0 in plan:
                o_ref[s, :, a:a + w] = x_ref[:, d0:d0 + w].astype(COMM_DTYPE)

    return pl.pallas_call(
        body, name=name, grid=(r // tr,), in_specs=[pl.BlockSpec((tr, t.shape[1]), lambda i: (i, 0)) for t in srcs],
        out_specs=pl.BlockSpec((N_DEV, tr, shard_w), lambda i: (0, i, 0)),
        out_shape=jax.ShapeDtypeStruct((N_DEV, r, shard_w), COMM_DTYPE), compiler_params=_params(("parallel",)),
    )(*srcs)


def _pack(arrs):
    rows = []
    for a in arrs:
        flat = a.reshape(-1).astype(F32)
        rows.append(jnp.pad(flat, (0, (-flat.shape[0]) % (8 * LANES))).reshape(-1, LANES))
    return jnp.concatenate(rows, axis=0)


def _unpack(pack, shapes):
    out, r = [], 0
    for s in shapes:
        size = math.prod(s)
        nr = -(-size // (8 * LANES)) * 8
        out.append(pack[r:r + nr].reshape(-1)[:size].reshape(s))
        r += nr
    return out


def _group_lanes(t):
    lead = t.shape[:-1]
    t = t.reshape(lead + (SSD_GROUPS, SSD_HPG))
    t = jnp.pad(t, [(0, 0)] * len(lead) + [(0, 0), (0, LANES - SSD_HPG)])
    return t.reshape(lead + (SSD_GROUPS * LANES,))


def _ungroup_lanes(t):
    lead = t.shape[:-1]
    return t.reshape(lead + (SSD_GROUPS, LANES))[..., :SSD_HPG].reshape(lead + (SSD_HEADS,))


def kernel(x, meta_tokens, mix_norm_w, w_in, ssd_conv_w, ssd_conv_b, ssd_dt_bias, ssd_A_log, ssd_D, ssd_norm_w, w_branch_ssd, w_branch_ret, w_out, ffn_norm_w, w_up, ffn_conv_w, ffn_conv_b, w_down, final_norm_w, loss_target, m_meta_tokens, m_mix_norm_w, m_w_in, m_ssd_conv_w, m_ssd_conv_b, m_ssd_dt_bias, m_ssd_A_log, m_ssd_D, m_ssd_norm_w, m_w_branch_ssd, m_w_branch_ret, m_w_out, m_ffn_norm_w, m_w_up, m_ffn_conv_w, m_ffn_conv_b, m_w_down, m_final_norm_w, v_meta_tokens, v_mix_norm_w, v_w_in, v_ssd_conv_w, v_ssd_conv_b, v_ssd_dt_bias, v_ssd_A_log, v_ssd_D, v_ssd_norm_w, v_w_branch_ssd, v_w_branch_ret, v_w_out, v_ffn_norm_w, v_w_up, v_ffn_conv_w, v_ffn_conv_b, v_w_down, v_final_norm_w):
    seq, d = x.shape[1], x.shape[2]
    rows = seq + PAD_ROWS
    tm = _row_tile(rows)
    me = _slot(_place())
    d_ff = w_down.shape[1] * N_DEV

    big = [w_in[0], w_branch_ssd[0], w_branch_ret[0], w_out[0], w_up[0], w_down[0]]
    first = _allgather([w_in[0].astype(COMM_DTYPE), meta_tokens, ssd_conv_w[0], ffn_conv_w[0]], "gather_first")
    rest_src = [b.astype(COMM_DTYPE) for b in big[1:]]
    rest_handle, rest_token = _exchange_start(rest_src, False, "gather_rest_start", after=first[0])
    cols = lambda t: jnp.transpose(t, (1, 0, 2)).reshape(t.shape[1], -1)
    rws = lambda t: t.reshape(-1, t.shape[2])
    conv_w, fconv_w = cols(first[2]), cols(first[3])
    meta_full = cols(first[1]) + rest_token[0, 0]
    widths = [SSD_D_INNER, SSD_CONV_DIM, SSD_HEADS, RET_HEADS * RET_QK, RET_HEADS * RET_QK, RET_HEADS * RET_V,
              RET_HEADS * RET_V, d, d]
    offs = [0]
    for wd in widths:
        offs.append(offs[-1] + wd)
    r0, z0 = 0, RET_HEADS * RET_HW
    g0 = z0 + widths[0]
    x0 = g0 + 2 * d
    dt0 = x0 + widths[1]
    in_p = dt0 + SSD_GROUPS * LANES
    pieces = []
    for hd in range(RET_HEADS):
        base = r0 + hd * RET_HW
        pieces += [(offs[3] + hd * RET_QK, RET_QK, base), (offs[4] + hd * RET_QK, RET_QK, base + RET_QK),
                   (offs[5] + hd * RET_V, RET_V, base + 2 * RET_QK), (offs[6] + hd * RET_V, RET_V, base + 2 * RET_QK + RET_V)]
    pieces += [(offs[0], widths[0], z0), (offs[7], d, g0), (offs[8], d, g0 + d), (offs[1], widths[1], x0)]
    pieces += [(offs[2] + SSD_HPG * grp, SSD_HPG, dt0 + LANES * grp) for grp in range(SSD_GROUPS)]
    in_plan = _column_plan(pieces, w_in.shape[2])
    w_in_p = _cols_from_shards(first[0], in_plan, in_p, [(dt0, SSD_GROUPS * LANES)], "w_in_columns")

    h0 = jnp.concatenate([jnp.zeros((FRONT, d), F32), meta_full, x[0]], axis=0)
    u1 = _rms_fwd(h0, mix_norm_w, "rms1")
    in_proj = lambda c0, width, dtype, nm: _mm(u1, w_in_p, mode="nn", out_dtype=dtype, tm=tm, tk=d, name="in_proj_" + nm,
                                               tn=_pick(width, (1024, 512)), b_n0=c0, n_out=width)
    qkvg = in_proj(r0, RET_HEADS * RET_HW, ACT_DTYPE, "qkvg")
    z = in_proj(z0, widths[0], ACT_DTYPE, "z")
    gates = in_proj(g0, 2 * d, ACT_DTYPE, "gates")
    dtr = in_proj(dt0, SSD_GROUPS * LANES, F32, "dt")
    xbc, xbc_c = _xbc_proj_conv(u1, w_in_p, x0, conv_w, ssd_conv_b)
    bias_p, alog_p, dsk_p = _group_lanes(ssd_dt_bias), _group_lanes(ssd_A_log), _group_lanes(ssd_D)
    y_ssd, y_scan, ssd_states = _ssd_fwd(xbc_c, dtr, z, bias_p, alog_p, dsk_p, ssd_norm_w)
    cos, sin, lgam = _rotary_tables(rows)
    y_ret, ret_states = _ret_fwd(qkvg, cos, sin, lgam)
    rest_own, rest = _exchange_wait(rest_handle, False, y_ret, "gather_rest_wait")
    rest = [lax.dynamic_update_index_in_dim(land, own, me, 0) for land, own in zip(rest, rest_own, strict=True)]
    w_bs, w_br, w_o, w_dn = rws(rest[0]), rws(rest[1]), rws(rest[2]), rws(rest[4])
    w_up_f = _cols_from_shards(rest[3], _column_plan([(0, 2 * d_ff, 0)], w_up.shape[2]), 2 * d_ff, [], "w_up_columns")
    bs = _mm(y_ssd, w_bs, mode="nn", out_dtype=ACT_DTYPE, tm=tm, tn=d, tk=SSD_D_INNER, name="branch_ssd")
    br = _mm(y_ret, w_br, mode="nn", out_dtype=ACT_DTYPE, tm=tm, tn=d, tk=RET_HEADS * RET_V, name="branch_ret")
    merged = _merge_fwd(bs, br, gates)
    h1 = _mm(merged, w_o, mode="nn", out_dtype=F32, tm=tm, tn=d, tk=d, name="out_proj", add=h0)
    u2 = _rms_fwd(h1, ffn_norm_w, "rms2")
    up_g, up_v, act = _ffn_up_conv(u2, w_up_f, fconv_w, ffn_conv_b)
    h2 = _mm(act, w_dn, mode="nn", out_dtype=F32, tm=tm, tn=d, tk=d_ff, name="ffn_down", add=h1)
    tgt = jnp.pad(loss_target[0], ((PAD_ROWS, 0), (0, 0)))
    dh2, loss_acc, g_final = _loss_head(h2, tgt, final_norm_w.reshape(1, d))

    tff = _pick(d_ff, (1408, 256))
    tkr = _pick(rows, (1664, 128))
    tkr2 = _pick(rows, (4160, 128))
    rparts = lambda t: t.reshape(N_DEV, -1, t.shape[1])
    d_act = _mm(dh2, w_dn, mode="nt", out_dtype=ACT_DTYPE, tm=tm, tn=tff, tk=d, name="d_act")
    g_w_dn = _mm(act, dh2, mode="tn", out_dtype=COMM_DTYPE, tm=tff, tn=d, tk=tkr, name="g_w_down")
    c_dn = [rparts(g_w_dn)]
    h_dn, t_dn = _exchange_start(c_dn, True, "scatter_down_start")
    d_up_g, d_up_v, g_fcw_g, g_fcb_g, g_fcw_v, g_fcb_v, g_w_up_g, g_w_up_v = _ffn_conv_bwd(
        up_g, up_v, d_act, fconv_w, ffn_conv_b + t_dn[0, 0], u2)
    g_fconv_w = jnp.concatenate([g_fcw_g, g_fcw_v], axis=1)
    g_fconv_b = jnp.concatenate([g_fcb_g, g_fcb_v], axis=1)
    c_up = [_shards_from_cols([g_w_up_g, g_w_up_v], [_column_plan([(0, d_ff, 0)], w_up.shape[2]),
                                                     _column_plan([(d_ff, d_ff, 0)], w_up.shape[2])], w_up.shape[2], "g_w_up_shards")]
    h_up, t_up = _exchange_start(c_up, True, "scatter_up_start")
    du2 = _mm(d_up_g, w_up_f, mode="nt", out_dtype=F32, tm=tm, tn=d, tk=d_ff, name="d_u2_gate", after=t_up)
    du2 = _mm(d_up_v, w_up_f, mode="nt", out_dtype=F32, tm=tm, tn=d, tk=d_ff, name="d_u2_value", add=du2, b_k0=d_ff)
    dh1, g_ffn_norm = _rms_bwd(du2, h1, ffn_norm_w, dh2, "rms2_bwd")
    g_w_o =_mm(merged, dh1, mode="tn", out_dtype=COMM_DTYPE, tm=d, tn=d, tk=tkr, name="g_w_out")
    dproj = lax.empty((rows, in_p), ACT_DTYPE)
    d_bs, d_br, dproj = _merge_bwd(dh1, w_o, bs, br, gates, dproj, g0)
    g_w_bs = _mm(y_ssd, d_bs, mode="tn", out_dtype=COMM_DTYPE, tm=1024, tn=d, tk=tkr2, name="g_w_branch_ssd")
    g_w_br = _mm(y_ret, d_br, mode="tn", out_dtype=COMM_DTYPE, tm=1024, tn=d, tk=tkr2, name="g_w_branch_ret")
    c_mid = [rparts(g_w_bs), rparts(g_w_br), rparts(g_w_o)]
    h_mid, t_mid = _exchange_start(c_mid, True, "scatter_mid_start")
    d_yscan, dproj, g_nw = _ssd_gate_bwd(y_scan, z, d_bs, w_bs, ssd_norm_w + t_mid[0, 0], dproj, z0)
    dxs, d_bm, d_cm, dproj, g_bias_p, g_alog_p, g_dsk_p = _ssd_bwd(
        xbc_c, dtr, bias_p, alog_p, dsk_p, ssd_states, d_yscan, dproj, dt0)
    dproj, g_conv_w, g_conv_b = _ssd_conv_bwd(xbc, dxs, d_bm, d_cm, conv_w, ssd_conv_b, dproj, x0)
    d_yret = _mm(d_br, w_br, mode="nt", out_dtype=ACT_DTYPE, tm=tm, tn=1024, tk=d, name="d_y_ret")
    dproj = _ret_bwd(qkvg, cos, sin, lgam, ret_states, d_yret, dproj)
    g_w_in_p = _mm(u1, dproj, mode="tn", out_dtype=F32, tm=d, tn=_pick(in_p, (768, 512)), tk=tkr2, name="g_w_in")
    c_in = [_shards_from_cols([g_w_in_p], [in_plan], w_in.shape[2], "g_w_in_shards")]
    h_in, t_in = _exchange_start(c_in, True, "scatter_in_start")
    du1 = _mm(dproj, w_in_p, mode="nt", out_dtype=F32, tm=tm, tn=d, tk=_pick(in_p, (4608, 512)), name="d_u1", after=t_in)
    dh0, g_mix_norm = _rms_bwd(du1, h0, mix_norm_w, dh1, "rms1_bwd")
    grad_x = dh0[PAD_ROWS:][None]

    landed = {}
    for key, handle, names in (("in", h_in, ["w_in"]), ("mid", h_mid, ["w_branch_ssd", "w_branch_ret", "w_out"]),
                               ("up", h_up, ["w_up"]), ("down", h_dn, ["w_down"])):
        srcs, lands = _exchange_wait(handle, True, dh0, f"scatter_{key}_wait")
        for nm, land, src in zip(names, lands, srcs, strict=True):
            landed[nm] = (lax.dynamic_index_in_dim(src, me, 0, keepdims=False), land)
    big_m = [m_w_in, m_w_branch_ssd, m_w_branch_ret, m_w_out, m_w_up, m_w_down]
    big_v = [v_w_in, v_w_branch_ssd, v_w_branch_ret, v_w_out, v_w_up, v_w_down]
    big_names = ["w_in", "w_branch_ssd", "w_branch_ret", "w_out", "w_up", "w_down"]
    big_out = {}
    for nm, w, m, v_ in zip(big_names, big, big_m, big_v, strict=True):
        big_out[nm] = [t[None] for t in _adam_shard(*landed[nm], w, m[0], v_[0], "adam_" + nm)]

    small_g = [dh0[FRONT:PAD_ROWS], g_mix_norm, g_conv_w, g_conv_b, _ungroup_lanes(g_bias_p), _ungroup_lanes(g_alog_p),
               _ungroup_lanes(g_dsk_p), g_nw, g_ffn_norm, g_fconv_w, g_fconv_b, g_final, loss_acc[0:1, 0:1]]
    total = _unpack(_allreduce_small(_pack(small_g)), [t.shape for t in small_g])
    loss = total[12].reshape(())
    shard = lambda t, width: lax.dynamic_slice_in_dim(t, me * width, width, axis=1)
    small_names = ["meta_tokens", "mix_norm_w", "ssd_conv_w", "ssd_conv_b", "ssd_dt_bias", "ssd_A_log", "ssd_D", "ssd_norm_w",
                   "ffn_norm_w", "ffn_conv_w", "ffn_conv_b", "final_norm_w"]
    small_w = [meta_tokens, mix_norm_w, ssd_conv_w, ssd_conv_b, ssd_dt_bias, ssd_A_log, ssd_D, ssd_norm_w, ffn_norm_w,
               ffn_conv_w, ffn_conv_b, final_norm_w]
    small_m = [m_meta_tokens, m_mix_norm_w, m_ssd_conv_w, m_ssd_conv_b, m_ssd_dt_bias, m_ssd_A_log, m_ssd_D, m_ssd_norm_w,
               m_ffn_norm_w, m_ffn_conv_w, m_ffn_conv_b, m_final_norm_w]
    small_v = [v_meta_tokens, v_mix_norm_w, v_ssd_conv_w, v_ssd_conv_b, v_ssd_dt_bias, v_ssd_A_log, v_ssd_D, v_ssd_norm_w,
               v_ffn_norm_w, v_ffn_conv_w, v_ffn_conv_b, v_final_norm_w]
    grads = total[:12]
    grads[0] = shard(grads[0], meta_tokens.shape[1])
    grads[2] = shard(grads[2], ssd_conv_w.shape[2])
    grads[9] = shard(grads[9], ffn_conv_w.shape[2])
    grads = [t.reshape(w.shape) for t, w in zip(grads, small_w, strict=True)]
    shapes = [w.shape for w in small_w]
    upd = _adam_small(_pack(small_w), _pack(grads), _pack(small_m), _pack(small_v))
    small_out = {nm: [gr_] + [u[i] for u in (_unpack(t, shapes) for t in upd)]
                 for i, (nm, gr_) in enumerate(zip(small_names, grads, strict=True))}

    order = ["meta_tokens", "mix_norm_w", "w_in", "ssd_conv_w", "ssd_conv_b", "ssd_dt_bias", "ssd_A_log", "ssd_D", "ssd_norm_w",
             "w_branch_ssd", "w_branch_ret", "w_out", "ffn_norm_w", "w_up", "ffn_conv_w", "ffn_conv_b", "w_down", "final_norm_w"]
    res = {**big_out, **small_out}
    return (loss, grad_x, *[res[nm][0] for nm in order], *[res[nm][1] for nm in order], *[res[nm][2] for nm in order],
            *[res[nm][3] for nm in order])
```

```python
import functools
import math

import jax
import jax.numpy as jnp
import numpy as np
from jax import lax
from jax.experimental import pallas as pl
from jax.experimental.pallas import tpu as pltpu

F32 = jnp.float32
MXU_DTYPE = jnp.bfloat16
ACT_DTYPE = jnp.bfloat16
COMM_DTYPE = jnp.bfloat16

N_META = 16
CHUNK = 128
FRONT = CHUNK - N_META
PAD_ROWS = FRONT + N_META
EPS = 1e-6
N_DEV = 8

SSD_D_INNER = 2048
SSD_HEAD_DIM = 64
SSD_HEADS = 32
SSD_GROUPS = 4
SSD_HPG = SSD_HEADS // SSD_GROUPS
SSD_STATE = 128
SSD_CONV = 4
SSD_CONV_DIM = SSD_D_INNER + 2 * SSD_GROUPS * SSD_STATE
SSD_GW = SSD_D_INNER // SSD_GROUPS
RET_HEADS = 4
RET_QK = 256
RET_V = 512
RET_HW = 2 * RET_QK + 2 * RET_V
ROPE_BASE = 10000.0
FFN_CONV = 3
HALO = 16
LANES = 128

ADAM_LR = 0.001
ADAM_B1 = 0.9
ADAM_B2 = 0.999
ADAM_EPS = 1e-08
ADAM_WD = 0.01
ADAM_STEP = 10

VMEM_LIMIT = 56 * 1024 * 1024
MESH = pl.DeviceIdType.MESH

NN = (((1,), (0,)), ((), ()))
NT = (((1,), (1,)), ((), ()))
TN = (((0,), (0,)), ((), ()))


def _params(sem):
    return pltpu.CompilerParams(dimension_semantics=sem, vmem_limit_bytes=VMEM_LIMIT)


def _mxu(a, b, dn):
    return lax.dot_general(a.astype(MXU_DTYPE), b.astype(MXU_DTYPE), dn, preferred_element_type=F32)


@functools.partial(jax.custom_vjp, nondiff_argnums=(2,))
def _dot(a, b, dn=NN):
    return _mxu(a, b, dn)


def _dot_fwd(a, b, dn):
    return _mxu(a, b, dn), (a, b)


def _dot_bwd(dn, res, g):
    a, b = res
    if dn == NN:
        return _mxu(g, b, NT), _mxu(a, g, TN)
    if dn == NT:
        return _mxu(g, b, NN), _mxu(g, a, TN)
    assert dn == TN
    return _mxu(b, g, NT), _mxu(a, g, NN)


_dot.defvjp(_dot_fwd, _dot_bwd)


def _silu(x):
    return x * jax.nn.sigmoid(x)


def _dsilu(x):
    s = jax.nn.sigmoid(x)
    return s * (1.0 + x * (1.0 - s))


def _row_tile(rows):
    return 640 if rows % 640 == 0 else 128


def _mm(a, b, *, mode, out_dtype, tm, tn, tk, name, add=None, after=None, b_k0=0, b_n0=0, n_out=None):
    if mode == "nt":
        (m, k), n = a.shape, b.shape[0]
        k2 = k if b_k0 % tk == 0 and b_k0 + k <= b.shape[1] else None
    else:
        (m, k) = a.shape if mode == "nn" else a.shape[::-1]
        k2 = b.shape[0]
        n = b.shape[1] if n_out is None else n_out
        assert b_n0 % tn == 0 and b_n0 + n <= b.shape[1]
    assert (b_k0 == 0 or mode == "nt") and ((b_n0 == 0 and n_out is None) or mode != "nt")
    assert k == k2 and m % tm == 0 and n % tn == 0 and k % tk == 0, (name, a.shape, b.shape, tm, tn, tk)
    kb0, nb0 = b_k0 // tk, b_n0 // tn
    nk = k // tk
    dn = {"nn": NN, "nt": NT, "tn": TN}[mode]
    has_add = add is not None
    n_in = 2 + has_add + (after is not None)

    def body(*refs):
        a_ref, b_ref = refs[0], refs[1]
        add_ref = refs[2] if has_add else None
        o_ref = refs[n_in]
        p = _dot(a_ref[...], b_ref[...], dn)
        if nk == 1:
            if has_add:
                p = p + add_ref[...]
            o_ref[...] = p.astype(out_dtype)
        else:
            acc_ref = refs[n_in + 1]
            kk = pl.program_id(2)

            @pl.when(kk == 0)
            def _():
                acc_ref[...] = p

            @pl.when(kk > 0)
            def _():
                acc_ref[...] += p

            @pl.when(kk == nk - 1)
            def _():
                r = acc_ref[...]
                if has_add:
                    r = r + add_ref[...]
                o_ref[...] = r.astype(out_dtype)

    if mode == "tn":
        a_spec = pl.BlockSpec((tk, tm), lambda j, i, kk: (kk, i))
    else:
        a_spec = pl.BlockSpec((tm, tk), lambda j, i, kk: (i, kk))
    if mode == "nt":
        b_spec = pl.BlockSpec((tn, tk), lambda j, i, kk: (j, kk + kb0))
    else:
        b_spec = pl.BlockSpec((tk, tn), lambda j, i, kk: (kk, j + nb0))
    o_spec = pl.BlockSpec((tm, tn), lambda j, i, kk: (i, j))
    in_specs = [a_spec, b_spec] + ([o_spec] if has_add else [])
    args = (a, b) + ((add,) if has_add else ())
    if after is not None:
        in_specs.append(pl.BlockSpec(memory_space=pl.ANY))
        args += (after,)
    return pl.pallas_call(
        body, name=name, grid=(n // tn, m // tm, nk), in_specs=in_specs, out_specs=o_spec,
        out_shape=jax.ShapeDtypeStruct((m, n), out_dtype),
        scratch_shapes=[pltpu.VMEM((tm, tn), F32)] if nk > 1 else [],
        compiler_params=_params(("parallel", "parallel", "arbitrary")),
    )(*args)


def _pick(n, cands):
    for c in cands:
        if n % c == 0:
            return c
    return n


def _rms_fwd(h, w, name):
    rows, d = h.shape
    tm = _row_tile(rows)

    def body(h_ref, w_ref, u_ref):
        x = h_ref[...]
        r = lax.rsqrt(jnp.mean(x * x, axis=-1, keepdims=True) + EPS)
        u_ref[...] = (x * r * w_ref[...]).astype(ACT_DTYPE)

    return pl.pallas_call(
        body, name=name, grid=(rows // tm,),
        in_specs=[pl.BlockSpec((tm, d), lambda i: (i, 0)), pl.BlockSpec((1, d), lambda i: (0, 0))],
        out_specs=pl.BlockSpec((tm, d), lambda i: (i, 0)),
        out_shape=jax.ShapeDtypeStruct((rows, d), ACT_DTYPE),
        compiler_params=_params(("parallel",)),
    )(h, w)


def _rms_bwd(du, h, w, dres, name):
    rows, d = h.shape
    tm = _row_tile(rows)

    def body(du_ref, h_ref, w_ref, dres_ref, dh_ref, dw_ref):
        x = h_ref[...]
        dy = du_ref[...].astype(F32)
        r = lax.rsqrt(jnp.mean(x * x, axis=-1, keepdims=True) + EPS)
        xhat = x * r
        dxn = dy * w_ref[...]
        dx = r * (dxn - xhat * jnp.mean(dxn * xhat, axis=-1, keepdims=True))
        dh_ref[...] = dres_ref[...] + dx

        @pl.when(pl.program_id(0) == 0)
        def _():
            dw_ref[...] = jnp.zeros_like(dw_ref)

        dw_ref[...] += jnp.sum(dy * xhat, axis=0, keepdims=True)

    return pl.pallas_call(
        body, name=name, grid=(rows // tm,),
        in_specs=[pl.BlockSpec((tm, d), lambda i: (i, 0)), pl.BlockSpec((tm, d), lambda i: (i, 0)),
                  pl.BlockSpec((1, d), lambda i: (0, 0)), pl.BlockSpec((tm, d), lambda i: (i, 0))],
        out_specs=[pl.BlockSpec((tm, d), lambda i: (i, 0)), pl.BlockSpec((1, d), lambda i: (0, 0))],
        out_shape=[jax.ShapeDtypeStruct((rows, d), F32), jax.ShapeDtypeStruct((1, d), F32)],
        compiler_params=_params(("arbitrary",)),
    )(du, h, w, dres)


def _loss_head(h2, tgt, w):
    rows, d = h2.shape
    tm = _row_tile(rows)

    def body(h_ref, t_ref, w_ref, dh_ref, loss_ref, dw_ref):
        i = pl.program_id(0)
        x = h_ref[...]
        r = lax.rsqrt(jnp.mean(x * x, axis=-1, keepdims=True) + EPS)
        xhat = x * r
        wv = w_ref[...]
        row = i * tm + lax.broadcasted_iota(jnp.int32, (tm, 1), 0)
        live = row >= PAD_ROWS
        diff = jnp.where(live, xhat * wv - t_ref[...], 0.0)
        dy = diff * (1.0 / d)
        dxn = dy * wv
        dh_ref[...] = r * (dxn - xhat * jnp.mean(dxn * xhat, axis=-1, keepdims=True))

        @pl.when(i == 0)
        def _():
            loss_ref[...] = jnp.zeros_like(loss_ref)
            dw_ref[...] = jnp.zeros_like(dw_ref)

        loss_ref[...] += 0.5 * jnp.sum(jnp.mean(diff * diff, axis=-1, keepdims=True))
        dw_ref[...] += jnp.sum(dy * xhat, axis=0, keepdims=True)

    return pl.pallas_call(
        body, name="loss_head", grid=(rows // tm,),
        in_specs=[pl.BlockSpec((tm, d), lambda i: (i, 0)), pl.BlockSpec((tm, d), lambda i: (i, 0)),
                  pl.BlockSpec((1, d), lambda i: (0, 0))],
        out_specs=[pl.BlockSpec((tm, d), lambda i: (i, 0)), pl.BlockSpec((8, LANES), lambda i: (0, 0)),
                   pl.BlockSpec((1, d), lambda i: (0, 0))],
        out_shape=[jax.ShapeDtypeStruct((rows, d), F32), jax.ShapeDtypeStruct((8, LANES), F32),
                   jax.ShapeDtypeStruct((1, d), F32)],
        compiler_params=_params(("arbitrary",)),
    )(h2, tgt, w)


def _prev_halo_spec(tm, width, col):
    return pl.BlockSpec((HALO, width), lambda j, i: (jnp.maximum(i * (tm // HALO) - 1, 0), col(j)))


def _next_halo_spec(tm, rows, width, col):
    last = rows // HALO - 1
    return pl.BlockSpec((HALO, width), lambda j, i: (jnp.minimum((i + 1) * (tm // HALO), last), col(j)))


def _conv_taps(cat, w_ref, b_ref, kw):
    acc = b_ref[...] + w_ref[kw - 1:kw, :] * cat
    for s in range(1, kw):
        acc = acc + w_ref[kw - 1 - s:kw - s, :] * pltpu.roll(cat, s, 0)
    return acc


def _conv_back(dpre, w_ref, kw):
    n = dpre.shape[0]
    acc = w_ref[kw - 1:kw, :] * dpre
    for s in range(1, kw):
        acc = acc + w_ref[kw - 1 - s:kw - s, :] * pltpu.roll(dpre, n - s, 0)
    return acc


def _xbc_proj_conv(u1, w_in_p, col0, w, b):
    rows, d = u1.shape
    width = w.shape[1]
    tm, tc = _row_tile(rows), 512
    cb0 = col0 // tc
    assert col0 % tc == 0

    def body(u_ref, m_ref, w_ref, b_ref, x_ref, o_ref, carry):
        i = pl.program_id(1)

        @pl.when(i == 0)
        def _():
            carry[...] = jnp.zeros_like(carry)

        xb = _mxu(u_ref[...], m_ref[...], NN).astype(ACT_DTYPE)
        x_ref[...] = xb
        x = xb.astype(F32)
        cat = jnp.concatenate([carry[...], x], axis=0)
        carry[...] = x[tm - HALO:, :]
        pre = _conv_taps(cat, w_ref, b_ref, SSD_CONV)[HALO:]
        row = i * tm + lax.broadcasted_iota(jnp.int32, (tm, 1), 0)
        o_ref[...] = jnp.where(row >= FRONT, _silu(pre), 0.0).astype(ACT_DTYPE)

    main = pl.BlockSpec((tm, tc), lambda j, i: (i, j))
    par = lambda r: pl.BlockSpec((r, tc), lambda j, i: (0, j))
    act = jax.ShapeDtypeStruct((rows, width), ACT_DTYPE)
    return pl.pallas_call(
        body, name="xbc_proj_conv", grid=(width // tc, rows // tm),
        in_specs=[pl.BlockSpec((tm, d), lambda j, i: (i, 0)), pl.BlockSpec((d, tc), lambda j, i: (0, cb0 + j)),
                  par(SSD_CONV), par(1)],
        out_specs=[main, main], out_shape=[act, act], scratch_shapes=[pltpu.VMEM((HALO, tc), F32)],
        compiler_params=_params(("parallel", "arbitrary")),
    )(u1, w_in_p, w, b)


def _ssd_conv_bwd(xbc, dxs, dbm, dcm, w, b, dproj, col0):
    rows, width = xbc.shape
    tm, tc = _row_tile(rows), 512
    kw = SSD_CONV
    nx = dxs.shape[1] // tc
    assert dbm.shape[1] == tc and dcm.shape[1] == tc and width == (nx + 2) * tc and col0 % tc == 0

    def body(x_ref, xp_ref, xn_ref, d0_ref, d0n_ref, d1_ref, d1n_ref, d2_ref, d2n_ref, w_ref, b_ref, dp_ref,
             dx_ref, dw_ref, db_ref):
        j, i = pl.program_id(0), pl.program_id(1)
        xp = jnp.where(i == 0, 0.0, xp_ref[...].astype(F32))
        cat = jnp.concatenate([xp, x_ref[...].astype(F32), xn_ref[...].astype(F32)], axis=0)
        sh = [cat] + [pltpu.roll(cat, s, 0) for s in range(1, kw)]
        pre = b_ref[...] + w_ref[kw - 1:kw, :] * sh[0]
        for s in range(1, kw):
            pre = pre + w_ref[kw - 1 - s:kw - s, :] * sh[s]
        pre = pre[HALO:]
        row = i * tm + lax.broadcasted_iota(jnp.int32, (tm + HALO, 1), 0)
        live = (row >= FRONT) & (row < rows)
        pick = lambda a, bb, c: jnp.where(j < nx, a[...], jnp.where(j == nx, bb[...], c[...])).astype(F32)
        dout = jnp.concatenate([pick(d0_ref, d1_ref, d2_ref), pick(d0n_ref, d1n_ref, d2n_ref)], axis=0)
        dpre = jnp.where(live, dout * _dsilu(pre), 0.0)
        dx_ref[...] = _conv_back(dpre, w_ref, kw)[:tm].astype(ACT_DTYPE)

        @pl.when(i == 0)
        def _():
            dw_ref[...] = jnp.zeros_like(dw_ref)
            db_ref[...] = jnp.zeros_like(db_ref)

        dmain = dpre[:tm]
        db_ref[...] += jnp.sum(dmain, axis=0, keepdims=True)
        for k in range(kw):
            dw_ref[k:k + 1, :] += jnp.sum(dmain * sh[kw - 1 - k][HALO:HALO + tm], axis=0, keepdims=True)

    main = pl.BlockSpec((tm, tc), lambda j, i: (i, j))
    par = lambda r: pl.BlockSpec((r, tc), lambda j, i: (0, j))
    col = lambda j: j
    xcol, zero = (lambda j: jnp.minimum(j, nx - 1)), (lambda j: 0)
    dspecs = lambda c: [pl.BlockSpec((tm, tc), lambda j, i: (i, c(j))), _next_halo_spec(tm, rows, tc, c)]
    return pl.pallas_call(
        body, name="ssd_conv_bwd", grid=(width // tc, rows // tm),
        in_specs=[main, _prev_halo_spec(tm, tc, col), _next_halo_spec(tm, rows, tc, col)]
        + dspecs(xcol) + dspecs(zero) + dspecs(zero) + [par(kw), par(1), pl.BlockSpec(memory_space=pl.ANY)],
        out_specs=[pl.BlockSpec((tm, tc), lambda j, i: (i, col0 // tc + j)), par(kw), par(1)],
        out_shape=[jax.ShapeDtypeStruct(dproj.shape, dproj.dtype), jax.ShapeDtypeStruct((kw, width), F32),
                   jax.ShapeDtypeStruct((1, width), F32)],
        input_output_aliases={11: 0}, compiler_params=_params(("parallel", "arbitrary")),
    )(xbc, xbc, xbc, dxs, dxs, dbm, dbm, dcm, dcm, w, b, dproj)


def _ffn_up_conv(u2, w_up, w, b):
    rows, d = u2.shape
    width = w_up.shape[1]
    dff = width // 2
    tm, tc = _row_tile(rows), _pick(dff, (256, 128))
    nb = dff // tc
    kw = FFN_CONV

    def body(u_ref, mg_ref, mv_ref, wg_ref, bg_ref, wv_ref, bv_ref, ug_ref, uv_ref, o_ref, cg, cv):
        i = pl.program_id(1)

        @pl.when(i == 0)
        def _():
            cg[...] = jnp.zeros_like(cg)
            cv[...] = jnp.zeros_like(cv)

        def pre(m_ref, up_ref, carry, w_ref, b_ref):
            upb = _mxu(u_ref[...], m_ref[...], NN).astype(ACT_DTYPE)
            up_ref[...] = upb
            x = upb.astype(F32)
            cat = jnp.concatenate([carry[...], x], axis=0)
            carry[...] = x[tm - HALO:, :]
            return _conv_taps(cat, w_ref, b_ref, kw)[HALO:]

        ag = pre(mg_ref, ug_ref, cg, wg_ref, bg_ref)
        av = pre(mv_ref, uv_ref, cv, wv_ref, bv_ref)
        o_ref[...] = (_silu(ag) * av).astype(ACT_DTYPE)

    gcol, vcol = (lambda j: j), (lambda j: j + nb)
    mat = lambda col: pl.BlockSpec((d, tc), lambda j, i: (0, col(j)))
    par = lambda r, col: pl.BlockSpec((r, tc), lambda j, i: (0, col(j)))
    out = pl.BlockSpec((tm, tc), lambda j, i: (i, j))
    act = jax.ShapeDtypeStruct((rows, dff), ACT_DTYPE)
    return pl.pallas_call(
        body, name="ffn_up_conv", grid=(nb, rows // tm),
        in_specs=[pl.BlockSpec((tm, d), lambda j, i: (i, 0)), mat(gcol), mat(vcol),
                  par(kw, gcol), par(1, gcol), par(kw, vcol), par(1, vcol)],
        out_specs=[out, out, out], out_shape=[act, act, act],
        scratch_shapes=[pltpu.VMEM((HALO, tc), F32), pltpu.VMEM((HALO, tc), F32)],
        compiler_params=_params(("parallel", "arbitrary")),
    )(u2, w_up, w_up, w, b, w, b)


def _ffn_conv_bwd(up_g, up_v, dact, w, b, u2):
    rows, dff = up_g.shape
    d = u2.shape[1]
    tm, tc = _row_tile(rows), _pick(dff, (256, 128))
    nb = dff // tc
    kw = FFN_CONV

    sb = 16

    def body(g_ref, gp_ref, gn_ref, v_ref, vp_ref, vn_ref, d_ref, dn_ref, wg_ref, bg_ref, wv_ref, bv_ref, u_ref,
             dxg_ref, dxv_ref, dwg_ref, dbg_ref, dwv_ref, dbv_ref, gwg_ref, gwv_ref, xg_s, xv_s, dd_s, og_s, ov_s):
        i = pl.program_id(1)
        last = i == rows // tm - 1
        for x_s, x_ref, xp_ref, xn_ref in ((xg_s, g_ref, gp_ref, gn_ref), (xv_s, v_ref, vp_ref, vn_ref)):
            x_s[0:HALO, :] = jnp.where(i == 0, 0.0, xp_ref[...].astype(F32))
            x_s[HALO:HALO + tm, :] = x_ref[...].astype(F32)
            x_s[HALO + tm:, :] = xn_ref[...].astype(F32)
        dd_s[0:tm, :] = d_ref[...].astype(F32)
        dd_s[tm:, :] = jnp.where(last, 0.0, dn_ref[...].astype(F32))

        wg = [wg_ref[k:k + 1, :] for k in range(kw)]
        wv = [wv_ref[k:k + 1, :] for k in range(kw)]
        bg, bv = bg_ref[...], bv_ref[...]

        def taps(x_s, e0, w, bias):
            win = x_s[pl.ds(e0 + HALO - sb, 2 * sb), :]
            sh = [win[sb:], pltpu.roll(win, 1, 0)[sb:], pltpu.roll(win, 2, 0)[sb:]]
            return bias + w[2] * sh[0] + w[1] * sh[1] + w[0] * sh[2], sh

        def dpre_of(e0):
            ag, sh_g = taps(xg_s, e0, wg, bg)
            av, sh_v = taps(xv_s, e0, wv, bv)
            dout = dd_s[pl.ds(e0, sb), :]
            s = jax.nn.sigmoid(ag)
            silu = ag * s
            return dout * av * (s + silu * (1.0 - s)), dout * silu, sh_g, sh_v

        def back(dp, nxt, w):
            cat = jnp.concatenate([dp, nxt], axis=0)
            return w[2] * dp + w[1] * pltpu.roll(cat, 2 * sb - 1, 0)[:sb] + w[0] * pltpu.roll(cat, 2 * sb - 2, 0)[:sb]

        nxt_g, nxt_v, _, _ = dpre_of(tm)
        acc_g = acc_v = tuple(jnp.zeros((sb, tc), F32) for _ in range(kw + 1))
        for e0 in range(tm - sb, -1, -sb):
            dpg, dpv, sh_g, sh_v = dpre_of(e0)
            og_s[e0:e0 + sb, :] = back(dpg, nxt_g, wg)
            ov_s[e0:e0 + sb, :] = back(dpv, nxt_v, wv)
            acc_g = tuple(a + dpg * t for a, t in zip(acc_g, (sh_g[2], sh_g[1], sh_g[0], 1.0)))
            acc_v = tuple(a + dpv * t for a, t in zip(acc_v, (sh_v[2], sh_v[1], sh_v[0], 1.0)))
            nxt_g, nxt_v = dpg, dpv

        @pl.when(i == 0)
        def _():
            for r in (dwg_ref, dbg_ref, dwv_ref, dbv_ref, gwg_ref, gwv_ref):
                r[...] = jnp.zeros_like(r)

        for acc, o_s, dx_ref, dw_ref, db_ref, gw_ref in ((acc_g, og_s, dxg_ref, dwg_ref, dbg_ref, gwg_ref),
                                                        (acc_v, ov_s, dxv_ref, dwv_ref, dbv_ref, gwv_ref)):
            dx = o_s[...].astype(ACT_DTYPE)
            dx_ref[...] = dx
            gw_ref[...] += _mxu(u_ref[...], dx, TN)
            for k in range(kw):
                dw_ref[k:k + 1, :] += jnp.sum(acc[k], axis=0, keepdims=True)
            db_ref[...] += jnp.sum(acc[kw], axis=0, keepdims=True)

    gcol, vcol = (lambda j: j), (lambda j: j + nb)
    main = lambda col: pl.BlockSpec((tm, tc), lambda j, i: (i, col(j)))
    par = lambda r, col: pl.BlockSpec((r, tc), lambda j, i: (0, col(j)))
    halos = lambda col: [_prev_halo_spec(tm, tc, col), _next_halo_spec(tm, rows, tc, col)]
    act_shape = jax.ShapeDtypeStruct((rows, dff), ACT_DTYPE)
    par_shapes = [jax.ShapeDtypeStruct((kw, dff), F32), jax.ShapeDtypeStruct((1, dff), F32)]
    gw_shape = jax.ShapeDtypeStruct((d, dff), F32)
    return pl.pallas_call(
        body, name="ffn_conv_bwd", grid=(nb, rows // tm),
        in_specs=[main(gcol)] + halos(gcol) + [main(gcol)] + halos(gcol) + [main(gcol), _next_halo_spec(tm, rows, tc, gcol),
                  par(kw, gcol), par(1, gcol), par(kw, vcol), par(1, vcol), pl.BlockSpec((tm, d), lambda j, i: (i, 0))],
        out_specs=[main(gcol), main(gcol), par(kw, gcol), par(1, gcol), par(kw, gcol), par(1, gcol), par(d, gcol), par(d, gcol)],
        out_shape=[act_shape, act_shape] + par_shapes + par_shapes + [gw_shape, gw_shape],
        scratch_shapes=[pltpu.VMEM((tm + 2 * HALO, tc), F32), pltpu.VMEM((tm + 2 * HALO, tc), F32),
                        pltpu.VMEM((tm + HALO, tc), F32), pltpu.VMEM((tm, tc), F32), pltpu.VMEM((tm, tc), F32)],
        compiler_params=_params(("parallel", "arbitrary")),
    )(up_g, up_g, up_g, up_v, up_v, up_v, dact, dact, w, b, w, b, u2)


def _ssd_scalars(dtr, dt_bias, a_log, live):
    q = CHUNK
    pre = dtr + dt_bias
    dt = jnp.where(live, jax.nn.softplus(pre), 0.0)
    a_neg = -jnp.exp(a_log)
    li = lax.broadcasted_iota(jnp.int32, (q, q), 0)
    si = lax.broadcasted_iota(jnp.int32, (q, q), 1)
    causal = li >= si
    tri = jnp.where(causal, 1.0, 0.0).astype(F32)
    a_cs = sum(_mxu(tri, p, NN) for p in _split(dt * a_neg, 3))
    return pre, dt, a_neg, a_cs, causal, tri


def _head_select():
    r = lax.broadcasted_iota(jnp.int32, (LANES, SSD_GW), 0)
    c = lax.broadcasted_iota(jnp.int32, (LANES, SSD_GW), 1)
    return jnp.where(c // SSD_HEAD_DIM == r, 1.0, 0.0).astype(MXU_DTYPE)


def _split(t, parts):
    out, rem = [], t
    for _ in range(parts):
        p = rem.astype(MXU_DTYPE)
        out.append(p)
        rem = rem - p.astype(F32)
    return out


def _stacked(ts, parts, sel, dn):
    out = _mxu(jnp.concatenate([p for t in ts for p in _split(t, parts)], axis=0), sel, dn)
    res, r0 = [], 0
    for t in ts:
        r = t.shape[0]
        res.append(sum(out[r0 + k * r:r0 + (k + 1) * r] for k in range(parts)))
        r0 += parts * r
    return res


def _head_cols(ts, sel):
    return _stacked(ts, 2, sel, NN)


def _head_sums(ts, sel):
    return _stacked(ts, 2, sel, NT)


def _half_masks():
    lane = lax.broadcasted_iota(jnp.int32, (CHUNK, LANES), 1)
    return lane < SSD_HEAD_DIM, lane >= SSD_HEAD_DIM


def _ssd_scan(xs, bm, cm, dtr, prev, dt_bias, a_log, d_skip, live):
    q = CHUNK
    sel = _head_select()
    _, dt, _, a_cs, causal, _ = _ssd_scalars(dtr, dt_bias, a_log, live)
    a_cs_t = a_cs.T
    a_end = a_cs[q - 1:q, :]
    dt_x, e_x, w_x, d_x = _head_cols([dt, jnp.exp(a_cs), jnp.exp(a_end - a_cs), jnp.broadcast_to(d_skip, (16, LANES))], sel)
    xdt = xs * dt_x
    cb = _dot(cm, bm, NT)
    y = _dot(cm, prev) * e_x + d_x[0:1] * xs
    new = prev * e_x[q - 1:q, :] + _dot(bm, xdt * w_x, TN)
    masks = _half_masks()
    ys = []
    for pp in range(SSD_HPG // 2):
        xpair = xdt[:, pp * LANES:(pp + 1) * LANES]
        acc = jnp.zeros((q, LANES), F32)
        for half in range(2):
            hh = 2 * pp + half
            decay = jnp.exp(jnp.where(causal, a_cs[:, hh:hh + 1] - a_cs_t[hh:hh + 1, :], -jnp.inf))
            acc = acc + _dot(cb * decay, jnp.where(masks[half], xpair, 0.0))
        ys.append(acc)
    return y + jnp.concatenate(ys, axis=1), new


def _ssd_gate(y, z, nw):
    yz = y * _silu(z)
    return yz * lax.rsqrt(jnp.mean(yz * yz, axis=-1, keepdims=True) + EPS) * nw


def _ssd_scan_bwd(xs, bm, cm, dtr, prev, dt_bias, a_log, d_skip, live, dy, dnew):
    q = CHUNK
    sel = _head_select()
    pre, dt, a_neg, a_cs, causal, tri = _ssd_scalars(dtr, dt_bias, a_log, live)
    a_cs_t = a_cs.T
    a_end = a_cs[q - 1:q, :]
    dt_x, e_x, w_x, d_x = _head_cols([dt, jnp.exp(a_cs), jnp.exp(a_end - a_cs), jnp.broadcast_to(d_skip, (16, LANES))], sel)
    g_x, d_x = e_x[q - 1:q, :], d_x[0:1]
    xdt = xs * dt_x
    u = xdt * w_x
    cb = _mxu(cm, bm, NT)
    cs = _mxu(cm, prev, NN)
    dye = dy * e_x
    dcm = _mxu(dye, prev, NT)
    dprev = _mxu(cm, dye, TN) + dnew * g_x
    dacs_x = dye * cs
    dbm = _mxu(u, dnew, NT)
    du = _mxu(bm, dnew, NN)
    dw_x = du * u
    dacs_x = dacs_x - dw_x
    dend_x = jnp.sum(dw_x + dnew * prev * g_x, axis=0, keepdims=True)
    dxdt = du * w_x
    lane = lax.broadcasted_iota(jnp.int32, (q, LANES), 1)
    sub = lax.broadcasted_iota(jnp.int32, (q, LANES), 0)
    dcb = jnp.zeros((q, q), F32)
    dacs = jnp.zeros((q, LANES), F32)
    dacs_t = jnp.zeros((q, LANES), F32)
    masks = _half_masks()
    dxdt_p = []
    for pp in range(SSD_HPG // 2):
        ps = slice(pp * LANES, (pp + 1) * LANES)
        acc = jnp.zeros((q, LANES), F32)
        for half in range(2):
            hh = 2 * pp + half
            decay = jnp.exp(jnp.where(causal, a_cs[:, hh:hh + 1] - a_cs_t[hh:hh + 1, :], -jnp.inf))
            m = cb * decay
            dyh = jnp.where(masks[half], dy[:, ps], 0.0)
            dm = _mxu(dyh, xdt[:, ps], NT)
            acc = acc + _mxu(m, dyh, TN)
            dcb = dcb + dm * decay
            p = dm * m
            dacs = jnp.where(lane == hh, jnp.sum(p, axis=1, keepdims=True), dacs)
            dacs_t = jnp.where(sub == hh, jnp.sum(p, axis=0, keepdims=True), dacs_t)
        dxdt_p.append(acc)
    dcm = dcm + _mxu(dcb, bm, NN)
    dbm = dbm + _mxu(dcb, cm, TN)
    dxdt = dxdt + jnp.concatenate(dxdt_p, axis=1)
    dxs = dy * d_x + dxdt * dt_x
    rows_x = jnp.concatenate([dend_x, jnp.sum(dy * xs, axis=0, keepdims=True), jnp.zeros((14, SSD_GW), F32)], axis=0)
    dacs_h, ddt_h, rows = _head_sums([dacs_x, dxdt * xs, rows_x], sel)
    dacs = dacs - dacs_t.T + dacs_h
    dacs = dacs + jnp.where(sub == q - 1, rows[0:1], 0.0)
    tri_t = jnp.where(causal, 0.0, 1.0).astype(F32) + jnp.where(lane == sub, 1.0, 0.0)
    da = jnp.dot(tri_t, dacs, precision=lax.Precision.HIGHEST, preferred_element_type=F32)
    ddt = ddt_h + da * a_neg
    dalog = jnp.sum(da * dt, axis=0, keepdims=True) * a_neg
    ddtr = jnp.where(live, ddt * jax.nn.sigmoid(pre), 0.0)
    dbias = jnp.sum(ddtr, axis=0, keepdims=True)
    return dxs, dbm, dcm, ddtr, dprev, dbias, dalog, rows[1:2]


def _chunks_per_step(nc):
    return 13 if nc % 13 == 0 else 1


def _ssd_specs(rev, nc):
    per = _chunks_per_step(nc)
    steps = nc // per
    sidx = (lambda s: steps - 1 - s) if rev else (lambda s: s)
    nb_b = SSD_D_INNER // SSD_STATE
    row = lambda width, col=lambda g: g: pl.BlockSpec((per * CHUNK, width), lambda g, s: (sidx(s), col(g)))
    par = lambda width: pl.BlockSpec((1, width), lambda g, s: (0, g))
    state = lambda: pl.BlockSpec((per, 1, SSD_STATE, SSD_GW), lambda g, s: (sidx(s), g, 0, 0))
    xbc = [row(SSD_GW), row(SSD_STATE, lambda g: nb_b + g), row(SSD_STATE, lambda g: nb_b + SSD_GROUPS + g)]
    return per, steps, sidx, row, par, state, xbc


def _ssd_fwd(xbc_c, dtr, z, dt_bias, a_log, d_skip, nw):
    rows = z.shape[0]
    nc = rows // CHUNK
    per, steps, _, row, par, state, xbc = _ssd_specs(False, nc)

    def body(xs_ref, b_ref, c_ref, dt_ref, z_ref, bias_ref, al_ref, dk_ref, nw_ref, o_ref, y_ref, st_ref, carry):
        s = pl.program_id(1)

        @pl.when(s == 0)
        def _():
            carry[...] = jnp.zeros_like(carry)

        for j in range(per):
            rs = pl.ds(j * CHUNK, CHUNK)
            live = (s * per + j) * CHUNK + lax.broadcasted_iota(jnp.int32, (CHUNK, 1), 0) >= FRONT
            prev = carry[...]
            st_ref[j, 0] = prev
            y, new = _ssd_scan(xs_ref[rs, :].astype(F32), b_ref[rs, :].astype(F32), c_ref[rs, :].astype(F32), dt_ref[rs, :],
                               prev, bias_ref[...], al_ref[...], dk_ref[...], live)
            y_ref[rs, :] = y.astype(ACT_DTYPE)
            o_ref[rs, :] = _ssd_gate(y, z_ref[rs, :].astype(F32), nw_ref[...]).astype(ACT_DTYPE)
            carry[...] = new

    act = jax.ShapeDtypeStruct((rows, SSD_D_INNER), ACT_DTYPE)
    return pl.pallas_call(
        body, name="ssd_fwd", grid=(SSD_GROUPS, steps),
        in_specs=xbc + [row(LANES), row(SSD_GW), par(LANES), par(LANES), par(LANES), par(SSD_GW)],
        out_specs=[row(SSD_GW), row(SSD_GW), state()],
        out_shape=[act, act, jax.ShapeDtypeStruct((nc, SSD_GROUPS, SSD_STATE, SSD_GW), F32)],
        scratch_shapes=[pltpu.VMEM((SSD_STATE, SSD_GW), F32)],
        compiler_params=_params(("parallel", "arbitrary")),
    )(xbc_c, xbc_c, xbc_c, dtr, z, dt_bias, a_log, d_skip, nw)


def _ssd_gate_bwd(y, z, d_bs, w_bs, nw, dproj, col0):
    rows = y.shape[0]
    d = d_bs.shape[1]
    tm = _row_tile(rows)
    assert col0 % SSD_GW == 0

    def body(y_ref, z_ref, db_ref, wb_ref, nw_ref, dp_ref, dy_ref, dz_ref, dnw_ref):
        yv, zv = y_ref[...].astype(F32), z_ref[...].astype(F32)
        dov = _mxu(db_ref[...], wb_ref[...], NT)
        s = jax.nn.sigmoid(zv)
        silu = zv * s
        yz = yv * silu
        r = lax.rsqrt(jnp.mean(yz * yz, axis=-1, keepdims=True) + EPS)
        yhat = yz * r
        dn = dov * nw_ref[...]
        dyz = r * (dn - yhat * jnp.mean(dn * yhat, axis=-1, keepdims=True))
        dy_ref[...] = (dyz * silu).astype(ACT_DTYPE)
        dz_ref[...] = (dyz * yv * (s + silu * (1.0 - s))).astype(ACT_DTYPE)

        @pl.when(pl.program_id(1) == 0)
        def _():
            dnw_ref[...] = jnp.zeros_like(dnw_ref)

        dnw_ref[...] += jnp.sum(dov * yhat, axis=0, keepdims=True)

    spec = pl.BlockSpec((tm, SSD_GW), lambda g, i: (i, g))
    par = pl.BlockSpec((1, SSD_GW), lambda g, i: (0, g))
    act = jax.ShapeDtypeStruct((rows, SSD_D_INNER), ACT_DTYPE)
    return pl.pallas_call(
        body, name="ssd_gate_bwd", grid=(SSD_GROUPS, rows // tm),
        in_specs=[spec, spec, pl.BlockSpec((tm, d), lambda g, i: (i, 0)), pl.BlockSpec((SSD_GW, d), lambda g, i: (g, 0)),
                  par, pl.BlockSpec(memory_space=pl.ANY)],
        out_specs=[spec, pl.BlockSpec((tm, SSD_GW), lambda g, i: (i, col0 // SSD_GW + g)), par],
        out_shape=[act, jax.ShapeDtypeStruct(dproj.shape, dproj.dtype), jax.ShapeDtypeStruct((1, SSD_D_INNER), F32)],
        input_output_aliases={5: 1}, compiler_params=_params(("parallel", "arbitrary")),
    )(y, z, d_bs, w_bs, nw, dproj)


def _ssd_bwd(xbc_c, dtr, dt_bias, a_log, d_skip, states, dy, dproj, col0):
    rows = dy.shape[0]
    nc = rows // CHUNK
    per, steps, sidx, row, par, state, xbc = _ssd_specs(True, nc)
    assert col0 % LANES == 0

    def body(xs_ref, b_ref, c_ref, dt_ref, bias_ref, al_ref, dk_ref, st_ref, dy_ref, dp_ref,
             dxs_ref, db_ref, dc_ref, ddt_ref, dbias_ref, dal_ref, ddk_ref, carry):
        s = pl.program_id(1)

        @pl.when(s == 0)
        def _():
            carry[...] = jnp.zeros_like(carry)
            for r in (dbias_ref, dal_ref, ddk_ref):
                r[...] = jnp.zeros_like(r)

        for j in reversed(range(per)):
            rs = pl.ds(j * CHUNK, CHUNK)
            live = (sidx(s) * per + j) * CHUNK + lax.broadcasted_iota(jnp.int32, (CHUNK, 1), 0) >= FRONT
            dxs, dbm, dcm, ddt, dprev, dbias, dal, ddk = _ssd_scan_bwd(
                xs_ref[rs, :].astype(F32), b_ref[rs, :].astype(F32), c_ref[rs, :].astype(F32), dt_ref[rs, :], st_ref[j, 0],
                bias_ref[...], al_ref[...], dk_ref[...], live, dy_ref[rs, :].astype(F32), carry[...])
            dxs_ref[rs, :] = dxs.astype(ACT_DTYPE)
            db_ref[rs, :] = dbm.astype(ACT_DTYPE)
            dc_ref[rs, :] = dcm.astype(ACT_DTYPE)
            ddt_ref[rs, :] = ddt.astype(ACT_DTYPE)
            carry[...] = dprev
            dbias_ref[...] += dbias
            dal_ref[...] += dal
            ddk_ref[...] += ddk

    bc = jax.ShapeDtypeStruct((rows, SSD_GROUPS * SSD_STATE), ACT_DTYPE)
    head = jax.ShapeDtypeStruct((1, SSD_GROUPS * LANES), F32)
    return pl.pallas_call(
        body, name="ssd_bwd", grid=(SSD_GROUPS, steps),
        in_specs=xbc + [row(LANES), par(LANES), par(LANES), par(LANES), state(), row(SSD_GW), pl.BlockSpec(memory_space=pl.ANY)],
        out_specs=[row(SSD_GW), row(SSD_STATE), row(SSD_STATE), row(LANES, lambda g: col0 // LANES + g),
                   par(LANES), par(LANES), par(LANES)],
        out_shape=[jax.ShapeDtypeStruct((rows, SSD_D_INNER), ACT_DTYPE), bc, bc,
                   jax.ShapeDtypeStruct(dproj.shape, dproj.dtype), head, head, head],
        input_output_aliases={9: 3}, scratch_shapes=[pltpu.VMEM((SSD_STATE, SSD_GW), F32)],
        compiler_params=_params(("parallel", "arbitrary")),
    )(xbc_c, xbc_c, xbc_c, dtr, dt_bias, a_log, d_skip, states, dy, dproj)


def _rotary_tables(rows):
    pos = np.arange(rows, dtype=np.float32) - np.float32(FRONT)
    inv_freq = np.float32(ROPE_BASE) ** (-np.linspace(0.0, 1.0, RET_QK // 2, dtype=np.float32))
    ang = (pos[:, None] * inv_freq[None, :]).astype(np.float32).astype(np.float64)
    lgam = np.log(1.0 - 2.0 ** (-5.0 - np.arange(RET_HEADS, dtype=np.float64))).astype(np.float32)
    lgam = np.broadcast_to(lgam[:, None, None], (RET_HEADS, 8, LANES))
    return jnp.asarray(np.cos(ang).astype(np.float32)), jnp.asarray(np.sin(ang).astype(np.float32)), jnp.asarray(lgam)


def _rotary(t, cos, sin):
    half = t.shape[-1] // 2
    t1, t2 = t[:, :half], t[:, half:]
    return jnp.concatenate([t1 * cos - t2 * sin, t2 * cos + t1 * sin], axis=1)


def _ret_chunk(qh, kh, vh, gh, prev, cos, sin, lg):
    q = CHUNK
    qr = _rotary(qh, cos, sin)
    kr = _rotary(kh, cos, sin) * (RET_QK ** -0.5)
    li = lax.broadcasted_iota(jnp.int32, (q, q), 0)
    si = lax.broadcasted_iota(jnp.int32, (q, q), 1)
    dist = (li - si).astype(F32)
    decay = jnp.exp(jnp.where(li >= si, dist * lg, -jnp.inf))
    idx = lax.broadcasted_iota(jnp.int32, (q, 1), 0).astype(F32)
    scores = _dot(qr, kr, NT) * decay
    out = _dot(scores, vh)
    kv = _dot(kr * jnp.exp((q - 1.0 - idx) * lg), vh, TN)
    out = out + _dot(qr, prev) * jnp.exp((idx + 1.0) * lg)
    new = prev * jnp.exp(q * lg) + kv
    out = out * lax.rsqrt(jnp.mean(out * out, axis=-1, keepdims=True) + EPS)
    return _silu(gh) * out, new


def _ret_specs(rev, nc):
    per = _chunks_per_step(nc)
    steps = nc // per
    sidx = (lambda s: steps - 1 - s) if rev else (lambda s: s)
    row = lambda width: pl.BlockSpec((per * CHUNK, width), lambda h, s: (sidx(s), h))
    tab = lambda: pl.BlockSpec((per * CHUNK, RET_QK // 2), lambda h, s: (sidx(s), 0))
    lgs = lambda: pl.BlockSpec((1, 8, LANES), lambda h, s: (h, 0, 0))
    state = lambda: pl.BlockSpec((per, 1, RET_QK, RET_V), lambda h, s: (sidx(s), h, 0, 0))
    part = lambda width, k: pl.BlockSpec((per * CHUNK, width), lambda h, s: (sidx(s), h * (RET_HW // width) + k))
    ins = [part(RET_QK, 0), part(RET_QK, 1), part(RET_V, 1), part(RET_V, 2), tab(), tab(), lgs()]
    return per, steps, sidx, row, state, ins


def _ret_fwd(qkvg, cos, sin, lgam):
    rows = qkvg.shape[0]
    nc = rows // CHUNK
    per, steps, _, row, state, ins = _ret_specs(False, nc)
    q = k = v = g = qkvg

    def body(q_ref, k_ref, v_ref, g_ref, cos_ref, sin_ref, lg_ref, y_ref, st_ref, carry):
        @pl.when(pl.program_id(1) == 0)
        def _():
            carry[...] = jnp.zeros_like(carry)

        for j in range(per):
            rs = pl.ds(j * CHUNK, CHUNK)
            prev = carry[...]
            st_ref[j, 0] = prev.astype(ACT_DTYPE)
            out, new = _ret_chunk(q_ref[rs, :].astype(F32), k_ref[rs, :].astype(F32), v_ref[rs, :].astype(F32),
                                  g_ref[rs, :].astype(F32), prev, cos_ref[rs, :], sin_ref[rs, :], lg_ref[0, 0:1, 0:1])
            y_ref[rs, :] = out.astype(ACT_DTYPE)
            carry[...] = new

    return pl.pallas_call(
        body, name="ret_fwd", grid=(RET_HEADS, steps), in_specs=ins, out_specs=[row(RET_V), state()],
        out_shape=[jax.ShapeDtypeStruct((rows, RET_HEADS * RET_V), ACT_DTYPE),
                   jax.ShapeDtypeStruct((nc, RET_HEADS, RET_QK, RET_V), ACT_DTYPE)],
        scratch_shapes=[pltpu.VMEM((RET_QK, RET_V), F32)],
        compiler_params=_params(("parallel", "arbitrary")),
    )(q, k, v, g, cos, sin, lgam)


def _ret_bwd(qkvg, cos, sin, lgam, states, dy, dproj):
    rows = qkvg.shape[0]
    nc = rows // CHUNK
    per, steps, sidx, row, state, ins = _ret_specs(True, nc)

    def body(q_ref, k_ref, v_ref, g_ref, cos_ref, sin_ref, lg_ref, st_ref, dy_ref, dp_ref, o_ref, carry):
        @pl.when(pl.program_id(1) == 0)
        def _():
            carry[...] = jnp.zeros_like(carry)

        for j in reversed(range(per)):
            rs = pl.ds(j * CHUNK, CHUNK)
            fn = functools.partial(_ret_chunk, cos=cos_ref[rs, :], sin=sin_ref[rs, :], lg=lg_ref[0, 0:1, 0:1])
            _, vjp = jax.vjp(fn, q_ref[rs, :].astype(F32), k_ref[rs, :].astype(F32), v_ref[rs, :].astype(F32),
                             g_ref[rs, :].astype(F32), st_ref[j, 0].astype(F32))
            dq, dk, dv, dg, dprev = vjp((dy_ref[rs, :].astype(F32), carry[...]))
            o_ref[rs, 0:RET_QK] = dq.astype(ACT_DTYPE)
            o_ref[rs, RET_QK:2 * RET_QK] = dk.astype(ACT_DTYPE)
            o_ref[rs, 2 * RET_QK:2 * RET_QK + RET_V] = dv.astype(ACT_DTYPE)
            o_ref[rs, 2 * RET_QK + RET_V:RET_HW] = dg.astype(ACT_DTYPE)
            carry[...] = dprev

    return pl.pallas_call(
        body, name="ret_bwd", grid=(RET_HEADS, steps),
        in_specs=ins + [state(), row(RET_V), pl.BlockSpec(memory_space=pl.ANY)],
        out_specs=pl.BlockSpec((per * CHUNK, RET_HW), lambda h, s: (sidx(s), h)),
        out_shape=jax.ShapeDtypeStruct(dproj.shape, dproj.dtype), input_output_aliases={9: 0},
        scratch_shapes=[pltpu.VMEM((RET_QK, RET_V), F32)],
        compiler_params=_params(("parallel", "arbitrary")),
    )(qkvg, qkvg, qkvg, qkvg, cos, sin, lgam, states, dy, dproj)


def _branches_merge(y_ssd, y_ret, w_bs, w_br, gates):
    rows, k = y_ssd.shape
    d = w_bs.shape[1]
    tm = _row_tile(rows)

    def body(ys_ref, yr_ref, ws_ref, wr_ref, gs_ref, gr_ref, bs_ref, br_ref, o_ref):
        bs = _mxu(ys_ref[...], ws_ref[...], NN).astype(ACT_DTYPE)
        br = _mxu(yr_ref[...], wr_ref[...], NN).astype(ACT_DTYPE)
        bs_ref[...] = bs
        br_ref[...] = br
        o_ref[...] = (jax.nn.sigmoid(gs_ref[...].astype(F32)) * bs.astype(F32)
                      + jax.nn.sigmoid(gr_ref[...].astype(F32)) * br.astype(F32)).astype(ACT_DTYPE)

    spec = pl.BlockSpec((tm, d), lambda i: (i, 0))
    yspec = pl.BlockSpec((tm, k), lambda i: (i, 0))
    wspec = pl.BlockSpec((k, d), lambda i: (0, 0))
    shp = jax.ShapeDtypeStruct((rows, d), ACT_DTYPE)
    return pl.pallas_call(
        body, name="branches_merge", grid=(rows // tm,),
        in_specs=[yspec, yspec, wspec, wspec, spec, pl.BlockSpec((tm, d), lambda i: (i, 1))],
        out_specs=[spec, spec, spec], out_shape=[shp, shp, shp], compiler_params=_params(("parallel",)),
    )(y_ssd, y_ret, w_bs, w_br, gates, gates)


def _merge_bwd(dh1, w_o, bs, br, gates, dproj, col0):
    rows, d = bs.shape
    tm = _row_tile(rows)
    assert col0 % (2 * d) == 0

    def body(dh_ref, wo_ref, bs_ref, br_ref, gs_ref, gr_ref, dp_ref, dbs_ref, dbr_ref, dg_ref):
        dmv = _mxu(dh_ref[...], wo_ref[...], NT)
        for k, (b_ref, g_ref, db_ref) in enumerate(((bs_ref, gs_ref, dbs_ref), (br_ref, gr_ref, dbr_ref))):
            s = jax.nn.sigmoid(g_ref[...].astype(F32))
            db_ref[...] = (dmv * s).astype(ACT_DTYPE)
            dg_ref[:, k * d:(k + 1) * d] = (dmv * b_ref[...].astype(F32) * s * (1.0 - s)).astype(ACT_DTYPE)

    spec = pl.BlockSpec((tm, d), lambda i: (i, 0))
    shp = jax.ShapeDtypeStruct((rows, d), ACT_DTYPE)
    return pl.pallas_call(
        body, name="merge_bwd", grid=(rows // tm,),
        in_specs=[spec, pl.BlockSpec(w_o.shape, lambda i: (0, 0)), spec, spec, spec, pl.BlockSpec((tm, d), lambda i: (i, 1)),
                  pl.BlockSpec(memory_space=pl.ANY)],
        out_specs=[spec, spec, pl.BlockSpec((tm, 2 * d), lambda i: (i, col0 // (2 * d)))],
        out_shape=[shp, shp, jax.ShapeDtypeStruct(dproj.shape, dproj.dtype)], input_output_aliases={6: 2},
        compiler_params=_params(("parallel",)),
    )(dh1, w_o, bs, br, gates, gates, dproj)


def _place():
    x, y, c = lax.axis_index("x"), lax.axis_index("y"), lax.axis_index("c")
    return x, y, c


def _slot(p):
    return 4 * p[0] + 2 * p[1] + p[2]


def _allgather(arrs, name):
    n = len(arrs)
    any_spec = pl.BlockSpec(memory_space=pl.ANY)

    def body(*refs):
        ins, outs = refs[:n], refs[n:2 * n]
        send_sems, recv_sems, local_sems = refs[2 * n:]
        x, y, c = _place()
        me, sibling = (x, y, c), (x, y, 1 - c)
        chips = [(1 - x, y), (x, 1 - y), (1 - x, 1 - y)]

        def copy(a, k, block, to, src=None):
            dst = outs[a].at[_slot(block)]
            return pltpu.make_async_remote_copy(
                src_ref=dst if src is None else src, dst_ref=dst, send_sem=send_sems.at[a * 7 + k],
                recv_sem=recv_sems.at[a * 7 + k], device_id=to, device_id_type=MESH)

        mine, first, passed = [], [], []
        for a in range(n):
            cp = pltpu.make_async_copy(ins[a], outs[a].at[_slot(me)], local_sems.at[a])
            cp.start()
            mine.append(cp)
            first.append(copy(a, 0, me, sibling, src=ins[a]))
            first += [copy(a, 1 + j, me, (*chip, c), src=ins[a]) for j, chip in enumerate(chips)]
        for cp in first:
            cp.start()
        for j, chip in enumerate(chips):
            for a in range(n):
                copy(a, 1 + j, (*chip, c), me).wait_recv()
                cp = copy(a, 4 + j, (*chip, c), sibling)
                cp.start()
                passed.append(cp)
        for a in range(n):
            copy(a, 0, sibling, me).wait_recv()
            for j, chip in enumerate(chips):
                copy(a, 4 + j, (*chip, 1 - c), me).wait_recv()
        for cp in first + passed:
            cp.wait_send()
        for cp in mine:
            cp.wait()

    return pl.pallas_call(
        body, name=name, in_specs=[any_spec] * n, out_specs=[any_spec] * n,
        out_shape=[jax.ShapeDtypeStruct((N_DEV,) + a.shape, a.dtype) for a in arrs],
        scratch_shapes=[pltpu.SemaphoreType.DMA((7 * n,)), pltpu.SemaphoreType.DMA((7 * n,)), pltpu.SemaphoreType.DMA((n,))],
    )(*arrs)


def _peers():
    x, y, c = _place()
    return (x, y, c), [(x ^ dx, y ^ dy, c ^ dc) for dx in (0, 1) for dy in (0, 1) for dc in (0, 1)][1:]


def _exchange_copies(srcs, lands, send_sems, recv_sems, scatter, sender):
    me, peers = _peers()
    out = []
    for a, (src, land) in enumerate(zip(srcs, lands, strict=True)):
        for k, peer in enumerate(peers):
            src_ref = src.at[_slot(peer)] if scatter else src
            out.append(pltpu.make_async_remote_copy(
                src_ref=src_ref, dst_ref=land.at[_slot(me if sender else peer)], send_sem=send_sems.at[a * 7 + k],
                recv_sem=recv_sems.at[a * 7 + k], device_id=peer, device_id_type=MESH))
    return out


_HBM = pl.BlockSpec(memory_space=pltpu.HBM)
_SEM = pl.BlockSpec(memory_space=pltpu.SEMAPHORE)
_EFFECT = pltpu.SideEffectType.DATAFLOW_SIDE_EFFECTING


def _exchange_start(srcs, scatter, name, after=None):
    n = len(srcs)
    land_shapes = [s.shape if scatter else (N_DEV,) + s.shape for s in srcs]
    n_in = 2 * n + (after is not None)

    def body(*refs):
        for cp in _exchange_copies(refs[:n], refs[n:2 * n], refs[n_in], refs[n_in + 1], scatter, True):
            cp.start()
        refs[-1][...] = jnp.zeros_like(refs[-1])

    args = [pltpu.with_memory_space_constraint(s, pltpu.HBM) for s in srcs]
    args += [pltpu.with_memory_space_constraint(lax.empty(shp, s.dtype), pltpu.HBM) for s, shp in zip(srcs, land_shapes)]
    thru_shapes = tuple(pltpu.HBM(a.shape, a.dtype) for a in args)
    extra = [] if after is None else [after]
    outs = pl.pallas_call(
        body, name=name,
        out_shape=(pltpu.SemaphoreType.DMA((7 * n,)), pltpu.SemaphoreType.DMA((7 * n,))) + thru_shapes
        + (jax.ShapeDtypeStruct((8, LANES), F32),),
        in_specs=[_HBM] * (2 * n) + [pl.BlockSpec(memory_space=pl.ANY)] * len(extra),
        out_specs=(_SEM, _SEM) + (_HBM,) * (2 * n) + (pl.BlockSpec(memory_space=pltpu.VMEM),),
        input_output_aliases={i: 2 + i for i in range(2 * n)},
        compiler_params=pltpu.CompilerParams(has_side_effects=_EFFECT),
    )(*args, *extra)
    return outs[:-1], outs[-1]


def _exchange_wait(handle, scatter, after, name):
    n = (len(handle) - 2) // 2
    thru = handle[2:]

    def body(*refs):
        for cp in _exchange_copies(refs[:n], refs[n:2 * n], refs[2 * n], refs[2 * n + 1], scatter, False):
            cp.wait_send()
            cp.wait_recv()

    outs = pl.pallas_call(
        body, name=name, out_shape=tuple(pltpu.HBM(t.shape, t.dtype) for t in thru),
        in_specs=[_HBM] * (2 * n) + [_SEM, _SEM, pl.BlockSpec(memory_space=pl.ANY)], out_specs=(_HBM,) * (2 * n),
        input_output_aliases={i: i for i in range(2 * n)},
        compiler_params=pltpu.CompilerParams(has_side_effects=_EFFECT),
    )(*thru, handle[0], handle[1], after)
    return list(outs[:n]), list(outs[n:])


def _allreduce_small(pack):
    rows, lanes = pack.shape

    def body(x_ref, o_ref, buf, send_sems, recv_sems):
        x, y, c = _place()
        me, sibling = (x, y, c), (x, y, 1 - c)
        chips = [(1 - x, y), (x, 1 - y), (1 - x, 1 - y)]

        def copy(k, block, to, src=None):
            dst = buf.at[_slot(block)]
            return pltpu.make_async_remote_copy(
                src_ref=dst if src is None else src, dst_ref=dst, send_sem=send_sems.at[k], recv_sem=recv_sems.at[k],
                device_id=to, device_id_type=MESH)

        buf[_slot(me)] = x_ref[...]
        first = [copy(0, me, sibling, src=x_ref)]
        first += [copy(1 + j, me, (*chip, c), src=x_ref) for j, chip in enumerate(chips)]
        for cp in first:
            cp.start()
        passed = [copy(4 + j, (*chip, c), sibling) for j, chip in enumerate(chips)]
        for j, chip in enumerate(chips):
            copy(1 + j, (*chip, c), me).wait_recv()
            passed[j].start()
        copy(0, sibling, me).wait_recv()
        for j, chip in enumerate(chips):
            copy(4 + j, (*chip, 1 - c), me).wait_recv()
        for cp in first + passed:
            cp.wait_send()
        acc = buf[0]
        for i in range(1, N_DEV):
            acc = acc + buf[i]
        o_ref[...] = acc

    vmem = pl.BlockSpec(memory_space=pltpu.VMEM)
    return pl.pallas_call(
        body, name="allreduce_small", in_specs=[vmem], out_specs=vmem,
        out_shape=jax.ShapeDtypeStruct((rows, lanes), F32),
        scratch_shapes=[pltpu.VMEM((N_DEV, rows, lanes), F32), pltpu.SemaphoreType.DMA((7,)), pltpu.SemaphoreType.DMA((7,))],
        compiler_params=pltpu.CompilerParams(vmem_limit_bytes=VMEM_LIMIT),
    )(pack)


def _adamw(w, g, m, v):
    m = ADAM_B1 * m + (1.0 - ADAM_B1) * g
    v = ADAM_B2 * v + (1.0 - ADAM_B2) * jnp.square(g)
    m_hat = m / (1.0 - ADAM_B1 ** ADAM_STEP)
    v_hat = v / (1.0 - ADAM_B2 ** ADAM_STEP)
    delta = -ADAM_LR * (m_hat / (jnp.sqrt(v_hat) + ADAM_EPS) + ADAM_WD * w)
    return delta, m, v


def _adam_shard(own, parts, w, m, v, name):
    r, c = w.shape
    tr = _pick(r, (128, 64, 32, 16, 8))

    def body(own_ref, p_ref, w_ref, m_ref, v_ref, g_ref, d_ref, nm_ref, nv_ref):
        _, peers = _peers()
        g = own_ref[...].astype(F32)
        for peer in peers:
            g = g + p_ref[_slot(peer)].astype(F32)
        g_ref[...] = g
        d_ref[...], nm_ref[...], nv_ref[...] = _adamw(w_ref[...], g, m_ref[...], v_ref[...])

    spec = pl.BlockSpec((tr, c), lambda i: (i, 0))
    shp = jax.ShapeDtypeStruct((r, c), F32)
    return pl.pallas_call(
        body, name=name, grid=(r // tr,),
        in_specs=[spec, pl.BlockSpec((N_DEV, tr, c), lambda i: (0, i, 0)), spec, spec, spec], out_specs=[spec] * 4,
        out_shape=[shp] * 4, compiler_params=_params(("parallel",)),
    )(own, parts, w, m, v)


def _adam_small(w, g, m, v):
    r, c = w.shape

    def body(w_ref, g_ref, m_ref, v_ref, d_ref, nm_ref, nv_ref):
        d_ref[...], nm_ref[...], nv_ref[...] = _adamw(w_ref[...], g_ref[...], m_ref[...], v_ref[...])

    shp = jax.ShapeDtypeStruct((r, c), F32)
    return pl.pallas_call(body, name="adam_small", out_shape=[shp] * 3)(w, g, m, v)


def _column_plan(pieces, shard_w):
    plan = []
    for c0, width, d0 in pieces:
        c = c0
        while c < c0 + width:
            s, a = divmod(c, shard_w)
            w = min(c0 + width - c, shard_w - a)
            plan.append((s, a, w, d0 + c - c0))
            c += w
    return plan


def _cols_from_shards(g, plan, out_w, zero, name):
    _, r, sw = g.shape
    tr = _pick(r, (128,))

    def body(x_ref, o_ref):
        for d0, w in zero:
            o_ref[:, d0:d0 + w] = jnp.zeros((tr, w), g.dtype)
        for s, a, w, d0 in plan:
            o_ref[:, d0:d0 + w] = x_ref[s, :, a:a + w]

    return pl.pallas_call(
        body, name=name, grid=(r // tr,), in_specs=[pl.BlockSpec((N_DEV, tr, sw), lambda i: (0, i, 0))],
        out_specs=pl.BlockSpec((tr, out_w), lambda i: (i, 0)), out_shape=jax.ShapeDtypeStruct((r, out_w), g.dtype),
        compiler_params=_params(("parallel",)),
    )(g)


def _shards_from_cols(srcs, plans, shard_w, name):
    r = srcs[0].shape[0]
    tr = _pick(r, (128,))
    n = len(srcs)

    def body(*refs):
        o_ref = refs[n]
        for x_ref, plan in zip(refs[:n], plans, strict=True):
            for s, a, w, d0 in plan:
                o_ref[s, :, a:a + w] = x_ref[:, d0:d0 + w].astype(COMM_DTYPE)

    return pl.pallas_call(
        body, name=name, grid=(r // tr,), in_specs=[pl.BlockSpec((tr, t.shape[1]), lambda i: (i, 0)) for t in srcs],
        out_specs=pl.BlockSpec((N_DEV, tr, shard_w), lambda i: (0, i, 0)),
        out_shape=jax.ShapeDtypeStruct((N_DEV, r, shard_w), COMM_DTYPE), compiler_params=_params(("parallel",)),
    )(*srcs)


def _pack(arrs):
    rows = []
    for a in arrs:
        flat = a.reshape(-1).astype(F32)
        rows.append(jnp.pad(flat, (0, (-flat.shape[0]) % (8 * LANES))).reshape(-1, LANES))
    return jnp.concatenate(rows, axis=0)


def _unpack(pack, shapes):
    out, r = [], 0
    for s in shapes:
        size = math.prod(s)
        nr = -(-size // (8 * LANES)) * 8
        out.append(pack[r:r + nr].reshape(-1)[:size].reshape(s))
        r += nr
    return out


def _group_lanes(t):
    lead = t.shape[:-1]
    t = t.reshape(lead + (SSD_GROUPS, SSD_HPG))
    t = jnp.pad(t, [(0, 0)] * len(lead) + [(0, 0), (0, LANES - SSD_HPG)])
    return t.reshape(lead + (SSD_GROUPS * LANES,))


def _ungroup_lanes(t):
    lead = t.shape[:-1]
    return t.reshape(lead + (SSD_GROUPS, LANES))[..., :SSD_HPG].reshape(lead + (SSD_HEADS,))


def kernel(x, meta_tokens, mix_norm_w, w_in, ssd_conv_w, ssd_conv_b, ssd_dt_bias, ssd_A_log, ssd_D, ssd_norm_w, w_branch_ssd, w_branch_ret, w_out, ffn_norm_w, w_up, ffn_conv_w, ffn_conv_b, w_down, final_norm_w, loss_target, m_meta_tokens, m_mix_norm_w, m_w_in, m_ssd_conv_w, m_ssd_conv_b, m_ssd_dt_bias, m_ssd_A_log, m_ssd_D, m_ssd_norm_w, m_w_branch_ssd, m_w_branch_ret, m_w_out, m_ffn_norm_w, m_w_up, m_ffn_conv_w, m_ffn_conv_b, m_w_down, m_final_norm_w, v_meta_tokens, v_mix_norm_w, v_w_in, v_ssd_conv_w, v_ssd_conv_b, v_ssd_dt_bias, v_ssd_A_log, v_ssd_D, v_ssd_norm_w, v_w_branch_ssd, v_w_branch_ret, v_w_out, v_ffn_norm_w, v_w_up, v_ffn_conv_w, v_ffn_conv_b, v_w_down, v_final_norm_w):
    seq, d = x.shape[1], x.shape[2]
    rows = seq + PAD_ROWS
    tm = _row_tile(rows)
    me = _slot(_place())
    d_ff = w_down.shape[1] * N_DEV

    big = [w_in[0], w_branch_ssd[0], w_branch_ret[0], w_out[0], w_up[0], w_down[0]]
    first = _allgather([w_in[0].astype(COMM_DTYPE), meta_tokens, ssd_conv_w[0], ffn_conv_w[0]], "gather_first")
    rest_src = [b.astype(COMM_DTYPE) for b in big[1:]]
    rest_handle, rest_token = _exchange_start(rest_src, False, "gather_rest_start", after=first[0])
    cols = lambda t: jnp.transpose(t, (1, 0, 2)).reshape(t.shape[1], -1)
    rws = lambda t: t.reshape(-1, t.shape[2])
    conv_w, fconv_w = cols(first[2]), cols(first[3])
    meta_full = cols(first[1]) + rest_token[0, 0]
    widths = [SSD_D_INNER, SSD_CONV_DIM, SSD_HEADS, RET_HEADS * RET_QK, RET_HEADS * RET_QK, RET_HEADS * RET_V,
              RET_HEADS * RET_V, d, d]
    offs = [0]
    for wd in widths:
        offs.append(offs[-1] + wd)
    r0, z0 = 0, RET_HEADS * RET_HW
    g0 = z0 + widths[0]
    x0 = g0 + 2 * d
    dt0 = x0 + widths[1]
    in_p = dt0 + SSD_GROUPS * LANES
    pieces = []
    for hd in range(RET_HEADS):
        base = r0 + hd * RET_HW
        pieces += [(offs[3] + hd * RET_QK, RET_QK, base), (offs[4] + hd * RET_QK, RET_QK, base + RET_QK),
                   (offs[5] + hd * RET_V, RET_V, base + 2 * RET_QK), (offs[6] + hd * RET_V, RET_V, base + 2 * RET_QK + RET_V)]
    pieces += [(offs[0], widths[0], z0), (offs[7], d, g0), (offs[8], d, g0 + d), (offs[1], widths[1], x0)]
    pieces += [(offs[2] + SSD_HPG * grp, SSD_HPG, dt0 + LANES * grp) for grp in range(SSD_GROUPS)]
    in_plan = _column_plan(pieces, w_in.shape[2])
    w_in_p = _cols_from_shards(first[0], in_plan, in_p, [(dt0, SSD_GROUPS * LANES)], "w_in_columns")

    h0 = jnp.concatenate([jnp.zeros((FRONT, d), F32), meta_full, x[0]], axis=0)
    u1 = _rms_fwd(h0, mix_norm_w, "rms1")
    in_proj = lambda c0, width, dtype, nm: _mm(u1, w_in_p, mode="nn", out_dtype=dtype, tm=tm, tk=d, name="in_proj_" + nm,
                                               tn=_pick(width, (1024, 512)), b_n0=c0, n_out=width)
    qkvg = in_proj(r0, RET_HEADS * RET_HW, ACT_DTYPE, "qkvg")
    z = in_proj(z0, widths[0], ACT_DTYPE, "z")
    gates = in_proj(g0, 2 * d, ACT_DTYPE, "gates")
    dtr = in_proj(dt0, SSD_GROUPS * LANES, F32, "dt")
    xbc, xbc_c = _xbc_proj_conv(u1, w_in_p, x0, conv_w, ssd_conv_b)
    bias_p, alog_p, dsk_p = _group_lanes(ssd_dt_bias), _group_lanes(ssd_A_log), _group_lanes(ssd_D)
    y_ssd, y_scan, ssd_states = _ssd_fwd(xbc_c, dtr, z, bias_p, alog_p, dsk_p, ssd_norm_w)
    cos, sin, lgam = _rotary_tables(rows)
    y_ret, ret_states = _ret_fwd(qkvg, cos, sin, lgam)
    rest_own, rest = _exchange_wait(rest_handle, False, y_ret, "gather_rest_wait")
    rest = [lax.dynamic_update_index_in_dim(land, own, me, 0) for land, own in zip(rest, rest_own, strict=True)]
    w_bs, w_br, w_o, w_dn = rws(rest[0]), rws(rest[1]), rws(rest[2]), rws(rest[4])
    w_up_f = _cols_from_shards(rest[3], _column_plan([(0, 2 * d_ff, 0)], w_up.shape[2]), 2 * d_ff, [], "w_up_columns")
    bs, br, merged = _branches_merge(y_ssd, y_ret, w_bs, w_br, gates)
    h1 = _mm(merged, w_o, mode="nn", out_dtype=F32, tm=tm, tn=d, tk=d, name="out_proj", add=h0)
    u2 = _rms_fwd(h1, ffn_norm_w, "rms2")
    up_g, up_v, act = _ffn_up_conv(u2, w_up_f, fconv_w, ffn_conv_b)
    h2 = _mm(act, w_dn, mode="nn", out_dtype=F32, tm=tm, tn=d, tk=d_ff, name="ffn_down", add=h1)
    tgt = jnp.pad(loss_target[0], ((PAD_ROWS, 0), (0, 0)))
    dh2, loss_acc, g_final = _loss_head(h2, tgt, final_norm_w.reshape(1, d))

    tff = _pick(d_ff, (1408, 256))
    tkr = _pick(rows, (1664, 128))
    tkr2 = _pick(rows, (4160, 128))
    rparts = lambda t: t.reshape(N_DEV, -1, t.shape[1])
    d_act = _mm(dh2, w_dn, mode="nt", out_dtype=ACT_DTYPE, tm=tm, tn=tff, tk=d, name="d_act")
    g_w_dn = _mm(act, dh2, mode="tn", out_dtype=COMM_DTYPE, tm=tff, tn=d, tk=tkr, name="g_w_down")
    c_dn = [rparts(g_w_dn)]
    h_dn, t_dn = _exchange_start(c_dn, True, "scatter_down_start")
    d_up_g, d_up_v, g_fcw_g, g_fcb_g, g_fcw_v, g_fcb_v, g_w_up_g, g_w_up_v = _ffn_conv_bwd(
        up_g, up_v, d_act, fconv_w, ffn_conv_b + t_dn[0, 0], u2)
    g_fconv_w = jnp.concatenate([g_fcw_g, g_fcw_v], axis=1)
    g_fconv_b = jnp.concatenate([g_fcb_g, g_fcb_v], axis=1)
    c_up = [_shards_from_cols([g_w_up_g, g_w_up_v], [_column_plan([(0, d_ff, 0)], w_up.shape[2]),
                                                     _column_plan([(d_ff, d_ff, 0)], w_up.shape[2])], w_up.shape[2], "g_w_up_shards")]
    h_up, t_up = _exchange_start(c_up, True, "scatter_up_start")
    du2 = _mm(d_up_g, w_up_f, mode="nt", out_dtype=F32, tm=tm, tn=d, tk=d_ff, name="d_u2_gate", after=t_up)
    du2 = _mm(d_up_v, w_up_f, mode="nt", out_dtype=F32, tm=tm, tn=d, tk=d_ff, name="d_u2_value", add=du2, b_k0=d_ff)
    dh1, g_ffn_norm = _rms_bwd(du2, h1, ffn_norm_w, dh2, "rms2_bwd")
    g_w_o =_mm(merged, dh1, mode="tn", out_dtype=COMM_DTYPE, tm=d, tn=d, tk=tkr, name="g_w_out")
    dproj = lax.empty((rows, in_p), ACT_DTYPE)
    d_bs, d_br, dproj = _merge_bwd(dh1, w_o, bs, br, gates, dproj, g0)
    g_w_bs = _mm(y_ssd, d_bs, mode="tn", out_dtype=COMM_DTYPE, tm=1024, tn=d, tk=tkr2, name="g_w_branch_ssd")
    g_w_br = _mm(y_ret, d_br, mode="tn", out_dtype=COMM_DTYPE, tm=1024, tn=d, tk=tkr2, name="g_w_branch_ret")
    c_mid = [rparts(g_w_bs), rparts(g_w_br), rparts(g_w_o)]
    h_mid, t_mid = _exchange_start(c_mid, True, "scatter_mid_start")
    d_yscan, dproj, g_nw = _ssd_gate_bwd(y_scan, z, d_bs, w_bs, ssd_norm_w + t_mid[0, 0], dproj, z0)
    dxs, d_bm, d_cm, dproj, g_bias_p, g_alog_p, g_dsk_p = _ssd_bwd(
        xbc_c, dtr, bias_p, alog_p, dsk_p, ssd_states, d_yscan, dproj, dt0)
    dproj, g_conv_w, g_conv_b = _ssd_conv_bwd(xbc, dxs, d_bm, d_cm, conv_w, ssd_conv_b, dproj, x0)
    d_yret = _mm(d_br, w_br, mode="nt", out_dtype=ACT_DTYPE, tm=tm, tn=1024, tk=d, name="d_y_ret")
    dproj = _ret_bwd(qkvg, cos, sin, lgam, ret_states, d_yret, dproj)
    g_w_in_p = _mm(u1, dproj, mode="tn", out_dtype=F32, tm=d, tn=_pick(in_p, (768, 512)), tk=tkr2, name="g_w_in")
    c_in = [_shards_from_cols([g_w_in_p], [in_plan], w_in.shape[2], "g_w_in_shards")]
    h_in, t_in = _exchange_start(c_in, True, "scatter_in_start")
    du1 = _mm(dproj, w_in_p, mode="nt", out_dtype=F32, tm=tm, tn=d, tk=_pick(in_p, (4608, 512)), name="d_u1", after=t_in)
    dh0, g_mix_norm = _rms_bwd(du1, h0, mix_norm_w, dh1, "rms1_bwd")
    grad_x = dh0[PAD_ROWS:][None]

    landed = {}
    for key, handle, names in (("in", h_in, ["w_in"]), ("mid", h_mid, ["w_branch_ssd", "w_branch_ret", "w_out"]),
                               ("up", h_up, ["w_up"]), ("down", h_dn, ["w_down"])):
        srcs, lands = _exchange_wait(handle, True, dh0, f"scatter_{key}_wait")
        for nm, land, src in zip(names, lands, srcs, strict=True):
            landed[nm] = (lax.dynamic_index_in_dim(src, me, 0, keepdims=False), land)
    big_m = [m_w_in, m_w_branch_ssd, m_w_branch_ret, m_w_out, m_w_up, m_w_down]
    big_v = [v_w_in, v_w_branch_ssd, v_w_branch_ret, v_w_out, v_w_up, v_w_down]
    big_names = ["w_in", "w_branch_ssd", "w_branch_ret", "w_out", "w_up", "w_down"]
    big_out = {}
    for nm, w, m, v_ in zip(big_names, big, big_m, big_v, strict=True):
        big_out[nm] = [t[None] for t in _adam_shard(*landed[nm], w, m[0], v_[0], "adam_" + nm)]

    small_g = [dh0[FRONT:PAD_ROWS], g_mix_norm, g_conv_w, g_conv_b, _ungroup_lanes(g_bias_p), _ungroup_lanes(g_alog_p),
               _ungroup_lanes(g_dsk_p), g_nw, g_ffn_norm, g_fconv_w, g_fconv_b, g_final, loss_acc[0:1, 0:1]]
    total = _unpack(_allreduce_small(_pack(small_g)), [t.shape for t in small_g])
    loss = total[12].reshape(())
    shard = lambda t, width: lax.dynamic_slice_in_dim(t, me * width, width, axis=1)
    small_names = ["meta_tokens", "mix_norm_w", "ssd_conv_w", "ssd_conv_b", "ssd_dt_bias", "ssd_A_log", "ssd_D", "ssd_norm_w",
                   "ffn_norm_w", "ffn_conv_w", "ffn_conv_b", "final_norm_w"]
    small_w = [meta_tokens, mix_norm_w, ssd_conv_w, ssd_conv_b, ssd_dt_bias, ssd_A_log, ssd_D, ssd_norm_w, ffn_norm_w,
               ffn_conv_w, ffn_conv_b, final_norm_w]
    small_m = [m_meta_tokens, m_mix_norm_w, m_ssd_conv_w, m_ssd_conv_b, m_ssd_dt_bias, m_ssd_A_log, m_ssd_D, m_ssd_norm_w,
               m_ffn_norm_w, m_ffn_conv_w, m_ffn_conv_b, m_final_norm_w]
    small_v = [v_meta_tokens, v_mix_norm_w, v_ssd_conv_w, v_ssd_conv_b, v_ssd_dt_bias, v_ssd_A_log, v_ssd_D, v_ssd_norm_w,
               v_ffn_norm_w, v_ffn_conv_w, v_ffn_conv_b, v_final_norm_w]
    grads = total[:12]
    grads[0] = shard(grads[0], meta_tokens.shape[1])
    grads[2] = shard(grads[2], ssd_conv_w.shape[2])
    grads[9] = shard(grads[9], ffn_conv_w.shape[2])
    grads = [t.reshape(w.shape) for t, w in zip(grads, small_w, strict=True)]
    shapes = [w.shape for w in small_w]
    upd = _adam_small(_pack(small_w), _pack(grads), _pack(small_m), _pack(small_v))
    small_out = {nm: [gr_] + [u[i] for u in (_unpack(t, shapes) for t in upd)]
                 for i, (nm, gr_) in enumerate(zip(small_names, grads, strict=True))}

    order = ["meta_tokens", "mix_norm_w", "w_in", "ssd_conv_w", "ssd_conv_b", "ssd_dt_bias", "ssd_A_log", "ssd_D", "ssd_norm_w",
             "w_branch_ssd", "w_branch_ret", "w_out", "ffn_norm_w", "w_up", "ffn_conv_w", "ffn_conv_b", "w_down", "final_norm_w"]
    res = {**big_out, **small_out}
    return (loss, grad_x, *[res[nm][0] for nm in order], *[res[nm][1] for nm in order], *[res[nm][2] for nm in order],
            *[res[nm][3] for nm in order])
```

```python
import functools
import math

import jax
import jax.numpy as jnp
import numpy as np
from jax import lax
from jax.experimental import pallas as pl
from jax.experimental.pallas import tpu as pltpu

F32 = jnp.float32
MXU_DTYPE = jnp.bfloat16
ACT_DTYPE = jnp.bfloat16
COMM_DTYPE = jnp.bfloat16

N_META = 16
CHUNK = 128
FRONT = CHUNK - N_META
PAD_ROWS = FRONT + N_META
EPS = 1e-6
N_DEV = 8

SSD_D_INNER = 2048
SSD_HEAD_DIM = 64
SSD_HEADS = 32
SSD_GROUPS = 4
SSD_HPG = SSD_HEADS // SSD_GROUPS
SSD_STATE = 128
SSD_CONV = 4
SSD_CONV_DIM = SSD_D_INNER + 2 * SSD_GROUPS * SSD_STATE
SSD_GW = SSD_D_INNER // SSD_GROUPS
RET_HEADS = 4
RET_QK = 256
RET_V = 512
RET_HW = 2 * RET_QK + 2 * RET_V
ROPE_BASE = 10000.0
FFN_CONV = 3
HALO = 16
LANES = 128

ADAM_LR = 0.001
ADAM_B1 = 0.9
ADAM_B2 = 0.999
ADAM_EPS = 1e-08
ADAM_WD = 0.01
ADAM_STEP = 10

VMEM_LIMIT = 56 * 1024 * 1024
MESH = pl.DeviceIdType.MESH

NN = (((1,), (0,)), ((), ()))
NT = (((1,), (1,)), ((), ()))
TN = (((0,), (0,)), ((), ()))


def _params(sem):
    return pltpu.CompilerParams(dimension_semantics=sem, vmem_limit_bytes=VMEM_LIMIT)


def _mxu(a, b, dn):
    return lax.dot_general(a.astype(MXU_DTYPE), b.astype(MXU_DTYPE), dn, preferred_element_type=F32)


@functools.partial(jax.custom_vjp, nondiff_argnums=(2,))
def _dot(a, b, dn=NN):
    return _mxu(a, b, dn)


def _dot_fwd(a, b, dn):
    return _mxu(a, b, dn), (a, b)


def _dot_bwd(dn, res, g):
    a, b = res
    if dn == NN:
        return _mxu(g, b, NT), _mxu(a, g, TN)
    if dn == NT:
        return _mxu(g, b, NN), _mxu(g, a, TN)
    assert dn == TN
    return _mxu(b, g, NT), _mxu(a, g, NN)


_dot.defvjp(_dot_fwd, _dot_bwd)


def _silu(x):
    return x * jax.nn.sigmoid(x)


def _dsilu(x):
    s = jax.nn.sigmoid(x)
    return s * (1.0 + x * (1.0 - s))


def _row_tile(rows):
    return 640 if rows % 640 == 0 else 128


def _mm(a, b, *, mode, out_dtype, tm, tn, tk, name, add=None, after=None, b_k0=0, b_n0=0, n_out=None):
    if mode == "nt":
        (m, k), n = a.shape, b.shape[0]
        k2 = k if b_k0 % tk == 0 and b_k0 + k <= b.shape[1] else None
    else:
        (m, k) = a.shape if mode == "nn" else a.shape[::-1]
        k2 = b.shape[0]
        n = b.shape[1] if n_out is None else n_out
        assert b_n0 % tn == 0 and b_n0 + n <= b.shape[1]
    assert (b_k0 == 0 or mode == "nt") and ((b_n0 == 0 and n_out is None) or mode != "nt")
    assert k == k2 and m % tm == 0 and n % tn == 0 and k % tk == 0, (name, a.shape, b.shape, tm, tn, tk)
    kb0, nb0 = b_k0 // tk, b_n0 // tn
    nk = k // tk
    dn = {"nn": NN, "nt": NT, "tn": TN}[mode]
    has_add = add is not None
    n_in = 2 + has_add + (after is not None)

    def body(*refs):
        a_ref, b_ref = refs[0], refs[1]
        add_ref = refs[2] if has_add else None
        o_ref = refs[n_in]
        p = _dot(a_ref[...], b_ref[...], dn)
        if nk == 1:
            if has_add:
                p = p + add_ref[...]
            o_ref[...] = p.astype(out_dtype)
        else:
            acc_ref = refs[n_in + 1]
            kk = pl.program_id(2)

            @pl.when(kk == 0)
            def _():
                acc_ref[...] = p

            @pl.when(kk > 0)
            def _():
                acc_ref[...] += p

            @pl.when(kk == nk - 1)
            def _():
                r = acc_ref[...]
                if has_add:
                    r = r + add_ref[...]
                o_ref[...] = r.astype(out_dtype)

    if mode == "tn":
        a_spec = pl.BlockSpec((tk, tm), lambda j, i, kk: (kk, i))
    else:
        a_spec = pl.BlockSpec((tm, tk), lambda j, i, kk: (i, kk))
    if mode == "nt":
        b_spec = pl.BlockSpec((tn, tk), lambda j, i, kk: (j, kk + kb0))
    else:
        b_spec = pl.BlockSpec((tk, tn), lambda j, i, kk: (kk, j + nb0))
    o_spec = pl.BlockSpec((tm, tn), lambda j, i, kk: (i, j))
    in_specs = [a_spec, b_spec] + ([o_spec] if has_add else [])
    args = (a, b) + ((add,) if has_add else ())
    if after is not None:
        in_specs.append(pl.BlockSpec(memory_space=pl.ANY))
        args += (after,)
    return pl.pallas_call(
        body, name=name, grid=(n // tn, m // tm, nk), in_specs=in_specs, out_specs=o_spec,
        out_shape=jax.ShapeDtypeStruct((m, n), out_dtype),
        scratch_shapes=[pltpu.VMEM((tm, tn), F32)] if nk > 1 else [],
        compiler_params=_params(("parallel", "parallel", "arbitrary")),
    )(*args)


def _pick(n, cands):
    for c in cands:
        if n % c == 0:
            return c
    return n


def _rms_fwd(h, w, name):
    rows, d = h.shape
    tm = _row_tile(rows)

    def body(h_ref, w_ref, u_ref):
        x = h_ref[...]
        r = lax.rsqrt(jnp.mean(x * x, axis=-1, keepdims=True) + EPS)
        u_ref[...] = (x * r * w_ref[...]).astype(ACT_DTYPE)

    return pl.pallas_call(
        body, name=name, grid=(rows // tm,),
        in_specs=[pl.BlockSpec((tm, d), lambda i: (i, 0)), pl.BlockSpec((1, d), lambda i: (0, 0))],
        out_specs=pl.BlockSpec((tm, d), lambda i: (i, 0)),
        out_shape=jax.ShapeDtypeStruct((rows, d), ACT_DTYPE),
        compiler_params=_params(("parallel",)),
    )(h, w)


def _rms_bwd(du, h, w, dres, name):
    rows, d = h.shape
    tm = _row_tile(rows)

    def body(du_ref, h_ref, w_ref, dres_ref, dh_ref, dw_ref):
        x = h_ref[...]
        dy = du_ref[...].astype(F32)
        r = lax.rsqrt(jnp.mean(x * x, axis=-1, keepdims=True) + EPS)
        xhat = x * r
        dxn = dy * w_ref[...]
        dx = r * (dxn - xhat * jnp.mean(dxn * xhat, axis=-1, keepdims=True))
        dh_ref[...] = dres_ref[...] + dx

        @pl.when(pl.program_id(0) == 0)
        def _():
            dw_ref[...] = jnp.zeros_like(dw_ref)

        dw_ref[...] += jnp.sum(dy * xhat, axis=0, keepdims=True)

    return pl.pallas_call(
        body, name=name, grid=(rows // tm,),
        in_specs=[pl.BlockSpec((tm, d), lambda i: (i, 0)), pl.BlockSpec((tm, d), lambda i: (i, 0)),
                  pl.BlockSpec((1, d), lambda i: (0, 0)), pl.BlockSpec((tm, d), lambda i: (i, 0))],
        out_specs=[pl.BlockSpec((tm, d), lambda i: (i, 0)), pl.BlockSpec((1, d), lambda i: (0, 0))],
        out_shape=[jax.ShapeDtypeStruct((rows, d), F32), jax.ShapeDtypeStruct((1, d), F32)],
        compiler_params=_params(("arbitrary",)),
    )(du, h, w, dres)


def _out_proj_norm(merged, w_o, h0, nw):
    rows, d = h0.shape
    tm = _row_tile(rows)

    def body(m_ref, w_ref, h_ref, nw_ref, h1_ref, u_ref):
        x = h_ref[...] + _mxu(m_ref[...], w_ref[...], NN)
        h1_ref[...] = x
        r = lax.rsqrt(jnp.mean(x * x, axis=-1, keepdims=True) + EPS)
        u_ref[...] = (x * r * nw_ref[...]).astype(ACT_DTYPE)

    spec = pl.BlockSpec((tm, d), lambda i: (i, 0))
    return pl.pallas_call(
        body, name="out_proj_norm", grid=(rows // tm,),
        in_specs=[pl.BlockSpec((tm, merged.shape[1]), lambda i: (i, 0)), pl.BlockSpec(w_o.shape, lambda i: (0, 0)), spec,
                  pl.BlockSpec((1, d), lambda i: (0, 0))],
        out_specs=[spec, spec], out_shape=[jax.ShapeDtypeStruct((rows, d), F32), jax.ShapeDtypeStruct((rows, d), ACT_DTYPE)],
        compiler_params=_params(("parallel",)),
    )(merged, w_o, h0, nw)


def _down_loss_head(act, w_dn, h1, tgt, w):
    rows, d = h1.shape
    tm = _row_tile(rows)

    def body(a_ref, wd_ref, h_ref, t_ref, w_ref, dh_ref, loss_ref, dw_ref):
        i = pl.program_id(0)
        x = h_ref[...] + _mxu(a_ref[...], wd_ref[...], NN)
        r = lax.rsqrt(jnp.mean(x * x, axis=-1, keepdims=True) + EPS)
        xhat = x * r
        wv = w_ref[...]
        row = i * tm + lax.broadcasted_iota(jnp.int32, (tm, 1), 0)
        live = row >= PAD_ROWS
        diff = jnp.where(live, xhat * wv - t_ref[...], 0.0)
        dy = diff * (1.0 / d)
        dxn = dy * wv
        dh_ref[...] = r * (dxn - xhat * jnp.mean(dxn * xhat, axis=-1, keepdims=True))

        @pl.when(i == 0)
        def _():
            loss_ref[...] = jnp.zeros_like(loss_ref)
            dw_ref[...] = jnp.zeros_like(dw_ref)

        loss_ref[...] += 0.5 * jnp.sum(jnp.mean(diff * diff, axis=-1, keepdims=True))
        dw_ref[...] += jnp.sum(dy * xhat, axis=0, keepdims=True)

    return pl.pallas_call(
        body, name="down_loss_head", grid=(rows // tm,),
        in_specs=[pl.BlockSpec((tm, act.shape[1]), lambda i: (i, 0)), pl.BlockSpec(w_dn.shape, lambda i: (0, 0)),
                  pl.BlockSpec((tm, d), lambda i: (i, 0)), pl.BlockSpec((tm, d), lambda i: (i, 0)),
                  pl.BlockSpec((1, d), lambda i: (0, 0))],
        out_specs=[pl.BlockSpec((tm, d), lambda i: (i, 0)), pl.BlockSpec((8, LANES), lambda i: (0, 0)),
                   pl.BlockSpec((1, d), lambda i: (0, 0))],
        out_shape=[jax.ShapeDtypeStruct((rows, d), F32), jax.ShapeDtypeStruct((8, LANES), F32),
                   jax.ShapeDtypeStruct((1, d), F32)],
        compiler_params=_params(("arbitrary",)),
    )(act, w_dn, h1, tgt, w)


def _prev_halo_spec(tm, width, col):
    return pl.BlockSpec((HALO, width), lambda j, i: (jnp.maximum(i * (tm // HALO) - 1, 0), col(j)))


def _next_halo_spec(tm, rows, width, col):
    last = rows // HALO - 1
    return pl.BlockSpec((HALO, width), lambda j, i: (jnp.minimum((i + 1) * (tm // HALO), last), col(j)))


def _conv_taps(cat, w_ref, b_ref, kw):
    acc = b_ref[...] + w_ref[kw - 1:kw, :] * cat
    for s in range(1, kw):
        acc = acc + w_ref[kw - 1 - s:kw - s, :] * pltpu.roll(cat, s, 0)
    return acc


def _conv_back(dpre, w_ref, kw):
    n = dpre.shape[0]
    acc = w_ref[kw - 1:kw, :] * dpre
    for s in range(1, kw):
        acc = acc + w_ref[kw - 1 - s:kw - s, :] * pltpu.roll(dpre, n - s, 0)
    return acc


def _xbc_proj_conv(u1, w_in_p, col0, w, b):
    rows, d = u1.shape
    width = w.shape[1]
    tm, tc = _row_tile(rows), 512
    cb0 = col0 // tc
    assert col0 % tc == 0

    def body(u_ref, m_ref, w_ref, b_ref, x_ref, o_ref, carry):
        i = pl.program_id(1)

        @pl.when(i == 0)
        def _():
            carry[...] = jnp.zeros_like(carry)

        xb = _mxu(u_ref[...], m_ref[...], NN).astype(ACT_DTYPE)
        x_ref[...] = xb
        x = xb.astype(F32)
        cat = jnp.concatenate([carry[...], x], axis=0)
        carry[...] = x[tm - HALO:, :]
        pre = _conv_taps(cat, w_ref, b_ref, SSD_CONV)[HALO:]
        row = i * tm + lax.broadcasted_iota(jnp.int32, (tm, 1), 0)
        o_ref[...] = jnp.where(row >= FRONT, _silu(pre), 0.0).astype(ACT_DTYPE)

    main = pl.BlockSpec((tm, tc), lambda j, i: (i, j))
    par = lambda r: pl.BlockSpec((r, tc), lambda j, i: (0, j))
    act = jax.ShapeDtypeStruct((rows, width), ACT_DTYPE)
    return pl.pallas_call(
        body, name="xbc_proj_conv", grid=(width // tc, rows // tm),
        in_specs=[pl.BlockSpec((tm, d), lambda j, i: (i, 0)), pl.BlockSpec((d, tc), lambda j, i: (0, cb0 + j)),
                  par(SSD_CONV), par(1)],
        out_specs=[main, main], out_shape=[act, act], scratch_shapes=[pltpu.VMEM((HALO, tc), F32)],
        compiler_params=_params(("parallel", "arbitrary")),
    )(u1, w_in_p, w, b)


def _ssd_conv_bwd(xbc, dxs, dbm, dcm, w, b, dproj, col0):
    rows, width = xbc.shape
    tm, tc = _row_tile(rows), 512
    kw = SSD_CONV
    nx = dxs.shape[1] // tc
    assert dbm.shape[1] == tc and dcm.shape[1] == tc and width == (nx + 2) * tc and col0 % tc == 0

    def body(x_ref, xp_ref, xn_ref, d0_ref, d0n_ref, d1_ref, d1n_ref, d2_ref, d2n_ref, w_ref, b_ref, dp_ref,
             dx_ref, dw_ref, db_ref):
        j, i = pl.program_id(0), pl.program_id(1)
        xp = jnp.where(i == 0, 0.0, xp_ref[...].astype(F32))
        cat = jnp.concatenate([xp, x_ref[...].astype(F32), xn_ref[...].astype(F32)], axis=0)
        sh = [cat] + [pltpu.roll(cat, s, 0) for s in range(1, kw)]
        pre = b_ref[...] + w_ref[kw - 1:kw, :] * sh[0]
        for s in range(1, kw):
            pre = pre + w_ref[kw - 1 - s:kw - s, :] * sh[s]
        pre = pre[HALO:]
        row = i * tm + lax.broadcasted_iota(jnp.int32, (tm + HALO, 1), 0)
        live = (row >= FRONT) & (row < rows)
        pick = lambda a, bb, c: jnp.where(j < nx, a[...], jnp.where(j == nx, bb[...], c[...])).astype(F32)
        dout = jnp.concatenate([pick(d0_ref, d1_ref, d2_ref), pick(d0n_ref, d1n_ref, d2n_ref)], axis=0)
        dpre = jnp.where(live, dout * _dsilu(pre), 0.0)
        dx_ref[...] = _conv_back(dpre, w_ref, kw)[:tm].astype(ACT_DTYPE)

        @pl.when(i == 0)
        def _():
            dw_ref[...] = jnp.zeros_like(dw_ref)
            db_ref[...] = jnp.zeros_like(db_ref)

        dmain = dpre[:tm]
        db_ref[...] += jnp.sum(dmain, axis=0, keepdims=True)
        for k in range(kw):
            dw_ref[k:k + 1, :] += jnp.sum(dmain * sh[kw - 1 - k][HALO:HALO + tm], axis=0, keepdims=True)

    main = pl.BlockSpec((tm, tc), lambda j, i: (i, j))
    par = lambda r: pl.BlockSpec((r, tc), lambda j, i: (0, j))
    col = lambda j: j
    xcol, zero = (lambda j: jnp.minimum(j, nx - 1)), (lambda j: 0)
    dspecs = lambda c: [pl.BlockSpec((tm, tc), lambda j, i: (i, c(j))), _next_halo_spec(tm, rows, tc, c)]
    return pl.pallas_call(
        body, name="ssd_conv_bwd", grid=(width // tc, rows // tm),
        in_specs=[main, _prev_halo_spec(tm, tc, col), _next_halo_spec(tm, rows, tc, col)]
        + dspecs(xcol) + dspecs(zero) + dspecs(zero) + [par(kw), par(1), pl.BlockSpec(memory_space=pl.ANY)],
        out_specs=[pl.BlockSpec((tm, tc), lambda j, i: (i, col0 // tc + j)), par(kw), par(1)],
        out_shape=[jax.ShapeDtypeStruct(dproj.shape, dproj.dtype), jax.ShapeDtypeStruct((kw, width), F32),
                   jax.ShapeDtypeStruct((1, width), F32)],
        input_output_aliases={11: 0}, compiler_params=_params(("parallel", "arbitrary")),
    )(xbc, xbc, xbc, dxs, dxs, dbm, dbm, dcm, dcm, w, b, dproj)


def _ffn_up_conv(u2, w_up, w, b):
    rows, d = u2.shape
    width = w_up.shape[1]
    dff = width // 2
    tm, tc = _row_tile(rows), _pick(dff, (256, 128))
    nb = dff // tc
    kw = FFN_CONV

    def body(u_ref, mg_ref, mv_ref, wg_ref, bg_ref, wv_ref, bv_ref, ug_ref, uv_ref, o_ref, cg, cv):
        i = pl.program_id(1)

        @pl.when(i == 0)
        def _():
            cg[...] = jnp.zeros_like(cg)
            cv[...] = jnp.zeros_like(cv)

        def pre(m_ref, up_ref, carry, w_ref, b_ref):
            upb = _mxu(u_ref[...], m_ref[...], NN).astype(ACT_DTYPE)
            up_ref[...] = upb
            x = upb.astype(F32)
            cat = jnp.concatenate([carry[...], x], axis=0)
            carry[...] = x[tm - HALO:, :]
            return _conv_taps(cat, w_ref, b_ref, kw)[HALO:]

        ag = pre(mg_ref, ug_ref, cg, wg_ref, bg_ref)
        av = pre(mv_ref, uv_ref, cv, wv_ref, bv_ref)
        o_ref[...] = (_silu(ag) * av).astype(ACT_DTYPE)

    gcol, vcol = (lambda j: j), (lambda j: j + nb)
    mat = lambda col: pl.BlockSpec((d, tc), lambda j, i: (0, col(j)))
    par = lambda r, col: pl.BlockSpec((r, tc), lambda j, i: (0, col(j)))
    out = pl.BlockSpec((tm, tc), lambda j, i: (i, j))
    act = jax.ShapeDtypeStruct((rows, dff), ACT_DTYPE)
    return pl.pallas_call(
        body, name="ffn_up_conv", grid=(nb, rows // tm),
        in_specs=[pl.BlockSpec((tm, d), lambda j, i: (i, 0)), mat(gcol), mat(vcol),
                  par(kw, gcol), par(1, gcol), par(kw, vcol), par(1, vcol)],
        out_specs=[out, out, out], out_shape=[act, act, act],
        scratch_shapes=[pltpu.VMEM((HALO, tc), F32), pltpu.VMEM((HALO, tc), F32)],
        compiler_params=_params(("parallel", "arbitrary")),
    )(u2, w_up, w_up, w, b, w, b)


def _ffn_conv_bwd(up_g, up_v, dact, w, b, u2):
    rows, dff = up_g.shape
    d = u2.shape[1]
    tm, tc = _row_tile(rows), _pick(dff, (256, 128))
    nb = dff // tc
    kw = FFN_CONV

    sb = 16

    def body(g_ref, gp_ref, gn_ref, v_ref, vp_ref, vn_ref, d_ref, dn_ref, wg_ref, bg_ref, wv_ref, bv_ref, u_ref,
             dxg_ref, dxv_ref, dwg_ref, dbg_ref, dwv_ref, dbv_ref, gwg_ref, gwv_ref, xg_s, xv_s, dd_s, og_s, ov_s):
        i = pl.program_id(1)
        last = i == rows // tm - 1
        for x_s, x_ref, xp_ref, xn_ref in ((xg_s, g_ref, gp_ref, gn_ref), (xv_s, v_ref, vp_ref, vn_ref)):
            x_s[0:HALO, :] = jnp.where(i == 0, 0.0, xp_ref[...].astype(F32))
            x_s[HALO:HALO + tm, :] = x_ref[...].astype(F32)
            x_s[HALO + tm:, :] = xn_ref[...].astype(F32)
        dd_s[0:tm, :] = d_ref[...].astype(F32)
        dd_s[tm:, :] = jnp.where(last, 0.0, dn_ref[...].astype(F32))

        wg = [wg_ref[k:k + 1, :] for k in range(kw)]
        wv = [wv_ref[k:k + 1, :] for k in range(kw)]
        bg, bv = bg_ref[...], bv_ref[...]

        def taps(x_s, e0, w, bias):
            win = x_s[pl.ds(e0 + HALO - sb, 2 * sb), :]
            sh = [win[sb:], pltpu.roll(win, 1, 0)[sb:], pltpu.roll(win, 2, 0)[sb:]]
            return bias + w[2] * sh[0] + w[1] * sh[1] + w[0] * sh[2], sh

        def dpre_of(e0):
            ag, sh_g = taps(xg_s, e0, wg, bg)
            av, sh_v = taps(xv_s, e0, wv, bv)
            dout = dd_s[pl.ds(e0, sb), :]
            s = jax.nn.sigmoid(ag)
            silu = ag * s
            return dout * av * (s + silu * (1.0 - s)), dout * silu, sh_g, sh_v

        def back(dp, nxt, w):
            cat = jnp.concatenate([dp, nxt], axis=0)
            return w[2] * dp + w[1] * pltpu.roll(cat, 2 * sb - 1, 0)[:sb] + w[0] * pltpu.roll(cat, 2 * sb - 2, 0)[:sb]

        nxt_g, nxt_v, _, _ = dpre_of(tm)
        acc_g = acc_v = tuple(jnp.zeros((sb, tc), F32) for _ in range(kw + 1))
        for e0 in range(tm - sb, -1, -sb):
            dpg, dpv, sh_g, sh_v = dpre_of(e0)
            og_s[e0:e0 + sb, :] = back(dpg, nxt_g, wg)
            ov_s[e0:e0 + sb, :] = back(dpv, nxt_v, wv)
            acc_g = tuple(a + dpg * t for a, t in zip(acc_g, (sh_g[2], sh_g[1], sh_g[0], 1.0)))
            acc_v = tuple(a + dpv * t for a, t in zip(acc_v, (sh_v[2], sh_v[1], sh_v[0], 1.0)))
            nxt_g, nxt_v = dpg, dpv

        @pl.when(i == 0)
        def _():
            for r in (dwg_ref, dbg_ref, dwv_ref, dbv_ref, gwg_ref, gwv_ref):
                r[...] = jnp.zeros_like(r)

        for acc, o_s, dx_ref, dw_ref, db_ref, gw_ref in ((acc_g, og_s, dxg_ref, dwg_ref, dbg_ref, gwg_ref),
                                                        (acc_v, ov_s, dxv_ref, dwv_ref, dbv_ref, gwv_ref)):
            dx = o_s[...].astype(ACT_DTYPE)
            dx_ref[...] = dx
            gw_ref[...] += _mxu(u_ref[...], dx, TN)
            for k in range(kw):
                dw_ref[k:k + 1, :] += jnp.sum(acc[k], axis=0, keepdims=True)
            db_ref[...] += jnp.sum(acc[kw], axis=0, keepdims=True)

    gcol, vcol = (lambda j: j), (lambda j: j + nb)
    main = lambda col: pl.BlockSpec((tm, tc), lambda j, i: (i, col(j)))
    par = lambda r, col: pl.BlockSpec((r, tc), lambda j, i: (0, col(j)))
    halos = lambda col: [_prev_halo_spec(tm, tc, col), _next_halo_spec(tm, rows, tc, col)]
    act_shape = jax.ShapeDtypeStruct((rows, dff), ACT_DTYPE)
    par_shapes = [jax.ShapeDtypeStruct((kw, dff), F32), jax.ShapeDtypeStruct((1, dff), F32)]
    gw_shape = jax.ShapeDtypeStruct((d, dff), F32)
    return pl.pallas_call(
        body, name="ffn_conv_bwd", grid=(nb, rows // tm),
        in_specs=[main(gcol)] + halos(gcol) + [main(gcol)] + halos(gcol) + [main(gcol), _next_halo_spec(tm, rows, tc, gcol),
                  par(kw, gcol), par(1, gcol), par(kw, vcol), par(1, vcol), pl.BlockSpec((tm, d), lambda j, i: (i, 0))],
        out_specs=[main(gcol), main(gcol), par(kw, gcol), par(1, gcol), par(kw, gcol), par(1, gcol), par(d, gcol), par(d, gcol)],
        out_shape=[act_shape, act_shape] + par_shapes + par_shapes + [gw_shape, gw_shape],
        scratch_shapes=[pltpu.VMEM((tm + 2 * HALO, tc), F32), pltpu.VMEM((tm + 2 * HALO, tc), F32),
                        pltpu.VMEM((tm + HALO, tc), F32), pltpu.VMEM((tm, tc), F32), pltpu.VMEM((tm, tc), F32)],
        compiler_params=_params(("parallel", "arbitrary")),
    )(up_g, up_g, up_g, up_v, up_v, up_v, dact, dact, w, b, w, b, u2)


def _ssd_scalars(dtr, dt_bias, a_log, live):
    q = CHUNK
    pre = dtr + dt_bias
    dt = jnp.where(live, jax.nn.softplus(pre), 0.0)
    a_neg = -jnp.exp(a_log)
    li = lax.broadcasted_iota(jnp.int32, (q, q), 0)
    si = lax.broadcasted_iota(jnp.int32, (q, q), 1)
    causal = li >= si
    tri = jnp.where(causal, 1.0, 0.0).astype(F32)
    a_cs = sum(_mxu(tri, p, NN) for p in _split(dt * a_neg, 3))
    return pre, dt, a_neg, a_cs, causal, tri


def _head_select():
    r = lax.broadcasted_iota(jnp.int32, (LANES, SSD_GW), 0)
    c = lax.broadcasted_iota(jnp.int32, (LANES, SSD_GW), 1)
    return jnp.where(c // SSD_HEAD_DIM == r, 1.0, 0.0).astype(MXU_DTYPE)


def _split(t, parts):
    out, rem = [], t
    for _ in range(parts):
        p = rem.astype(MXU_DTYPE)
        out.append(p)
        rem = rem - p.astype(F32)
    return out


def _stacked(ts, parts, sel, dn):
    out = _mxu(jnp.concatenate([p for t, n in zip(ts, parts, strict=True) for p in _split(t, n)], axis=0), sel, dn)
    res, r0 = [], 0
    for t, n in zip(ts, parts, strict=True):
        r = t.shape[0]
        res.append(sum(out[r0 + k * r:r0 + (k + 1) * r] for k in range(n)))
        r0 += n * r
    return res


def _head_cols(ts, sel):
    return _stacked(ts, [2] * len(ts), sel, NN)


def _head_sums(ts, parts, sel):
    return _stacked(ts, parts, sel, NT)


def _half_masks():
    lane = lax.broadcasted_iota(jnp.int32, (CHUNK, LANES), 1)
    return lane < SSD_HEAD_DIM, lane >= SSD_HEAD_DIM


def _ssd_scan(xs, bm, cm, dtr, prev, dt_bias, a_log, d_skip, live):
    q = CHUNK
    sel = _head_select()
    _, dt, _, a_cs, causal, _ = _ssd_scalars(dtr, dt_bias, a_log, live)
    a_cs_t = a_cs.T
    a_end = a_cs[q - 1:q, :]
    dt_x, e_x, w_x, d_x = _head_cols([dt, jnp.exp(a_cs), jnp.exp(a_end - a_cs), jnp.broadcast_to(d_skip, (16, LANES))], sel)
    xdt = xs * dt_x
    cb = _dot(cm, bm, NT)
    y = _dot(cm, prev) * e_x + d_x[0:1] * xs
    new = prev * e_x[q - 1:q, :] + _dot(bm, xdt * w_x, TN)
    masks = _half_masks()
    ys = []
    for pp in range(SSD_HPG // 2):
        xpair = xdt[:, pp * LANES:(pp + 1) * LANES]
        acc = jnp.zeros((q, LANES), F32)
        for half in range(2):
            hh = 2 * pp + half
            decay = jnp.exp(jnp.where(causal, a_cs[:, hh:hh + 1] - a_cs_t[hh:hh + 1, :], -jnp.inf))
            acc = acc + _dot(cb * decay, jnp.where(masks[half], xpair, 0.0))
        ys.append(acc)
    return y + jnp.concatenate(ys, axis=1), new


def _ssd_gate(y, z, nw):
    yz = y * _silu(z)
    return yz * lax.rsqrt(jnp.mean(yz * yz, axis=-1, keepdims=True) + EPS) * nw


def _ssd_scan_bwd(xs, bm, cm, dtr, prev, dt_bias, a_log, d_skip, live, dy, dnew):
    q = CHUNK
    sel = _head_select()
    pre, dt, a_neg, a_cs, causal, tri = _ssd_scalars(dtr, dt_bias, a_log, live)
    a_cs_t = a_cs.T
    a_end = a_cs[q - 1:q, :]
    dt_x, e_x, w_x, d_x = _head_cols([dt, jnp.exp(a_cs), jnp.exp(a_end - a_cs), jnp.broadcast_to(d_skip, (16, LANES))], sel)
    g_x, d_x = e_x[q - 1:q, :], d_x[0:1]
    xdt = xs * dt_x
    u = xdt * w_x
    cb = _mxu(cm, bm, NT)
    cs = _mxu(cm, prev, NN)
    dye = dy * e_x
    dcm = _mxu(dye, prev, NT)
    dprev = _mxu(cm, dye, TN) + dnew * g_x
    dacs_x = dye * cs
    dbm = _mxu(u, dnew, NT)
    du = _mxu(bm, dnew, NN)
    dw_x = du * u
    dacs_x = dacs_x - dw_x
    dend_x = jnp.sum(dw_x + dnew * prev * g_x, axis=0, keepdims=True)
    dxdt = du * w_x
    lane = lax.broadcasted_iota(jnp.int32, (q, LANES), 1)
    sub = lax.broadcasted_iota(jnp.int32, (q, LANES), 0)
    dcb = jnp.zeros((q, q), F32)
    dacs = jnp.zeros((q, LANES), F32)
    dacs_t = jnp.zeros((q, LANES), F32)
    masks = _half_masks()
    dxdt_p = []
    for pp in range(SSD_HPG // 2):
        ps = slice(pp * LANES, (pp + 1) * LANES)
        acc = jnp.zeros((q, LANES), F32)
        for half in range(2):
            hh = 2 * pp + half
            decay = jnp.exp(jnp.where(causal, a_cs[:, hh:hh + 1] - a_cs_t[hh:hh + 1, :], -jnp.inf))
            m = cb * decay
            dyh = jnp.where(masks[half], dy[:, ps], 0.0)
            dm = _mxu(dyh, xdt[:, ps], NT)
            acc = acc + _mxu(m, dyh, TN)
            dcb = dcb + dm * decay
            p = dm * m
            dacs = jnp.where(lane == hh, jnp.sum(p, axis=1, keepdims=True), dacs)
            dacs_t = jnp.where(sub == hh, jnp.sum(p, axis=0, keepdims=True), dacs_t)
        dxdt_p.append(acc)
    dcm = dcm + _mxu(dcb, bm, NN)
    dbm = dbm + _mxu(dcb, cm, TN)
    dxdt = dxdt + jnp.concatenate(dxdt_p, axis=1)
    dxs = dy * d_x + dxdt * dt_x
    rows_x = jnp.concatenate([dend_x, jnp.sum(dy * xs, axis=0, keepdims=True), jnp.zeros((14, SSD_GW), F32)], axis=0)
    dacs_h, ddt_h, rows = _head_sums([dacs_x, dxdt * xs, rows_x], [3, 2, 3], sel)
    dacs = dacs - dacs_t.T + dacs_h
    dacs = dacs + jnp.where(sub == q - 1, rows[0:1], 0.0)
    tri_t = jnp.where(causal, 0.0, 1.0).astype(F32) + jnp.where(lane == sub, 1.0, 0.0)
    da = jnp.dot(tri_t, dacs, precision=lax.Precision.HIGHEST, preferred_element_type=F32)
    ddt = ddt_h + da * a_neg
    dalog = jnp.sum(da * dt, axis=0, keepdims=True) * a_neg
    ddtr = jnp.where(live, ddt * jax.nn.sigmoid(pre), 0.0)
    dbias = jnp.sum(ddtr, axis=0, keepdims=True)
    return dxs, dbm, dcm, ddtr, dprev, dbias, dalog, rows[1:2]


def _chunks_per_step(nc):
    return 13 if nc % 13 == 0 else 1


def _ssd_specs(rev, nc):
    per = _chunks_per_step(nc)
    steps = nc // per
    sidx = (lambda s: steps - 1 - s) if rev else (lambda s: s)
    nb_b = SSD_D_INNER // SSD_STATE
    row = lambda width, col=lambda g: g: pl.BlockSpec((per * CHUNK, width), lambda g, s: (sidx(s), col(g)))
    par = lambda width: pl.BlockSpec((1, width), lambda g, s: (0, g))
    state = lambda: pl.BlockSpec((per, 1, SSD_STATE, SSD_GW), lambda g, s: (sidx(s), g, 0, 0))
    xbc = [row(SSD_GW), row(SSD_STATE, lambda g: nb_b + g), row(SSD_STATE, lambda g: nb_b + SSD_GROUPS + g)]
    return per, steps, sidx, row, par, state, xbc


def _ssd_fwd(xbc_c, dtr, z, dt_bias, a_log, d_skip, nw):
    rows = z.shape[0]
    nc = rows // CHUNK
    per, steps, _, row, par, state, xbc = _ssd_specs(False, nc)

    def body(xs_ref, b_ref, c_ref, dt_ref, z_ref, bias_ref, al_ref, dk_ref, nw_ref, o_ref, y_ref, st_ref, carry):
        s = pl.program_id(1)

        @pl.when(s == 0)
        def _():
            carry[...] = jnp.zeros_like(carry)

        for j in range(per):
            rs = pl.ds(j * CHUNK, CHUNK)
            live = (s * per + j) * CHUNK + lax.broadcasted_iota(jnp.int32, (CHUNK, 1), 0) >= FRONT
            prev = carry[...]
            st_ref[j, 0] = prev
            y, new = _ssd_scan(xs_ref[rs, :].astype(F32), b_ref[rs, :].astype(F32), c_ref[rs, :].astype(F32), dt_ref[rs, :],
                               prev, bias_ref[...], al_ref[...], dk_ref[...], live)
            y_ref[rs, :] = y.astype(ACT_DTYPE)
            o_ref[rs, :] = _ssd_gate(y, z_ref[rs, :].astype(F32), nw_ref[...]).astype(ACT_DTYPE)
            carry[...] = new

    act = jax.ShapeDtypeStruct((rows, SSD_D_INNER), ACT_DTYPE)
    return pl.pallas_call(
        body, name="ssd_fwd", grid=(SSD_GROUPS, steps),
        in_specs=xbc + [row(LANES), row(SSD_GW), par(LANES), par(LANES), par(LANES), par(SSD_GW)],
        out_specs=[row(SSD_GW), row(SSD_GW), state()],
        out_shape=[act, act, jax.ShapeDtypeStruct((nc, SSD_GROUPS, SSD_STATE, SSD_GW), F32)],
        scratch_shapes=[pltpu.VMEM((SSD_STATE, SSD_GW), F32)],
        compiler_params=_params(("parallel", "arbitrary")),
    )(xbc_c, xbc_c, xbc_c, dtr, z, dt_bias, a_log, d_skip, nw)


def _ssd_gate_bwd(y, z, d_bs, w_bs, nw, dproj, col0):
    rows = y.shape[0]
    d = d_bs.shape[1]
    tm = _row_tile(rows)
    assert col0 % SSD_GW == 0

    def body(y_ref, z_ref, db_ref, wb_ref, nw_ref, dp_ref, dy_ref, dz_ref, dnw_ref):
        yv, zv = y_ref[...].astype(F32), z_ref[...].astype(F32)
        dov = _mxu(db_ref[...], wb_ref[...], NT)
        s = jax.nn.sigmoid(zv)
        silu = zv * s
        yz = yv * silu
        r = lax.rsqrt(jnp.mean(yz * yz, axis=-1, keepdims=True) + EPS)
        yhat = yz * r
        dn = dov * nw_ref[...]
        dyz = r * (dn - yhat * jnp.mean(dn * yhat, axis=-1, keepdims=True))
        dy_ref[...] = (dyz * silu).astype(ACT_DTYPE)
        dz_ref[...] = (dyz * yv * (s + silu * (1.0 - s))).astype(ACT_DTYPE)

        @pl.when(pl.program_id(1) == 0)
        def _():
            dnw_ref[...] = jnp.zeros_like(dnw_ref)

        dnw_ref[...] += jnp.sum(dov * yhat, axis=0, keepdims=True)

    spec = pl.BlockSpec((tm, SSD_GW), lambda g, i: (i, g))
    par = pl.BlockSpec((1, SSD_GW), lambda g, i: (0, g))
    act = jax.ShapeDtypeStruct((rows, SSD_D_INNER), ACT_DTYPE)
    return pl.pallas_call(
        body, name="ssd_gate_bwd", grid=(SSD_GROUPS, rows // tm),
        in_specs=[spec, spec, pl.BlockSpec((tm, d), lambda g, i: (i, 0)), pl.BlockSpec((SSD_GW, d), lambda g, i: (g, 0)),
                  par, pl.BlockSpec(memory_space=pl.ANY)],
        out_specs=[spec, pl.BlockSpec((tm, SSD_GW), lambda g, i: (i, col0 // SSD_GW + g)), par],
        out_shape=[act, jax.ShapeDtypeStruct(dproj.shape, dproj.dtype), jax.ShapeDtypeStruct((1, SSD_D_INNER), F32)],
        input_output_aliases={5: 1}, compiler_params=_params(("parallel", "arbitrary")),
    )(y, z, d_bs, w_bs, nw, dproj)


def _ssd_bwd(xbc_c, dtr, dt_bias, a_log, d_skip, states, dy, dproj, col0):
    rows = dy.shape[0]
    nc = rows // CHUNK
    per, steps, sidx, row, par, state, xbc = _ssd_specs(True, nc)
    assert col0 % LANES == 0

    def body(xs_ref, b_ref, c_ref, dt_ref, bias_ref, al_ref, dk_ref, st_ref, dy_ref, dp_ref,
             dxs_ref, db_ref, dc_ref, ddt_ref, dbias_ref, dal_ref, ddk_ref, carry):
        s = pl.program_id(1)

        @pl.when(s == 0)
        def _():
            carry[...] = jnp.zeros_like(carry)
            for r in (dbias_ref, dal_ref, ddk_ref):
                r[...] = jnp.zeros_like(r)

        for j in reversed(range(per)):
            rs = pl.ds(j * CHUNK, CHUNK)
            live = (sidx(s) * per + j) * CHUNK + lax.broadcasted_iota(jnp.int32, (CHUNK, 1), 0) >= FRONT
            dxs, dbm, dcm, ddt, dprev, dbias, dal, ddk = _ssd_scan_bwd(
                xs_ref[rs, :].astype(F32), b_ref[rs, :].astype(F32), c_ref[rs, :].astype(F32), dt_ref[rs, :], st_ref[j, 0],
                bias_ref[...], al_ref[...], dk_ref[...], live, dy_ref[rs, :].astype(F32), carry[...])
            dxs_ref[rs, :] = dxs.astype(ACT_DTYPE)
            db_ref[rs, :] = dbm.astype(ACT_DTYPE)
            dc_ref[rs, :] = dcm.astype(ACT_DTYPE)
            ddt_ref[rs, :] = ddt.astype(ACT_DTYPE)
            carry[...] = dprev
            dbias_ref[...] += dbias
            dal_ref[...] += dal
            ddk_ref[...] += ddk

    bc = jax.ShapeDtypeStruct((rows, SSD_GROUPS * SSD_STATE), ACT_DTYPE)
    head = jax.ShapeDtypeStruct((1, SSD_GROUPS * LANES), F32)
    return pl.pallas_call(
        body, name="ssd_bwd", grid=(SSD_GROUPS, steps),
        in_specs=xbc + [row(LANES), par(LANES), par(LANES), par(LANES), state(), row(SSD_GW), pl.BlockSpec(memory_space=pl.ANY)],
        out_specs=[row(SSD_GW), row(SSD_STATE), row(SSD_STATE), row(LANES, lambda g: col0 // LANES + g),
                   par(LANES), par(LANES), par(LANES)],
        out_shape=[jax.ShapeDtypeStruct((rows, SSD_D_INNER), ACT_DTYPE), bc, bc,
                   jax.ShapeDtypeStruct(dproj.shape, dproj.dtype), head, head, head],
        input_output_aliases={9: 3}, scratch_shapes=[pltpu.VMEM((SSD_STATE, SSD_GW), F32)],
        compiler_params=_params(("parallel", "arbitrary")),
    )(xbc_c, xbc_c, xbc_c, dtr, dt_bias, a_log, d_skip, states, dy, dproj)


def _rotary_tables(rows):
    pos = np.arange(rows, dtype=np.float32) - np.float32(FRONT)
    inv_freq = np.float32(ROPE_BASE) ** (-np.linspace(0.0, 1.0, RET_QK // 2, dtype=np.float32))
    ang = (pos[:, None] * inv_freq[None, :]).astype(np.float32).astype(np.float64)
    lgam = np.log(1.0 - 2.0 ** (-5.0 - np.arange(RET_HEADS, dtype=np.float64))).astype(np.float32)
    lgam = np.broadcast_to(lgam[:, None, None], (RET_HEADS, 8, LANES))
    return jnp.asarray(np.cos(ang).astype(np.float32)), jnp.asarray(np.sin(ang).astype(np.float32)), jnp.asarray(lgam)


def _rotary(t, cos, sin):
    half = t.shape[-1] // 2
    t1, t2 = t[:, :half], t[:, half:]
    return jnp.concatenate([t1 * cos - t2 * sin, t2 * cos + t1 * sin], axis=1)


def _ret_chunk(qh, kh, vh, gh, prev, cos, sin, lg):
    q = CHUNK
    qr = _rotary(qh, cos, sin)
    kr = _rotary(kh, cos, sin) * (RET_QK ** -0.5)
    li = lax.broadcasted_iota(jnp.int32, (q, q), 0)
    si = lax.broadcasted_iota(jnp.int32, (q, q), 1)
    dist = (li - si).astype(F32)
    decay = jnp.exp(jnp.where(li >= si, dist * lg, -jnp.inf))
    idx = lax.broadcasted_iota(jnp.int32, (q, 1), 0).astype(F32)
    scores = _dot(qr, kr, NT) * decay
    out = _dot(scores, vh)
    kv = _dot(kr * jnp.exp((q - 1.0 - idx) * lg), vh, TN)
    out = out + _dot(qr, prev) * jnp.exp((idx + 1.0) * lg)
    new = prev * jnp.exp(q * lg) + kv
    out = out * lax.rsqrt(jnp.mean(out * out, axis=-1, keepdims=True) + EPS)
    return _silu(gh) * out, new


def _ret_specs(rev, nc):
    per = _chunks_per_step(nc)
    steps = nc // per
    sidx = (lambda s: steps - 1 - s) if rev else (lambda s: s)
    row = lambda width: pl.BlockSpec((per * CHUNK, width), lambda h, s: (sidx(s), h))
    tab = lambda: pl.BlockSpec((per * CHUNK, RET_QK // 2), lambda h, s: (sidx(s), 0))
    lgs = lambda: pl.BlockSpec((1, 8, LANES), lambda h, s: (h, 0, 0))
    state = lambda: pl.BlockSpec((per, 1, RET_QK, RET_V), lambda h, s: (sidx(s), h, 0, 0))
    part = lambda width, k: pl.BlockSpec((per * CHUNK, width), lambda h, s: (sidx(s), h * (RET_HW // width) + k))
    ins = [part(RET_QK, 0), part(RET_QK, 1), part(RET_V, 1), part(RET_V, 2), tab(), tab(), lgs()]
    return per, steps, sidx, row, state, ins


def _ret_fwd(qkvg, cos, sin, lgam):
    rows = qkvg.shape[0]
    nc = rows // CHUNK
    per, steps, _, row, state, ins = _ret_specs(False, nc)
    q = k = v = g = qkvg

    def body(q_ref, k_ref, v_ref, g_ref, cos_ref, sin_ref, lg_ref, y_ref, st_ref, carry):
        @pl.when(pl.program_id(1) == 0)
        def _():
            carry[...] = jnp.zeros_like(carry)

        for j in range(per):
            rs = pl.ds(j * CHUNK, CHUNK)
            prev = carry[...]
            st_ref[j, 0] = prev.astype(ACT_DTYPE)
            out, new = _ret_chunk(q_ref[rs, :].astype(F32), k_ref[rs, :].astype(F32), v_ref[rs, :].astype(F32),
                                  g_ref[rs, :].astype(F32), prev, cos_ref[rs, :], sin_ref[rs, :], lg_ref[0, 0:1, 0:1])
            y_ref[rs, :] = out.astype(ACT_DTYPE)
            carry[...] = new

    return pl.pallas_call(
        body, name="ret_fwd", grid=(RET_HEADS, steps), in_specs=ins, out_specs=[row(RET_V), state()],
        out_shape=[jax.ShapeDtypeStruct((rows, RET_HEADS * RET_V), ACT_DTYPE),
                   jax.ShapeDtypeStruct((nc, RET_HEADS, RET_QK, RET_V), ACT_DTYPE)],
        scratch_shapes=[pltpu.VMEM((RET_QK, RET_V), F32)],
        compiler_params=_params(("parallel", "arbitrary")),
    )(q, k, v, g, cos, sin, lgam)


def _ret_bwd(qkvg, cos, sin, lgam, states, dy, dproj):
    rows = qkvg.shape[0]
    nc = rows // CHUNK
    per, steps, sidx, row, state, ins = _ret_specs(True, nc)

    def body(q_ref, k_ref, v_ref, g_ref, cos_ref, sin_ref, lg_ref, st_ref, dy_ref, dp_ref, o_ref, carry):
        @pl.when(pl.program_id(1) == 0)
        def _():
            carry[...] = jnp.zeros_like(carry)

        for j in reversed(range(per)):
            rs = pl.ds(j * CHUNK, CHUNK)
            fn = functools.partial(_ret_chunk, cos=cos_ref[rs, :], sin=sin_ref[rs, :], lg=lg_ref[0, 0:1, 0:1])
            _, vjp = jax.vjp(fn, q_ref[rs, :].astype(F32), k_ref[rs, :].astype(F32), v_ref[rs, :].astype(F32),
                             g_ref[rs, :].astype(F32), st_ref[j, 0].astype(F32))
            dq, dk, dv, dg, dprev = vjp((dy_ref[rs, :].astype(F32), carry[...]))
            o_ref[rs, 0:RET_QK] = dq.astype(ACT_DTYPE)
            o_ref[rs, RET_QK:2 * RET_QK] = dk.astype(ACT_DTYPE)
            o_ref[rs, 2 * RET_QK:2 * RET_QK + RET_V] = dv.astype(ACT_DTYPE)
            o_ref[rs, 2 * RET_QK + RET_V:RET_HW] = dg.astype(ACT_DTYPE)
            carry[...] = dprev

    return pl.pallas_call(
        body, name="ret_bwd", grid=(RET_HEADS, steps),
        in_specs=ins + [state(), row(RET_V), pl.BlockSpec(memory_space=pl.ANY)],
        out_specs=pl.BlockSpec((per * CHUNK, RET_HW), lambda h, s: (sidx(s), h)),
        out_shape=jax.ShapeDtypeStruct(dproj.shape, dproj.dtype), input_output_aliases={9: 0},
        scratch_shapes=[pltpu.VMEM((RET_QK, RET_V), F32)],
        compiler_params=_params(("parallel", "arbitrary")),
    )(qkvg, qkvg, qkvg, qkvg, cos, sin, lgam, states, dy, dproj)


def _branches_merge(y_ssd, y_ret, w_bs, w_br, gates):
    rows, k = y_ssd.shape
    d = w_bs.shape[1]
    tm = _row_tile(rows)

    def body(ys_ref, yr_ref, ws_ref, wr_ref, gs_ref, gr_ref, bs_ref, br_ref, o_ref):
        bs = _mxu(ys_ref[...], ws_ref[...], NN).astype(ACT_DTYPE)
        br = _mxu(yr_ref[...], wr_ref[...], NN).astype(ACT_DTYPE)
        bs_ref[...] = bs
        br_ref[...] = br
        o_ref[...] = (jax.nn.sigmoid(gs_ref[...].astype(F32)) * bs.astype(F32)
                      + jax.nn.sigmoid(gr_ref[...].astype(F32)) * br.astype(F32)).astype(ACT_DTYPE)

    spec = pl.BlockSpec((tm, d), lambda i: (i, 0))
    yspec = pl.BlockSpec((tm, k), lambda i: (i, 0))
    wspec = pl.BlockSpec((k, d), lambda i: (0, 0))
    shp = jax.ShapeDtypeStruct((rows, d), ACT_DTYPE)
    return pl.pallas_call(
        body, name="branches_merge", grid=(rows // tm,),
        in_specs=[yspec, yspec, wspec, wspec, spec, pl.BlockSpec((tm, d), lambda i: (i, 1))],
        out_specs=[spec, spec, spec], out_shape=[shp, shp, shp], compiler_params=_params(("parallel",)),
    )(y_ssd, y_ret, w_bs, w_br, gates, gates)


def _merge_bwd(dh1, w_o, bs, br, gates, dproj, col0):
    rows, d = bs.shape
    tm = _row_tile(rows)
    assert col0 % (2 * d) == 0

    def body(dh_ref, wo_ref, bs_ref, br_ref, gs_ref, gr_ref, dp_ref, dbs_ref, dbr_ref, dg_ref):
        dmv = _mxu(dh_ref[...], wo_ref[...], NT)
        for k, (b_ref, g_ref, db_ref) in enumerate(((bs_ref, gs_ref, dbs_ref), (br_ref, gr_ref, dbr_ref))):
            s = jax.nn.sigmoid(g_ref[...].astype(F32))
            db_ref[...] = (dmv * s).astype(ACT_DTYPE)
            dg_ref[:, k * d:(k + 1) * d] = (dmv * b_ref[...].astype(F32) * s * (1.0 - s)).astype(ACT_DTYPE)

    spec = pl.BlockSpec((tm, d), lambda i: (i, 0))
    shp = jax.ShapeDtypeStruct((rows, d), ACT_DTYPE)
    return pl.pallas_call(
        body, name="merge_bwd", grid=(rows // tm,),
        in_specs=[spec, pl.BlockSpec(w_o.shape, lambda i: (0, 0)), spec, spec, spec, pl.BlockSpec((tm, d), lambda i: (i, 1)),
                  pl.BlockSpec(memory_space=pl.ANY)],
        out_specs=[spec, spec, pl.BlockSpec((tm, 2 * d), lambda i: (i, col0 // (2 * d)))],
        out_shape=[shp, shp, jax.ShapeDtypeStruct(dproj.shape, dproj.dtype)], input_output_aliases={6: 2},
        compiler_params=_params(("parallel",)),
    )(dh1, w_o, bs, br, gates, gates, dproj)


def _place():
    x, y, c = lax.axis_index("x"), lax.axis_index("y"), lax.axis_index("c")
    return x, y, c


def _slot(p):
    return 4 * p[0] + 2 * p[1] + p[2]


def _allgather(arrs, name):
    n = len(arrs)
    any_spec = pl.BlockSpec(memory_space=pl.ANY)

    def body(*refs):
        ins, outs = refs[:n], refs[n:2 * n]
        send_sems, recv_sems, local_sems = refs[2 * n:]
        x, y, c = _place()
        me, sibling = (x, y, c), (x, y, 1 - c)
        chips = [(1 - x, y), (x, 1 - y), (1 - x, 1 - y)]

        def copy(a, k, block, to, src=None):
            dst = outs[a].at[_slot(block)]
            return pltpu.make_async_remote_copy(
                src_ref=dst if src is None else src, dst_ref=dst, send_sem=send_sems.at[a * 7 + k],
                recv_sem=recv_sems.at[a * 7 + k], device_id=to, device_id_type=MESH)

        mine, first, passed = [], [], []
        for a in range(n):
            cp = pltpu.make_async_copy(ins[a], outs[a].at[_slot(me)], local_sems.at[a])
            cp.start()
            mine.append(cp)
            first.append(copy(a, 0, me, sibling, src=ins[a]))
            first += [copy(a, 1 + j, me, (*chip, c), src=ins[a]) for j, chip in enumerate(chips)]
        for cp in first:
            cp.start()
        for j, chip in enumerate(chips):
            for a in range(n):
                copy(a, 1 + j, (*chip, c), me).wait_recv()
                cp = copy(a, 4 + j, (*chip, c), sibling)
                cp.start()
                passed.append(cp)
        for a in range(n):
            copy(a, 0, sibling, me).wait_recv()
            for j, chip in enumerate(chips):
                copy(a, 4 + j, (*chip, 1 - c), me).wait_recv()
        for cp in first + passed:
            cp.wait_send()
        for cp in mine:
            cp.wait()

    return pl.pallas_call(
        body, name=name, in_specs=[any_spec] * n, out_specs=[any_spec] * n,
        out_shape=[jax.ShapeDtypeStruct((N_DEV,) + a.shape, a.dtype) for a in arrs],
        scratch_shapes=[pltpu.SemaphoreType.DMA((7 * n,)), pltpu.SemaphoreType.DMA((7 * n,)), pltpu.SemaphoreType.DMA((n,))],
    )(*arrs)


def _peers():
    x, y, c = _place()
    return (x, y, c), [(x ^ dx, y ^ dy, c ^ dc) for dx in (0, 1) for dy in (0, 1) for dc in (0, 1)][1:]


def _exchange_copies(srcs, lands, send_sems, recv_sems, scatter, sender):
    me, peers = _peers()
    out = []
    for a, (src, land) in enumerate(zip(srcs, lands, strict=True)):
        for k, peer in enumerate(peers):
            src_ref = src.at[_slot(peer)] if scatter else src
            out.append(pltpu.make_async_remote_copy(
                src_ref=src_ref, dst_ref=land.at[_slot(me if sender else peer)], send_sem=send_sems.at[a * 7 + k],
                recv_sem=recv_sems.at[a * 7 + k], device_id=peer, device_id_type=MESH))
    return out


_HBM = pl.BlockSpec(memory_space=pltpu.HBM)
_SEM = pl.BlockSpec(memory_space=pltpu.SEMAPHORE)
_EFFECT = pltpu.SideEffectType.DATAFLOW_SIDE_EFFECTING


def _exchange_start(srcs, scatter, name, after=None):
    n = len(srcs)
    land_shapes = [s.shape if scatter else (N_DEV,) + s.shape for s in srcs]
    n_in = 2 * n + (after is not None)

    def body(*refs):
        for cp in _exchange_copies(refs[:n], refs[n:2 * n], refs[n_in], refs[n_in + 1], scatter, True):
            cp.start()
        refs[-1][...] = jnp.zeros_like(refs[-1])

    args = [pltpu.with_memory_space_constraint(s, pltpu.HBM) for s in srcs]
    args += [pltpu.with_memory_space_constraint(lax.empty(shp, s.dtype), pltpu.HBM) for s, shp in zip(srcs, land_shapes)]
    thru_shapes = tuple(pltpu.HBM(a.shape, a.dtype) for a in args)
    extra = [] if after is None else [after]
    outs = pl.pallas_call(
        body, name=name,
        out_shape=(pltpu.SemaphoreType.DMA((7 * n,)), pltpu.SemaphoreType.DMA((7 * n,))) + thru_shapes
        + (jax.ShapeDtypeStruct((8, LANES), F32),),
        in_specs=[_HBM] * (2 * n) + [pl.BlockSpec(memory_space=pl.ANY)] * len(extra),
        out_specs=(_SEM, _SEM) + (_HBM,) * (2 * n) + (pl.BlockSpec(memory_space=pltpu.VMEM),),
        input_output_aliases={i: 2 + i for i in range(2 * n)},
        compiler_params=pltpu.CompilerParams(has_side_effects=_EFFECT),
    )(*args, *extra)
    return outs[:-1], outs[-1]


def _exchange_wait(handle, scatter, after, name):
    n = (len(handle) - 2) // 2
    thru = handle[2:]

    def body(*refs):
        for cp in _exchange_copies(refs[:n], refs[n:2 * n], refs[2 * n], refs[2 * n + 1], scatter, False):
            cp.wait_send()
            cp.wait_recv()

    outs = pl.pallas_call(
        body, name=name, out_shape=tuple(pltpu.HBM(t.shape, t.dtype) for t in thru),
        in_specs=[_HBM] * (2 * n) + [_SEM, _SEM, pl.BlockSpec(memory_space=pl.ANY)], out_specs=(_HBM,) * (2 * n),
        input_output_aliases={i: i for i in range(2 * n)},
        compiler_params=pltpu.CompilerParams(has_side_effects=_EFFECT),
    )(*thru, handle[0], handle[1], after)
    return list(outs[:n]), list(outs[n:])


def _allreduce_small(pack):
    rows, lanes = pack.shape

    def body(x_ref, o_ref, buf, send_sems, recv_sems):
        x, y, c = _place()
        me, sibling = (x, y, c), (x, y, 1 - c)
        chips = [(1 - x, y), (x, 1 - y), (1 - x, 1 - y)]

        def copy(k, block, to, src=None):
            dst = buf.at[_slot(block)]
            return pltpu.make_async_remote_copy(
                src_ref=dst if src is None else src, dst_ref=dst, send_sem=send_sems.at[k], recv_sem=recv_sems.at[k],
                device_id=to, device_id_type=MESH)

        buf[_slot(me)] = x_ref[...]
        first = [copy(0, me, sibling, src=x_ref)]
        first += [copy(1 + j, me, (*chip, c), src=x_ref) for j, chip in enumerate(chips)]
        for cp in first:
            cp.start()
        passed = [copy(4 + j, (*chip, c), sibling) for j, chip in enumerate(chips)]
        for j, chip in enumerate(chips):
            copy(1 + j, (*chip, c), me).wait_recv()
            passed[j].start()
        copy(0, sibling, me).wait_recv()
        for j, chip in enumerate(chips):
            copy(4 + j, (*chip, 1 - c), me).wait_recv()
        for cp in first + passed:
            cp.wait_send()
        acc = buf[0]
        for i in range(1, N_DEV):
            acc = acc + buf[i]
        o_ref[...] = acc

    vmem = pl.BlockSpec(memory_space=pltpu.VMEM)
    return pl.pallas_call(
        body, name="allreduce_small", in_specs=[vmem], out_specs=vmem,
        out_shape=jax.ShapeDtypeStruct((rows, lanes), F32),
        scratch_shapes=[pltpu.VMEM((N_DEV, rows, lanes), F32), pltpu.SemaphoreType.DMA((7,)), pltpu.SemaphoreType.DMA((7,))],
        compiler_params=pltpu.CompilerParams(vmem_limit_bytes=VMEM_LIMIT),
    )(pack)


def _adamw(w, g, m, v):
    m = ADAM_B1 * m + (1.0 - ADAM_B1) * g
    v = ADAM_B2 * v + (1.0 - ADAM_B2) * jnp.square(g)
    m_hat = m / (1.0 - ADAM_B1 ** ADAM_STEP)
    v_hat = v / (1.0 - ADAM_B2 ** ADAM_STEP)
    delta = -ADAM_LR * (m_hat / (jnp.sqrt(v_hat) + ADAM_EPS) + ADAM_WD * w)
    return delta, m, v


def _adam_shard(own, parts, w, m, v, name):
    r, c = w.shape
    tr = _pick(r, (128, 64, 32, 16, 8))

    def body(own_ref, p_ref, w_ref, m_ref, v_ref, g_ref, d_ref, nm_ref, nv_ref):
        _, peers = _peers()
        g = own_ref[...].astype(F32)
        for peer in peers:
            g = g + p_ref[_slot(peer)].astype(F32)
        g_ref[...] = g
        d_ref[...], nm_ref[...], nv_ref[...] = _adamw(w_ref[...], g, m_ref[...], v_ref[...])

    spec = pl.BlockSpec((tr, c), lambda i: (i, 0))
    shp = jax.ShapeDtypeStruct((r, c), F32)
    return pl.pallas_call(
        body, name=name, grid=(r // tr,),
        in_specs=[spec, pl.BlockSpec((N_DEV, tr, c), lambda i: (0, i, 0)), spec, spec, spec], out_specs=[spec] * 4,
        out_shape=[shp] * 4, compiler_params=_params(("parallel",)),
    )(own, parts, w, m, v)


def _adam_small(w, g, m, v):
    r, c = w.shape

    def body(w_ref, g_ref, m_ref, v_ref, d_ref, nm_ref, nv_ref):
        d_ref[...], nm_ref[...], nv_ref[...] = _adamw(w_ref[...], g_ref[...], m_ref[...], v_ref[...])

    shp = jax.ShapeDtypeStruct((r, c), F32)
    return pl.pallas_call(body, name="adam_small", out_shape=[shp] * 3)(w, g, m, v)


def _column_plan(pieces, shard_w):
    plan = []
    for c0, width, d0 in pieces:
        c = c0
        while c < c0 + width:
            s, a = divmod(c, shard_w)
            w = min(c0 + width - c, shard_w - a)
            plan.append((s, a, w, d0 + c - c0))
            c += w
    return plan


def _cols_from_shards(g, plan, out_w, zero, name):
    _, r, sw = g.shape
    tr = _pick(r, (128,))

    def body(x_ref, o_ref):
        for d0, w in zero:
            o_ref[:, d0:d0 + w] = jnp.zeros((tr, w), g.dtype)
        for s, a, w, d0 in plan:
            o_ref[:, d0:d0 + w] = x_ref[s, :, a:a + w]

    return pl.pallas_call(
        body, name=name, grid=(r // tr,), in_specs=[pl.BlockSpec((N_DEV, tr, sw), lambda i: (0, i, 0))],
        out_specs=pl.BlockSpec((tr, out_w), lambda i: (i, 0)), out_shape=jax.ShapeDtypeStruct((r, out_w), g.dtype),
        compiler_params=_params(("parallel",)),
    )(g)


def _shards_from_cols(srcs, plans, shard_w, name):
    r = srcs[0].shape[0]
    tr = _pick(r, (128,))
    n = len(srcs)

    def body(*refs):
        o_ref = refs[n]
        for x_ref, plan in zip(refs[:n], plans, strict=True):
            for s, a, w, d0 in plan:
                o_ref[s, :, a:a + w] = x_ref[:, d0:d0 + w].astype(COMM_DTYPE)

    return pl.pallas_call(
        body, name=name, grid=(r // tr,), in_specs=[pl.BlockSpec((tr, t.shape[1]), lambda i: (i, 0)) for t in srcs],
        out_specs=pl.BlockSpec((N_DEV, tr, shard_w), lambda i: (0, i, 0)),
        out_shape=jax.ShapeDtypeStruct((N_DEV, r, shard_w), COMM_DTYPE), compiler_params=_params(("parallel",)),
    )(*srcs)


def _pack(arrs):
    rows = []
    for a in arrs:
        flat = a.reshape(-1).astype(F32)
        rows.append(jnp.pad(flat, (0, (-flat.shape[0]) % (8 * LANES))).reshape(-1, LANES))
    return jnp.concatenate(rows, axis=0)


def _unpack(pack, shapes):
    out, r = [], 0
    for s in shapes:
        size = math.prod(s)
        nr = -(-size // (8 * LANES)) * 8
        out.append(pack[r:r + nr].reshape(-1)[:size].reshape(s))
        r += nr
    return out


def _group_lanes(t):
    lead = t.shape[:-1]
    t = t.reshape(lead + (SSD_GROUPS, SSD_HPG))
    t = jnp.pad(t, [(0, 0)] * len(lead) + [(0, 0), (0, LANES - SSD_HPG)])
    return t.reshape(lead + (SSD_GROUPS * LANES,))


def _ungroup_lanes(t):
    lead = t.shape[:-1]
    return t.reshape(lead + (SSD_GROUPS, LANES))[..., :SSD_HPG].reshape(lead + (SSD_HEADS,))


def kernel(x, meta_tokens, mix_norm_w, w_in, ssd_conv_w, ssd_conv_b, ssd_dt_bias, ssd_A_log, ssd_D, ssd_norm_w, w_branch_ssd, w_branch_ret, w_out, ffn_norm_w, w_up, ffn_conv_w, ffn_conv_b, w_down, final_norm_w, loss_target, m_meta_tokens, m_mix_norm_w, m_w_in, m_ssd_conv_w, m_ssd_conv_b, m_ssd_dt_bias, m_ssd_A_log, m_ssd_D, m_ssd_norm_w, m_w_branch_ssd, m_w_branch_ret, m_w_out, m_ffn_norm_w, m_w_up, m_ffn_conv_w, m_ffn_conv_b, m_w_down, m_final_norm_w, v_meta_tokens, v_mix_norm_w, v_w_in, v_ssd_conv_w, v_ssd_conv_b, v_ssd_dt_bias, v_ssd_A_log, v_ssd_D, v_ssd_norm_w, v_w_branch_ssd, v_w_branch_ret, v_w_out, v_ffn_norm_w, v_w_up, v_ffn_conv_w, v_ffn_conv_b, v_w_down, v_final_norm_w):
    seq, d = x.shape[1], x.shape[2]
    rows = seq + PAD_ROWS
    tm = _row_tile(rows)
    me = _slot(_place())
    d_ff = w_down.shape[1] * N_DEV

    big = [w_in[0], w_branch_ssd[0], w_branch_ret[0], w_out[0], w_up[0], w_down[0]]
    first = _allgather([w_in[0].astype(COMM_DTYPE), meta_tokens, ssd_conv_w[0], ffn_conv_w[0]], "gather_first")
    rest_src = [b.astype(COMM_DTYPE) for b in big[1:]]
    rest_handle, rest_token = _exchange_start(rest_src, False, "gather_rest_start", after=first[0])
    cols = lambda t: jnp.transpose(t, (1, 0, 2)).reshape(t.shape[1], -1)
    rws = lambda t: t.reshape(-1, t.shape[2])
    conv_w, fconv_w = cols(first[2]), cols(first[3])
    meta_full = cols(first[1]) + rest_token[0, 0]
    widths = [SSD_D_INNER, SSD_CONV_DIM, SSD_HEADS, RET_HEADS * RET_QK, RET_HEADS * RET_QK, RET_HEADS * RET_V,
              RET_HEADS * RET_V, d, d]
    offs = [0]
    for wd in widths:
        offs.append(offs[-1] + wd)
    r0, z0 = 0, RET_HEADS * RET_HW
    g0 = z0 + widths[0]
    x0 = g0 + 2 * d
    dt0 = x0 + widths[1]
    in_p = dt0 + SSD_GROUPS * LANES
    pieces = []
    for hd in range(RET_HEADS):
        base = r0 + hd * RET_HW
        pieces += [(offs[3] + hd * RET_QK, RET_QK, base), (offs[4] + hd * RET_QK, RET_QK, base + RET_QK),
                   (offs[5] + hd * RET_V, RET_V, base + 2 * RET_QK), (offs[6] + hd * RET_V, RET_V, base + 2 * RET_QK + RET_V)]
    pieces += [(offs[0], widths[0], z0), (offs[7], d, g0), (offs[8], d, g0 + d), (offs[1], widths[1], x0)]
    pieces += [(offs[2] + SSD_HPG * grp, SSD_HPG, dt0 + LANES * grp) for grp in range(SSD_GROUPS)]
    in_plan = _column_plan(pieces, w_in.shape[2])
    w_in_p = _cols_from_shards(first[0], in_plan, in_p, [(dt0, SSD_GROUPS * LANES)], "w_in_columns")

    h0 = jnp.concatenate([jnp.zeros((FRONT, d), F32), meta_full, x[0]], axis=0)
    u1 = _rms_fwd(h0, mix_norm_w, "rms1")
    in_proj = lambda c0, width, dtype, nm: _mm(u1, w_in_p, mode="nn", out_dtype=dtype, tm=tm, tk=d, name="in_proj_" + nm,
                                               tn=_pick(width, (1024, 512)), b_n0=c0, n_out=width)
    qkvg = in_proj(r0, RET_HEADS * RET_HW, ACT_DTYPE, "qkvg")
    z = in_proj(z0, widths[0], ACT_DTYPE, "z")
    gates = in_proj(g0, 2 * d, ACT_DTYPE, "gates")
    dtr = in_proj(dt0, SSD_GROUPS * LANES, F32, "dt")
    xbc, xbc_c = _xbc_proj_conv(u1, w_in_p, x0, conv_w, ssd_conv_b)
    bias_p, alog_p, dsk_p = _group_lanes(ssd_dt_bias), _group_lanes(ssd_A_log), _group_lanes(ssd_D)
    y_ssd, y_scan, ssd_states = _ssd_fwd(xbc_c, dtr, z, bias_p, alog_p, dsk_p, ssd_norm_w)
    cos, sin, lgam = _rotary_tables(rows)
    y_ret, ret_states = _ret_fwd(qkvg, cos, sin, lgam)
    rest_own, rest = _exchange_wait(rest_handle, False, y_ret, "gather_rest_wait")
    rest = [lax.dynamic_update_index_in_dim(land, own, me, 0) for land, own in zip(rest, rest_own, strict=True)]
    w_bs, w_br, w_o, w_dn = rws(rest[0]), rws(rest[1]), rws(rest[2]), rws(rest[4])
    w_up_f = _cols_from_shards(rest[3], _column_plan([(0, 2 * d_ff, 0)], w_up.shape[2]), 2 * d_ff, [], "w_up_columns")
    bs, br, merged = _branches_merge(y_ssd, y_ret, w_bs, w_br, gates)
    h1, u2 = _out_proj_norm(merged, w_o, h0, ffn_norm_w)
    up_g, up_v, act = _ffn_up_conv(u2, w_up_f, fconv_w, ffn_conv_b)
    tgt = jnp.pad(loss_target[0], ((PAD_ROWS, 0), (0, 0)))
    dh2, loss_acc, g_final = _down_loss_head(act, w_dn, h1, tgt, final_norm_w.reshape(1, d))

    tff = _pick(d_ff, (1408, 256))
    tkr = _pick(rows, (1664, 128))
    tkr2 = _pick(rows, (4160, 128))
    rparts = lambda t: t.reshape(N_DEV, -1, t.shape[1])
    d_act = _mm(dh2, w_dn, mode="nt", out_dtype=ACT_DTYPE, tm=tm, tn=tff, tk=d, name="d_act")
    g_w_dn = _mm(act, dh2, mode="tn", out_dtype=COMM_DTYPE, tm=tff, tn=d, tk=tkr, name="g_w_down")
    c_dn = [rparts(g_w_dn)]
    h_dn, t_dn = _exchange_start(c_dn, True, "scatter_down_start")
    d_up_g, d_up_v, g_fcw_g, g_fcb_g, g_fcw_v, g_fcb_v, g_w_up_g, g_w_up_v = _ffn_conv_bwd(
        up_g, up_v, d_act, fconv_w, ffn_conv_b + t_dn[0, 0], u2)
    g_fconv_w = jnp.concatenate([g_fcw_g, g_fcw_v], axis=1)
    g_fconv_b = jnp.concatenate([g_fcb_g, g_fcb_v], axis=1)
    c_up = [_shards_from_cols([g_w_up_g, g_w_up_v], [_column_plan([(0, d_ff, 0)], w_up.shape[2]),
                                                     _column_plan([(d_ff, d_ff, 0)], w_up.shape[2])], w_up.shape[2], "g_w_up_shards")]
    h_up, t_up = _exchange_start(c_up, True, "scatter_up_start")
    du2 = _mm(d_up_g, w_up_f, mode="nt", out_dtype=F32, tm=tm, tn=d, tk=d_ff, name="d_u2_gate", after=t_up)
    du2 = _mm(d_up_v, w_up_f, mode="nt", out_dtype=F32, tm=tm, tn=d, tk=d_ff, name="d_u2_value", add=du2, b_k0=d_ff)
    dh1, g_ffn_norm = _rms_bwd(du2, h1, ffn_norm_w, dh2, "rms2_bwd")
    g_w_o =_mm(merged, dh1, mode="tn", out_dtype=COMM_DTYPE, tm=d, tn=d, tk=tkr, name="g_w_out")
    dproj = lax.empty((rows, in_p), ACT_DTYPE)
    d_bs, d_br, dproj = _merge_bwd(dh1, w_o, bs, br, gates, dproj, g0)
    g_w_bs = _mm(y_ssd, d_bs, mode="tn", out_dtype=COMM_DTYPE, tm=1024, tn=d, tk=tkr2, name="g_w_branch_ssd")
    g_w_br = _mm(y_ret, d_br, mode="tn", out_dtype=COMM_DTYPE, tm=1024, tn=d, tk=tkr2, name="g_w_branch_ret")
    c_mid = [rparts(g_w_bs), rparts(g_w_br), rparts(g_w_o)]
    h_mid, t_mid = _exchange_start(c_mid, True, "scatter_mid_start")
    d_yscan, dproj, g_nw = _ssd_gate_bwd(y_scan, z, d_bs, w_bs, ssd_norm_w + t_mid[0, 0], dproj, z0)
    dxs, d_bm, d_cm, dproj, g_bias_p, g_alog_p, g_dsk_p = _ssd_bwd(
        xbc_c, dtr, bias_p, alog_p, dsk_p, ssd_states, d_yscan, dproj, dt0)
    dproj, g_conv_w, g_conv_b = _ssd_conv_bwd(xbc, dxs, d_bm, d_cm, conv_w, ssd_conv_b, dproj, x0)
    d_yret = _mm(d_br, w_br, mode="nt", out_dtype=ACT_DTYPE, tm=tm, tn=1024, tk=d, name="d_y_ret")
    dproj = _ret_bwd(qkvg, cos, sin, lgam, ret_states, d_yret, dproj)
    g_w_in_p = _mm(u1, dproj, mode="tn", out_dtype=F32, tm=d, tn=_pick(in_p, (768, 512)), tk=tkr2, name="g_w_in")
    c_in = [_shards_from_cols([g_w_in_p], [in_plan], w_in.shape[2], "g_w_in_shards")]
    h_in, t_in = _exchange_start(c_in, True, "scatter_in_start")
    du1 = _mm(dproj, w_in_p, mode="nt", out_dtype=F32, tm=tm, tn=d, tk=_pick(in_p, (4608, 512)), name="d_u1", after=t_in)
    dh0, g_mix_norm = _rms_bwd(du1, h0, mix_norm_w, dh1, "rms1_bwd")
    grad_x = dh0[PAD_ROWS:][None]

    landed = {}
    for key, handle, names in (("in", h_in, ["w_in"]), ("mid", h_mid, ["w_branch_ssd", "w_branch_ret", "w_out"]),
                               ("up", h_up, ["w_up"]), ("down", h_dn, ["w_down"])):
        srcs, lands = _exchange_wait(handle, True, dh0, f"scatter_{key}_wait")
        for nm, land, src in zip(names, lands, srcs, strict=True):
            landed[nm] = (lax.dynamic_index_in_dim(src, me, 0, keepdims=False), land)
    big_m = [m_w_in, m_w_branch_ssd, m_w_branch_ret, m_w_out, m_w_up, m_w_down]
    big_v = [v_w_in, v_w_branch_ssd, v_w_branch_ret, v_w_out, v_w_up, v_w_down]
    big_names = ["w_in", "w_branch_ssd", "w_branch_ret", "w_out", "w_up", "w_down"]
    big_out = {}
    for nm, w, m, v_ in zip(big_names, big, big_m, big_v, strict=True):
        big_out[nm] = [t[None] for t in _adam_shard(*landed[nm], w, m[0], v_[0], "adam_" + nm)]

    small_g = [dh0[FRONT:PAD_ROWS], g_mix_norm, g_conv_w, g_conv_b, _ungroup_lanes(g_bias_p), _ungroup_lanes(g_alog_p),
               _ungroup_lanes(g_dsk_p), g_nw, g_ffn_norm, g_fconv_w, g_fconv_b, g_final, loss_acc[0:1, 0:1]]
    total = _unpack(_allreduce_small(_pack(small_g)), [t.shape for t in small_g])
    loss = total[12].reshape(())
    shard = lambda t, width: lax.dynamic_slice_in_dim(t, me * width, width, axis=1)
    small_names = ["meta_tokens", "mix_norm_w", "ssd_conv_w", "ssd_conv_b", "ssd_dt_bias", "ssd_A_log", "ssd_D", "ssd_norm_w",
                   "ffn_norm_w", "ffn_conv_w", "ffn_conv_b", "final_norm_w"]
    small_w = [meta_tokens, mix_norm_w, ssd_conv_w, ssd_conv_b, ssd_dt_bias, ssd_A_log, ssd_D, ssd_norm_w, ffn_norm_w,
               ffn_conv_w, ffn_conv_b, final_norm_w]
    small_m = [m_meta_tokens, m_mix_norm_w, m_ssd_conv_w, m_ssd_conv_b, m_ssd_dt_bias, m_ssd_A_log, m_ssd_D, m_ssd_norm_w,
               m_ffn_norm_w, m_ffn_conv_w, m_ffn_conv_b, m_final_norm_w]
    small_v = [v_meta_tokens, v_mix_norm_w, v_ssd_conv_w, v_ssd_conv_b, v_ssd_dt_bias, v_ssd_A_log, v_ssd_D, v_ssd_norm_w,
               v_ffn_norm_w, v_ffn_conv_w, v_ffn_conv_b, v_final_norm_w]
    grads = total[:12]
    grads[0] = shard(grads[0], meta_tokens.shape[1])
    grads[2] = shard(grads[2], ssd_conv_w.shape[2])
    grads[9] = shard(grads[9], ffn_conv_w.shape[2])
    grads = [t.reshape(w.shape) for t, w in zip(grads, small_w, strict=True)]
    shapes = [w.shape for w in small_w]
    upd = _adam_small(_pack(small_w), _pack(grads), _pack(small_m), _pack(small_v))
    small_out = {nm: [gr_] + [u[i] for u in (_unpack(t, shapes) for t in upd)]
                 for i, (nm, gr_) in enumerate(zip(small_names, grads, strict=True))}

    order = ["meta_tokens", "mix_norm_w", "w_in", "ssd_conv_w", "ssd_conv_b", "ssd_dt_bias", "ssd_A_log", "ssd_D", "ssd_norm_w",
             "w_branch_ssd", "w_branch_ret", "w_out", "ffn_norm_w", "w_up", "ffn_conv_w", "ffn_conv_b", "w_down", "final_norm_w"]
    res = {**big_out, **small_out}
    return (loss, grad_x, *[res[nm][0] for nm in order], *[res[nm][1] for nm in order], *[res[nm][2] for nm in order],
            *[res[nm][3] for nm in order])
```

```python
import functools
import math

import jax
import jax.numpy as jnp
import numpy as np
from jax import lax
from jax.experimental import pallas as pl
from jax.experimental.pallas import tpu as pltpu

F32 = jnp.float32
MXU_DTYPE = jnp.bfloat16
ACT_DTYPE = jnp.bfloat16
COMM_DTYPE = jnp.bfloat16

N_META = 16
CHUNK = 128
FRONT = CHUNK - N_META
PAD_ROWS = FRONT + N_META
EPS = 1e-6
N_DEV = 8

SSD_D_INNER = 2048
SSD_HEAD_DIM = 64
SSD_HEADS = 32
SSD_GROUPS = 4
SSD_HPG = SSD_HEADS // SSD_GROUPS
SSD_STATE = 128
SSD_CONV = 4
SSD_CONV_DIM = SSD_D_INNER + 2 * SSD_GROUPS * SSD_STATE
SSD_GW = SSD_D_INNER // SSD_GROUPS
RET_HEADS = 4
RET_QK = 256
RET_V = 512
RET_HW = 2 * RET_QK + 2 * RET_V
ROPE_BASE = 10000.0
FFN_CONV = 3
HALO = 16
LANES = 128

ADAM_LR = 0.001
ADAM_B1 = 0.9
ADAM_B2 = 0.999
ADAM_EPS = 1e-08
ADAM_WD = 0.01
ADAM_STEP = 10

VMEM_LIMIT = 56 * 1024 * 1024
MESH = pl.DeviceIdType.MESH

NN = (((1,), (0,)), ((), ()))
NT = (((1,), (1,)), ((), ()))
TN = (((0,), (0,)), ((), ()))


def _params(sem):
    return pltpu.CompilerParams(dimension_semantics=sem, vmem_limit_bytes=VMEM_LIMIT)


def _mxu(a, b, dn):
    return lax.dot_general(a.astype(MXU_DTYPE), b.astype(MXU_DTYPE), dn, preferred_element_type=F32)


@functools.partial(jax.custom_vjp, nondiff_argnums=(2,))
def _dot(a, b, dn=NN):
    return _mxu(a, b, dn)


def _dot_fwd(a, b, dn):
    return _mxu(a, b, dn), (a, b)


def _dot_bwd(dn, res, g):
    a, b = res
    if dn == NN:
        return _mxu(g, b, NT), _mxu(a, g, TN)
    if dn == NT:
        return _mxu(g, b, NN), _mxu(g, a, TN)
    assert dn == TN
    return _mxu(b, g, NT), _mxu(a, g, NN)


_dot.defvjp(_dot_fwd, _dot_bwd)


def _silu(x):
    return x * jax.nn.sigmoid(x)


def _dsilu(x):
    s = jax.nn.sigmoid(x)
    return s * (1.0 + x * (1.0 - s))


def _row_tile(rows):
    return 640 if rows % 640 == 0 else 128


def _mm(a, b, *, mode, out_dtype, tm, tn, tk, name, add=None, after=None, b_k0=0, b_n0=0, n_out=None):
    if mode == "nt":
        (m, k), n = a.shape, b.shape[0]
        k2 = k if b_k0 % tk == 0 and b_k0 + k <= b.shape[1] else None
    else:
        (m, k) = a.shape if mode == "nn" else a.shape[::-1]
        k2 = b.shape[0]
        n = b.shape[1] if n_out is None else n_out
        assert b_n0 % tn == 0 and b_n0 + n <= b.shape[1]
    assert (b_k0 == 0 or mode == "nt") and ((b_n0 == 0 and n_out is None) or mode != "nt")
    assert k == k2 and m % tm == 0 and n % tn == 0 and k % tk == 0, (name, a.shape, b.shape, tm, tn, tk)
    kb0, nb0 = b_k0 // tk, b_n0 // tn
    nk = k // tk
    dn = {"nn": NN, "nt": NT, "tn": TN}[mode]
    has_add = add is not None
    n_in = 2 + has_add + (after is not None)

    def body(*refs):
        a_ref, b_ref = refs[0], refs[1]
        add_ref = refs[2] if has_add else None
        o_ref = refs[n_in]
        p = _dot(a_ref[...], b_ref[...], dn)
        if nk == 1:
            if has_add:
                p = p + add_ref[...]
            o_ref[...] = p.astype(out_dtype)
        else:
            acc_ref = refs[n_in + 1]
            kk = pl.program_id(2)

            @pl.when(kk == 0)
            def _():
                acc_ref[...] = p

            @pl.when(kk > 0)
            def _():
                acc_ref[...] += p

            @pl.when(kk == nk - 1)
            def _():
                r = acc_ref[...]
                if has_add:
                    r = r + add_ref[...]
                o_ref[...] = r.astype(out_dtype)

    if mode == "tn":
        a_spec = pl.BlockSpec((tk, tm), lambda j, i, kk: (kk, i))
    else:
        a_spec = pl.BlockSpec((tm, tk), lambda j, i, kk: (i, kk))
    if mode == "nt":
        b_spec = pl.BlockSpec((tn, tk), lambda j, i, kk: (j, kk + kb0))
    else:
        b_spec = pl.BlockSpec((tk, tn), lambda j, i, kk: (kk, j + nb0))
    o_spec = pl.BlockSpec((tm, tn), lambda j, i, kk: (i, j))
    in_specs = [a_spec, b_spec] + ([o_spec] if has_add else [])
    args = (a, b) + ((add,) if has_add else ())
    if after is not None:
        in_specs.append(pl.BlockSpec(memory_space=pl.ANY))
        args += (after,)
    return pl.pallas_call(
        body, name=name, grid=(n // tn, m // tm, nk), in_specs=in_specs, out_specs=o_spec,
        out_shape=jax.ShapeDtypeStruct((m, n), out_dtype),
        scratch_shapes=[pltpu.VMEM((tm, tn), F32)] if nk > 1 else [],
        compiler_params=_params(("parallel", "parallel", "arbitrary")),
    )(*args)


def _pick(n, cands):
    for c in cands:
        if n % c == 0:
            return c
    return n


def _rms_fwd(h, w, name):
    rows, d = h.shape
    tm = _row_tile(rows)

    def body(h_ref, w_ref, u_ref):
        x = h_ref[...]
        r = lax.rsqrt(jnp.mean(x * x, axis=-1, keepdims=True) + EPS)
        u_ref[...] = (x * r * w_ref[...]).astype(ACT_DTYPE)

    return pl.pallas_call(
        body, name=name, grid=(rows // tm,),
        in_specs=[pl.BlockSpec((tm, d), lambda i: (i, 0)), pl.BlockSpec((1, d), lambda i: (0, 0))],
        out_specs=pl.BlockSpec((tm, d), lambda i: (i, 0)),
        out_shape=jax.ShapeDtypeStruct((rows, d), ACT_DTYPE),
        compiler_params=_params(("parallel",)),
    )(h, w)


def _proj_norm_bwd(parts, w, tk, h, nw, dres, name, after=None):
    rows, d = h.shape
    tm = _row_tile(rows)
    steps = [p.shape[1] // tk for p in parts]
    starts = [sum(steps[:p]) for p in range(len(parts))]
    nk = sum(steps)
    assert all(p.shape[1] % tk == 0 for p in parts) and nk * tk == w.shape[1]
    n = len(parts)

    def body(*refs):
        a_refs, w_ref, h_ref, nw_ref, dres_ref = refs[:n], refs[n], refs[n + 1], refs[n + 2], refs[n + 3]
        dh_ref, dw_ref, acc = refs[-3], refs[-2], refs[-1]
        i, kk = pl.program_id(0), pl.program_id(1)
        for p in range(n):
            @pl.when((kk >= starts[p]) & (kk < starts[p] + steps[p]))
            def _(p=p):
                part = _mxu(a_refs[p][...], w_ref[...], NT)
                if starts[p] == 0:
                    @pl.when(kk == 0)
                    def _():
                        acc[...] = part

                    @pl.when(kk > 0)
                    def _():
                        acc[...] += part
                else:
                    acc[...] += part

        @pl.when(kk == nk - 1)
        def _():
            x, dy = h_ref[...], acc[...]
            r = lax.rsqrt(jnp.mean(x * x, axis=-1, keepdims=True) + EPS)
            xhat = x * r
            dxn = dy * nw_ref[...]
            dh_ref[...] = dres_ref[...] + r * (dxn - xhat * jnp.mean(dxn * xhat, axis=-1, keepdims=True))

            @pl.when(i == 0)
            def _():
                dw_ref[...] = jnp.zeros_like(dw_ref)

            dw_ref[...] += jnp.sum(dy * xhat, axis=0, keepdims=True)

    clip = lambda kk, p: jnp.clip(kk - starts[p], 0, steps[p] - 1)
    row = pl.BlockSpec((tm, d), lambda i, kk: (i, 0))
    in_specs = [pl.BlockSpec((tm, tk), lambda i, kk, p=p: (i, clip(kk, p))) for p in range(n)]
    in_specs += [pl.BlockSpec((d, tk), lambda i, kk: (0, kk)), row, pl.BlockSpec((1, d), lambda i, kk: (0, 0)), row]
    args = (*parts, w, h, nw, dres)
    if after is not None:
        in_specs.append(pl.BlockSpec(memory_space=pl.ANY))
        args += (after,)
    return pl.pallas_call(
        body, name=name, grid=(rows // tm, nk), in_specs=in_specs,
        out_specs=[row, pl.BlockSpec((1, d), lambda i, kk: (0, 0))],
        out_shape=[jax.ShapeDtypeStruct((rows, d), F32), jax.ShapeDtypeStruct((1, d), F32)],
        scratch_shapes=[pltpu.VMEM((tm, d), F32)], compiler_params=_params(("arbitrary", "arbitrary")),
    )(*args)


def _out_proj_norm(merged, w_o, h0, nw):
    rows, d = h0.shape
    tm = _row_tile(rows)

    def body(m_ref, w_ref, h_ref, nw_ref, h1_ref, u_ref):
        x = h_ref[...] + _mxu(m_ref[...], w_ref[...], NN)
        h1_ref[...] = x
        r = lax.rsqrt(jnp.mean(x * x, axis=-1, keepdims=True) + EPS)
        u_ref[...] = (x * r * nw_ref[...]).astype(ACT_DTYPE)

    spec = pl.BlockSpec((tm, d), lambda i: (i, 0))
    return pl.pallas_call(
        body, name="out_proj_norm", grid=(rows // tm,),
        in_specs=[pl.BlockSpec((tm, merged.shape[1]), lambda i: (i, 0)), pl.BlockSpec(w_o.shape, lambda i: (0, 0)), spec,
                  pl.BlockSpec((1, d), lambda i: (0, 0))],
        out_specs=[spec, spec], out_shape=[jax.ShapeDtypeStruct((rows, d), F32), jax.ShapeDtypeStruct((rows, d), ACT_DTYPE)],
        compiler_params=_params(("parallel",)),
    )(merged, w_o, h0, nw)


def _down_loss_head(act, w_dn, h1, tgt, w):
    rows, d = h1.shape
    tm = _row_tile(rows)

    def body(a_ref, wd_ref, h_ref, t_ref, w_ref, dh_ref, loss_ref, dw_ref):
        i = pl.program_id(0)
        x = h_ref[...] + _mxu(a_ref[...], wd_ref[...], NN)
        r = lax.rsqrt(jnp.mean(x * x, axis=-1, keepdims=True) + EPS)
        xhat = x * r
        wv = w_ref[...]
        row = i * tm + lax.broadcasted_iota(jnp.int32, (tm, 1), 0)
        live = row >= PAD_ROWS
        diff = jnp.where(live, xhat * wv - t_ref[...], 0.0)
        dy = diff * (1.0 / d)
        dxn = dy * wv
        dh_ref[...] = r * (dxn - xhat * jnp.mean(dxn * xhat, axis=-1, keepdims=True))

        @pl.when(i == 0)
        def _():
            loss_ref[...] = jnp.zeros_like(loss_ref)
            dw_ref[...] = jnp.zeros_like(dw_ref)

        loss_ref[...] += 0.5 * jnp.sum(jnp.mean(diff * diff, axis=-1, keepdims=True))
        dw_ref[...] += jnp.sum(dy * xhat, axis=0, keepdims=True)

    return pl.pallas_call(
        body, name="down_loss_head", grid=(rows // tm,),
        in_specs=[pl.BlockSpec((tm, act.shape[1]), lambda i: (i, 0)), pl.BlockSpec(w_dn.shape, lambda i: (0, 0)),
                  pl.BlockSpec((tm, d), lambda i: (i, 0)), pl.BlockSpec((tm, d), lambda i: (i, 0)),
                  pl.BlockSpec((1, d), lambda i: (0, 0))],
        out_specs=[pl.BlockSpec((tm, d), lambda i: (i, 0)), pl.BlockSpec((8, LANES), lambda i: (0, 0)),
                   pl.BlockSpec((1, d), lambda i: (0, 0))],
        out_shape=[jax.ShapeDtypeStruct((rows, d), F32), jax.ShapeDtypeStruct((8, LANES), F32),
                   jax.ShapeDtypeStruct((1, d), F32)],
        compiler_params=_params(("arbitrary",)),
    )(act, w_dn, h1, tgt, w)


def _prev_halo_spec(tm, width, col):
    return pl.BlockSpec((HALO, width), lambda j, i: (jnp.maximum(i * (tm // HALO) - 1, 0), col(j)))


def _next_halo_spec(tm, rows, width, col):
    last = rows // HALO - 1
    return pl.BlockSpec((HALO, width), lambda j, i: (jnp.minimum((i + 1) * (tm // HALO), last), col(j)))


def _conv_taps(cat, w_ref, b_ref, kw):
    acc = b_ref[...] + w_ref[kw - 1:kw, :] * cat
    for s in range(1, kw):
        acc = acc + w_ref[kw - 1 - s:kw - s, :] * pltpu.roll(cat, s, 0)
    return acc


def _conv_back(dpre, w_ref, kw):
    n = dpre.shape[0]
    acc = w_ref[kw - 1:kw, :] * dpre
    for s in range(1, kw):
        acc = acc + w_ref[kw - 1 - s:kw - s, :] * pltpu.roll(dpre, n - s, 0)
    return acc


def _xbc_proj_conv(u1, w_in_p, col0, w, b):
    rows, d = u1.shape
    width = w.shape[1]
    tm, tc = _row_tile(rows), 512
    cb0 = col0 // tc
    assert col0 % tc == 0

    def body(u_ref, m_ref, w_ref, b_ref, x_ref, o_ref, carry):
        i = pl.program_id(1)

        @pl.when(i == 0)
        def _():
            carry[...] = jnp.zeros_like(carry)

        xb = _mxu(u_ref[...], m_ref[...], NN).astype(ACT_DTYPE)
        x_ref[...] = xb
        x = xb.astype(F32)
        cat = jnp.concatenate([carry[...], x], axis=0)
        carry[...] = x[tm - HALO:, :]
        pre = _conv_taps(cat, w_ref, b_ref, SSD_CONV)[HALO:]
        row = i * tm + lax.broadcasted_iota(jnp.int32, (tm, 1), 0)
        o_ref[...] = jnp.where(row >= FRONT, _silu(pre), 0.0).astype(ACT_DTYPE)

    main = pl.BlockSpec((tm, tc), lambda j, i: (i, j))
    par = lambda r: pl.BlockSpec((r, tc), lambda j, i: (0, j))
    act = jax.ShapeDtypeStruct((rows, width), ACT_DTYPE)
    return pl.pallas_call(
        body, name="xbc_proj_conv", grid=(width // tc, rows // tm),
        in_specs=[pl.BlockSpec((tm, d), lambda j, i: (i, 0)), pl.BlockSpec((d, tc), lambda j, i: (0, cb0 + j)),
                  par(SSD_CONV), par(1)],
        out_specs=[main, main], out_shape=[act, act], scratch_shapes=[pltpu.VMEM((HALO, tc), F32)],
        compiler_params=_params(("parallel", "arbitrary")),
    )(u1, w_in_p, w, b)


def _ssd_conv_bwd(xbc, dxs, dbm, dcm, w, b, dproj, col0):
    rows, width = xbc.shape
    tm, tc = _row_tile(rows), 512
    kw = SSD_CONV
    nx = dxs.shape[1] // tc
    assert dbm.shape[1] == tc and dcm.shape[1] == tc and width == (nx + 2) * tc and col0 % tc == 0

    def body(x_ref, xp_ref, xn_ref, d0_ref, d0n_ref, d1_ref, d1n_ref, d2_ref, d2n_ref, w_ref, b_ref, dp_ref,
             dx_ref, dw_ref, db_ref):
        j, i = pl.program_id(0), pl.program_id(1)
        xp = jnp.where(i == 0, 0.0, xp_ref[...].astype(F32))
        cat = jnp.concatenate([xp, x_ref[...].astype(F32), xn_ref[...].astype(F32)], axis=0)
        sh = [cat] + [pltpu.roll(cat, s, 0) for s in range(1, kw)]
        pre = b_ref[...] + w_ref[kw - 1:kw, :] * sh[0]
        for s in range(1, kw):
            pre = pre + w_ref[kw - 1 - s:kw - s, :] * sh[s]
        pre = pre[HALO:]
        row = i * tm + lax.broadcasted_iota(jnp.int32, (tm + HALO, 1), 0)
        live = (row >= FRONT) & (row < rows)
        pick = lambda a, bb, c: jnp.where(j < nx, a[...], jnp.where(j == nx, bb[...], c[...])).astype(F32)
        dout = jnp.concatenate([pick(d0_ref, d1_ref, d2_ref), pick(d0n_ref, d1n_ref, d2n_ref)], axis=0)
        dpre = jnp.where(live, dout * _dsilu(pre), 0.0)
        dx_ref[...] = _conv_back(dpre, w_ref, kw)[:tm].astype(ACT_DTYPE)

        @pl.when(i == 0)
        def _():
            dw_ref[...] = jnp.zeros_like(dw_ref)
            db_ref[...] = jnp.zeros_like(db_ref)

        dmain = dpre[:tm]
        db_ref[...] += jnp.sum(dmain, axis=0, keepdims=True)
        for k in range(kw):
            dw_ref[k:k + 1, :] += jnp.sum(dmain * sh[kw - 1 - k][HALO:HALO + tm], axis=0, keepdims=True)

    main = pl.BlockSpec((tm, tc), lambda j, i: (i, j))
    par = lambda r: pl.BlockSpec((r, tc), lambda j, i: (0, j))
    col = lambda j: j
    xcol, zero = (lambda j: jnp.minimum(j, nx - 1)), (lambda j: 0)
    dspecs = lambda c: [pl.BlockSpec((tm, tc), lambda j, i: (i, c(j))), _next_halo_spec(tm, rows, tc, c)]
    return pl.pallas_call(
        body, name="ssd_conv_bwd", grid=(width // tc, rows // tm),
        in_specs=[main, _prev_halo_spec(tm, tc, col), _next_halo_spec(tm, rows, tc, col)]
        + dspecs(xcol) + dspecs(zero) + dspecs(zero) + [par(kw), par(1), pl.BlockSpec(memory_space=pl.ANY)],
        out_specs=[pl.BlockSpec((tm, tc), lambda j, i: (i, col0 // tc + j)), par(kw), par(1)],
        out_shape=[jax.ShapeDtypeStruct(dproj.shape, dproj.dtype), jax.ShapeDtypeStruct((kw, width), F32),
                   jax.ShapeDtypeStruct((1, width), F32)],
        input_output_aliases={11: 0}, compiler_params=_params(("parallel", "arbitrary")),
    )(xbc, xbc, xbc, dxs, dxs, dbm, dbm, dcm, dcm, w, b, dproj)


def _ffn_up_conv(u2, w_up, w, b):
    rows, d = u2.shape
    width = w_up.shape[1]
    dff = width // 2
    tm, tc = _row_tile(rows), _pick(dff, (256, 128))
    nb = dff // tc
    kw = FFN_CONV

    def body(u_ref, mg_ref, mv_ref, wg_ref, bg_ref, wv_ref, bv_ref, ug_ref, uv_ref, o_ref, cg, cv):
        i = pl.program_id(1)

        @pl.when(i == 0)
        def _():
            cg[...] = jnp.zeros_like(cg)
            cv[...] = jnp.zeros_like(cv)

        def pre(m_ref, up_ref, carry, w_ref, b_ref):
            upb = _mxu(u_ref[...], m_ref[...], NN).astype(ACT_DTYPE)
            up_ref[...] = upb
            x = upb.astype(F32)
            cat = jnp.concatenate([carry[...], x], axis=0)
            carry[...] = x[tm - HALO:, :]
            return _conv_taps(cat, w_ref, b_ref, kw)[HALO:]

        ag = pre(mg_ref, ug_ref, cg, wg_ref, bg_ref)
        av = pre(mv_ref, uv_ref, cv, wv_ref, bv_ref)
        o_ref[...] = (_silu(ag) * av).astype(ACT_DTYPE)

    gcol, vcol = (lambda j: j), (lambda j: j + nb)
    mat = lambda col: pl.BlockSpec((d, tc), lambda j, i: (0, col(j)))
    par = lambda r, col: pl.BlockSpec((r, tc), lambda j, i: (0, col(j)))
    out = pl.BlockSpec((tm, tc), lambda j, i: (i, j))
    act = jax.ShapeDtypeStruct((rows, dff), ACT_DTYPE)
    return pl.pallas_call(
        body, name="ffn_up_conv", grid=(nb, rows // tm),
        in_specs=[pl.BlockSpec((tm, d), lambda j, i: (i, 0)), mat(gcol), mat(vcol),
                  par(kw, gcol), par(1, gcol), par(kw, vcol), par(1, vcol)],
        out_specs=[out, out, out], out_shape=[act, act, act],
        scratch_shapes=[pltpu.VMEM((HALO, tc), F32), pltpu.VMEM((HALO, tc), F32)],
        compiler_params=_params(("parallel", "arbitrary")),
    )(u2, w_up, w_up, w, b, w, b)


def _ffn_conv_bwd(up_g, up_v, dact, w, b, u2):
    rows, dff = up_g.shape
    d = u2.shape[1]
    tm, tc = _row_tile(rows), _pick(dff, (256, 128))
    nb = dff // tc
    kw = FFN_CONV

    sb = 16

    def body(g_ref, gp_ref, gn_ref, v_ref, vp_ref, vn_ref, d_ref, dn_ref, wg_ref, bg_ref, wv_ref, bv_ref, u_ref,
             dxg_ref, dxv_ref, dwg_ref, dbg_ref, dwv_ref, dbv_ref, gwg_ref, gwv_ref, xg_s, xv_s, dd_s, og_s, ov_s):
        i = pl.program_id(1)
        last = i == rows // tm - 1
        for x_s, x_ref, xp_ref, xn_ref in ((xg_s, g_ref, gp_ref, gn_ref), (xv_s, v_ref, vp_ref, vn_ref)):
            x_s[0:HALO, :] = jnp.where(i == 0, 0.0, xp_ref[...].astype(F32))
            x_s[HALO:HALO + tm, :] = x_ref[...].astype(F32)
            x_s[HALO + tm:, :] = xn_ref[...].astype(F32)
        dd_s[0:tm, :] = d_ref[...].astype(F32)
        dd_s[tm:, :] = jnp.where(last, 0.0, dn_ref[...].astype(F32))

        wg = [wg_ref[k:k + 1, :] for k in range(kw)]
        wv = [wv_ref[k:k + 1, :] for k in range(kw)]
        bg, bv = bg_ref[...], bv_ref[...]

        def taps(x_s, e0, w, bias):
            win = x_s[pl.ds(e0 + HALO - sb, 2 * sb), :]
            sh = [win[sb:], pltpu.roll(win, 1, 0)[sb:], pltpu.roll(win, 2, 0)[sb:]]
            return bias + w[2] * sh[0] + w[1] * sh[1] + w[0] * sh[2], sh

        def dpre_of(e0):
            ag, sh_g = taps(xg_s, e0, wg, bg)
            av, sh_v = taps(xv_s, e0, wv, bv)
            dout = dd_s[pl.ds(e0, sb), :]
            s = jax.nn.sigmoid(ag)
            silu = ag * s
            return dout * av * (s + silu * (1.0 - s)), dout * silu, sh_g, sh_v

        def back(dp, nxt, w):
            cat = jnp.concatenate([dp, nxt], axis=0)
            return w[2] * dp + w[1] * pltpu.roll(cat, 2 * sb - 1, 0)[:sb] + w[0] * pltpu.roll(cat, 2 * sb - 2, 0)[:sb]

        nxt_g, nxt_v, _, _ = dpre_of(tm)
        acc_g = acc_v = tuple(jnp.zeros((sb, tc), F32) for _ in range(kw + 1))
        for e0 in range(tm - sb, -1, -sb):
            dpg, dpv, sh_g, sh_v = dpre_of(e0)
            og_s[e0:e0 + sb, :] = back(dpg, nxt_g, wg)
            ov_s[e0:e0 + sb, :] = back(dpv, nxt_v, wv)
            acc_g = tuple(a + dpg * t for a, t in zip(acc_g, (sh_g[2], sh_g[1], sh_g[0], 1.0)))
            acc_v = tuple(a + dpv * t for a, t in zip(acc_v, (sh_v[2], sh_v[1], sh_v[0], 1.0)))
            nxt_g, nxt_v = dpg, dpv

        @pl.when(i == 0)
        def _():
            for r in (dwg_ref, dbg_ref, dwv_ref, dbv_ref, gwg_ref, gwv_ref):
                r[...] = jnp.zeros_like(r)

        for acc, o_s, dx_ref, dw_ref, db_ref, gw_ref in ((acc_g, og_s, dxg_ref, dwg_ref, dbg_ref, gwg_ref),
                                                        (acc_v, ov_s, dxv_ref, dwv_ref, dbv_ref, gwv_ref)):
            dx = o_s[...].astype(ACT_DTYPE)
            dx_ref[...] = dx
            gw_ref[...] += _mxu(u_ref[...], dx, TN)
            for k in range(kw):
                dw_ref[k:k + 1, :] += jnp.sum(acc[k], axis=0, keepdims=True)
            db_ref[...] += jnp.sum(acc[kw], axis=0, keepdims=True)

    gcol, vcol = (lambda j: j), (lambda j: j + nb)
    main = lambda col: pl.BlockSpec((tm, tc), lambda j, i: (i, col(j)))
    par = lambda r, col: pl.BlockSpec((r, tc), lambda j, i: (0, col(j)))
    halos = lambda col: [_prev_halo_spec(tm, tc, col), _next_halo_spec(tm, rows, tc, col)]
    act_shape = jax.ShapeDtypeStruct((rows, dff), ACT_DTYPE)
    par_shapes = [jax.ShapeDtypeStruct((kw, dff), F32), jax.ShapeDtypeStruct((1, dff), F32)]
    gw_shape = jax.ShapeDtypeStruct((d, dff), F32)
    return pl.pallas_call(
        body, name="ffn_conv_bwd", grid=(nb, rows // tm),
        in_specs=[main(gcol)] + halos(gcol) + [main(gcol)] + halos(gcol) + [main(gcol), _next_halo_spec(tm, rows, tc, gcol),
                  par(kw, gcol), par(1, gcol), par(kw, vcol), par(1, vcol), pl.BlockSpec((tm, d), lambda j, i: (i, 0))],
        out_specs=[main(gcol), main(gcol), par(kw, gcol), par(1, gcol), par(kw, gcol), par(1, gcol), par(d, gcol), par(d, gcol)],
        out_shape=[act_shape, act_shape] + par_shapes + par_shapes + [gw_shape, gw_shape],
        scratch_shapes=[pltpu.VMEM((tm + 2 * HALO, tc), F32), pltpu.VMEM((tm + 2 * HALO, tc), F32),
                        pltpu.VMEM((tm + HALO, tc), F32), pltpu.VMEM((tm, tc), F32), pltpu.VMEM((tm, tc), F32)],
        compiler_params=_params(("parallel", "arbitrary")),
    )(up_g, up_g, up_g, up_v, up_v, up_v, dact, dact, w, b, w, b, u2)


def _ssd_scalars(dtr, dt_bias, a_log, live):
    q = CHUNK
    pre = dtr + dt_bias
    dt = jnp.where(live, jax.nn.softplus(pre), 0.0)
    a_neg = -jnp.exp(a_log)
    li = lax.broadcasted_iota(jnp.int32, (q, q), 0)
    si = lax.broadcasted_iota(jnp.int32, (q, q), 1)
    causal = li >= si
    tri = jnp.where(causal, 1.0, 0.0).astype(F32)
    a_cs = sum(_mxu(tri, p, NN) for p in _split(dt * a_neg, 3))
    return pre, dt, a_neg, a_cs, causal, tri


def _head_select():
    r = lax.broadcasted_iota(jnp.int32, (LANES, SSD_GW), 0)
    c = lax.broadcasted_iota(jnp.int32, (LANES, SSD_GW), 1)
    return jnp.where(c // SSD_HEAD_DIM == r, 1.0, 0.0).astype(MXU_DTYPE)


def _split(t, parts):
    out, rem = [], t
    for _ in range(parts):
        p = rem.astype(MXU_DTYPE)
        out.append(p)
        rem = rem - p.astype(F32)
    return out


def _stacked(ts, parts, sel, dn):
    out = _mxu(jnp.concatenate([p for t, n in zip(ts, parts, strict=True) for p in _split(t, n)], axis=0), sel, dn)
    res, r0 = [], 0
    for t, n in zip(ts, parts, strict=True):
        r = t.shape[0]
        res.append(sum(out[r0 + k * r:r0 + (k + 1) * r] for k in range(n)))
        r0 += n * r
    return res


def _head_cols(ts, sel):
    return _stacked(ts, [2] * len(ts), sel, NN)


def _head_sums(ts, parts, sel):
    return _stacked(ts, parts, sel, NT)


def _half_masks():
    lane = lax.broadcasted_iota(jnp.int32, (CHUNK, LANES), 1)
    return lane < SSD_HEAD_DIM, lane >= SSD_HEAD_DIM


def _ssd_scan(xs, bm, cm, dtr, prev, dt_bias, a_log, d_skip, live):
    q = CHUNK
    sel = _head_select()
    _, dt, _, a_cs, causal, _ = _ssd_scalars(dtr, dt_bias, a_log, live)
    a_cs_t = a_cs.T
    a_end = a_cs[q - 1:q, :]
    dt_x, e_x, w_x, d_x = _head_cols([dt, jnp.exp(a_cs), jnp.exp(a_end - a_cs), jnp.broadcast_to(d_skip, (16, LANES))], sel)
    xdt = xs * dt_x
    cb = _dot(cm, bm, NT)
    y = _dot(cm, prev) * e_x + d_x[0:1] * xs
    new = prev * e_x[q - 1:q, :] + _dot(bm, xdt * w_x, TN)
    masks = _half_masks()
    ys = []
    for pp in range(SSD_HPG // 2):
        xpair = xdt[:, pp * LANES:(pp + 1) * LANES]
        acc = jnp.zeros((q, LANES), F32)
        for half in range(2):
            hh = 2 * pp + half
            decay = jnp.exp(jnp.where(causal, a_cs[:, hh:hh + 1] - a_cs_t[hh:hh + 1, :], -jnp.inf))
            acc = acc + _dot(cb * decay, jnp.where(masks[half], xpair, 0.0))
        ys.append(acc)
    return y + jnp.concatenate(ys, axis=1), new


def _ssd_gate(y, z, nw):
    yz = y * _silu(z)
    return yz * lax.rsqrt(jnp.mean(yz * yz, axis=-1, keepdims=True) + EPS) * nw


def _ssd_scan_bwd(xs, bm, cm, dtr, prev, dt_bias, a_log, d_skip, live, dy, dnew):
    q = CHUNK
    sel = _head_select()
    pre, dt, a_neg, a_cs, causal, tri = _ssd_scalars(dtr, dt_bias, a_log, live)
    a_cs_t = a_cs.T
    a_end = a_cs[q - 1:q, :]
    dt_x, e_x, w_x, d_x = _head_cols([dt, jnp.exp(a_cs), jnp.exp(a_end - a_cs), jnp.broadcast_to(d_skip, (16, LANES))], sel)
    g_x, d_x = e_x[q - 1:q, :], d_x[0:1]
    xdt = xs * dt_x
    u = xdt * w_x
    cb = _mxu(cm, bm, NT)
    cs = _mxu(cm, prev, NN)
    dye = dy * e_x
    dcm = _mxu(dye, prev, NT)
    dprev = _mxu(cm, dye, TN) + dnew * g_x
    dacs_x = dye * cs
    dbm = _mxu(u, dnew, NT)
    du = _mxu(bm, dnew, NN)
    dw_x = du * u
    dacs_x = dacs_x - dw_x
    dend_x = jnp.sum(dw_x + dnew * prev * g_x, axis=0, keepdims=True)
    dxdt = du * w_x
    lane = lax.broadcasted_iota(jnp.int32, (q, LANES), 1)
    sub = lax.broadcasted_iota(jnp.int32, (q, LANES), 0)
    dcb = jnp.zeros((q, q), F32)
    dacs = jnp.zeros((q, LANES), F32)
    dacs_t = jnp.zeros((q, LANES), F32)
    masks = _half_masks()
    dxdt_p = []
    for pp in range(SSD_HPG // 2):
        ps = slice(pp * LANES, (pp + 1) * LANES)
        acc = jnp.zeros((q, LANES), F32)
        for half in range(2):
            hh = 2 * pp + half
            decay = jnp.exp(jnp.where(causal, a_cs[:, hh:hh + 1] - a_cs_t[hh:hh + 1, :], -jnp.inf))
            m = cb * decay
            dyh = jnp.where(masks[half], dy[:, ps], 0.0)
            dm = _mxu(dyh, xdt[:, ps], NT)
            acc = acc + _mxu(m, dyh, TN)
            dcb = dcb + dm * decay
            p = dm * m
            dacs = jnp.where(lane == hh, jnp.sum(p, axis=1, keepdims=True), dacs)
            dacs_t = jnp.where(sub == hh, jnp.sum(p, axis=0, keepdims=True), dacs_t)
        dxdt_p.append(acc)
    dcm = dcm + _mxu(dcb, bm, NN)
    dbm = dbm + _mxu(dcb, cm, TN)
    dxdt = dxdt + jnp.concatenate(dxdt_p, axis=1)
    dxs = dy * d_x + dxdt * dt_x
    rows_x = jnp.concatenate([dend_x, jnp.sum(dy * xs, axis=0, keepdims=True), jnp.zeros((14, SSD_GW), F32)], axis=0)
    dacs_h, ddt_h, rows = _head_sums([dacs_x, dxdt * xs, rows_x], [3, 2, 3], sel)
    dacs = dacs - dacs_t.T + dacs_h
    dacs = dacs + jnp.where(sub == q - 1, rows[0:1], 0.0)
    tri_t = jnp.where(causal, 0.0, 1.0).astype(F32) + jnp.where(lane == sub, 1.0, 0.0)
    da = jnp.dot(tri_t, dacs, precision=lax.Precision.HIGHEST, preferred_element_type=F32)
    ddt = ddt_h + da * a_neg
    dalog = jnp.sum(da * dt, axis=0, keepdims=True) * a_neg
    ddtr = jnp.where(live, ddt * jax.nn.sigmoid(pre), 0.0)
    dbias = jnp.sum(ddtr, axis=0, keepdims=True)
    return dxs, dbm, dcm, ddtr, dprev, dbias, dalog, rows[1:2]


def _chunks_per_step(nc):
    return 13 if nc % 13 == 0 else 1


def _ssd_specs(rev, nc):
    per = _chunks_per_step(nc)
    steps = nc // per
    sidx = (lambda s: steps - 1 - s) if rev else (lambda s: s)
    nb_b = SSD_D_INNER // SSD_STATE
    row = lambda width, col=lambda g: g: pl.BlockSpec((per * CHUNK, width), lambda g, s: (sidx(s), col(g)))
    par = lambda width: pl.BlockSpec((1, width), lambda g, s: (0, g))
    state = lambda: pl.BlockSpec((per, 1, SSD_STATE, SSD_GW), lambda g, s: (sidx(s), g, 0, 0))
    xbc = [row(SSD_GW), row(SSD_STATE, lambda g: nb_b + g), row(SSD_STATE, lambda g: nb_b + SSD_GROUPS + g)]
    return per, steps, sidx, row, par, state, xbc


def _ssd_fwd(xbc_c, dtr, z, dt_bias, a_log, d_skip, nw):
    rows = z.shape[0]
    nc = rows // CHUNK
    per, steps, _, row, par, state, xbc = _ssd_specs(False, nc)

    def body(xs_ref, b_ref, c_ref, dt_ref, z_ref, bias_ref, al_ref, dk_ref, nw_ref, o_ref, y_ref, st_ref, carry):
        s = pl.program_id(1)

        @pl.when(s == 0)
        def _():
            carry[...] = jnp.zeros_like(carry)

        for j in range(per):
            rs = pl.ds(j * CHUNK, CHUNK)
            live = (s * per + j) * CHUNK + lax.broadcasted_iota(jnp.int32, (CHUNK, 1), 0) >= FRONT
            prev = carry[...]
            st_ref[j, 0] = prev
            y, new = _ssd_scan(xs_ref[rs, :].astype(F32), b_ref[rs, :].astype(F32), c_ref[rs, :].astype(F32), dt_ref[rs, :],
                               prev, bias_ref[...], al_ref[...], dk_ref[...], live)
            y_ref[rs, :] = y.astype(ACT_DTYPE)
            o_ref[rs, :] = _ssd_gate(y, z_ref[rs, :].astype(F32), nw_ref[...]).astype(ACT_DTYPE)
            carry[...] = new

    act = jax.ShapeDtypeStruct((rows, SSD_D_INNER), ACT_DTYPE)
    return pl.pallas_call(
        body, name="ssd_fwd", grid=(SSD_GROUPS, steps),
        in_specs=xbc + [row(LANES), row(SSD_GW), par(LANES), par(LANES), par(LANES), par(SSD_GW)],
        out_specs=[row(SSD_GW), row(SSD_GW), state()],
        out_shape=[act, act, jax.ShapeDtypeStruct((nc, SSD_GROUPS, SSD_STATE, SSD_GW), F32)],
        scratch_shapes=[pltpu.VMEM((SSD_STATE, SSD_GW), F32)],
        compiler_params=_params(("parallel", "arbitrary")),
    )(xbc_c, xbc_c, xbc_c, dtr, z, dt_bias, a_log, d_skip, nw)


def _ssd_gate_bwd(y, z, d_bs, w_bs, nw, dproj, col0):
    rows = y.shape[0]
    d = d_bs.shape[1]
    tm = _row_tile(rows)
    assert col0 % SSD_GW == 0

    def body(y_ref, z_ref, db_ref, wb_ref, nw_ref, dp_ref, dy_ref, dz_ref, dnw_ref):
        yv, zv = y_ref[...].astype(F32), z_ref[...].astype(F32)
        dov = _mxu(db_ref[...], wb_ref[...], NT)
        s = jax.nn.sigmoid(zv)
        silu = zv * s
        yz = yv * silu
        r = lax.rsqrt(jnp.mean(yz * yz, axis=-1, keepdims=True) + EPS)
        yhat = yz * r
        dn = dov * nw_ref[...]
        dyz = r * (dn - yhat * jnp.mean(dn * yhat, axis=-1, keepdims=True))
        dy_ref[...] = (dyz * silu).astype(ACT_DTYPE)
        dz_ref[...] = (dyz * yv * (s + silu * (1.0 - s))).astype(ACT_DTYPE)

        @pl.when(pl.program_id(1) == 0)
        def _():
            dnw_ref[...] = jnp.zeros_like(dnw_ref)

        dnw_ref[...] += jnp.sum(dov * yhat, axis=0, keepdims=True)

    spec = pl.BlockSpec((tm, SSD_GW), lambda g, i: (i, g))
    par = pl.BlockSpec((1, SSD_GW), lambda g, i: (0, g))
    act = jax.ShapeDtypeStruct((rows, SSD_D_INNER), ACT_DTYPE)
    return pl.pallas_call(
        body, name="ssd_gate_bwd", grid=(SSD_GROUPS, rows // tm),
        in_specs=[spec, spec, pl.BlockSpec((tm, d), lambda g, i: (i, 0)), pl.BlockSpec((SSD_GW, d), lambda g, i: (g, 0)),
                  par, pl.BlockSpec(memory_space=pl.ANY)],
        out_specs=[spec, pl.BlockSpec((tm, SSD_GW), lambda g, i: (i, col0 // SSD_GW + g)), par],
        out_shape=[act, jax.ShapeDtypeStruct(dproj.shape, dproj.dtype), jax.ShapeDtypeStruct((1, SSD_D_INNER), F32)],
        input_output_aliases={5: 1}, compiler_params=_params(("parallel", "arbitrary")),
    )(y, z, d_bs, w_bs, nw, dproj)


def _ssd_bwd(xbc_c, dtr, dt_bias, a_log, d_skip, states, dy, dproj, col0):
    rows = dy.shape[0]
    nc = rows // CHUNK
    per, steps, sidx, row, par, state, xbc = _ssd_specs(True, nc)
    assert col0 % LANES == 0

    def body(xs_ref, b_ref, c_ref, dt_ref, bias_ref, al_ref, dk_ref, st_ref, dy_ref, dp_ref,
             dxs_ref, db_ref, dc_ref, ddt_ref, dbias_ref, dal_ref, ddk_ref, carry):
        s = pl.program_id(1)

        @pl.when(s == 0)
        def _():
            carry[...] = jnp.zeros_like(carry)
            for r in (dbias_ref, dal_ref, ddk_ref):
                r[...] = jnp.zeros_like(r)

        for j in reversed(range(per)):
            rs = pl.ds(j * CHUNK, CHUNK)
            live = (sidx(s) * per + j) * CHUNK + lax.broadcasted_iota(jnp.int32, (CHUNK, 1), 0) >= FRONT
            dxs, dbm, dcm, ddt, dprev, dbias, dal, ddk = _ssd_scan_bwd(
                xs_ref[rs, :].astype(F32), b_ref[rs, :].astype(F32), c_ref[rs, :].astype(F32), dt_ref[rs, :], st_ref[j, 0],
                bias_ref[...], al_ref[...], dk_ref[...], live, dy_ref[rs, :].astype(F32), carry[...])
            dxs_ref[rs, :] = dxs.astype(ACT_DTYPE)
            db_ref[rs, :] = dbm.astype(ACT_DTYPE)
            dc_ref[rs, :] = dcm.astype(ACT_DTYPE)
            ddt_ref[rs, :] = ddt.astype(ACT_DTYPE)
            carry[...] = dprev
            dbias_ref[...] += dbias
            dal_ref[...] += dal
            ddk_ref[...] += ddk

    bc = jax.ShapeDtypeStruct((rows, SSD_GROUPS * SSD_STATE), ACT_DTYPE)
    head = jax.ShapeDtypeStruct((1, SSD_GROUPS * LANES), F32)
    return pl.pallas_call(
        body, name="ssd_bwd", grid=(SSD_GROUPS, steps),
        in_specs=xbc + [row(LANES), par(LANES), par(LANES), par(LANES), state(), row(SSD_GW), pl.BlockSpec(memory_space=pl.ANY)],
        out_specs=[row(SSD_GW), row(SSD_STATE), row(SSD_STATE), row(LANES, lambda g: col0 // LANES + g),
                   par(LANES), par(LANES), par(LANES)],
        out_shape=[jax.ShapeDtypeStruct((rows, SSD_D_INNER), ACT_DTYPE), bc, bc,
                   jax.ShapeDtypeStruct(dproj.shape, dproj.dtype), head, head, head],
        input_output_aliases={9: 3}, scratch_shapes=[pltpu.VMEM((SSD_STATE, SSD_GW), F32)],
        compiler_params=_params(("parallel", "arbitrary")),
    )(xbc_c, xbc_c, xbc_c, dtr, dt_bias, a_log, d_skip, states, dy, dproj)


def _rotary_tables(rows):
    pos = np.arange(rows, dtype=np.float32) - np.float32(FRONT)
    inv_freq = np.float32(ROPE_BASE) ** (-np.linspace(0.0, 1.0, RET_QK // 2, dtype=np.float32))
    ang = (pos[:, None] * inv_freq[None, :]).astype(np.float32).astype(np.float64)
    lgam = np.log(1.0 - 2.0 ** (-5.0 - np.arange(RET_HEADS, dtype=np.float64))).astype(np.float32)
    lgam = np.broadcast_to(lgam[:, None, None], (RET_HEADS, 8, LANES))
    return jnp.asarray(np.cos(ang).astype(np.float32)), jnp.asarray(np.sin(ang).astype(np.float32)), jnp.asarray(lgam)


def _rotary(t, cos, sin):
    half = t.shape[-1] // 2
    t1, t2 = t[:, :half], t[:, half:]
    return jnp.concatenate([t1 * cos - t2 * sin, t2 * cos + t1 * sin], axis=1)


def _ret_chunk(qh, kh, vh, gh, prev, cos, sin, lg):
    q = CHUNK
    qr = _rotary(qh, cos, sin)
    kr = _rotary(kh, cos, sin) * (RET_QK ** -0.5)
    li = lax.broadcasted_iota(jnp.int32, (q, q), 0)
    si = lax.broadcasted_iota(jnp.int32, (q, q), 1)
    dist = (li - si).astype(F32)
    decay = jnp.exp(jnp.where(li >= si, dist * lg, -jnp.inf))
    idx = lax.broadcasted_iota(jnp.int32, (q, 1), 0).astype(F32)
    scores = _dot(qr, kr, NT) * decay
    out = _dot(scores, vh)
    kv = _dot(kr * jnp.exp((q - 1.0 - idx) * lg), vh, TN)
    out = out + _dot(qr, prev) * jnp.exp((idx + 1.0) * lg)
    new = prev * jnp.exp(q * lg) + kv
    out = out * lax.rsqrt(jnp.mean(out * out, axis=-1, keepdims=True) + EPS)
    return _silu(gh) * out, new


def _ret_specs(rev, nc):
    per = _chunks_per_step(nc)
    steps = nc // per
    sidx = (lambda s: steps - 1 - s) if rev else (lambda s: s)
    row = lambda width: pl.BlockSpec((per * CHUNK, width), lambda h, s: (sidx(s), h))
    tab = lambda: pl.BlockSpec((per * CHUNK, RET_QK // 2), lambda h, s: (sidx(s), 0))
    lgs = lambda: pl.BlockSpec((1, 8, LANES), lambda h, s: (h, 0, 0))
    state = lambda: pl.BlockSpec((per, 1, RET_QK, RET_V), lambda h, s: (sidx(s), h, 0, 0))
    part = lambda width, k: pl.BlockSpec((per * CHUNK, width), lambda h, s: (sidx(s), h * (RET_HW // width) + k))
    ins = [part(RET_QK, 0), part(RET_QK, 1), part(RET_V, 1), part(RET_V, 2), tab(), tab(), lgs()]
    return per, steps, sidx, row, state, ins


def _ret_fwd(qkvg, cos, sin, lgam):
    rows = qkvg.shape[0]
    nc = rows // CHUNK
    per, steps, _, row, state, ins = _ret_specs(False, nc)
    q = k = v = g = qkvg

    def body(q_ref, k_ref, v_ref, g_ref, cos_ref, sin_ref, lg_ref, y_ref, st_ref, carry):
        @pl.when(pl.program_id(1) == 0)
        def _():
            carry[...] = jnp.zeros_like(carry)

        for j in range(per):
            rs = pl.ds(j * CHUNK, CHUNK)
            prev = carry[...]
            st_ref[j, 0] = prev.astype(ACT_DTYPE)
            out, new = _ret_chunk(q_ref[rs, :].astype(F32), k_ref[rs, :].astype(F32), v_ref[rs, :].astype(F32),
                                  g_ref[rs, :].astype(F32), prev, cos_ref[rs, :], sin_ref[rs, :], lg_ref[0, 0:1, 0:1])
            y_ref[rs, :] = out.astype(ACT_DTYPE)
            carry[...] = new

    return pl.pallas_call(
        body, name="ret_fwd", grid=(RET_HEADS, steps), in_specs=ins, out_specs=[row(RET_V), state()],
        out_shape=[jax.ShapeDtypeStruct((rows, RET_HEADS * RET_V), ACT_DTYPE),
                   jax.ShapeDtypeStruct((nc, RET_HEADS, RET_QK, RET_V), ACT_DTYPE)],
        scratch_shapes=[pltpu.VMEM((RET_QK, RET_V), F32)],
        compiler_params=_params(("parallel", "arbitrary")),
    )(q, k, v, g, cos, sin, lgam)


def _ret_bwd(qkvg, cos, sin, lgam, states, dy, dproj):
    rows = qkvg.shape[0]
    nc = rows // CHUNK
    per, steps, sidx, row, state, ins = _ret_specs(True, nc)

    def body(q_ref, k_ref, v_ref, g_ref, cos_ref, sin_ref, lg_ref, st_ref, dy_ref, dp_ref, o_ref, carry):
        @pl.when(pl.program_id(1) == 0)
        def _():
            carry[...] = jnp.zeros_like(carry)

        for j in reversed(range(per)):
            rs = pl.ds(j * CHUNK, CHUNK)
            fn = functools.partial(_ret_chunk, cos=cos_ref[rs, :], sin=sin_ref[rs, :], lg=lg_ref[0, 0:1, 0:1])
            _, vjp = jax.vjp(fn, q_ref[rs, :].astype(F32), k_ref[rs, :].astype(F32), v_ref[rs, :].astype(F32),
                             g_ref[rs, :].astype(F32), st_ref[j, 0].astype(F32))
            dq, dk, dv, dg, dprev = vjp((dy_ref[rs, :].astype(F32), carry[...]))
            o_ref[rs, 0:RET_QK] = dq.astype(ACT_DTYPE)
            o_ref[rs, RET_QK:2 * RET_QK] = dk.astype(ACT_DTYPE)
            o_ref[rs, 2 * RET_QK:2 * RET_QK + RET_V] = dv.astype(ACT_DTYPE)
            o_ref[rs, 2 * RET_QK + RET_V:RET_HW] = dg.astype(ACT_DTYPE)
            carry[...] = dprev

    return pl.pallas_call(
        body, name="ret_bwd", grid=(RET_HEADS, steps),
        in_specs=ins + [state(), row(RET_V), pl.BlockSpec(memory_space=pl.ANY)],
        out_specs=pl.BlockSpec((per * CHUNK, RET_HW), lambda h, s: (sidx(s), h)),
        out_shape=jax.ShapeDtypeStruct(dproj.shape, dproj.dtype), input_output_aliases={9: 0},
        scratch_shapes=[pltpu.VMEM((RET_QK, RET_V), F32)],
        compiler_params=_params(("parallel", "arbitrary")),
    )(qkvg, qkvg, qkvg, qkvg, cos, sin, lgam, states, dy, dproj)


def _branches_merge(y_ssd, y_ret, w_bs, w_br, gates):
    rows, k = y_ssd.shape
    d = w_bs.shape[1]
    tm = _row_tile(rows)

    def body(ys_ref, yr_ref, ws_ref, wr_ref, gs_ref, gr_ref, bs_ref, br_ref, o_ref):
        bs = _mxu(ys_ref[...], ws_ref[...], NN).astype(ACT_DTYPE)
        br = _mxu(yr_ref[...], wr_ref[...], NN).astype(ACT_DTYPE)
        bs_ref[...] = bs
        br_ref[...] = br
        o_ref[...] = (jax.nn.sigmoid(gs_ref[...].astype(F32)) * bs.astype(F32)
                      + jax.nn.sigmoid(gr_ref[...].astype(F32)) * br.astype(F32)).astype(ACT_DTYPE)

    spec = pl.BlockSpec((tm, d), lambda i: (i, 0))
    yspec = pl.BlockSpec((tm, k), lambda i: (i, 0))
    wspec = pl.BlockSpec((k, d), lambda i: (0, 0))
    shp = jax.ShapeDtypeStruct((rows, d), ACT_DTYPE)
    return pl.pallas_call(
        body, name="branches_merge", grid=(rows // tm,),
        in_specs=[yspec, yspec, wspec, wspec, spec, pl.BlockSpec((tm, d), lambda i: (i, 1))],
        out_specs=[spec, spec, spec], out_shape=[shp, shp, shp], compiler_params=_params(("parallel",)),
    )(y_ssd, y_ret, w_bs, w_br, gates, gates)


def _merge_bwd(dh1, w_o, bs, br, gates, dproj, col0):
    rows, d = bs.shape
    tm = _row_tile(rows)
    assert col0 % (2 * d) == 0

    def body(dh_ref, wo_ref, bs_ref, br_ref, gs_ref, gr_ref, dp_ref, dbs_ref, dbr_ref, dg_ref):
        dmv = _mxu(dh_ref[...], wo_ref[...], NT)
        for k, (b_ref, g_ref, db_ref) in enumerate(((bs_ref, gs_ref, dbs_ref), (br_ref, gr_ref, dbr_ref))):
            s = jax.nn.sigmoid(g_ref[...].astype(F32))
            db_ref[...] = (dmv * s).astype(ACT_DTYPE)
            dg_ref[:, k * d:(k + 1) * d] = (dmv * b_ref[...].astype(F32) * s * (1.0 - s)).astype(ACT_DTYPE)

    spec = pl.BlockSpec((tm, d), lambda i: (i, 0))
    shp = jax.ShapeDtypeStruct((rows, d), ACT_DTYPE)
    return pl.pallas_call(
        body, name="merge_bwd", grid=(rows // tm,),
        in_specs=[spec, pl.BlockSpec(w_o.shape, lambda i: (0, 0)), spec, spec, spec, pl.BlockSpec((tm, d), lambda i: (i, 1)),
                  pl.BlockSpec(memory_space=pl.ANY)],
        out_specs=[spec, spec, pl.BlockSpec((tm, 2 * d), lambda i: (i, col0 // (2 * d)))],
        out_shape=[shp, shp, jax.ShapeDtypeStruct(dproj.shape, dproj.dtype)], input_output_aliases={6: 2},
        compiler_params=_params(("parallel",)),
    )(dh1, w_o, bs, br, gates, gates, dproj)


def _place():
    x, y, c = lax.axis_index("x"), lax.axis_index("y"), lax.axis_index("c")
    return x, y, c


def _slot(p):
    return 4 * p[0] + 2 * p[1] + p[2]


def _allgather(arrs, name):
    n = len(arrs)
    any_spec = pl.BlockSpec(memory_space=pl.ANY)

    def body(*refs):
        ins, outs = refs[:n], refs[n:2 * n]
        send_sems, recv_sems, local_sems = refs[2 * n:]
        x, y, c = _place()
        me, sibling = (x, y, c), (x, y, 1 - c)
        chips = [(1 - x, y), (x, 1 - y), (1 - x, 1 - y)]

        def copy(a, k, block, to, src=None):
            dst = outs[a].at[_slot(block)]
            return pltpu.make_async_remote_copy(
                src_ref=dst if src is None else src, dst_ref=dst, send_sem=send_sems.at[a * 7 + k],
                recv_sem=recv_sems.at[a * 7 + k], device_id=to, device_id_type=MESH)

        mine, first, passed = [], [], []
        for a in range(n):
            cp = pltpu.make_async_copy(ins[a], outs[a].at[_slot(me)], local_sems.at[a])
            cp.start()
            mine.append(cp)
            first.append(copy(a, 0, me, sibling, src=ins[a]))
            first += [copy(a, 1 + j, me, (*chip, c), src=ins[a]) for j, chip in enumerate(chips)]
        for cp in first:
            cp.start()
        for j, chip in enumerate(chips):
            for a in range(n):
                copy(a, 1 + j, (*chip, c), me).wait_recv()
                cp = copy(a, 4 + j, (*chip, c), sibling)
                cp.start()
                passed.append(cp)
        for a in range(n):
            copy(a, 0, sibling, me).wait_recv()
            for j, chip in enumerate(chips):
                copy(a, 4 + j, (*chip, 1 - c), me).wait_recv()
        for cp in first + passed:
            cp.wait_send()
        for cp in mine:
            cp.wait()

    return pl.pallas_call(
        body, name=name, in_specs=[any_spec] * n, out_specs=[any_spec] * n,
        out_shape=[jax.ShapeDtypeStruct((N_DEV,) + a.shape, a.dtype) for a in arrs],
        scratch_shapes=[pltpu.SemaphoreType.DMA((7 * n,)), pltpu.SemaphoreType.DMA((7 * n,)), pltpu.SemaphoreType.DMA((n,))],
    )(*arrs)


def _peers():
    x, y, c = _place()
    return (x, y, c), [(x ^ dx, y ^ dy, c ^ dc) for dx in (0, 1) for dy in (0, 1) for dc in (0, 1)][1:]


def _exchange_copies(srcs, lands, send_sems, recv_sems, scatter, sender):
    me, peers = _peers()
    out = []
    for a, (src, land) in enumerate(zip(srcs, lands, strict=True)):
        for k, peer in enumerate(peers):
            src_ref = src.at[_slot(peer)] if scatter else src
            out.append(pltpu.make_async_remote_copy(
                src_ref=src_ref, dst_ref=land.at[_slot(me if sender else peer)], send_sem=send_sems.at[a * 7 + k],
                recv_sem=recv_sems.at[a * 7 + k], device_id=peer, device_id_type=MESH))
    return out


_HBM = pl.BlockSpec(memory_space=pltpu.HBM)
_SEM = pl.BlockSpec(memory_space=pltpu.SEMAPHORE)
_EFFECT = pltpu.SideEffectType.DATAFLOW_SIDE_EFFECTING


def _exchange_start(srcs, scatter, name, after=None):
    n = len(srcs)
    land_shapes = [s.shape if scatter else (N_DEV,) + s.shape for s in srcs]
    n_in = 2 * n + (after is not None)

    def body(*refs):
        for cp in _exchange_copies(refs[:n], refs[n:2 * n], refs[n_in], refs[n_in + 1], scatter, True):
            cp.start()
        refs[-1][...] = jnp.zeros_like(refs[-1])

    args = [pltpu.with_memory_space_constraint(s, pltpu.HBM) for s in srcs]
    args += [pltpu.with_memory_space_constraint(lax.empty(shp, s.dtype), pltpu.HBM) for s, shp in zip(srcs, land_shapes)]
    thru_shapes = tuple(pltpu.HBM(a.shape, a.dtype) for a in args)
    extra = [] if after is None else [after]
    outs = pl.pallas_call(
        body, name=name,
        out_shape=(pltpu.SemaphoreType.DMA((7 * n,)), pltpu.SemaphoreType.DMA((7 * n,))) + thru_shapes
        + (jax.ShapeDtypeStruct((8, LANES), F32),),
        in_specs=[_HBM] * (2 * n) + [pl.BlockSpec(memory_space=pl.ANY)] * len(extra),
        out_specs=(_SEM, _SEM) + (_HBM,) * (2 * n) + (pl.BlockSpec(memory_space=pltpu.VMEM),),
        input_output_aliases={i: 2 + i for i in range(2 * n)},
        compiler_params=pltpu.CompilerParams(has_side_effects=_EFFECT),
    )(*args, *extra)
    return outs[:-1], outs[-1]


def _exchange_wait(handle, scatter, after, name):
    n = (len(handle) - 2) // 2
    thru = handle[2:]

    def body(*refs):
        for cp in _exchange_copies(refs[:n], refs[n:2 * n], refs[2 * n], refs[2 * n + 1], scatter, False):
            cp.wait_send()
            cp.wait_recv()

    outs = pl.pallas_call(
        body, name=name, out_shape=tuple(pltpu.HBM(t.shape, t.dtype) for t in thru),
        in_specs=[_HBM] * (2 * n) + [_SEM, _SEM, pl.BlockSpec(memory_space=pl.ANY)], out_specs=(_HBM,) * (2 * n),
        input_output_aliases={i: i for i in range(2 * n)},
        compiler_params=pltpu.CompilerParams(has_side_effects=_EFFECT),
    )(*thru, handle[0], handle[1], after)
    return list(outs[:n]), list(outs[n:])


def _allreduce_small(pack):
    rows, lanes = pack.shape

    def body(x_ref, o_ref, buf, send_sems, recv_sems):
        x, y, c = _place()
        me, sibling = (x, y, c), (x, y, 1 - c)
        chips = [(1 - x, y), (x, 1 - y), (1 - x, 1 - y)]

        def copy(k, block, to, src=None):
            dst = buf.at[_slot(block)]
            return pltpu.make_async_remote_copy(
                src_ref=dst if src is None else src, dst_ref=dst, send_sem=send_sems.at[k], recv_sem=recv_sems.at[k],
                device_id=to, device_id_type=MESH)

        buf[_slot(me)] = x_ref[...]
        first = [copy(0, me, sibling, src=x_ref)]
        first += [copy(1 + j, me, (*chip, c), src=x_ref) for j, chip in enumerate(chips)]
        for cp in first:
            cp.start()
        passed = [copy(4 + j, (*chip, c), sibling) for j, chip in enumerate(chips)]
        for j, chip in enumerate(chips):
            copy(1 + j, (*chip, c), me).wait_recv()
            passed[j].start()
        copy(0, sibling, me).wait_recv()
        for j, chip in enumerate(chips):
            copy(4 + j, (*chip, 1 - c), me).wait_recv()
        for cp in first + passed:
            cp.wait_send()
        acc = buf[0]
        for i in range(1, N_DEV):
            acc = acc + buf[i]
        o_ref[...] = acc

    vmem = pl.BlockSpec(memory_space=pltpu.VMEM)
    return pl.pallas_call(
        body, name="allreduce_small", in_specs=[vmem], out_specs=vmem,
        out_shape=jax.ShapeDtypeStruct((rows, lanes), F32),
        scratch_shapes=[pltpu.VMEM((N_DEV, rows, lanes), F32), pltpu.SemaphoreType.DMA((7,)), pltpu.SemaphoreType.DMA((7,))],
        compiler_params=pltpu.CompilerParams(vmem_limit_bytes=VMEM_LIMIT),
    )(pack)


def _adamw(w, g, m, v):
    m = ADAM_B1 * m + (1.0 - ADAM_B1) * g
    v = ADAM_B2 * v + (1.0 - ADAM_B2) * jnp.square(g)
    m_hat = m / (1.0 - ADAM_B1 ** ADAM_STEP)
    v_hat = v / (1.0 - ADAM_B2 ** ADAM_STEP)
    delta = -ADAM_LR * (m_hat / (jnp.sqrt(v_hat) + ADAM_EPS) + ADAM_WD * w)
    return delta, m, v


def _adam_shard(own, parts, w, m, v, name):
    r, c = w.shape
    tr = _pick(r, (128, 64, 32, 16, 8))

    def body(own_ref, p_ref, w_ref, m_ref, v_ref, g_ref, d_ref, nm_ref, nv_ref):
        _, peers = _peers()
        g = own_ref[...].astype(F32)
        for peer in peers:
            g = g + p_ref[_slot(peer)].astype(F32)
        g_ref[...] = g
        d_ref[...], nm_ref[...], nv_ref[...] = _adamw(w_ref[...], g, m_ref[...], v_ref[...])

    spec = pl.BlockSpec((tr, c), lambda i: (i, 0))
    shp = jax.ShapeDtypeStruct((r, c), F32)
    return pl.pallas_call(
        body, name=name, grid=(r // tr,),
        in_specs=[spec, pl.BlockSpec((N_DEV, tr, c), lambda i: (0, i, 0)), spec, spec, spec], out_specs=[spec] * 4,
        out_shape=[shp] * 4, compiler_params=_params(("parallel",)),
    )(own, parts, w, m, v)


def _adam_small(w, g, m, v):
    r, c = w.shape

    def body(w_ref, g_ref, m_ref, v_ref, d_ref, nm_ref, nv_ref):
        d_ref[...], nm_ref[...], nv_ref[...] = _adamw(w_ref[...], g_ref[...], m_ref[...], v_ref[...])

    shp = jax.ShapeDtypeStruct((r, c), F32)
    return pl.pallas_call(body, name="adam_small", out_shape=[shp] * 3)(w, g, m, v)


def _column_plan(pieces, shard_w):
    plan = []
    for c0, width, d0 in pieces:
        c = c0
        while c < c0 + width:
            s, a = divmod(c, shard_w)
            w = min(c0 + width - c, shard_w - a)
            plan.append((s, a, w, d0 + c - c0))
            c += w
    return plan


def _cols_from_shards(g, plan, out_w, zero, name):
    _, r, sw = g.shape
    tr = _pick(r, (128,))

    def body(x_ref, o_ref):
        for d0, w in zero:
            o_ref[:, d0:d0 + w] = jnp.zeros((tr, w), g.dtype)
        for s, a, w, d0 in plan:
            o_ref[:, d0:d0 + w] = x_ref[s, :, a:a + w]

    return pl.pallas_call(
        body, name=name, grid=(r // tr,), in_specs=[pl.BlockSpec((N_DEV, tr, sw), lambda i: (0, i, 0))],
        out_specs=pl.BlockSpec((tr, out_w), lambda i: (i, 0)), out_shape=jax.ShapeDtypeStruct((r, out_w), g.dtype),
        compiler_params=_params(("parallel",)),
    )(g)


def _shards_from_cols(srcs, plans, shard_w, name):
    r = srcs[0].shape[0]
    tr = _pick(r, (128,))
    n = len(srcs)

    def body(*refs):
        o_ref = refs[n]
        for x_ref, plan in zip(refs[:n], plans, strict=True):
            for s, a, w, d0 in plan:
                o_ref[s, :, a:a + w] = x_ref[:, d0:d0 + w].astype(COMM_DTYPE)

    return pl.pallas_call(
        body, name=name, grid=(r // tr,), in_specs=[pl.BlockSpec((tr, t.shape[1]), lambda i: (i, 0)) for t in srcs],
        out_specs=pl.BlockSpec((N_DEV, tr, shard_w), lambda i: (0, i, 0)),
        out_shape=jax.ShapeDtypeStruct((N_DEV, r, shard_w), COMM_DTYPE), compiler_params=_params(("parallel",)),
    )(*srcs)


def _pack(arrs):
    rows = []
    for a in arrs:
        flat = a.reshape(-1).astype(F32)
        rows.append(jnp.pad(flat, (0, (-flat.shape[0]) % (8 * LANES))).reshape(-1, LANES))
    return jnp.concatenate(rows, axis=0)


def _unpack(pack, shapes):
    out, r = [], 0
    for s in shapes:
        size = math.prod(s)
        nr = -(-size // (8 * LANES)) * 8
        out.append(pack[r:r + nr].reshape(-1)[:size].reshape(s))
        r += nr
    return out


def _group_lanes(t):
    lead = t.shape[:-1]
    t = t.reshape(lead + (SSD_GROUPS, SSD_HPG))
    t = jnp.pad(t, [(0, 0)] * len(lead) + [(0, 0), (0, LANES - SSD_HPG)])
    return t.reshape(lead + (SSD_GROUPS * LANES,))


def _ungroup_lanes(t):
    lead = t.shape[:-1]
    return t.reshape(lead + (SSD_GROUPS, LANES))[..., :SSD_HPG].reshape(lead + (SSD_HEADS,))


def kernel(x, meta_tokens, mix_norm_w, w_in, ssd_conv_w, ssd_conv_b, ssd_dt_bias, ssd_A_log, ssd_D, ssd_norm_w, w_branch_ssd, w_branch_ret, w_out, ffn_norm_w, w_up, ffn_conv_w, ffn_conv_b, w_down, final_norm_w, loss_target, m_meta_tokens, m_mix_norm_w, m_w_in, m_ssd_conv_w, m_ssd_conv_b, m_ssd_dt_bias, m_ssd_A_log, m_ssd_D, m_ssd_norm_w, m_w_branch_ssd, m_w_branch_ret, m_w_out, m_ffn_norm_w, m_w_up, m_ffn_conv_w, m_ffn_conv_b, m_w_down, m_final_norm_w, v_meta_tokens, v_mix_norm_w, v_w_in, v_ssd_conv_w, v_ssd_conv_b, v_ssd_dt_bias, v_ssd_A_log, v_ssd_D, v_ssd_norm_w, v_w_branch_ssd, v_w_branch_ret, v_w_out, v_ffn_norm_w, v_w_up, v_ffn_conv_w, v_ffn_conv_b, v_w_down, v_final_norm_w):
    seq, d = x.shape[1], x.shape[2]
    rows = seq + PAD_ROWS
    tm = _row_tile(rows)
    me = _slot(_place())
    d_ff = w_down.shape[1] * N_DEV

    big = [w_in[0], w_branch_ssd[0], w_branch_ret[0], w_out[0], w_up[0], w_down[0]]
    first = _allgather([w_in[0].astype(COMM_DTYPE), meta_tokens, ssd_conv_w[0], ffn_conv_w[0]], "gather_first")
    rest_src = [b.astype(COMM_DTYPE) for b in big[1:]]
    rest_handle, rest_token = _exchange_start(rest_src, False, "gather_rest_start", after=first[0])
    cols = lambda t: jnp.transpose(t, (1, 0, 2)).reshape(t.shape[1], -1)
    rws = lambda t: t.reshape(-1, t.shape[2])
    conv_w, fconv_w = cols(first[2]), cols(first[3])
    meta_full = cols(first[1]) + rest_token[0, 0]
    widths = [SSD_D_INNER, SSD_CONV_DIM, SSD_HEADS, RET_HEADS * RET_QK, RET_HEADS * RET_QK, RET_HEADS * RET_V,
              RET_HEADS * RET_V, d, d]
    offs = [0]
    for wd in widths:
        offs.append(offs[-1] + wd)
    r0, z0 = 0, RET_HEADS * RET_HW
    g0 = z0 + widths[0]
    x0 = g0 + 2 * d
    dt0 = x0 + widths[1]
    in_p = dt0 + SSD_GROUPS * LANES
    pieces = []
    for hd in range(RET_HEADS):
        base = r0 + hd * RET_HW
        pieces += [(offs[3] + hd * RET_QK, RET_QK, base), (offs[4] + hd * RET_QK, RET_QK, base + RET_QK),
                   (offs[5] + hd * RET_V, RET_V, base + 2 * RET_QK), (offs[6] + hd * RET_V, RET_V, base + 2 * RET_QK + RET_V)]
    pieces += [(offs[0], widths[0], z0), (offs[7], d, g0), (offs[8], d, g0 + d), (offs[1], widths[1], x0)]
    pieces += [(offs[2] + SSD_HPG * grp, SSD_HPG, dt0 + LANES * grp) for grp in range(SSD_GROUPS)]
    in_plan = _column_plan(pieces, w_in.shape[2])
    w_in_p = _cols_from_shards(first[0], in_plan, in_p, [(dt0, SSD_GROUPS * LANES)], "w_in_columns")

    h0 = jnp.concatenate([jnp.zeros((FRONT, d), F32), meta_full, x[0]], axis=0)
    u1 = _rms_fwd(h0, mix_norm_w, "rms1")
    in_proj = lambda c0, width, dtype, nm: _mm(u1, w_in_p, mode="nn", out_dtype=dtype, tm=tm, tk=d, name="in_proj_" + nm,
                                               tn=_pick(width, (1024, 512)), b_n0=c0, n_out=width)
    qkvg = in_proj(r0, RET_HEADS * RET_HW, ACT_DTYPE, "qkvg")
    z = in_proj(z0, widths[0], ACT_DTYPE, "z")
    gates = in_proj(g0, 2 * d, ACT_DTYPE, "gates")
    dtr = in_proj(dt0, SSD_GROUPS * LANES, F32, "dt")
    xbc, xbc_c = _xbc_proj_conv(u1, w_in_p, x0, conv_w, ssd_conv_b)
    bias_p, alog_p, dsk_p = _group_lanes(ssd_dt_bias), _group_lanes(ssd_A_log), _group_lanes(ssd_D)
    y_ssd, y_scan, ssd_states = _ssd_fwd(xbc_c, dtr, z, bias_p, alog_p, dsk_p, ssd_norm_w)
    cos, sin, lgam = _rotary_tables(rows)
    y_ret, ret_states = _ret_fwd(qkvg, cos, sin, lgam)
    rest_own, rest = _exchange_wait(rest_handle, False, y_ret, "gather_rest_wait")
    rest = [lax.dynamic_update_index_in_dim(land, own, me, 0) for land, own in zip(rest, rest_own, strict=True)]
    w_bs, w_br, w_o, w_dn = rws(rest[0]), rws(rest[1]), rws(rest[2]), rws(rest[4])
    w_up_f = _cols_from_shards(rest[3], _column_plan([(0, 2 * d_ff, 0)], w_up.shape[2]), 2 * d_ff, [], "w_up_columns")
    bs, br, merged = _branches_merge(y_ssd, y_ret, w_bs, w_br, gates)
    h1, u2 = _out_proj_norm(merged, w_o, h0, ffn_norm_w)
    up_g, up_v, act = _ffn_up_conv(u2, w_up_f, fconv_w, ffn_conv_b)
    tgt = jnp.pad(loss_target[0], ((PAD_ROWS, 0), (0, 0)))
    dh2, loss_acc, g_final = _down_loss_head(act, w_dn, h1, tgt, final_norm_w.reshape(1, d))

    tff = _pick(d_ff, (1408, 256))
    tkr = _pick(rows, (1664, 128))
    tkr2 = _pick(rows, (4160, 128))
    rparts = lambda t: t.reshape(N_DEV, -1, t.shape[1])
    d_act = _mm(dh2, w_dn, mode="nt", out_dtype=ACT_DTYPE, tm=tm, tn=tff, tk=d, name="d_act")
    g_w_dn = _mm(act, dh2, mode="tn", out_dtype=COMM_DTYPE, tm=tff, tn=d, tk=tkr, name="g_w_down")
    c_dn = [rparts(g_w_dn)]
    h_dn, t_dn = _exchange_start(c_dn, True, "scatter_down_start")
    d_up_g, d_up_v, g_fcw_g, g_fcb_g, g_fcw_v, g_fcb_v, g_w_up_g, g_w_up_v = _ffn_conv_bwd(
        up_g, up_v, d_act, fconv_w, ffn_conv_b + t_dn[0, 0], u2)
    g_fconv_w = jnp.concatenate([g_fcw_g, g_fcw_v], axis=1)
    g_fconv_b = jnp.concatenate([g_fcb_g, g_fcb_v], axis=1)
    c_up = [_shards_from_cols([g_w_up_g, g_w_up_v], [_column_plan([(0, d_ff, 0)], w_up.shape[2]),
                                                     _column_plan([(d_ff, d_ff, 0)], w_up.shape[2])], w_up.shape[2], "g_w_up_shards")]
    h_up, t_up = _exchange_start(c_up, True, "scatter_up_start")
    dh1, g_ffn_norm = _proj_norm_bwd([d_up_g, d_up_v], w_up_f, d_ff, h1, ffn_norm_w, dh2, "ffn_up_bwd", after=t_up)
    g_w_o =_mm(merged, dh1, mode="tn", out_dtype=COMM_DTYPE, tm=d, tn=d, tk=tkr, name="g_w_out")
    dproj = lax.empty((rows, in_p), ACT_DTYPE)
    d_bs, d_br, dproj = _merge_bwd(dh1, w_o, bs, br, gates, dproj, g0)
    g_w_bs = _mm(y_ssd, d_bs, mode="tn", out_dtype=COMM_DTYPE, tm=1024, tn=d, tk=tkr2, name="g_w_branch_ssd")
    g_w_br = _mm(y_ret, d_br, mode="tn", out_dtype=COMM_DTYPE, tm=1024, tn=d, tk=tkr2, name="g_w_branch_ret")
    c_mid = [rparts(g_w_bs), rparts(g_w_br), rparts(g_w_o)]
    h_mid, t_mid = _exchange_start(c_mid, True, "scatter_mid_start")
    d_yscan, dproj, g_nw = _ssd_gate_bwd(y_scan, z, d_bs, w_bs, ssd_norm_w + t_mid[0, 0], dproj, z0)
    dxs, d_bm, d_cm, dproj, g_bias_p, g_alog_p, g_dsk_p = _ssd_bwd(
        xbc_c, dtr, bias_p, alog_p, dsk_p, ssd_states, d_yscan, dproj, dt0)
    dproj, g_conv_w, g_conv_b = _ssd_conv_bwd(xbc, dxs, d_bm, d_cm, conv_w, ssd_conv_b, dproj, x0)
    d_yret = _mm(d_br, w_br, mode="nt", out_dtype=ACT_DTYPE, tm=tm, tn=1024, tk=d, name="d_y_ret")
    dproj = _ret_bwd(qkvg, cos, sin, lgam, ret_states, d_yret, dproj)
    g_w_in_p = _mm(u1, dproj, mode="tn", out_dtype=F32, tm=d, tn=_pick(in_p, (768, 512)), tk=tkr2, name="g_w_in")
    c_in = [_shards_from_cols([g_w_in_p], [in_plan], w_in.shape[2], "g_w_in_shards")]
    h_in, t_in = _exchange_start(c_in, True, "scatter_in_start")
    dh0, g_mix_norm = _proj_norm_bwd([dproj], w_in_p, _pick(in_p, (4608, 512)), h0, mix_norm_w, dh1, "in_proj_bwd", after=t_in)
    grad_x = dh0[PAD_ROWS:][None]

    landed = {}
    for key, handle, names in (("in", h_in, ["w_in"]), ("mid", h_mid, ["w_branch_ssd", "w_branch_ret", "w_out"]),
                               ("up", h_up, ["w_up"]), ("down", h_dn, ["w_down"])):
        srcs, lands = _exchange_wait(handle, True, dh0, f"scatter_{key}_wait")
        for nm, land, src in zip(names, lands, srcs, strict=True):
            landed[nm] = (lax.dynamic_index_in_dim(src, me, 0, keepdims=False), land)
    big_m = [m_w_in, m_w_branch_ssd, m_w_branch_ret, m_w_out, m_w_up, m_w_down]
    big_v = [v_w_in, v_w_branch_ssd, v_w_branch_ret, v_w_out, v_w_up, v_w_down]
    big_names = ["w_in", "w_branch_ssd", "w_branch_ret", "w_out", "w_up", "w_down"]
    big_out = {}
    for nm, w, m, v_ in zip(big_names, big, big_m, big_v, strict=True):
        big_out[nm] = [t[None] for t in _adam_shard(*landed[nm], w, m[0], v_[0], "adam_" + nm)]

    small_g = [dh0[FRONT:PAD_ROWS], g_mix_norm, g_conv_w, g_conv_b, _ungroup_lanes(g_bias_p), _ungroup_lanes(g_alog_p),
               _ungroup_lanes(g_dsk_p), g_nw, g_ffn_norm, g_fconv_w, g_fconv_b, g_final, loss_acc[0:1, 0:1]]
    total = _unpack(_allreduce_small(_pack(small_g)), [t.shape for t in small_g])
    loss = total[12].reshape(())
    shard = lambda t, width: lax.dynamic_slice_in_dim(t, me * width, width, axis=1)
    small_names = ["meta_tokens", "mix_norm_w", "ssd_conv_w", "ssd_conv_b", "ssd_dt_bias", "ssd_A_log", "ssd_D", "ssd_norm_w",
                   "ffn_norm_w", "ffn_conv_w", "ffn_conv_b", "final_norm_w"]
    small_w = [meta_tokens, mix_norm_w, ssd_conv_w, ssd_conv_b, ssd_dt_bias, ssd_A_log, ssd_D, ssd_norm_w, ffn_norm_w,
               ffn_conv_w, ffn_conv_b, final_norm_w]
    small_m = [m_meta_tokens, m_mix_norm_w, m_ssd_conv_w, m_ssd_conv_b, m_ssd_dt_bias, m_ssd_A_log, m_ssd_D, m_ssd_norm_w,
               m_ffn_norm_w, m_ffn_conv_w, m_ffn_conv_b, m_final_norm_w]
    small_v = [v_meta_tokens, v_mix_norm_w, v_ssd_conv_w, v_ssd_conv_b, v_ssd_dt_bias, v_ssd_A_log, v_ssd_D, v_ssd_norm_w,
               v_ffn_norm_w, v_ffn_conv_w, v_ffn_conv_b, v_final_norm_w]
    grads = total[:12]
    grads[0] = shard(grads[0], meta_tokens.shape[1])
    grads[2] = shard(grads[2], ssd_conv_w.shape[2])
    grads[9] = shard(grads[9], ffn_conv_w.shape[2])
    grads = [t.reshape(w.shape) for t, w in zip(grads, small_w, strict=True)]
    shapes = [w.shape for w in small_w]
    upd = _adam_small(_pack(small_w), _pack(grads), _pack(small_m), _pack(small_v))
    small_out = {nm: [gr_] + [u[i] for u in (_unpack(t, shapes) for t in upd)]
                 for i, (nm, gr_) in enumerate(zip(small_names, grads, strict=True))}

    order = ["meta_tokens", "mix_norm_w", "w_in", "ssd_conv_w", "ssd_conv_b", "ssd_dt_bias", "ssd_A_log", "ssd_D", "ssd_norm_w",
             "w_branch_ssd", "w_branch_ret", "w_out", "ffn_norm_w", "w_up", "ffn_conv_w", "ffn_conv_b", "w_down", "final_norm_w"]
    res = {**big_out, **small_out}
    return (loss, grad_x, *[res[nm][0] for nm in order], *[res[nm][1] for nm in order], *[res[nm][2] for nm in order],
            *[res[nm][3] for nm in order])
```

```python
import functools
import math

import jax
import jax.numpy as jnp
import numpy as np
from jax import lax
from jax.experimental import pallas as pl
from jax.experimental.pallas import tpu as pltpu

F32 = jnp.float32
MXU_DTYPE = jnp.bfloat16
ACT_DTYPE = jnp.bfloat16
COMM_DTYPE = jnp.bfloat16

N_META = 16
CHUNK = 128
FRONT = CHUNK - N_META
PAD_ROWS = FRONT + N_META
EPS = 1e-6
N_DEV = 8

SSD_D_INNER = 2048
SSD_HEAD_DIM = 64
SSD_HEADS = 32
SSD_GROUPS = 4
SSD_HPG = SSD_HEADS // SSD_GROUPS
SSD_STATE = 128
SSD_CONV = 4
SSD_CONV_DIM = SSD_D_INNER + 2 * SSD_GROUPS * SSD_STATE
SSD_GW = SSD_D_INNER // SSD_GROUPS
RET_HEADS = 4
RET_QK = 256
RET_V = 512
RET_HW = 2 * RET_QK + 2 * RET_V
ROPE_BASE = 10000.0
FFN_CONV = 3
HALO = 16
LANES = 128

ADAM_LR = 0.001
ADAM_B1 = 0.9
ADAM_B2 = 0.999
ADAM_EPS = 1e-08
ADAM_WD = 0.01
ADAM_STEP = 10

VMEM_LIMIT = 56 * 1024 * 1024
MESH = pl.DeviceIdType.MESH

NN = (((1,), (0,)), ((), ()))
NT = (((1,), (1,)), ((), ()))
TN = (((0,), (0,)), ((), ()))


def _params(sem):
    return pltpu.CompilerParams(dimension_semantics=sem, vmem_limit_bytes=VMEM_LIMIT)


def _mxu(a, b, dn):
    return lax.dot_general(a.astype(MXU_DTYPE), b.astype(MXU_DTYPE), dn, preferred_element_type=F32)


@functools.partial(jax.custom_vjp, nondiff_argnums=(2,))
def _dot(a, b, dn=NN):
    return _mxu(a, b, dn)


def _dot_fwd(a, b, dn):
    return _mxu(a, b, dn), (a, b)


def _dot_bwd(dn, res, g):
    a, b = res
    if dn == NN:
        return _mxu(g, b, NT), _mxu(a, g, TN)
    if dn == NT:
        return _mxu(g, b, NN), _mxu(g, a, TN)
    assert dn == TN
    return _mxu(b, g, NT), _mxu(a, g, NN)


_dot.defvjp(_dot_fwd, _dot_bwd)


def _silu(x):
    return x * jax.nn.sigmoid(x)


def _dsilu(x):
    s = jax.nn.sigmoid(x)
    return s * (1.0 + x * (1.0 - s))


def _row_tile(rows):
    return 640 if rows % 640 == 0 else 128


def _mm(a, b, *, mode, out_dtype, tm, tn, tk, name, b_n0=0, n_out=None):
    if mode == "nt":
        (m, k), (n, k2) = a.shape, b.shape
        assert b_n0 == 0 and n_out is None
    else:
        (m, k) = a.shape if mode == "nn" else a.shape[::-1]
        k2 = b.shape[0]
        n = b.shape[1] if n_out is None else n_out
        assert b_n0 % tn == 0 and b_n0 + n <= b.shape[1]
    assert k == k2 and m % tm == 0 and n % tn == 0 and k % tk == 0, (name, a.shape, b.shape, tm, tn, tk)
    nb0 = b_n0 // tn
    nk = k // tk
    dn = {"nn": NN, "nt": NT, "tn": TN}[mode]

    def body(a_ref, b_ref, o_ref, *acc):
        p = _mxu(a_ref[...], b_ref[...], dn)
        if nk == 1:
            o_ref[...] = p.astype(out_dtype)
        else:
            acc_ref, = acc
            kk = pl.program_id(2)

            @pl.when(kk == 0)
            def _():
                acc_ref[...] = p

            @pl.when(kk > 0)
            def _():
                acc_ref[...] += p

            @pl.when(kk == nk - 1)
            def _():
                o_ref[...] = acc_ref[...].astype(out_dtype)

    if mode == "tn":
        a_spec = pl.BlockSpec((tk, tm), lambda j, i, kk: (kk, i))
    else:
        a_spec = pl.BlockSpec((tm, tk), lambda j, i, kk: (i, kk))
    if mode == "nt":
        b_spec = pl.BlockSpec((tn, tk), lambda j, i, kk: (j, kk))
    else:
        b_spec = pl.BlockSpec((tk, tn), lambda j, i, kk: (kk, j + nb0))
    return pl.pallas_call(
        body, name=name, grid=(n // tn, m // tm, nk), in_specs=[a_spec, b_spec],
        out_specs=pl.BlockSpec((tm, tn), lambda j, i, kk: (i, j)), out_shape=jax.ShapeDtypeStruct((m, n), out_dtype),
        scratch_shapes=[pltpu.VMEM((tm, tn), F32)] if nk > 1 else [],
        compiler_params=_params(("parallel", "parallel", "arbitrary")),
    )(a, b)


def _pick(n, cands):
    for c in cands:
        if n % c == 0:
            return c
    return n


def _rms_fwd(h, w, name):
    rows, d = h.shape
    tm = _row_tile(rows)

    def body(h_ref, w_ref, u_ref):
        x = h_ref[...]
        r = lax.rsqrt(jnp.mean(x * x, axis=-1, keepdims=True) + EPS)
        u_ref[...] = (x * r * w_ref[...]).astype(ACT_DTYPE)

    return pl.pallas_call(
        body, name=name, grid=(rows // tm,),
        in_specs=[pl.BlockSpec((tm, d), lambda i: (i, 0)), pl.BlockSpec((1, d), lambda i: (0, 0))],
        out_specs=pl.BlockSpec((tm, d), lambda i: (i, 0)),
        out_shape=jax.ShapeDtypeStruct((rows, d), ACT_DTYPE),
        compiler_params=_params(("parallel",)),
    )(h, w)


def _proj_norm_bwd(parts, w, tk, h, nw, dres, name, after=None):
    rows, d = h.shape
    tm = _row_tile(rows)
    steps = [p.shape[1] // tk for p in parts]
    starts = [sum(steps[:p]) for p in range(len(parts))]
    nk = sum(steps)
    assert all(p.shape[1] % tk == 0 for p in parts) and nk * tk == w.shape[1]
    n = len(parts)

    def body(*refs):
        a_refs, w_ref, h_ref, nw_ref, dres_ref = refs[:n], refs[n], refs[n + 1], refs[n + 2], refs[n + 3]
        dh_ref, dw_ref, acc = refs[-3], refs[-2], refs[-1]
        i, kk = pl.program_id(0), pl.program_id(1)
        for p in range(n):
            @pl.when((kk >= starts[p]) & (kk < starts[p] + steps[p]))
            def _(p=p):
                part = _mxu(a_refs[p][...], w_ref[...], NT)
                if starts[p] == 0:
                    @pl.when(kk == 0)
                    def _():
                        acc[...] = part

                    @pl.when(kk > 0)
                    def _():
                        acc[...] += part
                else:
                    acc[...] += part

        @pl.when(kk == nk - 1)
        def _():
            x, dy = h_ref[...], acc[...]
            r = lax.rsqrt(jnp.mean(x * x, axis=-1, keepdims=True) + EPS)
            xhat = x * r
            dxn = dy * nw_ref[...]
            dh_ref[...] = dres_ref[...] + r * (dxn - xhat * jnp.mean(dxn * xhat, axis=-1, keepdims=True))

            @pl.when(i == 0)
            def _():
                dw_ref[...] = jnp.zeros_like(dw_ref)

            dw_ref[...] += jnp.sum(dy * xhat, axis=0, keepdims=True)

    clip = lambda kk, p: jnp.clip(kk - starts[p], 0, steps[p] - 1)
    row = pl.BlockSpec((tm, d), lambda i, kk: (i, 0))
    in_specs = [pl.BlockSpec((tm, tk), lambda i, kk, p=p: (i, clip(kk, p))) for p in range(n)]
    in_specs += [pl.BlockSpec((d, tk), lambda i, kk: (0, kk)), row, pl.BlockSpec((1, d), lambda i, kk: (0, 0)), row]
    args = (*parts, w, h, nw, dres)
    if after is not None:
        in_specs.append(pl.BlockSpec(memory_space=pl.ANY))
        args += (after,)
    return pl.pallas_call(
        body, name=name, grid=(rows // tm, nk), in_specs=in_specs,
        out_specs=[row, pl.BlockSpec((1, d), lambda i, kk: (0, 0))],
        out_shape=[jax.ShapeDtypeStruct((rows, d), F32), jax.ShapeDtypeStruct((1, d), F32)],
        scratch_shapes=[pltpu.VMEM((tm, d), F32)], compiler_params=_params(("arbitrary", "arbitrary")),
    )(*args)


def _out_proj_norm(merged, w_o, h0, nw):
    rows, d = h0.shape
    tm = _row_tile(rows)

    def body(m_ref, w_ref, h_ref, nw_ref, h1_ref, u_ref):
        x = h_ref[...] + _mxu(m_ref[...], w_ref[...], NN)
        h1_ref[...] = x
        r = lax.rsqrt(jnp.mean(x * x, axis=-1, keepdims=True) + EPS)
        u_ref[...] = (x * r * nw_ref[...]).astype(ACT_DTYPE)

    spec = pl.BlockSpec((tm, d), lambda i: (i, 0))
    return pl.pallas_call(
        body, name="out_proj_norm", grid=(rows // tm,),
        in_specs=[pl.BlockSpec((tm, merged.shape[1]), lambda i: (i, 0)), pl.BlockSpec(w_o.shape, lambda i: (0, 0)), spec,
                  pl.BlockSpec((1, d), lambda i: (0, 0))],
        out_specs=[spec, spec], out_shape=[jax.ShapeDtypeStruct((rows, d), F32), jax.ShapeDtypeStruct((rows, d), ACT_DTYPE)],
        compiler_params=_params(("parallel",)),
    )(merged, w_o, h0, nw)


def _down_loss_head(act, w_dn, h1, tgt, w):
    rows, d = h1.shape
    tm = _row_tile(rows)

    def body(a_ref, wd_ref, h_ref, t_ref, w_ref, dh_ref, loss_ref, dw_ref):
        i = pl.program_id(0)
        x = h_ref[...] + _mxu(a_ref[...], wd_ref[...], NN)
        r = lax.rsqrt(jnp.mean(x * x, axis=-1, keepdims=True) + EPS)
        xhat = x * r
        wv = w_ref[...]
        row = i * tm + lax.broadcasted_iota(jnp.int32, (tm, 1), 0)
        live = row >= PAD_ROWS
        diff = jnp.where(live, xhat * wv - t_ref[...], 0.0)
        dy = diff * (1.0 / d)
        dxn = dy * wv
        dh_ref[...] = r * (dxn - xhat * jnp.mean(dxn * xhat, axis=-1, keepdims=True))

        @pl.when(i == 0)
        def _():
            loss_ref[...] = jnp.zeros_like(loss_ref)
            dw_ref[...] = jnp.zeros_like(dw_ref)

        loss_ref[...] += 0.5 * jnp.sum(jnp.mean(diff * diff, axis=-1, keepdims=True))
        dw_ref[...] += jnp.sum(dy * xhat, axis=0, keepdims=True)

    return pl.pallas_call(
        body, name="down_loss_head", grid=(rows // tm,),
        in_specs=[pl.BlockSpec((tm, act.shape[1]), lambda i: (i, 0)), pl.BlockSpec(w_dn.shape, lambda i: (0, 0)),
                  pl.BlockSpec((tm, d), lambda i: (i, 0)), pl.BlockSpec((tm, d), lambda i: (i, 0)),
                  pl.BlockSpec((1, d), lambda i: (0, 0))],
        out_specs=[pl.BlockSpec((tm, d), lambda i: (i, 0)), pl.BlockSpec((8, LANES), lambda i: (0, 0)),
                   pl.BlockSpec((1, d), lambda i: (0, 0))],
        out_shape=[jax.ShapeDtypeStruct((rows, d), F32), jax.ShapeDtypeStruct((8, LANES), F32),
                   jax.ShapeDtypeStruct((1, d), F32)],
        compiler_params=_params(("arbitrary",)),
    )(act, w_dn, h1, tgt, w)


def _prev_halo_spec(tm, width, col):
    return pl.BlockSpec((HALO, width), lambda j, i: (jnp.maximum(i * (tm // HALO) - 1, 0), col(j)))


def _next_halo_spec(tm, rows, width, col):
    last = rows // HALO - 1
    return pl.BlockSpec((HALO, width), lambda j, i: (jnp.minimum((i + 1) * (tm // HALO), last), col(j)))


def _conv_taps(cat, w_ref, b_ref, kw):
    acc = b_ref[...] + w_ref[kw - 1:kw, :] * cat
    for s in range(1, kw):
        acc = acc + w_ref[kw - 1 - s:kw - s, :] * pltpu.roll(cat, s, 0)
    return acc


def _conv_back(dpre, w_ref, kw):
    n = dpre.shape[0]
    acc = w_ref[kw - 1:kw, :] * dpre
    for s in range(1, kw):
        acc = acc + w_ref[kw - 1 - s:kw - s, :] * pltpu.roll(dpre, n - s, 0)
    return acc


def _xbc_proj_conv(u1, w_in_p, col0, w, b):
    rows, d = u1.shape
    width = w.shape[1]
    tm, tc = _row_tile(rows), 512
    cb0 = col0 // tc
    assert col0 % tc == 0

    def body(u_ref, m_ref, w_ref, b_ref, x_ref, o_ref, carry):
        i = pl.program_id(1)

        @pl.when(i == 0)
        def _():
            carry[...] = jnp.zeros_like(carry)

        xb = _mxu(u_ref[...], m_ref[...], NN).astype(ACT_DTYPE)
        x_ref[...] = xb
        x = xb.astype(F32)
        cat = jnp.concatenate([carry[...], x], axis=0)
        carry[...] = x[tm - HALO:, :]
        pre = _conv_taps(cat, w_ref, b_ref, SSD_CONV)[HALO:]
        row = i * tm + lax.broadcasted_iota(jnp.int32, (tm, 1), 0)
        o_ref[...] = jnp.where(row >= FRONT, _silu(pre), 0.0).astype(ACT_DTYPE)

    main = pl.BlockSpec((tm, tc), lambda j, i: (i, j))
    par = lambda r: pl.BlockSpec((r, tc), lambda j, i: (0, j))
    act = jax.ShapeDtypeStruct((rows, width), ACT_DTYPE)
    return pl.pallas_call(
        body, name="xbc_proj_conv", grid=(width // tc, rows // tm),
        in_specs=[pl.BlockSpec((tm, d), lambda j, i: (i, 0)), pl.BlockSpec((d, tc), lambda j, i: (0, cb0 + j)),
                  par(SSD_CONV), par(1)],
        out_specs=[main, main], out_shape=[act, act], scratch_shapes=[pltpu.VMEM((HALO, tc), F32)],
        compiler_params=_params(("parallel", "arbitrary")),
    )(u1, w_in_p, w, b)


def _ssd_conv_bwd(xbc, dxs, dbm, dcm, w, b, dproj, col0):
    rows, width = xbc.shape
    tm, tc = _row_tile(rows), 512
    kw = SSD_CONV
    nx = dxs.shape[1] // tc
    assert dbm.shape[1] == tc and dcm.shape[1] == tc and width == (nx + 2) * tc and col0 % tc == 0

    def body(x_ref, xp_ref, xn_ref, d0_ref, d0n_ref, d1_ref, d1n_ref, d2_ref, d2n_ref, w_ref, b_ref, dp_ref,
             dx_ref, dw_ref, db_ref):
        j, i = pl.program_id(0), pl.program_id(1)
        xp = jnp.where(i == 0, 0.0, xp_ref[...].astype(F32))
        cat = jnp.concatenate([xp, x_ref[...].astype(F32), xn_ref[...].astype(F32)], axis=0)
        sh = [cat] + [pltpu.roll(cat, s, 0) for s in range(1, kw)]
        pre = b_ref[...] + w_ref[kw - 1:kw, :] * sh[0]
        for s in range(1, kw):
            pre = pre + w_ref[kw - 1 - s:kw - s, :] * sh[s]
        pre = pre[HALO:]
        row = i * tm + lax.broadcasted_iota(jnp.int32, (tm + HALO, 1), 0)
        live = (row >= FRONT) & (row < rows)
        pick = lambda a, bb, c: jnp.where(j < nx, a[...], jnp.where(j == nx, bb[...], c[...])).astype(F32)
        dout = jnp.concatenate([pick(d0_ref, d1_ref, d2_ref), pick(d0n_ref, d1n_ref, d2n_ref)], axis=0)
        dpre = jnp.where(live, dout * _dsilu(pre), 0.0)
        dx_ref[...] = _conv_back(dpre, w_ref, kw)[:tm].astype(ACT_DTYPE)

        @pl.when(i == 0)
        def _():
            dw_ref[...] = jnp.zeros_like(dw_ref)
            db_ref[...] = jnp.zeros_like(db_ref)

        dmain = dpre[:tm]
        db_ref[...] += jnp.sum(dmain, axis=0, keepdims=True)
        for k in range(kw):
            dw_ref[k:k + 1, :] += jnp.sum(dmain * sh[kw - 1 - k][HALO:HALO + tm], axis=0, keepdims=True)

    main = pl.BlockSpec((tm, tc), lambda j, i: (i, j))
    par = lambda r: pl.BlockSpec((r, tc), lambda j, i: (0, j))
    col = lambda j: j
    xcol, zero = (lambda j: jnp.minimum(j, nx - 1)), (lambda j: 0)
    dspecs = lambda c: [pl.BlockSpec((tm, tc), lambda j, i: (i, c(j))), _next_halo_spec(tm, rows, tc, c)]
    return pl.pallas_call(
        body, name="ssd_conv_bwd", grid=(width // tc, rows // tm),
        in_specs=[main, _prev_halo_spec(tm, tc, col), _next_halo_spec(tm, rows, tc, col)]
        + dspecs(xcol) + dspecs(zero) + dspecs(zero) + [par(kw), par(1), pl.BlockSpec(memory_space=pl.ANY)],
        out_specs=[pl.BlockSpec((tm, tc), lambda j, i: (i, col0 // tc + j)), par(kw), par(1)],
        out_shape=[jax.ShapeDtypeStruct(dproj.shape, dproj.dtype), jax.ShapeDtypeStruct((kw, width), F32),
                   jax.ShapeDtypeStruct((1, width), F32)],
        input_output_aliases={11: 0}, compiler_params=_params(("parallel", "arbitrary")),
    )(xbc, xbc, xbc, dxs, dxs, dbm, dbm, dcm, dcm, w, b, dproj)


def _ffn_up_conv(u2, w_up, w, b):
    rows, d = u2.shape
    width = w_up.shape[1]
    dff = width // 2
    tm, tc = _row_tile(rows), _pick(dff, (256, 128))
    nb = dff // tc
    kw = FFN_CONV

    def body(u_ref, mg_ref, mv_ref, wg_ref, bg_ref, wv_ref, bv_ref, ug_ref, uv_ref, o_ref, cg, cv):
        i = pl.program_id(1)

        @pl.when(i == 0)
        def _():
            cg[...] = jnp.zeros_like(cg)
            cv[...] = jnp.zeros_like(cv)

        def pre(m_ref, up_ref, carry, w_ref, b_ref):
            upb = _mxu(u_ref[...], m_ref[...], NN).astype(ACT_DTYPE)
            up_ref[...] = upb
            x = upb.astype(F32)
            cat = jnp.concatenate([carry[...], x], axis=0)
            carry[...] = x[tm - HALO:, :]
            return _conv_taps(cat, w_ref, b_ref, kw)[HALO:]

        ag = pre(mg_ref, ug_ref, cg, wg_ref, bg_ref)
        av = pre(mv_ref, uv_ref, cv, wv_ref, bv_ref)
        o_ref[...] = (_silu(ag) * av).astype(ACT_DTYPE)

    gcol, vcol = (lambda j: j), (lambda j: j + nb)
    mat = lambda col: pl.BlockSpec((d, tc), lambda j, i: (0, col(j)))
    par = lambda r, col: pl.BlockSpec((r, tc), lambda j, i: (0, col(j)))
    out = pl.BlockSpec((tm, tc), lambda j, i: (i, j))
    act = jax.ShapeDtypeStruct((rows, dff), ACT_DTYPE)
    return pl.pallas_call(
        body, name="ffn_up_conv", grid=(nb, rows // tm),
        in_specs=[pl.BlockSpec((tm, d), lambda j, i: (i, 0)), mat(gcol), mat(vcol),
                  par(kw, gcol), par(1, gcol), par(kw, vcol), par(1, vcol)],
        out_specs=[out, out, out], out_shape=[act, act, act],
        scratch_shapes=[pltpu.VMEM((HALO, tc), F32), pltpu.VMEM((HALO, tc), F32)],
        compiler_params=_params(("parallel", "arbitrary")),
    )(u2, w_up, w_up, w, b, w, b)


def _ffn_conv_bwd(up_g, up_v, dact, w, b, u2):
    rows, dff = up_g.shape
    d = u2.shape[1]
    tm, tc = _row_tile(rows), _pick(dff, (256, 128))
    nb = dff // tc
    kw = FFN_CONV

    sb = 16

    def body(g_ref, gp_ref, gn_ref, v_ref, vp_ref, vn_ref, d_ref, dn_ref, wg_ref, bg_ref, wv_ref, bv_ref, u_ref,
             dxg_ref, dxv_ref, dwg_ref, dbg_ref, dwv_ref, dbv_ref, gwg_ref, gwv_ref, xg_s, xv_s, dd_s, og_s, ov_s):
        i = pl.program_id(1)
        last = i == rows // tm - 1
        for x_s, x_ref, xp_ref, xn_ref in ((xg_s, g_ref, gp_ref, gn_ref), (xv_s, v_ref, vp_ref, vn_ref)):
            x_s[0:HALO, :] = jnp.where(i == 0, 0.0, xp_ref[...].astype(F32))
            x_s[HALO:HALO + tm, :] = x_ref[...].astype(F32)
            x_s[HALO + tm:, :] = xn_ref[...].astype(F32)
        dd_s[0:tm, :] = d_ref[...].astype(F32)
        dd_s[tm:, :] = jnp.where(last, 0.0, dn_ref[...].astype(F32))

        wg = [wg_ref[k:k + 1, :] for k in range(kw)]
        wv = [wv_ref[k:k + 1, :] for k in range(kw)]
        bg, bv = bg_ref[...], bv_ref[...]

        def taps(x_s, e0, w, bias):
            win = x_s[pl.ds(e0 + HALO - sb, 2 * sb), :]
            sh = [win[sb:], pltpu.roll(win, 1, 0)[sb:], pltpu.roll(win, 2, 0)[sb:]]
            return bias + w[2] * sh[0] + w[1] * sh[1] + w[0] * sh[2], sh

        def dpre_of(e0):
            ag, sh_g = taps(xg_s, e0, wg, bg)
            av, sh_v = taps(xv_s, e0, wv, bv)
            dout = dd_s[pl.ds(e0, sb), :]
            s = jax.nn.sigmoid(ag)
            silu = ag * s
            return dout * av * (s + silu * (1.0 - s)), dout * silu, sh_g, sh_v

        def back(dp, nxt, w):
            cat = jnp.concatenate([dp, nxt], axis=0)
            return w[2] * dp + w[1] * pltpu.roll(cat, 2 * sb - 1, 0)[:sb] + w[0] * pltpu.roll(cat, 2 * sb - 2, 0)[:sb]

        nxt_g, nxt_v, _, _ = dpre_of(tm)
        acc_g = acc_v = tuple(jnp.zeros((sb, tc), F32) for _ in range(kw + 1))
        for e0 in range(tm - sb, -1, -sb):
            dpg, dpv, sh_g, sh_v = dpre_of(e0)
            og_s[e0:e0 + sb, :] = back(dpg, nxt_g, wg)
            ov_s[e0:e0 + sb, :] = back(dpv, nxt_v, wv)
            acc_g = tuple(a + dpg * t for a, t in zip(acc_g, (sh_g[2], sh_g[1], sh_g[0], 1.0)))
            acc_v = tuple(a + dpv * t for a, t in zip(acc_v, (sh_v[2], sh_v[1], sh_v[0], 1.0)))
            nxt_g, nxt_v = dpg, dpv

        @pl.when(i == 0)
        def _():
            for r in (dwg_ref, dbg_ref, dwv_ref, dbv_ref, gwg_ref, gwv_ref):
                r[...] = jnp.zeros_like(r)

        for acc, o_s, dx_ref, dw_ref, db_ref, gw_ref in ((acc_g, og_s, dxg_ref, dwg_ref, dbg_ref, gwg_ref),
                                                        (acc_v, ov_s, dxv_ref, dwv_ref, dbv_ref, gwv_ref)):
            dx = o_s[...].astype(ACT_DTYPE)
            dx_ref[...] = dx
            gw_ref[...] += _mxu(u_ref[...], dx, TN)
            for k in range(kw):
                dw_ref[k:k + 1, :] += jnp.sum(acc[k], axis=0, keepdims=True)
            db_ref[...] += jnp.sum(acc[kw], axis=0, keepdims=True)

    gcol, vcol = (lambda j: j), (lambda j: j + nb)
    main = lambda col: pl.BlockSpec((tm, tc), lambda j, i: (i, col(j)))
    par = lambda r, col: pl.BlockSpec((r, tc), lambda j, i: (0, col(j)))
    halos = lambda col: [_prev_halo_spec(tm, tc, col), _next_halo_spec(tm, rows, tc, col)]
    act_shape = jax.ShapeDtypeStruct((rows, dff), ACT_DTYPE)
    par_shapes = [jax.ShapeDtypeStruct((kw, dff), F32), jax.ShapeDtypeStruct((1, dff), F32)]
    gw_shape = jax.ShapeDtypeStruct((d, dff), F32)
    return pl.pallas_call(
        body, name="ffn_conv_bwd", grid=(nb, rows // tm),
        in_specs=[main(gcol)] + halos(gcol) + [main(gcol)] + halos(gcol) + [main(gcol), _next_halo_spec(tm, rows, tc, gcol),
                  par(kw, gcol), par(1, gcol), par(kw, vcol), par(1, vcol), pl.BlockSpec((tm, d), lambda j, i: (i, 0))],
        out_specs=[main(gcol), main(gcol), par(kw, gcol), par(1, gcol), par(kw, gcol), par(1, gcol), par(d, gcol), par(d, gcol)],
        out_shape=[act_shape, act_shape] + par_shapes + par_shapes + [gw_shape, gw_shape],
        scratch_shapes=[pltpu.VMEM((tm + 2 * HALO, tc), F32), pltpu.VMEM((tm + 2 * HALO, tc), F32),
                        pltpu.VMEM((tm + HALO, tc), F32), pltpu.VMEM((tm, tc), F32), pltpu.VMEM((tm, tc), F32)],
        compiler_params=_params(("parallel", "arbitrary")),
    )(up_g, up_g, up_g, up_v, up_v, up_v, dact, dact, w, b, w, b, u2)


def _ssd_scalars(dtr, dt_bias, a_log, live):
    q = CHUNK
    pre = dtr + dt_bias
    dt = jnp.where(live, jax.nn.softplus(pre), 0.0)
    a_neg = -jnp.exp(a_log)
    li = lax.broadcasted_iota(jnp.int32, (q, q), 0)
    si = lax.broadcasted_iota(jnp.int32, (q, q), 1)
    causal = li >= si
    tri = jnp.where(causal, 1.0, 0.0).astype(F32)
    a_cs = sum(_mxu(tri, p, NN) for p in _split(dt * a_neg, 3))
    return pre, dt, a_neg, a_cs, causal, tri


def _head_select():
    r = lax.broadcasted_iota(jnp.int32, (LANES, SSD_GW), 0)
    c = lax.broadcasted_iota(jnp.int32, (LANES, SSD_GW), 1)
    return jnp.where(c // SSD_HEAD_DIM == r, 1.0, 0.0).astype(MXU_DTYPE)


def _split(t, parts):
    out, rem = [], t
    for _ in range(parts):
        p = rem.astype(MXU_DTYPE)
        out.append(p)
        rem = rem - p.astype(F32)
    return out


def _stacked(ts, parts, sel, dn):
    out = _mxu(jnp.concatenate([p for t, n in zip(ts, parts, strict=True) for p in _split(t, n)], axis=0), sel, dn)
    res, r0 = [], 0
    for t, n in zip(ts, parts, strict=True):
        r = t.shape[0]
        res.append(sum(out[r0 + k * r:r0 + (k + 1) * r] for k in range(n)))
        r0 += n * r
    return res


def _head_cols(ts, sel):
    return _stacked(ts, [2] * len(ts), sel, NN)


def _head_sums(ts, parts, sel):
    return _stacked(ts, parts, sel, NT)


def _half_masks():
    lane = lax.broadcasted_iota(jnp.int32, (CHUNK, LANES), 1)
    return lane < SSD_HEAD_DIM, lane >= SSD_HEAD_DIM


def _ssd_scan(xs, bm, cm, dtr, prev, dt_bias, a_log, d_skip, live):
    q = CHUNK
    sel = _head_select()
    _, dt, _, a_cs, causal, _ = _ssd_scalars(dtr, dt_bias, a_log, live)
    a_cs_t = a_cs.T
    a_end = a_cs[q - 1:q, :]
    dt_x, e_x, w_x, d_x = _head_cols([dt, jnp.exp(a_cs), jnp.exp(a_end - a_cs), jnp.broadcast_to(d_skip, (16, LANES))], sel)
    xdt = xs * dt_x
    cb = _dot(cm, bm, NT)
    y = _dot(cm, prev) * e_x + d_x[0:1] * xs
    new = prev * e_x[q - 1:q, :] + _dot(bm, xdt * w_x, TN)
    masks = _half_masks()
    ys = []
    for pp in range(SSD_HPG // 2):
        xpair = xdt[:, pp * LANES:(pp + 1) * LANES]
        acc = jnp.zeros((q, LANES), F32)
        for half in range(2):
            hh = 2 * pp + half
            decay = jnp.exp(jnp.where(causal, a_cs[:, hh:hh + 1] - a_cs_t[hh:hh + 1, :], -jnp.inf))
            acc = acc + _dot(cb * decay, jnp.where(masks[half], xpair, 0.0))
        ys.append(acc)
    return y + jnp.concatenate(ys, axis=1), new


def _ssd_gate(y, z, nw):
    yz = y * _silu(z)
    return yz * lax.rsqrt(jnp.mean(yz * yz, axis=-1, keepdims=True) + EPS) * nw


def _ssd_scan_bwd(xs, bm, cm, dtr, prev, dt_bias, a_log, d_skip, live, dy, dnew):
    q = CHUNK
    sel = _head_select()
    pre, dt, a_neg, a_cs, causal, tri = _ssd_scalars(dtr, dt_bias, a_log, live)
    a_cs_t = a_cs.T
    a_end = a_cs[q - 1:q, :]
    dt_x, e_x, w_x, d_x = _head_cols([dt, jnp.exp(a_cs), jnp.exp(a_end - a_cs), jnp.broadcast_to(d_skip, (16, LANES))], sel)
    g_x, d_x = e_x[q - 1:q, :], d_x[0:1]
    xdt = xs * dt_x
    u = xdt * w_x
    cb = _mxu(cm, bm, NT)
    cs = _mxu(cm, prev, NN)
    dye = dy * e_x
    dcm = _mxu(dye, prev, NT)
    dprev = _mxu(cm, dye, TN) + dnew * g_x
    dacs_x = dye * cs
    dbm = _mxu(u, dnew, NT)
    du = _mxu(bm, dnew, NN)
    dw_x = du * u
    dacs_x = dacs_x - dw_x
    dend_x = jnp.sum(dw_x + dnew * prev * g_x, axis=0, keepdims=True)
    dxdt = du * w_x
    lane = lax.broadcasted_iota(jnp.int32, (q, LANES), 1)
    sub = lax.broadcasted_iota(jnp.int32, (q, LANES), 0)
    dcb = jnp.zeros((q, q), F32)
    dacs = jnp.zeros((q, LANES), F32)
    dacs_t = jnp.zeros((q, LANES), F32)
    masks = _half_masks()
    dxdt_p = []
    for pp in range(SSD_HPG // 2):
        ps = slice(pp * LANES, (pp + 1) * LANES)
        acc = jnp.zeros((q, LANES), F32)
        for half in range(2):
            hh = 2 * pp + half
            decay = jnp.exp(jnp.where(causal, a_cs[:, hh:hh + 1] - a_cs_t[hh:hh + 1, :], -jnp.inf))
            m = cb * decay
            dyh = jnp.where(masks[half], dy[:, ps], 0.0)
            dm = _mxu(dyh, xdt[:, ps], NT)
            acc = acc + _mxu(m, dyh, TN)
            dcb = dcb + dm * decay
            p = dm * m
            dacs = jnp.where(lane == hh, jnp.sum(p, axis=1, keepdims=True), dacs)
            dacs_t = jnp.where(sub == hh, jnp.sum(p, axis=0, keepdims=True), dacs_t)
        dxdt_p.append(acc)
    dcm = dcm + _mxu(dcb, bm, NN)
    dbm = dbm + _mxu(dcb, cm, TN)
    dxdt = dxdt + jnp.concatenate(dxdt_p, axis=1)
    dxs = dy * d_x + dxdt * dt_x
    rows_x = jnp.concatenate([dend_x, jnp.sum(dy * xs, axis=0, keepdims=True), jnp.zeros((14, SSD_GW), F32)], axis=0)
    dacs_h, ddt_h, rows = _head_sums([dacs_x, dxdt * xs, rows_x], [3, 2, 3], sel)
    dacs = dacs - dacs_t.T + dacs_h
    dacs = dacs + jnp.where(sub == q - 1, rows[0:1], 0.0)
    tri_t = jnp.where(causal, 0.0, 1.0).astype(F32) + jnp.where(lane == sub, 1.0, 0.0)
    da = jnp.dot(tri_t, dacs, precision=lax.Precision.HIGHEST, preferred_element_type=F32)
    ddt = ddt_h + da * a_neg
    dalog = jnp.sum(da * dt, axis=0, keepdims=True) * a_neg
    ddtr = jnp.where(live, ddt * jax.nn.sigmoid(pre), 0.0)
    dbias = jnp.sum(ddtr, axis=0, keepdims=True)
    return dxs, dbm, dcm, ddtr, dprev, dbias, dalog, rows[1:2]


def _chunks_per_step(nc):
    return 13 if nc % 13 == 0 else 1


def _ssd_specs(rev, nc):
    per = _chunks_per_step(nc)
    steps = nc // per
    sidx = (lambda s: steps - 1 - s) if rev else (lambda s: s)
    nb_b = SSD_D_INNER // SSD_STATE
    row = lambda width, col=lambda g: g: pl.BlockSpec((per * CHUNK, width), lambda g, s: (sidx(s), col(g)))
    par = lambda width: pl.BlockSpec((1, width), lambda g, s: (0, g))
    state = lambda: pl.BlockSpec((per, 1, SSD_STATE, SSD_GW), lambda g, s: (sidx(s), g, 0, 0))
    xbc = [row(SSD_GW), row(SSD_STATE, lambda g: nb_b + g), row(SSD_STATE, lambda g: nb_b + SSD_GROUPS + g)]
    return per, steps, sidx, row, par, state, xbc


def _ssd_fwd(xbc_c, dtr, z, dt_bias, a_log, d_skip, nw):
    rows = z.shape[0]
    nc = rows // CHUNK
    per, steps, _, row, par, state, xbc = _ssd_specs(False, nc)

    def body(xs_ref, b_ref, c_ref, dt_ref, z_ref, bias_ref, al_ref, dk_ref, nw_ref, o_ref, y_ref, st_ref, carry):
        s = pl.program_id(1)

        @pl.when(s == 0)
        def _():
            carry[...] = jnp.zeros_like(carry)

        for j in range(per):
            rs = pl.ds(j * CHUNK, CHUNK)
            live = (s * per + j) * CHUNK + lax.broadcasted_iota(jnp.int32, (CHUNK, 1), 0) >= FRONT
            prev = carry[...]
            st_ref[j, 0] = prev
            y, new = _ssd_scan(xs_ref[rs, :].astype(F32), b_ref[rs, :].astype(F32), c_ref[rs, :].astype(F32), dt_ref[rs, :],
                               prev, bias_ref[...], al_ref[...], dk_ref[...], live)
            y_ref[rs, :] = y.astype(ACT_DTYPE)
            o_ref[rs, :] = _ssd_gate(y, z_ref[rs, :].astype(F32), nw_ref[...]).astype(ACT_DTYPE)
            carry[...] = new

    act = jax.ShapeDtypeStruct((rows, SSD_D_INNER), ACT_DTYPE)
    return pl.pallas_call(
        body, name="ssd_fwd", grid=(SSD_GROUPS, steps),
        in_specs=xbc + [row(LANES), row(SSD_GW), par(LANES), par(LANES), par(LANES), par(SSD_GW)],
        out_specs=[row(SSD_GW), row(SSD_GW), state()],
        out_shape=[act, act, jax.ShapeDtypeStruct((nc, SSD_GROUPS, SSD_STATE, SSD_GW), F32)],
        scratch_shapes=[pltpu.VMEM((SSD_STATE, SSD_GW), F32)],
        compiler_params=_params(("parallel", "arbitrary")),
    )(xbc_c, xbc_c, xbc_c, dtr, z, dt_bias, a_log, d_skip, nw)


def _ssd_gate_bwd(y, z, d_bs, w_bs, nw, dproj, col0):
    rows = y.shape[0]
    d = d_bs.shape[1]
    tm = _row_tile(rows)
    assert col0 % SSD_GW == 0

    def body(y_ref, z_ref, db_ref, wb_ref, nw_ref, dp_ref, dy_ref, dz_ref, dnw_ref):
        yv, zv = y_ref[...].astype(F32), z_ref[...].astype(F32)
        dov = _mxu(db_ref[...], wb_ref[...], NT)
        s = jax.nn.sigmoid(zv)
        silu = zv * s
        yz = yv * silu
        r = lax.rsqrt(jnp.mean(yz * yz, axis=-1, keepdims=True) + EPS)
        yhat = yz * r
        dn = dov * nw_ref[...]
        dyz = r * (dn - yhat * jnp.mean(dn * yhat, axis=-1, keepdims=True))
        dy_ref[...] = (dyz * silu).astype(ACT_DTYPE)
        dz_ref[...] = (dyz * yv * (s + silu * (1.0 - s))).astype(ACT_DTYPE)

        @pl.when(pl.program_id(1) == 0)
        def _():
            dnw_ref[...] = jnp.zeros_like(dnw_ref)

        dnw_ref[...] += jnp.sum(dov * yhat, axis=0, keepdims=True)

    spec = pl.BlockSpec((tm, SSD_GW), lambda g, i: (i, g))
    par = pl.BlockSpec((1, SSD_GW), lambda g, i: (0, g))
    act = jax.ShapeDtypeStruct((rows, SSD_D_INNER), ACT_DTYPE)
    return pl.pallas_call(
        body, name="ssd_gate_bwd", grid=(SSD_GROUPS, rows // tm),
        in_specs=[spec, spec, pl.BlockSpec((tm, d), lambda g, i: (i, 0)), pl.BlockSpec((SSD_GW, d), lambda g, i: (g, 0)),
                  par, pl.BlockSpec(memory_space=pl.ANY)],
        out_specs=[spec, pl.BlockSpec((tm, SSD_GW), lambda g, i: (i, col0 // SSD_GW + g)), par],
        out_shape=[act, jax.ShapeDtypeStruct(dproj.shape, dproj.dtype), jax.ShapeDtypeStruct((1, SSD_D_INNER), F32)],
        input_output_aliases={5: 1}, compiler_params=_params(("parallel", "arbitrary")),
    )(y, z, d_bs, w_bs, nw, dproj)


def _ssd_bwd(xbc_c, dtr, dt_bias, a_log, d_skip, states, dy, dproj, col0):
    rows = dy.shape[0]
    nc = rows // CHUNK
    per, steps, sidx, row, par, state, xbc = _ssd_specs(True, nc)
    assert col0 % LANES == 0

    def body(xs_ref, b_ref, c_ref, dt_ref, bias_ref, al_ref, dk_ref, st_ref, dy_ref, dp_ref,
             dxs_ref, db_ref, dc_ref, ddt_ref, dbias_ref, dal_ref, ddk_ref, carry):
        s = pl.program_id(1)

        @pl.when(s == 0)
        def _():
            carry[...] = jnp.zeros_like(carry)
            for r in (dbias_ref, dal_ref, ddk_ref):
                r[...] = jnp.zeros_like(r)

        for j in reversed(range(per)):
            rs = pl.ds(j * CHUNK, CHUNK)
            live = (sidx(s) * per + j) * CHUNK + lax.broadcasted_iota(jnp.int32, (CHUNK, 1), 0) >= FRONT
            dxs, dbm, dcm, ddt, dprev, dbias, dal, ddk = _ssd_scan_bwd(
                xs_ref[rs, :].astype(F32), b_ref[rs, :].astype(F32), c_ref[rs, :].astype(F32), dt_ref[rs, :], st_ref[j, 0],
                bias_ref[...], al_ref[...], dk_ref[...], live, dy_ref[rs, :].astype(F32), carry[...])
            dxs_ref[rs, :] = dxs.astype(ACT_DTYPE)
            db_ref[rs, :] = dbm.astype(ACT_DTYPE)
            dc_ref[rs, :] = dcm.astype(ACT_DTYPE)
            ddt_ref[rs, :] = ddt.astype(ACT_DTYPE)
            carry[...] = dprev
            dbias_ref[...] += dbias
            dal_ref[...] += dal
            ddk_ref[...] += ddk

    bc = jax.ShapeDtypeStruct((rows, SSD_GROUPS * SSD_STATE), ACT_DTYPE)
    head = jax.ShapeDtypeStruct((1, SSD_GROUPS * LANES), F32)
    return pl.pallas_call(
        body, name="ssd_bwd", grid=(SSD_GROUPS, steps),
        in_specs=xbc + [row(LANES), par(LANES), par(LANES), par(LANES), state(), row(SSD_GW), pl.BlockSpec(memory_space=pl.ANY)],
        out_specs=[row(SSD_GW), row(SSD_STATE), row(SSD_STATE), row(LANES, lambda g: col0 // LANES + g),
                   par(LANES), par(LANES), par(LANES)],
        out_shape=[jax.ShapeDtypeStruct((rows, SSD_D_INNER), ACT_DTYPE), bc, bc,
                   jax.ShapeDtypeStruct(dproj.shape, dproj.dtype), head, head, head],
        input_output_aliases={9: 3}, scratch_shapes=[pltpu.VMEM((SSD_STATE, SSD_GW), F32)],
        compiler_params=_params(("parallel", "arbitrary")),
    )(xbc_c, xbc_c, xbc_c, dtr, dt_bias, a_log, d_skip, states, dy, dproj)


def _rotary_tables(rows):
    pos = np.arange(rows, dtype=np.float32) - np.float32(FRONT)
    inv_freq = np.float32(ROPE_BASE) ** (-np.linspace(0.0, 1.0, RET_QK // 2, dtype=np.float32))
    ang = (pos[:, None] * inv_freq[None, :]).astype(np.float32).astype(np.float64)
    lgam = np.log(1.0 - 2.0 ** (-5.0 - np.arange(RET_HEADS, dtype=np.float64))).astype(np.float32)
    lgam = np.broadcast_to(lgam[:, None, None], (RET_HEADS, 8, LANES))
    return jnp.asarray(np.cos(ang).astype(np.float32)), jnp.asarray(np.sin(ang).astype(np.float32)), jnp.asarray(lgam)


def _rotary(t, cos, sin):
    half = t.shape[-1] // 2
    t1, t2 = t[:, :half], t[:, half:]
    return jnp.concatenate([t1 * cos - t2 * sin, t2 * cos + t1 * sin], axis=1)


def _ret_chunk(qh, kh, vh, gh, prev, cos, sin, lg):
    q = CHUNK
    qr = _rotary(qh, cos, sin)
    kr = _rotary(kh, cos, sin) * (RET_QK ** -0.5)
    li = lax.broadcasted_iota(jnp.int32, (q, q), 0)
    si = lax.broadcasted_iota(jnp.int32, (q, q), 1)
    dist = (li - si).astype(F32)
    decay = jnp.exp(jnp.where(li >= si, dist * lg, -jnp.inf))
    idx = lax.broadcasted_iota(jnp.int32, (q, 1), 0).astype(F32)
    scores = _dot(qr, kr, NT) * decay
    out = _dot(scores, vh)
    kv = _dot(kr * jnp.exp((q - 1.0 - idx) * lg), vh, TN)
    out = out + _dot(qr, prev) * jnp.exp((idx + 1.0) * lg)
    new = prev * jnp.exp(q * lg) + kv
    out = out * lax.rsqrt(jnp.mean(out * out, axis=-1, keepdims=True) + EPS)
    return _silu(gh) * out, new


def _ret_specs(rev, nc):
    per = _chunks_per_step(nc)
    steps = nc // per
    sidx = (lambda s: steps - 1 - s) if rev else (lambda s: s)
    row = lambda width: pl.BlockSpec((per * CHUNK, width), lambda h, s: (sidx(s), h))
    tab = lambda: pl.BlockSpec((per * CHUNK, RET_QK // 2), lambda h, s: (sidx(s), 0))
    lgs = lambda: pl.BlockSpec((1, 8, LANES), lambda h, s: (h, 0, 0))
    state = lambda: pl.BlockSpec((per, 1, RET_QK, RET_V), lambda h, s: (sidx(s), h, 0, 0))
    part = lambda width, k: pl.BlockSpec((per * CHUNK, width), lambda h, s: (sidx(s), h * (RET_HW // width) + k))
    ins = [part(RET_QK, 0), part(RET_QK, 1), part(RET_V, 1), part(RET_V, 2), tab(), tab(), lgs()]
    return per, steps, sidx, row, state, ins


def _ret_fwd(qkvg, cos, sin, lgam):
    rows = qkvg.shape[0]
    nc = rows // CHUNK
    per, steps, _, row, state, ins = _ret_specs(False, nc)
    q = k = v = g = qkvg

    def body(q_ref, k_ref, v_ref, g_ref, cos_ref, sin_ref, lg_ref, y_ref, st_ref, carry):
        @pl.when(pl.program_id(1) == 0)
        def _():
            carry[...] = jnp.zeros_like(carry)

        for j in range(per):
            rs = pl.ds(j * CHUNK, CHUNK)
            prev = carry[...]
            st_ref[j, 0] = prev.astype(ACT_DTYPE)
            out, new = _ret_chunk(q_ref[rs, :].astype(F32), k_ref[rs, :].astype(F32), v_ref[rs, :].astype(F32),
                                  g_ref[rs, :].astype(F32), prev, cos_ref[rs, :], sin_ref[rs, :], lg_ref[0, 0:1, 0:1])
            y_ref[rs, :] = out.astype(ACT_DTYPE)
            carry[...] = new

    return pl.pallas_call(
        body, name="ret_fwd", grid=(RET_HEADS, steps), in_specs=ins, out_specs=[row(RET_V), state()],
        out_shape=[jax.ShapeDtypeStruct((rows, RET_HEADS * RET_V), ACT_DTYPE),
                   jax.ShapeDtypeStruct((nc, RET_HEADS, RET_QK, RET_V), ACT_DTYPE)],
        scratch_shapes=[pltpu.VMEM((RET_QK, RET_V), F32)],
        compiler_params=_params(("parallel", "arbitrary")),
    )(q, k, v, g, cos, sin, lgam)


def _ret_bwd(qkvg, cos, sin, lgam, states, dy, dproj):
    rows = qkvg.shape[0]
    nc = rows // CHUNK
    per, steps, sidx, row, state, ins = _ret_specs(True, nc)

    def body(q_ref, k_ref, v_ref, g_ref, cos_ref, sin_ref, lg_ref, st_ref, dy_ref, dp_ref, o_ref, carry):
        @pl.when(pl.program_id(1) == 0)
        def _():
            carry[...] = jnp.zeros_like(carry)

        for j in reversed(range(per)):
            rs = pl.ds(j * CHUNK, CHUNK)
            fn = functools.partial(_ret_chunk, cos=cos_ref[rs, :], sin=sin_ref[rs, :], lg=lg_ref[0, 0:1, 0:1])
            _, vjp = jax.vjp(fn, q_ref[rs, :].astype(F32), k_ref[rs, :].astype(F32), v_ref[rs, :].astype(F32),
                             g_ref[rs, :].astype(F32), st_ref[j, 0].astype(F32))
            dq, dk, dv, dg, dprev = vjp((dy_ref[rs, :].astype(F32), carry[...]))
            o_ref[rs, 0:RET_QK] = dq.astype(ACT_DTYPE)
            o_ref[rs, RET_QK:2 * RET_QK] = dk.astype(ACT_DTYPE)
            o_ref[rs, 2 * RET_QK:2 * RET_QK + RET_V] = dv.astype(ACT_DTYPE)
            o_ref[rs, 2 * RET_QK + RET_V:RET_HW] = dg.astype(ACT_DTYPE)
            carry[...] = dprev

    return pl.pallas_call(
        body, name="ret_bwd", grid=(RET_HEADS, steps),
        in_specs=ins + [state(), row(RET_V), pl.BlockSpec(memory_space=pl.ANY)],
        out_specs=pl.BlockSpec((per * CHUNK, RET_HW), lambda h, s: (sidx(s), h)),
        out_shape=jax.ShapeDtypeStruct(dproj.shape, dproj.dtype), input_output_aliases={9: 0},
        scratch_shapes=[pltpu.VMEM((RET_QK, RET_V), F32)],
        compiler_params=_params(("parallel", "arbitrary")),
    )(qkvg, qkvg, qkvg, qkvg, cos, sin, lgam, states, dy, dproj)


def _branches_merge(y_ssd, y_ret, w_bs, w_br, gates):
    rows, k = y_ssd.shape
    d = w_bs.shape[1]
    tm = _row_tile(rows)

    def body(ys_ref, yr_ref, ws_ref, wr_ref, gs_ref, gr_ref, bs_ref, br_ref, o_ref):
        bs = _mxu(ys_ref[...], ws_ref[...], NN).astype(ACT_DTYPE)
        br = _mxu(yr_ref[...], wr_ref[...], NN).astype(ACT_DTYPE)
        bs_ref[...] = bs
        br_ref[...] = br
        o_ref[...] = (jax.nn.sigmoid(gs_ref[...].astype(F32)) * bs.astype(F32)
                      + jax.nn.sigmoid(gr_ref[...].astype(F32)) * br.astype(F32)).astype(ACT_DTYPE)

    spec = pl.BlockSpec((tm, d), lambda i: (i, 0))
    yspec = pl.BlockSpec((tm, k), lambda i: (i, 0))
    wspec = pl.BlockSpec((k, d), lambda i: (0, 0))
    shp = jax.ShapeDtypeStruct((rows, d), ACT_DTYPE)
    return pl.pallas_call(
        body, name="branches_merge", grid=(rows // tm,),
        in_specs=[yspec, yspec, wspec, wspec, spec, pl.BlockSpec((tm, d), lambda i: (i, 1))],
        out_specs=[spec, spec, spec], out_shape=[shp, shp, shp], compiler_params=_params(("parallel",)),
    )(y_ssd, y_ret, w_bs, w_br, gates, gates)


def _merge_bwd(dh1, w_o, bs, br, gates, dproj, col0):
    rows, d = bs.shape
    tm = _row_tile(rows)
    assert col0 % (2 * d) == 0

    def body(dh_ref, wo_ref, bs_ref, br_ref, gs_ref, gr_ref, dp_ref, dbs_ref, dbr_ref, dg_ref):
        dmv = _mxu(dh_ref[...], wo_ref[...], NT)
        for k, (b_ref, g_ref, db_ref) in enumerate(((bs_ref, gs_ref, dbs_ref), (br_ref, gr_ref, dbr_ref))):
            s = jax.nn.sigmoid(g_ref[...].astype(F32))
            db_ref[...] = (dmv * s).astype(ACT_DTYPE)
            dg_ref[:, k * d:(k + 1) * d] = (dmv * b_ref[...].astype(F32) * s * (1.0 - s)).astype(ACT_DTYPE)

    spec = pl.BlockSpec((tm, d), lambda i: (i, 0))
    shp = jax.ShapeDtypeStruct((rows, d), ACT_DTYPE)
    return pl.pallas_call(
        body, name="merge_bwd", grid=(rows // tm,),
        in_specs=[spec, pl.BlockSpec(w_o.shape, lambda i: (0, 0)), spec, spec, spec, pl.BlockSpec((tm, d), lambda i: (i, 1)),
                  pl.BlockSpec(memory_space=pl.ANY)],
        out_specs=[spec, spec, pl.BlockSpec((tm, 2 * d), lambda i: (i, col0 // (2 * d)))],
        out_shape=[shp, shp, jax.ShapeDtypeStruct(dproj.shape, dproj.dtype)], input_output_aliases={6: 2},
        compiler_params=_params(("parallel",)),
    )(dh1, w_o, bs, br, gates, gates, dproj)


def _place():
    x, y, c = lax.axis_index("x"), lax.axis_index("y"), lax.axis_index("c")
    return x, y, c


def _slot(p):
    return 4 * p[0] + 2 * p[1] + p[2]


def _allgather(arrs, name):
    n = len(arrs)
    any_spec = pl.BlockSpec(memory_space=pl.ANY)

    def body(*refs):
        ins, outs = refs[:n], refs[n:2 * n]
        send_sems, recv_sems, local_sems = refs[2 * n:]
        x, y, c = _place()
        me, sibling = (x, y, c), (x, y, 1 - c)
        chips = [(1 - x, y), (x, 1 - y), (1 - x, 1 - y)]

        def copy(a, k, block, to, src=None):
            dst = outs[a].at[_slot(block)]
            return pltpu.make_async_remote_copy(
                src_ref=dst if src is None else src, dst_ref=dst, send_sem=send_sems.at[a * 7 + k],
                recv_sem=recv_sems.at[a * 7 + k], device_id=to, device_id_type=MESH)

        mine, first, passed = [], [], []
        for a in range(n):
            cp = pltpu.make_async_copy(ins[a], outs[a].at[_slot(me)], local_sems.at[a])
            cp.start()
            mine.append(cp)
            first.append(copy(a, 0, me, sibling, src=ins[a]))
            first += [copy(a, 1 + j, me, (*chip, c), src=ins[a]) for j, chip in enumerate(chips)]
        for cp in first:
            cp.start()
        for j, chip in enumerate(chips):
            for a in range(n):
                copy(a, 1 + j, (*chip, c), me).wait_recv()
                cp = copy(a, 4 + j, (*chip, c), sibling)
                cp.start()
                passed.append(cp)
        for a in range(n):
            copy(a, 0, sibling, me).wait_recv()
            for j, chip in enumerate(chips):
                copy(a, 4 + j, (*chip, 1 - c), me).wait_recv()
        for cp in first + passed:
            cp.wait_send()
        for cp in mine:
            cp.wait()

    return pl.pallas_call(
        body, name=name, in_specs=[any_spec] * n, out_specs=[any_spec] * n,
        out_shape=[jax.ShapeDtypeStruct((N_DEV,) + a.shape, a.dtype) for a in arrs],
        scratch_shapes=[pltpu.SemaphoreType.DMA((7 * n,)), pltpu.SemaphoreType.DMA((7 * n,)), pltpu.SemaphoreType.DMA((n,))],
    )(*arrs)


def _peers():
    x, y, c = _place()
    return (x, y, c), [(x ^ dx, y ^ dy, c ^ dc) for dx in (0, 1) for dy in (0, 1) for dc in (0, 1)][1:]


def _exchange_copies(srcs, lands, send_sems, recv_sems, scatter, sender):
    me, peers = _peers()
    out = []
    for a, (src, land) in enumerate(zip(srcs, lands, strict=True)):
        for k, peer in enumerate(peers):
            src_ref = src.at[_slot(peer)] if scatter else src
            out.append(pltpu.make_async_remote_copy(
                src_ref=src_ref, dst_ref=land.at[_slot(me if sender else peer)], send_sem=send_sems.at[a * 7 + k],
                recv_sem=recv_sems.at[a * 7 + k], device_id=peer, device_id_type=MESH))
    return out


_HBM = pl.BlockSpec(memory_space=pltpu.HBM)
_SEM = pl.BlockSpec(memory_space=pltpu.SEMAPHORE)
_EFFECT = pltpu.SideEffectType.DATAFLOW_SIDE_EFFECTING


def _exchange_start(srcs, scatter, name, after=None):
    n = len(srcs)
    land_shapes = [s.shape if scatter else (N_DEV,) + s.shape for s in srcs]
    n_in = 2 * n + (after is not None)

    def body(*refs):
        for cp in _exchange_copies(refs[:n], refs[n:2 * n], refs[n_in], refs[n_in + 1], scatter, True):
            cp.start()
        refs[-1][...] = jnp.zeros_like(refs[-1])

    args = [pltpu.with_memory_space_constraint(s, pltpu.HBM) for s in srcs]
    args += [pltpu.with_memory_space_constraint(lax.empty(shp, s.dtype), pltpu.HBM) for s, shp in zip(srcs, land_shapes)]
    thru_shapes = tuple(pltpu.HBM(a.shape, a.dtype) for a in args)
    extra = [] if after is None else [after]
    outs = pl.pallas_call(
        body, name=name,
        out_shape=(pltpu.SemaphoreType.DMA((7 * n,)), pltpu.SemaphoreType.DMA((7 * n,))) + thru_shapes
        + (jax.ShapeDtypeStruct((8, LANES), F32),),
        in_specs=[_HBM] * (2 * n) + [pl.BlockSpec(memory_space=pl.ANY)] * len(extra),
        out_specs=(_SEM, _SEM) + (_HBM,) * (2 * n) + (pl.BlockSpec(memory_space=pltpu.VMEM),),
        input_output_aliases={i: 2 + i for i in range(2 * n)},
        compiler_params=pltpu.CompilerParams(has_side_effects=_EFFECT),
    )(*args, *extra)
    return outs[:-1], outs[-1]


def _exchange_wait(handle, scatter, after, name):
    n = (len(handle) - 2) // 2
    thru = handle[2:]

    def body(*refs):
        for cp in _exchange_copies(refs[:n], refs[n:2 * n], refs[2 * n], refs[2 * n + 1], scatter, False):
            cp.wait_send()
            cp.wait_recv()

    outs = pl.pallas_call(
        body, name=name, out_shape=tuple(pltpu.HBM(t.shape, t.dtype) for t in thru),
        in_specs=[_HBM] * (2 * n) + [_SEM, _SEM, pl.BlockSpec(memory_space=pl.ANY)], out_specs=(_HBM,) * (2 * n),
        input_output_aliases={i: i for i in range(2 * n)},
        compiler_params=pltpu.CompilerParams(has_side_effects=_EFFECT),
    )(*thru, handle[0], handle[1], after)
    return list(outs[:n]), list(outs[n:])


def _allreduce_small(pack):
    rows, lanes = pack.shape

    def body(x_ref, o_ref, buf, send_sems, recv_sems):
        x, y, c = _place()
        me, sibling = (x, y, c), (x, y, 1 - c)
        chips = [(1 - x, y), (x, 1 - y), (1 - x, 1 - y)]

        def copy(k, block, to, src=None):
            dst = buf.at[_slot(block)]
            return pltpu.make_async_remote_copy(
                src_ref=dst if src is None else src, dst_ref=dst, send_sem=send_sems.at[k], recv_sem=recv_sems.at[k],
                device_id=to, device_id_type=MESH)

        buf[_slot(me)] = x_ref[...]
        first = [copy(0, me, sibling, src=x_ref)]
        first += [copy(1 + j, me, (*chip, c), src=x_ref) for j, chip in enumerate(chips)]
        for cp in first:
            cp.start()
        passed = [copy(4 + j, (*chip, c), sibling) for j, chip in enumerate(chips)]
        for j, chip in enumerate(chips):
            copy(1 + j, (*chip, c), me).wait_recv()
            passed[j].start()
        copy(0, sibling, me).wait_recv()
        for j, chip in enumerate(chips):
            copy(4 + j, (*chip, 1 - c), me).wait_recv()
        for cp in first + passed:
            cp.wait_send()
        acc = buf[0]
        for i in range(1, N_DEV):
            acc = acc + buf[i]
        o_ref[...] = acc

    vmem = pl.BlockSpec(memory_space=pltpu.VMEM)
    return pl.pallas_call(
        body, name="allreduce_small", in_specs=[vmem], out_specs=vmem,
        out_shape=jax.ShapeDtypeStruct((rows, lanes), F32),
        scratch_shapes=[pltpu.VMEM((N_DEV, rows, lanes), F32), pltpu.SemaphoreType.DMA((7,)), pltpu.SemaphoreType.DMA((7,))],
        compiler_params=pltpu.CompilerParams(vmem_limit_bytes=VMEM_LIMIT),
    )(pack)


def _adamw(w, g, m, v):
    m = ADAM_B1 * m + (1.0 - ADAM_B1) * g
    v = ADAM_B2 * v + (1.0 - ADAM_B2) * jnp.square(g)
    m_hat = m / (1.0 - ADAM_B1 ** ADAM_STEP)
    v_hat = v / (1.0 - ADAM_B2 ** ADAM_STEP)
    delta = -ADAM_LR * (m_hat / (jnp.sqrt(v_hat) + ADAM_EPS) + ADAM_WD * w)
    return delta, m, v


def _adam_shard(own, parts, w, m, v, name):
    r, c = w.shape
    tr = _pick(r, (128, 64, 32, 16, 8))

    def body(own_ref, p_ref, w_ref, m_ref, v_ref, g_ref, d_ref, nm_ref, nv_ref):
        _, peers = _peers()
        g = own_ref[...].astype(F32)
        for peer in peers:
            g = g + p_ref[_slot(peer)].astype(F32)
        g_ref[...] = g
        d_ref[...], nm_ref[...], nv_ref[...] = _adamw(w_ref[...], g, m_ref[...], v_ref[...])

    spec = pl.BlockSpec((tr, c), lambda i: (i, 0))
    shp = jax.ShapeDtypeStruct((r, c), F32)
    return pl.pallas_call(
        body, name=name, grid=(r // tr,),
        in_specs=[spec, pl.BlockSpec((N_DEV, tr, c), lambda i: (0, i, 0)), spec, spec, spec], out_specs=[spec] * 4,
        out_shape=[shp] * 4, compiler_params=_params(("parallel",)),
    )(own, parts, w, m, v)


def _adam_small(w, g, m, v):
    r, c = w.shape

    def body(w_ref, g_ref, m_ref, v_ref, d_ref, nm_ref, nv_ref):
        d_ref[...], nm_ref[...], nv_ref[...] = _adamw(w_ref[...], g_ref[...], m_ref[...], v_ref[...])

    shp = jax.ShapeDtypeStruct((r, c), F32)
    return pl.pallas_call(body, name="adam_small", out_shape=[shp] * 3)(w, g, m, v)


def _column_plan(pieces, shard_w):
    plan = []
    for c0, width, d0 in pieces:
        c = c0
        while c < c0 + width:
            s, a = divmod(c, shard_w)
            w = min(c0 + width - c, shard_w - a)
            plan.append((s, a, w, d0 + c - c0))
            c += w
    return plan


def _cols_from_shards(g, plan, out_w, zero, name):
    _, r, sw = g.shape
    tr = _pick(r, (128,))

    def body(x_ref, o_ref):
        for d0, w in zero:
            o_ref[:, d0:d0 + w] = jnp.zeros((tr, w), g.dtype)
        for s, a, w, d0 in plan:
            o_ref[:, d0:d0 + w] = x_ref[s, :, a:a + w]

    return pl.pallas_call(
        body, name=name, grid=(r // tr,), in_specs=[pl.BlockSpec((N_DEV, tr, sw), lambda i: (0, i, 0))],
        out_specs=pl.BlockSpec((tr, out_w), lambda i: (i, 0)), out_shape=jax.ShapeDtypeStruct((r, out_w), g.dtype),
        compiler_params=_params(("parallel",)),
    )(g)


def _shards_from_cols(srcs, plans, shard_w, name):
    r = srcs[0].shape[0]
    tr = _pick(r, (128,))
    n = len(srcs)

    def body(*refs):
        o_ref = refs[n]
        for x_ref, plan in zip(refs[:n], plans, strict=True):
            for s, a, w, d0 in plan:
                o_ref[s, :, a:a + w] = x_ref[:, d0:d0 + w].astype(COMM_DTYPE)

    return pl.pallas_call(
        body, name=name, grid=(r // tr,), in_specs=[pl.BlockSpec((tr, t.shape[1]), lambda i: (i, 0)) for t in srcs],
        out_specs=pl.BlockSpec((N_DEV, tr, shard_w), lambda i: (0, i, 0)),
        out_shape=jax.ShapeDtypeStruct((N_DEV, r, shard_w), COMM_DTYPE), compiler_params=_params(("parallel",)),
    )(*srcs)


def _pack(arrs):
    rows = []
    for a in arrs:
        flat = a.reshape(-1).astype(F32)
        rows.append(jnp.pad(flat, (0, (-flat.shape[0]) % (8 * LANES))).reshape(-1, LANES))
    return jnp.concatenate(rows, axis=0)


def _unpack(pack, shapes):
    out, r = [], 0
    for s in shapes:
        size = math.prod(s)
        nr = -(-size // (8 * LANES)) * 8
        out.append(pack[r:r + nr].reshape(-1)[:size].reshape(s))
        r += nr
    return out


def _group_lanes(t):
    lead = t.shape[:-1]
    t = t.reshape(lead + (SSD_GROUPS, SSD_HPG))
    t = jnp.pad(t, [(0, 0)] * len(lead) + [(0, 0), (0, LANES - SSD_HPG)])
    return t.reshape(lead + (SSD_GROUPS * LANES,))


def _ungroup_lanes(t):
    lead = t.shape[:-1]
    return t.reshape(lead + (SSD_GROUPS, LANES))[..., :SSD_HPG].reshape(lead + (SSD_HEADS,))


def kernel(x, meta_tokens, mix_norm_w, w_in, ssd_conv_w, ssd_conv_b, ssd_dt_bias, ssd_A_log, ssd_D, ssd_norm_w, w_branch_ssd, w_branch_ret, w_out, ffn_norm_w, w_up, ffn_conv_w, ffn_conv_b, w_down, final_norm_w, loss_target, m_meta_tokens, m_mix_norm_w, m_w_in, m_ssd_conv_w, m_ssd_conv_b, m_ssd_dt_bias, m_ssd_A_log, m_ssd_D, m_ssd_norm_w, m_w_branch_ssd, m_w_branch_ret, m_w_out, m_ffn_norm_w, m_w_up, m_ffn_conv_w, m_ffn_conv_b, m_w_down, m_final_norm_w, v_meta_tokens, v_mix_norm_w, v_w_in, v_ssd_conv_w, v_ssd_conv_b, v_ssd_dt_bias, v_ssd_A_log, v_ssd_D, v_ssd_norm_w, v_w_branch_ssd, v_w_branch_ret, v_w_out, v_ffn_norm_w, v_w_up, v_ffn_conv_w, v_ffn_conv_b, v_w_down, v_final_norm_w):
    seq, d = x.shape[1], x.shape[2]
    rows = seq + PAD_ROWS
    tm = _row_tile(rows)
    me = _slot(_place())
    d_ff = w_down.shape[1] * N_DEV

    big = [w_in[0], w_branch_ssd[0], w_branch_ret[0], w_out[0], w_up[0], w_down[0]]
    first = _allgather([w_in[0].astype(COMM_DTYPE), meta_tokens, ssd_conv_w[0], ffn_conv_w[0]], "gather_first")
    rest_src = [b.astype(COMM_DTYPE) for b in big[1:]]
    rest_handle, rest_token = _exchange_start(rest_src, False, "gather_rest_start", after=first[0])
    cols = lambda t: jnp.transpose(t, (1, 0, 2)).reshape(t.shape[1], -1)
    rws = lambda t: t.reshape(-1, t.shape[2])
    conv_w, fconv_w = cols(first[2]), cols(first[3])
    meta_full = cols(first[1]) + rest_token[0, 0]
    widths = [SSD_D_INNER, SSD_CONV_DIM, SSD_HEADS, RET_HEADS * RET_QK, RET_HEADS * RET_QK, RET_HEADS * RET_V,
              RET_HEADS * RET_V, d, d]
    offs = [0]
    for wd in widths:
        offs.append(offs[-1] + wd)
    r0, z0 = 0, RET_HEADS * RET_HW
    g0 = z0 + widths[0]
    x0 = g0 + 2 * d
    dt0 = x0 + widths[1]
    in_p = dt0 + SSD_GROUPS * LANES
    pieces = []
    for hd in range(RET_HEADS):
        base = r0 + hd * RET_HW
        pieces += [(offs[3] + hd * RET_QK, RET_QK, base), (offs[4] + hd * RET_QK, RET_QK, base + RET_QK),
                   (offs[5] + hd * RET_V, RET_V, base + 2 * RET_QK), (offs[6] + hd * RET_V, RET_V, base + 2 * RET_QK + RET_V)]
    pieces += [(offs[0], widths[0], z0), (offs[7], d, g0), (offs[8], d, g0 + d), (offs[1], widths[1], x0)]
    pieces += [(offs[2] + SSD_HPG * grp, SSD_HPG, dt0 + LANES * grp) for grp in range(SSD_GROUPS)]
    in_plan = _column_plan(pieces, w_in.shape[2])
    w_in_p = _cols_from_shards(first[0], in_plan, in_p, [(dt0, SSD_GROUPS * LANES)], "w_in_columns")

    h0 = jnp.concatenate([jnp.zeros((FRONT, d), F32), meta_full, x[0]], axis=0)
    u1 = _rms_fwd(h0, mix_norm_w, "rms1")
    in_proj = lambda c0, width, dtype, nm: _mm(u1, w_in_p, mode="nn", out_dtype=dtype, tm=tm, tk=d, name="in_proj_" + nm,
                                               tn=_pick(width, (1024, 512)), b_n0=c0, n_out=width)
    qkvg = in_proj(r0, RET_HEADS * RET_HW, ACT_DTYPE, "qkvg")
    z = in_proj(z0, widths[0], ACT_DTYPE, "z")
    gates = in_proj(g0, 2 * d, ACT_DTYPE, "gates")
    dtr = in_proj(dt0, SSD_GROUPS * LANES, F32, "dt")
    xbc, xbc_c = _xbc_proj_conv(u1, w_in_p, x0, conv_w, ssd_conv_b)
    bias_p, alog_p, dsk_p = _group_lanes(ssd_dt_bias), _group_lanes(ssd_A_log), _group_lanes(ssd_D)
    y_ssd, y_scan, ssd_states = _ssd_fwd(xbc_c, dtr, z, bias_p, alog_p, dsk_p, ssd_norm_w)
    cos, sin, lgam = _rotary_tables(rows)
    y_ret, ret_states = _ret_fwd(qkvg, cos, sin, lgam)
    rest_own, rest = _exchange_wait(rest_handle, False, y_ret, "gather_rest_wait")
    rest = [lax.dynamic_update_index_in_dim(land, own, me, 0) for land, own in zip(rest, rest_own, strict=True)]
    w_bs, w_br, w_o, w_dn = rws(rest[0]), rws(rest[1]), rws(rest[2]), rws(rest[4])
    w_up_f = _cols_from_shards(rest[3], _column_plan([(0, 2 * d_ff, 0)], w_up.shape[2]), 2 * d_ff, [], "w_up_columns")
    bs, br, merged = _branches_merge(y_ssd, y_ret, w_bs, w_br, gates)
    h1, u2 = _out_proj_norm(merged, w_o, h0, ffn_norm_w)
    up_g, up_v, act = _ffn_up_conv(u2, w_up_f, fconv_w, ffn_conv_b)
    tgt = jnp.pad(loss_target[0], ((PAD_ROWS, 0), (0, 0)))
    dh2, loss_acc, g_final = _down_loss_head(act, w_dn, h1, tgt, final_norm_w.reshape(1, d))

    tff = _pick(d_ff, (1408, 256))
    tkr = _pick(rows, (1664, 128))
    tkr2 = _pick(rows, (4160, 128))
    rparts = lambda t: t.reshape(N_DEV, -1, t.shape[1])
    d_act = _mm(dh2, w_dn, mode="nt", out_dtype=ACT_DTYPE, tm=tm, tn=tff, tk=d, name="d_act")
    g_w_dn = _mm(act, dh2, mode="tn", out_dtype=COMM_DTYPE, tm=tff, tn=d, tk=tkr, name="g_w_down")
    c_dn = [rparts(g_w_dn)]
    h_dn, t_dn = _exchange_start(c_dn, True, "scatter_down_start")
    d_up_g, d_up_v, g_fcw_g, g_fcb_g, g_fcw_v, g_fcb_v, g_w_up_g, g_w_up_v = _ffn_conv_bwd(
        up_g, up_v, d_act, fconv_w, ffn_conv_b + t_dn[0, 0], u2)
    g_fconv_w = jnp.concatenate([g_fcw_g, g_fcw_v], axis=1)
    g_fconv_b = jnp.concatenate([g_fcb_g, g_fcb_v], axis=1)
    c_up = [_shards_from_cols([g_w_up_g, g_w_up_v], [_column_plan([(0, d_ff, 0)], w_up.shape[2]),
                                                     _column_plan([(d_ff, d_ff, 0)], w_up.shape[2])], w_up.shape[2], "g_w_up_shards")]
    h_up, t_up = _exchange_start(c_up, True, "scatter_up_start")
    dh1, g_ffn_norm = _proj_norm_bwd([d_up_g, d_up_v], w_up_f, d_ff, h1, ffn_norm_w, dh2, "ffn_up_bwd", after=t_up)
    g_w_o =_mm(merged, dh1, mode="tn", out_dtype=COMM_DTYPE, tm=d, tn=d, tk=tkr, name="g_w_out")
    dproj = lax.empty((rows, in_p), ACT_DTYPE)
    d_bs, d_br, dproj = _merge_bwd(dh1, w_o, bs, br, gates, dproj, g0)
    g_w_bs = _mm(y_ssd, d_bs, mode="tn", out_dtype=COMM_DTYPE, tm=1024, tn=d, tk=tkr2, name="g_w_branch_ssd")
    g_w_br = _mm(y_ret, d_br, mode="tn", out_dtype=COMM_DTYPE, tm=1024, tn=d, tk=tkr2, name="g_w_branch_ret")
    c_mid = [rparts(g_w_bs), rparts(g_w_br), rparts(g_w_o)]
    h_mid, t_mid = _exchange_start(c_mid, True, "scatter_mid_start")
    d_yscan, dproj, g_nw = _ssd_gate_bwd(y_scan, z, d_bs, w_bs, ssd_norm_w + t_mid[0, 0], dproj, z0)
    dxs, d_bm, d_cm, dproj, g_bias_p, g_alog_p, g_dsk_p = _ssd_bwd(
        xbc_c, dtr, bias_p, alog_p, dsk_p, ssd_states, d_yscan, dproj, dt0)
    dproj, g_conv_w, g_conv_b = _ssd_conv_bwd(xbc, dxs, d_bm, d_cm, conv_w, ssd_conv_b, dproj, x0)
    d_yret = _mm(d_br, w_br, mode="nt", out_dtype=ACT_DTYPE, tm=tm, tn=1024, tk=d, name="d_y_ret")
    dproj = _ret_bwd(qkvg, cos, sin, lgam, ret_states, d_yret, dproj)
    g_w_in_p = _mm(u1, dproj, mode="tn", out_dtype=F32, tm=d, tn=_pick(in_p, (768, 512)), tk=tkr2, name="g_w_in")
    c_in = [_shards_from_cols([g_w_in_p], [in_plan], w_in.shape[2], "g_w_in_shards")]
    h_in, t_in = _exchange_start(c_in, True, "scatter_in_start")
    dh0, g_mix_norm = _proj_norm_bwd([dproj], w_in_p, _pick(in_p, (4608, 512)), h0, mix_norm_w, dh1, "in_proj_bwd", after=t_in)
    grad_x = dh0[PAD_ROWS:][None]

    landed = {}
    for key, handle, names in (("in", h_in, ["w_in"]), ("mid", h_mid, ["w_branch_ssd", "w_branch_ret", "w_out"]),
                               ("up", h_up, ["w_up"]), ("down", h_dn, ["w_down"])):
        srcs, lands = _exchange_wait(handle, True, dh0, f"scatter_{key}_wait")
        for nm, land, src in zip(names, lands, srcs, strict=True):
            landed[nm] = (lax.dynamic_index_in_dim(src, me, 0, keepdims=False), land)
    big_m = [m_w_in, m_w_branch_ssd, m_w_branch_ret, m_w_out, m_w_up, m_w_down]
    big_v = [v_w_in, v_w_branch_ssd, v_w_branch_ret, v_w_out, v_w_up, v_w_down]
    big_names = ["w_in", "w_branch_ssd", "w_branch_ret", "w_out", "w_up", "w_down"]
    big_out = {}
    for nm, w, m, v_ in zip(big_names, big, big_m, big_v, strict=True):
        big_out[nm] = [t[None] for t in _adam_shard(*landed[nm], w, m[0], v_[0], "adam_" + nm)]

    small_g = [dh0[FRONT:PAD_ROWS], g_mix_norm, g_conv_w, g_conv_b, _ungroup_lanes(g_bias_p), _ungroup_lanes(g_alog_p),
               _ungroup_lanes(g_dsk_p), g_nw, g_ffn_norm, g_fconv_w, g_fconv_b, g_final, loss_acc[0:1, 0:1]]
    total = _unpack(_allreduce_small(_pack(small_g)), [t.shape for t in small_g])
    loss = total[12].reshape(())
    shard = lambda t, width: lax.dynamic_slice_in_dim(t, me * width, width, axis=1)
    small_names = ["meta_tokens", "mix_norm_w", "ssd_conv_w", "ssd_conv_b", "ssd_dt_bias", "ssd_A_log", "ssd_D", "ssd_norm_w",
                   "ffn_norm_w", "ffn_conv_w", "ffn_conv_b", "final_norm_w"]
    small_w = [meta_tokens, mix_norm_w, ssd_conv_w, ssd_conv_b, ssd_dt_bias, ssd_A_log, ssd_D, ssd_norm_w, ffn_norm_w,
               ffn_conv_w, ffn_conv_b, final_norm_w]
    small_m = [m_meta_tokens, m_mix_norm_w, m_ssd_conv_w, m_ssd_conv_b, m_ssd_dt_bias, m_ssd_A_log, m_ssd_D, m_ssd_norm_w,
               m_ffn_norm_w, m_ffn_conv_w, m_ffn_conv_b, m_final_norm_w]
    small_v = [v_meta_tokens, v_mix_norm_w, v_ssd_conv_w, v_ssd_conv_b, v_ssd_dt_bias, v_ssd_A_log, v_ssd_D, v_ssd_norm_w,
               v_ffn_norm_w, v_ffn_conv_w, v_ffn_conv_b, v_final_norm_w]
    grads = total[:12]
    grads[0] = shard(grads[0], meta_tokens.shape[1])
    grads[2] = shard(grads[2], ssd_conv_w.shape[2])
    grads[9] = shard(grads[9], ffn_conv_w.shape[2])
    grads = [t.reshape(w.shape) for t, w in zip(grads, small_w, strict=True)]
    shapes = [w.shape for w in small_w]
    upd = _adam_small(_pack(small_w), _pack(grads), _pack(small_m), _pack(small_v))
    small_out = {nm: [gr_] + [u[i] for u in (_unpack(t, shapes) for t in upd)]
                 for i, (nm, gr_) in enumerate(zip(small_names, grads, strict=True))}

    order = ["meta_tokens", "mix_norm_w", "w_in", "ssd_conv_w", "ssd_conv_b", "ssd_dt_bias", "ssd_A_log", "ssd_D", "ssd_norm_w",
             "w_branch_ssd", "w_branch_ret", "w_out", "ffn_norm_w", "w_up", "ffn_conv_w", "ffn_conv_b", "w_down", "final_norm_w"]
    res = {**big_out, **small_out}
    return (loss, grad_x, *[res[nm][0] for nm in order], *[res[nm][1] for nm in order], *[res[nm][2] for nm in order],
            *[res[nm][3] for nm in order])
```

```python
import functools
import math

import jax
import jax.numpy as jnp
import numpy as np
from jax import lax
from jax.experimental import pallas as pl
from jax.experimental.pallas import tpu as pltpu

F32 = jnp.float32
MXU_DTYPE = jnp.bfloat16
ACT_DTYPE = jnp.bfloat16
COMM_DTYPE = jnp.bfloat16

N_META = 16
CHUNK = 128
FRONT = CHUNK - N_META
PAD_ROWS = FRONT + N_META
EPS = 1e-6
N_DEV = 8

SSD_D_INNER = 2048
SSD_HEAD_DIM = 64
SSD_HEADS = 32
SSD_GROUPS = 4
SSD_HPG = SSD_HEADS // SSD_GROUPS
SSD_STATE = 128
SSD_CONV = 4
SSD_CONV_DIM = SSD_D_INNER + 2 * SSD_GROUPS * SSD_STATE
SSD_GW = SSD_D_INNER // SSD_GROUPS
RET_HEADS = 4
RET_QK = 256
RET_V = 512
RET_HW = 2 * RET_QK + 2 * RET_V
ROPE_BASE = 10000.0
FFN_CONV = 3
HALO = 16
LANES = 128

ADAM_LR = 0.001
ADAM_B1 = 0.9
ADAM_B2 = 0.999
ADAM_EPS = 1e-08
ADAM_WD = 0.01
ADAM_STEP = 10

VMEM_LIMIT = 56 * 1024 * 1024
MESH = pl.DeviceIdType.MESH

NN = (((1,), (0,)), ((), ()))
NT = (((1,), (1,)), ((), ()))
TN = (((0,), (0,)), ((), ()))


def _params(sem):
    return pltpu.CompilerParams(dimension_semantics=sem, vmem_limit_bytes=VMEM_LIMIT)


def _mxu(a, b, dn):
    return lax.dot_general(a.astype(MXU_DTYPE), b.astype(MXU_DTYPE), dn, preferred_element_type=F32)


@functools.partial(jax.custom_vjp, nondiff_argnums=(2,))
def _dot(a, b, dn=NN):
    return _mxu(a, b, dn)


def _dot_fwd(a, b, dn):
    return _mxu(a, b, dn), (a, b)


def _dot_bwd(dn, res, g):
    a, b = res
    if dn == NN:
        return _mxu(g, b, NT), _mxu(a, g, TN)
    if dn == NT:
        return _mxu(g, b, NN), _mxu(g, a, TN)
    assert dn == TN
    return _mxu(b, g, NT), _mxu(a, g, NN)


_dot.defvjp(_dot_fwd, _dot_bwd)


def _silu(x):
    return x * jax.nn.sigmoid(x)


def _dsilu(x):
    s = jax.nn.sigmoid(x)
    return s * (1.0 + x * (1.0 - s))


def _row_tile(rows):
    return 640 if rows % 640 == 0 else 128


def _mm(a, b, *, mode, out_dtype, tm, tn, tk, name, b_n0=0, n_out=None):
    if mode == "nt":
        (m, k), (n, k2) = a.shape, b.shape
        assert b_n0 == 0 and n_out is None
    else:
        (m, k) = a.shape if mode == "nn" else a.shape[::-1]
        k2 = b.shape[0]
        n = b.shape[1] if n_out is None else n_out
        assert b_n0 % tn == 0 and b_n0 + n <= b.shape[1]
    assert k == k2 and m % tm == 0 and n % tn == 0 and k % tk == 0, (name, a.shape, b.shape, tm, tn, tk)
    nb0 = b_n0 // tn
    nk = k // tk
    dn = {"nn": NN, "nt": NT, "tn": TN}[mode]

    def body(a_ref, b_ref, o_ref, *acc):
        p = _mxu(a_ref[...], b_ref[...], dn)
        if nk == 1:
            o_ref[...] = p.astype(out_dtype)
        else:
            acc_ref, = acc
            kk = pl.program_id(2)

            @pl.when(kk == 0)
            def _():
                acc_ref[...] = p

            @pl.when(kk > 0)
            def _():
                acc_ref[...] += p

            @pl.when(kk == nk - 1)
            def _():
                o_ref[...] = acc_ref[...].astype(out_dtype)

    if mode == "tn":
        a_spec = pl.BlockSpec((tk, tm), lambda j, i, kk: (kk, i))
    else:
        a_spec = pl.BlockSpec((tm, tk), lambda j, i, kk: (i, kk))
    if mode == "nt":
        b_spec = pl.BlockSpec((tn, tk), lambda j, i, kk: (j, kk))
    else:
        b_spec = pl.BlockSpec((tk, tn), lambda j, i, kk: (kk, j + nb0))
    return pl.pallas_call(
        body, name=name, grid=(n // tn, m // tm, nk), in_specs=[a_spec, b_spec],
        out_specs=pl.BlockSpec((tm, tn), lambda j, i, kk: (i, j)), out_shape=jax.ShapeDtypeStruct((m, n), out_dtype),
        scratch_shapes=[pltpu.VMEM((tm, tn), F32)] if nk > 1 else [],
        compiler_params=_params(("parallel", "parallel", "arbitrary")),
    )(a, b)


def _pick(n, cands):
    for c in cands:
        if n % c == 0:
            return c
    return n


def _rms_fwd(h, w, name):
    rows, d = h.shape
    tm = _row_tile(rows)

    def body(h_ref, w_ref, u_ref):
        x = h_ref[...]
        r = lax.rsqrt(jnp.mean(x * x, axis=-1, keepdims=True) + EPS)
        u_ref[...] = (x * r * w_ref[...]).astype(ACT_DTYPE)

    return pl.pallas_call(
        body, name=name, grid=(rows // tm,),
        in_specs=[pl.BlockSpec((tm, d), lambda i: (i, 0)), pl.BlockSpec((1, d), lambda i: (0, 0))],
        out_specs=pl.BlockSpec((tm, d), lambda i: (i, 0)),
        out_shape=jax.ShapeDtypeStruct((rows, d), ACT_DTYPE),
        compiler_params=_params(("parallel",)),
    )(h, w)


def _proj_norm_bwd(parts, w, tk, h, nw, dres, name, after=None, mxu_copy=False):
    rows, d = h.shape
    tm = _row_tile(rows)
    steps = [p.shape[1] // tk for p in parts]
    starts = [sum(steps[:p]) for p in range(len(parts))]
    nk = sum(steps)
    assert all(p.shape[1] % tk == 0 for p in parts) and nk * tk == w.shape[1]
    n = len(parts)

    def body(*refs):
        a_refs, w_ref, h_ref, nw_ref, dres_ref = refs[:n], refs[n], refs[n + 1], refs[n + 2], refs[n + 3]
        dh_ref, dw_ref, acc = refs[n_in], refs[n_in + 1], refs[-1]
        i, kk = pl.program_id(0), pl.program_id(1)
        for p in range(n):
            @pl.when((kk >= starts[p]) & (kk < starts[p] + steps[p]))
            def _(p=p):
                part = _mxu(a_refs[p][...], w_ref[...], NT)
                if starts[p] == 0:
                    @pl.when(kk == 0)
                    def _():
                        acc[...] = part

                    @pl.when(kk > 0)
                    def _():
                        acc[...] += part
                else:
                    acc[...] += part

        @pl.when(kk == nk - 1)
        def _():
            x, dy = h_ref[...], acc[...]
            r = lax.rsqrt(jnp.mean(x * x, axis=-1, keepdims=True) + EPS)
            xhat = x * r
            dxn = dy * nw_ref[...]
            dh = dres_ref[...] + r * (dxn - xhat * jnp.mean(dxn * xhat, axis=-1, keepdims=True))
            dh_ref[...] = dh
            if mxu_copy:
                refs[n_in + 2][...] = dh.astype(MXU_DTYPE)

            @pl.when(i == 0)
            def _():
                dw_ref[...] = jnp.zeros_like(dw_ref)

            dw_ref[...] += jnp.sum(dy * xhat, axis=0, keepdims=True)

    n_in = n + 4 + (after is not None)
    clip = lambda kk, p: jnp.clip(kk - starts[p], 0, steps[p] - 1)
    row = pl.BlockSpec((tm, d), lambda i, kk: (i, 0))
    in_specs = [pl.BlockSpec((tm, tk), lambda i, kk, p=p: (i, clip(kk, p))) for p in range(n)]
    in_specs += [pl.BlockSpec((d, tk), lambda i, kk: (0, kk)), row, pl.BlockSpec((1, d), lambda i, kk: (0, 0)), row]
    args = (*parts, w, h, nw, dres)
    if after is not None:
        in_specs.append(pl.BlockSpec(memory_space=pl.ANY))
        args += (after,)
    return pl.pallas_call(
        body, name=name, grid=(rows // tm, nk), in_specs=in_specs,
        out_specs=[row, pl.BlockSpec((1, d), lambda i, kk: (0, 0))] + [row] * mxu_copy,
        out_shape=[jax.ShapeDtypeStruct((rows, d), F32), jax.ShapeDtypeStruct((1, d), F32)]
        + [jax.ShapeDtypeStruct((rows, d), MXU_DTYPE)] * mxu_copy,
        scratch_shapes=[pltpu.VMEM((tm, d), F32)], compiler_params=_params(("arbitrary", "arbitrary")),
    )(*args)


def _out_proj_norm(merged, w_o, h0, nw):
    rows, d = h0.shape
    tm = _row_tile(rows)

    def body(m_ref, w_ref, h_ref, nw_ref, h1_ref, u_ref):
        x = h_ref[...] + _mxu(m_ref[...], w_ref[...], NN)
        h1_ref[...] = x
        r = lax.rsqrt(jnp.mean(x * x, axis=-1, keepdims=True) + EPS)
        u_ref[...] = (x * r * nw_ref[...]).astype(ACT_DTYPE)

    spec = pl.BlockSpec((tm, d), lambda i: (i, 0))
    return pl.pallas_call(
        body, name="out_proj_norm", grid=(rows // tm,),
        in_specs=[pl.BlockSpec((tm, merged.shape[1]), lambda i: (i, 0)), pl.BlockSpec(w_o.shape, lambda i: (0, 0)), spec,
                  pl.BlockSpec((1, d), lambda i: (0, 0))],
        out_specs=[spec, spec], out_shape=[jax.ShapeDtypeStruct((rows, d), F32), jax.ShapeDtypeStruct((rows, d), ACT_DTYPE)],
        compiler_params=_params(("parallel",)),
    )(merged, w_o, h0, nw)


def _down_loss_head(act, w_dn, h1, tgt, w):
    rows, d = h1.shape
    tm = _row_tile(rows)

    def body(a_ref, wd_ref, h_ref, t_ref, w_ref, dh_ref, loss_ref, dw_ref, dhb_ref):
        i = pl.program_id(0)
        x = h_ref[...] + _mxu(a_ref[...], wd_ref[...], NN)
        r = lax.rsqrt(jnp.mean(x * x, axis=-1, keepdims=True) + EPS)
        xhat = x * r
        wv = w_ref[...]
        row = i * tm + lax.broadcasted_iota(jnp.int32, (tm, 1), 0)
        live = row >= PAD_ROWS
        diff = jnp.where(live, xhat * wv - t_ref[...], 0.0)
        dy = diff * (1.0 / d)
        dxn = dy * wv
        dh = r * (dxn - xhat * jnp.mean(dxn * xhat, axis=-1, keepdims=True))
        dh_ref[...] = dh
        dhb_ref[...] = dh.astype(MXU_DTYPE)

        @pl.when(i == 0)
        def _():
            loss_ref[...] = jnp.zeros_like(loss_ref)
            dw_ref[...] = jnp.zeros_like(dw_ref)

        loss_ref[...] += 0.5 * jnp.sum(jnp.mean(diff * diff, axis=-1, keepdims=True))
        dw_ref[...] += jnp.sum(dy * xhat, axis=0, keepdims=True)

    return pl.pallas_call(
        body, name="down_loss_head", grid=(rows // tm,),
        in_specs=[pl.BlockSpec((tm, act.shape[1]), lambda i: (i, 0)), pl.BlockSpec(w_dn.shape, lambda i: (0, 0)),
                  pl.BlockSpec((tm, d), lambda i: (i, 0)), pl.BlockSpec((tm, d), lambda i: (i, 0)),
                  pl.BlockSpec((1, d), lambda i: (0, 0))],
        out_specs=[pl.BlockSpec((tm, d), lambda i: (i, 0)), pl.BlockSpec((8, LANES), lambda i: (0, 0)),
                   pl.BlockSpec((1, d), lambda i: (0, 0)), pl.BlockSpec((tm, d), lambda i: (i, 0))],
        out_shape=[jax.ShapeDtypeStruct((rows, d), F32), jax.ShapeDtypeStruct((8, LANES), F32),
                   jax.ShapeDtypeStruct((1, d), F32), jax.ShapeDtypeStruct((rows, d), MXU_DTYPE)],
        compiler_params=_params(("arbitrary",)),
    )(act, w_dn, h1, tgt, w)


def _prev_halo_spec(tm, width, col):
    return pl.BlockSpec((HALO, width), lambda j, i: (jnp.maximum(i * (tm // HALO) - 1, 0), col(j)))


def _next_halo_spec(tm, rows, width, col):
    last = rows // HALO - 1
    return pl.BlockSpec((HALO, width), lambda j, i: (jnp.minimum((i + 1) * (tm // HALO), last), col(j)))


def _conv_taps(cat, w_ref, b_ref, kw):
    acc = b_ref[...] + w_ref[kw - 1:kw, :] * cat
    for s in range(1, kw):
        acc = acc + w_ref[kw - 1 - s:kw - s, :] * pltpu.roll(cat, s, 0)
    return acc


def _conv_back(dpre, w_ref, kw):
    n = dpre.shape[0]
    acc = w_ref[kw - 1:kw, :] * dpre
    for s in range(1, kw):
        acc = acc + w_ref[kw - 1 - s:kw - s, :] * pltpu.roll(dpre, n - s, 0)
    return acc


def _xbc_proj_conv(u1, w_in_p, col0, w, b):
    rows, d = u1.shape
    width = w.shape[1]
    tm, tc = _row_tile(rows), 512
    cb0 = col0 // tc
    assert col0 % tc == 0

    def body(u_ref, m_ref, w_ref, b_ref, x_ref, o_ref, carry):
        i = pl.program_id(1)

        @pl.when(i == 0)
        def _():
            carry[...] = jnp.zeros_like(carry)

        xb = _mxu(u_ref[...], m_ref[...], NN).astype(ACT_DTYPE)
        x_ref[...] = xb
        x = xb.astype(F32)
        cat = jnp.concatenate([carry[...], x], axis=0)
        carry[...] = x[tm - HALO:, :]
        pre = _conv_taps(cat, w_ref, b_ref, SSD_CONV)[HALO:]
        row = i * tm + lax.broadcasted_iota(jnp.int32, (tm, 1), 0)
        o_ref[...] = jnp.where(row >= FRONT, _silu(pre), 0.0).astype(ACT_DTYPE)

    main = pl.BlockSpec((tm, tc), lambda j, i: (i, j))
    par = lambda r: pl.BlockSpec((r, tc), lambda j, i: (0, j))
    act = jax.ShapeDtypeStruct((rows, width), ACT_DTYPE)
    return pl.pallas_call(
        body, name="xbc_proj_conv", grid=(width // tc, rows // tm),
        in_specs=[pl.BlockSpec((tm, d), lambda j, i: (i, 0)), pl.BlockSpec((d, tc), lambda j, i: (0, cb0 + j)),
                  par(SSD_CONV), par(1)],
        out_specs=[main, main], out_shape=[act, act], scratch_shapes=[pltpu.VMEM((HALO, tc), F32)],
        compiler_params=_params(("parallel", "arbitrary")),
    )(u1, w_in_p, w, b)


def _ssd_conv_bwd(xbc, dxs, dbm, dcm, w, b, dproj, col0):
    rows, width = xbc.shape
    tm, tc = _row_tile(rows), 512
    kw = SSD_CONV
    nx = dxs.shape[1] // tc
    assert dbm.shape[1] == tc and dcm.shape[1] == tc and width == (nx + 2) * tc and col0 % tc == 0

    def body(x_ref, xp_ref, xn_ref, d0_ref, d0n_ref, d1_ref, d1n_ref, d2_ref, d2n_ref, w_ref, b_ref, dp_ref,
             dx_ref, dw_ref, db_ref):
        j, i = pl.program_id(0), pl.program_id(1)
        xp = jnp.where(i == 0, 0.0, xp_ref[...].astype(F32))
        cat = jnp.concatenate([xp, x_ref[...].astype(F32), xn_ref[...].astype(F32)], axis=0)
        sh = [cat] + [pltpu.roll(cat, s, 0) for s in range(1, kw)]
        pre = b_ref[...] + w_ref[kw - 1:kw, :] * sh[0]
        for s in range(1, kw):
            pre = pre + w_ref[kw - 1 - s:kw - s, :] * sh[s]
        pre = pre[HALO:]
        row = i * tm + lax.broadcasted_iota(jnp.int32, (tm + HALO, 1), 0)
        live = (row >= FRONT) & (row < rows)
        pick = lambda a, bb, c: jnp.where(j < nx, a[...], jnp.where(j == nx, bb[...], c[...])).astype(F32)
        dout = jnp.concatenate([pick(d0_ref, d1_ref, d2_ref), pick(d0n_ref, d1n_ref, d2n_ref)], axis=0)
        dpre = jnp.where(live, dout * _dsilu(pre), 0.0)
        dx_ref[...] = _conv_back(dpre, w_ref, kw)[:tm].astype(ACT_DTYPE)

        @pl.when(i == 0)
        def _():
            dw_ref[...] = jnp.zeros_like(dw_ref)
            db_ref[...] = jnp.zeros_like(db_ref)

        dmain = dpre[:tm]
        db_ref[...] += jnp.sum(dmain, axis=0, keepdims=True)
        for k in range(kw):
            dw_ref[k:k + 1, :] += jnp.sum(dmain * sh[kw - 1 - k][HALO:HALO + tm], axis=0, keepdims=True)

    main = pl.BlockSpec((tm, tc), lambda j, i: (i, j))
    par = lambda r: pl.BlockSpec((r, tc), lambda j, i: (0, j))
    col = lambda j: j
    xcol, zero = (lambda j: jnp.minimum(j, nx - 1)), (lambda j: 0)
    dspecs = lambda c: [pl.BlockSpec((tm, tc), lambda j, i: (i, c(j))), _next_halo_spec(tm, rows, tc, c)]
    return pl.pallas_call(
        body, name="ssd_conv_bwd", grid=(width // tc, rows // tm),
        in_specs=[main, _prev_halo_spec(tm, tc, col), _next_halo_spec(tm, rows, tc, col)]
        + dspecs(xcol) + dspecs(zero) + dspecs(zero) + [par(kw), par(1), pl.BlockSpec(memory_space=pl.ANY)],
        out_specs=[pl.BlockSpec((tm, tc), lambda j, i: (i, col0 // tc + j)), par(kw), par(1)],
        out_shape=[jax.ShapeDtypeStruct(dproj.shape, dproj.dtype), jax.ShapeDtypeStruct((kw, width), F32),
                   jax.ShapeDtypeStruct((1, width), F32)],
        input_output_aliases={11: 0}, compiler_params=_params(("parallel", "arbitrary")),
    )(xbc, xbc, xbc, dxs, dxs, dbm, dbm, dcm, dcm, w, b, dproj)


def _ffn_up_conv(u2, w_up, w, b):
    rows, d = u2.shape
    width = w_up.shape[1]
    dff = width // 2
    tm, tc = _row_tile(rows), _pick(dff, (256, 128))
    nb = dff // tc
    kw = FFN_CONV

    def body(u_ref, mg_ref, mv_ref, wg_ref, bg_ref, wv_ref, bv_ref, ug_ref, uv_ref, o_ref, cg, cv):
        i = pl.program_id(1)

        @pl.when(i == 0)
        def _():
            cg[...] = jnp.zeros_like(cg)
            cv[...] = jnp.zeros_like(cv)

        def pre(m_ref, up_ref, carry, w_ref, b_ref):
            upb = _mxu(u_ref[...], m_ref[...], NN).astype(ACT_DTYPE)
            up_ref[...] = upb
            x = upb.astype(F32)
            cat = jnp.concatenate([carry[...], x], axis=0)
            carry[...] = x[tm - HALO:, :]
            return _conv_taps(cat, w_ref, b_ref, kw)[HALO:]

        ag = pre(mg_ref, ug_ref, cg, wg_ref, bg_ref)
        av = pre(mv_ref, uv_ref, cv, wv_ref, bv_ref)
        o_ref[...] = (_silu(ag) * av).astype(ACT_DTYPE)

    gcol, vcol = (lambda j: j), (lambda j: j + nb)
    mat = lambda col: pl.BlockSpec((d, tc), lambda j, i: (0, col(j)))
    par = lambda r, col: pl.BlockSpec((r, tc), lambda j, i: (0, col(j)))
    out = pl.BlockSpec((tm, tc), lambda j, i: (i, j))
    act = jax.ShapeDtypeStruct((rows, dff), ACT_DTYPE)
    return pl.pallas_call(
        body, name="ffn_up_conv", grid=(nb, rows // tm),
        in_specs=[pl.BlockSpec((tm, d), lambda j, i: (i, 0)), mat(gcol), mat(vcol),
                  par(kw, gcol), par(1, gcol), par(kw, vcol), par(1, vcol)],
        out_specs=[out, out, out], out_shape=[act, act, act],
        scratch_shapes=[pltpu.VMEM((HALO, tc), F32), pltpu.VMEM((HALO, tc), F32)],
        compiler_params=_params(("parallel", "arbitrary")),
    )(u2, w_up, w_up, w, b, w, b)


def _ffn_conv_bwd(up_g, up_v, dact, w, b, u2):
    rows, dff = up_g.shape
    d = u2.shape[1]
    tm, tc = _row_tile(rows), _pick(dff, (256, 128))
    nb = dff // tc
    kw = FFN_CONV

    sb = 16

    def body(g_ref, gp_ref, gn_ref, v_ref, vp_ref, vn_ref, d_ref, dn_ref, wg_ref, bg_ref, wv_ref, bv_ref, u_ref,
             dxg_ref, dxv_ref, dwg_ref, dbg_ref, dwv_ref, dbv_ref, gwg_ref, gwv_ref, xg_s, xv_s, dd_s, og_s, ov_s):
        i = pl.program_id(1)
        last = i == rows // tm - 1
        for x_s, x_ref, xp_ref, xn_ref in ((xg_s, g_ref, gp_ref, gn_ref), (xv_s, v_ref, vp_ref, vn_ref)):
            x_s[0:HALO, :] = jnp.where(i == 0, 0.0, xp_ref[...].astype(F32))
            x_s[HALO:HALO + tm, :] = x_ref[...].astype(F32)
            x_s[HALO + tm:, :] = xn_ref[...].astype(F32)
        dd_s[0:tm, :] = d_ref[...].astype(F32)
        dd_s[tm:, :] = jnp.where(last, 0.0, dn_ref[...].astype(F32))

        wg = [wg_ref[k:k + 1, :] for k in range(kw)]
        wv = [wv_ref[k:k + 1, :] for k in range(kw)]
        bg, bv = bg_ref[...], bv_ref[...]

        def taps(x_s, e0, w, bias):
            win = x_s[pl.ds(e0 + HALO - sb, 2 * sb), :]
            sh = [win[sb:], pltpu.roll(win, 1, 0)[sb:], pltpu.roll(win, 2, 0)[sb:]]
            return bias + w[2] * sh[0] + w[1] * sh[1] + w[0] * sh[2], sh

        def dpre_of(e0):
            ag, sh_g = taps(xg_s, e0, wg, bg)
            av, sh_v = taps(xv_s, e0, wv, bv)
            dout = dd_s[pl.ds(e0, sb), :]
            s = jax.nn.sigmoid(ag)
            silu = ag * s
            return dout * av * (s + silu * (1.0 - s)), dout * silu, sh_g, sh_v

        def back(dp, nxt, w):
            cat = jnp.concatenate([dp, nxt], axis=0)
            return w[2] * dp + w[1] * pltpu.roll(cat, 2 * sb - 1, 0)[:sb] + w[0] * pltpu.roll(cat, 2 * sb - 2, 0)[:sb]

        nxt_g, nxt_v, _, _ = dpre_of(tm)
        acc_g = acc_v = tuple(jnp.zeros((sb, tc), F32) for _ in range(kw + 1))
        for e0 in range(tm - sb, -1, -sb):
            dpg, dpv, sh_g, sh_v = dpre_of(e0)
            og_s[e0:e0 + sb, :] = back(dpg, nxt_g, wg)
            ov_s[e0:e0 + sb, :] = back(dpv, nxt_v, wv)
            acc_g = tuple(a + dpg * t for a, t in zip(acc_g, (sh_g[2], sh_g[1], sh_g[0], 1.0)))
            acc_v = tuple(a + dpv * t for a, t in zip(acc_v, (sh_v[2], sh_v[1], sh_v[0], 1.0)))
            nxt_g, nxt_v = dpg, dpv

        @pl.when(i == 0)
        def _():
            for r in (dwg_ref, dbg_ref, dwv_ref, dbv_ref, gwg_ref, gwv_ref):
                r[...] = jnp.zeros_like(r)

        for acc, o_s, dx_ref, dw_ref, db_ref, gw_ref in ((acc_g, og_s, dxg_ref, dwg_ref, dbg_ref, gwg_ref),
                                                        (acc_v, ov_s, dxv_ref, dwv_ref, dbv_ref, gwv_ref)):
            dx = o_s[...].astype(ACT_DTYPE)
            dx_ref[...] = dx
            gw_ref[...] += _mxu(u_ref[...], dx, TN)
            for k in range(kw):
                dw_ref[k:k + 1, :] += jnp.sum(acc[k], axis=0, keepdims=True)
            db_ref[...] += jnp.sum(acc[kw], axis=0, keepdims=True)

    gcol, vcol = (lambda j: j), (lambda j: j + nb)
    main = lambda col: pl.BlockSpec((tm, tc), lambda j, i: (i, col(j)))
    par = lambda r, col: pl.BlockSpec((r, tc), lambda j, i: (0, col(j)))
    halos = lambda col: [_prev_halo_spec(tm, tc, col), _next_halo_spec(tm, rows, tc, col)]
    act_shape = jax.ShapeDtypeStruct((rows, dff), ACT_DTYPE)
    par_shapes = [jax.ShapeDtypeStruct((kw, dff), F32), jax.ShapeDtypeStruct((1, dff), F32)]
    gw_shape = jax.ShapeDtypeStruct((d, dff), F32)
    return pl.pallas_call(
        body, name="ffn_conv_bwd", grid=(nb, rows // tm),
        in_specs=[main(gcol)] + halos(gcol) + [main(gcol)] + halos(gcol) + [main(gcol), _next_halo_spec(tm, rows, tc, gcol),
                  par(kw, gcol), par(1, gcol), par(kw, vcol), par(1, vcol), pl.BlockSpec((tm, d), lambda j, i: (i, 0))],
        out_specs=[main(gcol), main(gcol), par(kw, gcol), par(1, gcol), par(kw, gcol), par(1, gcol), par(d, gcol), par(d, gcol)],
        out_shape=[act_shape, act_shape] + par_shapes + par_shapes + [gw_shape, gw_shape],
        scratch_shapes=[pltpu.VMEM((tm + 2 * HALO, tc), F32), pltpu.VMEM((tm + 2 * HALO, tc), F32),
                        pltpu.VMEM((tm + HALO, tc), F32), pltpu.VMEM((tm, tc), F32), pltpu.VMEM((tm, tc), F32)],
        compiler_params=_params(("parallel", "arbitrary")),
    )(up_g, up_g, up_g, up_v, up_v, up_v, dact, dact, w, b, w, b, u2)


def _ssd_scalars(dtr, dt_bias, a_log, live):
    q = CHUNK
    pre = dtr + dt_bias
    dt = jnp.where(live, jax.nn.softplus(pre), 0.0)
    a_neg = -jnp.exp(a_log)
    li = lax.broadcasted_iota(jnp.int32, (q, q), 0)
    si = lax.broadcasted_iota(jnp.int32, (q, q), 1)
    causal = li >= si
    tri = jnp.where(causal, 1.0, 0.0).astype(F32)
    a_cs = sum(_mxu(tri, p, NN) for p in _split(dt * a_neg, 3))
    return pre, dt, a_neg, a_cs, causal, tri


def _head_select():
    r = lax.broadcasted_iota(jnp.int32, (LANES, SSD_GW), 0)
    c = lax.broadcasted_iota(jnp.int32, (LANES, SSD_GW), 1)
    return jnp.where(c // SSD_HEAD_DIM == r, 1.0, 0.0).astype(MXU_DTYPE)


def _split(t, parts):
    out, rem = [], t
    for _ in range(parts):
        p = rem.astype(MXU_DTYPE)
        out.append(p)
        rem = rem - p.astype(F32)
    return out


def _stacked(ts, parts, sel, dn):
    out = _mxu(jnp.concatenate([p for t, n in zip(ts, parts, strict=True) for p in _split(t, n)], axis=0), sel, dn)
    res, r0 = [], 0
    for t, n in zip(ts, parts, strict=True):
        r = t.shape[0]
        res.append(sum(out[r0 + k * r:r0 + (k + 1) * r] for k in range(n)))
        r0 += n * r
    return res


def _head_cols(ts, sel):
    return _stacked(ts, [2] * len(ts), sel, NN)


def _head_sums(ts, parts, sel):
    return _stacked(ts, parts, sel, NT)


def _half_masks():
    lane = lax.broadcasted_iota(jnp.int32, (CHUNK, LANES), 1)
    return lane < SSD_HEAD_DIM, lane >= SSD_HEAD_DIM


def _ssd_scan(xs, bm, cm, dtr, prev, dt_bias, a_log, d_skip, live):
    q = CHUNK
    sel = _head_select()
    _, dt, _, a_cs, causal, _ = _ssd_scalars(dtr, dt_bias, a_log, live)
    a_cs_t = a_cs.T
    a_end = a_cs[q - 1:q, :]
    dt_x, e_x, w_x, d_x = _head_cols([dt, jnp.exp(a_cs), jnp.exp(a_end - a_cs), jnp.broadcast_to(d_skip, (16, LANES))], sel)
    xdt = xs * dt_x
    cb = _dot(cm, bm, NT)
    y = _dot(cm, prev) * e_x + d_x[0:1] * xs
    new = prev * e_x[q - 1:q, :] + _dot(bm, xdt * w_x, TN)
    masks = _half_masks()
    ys = []
    for pp in range(SSD_HPG // 2):
        xpair = xdt[:, pp * LANES:(pp + 1) * LANES]
        acc = jnp.zeros((q, LANES), F32)
        for half in range(2):
            hh = 2 * pp + half
            decay = jnp.exp(jnp.where(causal, a_cs[:, hh:hh + 1] - a_cs_t[hh:hh + 1, :], -jnp.inf))
            acc = acc + _dot(cb * decay, jnp.where(masks[half], xpair, 0.0))
        ys.append(acc)
    return y + jnp.concatenate(ys, axis=1), new


def _ssd_gate(y, z, nw):
    yz = y * _silu(z)
    return yz * lax.rsqrt(jnp.mean(yz * yz, axis=-1, keepdims=True) + EPS) * nw


def _ssd_scan_bwd(xs, bm, cm, dtr, prev, dt_bias, a_log, d_skip, live, dy, dnew):
    q = CHUNK
    sel = _head_select()
    pre, dt, a_neg, a_cs, causal, tri = _ssd_scalars(dtr, dt_bias, a_log, live)
    a_cs_t = a_cs.T
    a_end = a_cs[q - 1:q, :]
    dt_x, e_x, w_x, d_x = _head_cols([dt, jnp.exp(a_cs), jnp.exp(a_end - a_cs), jnp.broadcast_to(d_skip, (16, LANES))], sel)
    g_x, d_x = e_x[q - 1:q, :], d_x[0:1]
    xdt = xs * dt_x
    u = xdt * w_x
    cb = _mxu(cm, bm, NT)
    cs = _mxu(cm, prev, NN)
    dye = dy * e_x
    dcm = _mxu(dye, prev, NT)
    dprev = _mxu(cm, dye, TN) + dnew * g_x
    dacs_x = dye * cs
    dbm = _mxu(u, dnew, NT)
    du = _mxu(bm, dnew, NN)
    dw_x = du * u
    dacs_x = dacs_x - dw_x
    dend_x = jnp.sum(dw_x + dnew * prev * g_x, axis=0, keepdims=True)
    dxdt = du * w_x
    lane = lax.broadcasted_iota(jnp.int32, (q, LANES), 1)
    sub = lax.broadcasted_iota(jnp.int32, (q, LANES), 0)
    dcb = jnp.zeros((q, q), F32)
    dacs = jnp.zeros((q, LANES), F32)
    dacs_t = jnp.zeros((q, LANES), F32)
    masks = _half_masks()
    dxdt_p = []
    for pp in range(SSD_HPG // 2):
        ps = slice(pp * LANES, (pp + 1) * LANES)
        acc = jnp.zeros((q, LANES), F32)
        for half in range(2):
            hh = 2 * pp + half
            decay = jnp.exp(jnp.where(causal, a_cs[:, hh:hh + 1] - a_cs_t[hh:hh + 1, :], -jnp.inf))
            m = cb * decay
            dyh = jnp.where(masks[half], dy[:, ps], 0.0)
            dm = _mxu(dyh, xdt[:, ps], NT)
            acc = acc + _mxu(m, dyh, TN)
            dcb = dcb + dm * decay
            p = dm * m
            dacs = jnp.where(lane == hh, jnp.sum(p, axis=1, keepdims=True), dacs)
            dacs_t = jnp.where(sub == hh, jnp.sum(p, axis=0, keepdims=True), dacs_t)
        dxdt_p.append(acc)
    dcm = dcm + _mxu(dcb, bm, NN)
    dbm = dbm + _mxu(dcb, cm, TN)
    dxdt = dxdt + jnp.concatenate(dxdt_p, axis=1)
    dxs = dy * d_x + dxdt * dt_x
    rows_x = jnp.concatenate([dend_x, jnp.sum(dy * xs, axis=0, keepdims=True), jnp.zeros((14, SSD_GW), F32)], axis=0)
    dacs_h, ddt_h, rows = _head_sums([dacs_x, dxdt * xs, rows_x], [3, 2, 3], sel)
    dacs = dacs - dacs_t.T + dacs_h
    dacs = dacs + jnp.where(sub == q - 1, rows[0:1], 0.0)
    tri_t = jnp.where(causal, 0.0, 1.0).astype(F32) + jnp.where(lane == sub, 1.0, 0.0)
    da = jnp.dot(tri_t, dacs, precision=lax.Precision.HIGHEST, preferred_element_type=F32)
    ddt = ddt_h + da * a_neg
    dalog = jnp.sum(da * dt, axis=0, keepdims=True) * a_neg
    ddtr = jnp.where(live, ddt * jax.nn.sigmoid(pre), 0.0)
    dbias = jnp.sum(ddtr, axis=0, keepdims=True)
    return dxs, dbm, dcm, ddtr, dprev, dbias, dalog, rows[1:2]


def _chunks_per_step(nc):
    return 13 if nc % 13 == 0 else 1


def _ssd_specs(rev, nc):
    per = _chunks_per_step(nc)
    steps = nc // per
    sidx = (lambda s: steps - 1 - s) if rev else (lambda s: s)
    nb_b = SSD_D_INNER // SSD_STATE
    row = lambda width, col=lambda g: g: pl.BlockSpec((per * CHUNK, width), lambda g, s: (sidx(s), col(g)))
    par = lambda width: pl.BlockSpec((1, width), lambda g, s: (0, g))
    state = lambda: pl.BlockSpec((per, 1, SSD_STATE, SSD_GW), lambda g, s: (sidx(s), g, 0, 0))
    xbc = [row(SSD_GW), row(SSD_STATE, lambda g: nb_b + g), row(SSD_STATE, lambda g: nb_b + SSD_GROUPS + g)]
    return per, steps, sidx, row, par, state, xbc


def _ssd_fwd(xbc_c, dtr, z, dt_bias, a_log, d_skip, nw):
    rows = z.shape[0]
    nc = rows // CHUNK
    per, steps, _, row, par, state, xbc = _ssd_specs(False, nc)

    def body(xs_ref, b_ref, c_ref, dt_ref, z_ref, bias_ref, al_ref, dk_ref, nw_ref, o_ref, y_ref, st_ref, carry):
        s = pl.program_id(1)

        @pl.when(s == 0)
        def _():
            carry[...] = jnp.zeros_like(carry)

        for j in range(per):
            rs = pl.ds(j * CHUNK, CHUNK)
            live = (s * per + j) * CHUNK + lax.broadcasted_iota(jnp.int32, (CHUNK, 1), 0) >= FRONT
            prev = carry[...]
            st_ref[j, 0] = prev
            y, new = _ssd_scan(xs_ref[rs, :].astype(F32), b_ref[rs, :].astype(F32), c_ref[rs, :].astype(F32), dt_ref[rs, :],
                               prev, bias_ref[...], al_ref[...], dk_ref[...], live)
            y_ref[rs, :] = y.astype(ACT_DTYPE)
            o_ref[rs, :] = _ssd_gate(y, z_ref[rs, :].astype(F32), nw_ref[...]).astype(ACT_DTYPE)
            carry[...] = new

    act = jax.ShapeDtypeStruct((rows, SSD_D_INNER), ACT_DTYPE)
    return pl.pallas_call(
        body, name="ssd_fwd", grid=(SSD_GROUPS, steps),
        in_specs=xbc + [row(LANES), row(SSD_GW), par(LANES), par(LANES), par(LANES), par(SSD_GW)],
        out_specs=[row(SSD_GW), row(SSD_GW), state()],
        out_shape=[act, act, jax.ShapeDtypeStruct((nc, SSD_GROUPS, SSD_STATE, SSD_GW), F32)],
        scratch_shapes=[pltpu.VMEM((SSD_STATE, SSD_GW), F32)],
        compiler_params=_params(("parallel", "arbitrary")),
    )(xbc_c, xbc_c, xbc_c, dtr, z, dt_bias, a_log, d_skip, nw)


def _ssd_gate_bwd(y, z, d_bs, w_bs, nw, dproj, col0):
    rows = y.shape[0]
    d = d_bs.shape[1]
    tm = _row_tile(rows)
    assert col0 % SSD_GW == 0

    def body(y_ref, z_ref, db_ref, wb_ref, nw_ref, dp_ref, dy_ref, dz_ref, dnw_ref):
        yv, zv = y_ref[...].astype(F32), z_ref[...].astype(F32)
        dov = _mxu(db_ref[...], wb_ref[...], NT)
        s = jax.nn.sigmoid(zv)
        silu = zv * s
        yz = yv * silu
        r = lax.rsqrt(jnp.mean(yz * yz, axis=-1, keepdims=True) + EPS)
        yhat = yz * r
        dn = dov * nw_ref[...]
        dyz = r * (dn - yhat * jnp.mean(dn * yhat, axis=-1, keepdims=True))
        dy_ref[...] = (dyz * silu).astype(ACT_DTYPE)
        dz_ref[...] = (dyz * yv * (s + silu * (1.0 - s))).astype(ACT_DTYPE)

        @pl.when(pl.program_id(1) == 0)
        def _():
            dnw_ref[...] = jnp.zeros_like(dnw_ref)

        dnw_ref[...] += jnp.sum(dov * yhat, axis=0, keepdims=True)

    spec = pl.BlockSpec((tm, SSD_GW), lambda g, i: (i, g))
    par = pl.BlockSpec((1, SSD_GW), lambda g, i: (0, g))
    act = jax.ShapeDtypeStruct((rows, SSD_D_INNER), ACT_DTYPE)
    return pl.pallas_call(
        body, name="ssd_gate_bwd", grid=(SSD_GROUPS, rows // tm),
        in_specs=[spec, spec, pl.BlockSpec((tm, d), lambda g, i: (i, 0)), pl.BlockSpec((SSD_GW, d), lambda g, i: (g, 0)),
                  par, pl.BlockSpec(memory_space=pl.ANY)],
        out_specs=[spec, pl.BlockSpec((tm, SSD_GW), lambda g, i: (i, col0 // SSD_GW + g)), par],
        out_shape=[act, jax.ShapeDtypeStruct(dproj.shape, dproj.dtype), jax.ShapeDtypeStruct((1, SSD_D_INNER), F32)],
        input_output_aliases={5: 1}, compiler_params=_params(("parallel", "arbitrary")),
    )(y, z, d_bs, w_bs, nw, dproj)


def _ssd_bwd(xbc_c, dtr, dt_bias, a_log, d_skip, states, dy, dproj, col0):
    rows = dy.shape[0]
    nc = rows // CHUNK
    per, steps, sidx, row, par, state, xbc = _ssd_specs(True, nc)
    assert col0 % LANES == 0

    def body(xs_ref, b_ref, c_ref, dt_ref, bias_ref, al_ref, dk_ref, st_ref, dy_ref, dp_ref,
             dxs_ref, db_ref, dc_ref, ddt_ref, dbias_ref, dal_ref, ddk_ref, carry):
        s = pl.program_id(1)

        @pl.when(s == 0)
        def _():
            carry[...] = jnp.zeros_like(carry)
            for r in (dbias_ref, dal_ref, ddk_ref):
                r[...] = jnp.zeros_like(r)

        for j in reversed(range(per)):
            rs = pl.ds(j * CHUNK, CHUNK)
            live = (sidx(s) * per + j) * CHUNK + lax.broadcasted_iota(jnp.int32, (CHUNK, 1), 0) >= FRONT
            dxs, dbm, dcm, ddt, dprev, dbias, dal, ddk = _ssd_scan_bwd(
                xs_ref[rs, :].astype(F32), b_ref[rs, :].astype(F32), c_ref[rs, :].astype(F32), dt_ref[rs, :], st_ref[j, 0],
                bias_ref[...], al_ref[...], dk_ref[...], live, dy_ref[rs, :].astype(F32), carry[...])
            dxs_ref[rs, :] = dxs.astype(ACT_DTYPE)
            db_ref[rs, :] = dbm.astype(ACT_DTYPE)
            dc_ref[rs, :] = dcm.astype(ACT_DTYPE)
            ddt_ref[rs, :] = ddt.astype(ACT_DTYPE)
            carry[...] = dprev
            dbias_ref[...] += dbias
            dal_ref[...] += dal
            ddk_ref[...] += ddk

    bc = jax.ShapeDtypeStruct((rows, SSD_GROUPS * SSD_STATE), ACT_DTYPE)
    head = jax.ShapeDtypeStruct((1, SSD_GROUPS * LANES), F32)
    return pl.pallas_call(
        body, name="ssd_bwd", grid=(SSD_GROUPS, steps),
        in_specs=xbc + [row(LANES), par(LANES), par(LANES), par(LANES), state(), row(SSD_GW), pl.BlockSpec(memory_space=pl.ANY)],
        out_specs=[row(SSD_GW), row(SSD_STATE), row(SSD_STATE), row(LANES, lambda g: col0 // LANES + g),
                   par(LANES), par(LANES), par(LANES)],
        out_shape=[jax.ShapeDtypeStruct((rows, SSD_D_INNER), ACT_DTYPE), bc, bc,
                   jax.ShapeDtypeStruct(dproj.shape, dproj.dtype), head, head, head],
        input_output_aliases={9: 3}, scratch_shapes=[pltpu.VMEM((SSD_STATE, SSD_GW), F32)],
        compiler_params=_params(("parallel", "arbitrary")),
    )(xbc_c, xbc_c, xbc_c, dtr, dt_bias, a_log, d_skip, states, dy, dproj)


def _rotary_tables(rows):
    pos = np.arange(rows, dtype=np.float32) - np.float32(FRONT)
    inv_freq = np.float32(ROPE_BASE) ** (-np.linspace(0.0, 1.0, RET_QK // 2, dtype=np.float32))
    ang = (pos[:, None] * inv_freq[None, :]).astype(np.float32).astype(np.float64)
    lgam = np.log(1.0 - 2.0 ** (-5.0 - np.arange(RET_HEADS, dtype=np.float64))).astype(np.float32)
    lgam = np.broadcast_to(lgam[:, None, None], (RET_HEADS, 8, LANES))
    return jnp.asarray(np.cos(ang).astype(np.float32)), jnp.asarray(np.sin(ang).astype(np.float32)), jnp.asarray(lgam)


def _rotary(t, cos, sin):
    half = t.shape[-1] // 2
    t1, t2 = t[:, :half], t[:, half:]
    return jnp.concatenate([t1 * cos - t2 * sin, t2 * cos + t1 * sin], axis=1)


def _ret_chunk(qh, kh, vh, gh, prev, cos, sin, lg):
    q = CHUNK
    qr = _rotary(qh, cos, sin)
    kr = _rotary(kh, cos, sin) * (RET_QK ** -0.5)
    li = lax.broadcasted_iota(jnp.int32, (q, q), 0)
    si = lax.broadcasted_iota(jnp.int32, (q, q), 1)
    dist = (li - si).astype(F32)
    decay = jnp.exp(jnp.where(li >= si, dist * lg, -jnp.inf))
    idx = lax.broadcasted_iota(jnp.int32, (q, 1), 0).astype(F32)
    scores = _dot(qr, kr, NT) * decay
    out = _dot(scores, vh)
    kv = _dot(kr * jnp.exp((q - 1.0 - idx) * lg), vh, TN)
    out = out + _dot(qr, prev) * jnp.exp((idx + 1.0) * lg)
    new = prev * jnp.exp(q * lg) + kv
    out = out * lax.rsqrt(jnp.mean(out * out, axis=-1, keepdims=True) + EPS)
    return _silu(gh) * out, new


def _ret_specs(rev, nc):
    per = _chunks_per_step(nc)
    steps = nc // per
    sidx = (lambda s: steps - 1 - s) if rev else (lambda s: s)
    row = lambda width: pl.BlockSpec((per * CHUNK, width), lambda h, s: (sidx(s), h))
    tab = lambda: pl.BlockSpec((per * CHUNK, RET_QK // 2), lambda h, s: (sidx(s), 0))
    lgs = lambda: pl.BlockSpec((1, 8, LANES), lambda h, s: (h, 0, 0))
    state = lambda: pl.BlockSpec((per, 1, RET_QK, RET_V), lambda h, s: (sidx(s), h, 0, 0))
    part = lambda width, k: pl.BlockSpec((per * CHUNK, width), lambda h, s: (sidx(s), h * (RET_HW // width) + k))
    ins = [part(RET_QK, 0), part(RET_QK, 1), part(RET_V, 1), part(RET_V, 2), tab(), tab(), lgs()]
    return per, steps, sidx, row, state, ins


def _ret_fwd(qkvg, cos, sin, lgam):
    rows = qkvg.shape[0]
    nc = rows // CHUNK
    per, steps, _, row, state, ins = _ret_specs(False, nc)
    q = k = v = g = qkvg

    def body(q_ref, k_ref, v_ref, g_ref, cos_ref, sin_ref, lg_ref, y_ref, st_ref, carry):
        @pl.when(pl.program_id(1) == 0)
        def _():
            carry[...] = jnp.zeros_like(carry)

        for j in range(per):
            rs = pl.ds(j * CHUNK, CHUNK)
            prev = carry[...]
            st_ref[j, 0] = prev.astype(ACT_DTYPE)
            out, new = _ret_chunk(q_ref[rs, :].astype(F32), k_ref[rs, :].astype(F32), v_ref[rs, :].astype(F32),
                                  g_ref[rs, :].astype(F32), prev, cos_ref[rs, :], sin_ref[rs, :], lg_ref[0, 0:1, 0:1])
            y_ref[rs, :] = out.astype(ACT_DTYPE)
            carry[...] = new

    return pl.pallas_call(
        body, name="ret_fwd", grid=(RET_HEADS, steps), in_specs=ins, out_specs=[row(RET_V), state()],
        out_shape=[jax.ShapeDtypeStruct((rows, RET_HEADS * RET_V), ACT_DTYPE),
                   jax.ShapeDtypeStruct((nc, RET_HEADS, RET_QK, RET_V), ACT_DTYPE)],
        scratch_shapes=[pltpu.VMEM((RET_QK, RET_V), F32)],
        compiler_params=_params(("parallel", "arbitrary")),
    )(q, k, v, g, cos, sin, lgam)


def _ret_bwd(qkvg, cos, sin, lgam, states, dy, dproj):
    rows = qkvg.shape[0]
    nc = rows // CHUNK
    per, steps, sidx, row, state, ins = _ret_specs(True, nc)

    def body(q_ref, k_ref, v_ref, g_ref, cos_ref, sin_ref, lg_ref, st_ref, dy_ref, dp_ref, o_ref, carry):
        @pl.when(pl.program_id(1) == 0)
        def _():
            carry[...] = jnp.zeros_like(carry)

        for j in reversed(range(per)):
            rs = pl.ds(j * CHUNK, CHUNK)
            fn = functools.partial(_ret_chunk, cos=cos_ref[rs, :], sin=sin_ref[rs, :], lg=lg_ref[0, 0:1, 0:1])
            _, vjp = jax.vjp(fn, q_ref[rs, :].astype(F32), k_ref[rs, :].astype(F32), v_ref[rs, :].astype(F32),
                             g_ref[rs, :].astype(F32), st_ref[j, 0].astype(F32))
            dq, dk, dv, dg, dprev = vjp((dy_ref[rs, :].astype(F32), carry[...]))
            o_ref[rs, 0:RET_QK] = dq.astype(ACT_DTYPE)
            o_ref[rs, RET_QK:2 * RET_QK] = dk.astype(ACT_DTYPE)
            o_ref[rs, 2 * RET_QK:2 * RET_QK + RET_V] = dv.astype(ACT_DTYPE)
            o_ref[rs, 2 * RET_QK + RET_V:RET_HW] = dg.astype(ACT_DTYPE)
            carry[...] = dprev

    return pl.pallas_call(
        body, name="ret_bwd", grid=(RET_HEADS, steps),
        in_specs=ins + [state(), row(RET_V), pl.BlockSpec(memory_space=pl.ANY)],
        out_specs=pl.BlockSpec((per * CHUNK, RET_HW), lambda h, s: (sidx(s), h)),
        out_shape=jax.ShapeDtypeStruct(dproj.shape, dproj.dtype), input_output_aliases={9: 0},
        scratch_shapes=[pltpu.VMEM((RET_QK, RET_V), F32)],
        compiler_params=_params(("parallel", "arbitrary")),
    )(qkvg, qkvg, qkvg, qkvg, cos, sin, lgam, states, dy, dproj)


def _branches_merge(y_ssd, y_ret, w_bs, w_br, gates):
    rows, k = y_ssd.shape
    d = w_bs.shape[1]
    tm = _row_tile(rows)

    def body(ys_ref, yr_ref, ws_ref, wr_ref, gs_ref, gr_ref, bs_ref, br_ref, o_ref):
        bs = _mxu(ys_ref[...], ws_ref[...], NN).astype(ACT_DTYPE)
        br = _mxu(yr_ref[...], wr_ref[...], NN).astype(ACT_DTYPE)
        bs_ref[...] = bs
        br_ref[...] = br
        o_ref[...] = (jax.nn.sigmoid(gs_ref[...].astype(F32)) * bs.astype(F32)
                      + jax.nn.sigmoid(gr_ref[...].astype(F32)) * br.astype(F32)).astype(ACT_DTYPE)

    spec = pl.BlockSpec((tm, d), lambda i: (i, 0))
    yspec = pl.BlockSpec((tm, k), lambda i: (i, 0))
    wspec = pl.BlockSpec((k, d), lambda i: (0, 0))
    shp = jax.ShapeDtypeStruct((rows, d), ACT_DTYPE)
    return pl.pallas_call(
        body, name="branches_merge", grid=(rows // tm,),
        in_specs=[yspec, yspec, wspec, wspec, spec, pl.BlockSpec((tm, d), lambda i: (i, 1))],
        out_specs=[spec, spec, spec], out_shape=[shp, shp, shp], compiler_params=_params(("parallel",)),
    )(y_ssd, y_ret, w_bs, w_br, gates, gates)


def _merge_bwd(dh1, w_o, bs, br, gates, dproj, col0):
    rows, d = bs.shape
    tm = _row_tile(rows)
    assert col0 % (2 * d) == 0

    def body(dh_ref, wo_ref, bs_ref, br_ref, gs_ref, gr_ref, dp_ref, dbs_ref, dbr_ref, dg_ref):
        dmv = _mxu(dh_ref[...], wo_ref[...], NT)
        for k, (b_ref, g_ref, db_ref) in enumerate(((bs_ref, gs_ref, dbs_ref), (br_ref, gr_ref, dbr_ref))):
            s = jax.nn.sigmoid(g_ref[...].astype(F32))
            db_ref[...] = (dmv * s).astype(ACT_DTYPE)
            dg_ref[:, k * d:(k + 1) * d] = (dmv * b_ref[...].astype(F32) * s * (1.0 - s)).astype(ACT_DTYPE)

    spec = pl.BlockSpec((tm, d), lambda i: (i, 0))
    shp = jax.ShapeDtypeStruct((rows, d), ACT_DTYPE)
    return pl.pallas_call(
        body, name="merge_bwd", grid=(rows // tm,),
        in_specs=[spec, pl.BlockSpec(w_o.shape, lambda i: (0, 0)), spec, spec, spec, pl.BlockSpec((tm, d), lambda i: (i, 1)),
                  pl.BlockSpec(memory_space=pl.ANY)],
        out_specs=[spec, spec, pl.BlockSpec((tm, 2 * d), lambda i: (i, col0 // (2 * d)))],
        out_shape=[shp, shp, jax.ShapeDtypeStruct(dproj.shape, dproj.dtype)], input_output_aliases={6: 2},
        compiler_params=_params(("parallel",)),
    )(dh1, w_o, bs, br, gates, gates, dproj)


def _place():
    x, y, c = lax.axis_index("x"), lax.axis_index("y"), lax.axis_index("c")
    return x, y, c


def _slot(p):
    return 4 * p[0] + 2 * p[1] + p[2]


def _allgather(arrs, name):
    n = len(arrs)
    any_spec = pl.BlockSpec(memory_space=pl.ANY)

    def body(*refs):
        ins, outs = refs[:n], refs[n:2 * n]
        send_sems, recv_sems, local_sems = refs[2 * n:]
        x, y, c = _place()
        me, sibling = (x, y, c), (x, y, 1 - c)
        chips = [(1 - x, y), (x, 1 - y), (1 - x, 1 - y)]

        def copy(a, k, block, to, src=None):
            dst = outs[a].at[_slot(block)]
            return pltpu.make_async_remote_copy(
                src_ref=dst if src is None else src, dst_ref=dst, send_sem=send_sems.at[a * 7 + k],
                recv_sem=recv_sems.at[a * 7 + k], device_id=to, device_id_type=MESH)

        mine, first, passed = [], [], []
        for a in range(n):
            cp = pltpu.make_async_copy(ins[a], outs[a].at[_slot(me)], local_sems.at[a])
            cp.start()
            mine.append(cp)
            first.append(copy(a, 0, me, sibling, src=ins[a]))
            first += [copy(a, 1 + j, me, (*chip, c), src=ins[a]) for j, chip in enumerate(chips)]
        for cp in first:
            cp.start()
        for j, chip in enumerate(chips):
            for a in range(n):
                copy(a, 1 + j, (*chip, c), me).wait_recv()
                cp = copy(a, 4 + j, (*chip, c), sibling)
                cp.start()
                passed.append(cp)
        for a in range(n):
            copy(a, 0, sibling, me).wait_recv()
            for j, chip in enumerate(chips):
                copy(a, 4 + j, (*chip, 1 - c), me).wait_recv()
        for cp in first + passed:
            cp.wait_send()
        for cp in mine:
            cp.wait()

    return pl.pallas_call(
        body, name=name, in_specs=[any_spec] * n, out_specs=[any_spec] * n,
        out_shape=[jax.ShapeDtypeStruct((N_DEV,) + a.shape, a.dtype) for a in arrs],
        scratch_shapes=[pltpu.SemaphoreType.DMA((7 * n,)), pltpu.SemaphoreType.DMA((7 * n,)), pltpu.SemaphoreType.DMA((n,))],
    )(*arrs)


def _peers():
    x, y, c = _place()
    return (x, y, c), [(x ^ dx, y ^ dy, c ^ dc) for dx in (0, 1) for dy in (0, 1) for dc in (0, 1)][1:]


def _exchange_copies(srcs, lands, send_sems, recv_sems, scatter, sender):
    me, peers = _peers()
    out = []
    for a, (src, land) in enumerate(zip(srcs, lands, strict=True)):
        for k, peer in enumerate(peers):
            src_ref = src.at[_slot(peer)] if scatter else src
            out.append(pltpu.make_async_remote_copy(
                src_ref=src_ref, dst_ref=land.at[_slot(me if sender else peer)], send_sem=send_sems.at[a * 7 + k],
                recv_sem=recv_sems.at[a * 7 + k], device_id=peer, device_id_type=MESH))
    return out


_HBM = pl.BlockSpec(memory_space=pltpu.HBM)
_SEM = pl.BlockSpec(memory_space=pltpu.SEMAPHORE)
_EFFECT = pltpu.SideEffectType.DATAFLOW_SIDE_EFFECTING


def _exchange_start(srcs, scatter, name, after=None):
    n = len(srcs)
    land_shapes = [s.shape if scatter else (N_DEV,) + s.shape for s in srcs]
    n_in = 2 * n + (after is not None)

    def body(*refs):
        for cp in _exchange_copies(refs[:n], refs[n:2 * n], refs[n_in], refs[n_in + 1], scatter, True):
            cp.start()
        refs[-1][...] = jnp.zeros_like(refs[-1])

    args = [pltpu.with_memory_space_constraint(s, pltpu.HBM) for s in srcs]
    args += [pltpu.with_memory_space_constraint(lax.empty(shp, s.dtype), pltpu.HBM) for s, shp in zip(srcs, land_shapes)]
    thru_shapes = tuple(pltpu.HBM(a.shape, a.dtype) for a in args)
    extra = [] if after is None else [after]
    outs = pl.pallas_call(
        body, name=name,
        out_shape=(pltpu.SemaphoreType.DMA((7 * n,)), pltpu.SemaphoreType.DMA((7 * n,))) + thru_shapes
        + (jax.ShapeDtypeStruct((8, LANES), F32),),
        in_specs=[_HBM] * (2 * n) + [pl.BlockSpec(memory_space=pl.ANY)] * len(extra),
        out_specs=(_SEM, _SEM) + (_HBM,) * (2 * n) + (pl.BlockSpec(memory_space=pltpu.VMEM),),
        input_output_aliases={i: 2 + i for i in range(2 * n)},
        compiler_params=pltpu.CompilerParams(has_side_effects=_EFFECT),
    )(*args, *extra)
    return outs[:-1], outs[-1]


def _exchange_wait(handle, scatter, after, name):
    n = (len(handle) - 2) // 2
    thru = handle[2:]

    def body(*refs):
        for cp in _exchange_copies(refs[:n], refs[n:2 * n], refs[2 * n], refs[2 * n + 1], scatter, False):
            cp.wait_send()
            cp.wait_recv()

    outs = pl.pallas_call(
        body, name=name, out_shape=tuple(pltpu.HBM(t.shape, t.dtype) for t in thru),
        in_specs=[_HBM] * (2 * n) + [_SEM, _SEM, pl.BlockSpec(memory_space=pl.ANY)], out_specs=(_HBM,) * (2 * n),
        input_output_aliases={i: i for i in range(2 * n)},
        compiler_params=pltpu.CompilerParams(has_side_effects=_EFFECT),
    )(*thru, handle[0], handle[1], after)
    return list(outs[:n]), list(outs[n:])


def _allreduce_small(pack):
    rows, lanes = pack.shape

    def body(x_ref, o_ref, buf, send_sems, recv_sems):
        x, y, c = _place()
        me, sibling = (x, y, c), (x, y, 1 - c)
        chips = [(1 - x, y), (x, 1 - y), (1 - x, 1 - y)]

        def copy(k, block, to, src=None):
            dst = buf.at[_slot(block)]
            return pltpu.make_async_remote_copy(
                src_ref=dst if src is None else src, dst_ref=dst, send_sem=send_sems.at[k], recv_sem=recv_sems.at[k],
                device_id=to, device_id_type=MESH)

        buf[_slot(me)] = x_ref[...]
        first = [copy(0, me, sibling, src=x_ref)]
        first += [copy(1 + j, me, (*chip, c), src=x_ref) for j, chip in enumerate(chips)]
        for cp in first:
            cp.start()
        passed = [copy(4 + j, (*chip, c), sibling) for j, chip in enumerate(chips)]
        for j, chip in enumerate(chips):
            copy(1 + j, (*chip, c), me).wait_recv()
            passed[j].start()
        copy(0, sibling, me).wait_recv()
        for j, chip in enumerate(chips):
            copy(4 + j, (*chip, 1 - c), me).wait_recv()
        for cp in first + passed:
            cp.wait_send()
        acc = buf[0]
        for i in range(1, N_DEV):
            acc = acc + buf[i]
        o_ref[...] = acc

    vmem = pl.BlockSpec(memory_space=pltpu.VMEM)
    return pl.pallas_call(
        body, name="allreduce_small", in_specs=[vmem], out_specs=vmem,
        out_shape=jax.ShapeDtypeStruct((rows, lanes), F32),
        scratch_shapes=[pltpu.VMEM((N_DEV, rows, lanes), F32), pltpu.SemaphoreType.DMA((7,)), pltpu.SemaphoreType.DMA((7,))],
        compiler_params=pltpu.CompilerParams(vmem_limit_bytes=VMEM_LIMIT),
    )(pack)


def _adamw(w, g, m, v):
    m = ADAM_B1 * m + (1.0 - ADAM_B1) * g
    v = ADAM_B2 * v + (1.0 - ADAM_B2) * jnp.square(g)
    m_hat = m / (1.0 - ADAM_B1 ** ADAM_STEP)
    v_hat = v / (1.0 - ADAM_B2 ** ADAM_STEP)
    delta = -ADAM_LR * (m_hat / (jnp.sqrt(v_hat) + ADAM_EPS) + ADAM_WD * w)
    return delta, m, v


def _adam_shard(own, parts, w, m, v, name):
    r, c = w.shape
    tr = _pick(r, (128, 64, 32, 16, 8))

    def body(own_ref, p_ref, w_ref, m_ref, v_ref, g_ref, d_ref, nm_ref, nv_ref):
        _, peers = _peers()
        g = own_ref[...].astype(F32)
        for peer in peers:
            g = g + p_ref[_slot(peer)].astype(F32)
        g_ref[...] = g
        d_ref[...], nm_ref[...], nv_ref[...] = _adamw(w_ref[...], g, m_ref[...], v_ref[...])

    spec = pl.BlockSpec((tr, c), lambda i: (i, 0))
    shp = jax.ShapeDtypeStruct((r, c), F32)
    return pl.pallas_call(
        body, name=name, grid=(r // tr,),
        in_specs=[spec, pl.BlockSpec((N_DEV, tr, c), lambda i: (0, i, 0)), spec, spec, spec], out_specs=[spec] * 4,
        out_shape=[shp] * 4, compiler_params=_params(("parallel",)),
    )(own, parts, w, m, v)


def _adam_small(w, g, m, v):
    r, c = w.shape

    def body(w_ref, g_ref, m_ref, v_ref, d_ref, nm_ref, nv_ref):
        d_ref[...], nm_ref[...], nv_ref[...] = _adamw(w_ref[...], g_ref[...], m_ref[...], v_ref[...])

    shp = jax.ShapeDtypeStruct((r, c), F32)
    return pl.pallas_call(body, name="adam_small", out_shape=[shp] * 3)(w, g, m, v)


def _column_plan(pieces, shard_w):
    plan = []
    for c0, width, d0 in pieces:
        c = c0
        while c < c0 + width:
            s, a = divmod(c, shard_w)
            w = min(c0 + width - c, shard_w - a)
            plan.append((s, a, w, d0 + c - c0))
            c += w
    return plan


def _cols_from_shards(g, plan, out_w, zero, name):
    _, r, sw = g.shape
    tr = _pick(r, (128,))

    def body(x_ref, o_ref):
        for d0, w in zero:
            o_ref[:, d0:d0 + w] = jnp.zeros((tr, w), g.dtype)
        for s, a, w, d0 in plan:
            o_ref[:, d0:d0 + w] = x_ref[s, :, a:a + w]

    return pl.pallas_call(
        body, name=name, grid=(r // tr,), in_specs=[pl.BlockSpec((N_DEV, tr, sw), lambda i: (0, i, 0))],
        out_specs=pl.BlockSpec((tr, out_w), lambda i: (i, 0)), out_shape=jax.ShapeDtypeStruct((r, out_w), g.dtype),
        compiler_params=_params(("parallel",)),
    )(g)


def _shards_from_cols(srcs, plans, shard_w, name):
    r = srcs[0].shape[0]
    tr = _pick(r, (128,))
    n = len(srcs)

    def body(*refs):
        o_ref = refs[n]
        for x_ref, plan in zip(refs[:n], plans, strict=True):
            for s, a, w, d0 in plan:
                o_ref[s, :, a:a + w] = x_ref[:, d0:d0 + w].astype(COMM_DTYPE)

    return pl.pallas_call(
        body, name=name, grid=(r // tr,), in_specs=[pl.BlockSpec((tr, t.shape[1]), lambda i: (i, 0)) for t in srcs],
        out_specs=pl.BlockSpec((N_DEV, tr, shard_w), lambda i: (0, i, 0)),
        out_shape=jax.ShapeDtypeStruct((N_DEV, r, shard_w), COMM_DTYPE), compiler_params=_params(("parallel",)),
    )(*srcs)


def _pack(arrs):
    rows = []
    for a in arrs:
        flat = a.reshape(-1).astype(F32)
        rows.append(jnp.pad(flat, (0, (-flat.shape[0]) % (8 * LANES))).reshape(-1, LANES))
    return jnp.concatenate(rows, axis=0)


def _unpack(pack, shapes):
    out, r = [], 0
    for s in shapes:
        size = math.prod(s)
        nr = -(-size // (8 * LANES)) * 8
        out.append(pack[r:r + nr].reshape(-1)[:size].reshape(s))
        r += nr
    return out


def _group_lanes(t):
    lead = t.shape[:-1]
    t = t.reshape(lead + (SSD_GROUPS, SSD_HPG))
    t = jnp.pad(t, [(0, 0)] * len(lead) + [(0, 0), (0, LANES - SSD_HPG)])
    return t.reshape(lead + (SSD_GROUPS * LANES,))


def _ungroup_lanes(t):
    lead = t.shape[:-1]
    return t.reshape(lead + (SSD_GROUPS, LANES))[..., :SSD_HPG].reshape(lead + (SSD_HEADS,))


def kernel(x, meta_tokens, mix_norm_w, w_in, ssd_conv_w, ssd_conv_b, ssd_dt_bias, ssd_A_log, ssd_D, ssd_norm_w, w_branch_ssd, w_branch_ret, w_out, ffn_norm_w, w_up, ffn_conv_w, ffn_conv_b, w_down, final_norm_w, loss_target, m_meta_tokens, m_mix_norm_w, m_w_in, m_ssd_conv_w, m_ssd_conv_b, m_ssd_dt_bias, m_ssd_A_log, m_ssd_D, m_ssd_norm_w, m_w_branch_ssd, m_w_branch_ret, m_w_out, m_ffn_norm_w, m_w_up, m_ffn_conv_w, m_ffn_conv_b, m_w_down, m_final_norm_w, v_meta_tokens, v_mix_norm_w, v_w_in, v_ssd_conv_w, v_ssd_conv_b, v_ssd_dt_bias, v_ssd_A_log, v_ssd_D, v_ssd_norm_w, v_w_branch_ssd, v_w_branch_ret, v_w_out, v_ffn_norm_w, v_w_up, v_ffn_conv_w, v_ffn_conv_b, v_w_down, v_final_norm_w):
    seq, d = x.shape[1], x.shape[2]
    rows = seq + PAD_ROWS
    tm = _row_tile(rows)
    me = _slot(_place())
    d_ff = w_down.shape[1] * N_DEV

    big = [w_in[0], w_branch_ssd[0], w_branch_ret[0], w_out[0], w_up[0], w_down[0]]
    first = _allgather([w_in[0].astype(COMM_DTYPE), meta_tokens, ssd_conv_w[0], ffn_conv_w[0]], "gather_first")
    rest_src = [b.astype(COMM_DTYPE) for b in big[1:]]
    rest_handle, rest_token = _exchange_start(rest_src, False, "gather_rest_start", after=first[0])
    cols = lambda t: jnp.transpose(t, (1, 0, 2)).reshape(t.shape[1], -1)
    rws = lambda t: t.reshape(-1, t.shape[2])
    conv_w, fconv_w = cols(first[2]), cols(first[3])
    meta_full = cols(first[1]) + rest_token[0, 0]
    widths = [SSD_D_INNER, SSD_CONV_DIM, SSD_HEADS, RET_HEADS * RET_QK, RET_HEADS * RET_QK, RET_HEADS * RET_V,
              RET_HEADS * RET_V, d, d]
    offs = [0]
    for wd in widths:
        offs.append(offs[-1] + wd)
    r0, z0 = 0, RET_HEADS * RET_HW
    g0 = z0 + widths[0]
    x0 = g0 + 2 * d
    dt0 = x0 + widths[1]
    in_p = dt0 + SSD_GROUPS * LANES
    pieces = []
    for hd in range(RET_HEADS):
        base = r0 + hd * RET_HW
        pieces += [(offs[3] + hd * RET_QK, RET_QK, base), (offs[4] + hd * RET_QK, RET_QK, base + RET_QK),
                   (offs[5] + hd * RET_V, RET_V, base + 2 * RET_QK), (offs[6] + hd * RET_V, RET_V, base + 2 * RET_QK + RET_V)]
    pieces += [(offs[0], widths[0], z0), (offs[7], d, g0), (offs[8], d, g0 + d), (offs[1], widths[1], x0)]
    pieces += [(offs[2] + SSD_HPG * grp, SSD_HPG, dt0 + LANES * grp) for grp in range(SSD_GROUPS)]
    in_plan = _column_plan(pieces, w_in.shape[2])
    w_in_p = _cols_from_shards(first[0], in_plan, in_p, [(dt0, SSD_GROUPS * LANES)], "w_in_columns")

    h0 = jnp.concatenate([jnp.zeros((FRONT, d), F32), meta_full, x[0]], axis=0)
    u1 = _rms_fwd(h0, mix_norm_w, "rms1")
    in_proj = lambda c0, width, dtype, nm: _mm(u1, w_in_p, mode="nn", out_dtype=dtype, tm=tm, tk=d, name="in_proj_" + nm,
                                               tn=_pick(width, (1024, 512)), b_n0=c0, n_out=width)
    qkvg = in_proj(r0, RET_HEADS * RET_HW, ACT_DTYPE, "qkvg")
    z = in_proj(z0, widths[0], ACT_DTYPE, "z")
    gates = in_proj(g0, 2 * d, ACT_DTYPE, "gates")
    dtr = in_proj(dt0, SSD_GROUPS * LANES, F32, "dt")
    xbc, xbc_c = _xbc_proj_conv(u1, w_in_p, x0, conv_w, ssd_conv_b)
    bias_p, alog_p, dsk_p = _group_lanes(ssd_dt_bias), _group_lanes(ssd_A_log), _group_lanes(ssd_D)
    y_ssd, y_scan, ssd_states = _ssd_fwd(xbc_c, dtr, z, bias_p, alog_p, dsk_p, ssd_norm_w)
    cos, sin, lgam = _rotary_tables(rows)
    y_ret, ret_states = _ret_fwd(qkvg, cos, sin, lgam)
    rest_own, rest = _exchange_wait(rest_handle, False, y_ret, "gather_rest_wait")
    rest = [lax.dynamic_update_index_in_dim(land, own, me, 0) for land, own in zip(rest, rest_own, strict=True)]
    w_bs, w_br, w_o, w_dn = rws(rest[0]), rws(rest[1]), rws(rest[2]), rws(rest[4])
    w_up_f = _cols_from_shards(rest[3], _column_plan([(0, 2 * d_ff, 0)], w_up.shape[2]), 2 * d_ff, [], "w_up_columns")
    bs, br, merged = _branches_merge(y_ssd, y_ret, w_bs, w_br, gates)
    h1, u2 = _out_proj_norm(merged, w_o, h0, ffn_norm_w)
    up_g, up_v, act = _ffn_up_conv(u2, w_up_f, fconv_w, ffn_conv_b)
    tgt = jnp.pad(loss_target[0], ((PAD_ROWS, 0), (0, 0)))
    dh2, loss_acc, g_final, dh2_m = _down_loss_head(act, w_dn, h1, tgt, final_norm_w.reshape(1, d))

    tff = _pick(d_ff, (1408, 256))
    tkr = _pick(rows, (1664, 128))
    tkr2 = _pick(rows, (4160, 128))
    rparts = lambda t: t.reshape(N_DEV, -1, t.shape[1])
    d_act = _mm(dh2_m, w_dn, mode="nt", out_dtype=ACT_DTYPE, tm=tm, tn=tff, tk=d, name="d_act")
    g_w_dn = _mm(act, dh2_m, mode="tn", out_dtype=COMM_DTYPE, tm=tff, tn=d, tk=tkr, name="g_w_down")
    c_dn = [rparts(g_w_dn)]
    h_dn, t_dn = _exchange_start(c_dn, True, "scatter_down_start")
    d_up_g, d_up_v, g_fcw_g, g_fcb_g, g_fcw_v, g_fcb_v, g_w_up_g, g_w_up_v = _ffn_conv_bwd(
        up_g, up_v, d_act, fconv_w, ffn_conv_b + t_dn[0, 0], u2)
    g_fconv_w = jnp.concatenate([g_fcw_g, g_fcw_v], axis=1)
    g_fconv_b = jnp.concatenate([g_fcb_g, g_fcb_v], axis=1)
    c_up = [_shards_from_cols([g_w_up_g, g_w_up_v], [_column_plan([(0, d_ff, 0)], w_up.shape[2]),
                                                     _column_plan([(d_ff, d_ff, 0)], w_up.shape[2])], w_up.shape[2], "g_w_up_shards")]
    h_up, t_up = _exchange_start(c_up, True, "scatter_up_start")
    dh1, g_ffn_norm, dh1_m = _proj_norm_bwd([d_up_g, d_up_v], w_up_f, d_ff, h1, ffn_norm_w, dh2, "ffn_up_bwd", after=t_up,
                                            mxu_copy=True)
    g_w_o = _mm(merged, dh1_m, mode="tn", out_dtype=COMM_DTYPE, tm=d, tn=d, tk=tkr2, name="g_w_out")
    dproj = lax.empty((rows, in_p), ACT_DTYPE)
    d_bs, d_br, dproj = _merge_bwd(dh1_m, w_o, bs, br, gates, dproj, g0)
    g_w_bs = _mm(y_ssd, d_bs, mode="tn", out_dtype=COMM_DTYPE, tm=1024, tn=d, tk=tkr2, name="g_w_branch_ssd")
    g_w_br = _mm(y_ret, d_br, mode="tn", out_dtype=COMM_DTYPE, tm=1024, tn=d, tk=tkr2, name="g_w_branch_ret")
    c_mid = [rparts(g_w_bs), rparts(g_w_br), rparts(g_w_o)]
    h_mid, t_mid = _exchange_start(c_mid, True, "scatter_mid_start")
    d_yscan, dproj, g_nw = _ssd_gate_bwd(y_scan, z, d_bs, w_bs, ssd_norm_w + t_mid[0, 0], dproj, z0)
    dxs, d_bm, d_cm, dproj, g_bias_p, g_alog_p, g_dsk_p = _ssd_bwd(
        xbc_c, dtr, bias_p, alog_p, dsk_p, ssd_states, d_yscan, dproj, dt0)
    dproj, g_conv_w, g_conv_b = _ssd_conv_bwd(xbc, dxs, d_bm, d_cm, conv_w, ssd_conv_b, dproj, x0)
    d_yret = _mm(d_br, w_br, mode="nt", out_dtype=ACT_DTYPE, tm=tm, tn=1024, tk=d, name="d_y_ret")
    dproj = _ret_bwd(qkvg, cos, sin, lgam, ret_states, d_yret, dproj)
    g_w_in_p = _mm(u1, dproj, mode="tn", out_dtype=F32, tm=d, tn=_pick(in_p, (768, 512)), tk=tkr2, name="g_w_in")
    c_in = [_shards_from_cols([g_w_in_p], [in_plan], w_in.shape[2], "g_w_in_shards")]
    h_in, t_in = _exchange_start(c_in, True, "scatter_in_start")
    dh0, g_mix_norm = _proj_norm_bwd([dproj], w_in_p, _pick(in_p, (4608, 512)), h0, mix_norm_w, dh1, "in_proj_bwd", after=t_in)
    grad_x = dh0[PAD_ROWS:][None]

    landed = {}
    for key, handle, names in (("in", h_in, ["w_in"]), ("mid", h_mid, ["w_branch_ssd", "w_branch_ret", "w_out"]),
                               ("up", h_up, ["w_up"]), ("down", h_dn, ["w_down"])):
        srcs, lands = _exchange_wait(handle, True, dh0, f"scatter_{key}_wait")
        for nm, land, src in zip(names, lands, srcs, strict=True):
            landed[nm] = (lax.dynamic_index_in_dim(src, me, 0, keepdims=False), land)
    big_m = [m_w_in, m_w_branch_ssd, m_w_branch_ret, m_w_out, m_w_up, m_w_down]
    big_v = [v_w_in, v_w_branch_ssd, v_w_branch_ret, v_w_out, v_w_up, v_w_down]
    big_names = ["w_in", "w_branch_ssd", "w_branch_ret", "w_out", "w_up", "w_down"]
    big_out = {}
    for nm, w, m, v_ in zip(big_names, big, big_m, big_v, strict=True):
        big_out[nm] = [t[None] for t in _adam_shard(*landed[nm], w, m[0], v_[0], "adam_" + nm)]

    small_g = [dh0[FRONT:PAD_ROWS], g_mix_norm, g_conv_w, g_conv_b, _ungroup_lanes(g_bias_p), _ungroup_lanes(g_alog_p),
               _ungroup_lanes(g_dsk_p), g_nw, g_ffn_norm, g_fconv_w, g_fconv_b, g_final, loss_acc[0:1, 0:1]]
    total = _unpack(_allreduce_small(_pack(small_g)), [t.shape for t in small_g])
    loss = total[12].reshape(())
    shard = lambda t, width: lax.dynamic_slice_in_dim(t, me * width, width, axis=1)
    small_names = ["meta_tokens", "mix_norm_w", "ssd_conv_w", "ssd_conv_b", "ssd_dt_bias", "ssd_A_log", "ssd_D", "ssd_norm_w",
                   "ffn_norm_w", "ffn_conv_w", "ffn_conv_b", "final_norm_w"]
    small_w = [meta_tokens, mix_norm_w, ssd_conv_w, ssd_conv_b, ssd_dt_bias, ssd_A_log, ssd_D, ssd_norm_w, ffn_norm_w,
               ffn_conv_w, ffn_conv_b, final_norm_w]
    small_m = [m_meta_tokens, m_mix_norm_w, m_ssd_conv_w, m_ssd_conv_b, m_ssd_dt_bias, m_ssd_A_log, m_ssd_D, m_ssd_norm_w,
               m_ffn_norm_w, m_ffn_conv_w, m_ffn_conv_b, m_final_norm_w]
    small_v = [v_meta_tokens, v_mix_norm_w, v_ssd_conv_w, v_ssd_conv_b, v_ssd_dt_bias, v_ssd_A_log, v_ssd_D, v_ssd_norm_w,
               v_ffn_norm_w, v_ffn_conv_w, v_ffn_conv_b, v_final_norm_w]
    grads = total[:12]
    grads[0] = shard(grads[0], meta_tokens.shape[1])
    grads[2] = shard(grads[2], ssd_conv_w.shape[2])
    grads[9] = shard(grads[9], ffn_conv_w.shape[2])
    grads = [t.reshape(w.shape) for t, w in zip(grads, small_w, strict=True)]
    shapes = [w.shape for w in small_w]
    upd = _adam_small(_pack(small_w), _pack(grads), _pack(small_m), _pack(small_v))
    small_out = {nm: [gr_] + [u[i] for u in (_unpack(t, shapes) for t in upd)]
                 for i, (nm, gr_) in enumerate(zip(small_names, grads, strict=True))}

    order = ["meta_tokens", "mix_norm_w", "w_in", "ssd_conv_w", "ssd_conv_b", "ssd_dt_bias", "ssd_A_log", "ssd_D", "ssd_norm_w",
             "w_branch_ssd", "w_branch_ret", "w_out", "ffn_norm_w", "w_up", "ffn_conv_w", "ffn_conv_b", "w_down", "final_norm_w"]
    res = {**big_out, **small_out}
    return (loss, grad_x, *[res[nm][0] for nm in order], *[res[nm][1] for nm in order], *[res[nm][2] for nm in order],
            *[res[nm][3] for nm in order])
```
